```python
import math
import jax
import jax.numpy as jnp
from jax import lax
import numpy as np

D_MODEL = 2048
BATCH = 8
SEQ = 4096
DEPTH = 4

HEAD_DIM = 128
DN_HEADS = 6
DN_CONV = 4
DN_CHUNK = 64
POOL_WINDOWS = (2, 4, 8, 16)
POOL_GROUPS = 4
POOL_GROUP_DIM = 128
SWA_HEADS = 6
SWA_KV_HEADS = 2
SWA_WINDOW = 128
SWA_BLOCK = 128
ROPE_THETA = 10000.0
D_FF = 5632
FFN_CONV = 3
NORM_EPS = 1e-6

DN_W = DN_HEADS * HEAD_DIM
POOL_W = POOL_GROUPS * POOL_GROUP_DIM
SWA_W = SWA_HEADS * HEAD_DIM
SWA_KV_W = SWA_KV_HEADS * HEAD_DIM
MIX_W = DN_W + POOL_W + SWA_W
IN_SIZES = (3 * DN_W, DN_W, DN_HEADS, DN_HEADS, POOL_W, SWA_W, SWA_KV_W, SWA_KV_W)
IN_W = int(sum(IN_SIZES))
IN_OFFSETS = tuple(int(v) for v in np.cumsum(IN_SIZES)[:-1])

kernel_name = "hybrid_parallel_heads_deltanet_pool_swa"


def rms_norm(x, w):
    xf = x.astype(jnp.float32)
    y = xf * lax.rsqrt(jnp.mean(xf * xf, axis=-1, keepdims=True) + NORM_EPS)
    return (y * w.astype(jnp.float32)).astype(x.dtype)


def l2_norm(x):
    return x * lax.rsqrt(jnp.sum(x * x, axis=-1, keepdims=True) + NORM_EPS)


def causal_dwconv(x, w):
    K = w.shape[0]
    T = x.shape[1]
    xp = jnp.pad(x, ((0, 0), (K - 1, 0), (0, 0)))
    return sum(xp[:, k:k + T] * w[k] for k in range(K))


def rope(x, cos, sin):
    xf = x.astype(jnp.float32)
    x1, x2 = jnp.split(xf, 2, axis=-1)
    return jnp.concatenate([x1 * cos - x2 * sin, x2 * cos + x1 * sin], axis=-1).astype(x.dtype)


def unit_lower_inverse(lmat):
    C = lmat.shape[-1]
    eye = jnp.eye(C, dtype=jnp.float32)
    nil = -lmat
    inv = eye + nil
    powk = nil
    for _ in range(int(math.log2(C)) - 1):
        powk = jnp.matmul(powk, powk)
        inv = jnp.matmul(inv, eye + powk)
    return inv


def chunk_gated_delta_rule(q, k, v, g, beta):
    B, T, H, D = q.shape
    C = DN_CHUNK
    N = T // C
    q, k, v = (t.reshape(B, N, C, H, D) for t in (q, k, v))
    g = g.reshape(B, N, C, H)
    beta = beta.reshape(B, N, C, H)
    gc = jnp.cumsum(g, axis=2)
    gct = jnp.moveaxis(gc, 2, 3)
    idx = jnp.arange(C)
    causal = idx[:, None] >= idx[None, :]
    strict = idx[:, None] > idx[None, :]
    decay = jnp.exp(jnp.where(causal, gct[..., :, None] - gct[..., None, :], -jnp.inf))
    kb = k * beta[..., None]
    vb = v * beta[..., None]
    lmat = jnp.where(strict, jnp.einsum('bnihd,bnjhd->bnhij', kb, k) * decay, 0.0)
    tinv = unit_lower_inverse(lmat)
    eg = jnp.exp(gc)[..., None]
    u = jnp.einsum('bnhij,bnjhd->bnihd', tinv, vb)
    w = jnp.einsum('bnhij,bnjhd->bnihd', tinv, kb * eg)
    a_intra = jnp.einsum('bnihd,bnjhd->bnhij', q, k) * decay
    g_last = gc[:, :, -1:, :]
    q_dec = q * eg
    k_dec = k * jnp.exp(g_last - gc)[..., None]
    cdec = jnp.exp(g_last[:, :, 0, :])

    def step(S, xs):
        u_n, w_n, q_n, k_n, a_n, c_n = xs
        v_new = u_n - jnp.einsum('bchk,bhkv->bchv', w_n, S)
        o_n = jnp.einsum('bchk,bhkv->bchv', q_n, S) + jnp.einsum('bhij,bjhv->bihv', a_n, v_new)
        S = S * c_n[:, :, None, None] + jnp.einsum('bchk,bchv->bhkv', k_n, v_new)
        return S, o_n

    xs = tuple(jnp.moveaxis(t, 1, 0) for t in (u, w, q_dec, k_dec, a_intra, cdec))
    S0 = jnp.zeros((B, H, D, D), jnp.float32)
    _, o = lax.scan(step, S0, xs)
    return jnp.moveaxis(o, 0, 1).reshape(B, T, H, D)


def gated_deltanet(p_qkv, p_z, p_b, p_a, conv_w, a_log, dt_bias, norm_w):
    B, T, _ = p_qkv.shape
    f32 = jnp.float32
    qkv = jax.nn.silu(causal_dwconv(p_qkv, conv_w)).astype(f32)
    q, k, v = jnp.split(qkv, 3, axis=-1)
    shp = (B, T, DN_HEADS, HEAD_DIM)
    q = l2_norm(q.reshape(shp)) * (HEAD_DIM ** -0.5)
    k = l2_norm(k.reshape(shp))
    v = v.reshape(shp)
    beta = jax.nn.sigmoid(p_b.astype(f32))
    g = -jnp.exp(a_log.astype(f32)) * jax.nn.softplus(p_a.astype(f32) + dt_bias.astype(f32))
    o = chunk_gated_delta_rule(q, k, v, g, beta)
    o = rms_norm(o, norm_w) * jax.nn.silu(p_z.astype(f32).reshape(shp))
    return o.reshape(B, T, DN_W).astype(p_qkv.dtype)


def multiscale_pool(p_pool, pool_w, pool_scale):
    B, T, _ = p_pool.shape
    xg = p_pool.astype(jnp.float32).reshape(B, T, POOL_GROUPS, POOL_GROUP_DIM)
    cs = jnp.cumsum(xg, axis=1)
    t = jnp.arange(T)
    outs = []
    for gi, win in enumerate(POOL_WINDOWS):
        c = cs[:, :, gi]
        prev = jnp.pad(c, ((0, 0), (win, 0), (0, 0)))[:, :T]
        cnt = jnp.minimum(t + 1, win).astype(jnp.float32)
        outs.append((c - prev) / cnt[None, :, None] - xg[:, :, gi])
    y = jnp.stack(outs, axis=2)
    y = jnp.einsum('btgc,gcd->btgd', y, pool_w.astype(jnp.float32))
    return (y.reshape(B, T, POOL_W) * pool_scale.astype(jnp.float32)).astype(p_pool.dtype)


def band_blocks(t, blk):
    B, T = t.shape[:2]
    nb = T // blk
    tp = jnp.pad(t, ((0, 0), (blk, 0), (0, 0), (0, 0))).reshape(B, nb + 1, blk, *t.shape[2:])
    return jnp.concatenate([tp[:, :-1], tp[:, 1:]], axis=2)


def band_mask(nb, blk):
    ii = jnp.arange(blk)[:, None]
    jj = jnp.arange(2 * blk)[None, :]
    diff = blk + ii - jj
    in_win = (diff >= 0) & (diff < SWA_WINDOW)
    blk_idx = jnp.arange(nb)[:, None, None]
    return in_win[None] & ((blk_idx > 0) | (jj >= blk)[None])


def swa_sink_attention(p_q, p_k, p_v, sinks, cos, sin):
    B, T, _ = p_q.shape
    G = SWA_HEADS // SWA_KV_HEADS
    bq = SWA_BLOCK
    nb = T // bq
    q = rope(p_q.reshape(B, T, SWA_HEADS, HEAD_DIM), cos, sin)
    k = rope(p_k.reshape(B, T, SWA_KV_HEADS, HEAD_DIM), cos, sin)
    v = p_v.reshape(B, T, SWA_KV_HEADS, HEAD_DIM)
    qb = q.reshape(B, nb, bq, SWA_KV_HEADS, G, HEAD_DIM)
    kb = band_blocks(k, bq)
    vb = band_blocks(v, bq)
    s = jnp.einsum('bnihgd,bnjhd->bnhgij', qb, kb, preferred_element_type=jnp.float32) * (HEAD_DIM ** -0.5)
    s = jnp.where(band_mask(nb, bq)[None, :, None, None], s, -jnp.inf)
    sink = jnp.broadcast_to(
        sinks.astype(jnp.float32).reshape(SWA_KV_HEADS, G)[None, None, :, :, None, None],
        s.shape[:-1] + (1,))
    p = jax.nn.softmax(jnp.concatenate([s, sink], axis=-1), axis=-1)[..., :-1]
    o = jnp.einsum('bnhgij,bnjhd->bnihgd', p.astype(v.dtype), vb)
    return o.reshape(B, T, SWA_W)


def conv_glu_ffn(h, w_up, conv_w, conv_b, w_down):
    u = causal_dwconv(h @ w_up, conv_w) + conv_b
    a, b = jnp.split(u, 2, axis=-1)
    return (jax.nn.silu(a) * b) @ w_down


def _fwd_setup_inputs(seed: int = 0) -> dict:
    key = jax.random.key(seed)
    ks = jax.random.split(key, 24)
    f32 = jnp.float32

    def nrm(k, shape, scale):
        return jax.random.normal(k, shape, f32) * scale

    def gain(k, shape):
        return 1.0 + 0.02 * jax.random.normal(k, shape, f32)

    x = jax.random.normal(ks[0], (BATCH, SEQ, D_MODEL), f32)
    offs = jax.random.randint(ks[1], (BATCH, 1), 0, 1024)
    positions = (offs + jnp.arange(SEQ)[None, :]).astype(jnp.int32)
    dt = jnp.exp(jax.random.uniform(ks[2], (DEPTH, DN_HEADS), f32, math.log(1e-3), math.log(1e-1)))
    return {
        "x": x,
        "positions": positions,
        "norm_mix_pre": gain(ks[3], (DEPTH, D_MODEL)),
        "w_in": nrm(ks[4], (DEPTH, D_MODEL, IN_W), D_MODEL ** -0.5),
        "dn_conv_w": nrm(ks[5], (DEPTH, DN_CONV, 3 * DN_W), DN_CONV ** -0.5),
        "dn_a_log": jnp.log(jax.random.uniform(ks[6], (DEPTH, DN_HEADS), f32, 1.0, 16.0)),
        "dn_dt_bias": dt + jnp.log(-jnp.expm1(-dt)),
        "dn_norm_w": gain(ks[7], (DEPTH, HEAD_DIM)),
        "pool_w": nrm(ks[8], (DEPTH, POOL_GROUPS, POOL_GROUP_DIM, POOL_GROUP_DIM), POOL_GROUP_DIM ** -0.5),
        "pool_scale": gain(ks[9], (DEPTH, POOL_W)),
        "swa_sinks": nrm(ks[10], (DEPTH, SWA_HEADS), 1.0),
        "w_out": nrm(ks[11], (DEPTH, MIX_W, D_MODEL), MIX_W ** -0.5),
        "norm_mix_post": gain(ks[12], (DEPTH, D_MODEL)),
        "norm_ffn_pre": gain(ks[13], (DEPTH, D_MODEL)),
        "ffn_w_up": nrm(ks[14], (DEPTH, D_MODEL, 2 * D_FF), D_MODEL ** -0.5),
        "ffn_conv_w": nrm(ks[15], (DEPTH, FFN_CONV, 2 * D_FF), FFN_CONV ** -0.5),
        "ffn_conv_b": nrm(ks[16], (DEPTH, 2 * D_FF), 0.01),
        "ffn_w_down": nrm(ks[17], (DEPTH, D_FF, D_MODEL), D_FF ** -0.5),
        "norm_ffn_post": gain(ks[18], (DEPTH, D_MODEL)),
    }


def _fwd_reference(x, positions, norm_mix_pre, w_in, dn_conv_w, dn_a_log, dn_dt_bias, dn_norm_w,
              pool_w, pool_scale, swa_sinks, w_out, norm_mix_post, norm_ffn_pre,
              ffn_w_up, ffn_conv_w, ffn_conv_b, ffn_w_down, norm_ffn_post):
    inv_freq = 1.0 / (ROPE_THETA ** (jnp.arange(0, HEAD_DIM, 2, dtype=jnp.float32) / HEAD_DIM))
    ang = positions.astype(jnp.float32)[..., None] * inv_freq
    cos = jnp.cos(ang)[:, :, None, :]
    sin = jnp.sin(ang)[:, :, None, :]
    for l in range(DEPTH):
        h = rms_norm(x, norm_mix_pre[l])
        p = h @ w_in[l]
        dn_qkv, dn_z, dn_b, dn_a, p_pool, sq, sk, sv = jnp.split(p, IN_OFFSETS, axis=-1)
        y_dn = gated_deltanet(dn_qkv, dn_z, dn_b, dn_a, dn_conv_w[l], dn_a_log[l], dn_dt_bias[l], dn_norm_w[l])
        y_pool = multiscale_pool(p_pool, pool_w[l], pool_scale[l])
        y_swa = swa_sink_attention(sq, sk, sv, swa_sinks[l], cos, sin)
        mix = jnp.concatenate([y_dn, y_pool, y_swa], axis=-1) @ w_out[l]
        x = x + rms_norm(mix, norm_mix_post[l])
        h = rms_norm(x, norm_ffn_pre[l])
        f = conv_glu_ffn(h, ffn_w_up[l], ffn_conv_w[l], ffn_conv_b[l], ffn_w_down[l])
        x = x + rms_norm(f, norm_ffn_post[l])
    return x


import jax as _jax
import jax.numpy as _jnp

TWIN_FORMAT = 'train_step'
FWD_PARAMS = ['x', 'positions', 'norm_mix_pre', 'w_in', 'dn_conv_w', 'dn_a_log', 'dn_dt_bias', 'dn_norm_w', 'pool_w', 'pool_scale', 'swa_sinks', 'w_out', 'norm_mix_post', 'norm_ffn_pre', 'ffn_w_up', 'ffn_conv_w', 'ffn_conv_b', 'ffn_w_down', 'norm_ffn_post']
TWIN_WEIGHTS = ['norm_mix_pre', 'w_in', 'dn_conv_w', 'dn_a_log', 'dn_dt_bias', 'dn_norm_w', 'pool_w', 'pool_scale', 'swa_sinks', 'w_out', 'norm_mix_post', 'norm_ffn_pre', 'ffn_w_up', 'ffn_conv_w', 'ffn_conv_b', 'ffn_w_down', 'norm_ffn_post']
TWIN_DIFF_INPUT = 'x'
TWIN_INPUTS = ['x', 'positions', 'norm_mix_pre', 'w_in', 'dn_conv_w', 'dn_a_log', 'dn_dt_bias', 'dn_norm_w', 'pool_w', 'pool_scale', 'swa_sinks', 'w_out', 'norm_mix_post', 'norm_ffn_pre', 'ffn_w_up', 'ffn_conv_w', 'ffn_conv_b', 'ffn_w_down', 'norm_ffn_post', 'loss_target', 'm_norm_mix_pre', 'm_w_in', 'm_dn_conv_w', 'm_dn_a_log', 'm_dn_dt_bias', 'm_dn_norm_w', 'm_pool_w', 'm_pool_scale', 'm_swa_sinks', 'm_w_out', 'm_norm_mix_post', 'm_norm_ffn_pre', 'm_ffn_w_up', 'm_ffn_conv_w', 'm_ffn_conv_b', 'm_ffn_w_down', 'm_norm_ffn_post', 'v_norm_mix_pre', 'v_w_in', 'v_dn_conv_w', 'v_dn_a_log', 'v_dn_dt_bias', 'v_dn_norm_w', 'v_pool_w', 'v_pool_scale', 'v_swa_sinks', 'v_w_out', 'v_norm_mix_post', 'v_norm_ffn_pre', 'v_ffn_w_up', 'v_ffn_conv_w', 'v_ffn_conv_b', 'v_ffn_w_down', 'v_norm_ffn_post']
TWIN_OUTPUTS = ['loss', 'grad_x', 'grad_norm_mix_pre', 'grad_w_in', 'grad_dn_conv_w', 'grad_dn_a_log', 'grad_dn_dt_bias', 'grad_dn_norm_w', 'grad_pool_w', 'grad_pool_scale', 'grad_swa_sinks', 'grad_w_out', 'grad_norm_mix_post', 'grad_norm_ffn_pre', 'grad_ffn_w_up', 'grad_ffn_conv_w', 'grad_ffn_conv_b', 'grad_ffn_w_down', 'grad_norm_ffn_post', 'delta_norm_mix_pre', 'delta_w_in', 'delta_dn_conv_w', 'delta_dn_a_log', 'delta_dn_dt_bias', 'delta_dn_norm_w', 'delta_pool_w', 'delta_pool_scale', 'delta_swa_sinks', 'delta_w_out', 'delta_norm_mix_post', 'delta_norm_ffn_pre', 'delta_ffn_w_up', 'delta_ffn_conv_w', 'delta_ffn_conv_b', 'delta_ffn_w_down', 'delta_norm_ffn_post', 'new_m_norm_mix_pre', 'new_m_w_in', 'new_m_dn_conv_w', 'new_m_dn_a_log', 'new_m_dn_dt_bias', 'new_m_dn_norm_w', 'new_m_pool_w', 'new_m_pool_scale', 'new_m_swa_sinks', 'new_m_w_out', 'new_m_norm_mix_post', 'new_m_norm_ffn_pre', 'new_m_ffn_w_up', 'new_m_ffn_conv_w', 'new_m_ffn_conv_b', 'new_m_ffn_w_down', 'new_m_norm_ffn_post', 'new_v_norm_mix_pre', 'new_v_w_in', 'new_v_dn_conv_w', 'new_v_dn_a_log', 'new_v_dn_dt_bias', 'new_v_dn_norm_w', 'new_v_pool_w', 'new_v_pool_scale', 'new_v_swa_sinks', 'new_v_w_out', 'new_v_norm_mix_post', 'new_v_norm_ffn_pre', 'new_v_ffn_w_up', 'new_v_ffn_conv_w', 'new_v_ffn_conv_b', 'new_v_ffn_w_down', 'new_v_norm_ffn_post']
TWIN_LEAF_KINDS = {'loss': 'loss', 'grad_x': 'grad_x', 'grad_norm_mix_pre': 'grad_w', 'grad_w_in': 'grad_w', 'grad_dn_conv_w': 'grad_w', 'grad_dn_a_log': 'grad_w', 'grad_dn_dt_bias': 'grad_w', 'grad_dn_norm_w': 'grad_w', 'grad_pool_w': 'grad_w', 'grad_pool_scale': 'grad_w', 'grad_swa_sinks': 'grad_w', 'grad_w_out': 'grad_w', 'grad_norm_mix_post': 'grad_w', 'grad_norm_ffn_pre': 'grad_w', 'grad_ffn_w_up': 'grad_w', 'grad_ffn_conv_w': 'grad_w', 'grad_ffn_conv_b': 'grad_w', 'grad_ffn_w_down': 'grad_w', 'grad_norm_ffn_post': 'grad_w', 'delta_norm_mix_pre': 'delta_w', 'delta_w_in': 'delta_w', 'delta_dn_conv_w': 'delta_w', 'delta_dn_a_log': 'delta_w', 'delta_dn_dt_bias': 'delta_w', 'delta_dn_norm_w': 'delta_w', 'delta_pool_w': 'delta_w', 'delta_pool_scale': 'delta_w', 'delta_swa_sinks': 'delta_w', 'delta_w_out': 'delta_w', 'delta_norm_mix_post': 'delta_w', 'delta_norm_ffn_pre': 'delta_w', 'delta_ffn_w_up': 'delta_w', 'delta_ffn_conv_w': 'delta_w', 'delta_ffn_conv_b': 'delta_w', 'delta_ffn_w_down': 'delta_w', 'delta_norm_ffn_post': 'delta_w', 'new_m_norm_mix_pre': 'new_m', 'new_m_w_in': 'new_m', 'new_m_dn_conv_w': 'new_m', 'new_m_dn_a_log': 'new_m', 'new_m_dn_dt_bias': 'new_m', 'new_m_dn_norm_w': 'new_m', 'new_m_pool_w': 'new_m', 'new_m_pool_scale': 'new_m', 'new_m_swa_sinks': 'new_m', 'new_m_w_out': 'new_m', 'new_m_norm_mix_post': 'new_m', 'new_m_norm_ffn_pre': 'new_m', 'new_m_ffn_w_up': 'new_m', 'new_m_ffn_conv_w': 'new_m', 'new_m_ffn_conv_b': 'new_m', 'new_m_ffn_w_down': 'new_m', 'new_m_norm_ffn_post': 'new_m', 'new_v_norm_mix_pre': 'new_v', 'new_v_w_in': 'new_v', 'new_v_dn_conv_w': 'new_v', 'new_v_dn_a_log': 'new_v', 'new_v_dn_dt_bias': 'new_v', 'new_v_dn_norm_w': 'new_v', 'new_v_pool_w': 'new_v', 'new_v_pool_scale': 'new_v', 'new_v_swa_sinks': 'new_v', 'new_v_w_out': 'new_v', 'new_v_norm_mix_post': 'new_v', 'new_v_norm_ffn_pre': 'new_v', 'new_v_ffn_w_up': 'new_v', 'new_v_ffn_conv_w': 'new_v', 'new_v_ffn_conv_b': 'new_v', 'new_v_ffn_w_down': 'new_v', 'new_v_norm_ffn_post': 'new_v'}


def _forward(args):
    return _fwd_reference(*[args[k] for k in FWD_PARAMS])


def _output_shape():
    def fwd():
        inp = _fwd_setup_inputs(0)
        return _fwd_reference(*[inp[k] for k in FWD_PARAMS])
    out = _jax.eval_shape(fwd)
    return out.shape, out.dtype

N_MICROBATCH = 1
ADAM_LR = 0.001
ADAM_B1 = 0.9
ADAM_B2 = 0.999
ADAM_EPS = 1e-08
ADAM_WD = 0.01
ADAM_STEP = 10
PER_EXAMPLE_BATCH_AXIS = {'x': 0, 'positions': 0, 'loss_target': 0}
SHARED_INPUTS = []
_WEIGHT_DTYPES = {'norm_mix_pre': _jnp.float32, 'w_in': _jnp.float32, 'dn_conv_w': _jnp.float32, 'dn_a_log': _jnp.float32, 'dn_dt_bias': _jnp.float32, 'dn_norm_w': _jnp.float32, 'pool_w': _jnp.float32, 'pool_scale': _jnp.float32, 'swa_sinks': _jnp.float32, 'w_out': _jnp.float32, 'norm_mix_post': _jnp.float32, 'norm_ffn_pre': _jnp.float32, 'ffn_w_up': _jnp.float32, 'ffn_conv_w': _jnp.float32, 'ffn_conv_b': _jnp.float32, 'ffn_w_down': _jnp.float32, 'norm_ffn_post': _jnp.float32}
MOMENT_SCALE = {'norm_mix_pre': 1.054597e+00, 'w_in': 6.824017e-01, 'dn_conv_w': 6.764771e-01, 'dn_a_log': 2.222634e+00, 'dn_dt_bias': 2.192128e+00, 'dn_norm_w': 3.762862e+00, 'pool_w': 1.531796e+00, 'pool_scale': 1.729503e+00, 'swa_sinks': 7.596883e-01, 'w_out': 1.192735e+00, 'norm_mix_post': 1.599943e+01, 'norm_ffn_pre': 7.012520e-01, 'ffn_w_up': 3.132049e-01, 'ffn_conv_w': 3.332263e-01, 'ffn_conv_b': 1.024686e+00, 'ffn_w_down': 5.758864e-01, 'norm_ffn_post': 1.594680e+01}


def _to_microbatches(a, axis):
    t = _jnp.moveaxis(a, axis, 0)
    t = t.reshape((N_MICROBATCH, t.shape[0] // N_MICROBATCH) + t.shape[1:])
    return _jnp.moveaxis(t, 1, axis + 1)


def setup_inputs(seed: int = 0) -> dict:
    inp = _fwd_setup_inputs(seed)
    key = _jax.random.fold_in(_jax.random.key(seed), 7919)
    shape, _ = _output_shape()
    out = dict(inp)
    out["loss_target"] = _jax.random.normal(_jax.random.fold_in(key, 0), shape, _jnp.float32)
    for i, name in enumerate(TWIN_WEIGHTS):
        w = inp[name].astype(_jnp.float32)
        if MOMENT_SCALE is None:
            s = _jnp.sqrt(_jnp.mean(_jnp.square(w)) + 1e-30)
        else:
            s = MOMENT_SCALE[name]
        km, kv = _jax.random.split(_jax.random.fold_in(key, i + 1))
        out[name] = w
        out["m_" + name] = s * _jax.random.normal(km, w.shape, _jnp.float32)
        out["v_" + name] = (s * s) * _jax.random.uniform(kv, w.shape, _jnp.float32, 0.5, 1.5)
    if N_MICROBATCH > 1:
        for name, axis in PER_EXAMPLE_BATCH_AXIS.items():
            out[name] = _to_microbatches(out[name], axis)
    return {'x': out['x'], 'positions': out['positions'], 'norm_mix_pre': out['norm_mix_pre'], 'w_in': out['w_in'], 'dn_conv_w': out['dn_conv_w'], 'dn_a_log': out['dn_a_log'], 'dn_dt_bias': out['dn_dt_bias'], 'dn_norm_w': out['dn_norm_w'], 'pool_w': out['pool_w'], 'pool_scale': out['pool_scale'], 'swa_sinks': out['swa_sinks'], 'w_out': out['w_out'], 'norm_mix_post': out['norm_mix_post'], 'norm_ffn_pre': out['norm_ffn_pre'], 'ffn_w_up': out['ffn_w_up'], 'ffn_conv_w': out['ffn_conv_w'], 'ffn_conv_b': out['ffn_conv_b'], 'ffn_w_down': out['ffn_w_down'], 'norm_ffn_post': out['norm_ffn_post'], 'loss_target': out['loss_target'], 'm_norm_mix_pre': out['m_norm_mix_pre'], 'm_w_in': out['m_w_in'], 'm_dn_conv_w': out['m_dn_conv_w'], 'm_dn_a_log': out['m_dn_a_log'], 'm_dn_dt_bias': out['m_dn_dt_bias'], 'm_dn_norm_w': out['m_dn_norm_w'], 'm_pool_w': out['m_pool_w'], 'm_pool_scale': out['m_pool_scale'], 'm_swa_sinks': out['m_swa_sinks'], 'm_w_out': out['m_w_out'], 'm_norm_mix_post': out['m_norm_mix_post'], 'm_norm_ffn_pre': out['m_norm_ffn_pre'], 'm_ffn_w_up': out['m_ffn_w_up'], 'm_ffn_conv_w': out['m_ffn_conv_w'], 'm_ffn_conv_b': out['m_ffn_conv_b'], 'm_ffn_w_down': out['m_ffn_w_down'], 'm_norm_ffn_post': out['m_norm_ffn_post'], 'v_norm_mix_pre': out['v_norm_mix_pre'], 'v_w_in': out['v_w_in'], 'v_dn_conv_w': out['v_dn_conv_w'], 'v_dn_a_log': out['v_dn_a_log'], 'v_dn_dt_bias': out['v_dn_dt_bias'], 'v_dn_norm_w': out['v_dn_norm_w'], 'v_pool_w': out['v_pool_w'], 'v_pool_scale': out['v_pool_scale'], 'v_swa_sinks': out['v_swa_sinks'], 'v_w_out': out['v_w_out'], 'v_norm_mix_post': out['v_norm_mix_post'], 'v_norm_ffn_pre': out['v_norm_ffn_pre'], 'v_ffn_w_up': out['v_ffn_w_up'], 'v_ffn_conv_w': out['v_ffn_conv_w'], 'v_ffn_conv_b': out['v_ffn_conv_b'], 'v_ffn_w_down': out['v_ffn_w_down'], 'v_norm_ffn_post': out['v_norm_ffn_post']}


def _loss(weights, diff, rest, loss_target):
    with _jax.named_scope("forward"):
        args = {**rest, TWIN_DIFF_INPUT: diff, **{k: w.astype(_WEIGHT_DTYPES[k]) for k, w in weights.items()}}
        y = _forward(args)
    with _jax.named_scope("loss_head"):
        err = _jnp.square(y.astype(_jnp.float32) - loss_target)
        return 0.5 * _jnp.sum(_jnp.mean(err, axis=-1)) if err.ndim else 0.5 * err


def _adamw(w, g, m, v):
    m = ADAM_B1 * m + (1.0 - ADAM_B1) * g
    v = ADAM_B2 * v + (1.0 - ADAM_B2) * _jnp.square(g)
    m_hat = m / (1.0 - ADAM_B1 ** ADAM_STEP)
    v_hat = v / (1.0 - ADAM_B2 ** ADAM_STEP)
    delta = -ADAM_LR * (m_hat / (_jnp.sqrt(v_hat) + ADAM_EPS) + ADAM_WD * w)
    return delta, m, v


def reference(x, positions, norm_mix_pre, w_in, dn_conv_w, dn_a_log, dn_dt_bias, dn_norm_w, pool_w, pool_scale, swa_sinks, w_out, norm_mix_post, norm_ffn_pre, ffn_w_up, ffn_conv_w, ffn_conv_b, ffn_w_down, norm_ffn_post, loss_target, m_norm_mix_pre, m_w_in, m_dn_conv_w, m_dn_a_log, m_dn_dt_bias, m_dn_norm_w, m_pool_w, m_pool_scale, m_swa_sinks, m_w_out, m_norm_mix_post, m_norm_ffn_pre, m_ffn_w_up, m_ffn_conv_w, m_ffn_conv_b, m_ffn_w_down, m_norm_ffn_post, v_norm_mix_pre, v_w_in, v_dn_conv_w, v_dn_a_log, v_dn_dt_bias, v_dn_norm_w, v_pool_w, v_pool_scale, v_swa_sinks, v_w_out, v_norm_mix_post, v_norm_ffn_pre, v_ffn_w_up, v_ffn_conv_w, v_ffn_conv_b, v_ffn_w_down, v_norm_ffn_post):
    given = dict(x=x, positions=positions, norm_mix_pre=norm_mix_pre, w_in=w_in, dn_conv_w=dn_conv_w, dn_a_log=dn_a_log, dn_dt_bias=dn_dt_bias, dn_norm_w=dn_norm_w, pool_w=pool_w, pool_scale=pool_scale, swa_sinks=swa_sinks, w_out=w_out, norm_mix_post=norm_mix_post, norm_ffn_pre=norm_ffn_pre, ffn_w_up=ffn_w_up, ffn_conv_w=ffn_conv_w, ffn_conv_b=ffn_conv_b, ffn_w_down=ffn_w_down, norm_ffn_post=norm_ffn_post, loss_target=loss_target, m_norm_mix_pre=m_norm_mix_pre, m_w_in=m_w_in, m_dn_conv_w=m_dn_conv_w, m_dn_a_log=m_dn_a_log, m_dn_dt_bias=m_dn_dt_bias, m_dn_norm_w=m_dn_norm_w, m_pool_w=m_pool_w, m_pool_scale=m_pool_scale, m_swa_sinks=m_swa_sinks, m_w_out=m_w_out, m_norm_mix_post=m_norm_mix_post, m_norm_ffn_pre=m_norm_ffn_pre, m_ffn_w_up=m_ffn_w_up, m_ffn_conv_w=m_ffn_conv_w, m_ffn_conv_b=m_ffn_conv_b, m_ffn_w_down=m_ffn_w_down, m_norm_ffn_post=m_norm_ffn_post, v_norm_mix_pre=v_norm_mix_pre, v_w_in=v_w_in, v_dn_conv_w=v_dn_conv_w, v_dn_a_log=v_dn_a_log, v_dn_dt_bias=v_dn_dt_bias, v_dn_norm_w=v_dn_norm_w, v_pool_w=v_pool_w, v_pool_scale=v_pool_scale, v_swa_sinks=v_swa_sinks, v_w_out=v_w_out, v_norm_mix_post=v_norm_mix_post, v_norm_ffn_pre=v_norm_ffn_pre, v_ffn_w_up=v_ffn_w_up, v_ffn_conv_w=v_ffn_conv_w, v_ffn_conv_b=v_ffn_conv_b, v_ffn_w_down=v_ffn_w_down, v_norm_ffn_post=v_norm_ffn_post)
    weights = {n: given[n] for n in TWIN_WEIGHTS}
    shared = {n: given[n] for n in SHARED_INPUTS}
    per_example = {n: given[n] for n in ['x', 'positions']}
    grad_fn = _jax.value_and_grad(_loss, argnums=(0, 1))

    def one_microbatch(ex, loss_target):
        ex = dict(ex)
        diff = ex.pop(TWIN_DIFF_INPUT)
        return grad_fn(weights, diff, {**shared, **ex}, loss_target)

    if N_MICROBATCH == 1:
        loss, (grad_w, grad_x) = one_microbatch(per_example, given["loss_target"])
    else:
        def body(carry, xs):
            loss_sum, grad_sum = carry
            l_k, (gw_k, gx_k) = one_microbatch(xs[0], xs[1])
            with _jax.named_scope("update"):
                return (loss_sum + l_k, _jax.tree.map(_jnp.add, grad_sum, gw_k)), gx_k

        init = (_jnp.zeros((), _jnp.float32), _jax.tree.map(_jnp.zeros_like, weights))
        (loss, grad_w), grad_x = _jax.lax.scan(body, init, (per_example, given["loss_target"]))
    with _jax.named_scope("update"):
        delta_w, new_m, new_v = {}, {}, {}
        for n in TWIN_WEIGHTS:
            delta_w[n], new_m[n], new_v[n] = _adamw(weights[n], grad_w[n], given["m_" + n], given["v_" + n])
    return (loss, grad_x, *[grad_w[n] for n in TWIN_WEIGHTS], *[delta_w[n] for n in TWIN_WEIGHTS],
            *[new_m[n] for n in TWIN_WEIGHTS], *[new_v[n] for n in TWIN_WEIGHTS])
```

```python
import functools
import math

import jax
import jax.numpy as jnp
from jax import lax
from jax.experimental import pallas as pl
from jax.experimental.pallas import tpu as pltpu

F32 = jnp.float32
BF16 = jnp.bfloat16

HEAD_DIM = 128
DN_HEADS = 6
DN_CONV = 4
DN_CHUNK = 64
POOL_GROUPS = 4
SWA_HEADS = 6
SWA_KV_HEADS = 2
SWA_GROUP = SWA_HEADS // SWA_KV_HEADS
SWA_BLOCK = 128
ROPE_THETA = 10000.0
FFN_CONV = 3
NORM_EPS = 1e-6
DN_W = DN_HEADS * HEAD_DIM
POOL_W = POOL_GROUPS * HEAD_DIM
SWA_W = SWA_HEADS * HEAD_DIM
SWA_KV_W = SWA_KV_HEADS * HEAD_DIM
MIX_W = DN_W + POOL_W + SWA_W
IN_TRUE = 3 * DN_W + DN_W + 2 * DN_HEADS + POOL_W + SWA_W + 2 * SWA_KV_W
GATE_END = 4 * DN_W + 2 * DN_HEADS
CB_Z = 18
CB_BD = 24
CB_POOL = 25
CB_SQ = 29
CB_SK = 35
CB_SV = 37
IN_PAD = 40 * 128
ADAM_LR, ADAM_B1, ADAM_B2, ADAM_EPS, ADAM_WD, ADAM_STEP = 0.001, 0.9, 0.999, 1e-08, 0.01, 10

VMEM_LIMIT = 48 * 1024 * 1024
HIGH = lax.Precision.HIGHEST


def _cp(sem):
    return pltpu.CompilerParams(dimension_semantics=sem, vmem_limit_bytes=VMEM_LIMIT)


def _tile(n, prefs):
    for p in prefs:
        if n % p == 0:
            return p
    return n


def _rows(t):
    return _tile(t, (256, 128))


_DN = {"nn": (((1,), (0,)), ((), ())), "nt": (((1,), (1,)), ((), ())), "tn": (((0,), (0,)), ((), ()))}


def _mm(a, b, mode, out_dtype, name):
    if mode == "tn":
        k, m = a.shape
    else:
        m, k = a.shape
    n = b.shape[0] if mode == "nt" else b.shape[1]
    tm = _tile(m, (1024, 512, 256, 128))
    tn = _tile(n, (1024, 512, 256, 128))
    tk = _tile(k, (512, 256, 128))
    nk = k // tk

    def body(a_ref, b_ref, o_ref, acc_ref):
        kk = pl.program_id(2)

        @pl.when(kk == 0)
        def _():
            acc_ref[...] = jnp.zeros_like(acc_ref)

        acc_ref[...] += lax.dot_general(a_ref[...], b_ref[...], _DN[mode], preferred_element_type=F32)

        @pl.when(kk == nk - 1)
        def _():
            o_ref[...] = acc_ref[...].astype(o_ref.dtype)

    a_spec = pl.BlockSpec((tk, tm), lambda i, j, kk: (kk, i)) if mode == "tn" else pl.BlockSpec((tm, tk), lambda i, j, kk: (i, kk))
    b_spec = pl.BlockSpec((tn, tk), lambda i, j, kk: (j, kk)) if mode == "nt" else pl.BlockSpec((tk, tn), lambda i, j, kk: (kk, j))
    return pl.pallas_call(
        body, name=name, grid=(m // tm, n // tn, nk),
        in_specs=[a_spec, b_spec], out_specs=pl.BlockSpec((tm, tn), lambda i, j, kk: (i, j)),
        out_shape=jax.ShapeDtypeStruct((m, n), out_dtype),
        scratch_shapes=[pltpu.VMEM((tm, tn), F32)],
        compiler_params=_cp(("parallel", "parallel", "arbitrary")),
    )(a, b)


def _rms(x, w):
    return x * lax.rsqrt(jnp.mean(x * x, axis=-1, keepdims=True) + NORM_EPS) * w


def _norm_fwd(x, w, name):
    t, d = x.shape
    r = _rows(t)

    def body(x_ref, w_ref, h_ref):
        h_ref[...] = _rms(x_ref[...], w_ref[...]).astype(h_ref.dtype)

    return pl.pallas_call(
        body, name=name, grid=(t // r,),
        in_specs=[pl.BlockSpec((r, d), lambda i: (i, 0)), pl.BlockSpec((1, d), lambda i: (0, 0))],
        out_specs=pl.BlockSpec((r, d), lambda i: (i, 0)),
        out_shape=jax.ShapeDtypeStruct((t, d), BF16), compiler_params=_cp(("parallel",)),
    )(x, w)


def _resnorm_fwd(x, y, w, name):
    t, d = x.shape
    r = _rows(t)

    def body(x_ref, y_ref, w_ref, o_ref):
        o_ref[...] = x_ref[...] + _rms(y_ref[...], w_ref[...])

    return pl.pallas_call(
        body, name=name, grid=(t // r,),
        in_specs=[pl.BlockSpec((r, d), lambda i: (i, 0)), pl.BlockSpec((r, d), lambda i: (i, 0)),
                  pl.BlockSpec((1, d), lambda i: (0, 0))],
        out_specs=pl.BlockSpec((r, d), lambda i: (i, 0)),
        out_shape=jax.ShapeDtypeStruct((t, d), F32), compiler_params=_cp(("parallel",)),
    )(x, y, w)


def _norm_bwd(x, w, dh, add, out_dtype, name):
    t, d = x.shape
    r = _rows(t)
    has_add = add is not None

    def body(*refs):
        if has_add:
            x_ref, w_ref, dh_ref, add_ref, dx_ref, dw_ref = refs
        else:
            x_ref, w_ref, dh_ref, dx_ref, dw_ref = refs
        xv = x_ref[...]
        g = dh_ref[...].astype(F32)
        rs = lax.rsqrt(jnp.mean(xv * xv, axis=-1, keepdims=True) + NORM_EPS)
        xh = xv * rs
        gw = g * w_ref[...]
        dx = rs * (gw - xh * jnp.mean(gw * xh, axis=-1, keepdims=True))
        if has_add:
            dx = dx + add_ref[...]
        dx_ref[...] = dx.astype(dx_ref.dtype)

        @pl.when(pl.program_id(0) == 0)
        def _():
            dw_ref[...] = jnp.zeros_like(dw_ref)

        dw_ref[...] += jnp.sum(g * xh, axis=0, keepdims=True)

    row = pl.BlockSpec((r, d), lambda i: (i, 0))
    vec = pl.BlockSpec((1, d), lambda i: (0, 0))
    ins = [x, w, dh] + ([add] if has_add else [])
    return pl.pallas_call(
        body, name=name, grid=(t // r,),
        in_specs=[row, vec, row] + ([row] if has_add else []),
        out_specs=[row, vec],
        out_shape=[jax.ShapeDtypeStruct((t, d), out_dtype), jax.ShapeDtypeStruct((1, d), F32)],
        compiler_params=_cp(("arbitrary",)),
    )(*ins)


def _loss_head(y, target, name):
    t, d = y.shape
    r = _rows(t)

    def body(y_ref, t_ref, l_ref, g_ref):
        e = y_ref[...] - t_ref[...]
        g_ref[...] = e * (1.0 / d)

        @pl.when(pl.program_id(0) == 0)
        def _():
            l_ref[...] = jnp.zeros_like(l_ref)

        l_ref[...] += jnp.sum(e * e) * (0.5 / d)

    row = pl.BlockSpec((r, d), lambda i: (i, 0))
    return pl.pallas_call(
        body, name=name, grid=(t // r,), in_specs=[row, row],
        out_specs=[pl.BlockSpec((1, 128), lambda i: (0, 0)), row],
        out_shape=[jax.ShapeDtypeStruct((1, 128), F32), jax.ShapeDtypeStruct((t, d), F32)],
        compiler_params=_cp(("arbitrary",)),
    )(y, target)


def _down(x, s):
    return x if s == 0 else pltpu.roll(x, s, 0)


def _up(x, s):
    return x if s == 0 else pltpu.roll(x, x.shape[0] - s, 0)


def _halo(t, r, hh, tc, col):
    q = r // hh
    last = t // hh - 1
    tile = pl.BlockSpec((r, tc), lambda j, i: (i, col(j)))
    prev = pl.BlockSpec((hh, tc), lambda j, i: (jnp.maximum(i * q - 1, 0), col(j)))
    nxt = pl.BlockSpec((hh, tc), lambda j, i: (jnp.minimum((i + 1) * q, last), col(j)))
    return tile, prev, nxt


def _sig(x):
    return 1.0 / (1.0 + jnp.exp(-x))


def _dsilu(x, s):
    return s * (1.0 + x * (1.0 - s))


def _dn_pre_fwd(p, conv_w, name):
    t = p.shape[0]
    r = _rows(t)
    nc = 3 * DN_HEADS

    def body(x_ref, xp_ref, w_ref, o_ref):
        j, i = pl.program_id(0), pl.program_id(1)
        xe = jnp.concatenate([jnp.where(i == 0, 0.0, xp_ref[...]), x_ref[...]], axis=0)
        c = sum(_down(xe, DN_CONV - 1 - k) * w_ref[pl.ds(k, 1), :] for k in range(DN_CONV))[8:]
        a = c * _sig(c)
        fac = lax.rsqrt(jnp.sum(a * a, axis=-1, keepdims=True) + NORM_EPS)
        o_ref[...] = a * jnp.where(j < DN_HEADS, fac * HEAD_DIM ** -0.5, jnp.where(j < 2 * DN_HEADS, fac, 1.0))

    tile, prev, _ = _halo(t, r, 8, 128, lambda j: j)
    return pl.pallas_call(
        body, name=name, grid=(nc, t // r),
        in_specs=[tile, prev, pl.BlockSpec((DN_CONV, 128), lambda j, i: (0, j))],
        out_specs=tile, out_shape=jax.ShapeDtypeStruct((t, nc * 128), F32),
        compiler_params=_cp(("parallel", "parallel")),
    )(p, p, conv_w)


def _dn_pre_bwd(p, conv_w, dqkv, name):
    t = p.shape[0]
    r = _rows(t)
    ni = t // r
    nc = 3 * DN_HEADS

    def body(x_ref, xp_ref, xn_ref, w_ref, d_ref, dn_ref, dx_ref, dw_ref):
        j, i = pl.program_id(0), pl.program_id(1)
        xe = jnp.concatenate([jnp.where(i == 0, 0.0, xp_ref[...]), x_ref[...], xn_ref[...]], axis=0)
        de = jnp.concatenate([jnp.zeros((8, 128), F32), d_ref[...], jnp.where(i == ni - 1, 0.0, dn_ref[...])], axis=0)
        xs = [_down(xe, DN_CONV - 1 - k) for k in range(DN_CONV)]
        c = sum(xs[k] * w_ref[pl.ds(k, 1), :] for k in range(DN_CONV))
        s = _sig(c)
        a = c * s
        fac = lax.rsqrt(jnp.sum(a * a, axis=-1, keepdims=True) + NORM_EPS)
        dot = jnp.sum(de * a, axis=-1, keepdims=True)
        dnorm = fac * de - a * (fac * fac * fac) * dot
        da = jnp.where(j < DN_HEADS, dnorm * HEAD_DIM ** -0.5, jnp.where(j < 2 * DN_HEADS, dnorm, de))
        dc = da * _dsilu(c, s)
        dx_ref[...] = sum(_up(dc, DN_CONV - 1 - k) * w_ref[pl.ds(k, 1), :] for k in range(DN_CONV))[8:8 + r]

        @pl.when(i == 0)
        def _():
            dw_ref[...] = jnp.zeros_like(dw_ref)

        for k in range(DN_CONV):
            dw_ref[pl.ds(k, 1), :] += jnp.sum((dc * xs[k])[8:8 + r], axis=0, keepdims=True)

    tile, prev, nxt = _halo(t, r, 8, 128, lambda j: j)
    wspec = pl.BlockSpec((DN_CONV, 128), lambda j, i: (0, j))
    return pl.pallas_call(
        body, name=name, grid=(nc, ni),
        in_specs=[tile, prev, nxt, wspec, tile, nxt],
        out_specs=[tile, wspec],
        out_shape=[jax.ShapeDtypeStruct((t, nc * 128), F32), jax.ShapeDtypeStruct((DN_CONV, nc * 128), F32)],
        compiler_params=_cp(("parallel", "arbitrary")),
    )(p, p, p, conv_w, dqkv, dqkv)


def _ffn_act_fwd(up, cw, cb, name):
    t, f2 = up.shape
    f = f2 // 2
    r = _rows(t)
    tc = _tile(f, (512, 256, 128))
    nj = f // tc

    def body(a_ref, ap_ref, b_ref, bp_ref, wa_ref, wb_ref, ca_ref, cb_ref, o_ref):
        i = pl.program_id(1)

        def conv(x_ref, xp_ref, w_ref, c_ref):
            xe = jnp.concatenate([jnp.where(i == 0, 0.0, xp_ref[...]), x_ref[...]], axis=0)
            return sum(_down(xe, FFN_CONV - 1 - k) * w_ref[pl.ds(k, 1), :] for k in range(FFN_CONV))[8:] + c_ref[...]

        ua = conv(a_ref, ap_ref, wa_ref, ca_ref)
        ub = conv(b_ref, bp_ref, wb_ref, cb_ref)
        o_ref[...] = (ua * _sig(ua) * ub).astype(o_ref.dtype)

    ta, pa, _ = _halo(t, r, 8, tc, lambda j: j)
    tb, pb, _ = _halo(t, r, 8, tc, lambda j: j + nj)
    wa = pl.BlockSpec((FFN_CONV, tc), lambda j, i: (0, j))
    wb = pl.BlockSpec((FFN_CONV, tc), lambda j, i: (0, j + nj))
    ca = pl.BlockSpec((1, tc), lambda j, i: (0, j))
    cbs = pl.BlockSpec((1, tc), lambda j, i: (0, j + nj))
    return pl.pallas_call(
        body, name=name, grid=(nj, t // r),
        in_specs=[ta, pa, tb, pb, wa, wb, ca, cbs], out_specs=ta,
        out_shape=jax.ShapeDtypeStruct((t, f), BF16), compiler_params=_cp(("parallel", "parallel")),
    )(up, up, up, up, cw, cw, cb, cb)


def _ffn_act_bwd(up, cw, cb, dact, name):
    t, f2 = up.shape
    f = f2 // 2
    r = _rows(t)
    ni = t // r
    tc = _tile(f, (512, 256, 128))
    nj = f // tc

    def body(a_ref, ap_ref, an_ref, b_ref, bp_ref, bn_ref, wa_ref, wb_ref, ca_ref, cb_ref, d_ref, dn_ref,
             dua_ref, dub_ref, dwa_ref, dwb_ref, dca_ref, dcb_ref):
        i = pl.program_id(1)

        def ext(x_ref, xp_ref, xn_ref):
            return jnp.concatenate([jnp.where(i == 0, 0.0, xp_ref[...]), x_ref[...], xn_ref[...]], axis=0)

        ae, be = ext(a_ref, ap_ref, an_ref), ext(b_ref, bp_ref, bn_ref)
        as_ = [_down(ae, FFN_CONV - 1 - k) for k in range(FFN_CONV)]
        bs_ = [_down(be, FFN_CONV - 1 - k) for k in range(FFN_CONV)]
        ua = sum(as_[k] * wa_ref[pl.ds(k, 1), :] for k in range(FFN_CONV)) + ca_ref[...]
        ub = sum(bs_[k] * wb_ref[pl.ds(k, 1), :] for k in range(FFN_CONV)) + cb_ref[...]
        de = jnp.concatenate([jnp.zeros((8, tc), F32), d_ref[...].astype(F32),
                              jnp.where(i == ni - 1, 0.0, dn_ref[...].astype(F32))], axis=0)
        s = _sig(ua)
        dua = de * ub * _dsilu(ua, s)
        dub = de * ua * s
        dua_ref[...] = sum(_up(dua, FFN_CONV - 1 - k) * wa_ref[pl.ds(k, 1), :] for k in range(FFN_CONV))[8:8 + r].astype(dua_ref.dtype)
        dub_ref[...] = sum(_up(dub, FFN_CONV - 1 - k) * wb_ref[pl.ds(k, 1), :] for k in range(FFN_CONV))[8:8 + r].astype(dub_ref.dtype)

        @pl.when(i == 0)
        def _():
            dwa_ref[...] = jnp.zeros_like(dwa_ref)
            dwb_ref[...] = jnp.zeros_like(dwb_ref)
            dca_ref[...] = jnp.zeros_like(dca_ref)
            dcb_ref[...] = jnp.zeros_like(dcb_ref)

        for k in range(FFN_CONV):
            dwa_ref[pl.ds(k, 1), :] += jnp.sum((dua * as_[k])[8:8 + r], axis=0, keepdims=True)
            dwb_ref[pl.ds(k, 1), :] += jnp.sum((dub * bs_[k])[8:8 + r], axis=0, keepdims=True)
        dca_ref[...] += jnp.sum(dua[8:8 + r], axis=0, keepdims=True)
        dcb_ref[...] += jnp.sum(dub[8:8 + r], axis=0, keepdims=True)

    ta, pa, na = _halo(t, r, 8, tc, lambda j: j)
    tb, pb, nb = _halo(t, r, 8, tc, lambda j: j + nj)
    wa = pl.BlockSpec((FFN_CONV, tc), lambda j, i: (0, j))
    wb = pl.BlockSpec((FFN_CONV, tc), lambda j, i: (0, j + nj))
    ca = pl.BlockSpec((1, tc), lambda j, i: (0, j))
    cbs = pl.BlockSpec((1, tc), lambda j, i: (0, j + nj))
    return pl.pallas_call(
        body, name=name, grid=(nj, ni),
        in_specs=[ta, pa, na, tb, pb, nb, wa, wb, ca, cbs, ta, na],
        out_specs=[ta, ta, wa, wa, ca, ca],
        out_shape=[jax.ShapeDtypeStruct((t, f), BF16), jax.ShapeDtypeStruct((t, f), BF16),
                   jax.ShapeDtypeStruct((FFN_CONV, f), F32), jax.ShapeDtypeStruct((FFN_CONV, f), F32),
                   jax.ShapeDtypeStruct((1, f), F32), jax.ShapeDtypeStruct((1, f), F32)],
        compiler_params=_cp(("parallel", "arbitrary")),
    )(up, up, up, up, up, up, cw, cw, cb, cb, dact, dact)


def _pool_pick(g, vals):
    return jnp.where(g == 0, vals[0], jnp.where(g == 1, vals[1], jnp.where(g == 2, vals[2], vals[3])))


def _pool_pre(xe, g, t0):
    s1 = xe + _down(xe, 1)
    s2 = s1 + _down(s1, 2)
    s3 = s2 + _down(s2, 4)
    s4 = s3 + _down(s3, 8)
    r = xe.shape[0] - 16
    pos = (t0 + lax.broadcasted_iota(jnp.int32, (r, 1), 0)).astype(F32)
    cnt = jnp.minimum(pos + 1.0, _pool_pick(g, (2.0, 4.0, 8.0, 16.0)))
    return _pool_pick(g, (s1, s2, s3, s4))[16:] / cnt - xe[16:]


def _pool_fwd(p, pool_w, scale, name):
    t = p.shape[0]
    r = _rows(t)

    def body(x_ref, xp_ref, w_ref, sc_ref, o_ref):
        g, i = pl.program_id(0), pl.program_id(1)
        xe = jnp.concatenate([jnp.where(i == 0, 0.0, xp_ref[...]), x_ref[...]], axis=0)
        pre = _pool_pre(xe, g, i * r)
        o_ref[...] = jnp.dot(pre, w_ref[0], preferred_element_type=F32) * sc_ref[...]

    tile, prev, _ = _halo(t, r, 16, 128, lambda j: CB_POOL + j)
    return pl.pallas_call(
        body, name=name, grid=(POOL_GROUPS, t // r),
        in_specs=[tile, prev, pl.BlockSpec((1, 128, 128), lambda j, i: (j, 0, 0)), pl.BlockSpec((1, 128), lambda j, i: (0, j))],
        out_specs=pl.BlockSpec((r, 128), lambda j, i: (i, j)),
        out_shape=jax.ShapeDtypeStruct((t, POOL_W), F32), compiler_params=_cp(("parallel", "parallel")),
    )(p, p, pool_w, scale)


def _pool_bwd(p, pool_w, scale, dycat, name):
    t = p.shape[0]
    r = _rows(t)
    ni = t // r

    def body(x_ref, xp_ref, w_ref, sc_ref, d_ref, dn_ref, dx_ref, dw_ref, dsc_ref):
        g, i = pl.program_id(0), pl.program_id(1)
        xe = jnp.concatenate([jnp.where(i == 0, 0.0, xp_ref[...]), x_ref[...]], axis=0)
        pre = _pool_pre(xe, g, i * r)
        w = w_ref[0]
        dy = d_ref[...]
        dye = jnp.concatenate([dy, jnp.where(i == ni - 1, 0.0, dn_ref[...])], axis=0)
        dpre = lax.dot_general(dye * sc_ref[...], w, _DN["nt"], preferred_element_type=F32)
        pos = (i * r + lax.broadcasted_iota(jnp.int32, (r + 16, 1), 0)).astype(F32)
        dm = dpre / jnp.minimum(pos + 1.0, _pool_pick(g, (2.0, 4.0, 8.0, 16.0)))
        a1 = dm + _up(dm, 1)
        a2 = a1 + _up(a1, 2)
        a3 = a2 + _up(a2, 4)
        a4 = a3 + _up(a3, 8)
        dx_ref[...] = (_pool_pick(g, (a1, a2, a3, a4)) - dpre)[:r]

        @pl.when(i == 0)
        def _():
            dw_ref[...] = jnp.zeros_like(dw_ref)
            dsc_ref[...] = jnp.zeros_like(dsc_ref)

        dw_ref[0] += lax.dot_general(pre, dy * sc_ref[...], _DN["tn"], preferred_element_type=F32)
        dsc_ref[...] += jnp.sum(dy * jnp.dot(pre, w, preferred_element_type=F32), axis=0, keepdims=True)

    tile, prev, _ = _halo(t, r, 16, 128, lambda j: CB_POOL + j)
    dtile, _, dnxt = _halo(t, r, 16, 128, lambda j: DN_W // 128 + j)
    wspec = pl.BlockSpec((1, 128, 128), lambda j, i: (j, 0, 0))
    sspec = pl.BlockSpec((1, 128), lambda j, i: (0, j))
    return pl.pallas_call(
        body, name=name, grid=(POOL_GROUPS, ni),
        in_specs=[tile, prev, wspec, sspec, dtile, dnxt],
        out_specs=[pl.BlockSpec((r, 128), lambda j, i: (i, j)), wspec, sspec],
        out_shape=[jax.ShapeDtypeStruct((t, POOL_W), F32), jax.ShapeDtypeStruct((POOL_GROUPS, 128, 128), F32),
                   jax.ShapeDtypeStruct((1, POOL_W), F32)],
        compiler_params=_cp(("parallel", "arbitrary")),
    )(p, p, pool_w, scale, dycat, dycat)


def _dot(a, b, mode="nn", precision=None):
    return lax.dot_general(a, b, _DN[mode], precision=precision, preferred_element_type=F32)


def _dn_consts():
    c = DN_CHUNK
    ii = lax.broadcasted_iota(jnp.int32, (c, c), 0)
    jj = lax.broadcasted_iota(jnp.int32, (c, c), 1)
    one, zero = jnp.ones((c, c), F32), jnp.zeros((c, c), F32)
    return dict(ltri=jnp.where(ii >= jj, one, zero), utri=jnp.where(ii <= jj, one, zero), ones=one,
                causal=ii >= jj, strict=ii > jj, eye=jnp.where(ii == jj, one, zero))


def _dn_chunk(q, k, v, z, bcol, acol, s_in, alog, dtb, nw, cs):
    c = DN_CHUNK
    beta = _sig(bcol)
    xa = acol + dtb
    g = -jnp.exp(alog) * (jnp.maximum(xa, 0.0) + jnp.log(1.0 + jnp.exp(-jnp.abs(xa))))
    gb = jnp.broadcast_to(g, (c, HEAD_DIM))
    gbc = jnp.broadcast_to(g, (c, c))
    gc = _dot(cs["ltri"], gb, precision=HIGH)
    gcol = _dot(cs["ltri"], gbc, precision=HIGH)
    grow = _dot(cs["ones"], gbc * cs["utri"], precision=HIGH)
    decay = jnp.exp(jnp.where(cs["causal"], gcol - grow, -1e30))
    kb = k * beta
    vb = v * beta
    nil = -jnp.where(cs["strict"], _dot(kb, k, "nt") * decay, 0.0)
    inv = cs["eye"] + nil
    powk = nil
    for _ in range(int(math.log2(c)) - 1):
        powk = _dot(powk, powk)
        inv = _dot(inv, cs["eye"] + powk)
    eg = jnp.exp(gc)
    u = _dot(inv, vb)
    w = _dot(inv, kb * eg)
    a = _dot(q, k, "nt") * decay
    v_new = u - _dot(w, s_in)
    o = _dot(q * eg, s_in) + _dot(a, v_new)
    glast = jnp.sum(gb, axis=0, keepdims=True)
    s_out = s_in * jnp.exp(glast) + _dot(k * jnp.exp(glast - gc), v_new, "tn")
    on = o * lax.rsqrt(jnp.mean(o * o, axis=-1, keepdims=True) + NORM_EPS) * nw
    return on * (z * _sig(z)), s_out


def _lane_pick(x, lane, idx):
    return jnp.sum(jnp.where(lane == idx, x, 0.0), axis=1, keepdims=True)


def _dn_fwd(qkv, p, alog, dtb, nw, name):
    t = qkv.shape[0]
    c = DN_CHUNK
    n = t // c

    def body(q_ref, k_ref, v_ref, z_ref, bd_ref, al_ref, dt_ref, nw_ref, y_ref, ss_ref, s_scr):
        @pl.when(pl.program_id(0) == 0)
        def _():
            s_scr[...] = jnp.zeros_like(s_scr)

        cs = _dn_consts()
        lane = lax.broadcasted_iota(jnp.int32, (1, 128), 1)
        bd = bd_ref[...]
        for h in range(DN_HEADS):
            sl = slice(h * 128, (h + 1) * 128)
            s_in = s_scr[h]
            ss_ref[0, h] = s_in
            y, s_out = _dn_chunk(q_ref[:, sl], k_ref[:, sl], v_ref[:, sl], z_ref[:, sl],
                                 _lane_pick(bd, lane, h), _lane_pick(bd, lane, DN_HEADS + h), s_in,
                                 _lane_pick(al_ref[...], lane, h), _lane_pick(dt_ref[...], lane, h), nw_ref[...], cs)
            y_ref[:, sl] = y
            s_scr[h] = s_out

    wide = lambda j: pl.BlockSpec((c, DN_W), lambda i: (i, j))
    vec = pl.BlockSpec((1, 128), lambda i: (0, 0))
    return pl.pallas_call(
        body, name=name, grid=(n,),
        in_specs=[wide(0), wide(1), wide(2), wide(3), pl.BlockSpec((c, 128), lambda i: (i, CB_BD)), vec, vec, vec],
        out_specs=[wide(0), pl.BlockSpec((1, DN_HEADS, 128, 128), lambda i: (i, 0, 0, 0))],
        out_shape=[jax.ShapeDtypeStruct((t, DN_W), F32), jax.ShapeDtypeStruct((n, DN_HEADS, 128, 128), F32)],
        scratch_shapes=[pltpu.VMEM((DN_HEADS, 128, 128), F32)],
        compiler_params=_cp(("arbitrary",)),
    )(qkv, qkv, qkv, p, p, alog, dtb, nw)


def _dn_bwd(qkv, p, alog, dtb, nw, states, dycat, name):
    t = qkv.shape[0]
    c = DN_CHUNK
    n = t // c

    def body(q_ref, k_ref, v_ref, z_ref, bd_ref, al_ref, dt_ref, nw_ref, ss_ref, dy_ref,
             dqkv_ref, dz_ref, dbd_ref, dal_ref, ddt_ref, dnw_ref, ds_scr):
        @pl.when(pl.program_id(0) == 0)
        def _():
            ds_scr[...] = jnp.zeros_like(ds_scr)
            dal_ref[...] = jnp.zeros_like(dal_ref)
            ddt_ref[...] = jnp.zeros_like(ddt_ref)
            dnw_ref[...] = jnp.zeros_like(dnw_ref)

        cs = _dn_consts()
        lane = lax.broadcasted_iota(jnp.int32, (1, 128), 1)
        bd = bd_ref[...]
        dbd = jnp.zeros((c, 128), F32)
        dal = jnp.zeros((1, 128), F32)
        ddt = jnp.zeros((1, 128), F32)
        dnw = jnp.zeros((1, 128), F32)
        for h in range(DN_HEADS):
            sl = slice(h * 128, (h + 1) * 128)
            args = (q_ref[:, sl], k_ref[:, sl], v_ref[:, sl], z_ref[:, sl],
                    _lane_pick(bd, lane, h), _lane_pick(bd, lane, DN_HEADS + h), ss_ref[0, h],
                    _lane_pick(al_ref[...], lane, h), _lane_pick(dt_ref[...], lane, h), nw_ref[...])
            _, vjp = jax.vjp(functools.partial(_dn_chunk, cs=cs), *args)
            gq, gk, gv, gz, gb, ga, gs, gal, gdt, gnw = vjp((dy_ref[:, sl], ds_scr[h]))
            dqkv_ref[:, sl] = gq
            dqkv_ref[:, DN_W + h * 128:DN_W + (h + 1) * 128] = gk
            dqkv_ref[:, 2 * DN_W + h * 128:2 * DN_W + (h + 1) * 128] = gv
            dz_ref[:, sl] = gz
            ds_scr[h] = gs
            dbd = dbd + jnp.where(lane == h, gb, 0.0) + jnp.where(lane == DN_HEADS + h, ga, 0.0)
            dal = dal + jnp.where(lane == h, gal, 0.0)
            ddt = ddt + jnp.where(lane == h, gdt, 0.0)
            dnw = dnw + gnw
        dbd_ref[...] = dbd
        dal_ref[...] += dal
        ddt_ref[...] += ddt
        dnw_ref[...] += dnw

    rev = lambda i: n - 1 - i
    wide = lambda j: pl.BlockSpec((c, DN_W), lambda i: (rev(i), j))
    vec = pl.BlockSpec((1, 128), lambda i: (0, 0))
    return pl.pallas_call(
        body, name=name, grid=(n,),
        in_specs=[wide(0), wide(1), wide(2), wide(3), pl.BlockSpec((c, 128), lambda i: (rev(i), CB_BD)), vec, vec, vec,
                  pl.BlockSpec((1, DN_HEADS, 128, 128), lambda i: (rev(i), 0, 0, 0)), wide(0)],
        out_specs=[pl.BlockSpec((c, 3 * DN_W), lambda i: (rev(i), 0)), wide(0),
                   pl.BlockSpec((c, 128), lambda i: (rev(i), 0)), vec, vec, vec],
        out_shape=[jax.ShapeDtypeStruct((t, 3 * DN_W), F32), jax.ShapeDtypeStruct((t, DN_W), F32),
                   jax.ShapeDtypeStruct((t, 128), F32), jax.ShapeDtypeStruct((1, 128), F32),
                   jax.ShapeDtypeStruct((1, 128), F32), jax.ShapeDtypeStruct((1, 128), F32)],
        scratch_shapes=[pltpu.VMEM((DN_HEADS, 128, 128), F32)],
        compiler_params=_cp(("arbitrary",)),
    )(qkv, qkv, qkv, p, p, alog, dtb, nw, states, dycat)


def _rope(x, cosf, sins):
    return x * cosf + pltpu.roll(x, HEAD_DIM // 2, 1) * sins


def _rope_t(d, cosf, sins):
    return d * cosf + pltpu.roll(d * sins, HEAD_DIM // 2, 1)


def _swa_masks():
    b = SWA_BLOCK
    i = lax.broadcasted_iota(jnp.int32, (SWA_GROUP * b, b), 0) & (b - 1)
    j = lax.broadcasted_iota(jnp.int32, (SWA_GROUP * b, b), 1)
    return j > i, j <= i


def _swa_sink_col(sinks_ref, h):
    b = SWA_BLOCK
    r = lax.broadcasted_iota(jnp.int32, (SWA_GROUP * b, 1), 0)
    s = [sinks_ref[h * SWA_GROUP + g] for g in range(SWA_GROUP)]
    return jnp.where(r < b, s[0], jnp.where(r < 2 * b, s[1], s[2]))


def _swa_specs(t, h_first):
    nb = t // SWA_BLOCK

    def at(col, off):
        def imap(h, n):
            return (jnp.clip(n + off, 0, nb - 1), col(h))
        return pl.BlockSpec((SWA_BLOCK, 128), imap)
    return at


def _swa_fwd(p, cosf, sins, sinks, name):
    t = p.shape[0]
    b = SWA_BLOCK
    nb = t // b
    at = _swa_specs(t, None)
    scale = HEAD_DIM ** -0.5

    def body(q0, q1, q2, kp, kc, vp, vc, cc, sc, cp, sp, sinks_ref, o_ref, lse_ref):
        h, n = pl.program_id(0), pl.program_id(1)
        qs = jnp.concatenate([_rope(q[...], cc[...], sc[...]) for q in (q0, q1, q2)], axis=0)
        ks = jnp.concatenate([_rope(kp[...], cp[...], sp[...]), _rope(kc[...], cc[...], sc[...])], axis=0)
        vs = jnp.concatenate([vp[...], vc[...]], axis=0)
        mp, mc = _swa_masks()
        mask = jnp.concatenate([mp & (n > 0), mc], axis=1)
        s = jnp.where(mask, _dot(qs, ks, "nt") * scale, -1e30)
        sink = _swa_sink_col(sinks_ref, h)
        m = jnp.maximum(jnp.max(s, axis=1, keepdims=True), sink)
        e = jnp.exp(s - m)
        l = jnp.sum(e, axis=1, keepdims=True) + jnp.exp(sink - m)
        o = _dot(e, vs) / l
        lse = m + jnp.log(l)
        lane = lax.broadcasted_iota(jnp.int32, (1, 128), 1)
        tile = jnp.zeros((b, 128), F32)
        for g in range(SWA_GROUP):
            o_ref[:, g * 128:(g + 1) * 128] = o[g * b:(g + 1) * b]
            tile = tile + jnp.where(lane == g, lse[g * b:(g + 1) * b], 0.0)
        lse_ref[0] = tile

    qcol = lambda g: (lambda h: CB_SQ + h * SWA_GROUP + g)
    kcol, vcol, one = (lambda h: CB_SK + h), (lambda h: CB_SV + h), (lambda h: 0)
    in_specs = [at(qcol(0), 0), at(qcol(1), 0), at(qcol(2), 0), at(kcol, -1), at(kcol, 0), at(vcol, -1), at(vcol, 0),
                at(one, 0), at(one, 0), at(one, -1), at(one, -1), pl.BlockSpec(memory_space=pltpu.SMEM)]
    return pl.pallas_call(
        body, name=name, grid=(SWA_KV_HEADS, nb), in_specs=in_specs,
        out_specs=[pl.BlockSpec((b, SWA_GROUP * 128), lambda h, n: (n, h)), pl.BlockSpec((1, b, 128), lambda h, n: (h, n, 0))],
        out_shape=[jax.ShapeDtypeStruct((t, SWA_W), F32), jax.ShapeDtypeStruct((SWA_KV_HEADS, t, 128), F32)],
        compiler_params=_cp(("parallel", "parallel")),
    )(p, p, p, p, p, p, p, cosf, sins, cosf, sins, sinks)


def _swa_bwd(p, cosf, sins, sinks, o, lse, dycat, name):
    t = p.shape[0]
    b = SWA_BLOCK
    nb = t // b
    at = _swa_specs(t, None)
    scale = HEAD_DIM ** -0.5
    gb = SWA_GROUP * b

    def body(q0, q1, q2, r0, r1, r2, kp, kc, vp, vc, cc, sc, cp, sp, cn, sn, d0, d1, d2, e0, e1, e2,
             oc_ref, on_ref, lc_ref, ln_ref, sinks_ref, dq_ref, dk_ref, dv_ref, dsk_ref):
        h, n = pl.program_id(0), pl.program_id(1)
        lane = lax.broadcasted_iota(jnp.int32, (1, 128), 1)
        stack = lambda refs: jnp.concatenate([x[...] for x in refs], axis=0)
        q_c = jnp.concatenate([_rope(q[...], cc[...], sc[...]) for q in (q0, q1, q2)], axis=0)
        q_n = jnp.concatenate([_rope(q[...], cn[...], sn[...]) for q in (r0, r1, r2)], axis=0)
        k_p = _rope(kp[...], cp[...], sp[...])
        k_c = _rope(kc[...], cc[...], sc[...])
        do_c, do_n = stack((d0, d1, d2)), stack((e0, e1, e2))
        o_c = jnp.concatenate([oc_ref[:, g * 128:(g + 1) * 128] for g in range(SWA_GROUP)], axis=0)
        o_n = jnp.concatenate([on_ref[:, g * 128:(g + 1) * 128] for g in range(SWA_GROUP)], axis=0)
        lse_c = jnp.concatenate([_lane_pick(lc_ref[0], lane, g) for g in range(SWA_GROUP)], axis=0)
        lse_n = jnp.concatenate([_lane_pick(ln_ref[0], lane, g) for g in range(SWA_GROUP)], axis=0)
        dl_c = jnp.sum(do_c * o_c, axis=1, keepdims=True)
        dl_n = jnp.sum(do_n * o_n, axis=1, keepdims=True)
        mp, mc = _swa_masks()

        def pair(qr, kr, v, do, lse_, dl, mask):
            s = _dot(qr, kr, "nt") * scale
            pr = jnp.where(mask, jnp.exp(s - lse_), 0.0)
            ds = pr * (_dot(do, v, "nt") - dl) * scale
            return _dot(ds, kr), _dot(ds, qr, "tn"), _dot(pr, do, "tn")

        dq_a, _, _ = pair(q_c, k_p, vp[...], do_c, lse_c, dl_c, mp & (n > 0))
        dq_b, dk_b, dv_b = pair(q_c, k_c, vc[...], do_c, lse_c, dl_c, mc)
        _, dk_n, dv_n = pair(q_n, k_c, vc[...], do_n, lse_n, dl_n, mp & (n < nb - 1))
        dq = dq_a + dq_b
        for g in range(SWA_GROUP):
            dq_ref[:, g * 128:(g + 1) * 128] = _rope_t(dq[g * b:(g + 1) * b], cc[...], sc[...])
        dk_ref[...] = _rope_t(dk_b + dk_n, cc[...], sc[...])
        dv_ref[...] = dv_b + dv_n

        @pl.when(n == 0)
        def _():
            dsk_ref[...] = jnp.zeros_like(dsk_ref)

        w = -jnp.exp(_swa_sink_col(sinks_ref, h) - lse_c) * dl_c
        acc = jnp.zeros((1, 128), F32)
        for g in range(SWA_GROUP):
            acc = acc + jnp.where(lane == g, jnp.sum(w[g * b:(g + 1) * b], axis=0, keepdims=True), 0.0)
        dsk_ref[0] += jnp.broadcast_to(acc, (8, 128))

    qcol = lambda g: (lambda h: CB_SQ + h * SWA_GROUP + g)
    dcol = lambda g: (lambda h: (DN_W + POOL_W) // 128 + h * SWA_GROUP + g)
    kcol, vcol, one = (lambda h: CB_SK + h), (lambda h: CB_SV + h), (lambda h: 0)
    wide = lambda off: pl.BlockSpec((b, SWA_GROUP * 128), lambda h, n: (jnp.clip(n + off, 0, nb - 1), h))
    lspec = lambda off: pl.BlockSpec((1, b, 128), lambda h, n: (h, jnp.clip(n + off, 0, nb - 1), 0))
    in_specs = ([at(qcol(g), 0) for g in range(3)] + [at(qcol(g), 1) for g in range(3)]
                + [at(kcol, -1), at(kcol, 0), at(vcol, -1), at(vcol, 0)]
                + [at(one, 0), at(one, 0), at(one, -1), at(one, -1), at(one, 1), at(one, 1)]
                + [at(dcol(g), 0) for g in range(3)] + [at(dcol(g), 1) for g in range(3)]
                + [wide(0), wide(1), lspec(0), lspec(1), pl.BlockSpec(memory_space=pltpu.SMEM)])
    kv_out = pl.BlockSpec((b, 128), lambda h, n: (n, h))
    return pl.pallas_call(
        body, name=name, grid=(SWA_KV_HEADS, nb), in_specs=in_specs,
        out_specs=[wide(0), kv_out, kv_out, pl.BlockSpec((1, 8, 128), lambda h, n: (h, 0, 0))],
        out_shape=[jax.ShapeDtypeStruct((t, SWA_W), F32), jax.ShapeDtypeStruct((t, SWA_KV_W), F32),
                   jax.ShapeDtypeStruct((t, SWA_KV_W), F32), jax.ShapeDtypeStruct((SWA_KV_HEADS, 8, 128), F32)],
        compiler_params=_cp(("parallel", "arbitrary")),
    )(*([p] * 10), cosf, sins, cosf, sins, cosf, sins, *([dycat] * 6), o, o, lse, lse, sinks)


def _adam_math(w, g, m, v):
    m = ADAM_B1 * m + (1.0 - ADAM_B1) * g
    v = ADAM_B2 * v + (1.0 - ADAM_B2) * (g * g)
    m_hat = m / (1.0 - ADAM_B1 ** ADAM_STEP)
    v_hat = v / (1.0 - ADAM_B2 ** ADAM_STEP)
    return -ADAM_LR * (m_hat / (jnp.sqrt(v_hat) + ADAM_EPS) + ADAM_WD * w), m, v


def _adamw(w, g, m, v, name):
    shape = w.shape
    cols = shape[-1]
    rows = math.prod(shape[:-1])
    flat = lambda a: a.reshape(rows, cols)
    r = rows
    for cand in (512, 256, 128, 64, 32, 16, 8):
        if rows % cand == 0 and cand * cols * 4 <= (1 << 20):
            r = cand
            break

    def body(w_ref, g_ref, m_ref, v_ref, d_ref, nm_ref, nv_ref):
        d_ref[...], nm_ref[...], nv_ref[...] = _adam_math(w_ref[...], g_ref[...], m_ref[...], v_ref[...])

    spec = pl.BlockSpec((r, cols), lambda i: (i, 0))
    outs = pl.pallas_call(
        body, name=name, grid=(rows // r,), in_specs=[spec] * 4, out_specs=[spec] * 3,
        out_shape=[jax.ShapeDtypeStruct((rows, cols), F32)] * 3, compiler_params=_cp(("parallel",)),
    )(flat(w), flat(g), flat(m), flat(v))
    return tuple(o.reshape(shape) for o in outs)


BIG = ("w_in", "dn_conv_w", "w_out", "ffn_w_up", "ffn_conv_w", "ffn_w_down")
SHARD_AXIS = {"w_in": 2, "dn_conv_w": 2, "w_out": 1, "ffn_w_up": 2, "ffn_conv_w": 2, "ffn_w_down": 1}
SMALL = ("norm_mix_pre", "dn_a_log", "dn_dt_bias", "dn_norm_w", "pool_w", "pool_scale", "swa_sinks",
         "norm_mix_post", "norm_ffn_pre", "ffn_conv_b", "norm_ffn_post")
WEIGHTS = ("norm_mix_pre", "w_in", "dn_conv_w", "dn_a_log", "dn_dt_bias", "dn_norm_w", "pool_w", "pool_scale",
           "swa_sinks", "w_out", "norm_mix_post", "norm_ffn_pre", "ffn_w_up", "ffn_conv_w", "ffn_conv_b",
           "ffn_w_down", "norm_ffn_post")


def _pad_in(w):
    z = lambda n: jnp.zeros(w.shape[:-1] + (n,), w.dtype)
    return jnp.concatenate([w[..., :GATE_END], z(CB_POOL * 128 - GATE_END), w[..., GATE_END:],
                            z(IN_PAD - CB_POOL * 128 - (IN_TRUE - GATE_END))], axis=-1)


def _unpad_in(g):
    return jnp.concatenate([g[..., :GATE_END], g[..., CB_POOL * 128:CB_POOL * 128 + IN_TRUE - GATE_END]], axis=-1)


def _lanes(v):
    return jnp.zeros((1, 128), F32).at[0, :v.shape[0]].set(v)


def _rope_tables(positions):
    inv_freq = 1.0 / (ROPE_THETA ** (jnp.arange(0, HEAD_DIM, 2, dtype=F32) / HEAD_DIM))
    ang = positions.astype(F32)[:, None] * inv_freq
    cos, sin = jnp.cos(ang), jnp.sin(ang)
    return jnp.concatenate([cos, cos], axis=-1), jnp.concatenate([-sin, sin], axis=-1)


def _local_step(x, positions, target, w):
    depth = w["w_out"].shape[0]
    t = x.shape[0]
    cosf, sins = _rope_tables(positions)
    saved = []
    for l in range(depth):
        nm = f"l{l}_"
        n1, n2, n3, n4 = (w[k][l][None] for k in ("norm_mix_pre", "norm_mix_post", "norm_ffn_pre", "norm_ffn_post"))
        alog, dtb, dnw = _lanes(w["dn_a_log"][l]), _lanes(w["dn_dt_bias"][l]), w["dn_norm_w"][l][None]
        psc, cb = w["pool_scale"][l][None], w["ffn_conv_b"][l][None]
        h = _norm_fwd(x, n1, nm + "norm1")
        p = _mm(h, w["w_in"][l], "nn", F32, nm + "in_proj")
        qkv = _dn_pre_fwd(p, w["dn_conv_w"][l], nm + "dn_pre")
        y_dn, st = _dn_fwd(qkv, p, alog, dtb, dnw, nm + "dn")
        y_pool = _pool_fwd(p, w["pool_w"][l], psc, nm + "pool")
        y_swa, lse = _swa_fwd(p, cosf, sins, w["swa_sinks"][l], nm + "swa")
        ycat = jnp.concatenate([y_dn, y_pool, y_swa], axis=1).astype(BF16)
        mix = _mm(ycat, w["w_out"][l], "nn", F32, nm + "out_proj")
        x1 = _resnorm_fwd(x, mix, n2, nm + "res1")
        h2 = _norm_fwd(x1, n3, nm + "norm3")
        up = _mm(h2, w["ffn_w_up"][l], "nn", F32, nm + "ffn_up")
        act = _ffn_act_fwd(up, w["ffn_conv_w"][l], cb, nm + "ffn_act")
        f = _mm(act, w["ffn_w_down"][l], "nn", F32, nm + "ffn_down")
        x2 = _resnorm_fwd(x1, f, n4, nm + "res2")
        saved.append(dict(x=x, h=h, p=p, qkv=qkv, st=st, y_swa=y_swa, lse=lse, ycat=ycat, mix=mix, x1=x1, h2=h2,
                          up=up, act=act, f=f, n=(n1, n2, n3, n4), alog=alog, dtb=dtb, dnw=dnw, psc=psc, cb=cb))
        x = x2
    loss, dx = _loss_head(x, target, "loss_head")
    grads = {k: [None] * depth for k in WEIGHTS}
    for l in reversed(range(depth)):
        nm, s = f"l{l}_b_", saved[l]
        n1, n2, n3, n4 = s["n"]
        df, g4 = _norm_bwd(s["f"], n4, dx, None, BF16, nm + "res2")
        dact = _mm(df, w["ffn_w_down"][l], "nt", F32, nm + "ffn_down_dx")
        grads["ffn_w_down"][l] = _mm(s["act"], df, "tn", F32, nm + "ffn_down_dw")
        dua, dub, dwa, dwb, dca, dcb = _ffn_act_bwd(s["up"], w["ffn_conv_w"][l], s["cb"], dact, nm + "ffn_act")
        dup = jnp.concatenate([dua, dub], axis=1)
        grads["ffn_conv_w"][l] = jnp.concatenate([dwa, dwb], axis=1)
        grads["ffn_conv_b"][l] = jnp.concatenate([dca, dcb], axis=1)[0]
        grads["ffn_w_up"][l] = _mm(s["h2"], dup, "tn", F32, nm + "ffn_up_dw")
        dh2 = _mm(dup, w["ffn_w_up"][l], "nt", BF16, nm + "ffn_up_dx")
        dx1, g3 = _norm_bwd(s["x1"], n3, dh2, dx, F32, nm + "norm3")
        dmix, g2 = _norm_bwd(s["mix"], n2, dx1, None, BF16, nm + "res1")
        grads["w_out"][l] = _mm(s["ycat"], dmix, "tn", F32, nm + "out_proj_dw")
        dycat = _mm(dmix, w["w_out"][l], "nt", F32, nm + "out_proj_dx")
        dqkv, dz, dbd, gal, gdt, gnw = _dn_bwd(s["qkv"], s["p"], s["alog"], s["dtb"], s["dnw"], s["st"], dycat, nm + "dn")
        dpq, gconv = _dn_pre_bwd(s["p"], w["dn_conv_w"][l], dqkv, nm + "dn_pre")
        dpool, gpw, gpsc = _pool_bwd(s["p"], w["pool_w"][l], s["psc"], dycat, nm + "pool")
        dsq, dsk, dsv, gsk = _swa_bwd(s["p"], cosf, sins, w["swa_sinks"][l], s["y_swa"], s["lse"], dycat, nm + "swa")
        dp = jnp.concatenate([dpq, dz, dbd, dpool, dsq, dsk, dsv, jnp.zeros((t, 128), F32)], axis=1).astype(BF16)
        grads["w_in"][l] = _unpad_in(_mm(s["h"], dp, "tn", F32, nm + "in_proj_dw"))
        dh = _mm(dp, w["w_in"][l], "nt", BF16, nm + "in_proj_dx")
        dx, g1 = _norm_bwd(s["x"], n1, dh, dx1, F32, nm + "norm1")
        grads["norm_mix_pre"][l], grads["norm_mix_post"][l] = g1[0], g2[0]
        grads["norm_ffn_pre"][l], grads["norm_ffn_post"][l] = g3[0], g4[0]
        grads["dn_conv_w"][l] = gconv
        grads["dn_a_log"][l], grads["dn_dt_bias"][l], grads["dn_norm_w"][l] = gal[0, :DN_HEADS], gdt[0, :DN_HEADS], gnw[0]
        grads["pool_w"][l], grads["pool_scale"][l] = gpw, gpsc[0]
        grads["swa_sinks"][l] = gsk[:, 0, :SWA_GROUP].reshape(SWA_HEADS)
    return loss, dx, grads


def _flat2(a):
    return a.reshape(math.prod(a.shape[:-1]), a.shape[-1])


def _ew_rows(rows, cols, n_arrays):
    for cand in (512, 256, 128, 64, 32, 16):
        if rows % cand == 0 and cand * cols * 4 * n_arrays <= (8 << 20):
            return cand
    return rows


def _cast(a, dtype, name):
    a2 = _flat2(a)
    rows, cols = a2.shape
    r = _ew_rows(rows, cols, 2)

    def body(a_ref, o_ref):
        o_ref[...] = a_ref[...].astype(o_ref.dtype)

    spec = pl.BlockSpec((r, cols), lambda i: (i, 0))
    return pl.pallas_call(body, name=name, grid=(rows // r,), in_specs=[spec], out_specs=spec,
                          out_shape=jax.ShapeDtypeStruct((rows, cols), dtype), compiler_params=_cp(("parallel",)),
                          )(a2).reshape(a.shape)


def _add2(a, b, dtype, name):
    a2, b2 = _flat2(a), _flat2(b)
    rows, cols = a2.shape
    r = _ew_rows(rows, cols, 3)

    def body(a_ref, b_ref, o_ref):
        o_ref[...] = (a_ref[...].astype(F32) + b_ref[...].astype(F32)).astype(o_ref.dtype)

    spec = pl.BlockSpec((r, cols), lambda i: (i, 0))
    return pl.pallas_call(body, name=name, grid=(rows // r,), in_specs=[spec, spec], out_specs=spec,
                          out_shape=jax.ShapeDtypeStruct((rows, cols), dtype), compiler_params=_cp(("parallel",)),
                          )(a2, b2).reshape(a.shape)


def _sum_slots(a, name):
    s = a.shape[0]
    a3 = a.reshape(s, math.prod(a.shape[1:-1]), a.shape[-1])
    _, rows, cols = a3.shape
    r = _ew_rows(rows, cols, s + 1)

    def body(a_ref, o_ref):
        acc = a_ref[0].astype(F32)
        for k in range(1, s):
            acc = acc + a_ref[k].astype(F32)
        o_ref[...] = acc

    return pl.pallas_call(body, name=name, grid=(rows // r,),
                          in_specs=[pl.BlockSpec((s, r, cols), lambda i: (0, i, 0))],
                          out_specs=pl.BlockSpec((r, cols), lambda i: (i, 0)),
                          out_shape=jax.ShapeDtypeStruct((rows, cols), F32), compiler_params=_cp(("parallel",)),
                          )(a3).reshape(a.shape[1:])


MESH = pl.DeviceIdType.MESH
ANY = pl.BlockSpec(memory_space=pl.ANY)


def _place():
    x, y, c = lax.axis_index("x"), lax.axis_index("y"), lax.axis_index("c")
    chips = [(1 - x, y), (x, 1 - y), (1 - x, 1 - y)]
    return x, y, c, chips


def _gather_chips(arrs, name):
    na = len(arrs)
    l2 = arrs[0].shape[0] // 2

    def body(*refs):
        ins, outs = refs[:na], refs[na:2 * na]
        send1, recv1, send2, recv2, local = refs[2 * na:]
        x, y, c, chips = _place()
        me = 2 * x + y
        half = pl.ds(c * l2, l2)
        other = pl.ds((1 - c) * l2, l2)
        locals_ = [pltpu.make_async_copy(ins[k], outs[k].at[me], local.at[k]) for k in range(na)]
        for cp in locals_:
            cp.start()

        def first(k, j):
            px, py = chips[j]
            return pltpu.make_async_remote_copy(ins[k].at[half], outs[k].at[me, half], send1.at[k, j], recv1.at[k, j],
                                                device_id=(px, py, c), device_id_type=MESH)

        def landed(k, j):
            px, py = chips[j]
            return outs[k].at[2 * px + py, half]

        def passed(k, j):
            return pltpu.make_async_remote_copy(landed(k, j), landed(k, j), send2.at[k, j], recv2.at[k, j],
                                                device_id=(x, y, 1 - c), device_id_type=MESH)

        def from_sibling(k, j):
            px, py = chips[j]
            dst = outs[k].at[2 * px + py, other]
            return pltpu.make_async_remote_copy(dst, dst, send2.at[k, j], recv2.at[k, j],
                                                device_id=(x, y, 1 - c), device_id_type=MESH)

        for k in range(na):
            for j in range(3):
                first(k, j).start()
        for k in range(na):
            for j in range(3):
                first(k, j).wait_recv()
                passed(k, j).start()
        for k in range(na):
            for j in range(3):
                from_sibling(k, j).wait_recv()
        for k in range(na):
            for j in range(3):
                first(k, j).wait_send()
                passed(k, j).wait_send()
        for cp in locals_:
            cp.wait()

    return pl.pallas_call(
        body, name=name, in_specs=[ANY] * na, out_specs=[ANY] * na,
        out_shape=[jax.ShapeDtypeStruct((4,) + a.shape, a.dtype) for a in arrs],
        scratch_shapes=[pltpu.SemaphoreType.DMA((na, 3))] * 4 + [pltpu.SemaphoreType.DMA((na,))],
    )(*arrs)


def _swap_sibling(arrs, name):
    na = len(arrs)

    def body(*refs):
        ins, outs, send, recv = refs[:na], refs[na:2 * na], refs[2 * na], refs[2 * na + 1]
        x, y, c, _ = _place()
        cps = [pltpu.make_async_remote_copy(ins[k], outs[k], send.at[k], recv.at[k], device_id=(x, y, 1 - c),
                                            device_id_type=MESH) for k in range(na)]
        for cp in cps:
            cp.start()
        for cp in cps:
            cp.wait()

    return pl.pallas_call(
        body, name=name, in_specs=[ANY] * na, out_specs=[ANY] * na,
        out_shape=[jax.ShapeDtypeStruct(a.shape, a.dtype) for a in arrs],
        scratch_shapes=[pltpu.SemaphoreType.DMA((na,))] * 2,
    )(*arrs)


def _scatter_chips(arrs, name):
    na = len(arrs)

    def body(*refs):
        ins, outs, send, recv, local = refs[:na], refs[na:2 * na], refs[2 * na], refs[2 * na + 1], refs[2 * na + 2]
        x, y, c, chips = _place()
        me = 2 * x + y
        locals_ = [pltpu.make_async_copy(ins[k].at[me], outs[k].at[me], local.at[k]) for k in range(na)]
        for cp in locals_:
            cp.start()
        cps = []
        for k in range(na):
            for j, (px, py) in enumerate(chips):
                cps.append(pltpu.make_async_remote_copy(ins[k].at[2 * px + py], outs[k].at[me], send.at[k, j], recv.at[k, j],
                                                        device_id=(px, py, c), device_id_type=MESH))
        for cp in cps:
            cp.start()
        for cp in cps:
            cp.wait()
        for cp in locals_:
            cp.wait()

    return pl.pallas_call(
        body, name=name, in_specs=[ANY] * na, out_specs=[ANY] * na,
        out_shape=[jax.ShapeDtypeStruct(a.shape, a.dtype) for a in arrs],
        scratch_shapes=[pltpu.SemaphoreType.DMA((na, 3))] * 2 + [pltpu.SemaphoreType.DMA((na,))],
    )(*arrs)


def _join_halves(arrs, name):
    na = len(arrs)
    l2 = arrs[0].shape[0]

    def body(*refs):
        ins, outs, send, recv, local = refs[:na], refs[na:2 * na], refs[2 * na], refs[2 * na + 1], refs[2 * na + 2]
        x, y, c, _ = _place()
        half = pl.ds(c * l2, l2)
        locals_ = [pltpu.make_async_copy(ins[k], outs[k].at[half], local.at[k]) for k in range(na)]
        cps = [pltpu.make_async_remote_copy(ins[k], outs[k].at[half], send.at[k], recv.at[k], device_id=(x, y, 1 - c),
                                            device_id_type=MESH) for k in range(na)]
        for cp in locals_ + cps:
            cp.start()
        for cp in cps:
            cp.wait()
        for cp in locals_:
            cp.wait()

    return pl.pallas_call(
        body, name=name, in_specs=[ANY] * na, out_specs=[ANY] * na,
        out_shape=[jax.ShapeDtypeStruct((2 * l2,) + a.shape[1:], a.dtype) for a in arrs],
        scratch_shapes=[pltpu.SemaphoreType.DMA((na,))] * 3,
    )(*arrs)


def _gather_all(a, name):
    def body(a_ref, o_ref, send, recv, local):
        x, y, c, _ = _place()
        me = 4 * x + 2 * y + c
        mine = pltpu.make_async_copy(a_ref, o_ref.at[me], local)
        mine.start()
        cps = []
        for j in range(1, 8):
            peer = (x ^ (j >> 2), y ^ ((j >> 1) & 1), c ^ (j & 1))
            cps.append(pltpu.make_async_remote_copy(a_ref, o_ref.at[me], send.at[j - 1], recv.at[j - 1],
                                                    device_id=peer, device_id_type=MESH))
        for cp in cps:
            cp.start()
        for cp in cps:
            cp.wait()
        mine.wait()

    return pl.pallas_call(
        body, name=name, in_specs=[ANY], out_specs=ANY,
        out_shape=jax.ShapeDtypeStruct((8,) + a.shape, a.dtype),
        scratch_shapes=[pltpu.SemaphoreType.DMA((7,)), pltpu.SemaphoreType.DMA((7,)), pltpu.SemaphoreType.DMA],
    )(a)


def _to_full(name, g):
    ax = SHARD_AXIS[name]
    g = jnp.moveaxis(g, 0, ax)
    return g.reshape(g.shape[:ax] + (4 * g.shape[ax + 1],) + g.shape[ax + 2:])


def _to_pieces(name, g):
    ax = SHARD_AXIS[name]
    g = g.reshape(g.shape[:ax] + (4, g.shape[ax] // 4) + g.shape[ax + 1:])
    return jnp.moveaxis(g, ax, 0)


def _pack(parts):
    flat = jnp.concatenate([p.reshape(-1) for p in parts])
    n = flat.shape[0]
    rows = -(-n // 1024) * 8
    return jnp.pad(flat, (0, rows * 128 - n)).reshape(rows, 128)


def _unpack(buf, like):
    flat, out, off = buf.reshape(-1), [], 0
    for p in like:
        out.append(flat[off:off + p.size].reshape(p.shape))
        off += p.size
    return out


def kernel(x, positions, norm_mix_pre, w_in, dn_conv_w, dn_a_log, dn_dt_bias, dn_norm_w, pool_w, pool_scale, swa_sinks, w_out, norm_mix_post, norm_ffn_pre, ffn_w_up, ffn_conv_w, ffn_conv_b, ffn_w_down, norm_ffn_post, loss_target, m_norm_mix_pre, m_w_in, m_dn_conv_w, m_dn_a_log, m_dn_dt_bias, m_dn_norm_w, m_pool_w, m_pool_scale, m_swa_sinks, m_w_out, m_norm_mix_post, m_norm_ffn_pre, m_ffn_w_up, m_ffn_conv_w, m_ffn_conv_b, m_ffn_w_down, m_norm_ffn_post, v_norm_mix_pre, v_w_in, v_dn_conv_w, v_dn_a_log, v_dn_dt_bias, v_dn_norm_w, v_pool_w, v_pool_scale, v_swa_sinks, v_w_out, v_norm_mix_post, v_norm_ffn_pre, v_ffn_w_up, v_ffn_conv_w, v_ffn_conv_b, v_ffn_w_down, v_norm_ffn_post):
    wts = dict(zip(WEIGHTS, (norm_mix_pre, w_in, dn_conv_w, dn_a_log, dn_dt_bias, dn_norm_w, pool_w, pool_scale, swa_sinks,
                             w_out, norm_mix_post, norm_ffn_pre, ffn_w_up, ffn_conv_w, ffn_conv_b, ffn_w_down, norm_ffn_post)))
    mom = dict(zip(WEIGHTS, (m_norm_mix_pre, m_w_in, m_dn_conv_w, m_dn_a_log, m_dn_dt_bias, m_dn_norm_w, m_pool_w, m_pool_scale,
                             m_swa_sinks, m_w_out, m_norm_mix_post, m_norm_ffn_pre, m_ffn_w_up, m_ffn_conv_w, m_ffn_conv_b,
                             m_ffn_w_down, m_norm_ffn_post)))
    var = dict(zip(WEIGHTS, (v_norm_mix_pre, v_w_in, v_dn_conv_w, v_dn_a_log, v_dn_dt_bias, v_dn_norm_w, v_pool_w, v_pool_scale,
                             v_swa_sinks, v_w_out, v_norm_mix_post, v_norm_ffn_pre, v_ffn_w_up, v_ffn_conv_w, v_ffn_conv_b,
                             v_ffn_w_down, v_norm_ffn_post)))
    depth = w_out.shape[0]
    l2 = depth // 2
    c = lax.axis_index("c")
    mm_names = ("w_in", "w_out", "ffn_w_up", "ffn_w_down")

    shards = [(_cast(wts[k], BF16, "cast_" + k) if k in mm_names else wts[k]) for k in BIG]
    full = {k: _to_full(k, g) for k, g in zip(BIG, _gather_chips(shards, "gather_weights"))}
    w = dict(wts)
    w.update(full)
    w["w_in"] = _pad_in(full["w_in"])

    loss, dx, grads = _local_step(x[0], positions[0], loss_target[0], w)
    loss = lax.psum(loss[0, 0], ("x", "y", "c"))

    stacked = {k: jnp.stack(grads[k]) for k in BIG}
    mine = [_to_pieces(k, lax.dynamic_slice_in_dim(stacked[k], c * l2, l2, 0)).astype(BF16) for k in BIG]
    sibs = [_to_pieces(k, lax.dynamic_slice_in_dim(stacked[k], (1 - c) * l2, l2, 0)).astype(BF16) for k in BIG]
    got = _swap_sibling(sibs, "grads_to_sibling")
    chip_sum = [_add2(a, b, BF16, "chip_sum_" + k) for k, a, b in zip(BIG, mine, got)]
    slabs = _scatter_chips(chip_sum, "grads_to_owner")
    halves = [_sum_slots(s, "owner_sum_" + k) for k, s in zip(BIG, slabs)]
    g_big = dict(zip(BIG, _join_halves(halves, "grads_join")))

    small_like = [wts[k] for k in SMALL]
    g_small_buf = _sum_slots(_gather_all(_pack([jnp.stack(grads[k]) for k in SMALL]), "gather_small"), "sum_small")
    d_s, m_s, v_s = _adamw(_pack(small_like), g_small_buf, _pack([mom[k] for k in SMALL]), _pack([var[k] for k in SMALL]), "adam_small")
    g_small = dict(zip(SMALL, _unpack(g_small_buf, small_like)))
    upd_small = {n: dict(zip(SMALL, _unpack(b, small_like))) for n, b in (("d", d_s), ("m", m_s), ("v", v_s))}

    g_all, d_all, m_all, v_all = {}, {}, {}, {}
    for k in WEIGHTS:
        if k in BIG:
            g_all[k] = g_big[k]
            d_all[k], m_all[k], v_all[k] = _adamw(wts[k], g_big[k], mom[k], var[k], "adam_" + k)
        else:
            g_all[k], d_all[k], m_all[k], v_all[k] = g_small[k], upd_small["d"][k], upd_small["m"][k], upd_small["v"][k]
    return (loss, dx[None], *[g_all[k] for k in WEIGHTS], *[d_all[k] for k in WEIGHTS],
            *[m_all[k] for k in WEIGHTS], *[v_all[k] for k in WEIGHTS])
```

```python
import functools
import math

import jax
import jax.numpy as jnp
from jax import lax
from jax.experimental import pallas as pl
from jax.experimental.pallas import tpu as pltpu

F32 = jnp.float32
BF16 = jnp.bfloat16

HEAD_DIM = 128
DN_HEADS = 6
DN_CONV = 4
DN_CHUNK = 64
POOL_GROUPS = 4
SWA_HEADS = 6
SWA_KV_HEADS = 2
SWA_GROUP = SWA_HEADS // SWA_KV_HEADS
SWA_BLOCK = 128
ROPE_THETA = 10000.0
FFN_CONV = 3
NORM_EPS = 1e-6
DN_W = DN_HEADS * HEAD_DIM
POOL_W = POOL_GROUPS * HEAD_DIM
SWA_W = SWA_HEADS * HEAD_DIM
SWA_KV_W = SWA_KV_HEADS * HEAD_DIM
MIX_W = DN_W + POOL_W + SWA_W
IN_TRUE = 3 * DN_W + DN_W + 2 * DN_HEADS + POOL_W + SWA_W + 2 * SWA_KV_W
GATE_END = 4 * DN_W + 2 * DN_HEADS
CB_Z = 18
CB_BD = 24
CB_POOL = 25
CB_SQ = 29
CB_SK = 35
CB_SV = 37
IN_PAD = 40 * 128
ADAM_LR, ADAM_B1, ADAM_B2, ADAM_EPS, ADAM_WD, ADAM_STEP = 0.001, 0.9, 0.999, 1e-08, 0.01, 10

VMEM_LIMIT = 48 * 1024 * 1024
MM_TK_MAX = 2816
HIGH = lax.Precision.HIGHEST


def _cp(sem):
    return pltpu.CompilerParams(dimension_semantics=sem, vmem_limit_bytes=VMEM_LIMIT)


def _tile(n, prefs):
    for p in prefs:
        if n % p == 0:
            return p
    return n


def _rows(t):
    return _tile(t, (256, 128))


_DN = {"nn": (((1,), (0,)), ((), ())), "nt": (((1,), (1,)), ((), ())), "tn": (((0,), (0,)), ((), ()))}


def _mm(a, b, mode, out_dtype, name):
    if mode == "tn":
        k, m = a.shape
    else:
        m, k = a.shape
    n = b.shape[0] if mode == "nt" else b.shape[1]
    tm = _tile(m, (1024, 512, 256, 128))
    tn = _tile(n, (1024, 512, 256, 128))
    tk = max([d for d in range(128, min(k, MM_TK_MAX) + 1, 128) if k % d == 0], default=k)
    nk = k // tk

    def body(a_ref, b_ref, o_ref, *scratch):
        part = lax.dot_general(a_ref[...], b_ref[...], _DN[mode], preferred_element_type=F32)
        if nk == 1:
            o_ref[...] = part.astype(o_ref.dtype)
            return
        acc_ref, = scratch
        kk = pl.program_id(2)

        @pl.when(kk == 0)
        def _():
            acc_ref[...] = part

        @pl.when(kk > 0)
        def _():
            acc_ref[...] += part

        @pl.when(kk == nk - 1)
        def _():
            o_ref[...] = acc_ref[...].astype(o_ref.dtype)

    a_spec = pl.BlockSpec((tk, tm), lambda i, j, kk: (kk, i)) if mode == "tn" else pl.BlockSpec((tm, tk), lambda i, j, kk: (i, kk))
    b_spec = pl.BlockSpec((tn, tk), lambda i, j, kk: (j, kk)) if mode == "nt" else pl.BlockSpec((tk, tn), lambda i, j, kk: (kk, j))
    return pl.pallas_call(
        body, name=name, grid=(m // tm, n // tn, nk),
        in_specs=[a_spec, b_spec], out_specs=pl.BlockSpec((tm, tn), lambda i, j, kk: (i, j)),
        out_shape=jax.ShapeDtypeStruct((m, n), out_dtype),
        scratch_shapes=[pltpu.VMEM((tm, tn), F32)] if nk > 1 else [],
        compiler_params=_cp(("parallel", "parallel", "arbitrary")),
    )(a, b)


def _rms(x, w):
    return x * lax.rsqrt(jnp.mean(x * x, axis=-1, keepdims=True) + NORM_EPS) * w


def _norm_fwd(x, w, name):
    t, d = x.shape
    r = _rows(t)

    def body(x_ref, w_ref, h_ref):
        h_ref[...] = _rms(x_ref[...], w_ref[...]).astype(h_ref.dtype)

    return pl.pallas_call(
        body, name=name, grid=(t // r,),
        in_specs=[pl.BlockSpec((r, d), lambda i: (i, 0)), pl.BlockSpec((1, d), lambda i: (0, 0))],
        out_specs=pl.BlockSpec((r, d), lambda i: (i, 0)),
        out_shape=jax.ShapeDtypeStruct((t, d), BF16), compiler_params=_cp(("parallel",)),
    )(x, w)


def _resnorm_fwd(x, y, w, name):
    t, d = x.shape
    r = _rows(t)

    def body(x_ref, y_ref, w_ref, o_ref):
        o_ref[...] = x_ref[...] + _rms(y_ref[...], w_ref[...])

    return pl.pallas_call(
        body, name=name, grid=(t // r,),
        in_specs=[pl.BlockSpec((r, d), lambda i: (i, 0)), pl.BlockSpec((r, d), lambda i: (i, 0)),
                  pl.BlockSpec((1, d), lambda i: (0, 0))],
        out_specs=pl.BlockSpec((r, d), lambda i: (i, 0)),
        out_shape=jax.ShapeDtypeStruct((t, d), F32), compiler_params=_cp(("parallel",)),
    )(x, y, w)


def _norm_bwd(x, w, dh, add, out_dtype, name):
    t, d = x.shape
    r = _rows(t)
    has_add = add is not None

    def body(*refs):
        if has_add:
            x_ref, w_ref, dh_ref, add_ref, dx_ref, dw_ref = refs
        else:
            x_ref, w_ref, dh_ref, dx_ref, dw_ref = refs
        xv = x_ref[...]
        g = dh_ref[...].astype(F32)
        rs = lax.rsqrt(jnp.mean(xv * xv, axis=-1, keepdims=True) + NORM_EPS)
        xh = xv * rs
        gw = g * w_ref[...]
        dx = rs * (gw - xh * jnp.mean(gw * xh, axis=-1, keepdims=True))
        if has_add:
            dx = dx + add_ref[...]
        dx_ref[...] = dx.astype(dx_ref.dtype)

        @pl.when(pl.program_id(0) == 0)
        def _():
            dw_ref[...] = jnp.zeros_like(dw_ref)

        dw_ref[...] += jnp.sum(g * xh, axis=0, keepdims=True)

    row = pl.BlockSpec((r, d), lambda i: (i, 0))
    vec = pl.BlockSpec((1, d), lambda i: (0, 0))
    ins = [x, w, dh] + ([add] if has_add else [])
    return pl.pallas_call(
        body, name=name, grid=(t // r,),
        in_specs=[row, vec, row] + ([row] if has_add else []),
        out_specs=[row, vec],
        out_shape=[jax.ShapeDtypeStruct((t, d), out_dtype), jax.ShapeDtypeStruct((1, d), F32)],
        compiler_params=_cp(("arbitrary",)),
    )(*ins)


def _loss_head(y, target, name):
    t, d = y.shape
    r = _rows(t)

    def body(y_ref, t_ref, l_ref, g_ref):
        e = y_ref[...] - t_ref[...]
        g_ref[...] = e * (1.0 / d)

        @pl.when(pl.program_id(0) == 0)
        def _():
            l_ref[...] = jnp.zeros_like(l_ref)

        l_ref[...] += jnp.sum(e * e) * (0.5 / d)

    row = pl.BlockSpec((r, d), lambda i: (i, 0))
    return pl.pallas_call(
        body, name=name, grid=(t // r,), in_specs=[row, row],
        out_specs=[pl.BlockSpec((1, 128), lambda i: (0, 0)), row],
        out_shape=[jax.ShapeDtypeStruct((1, 128), F32), jax.ShapeDtypeStruct((t, d), F32)],
        compiler_params=_cp(("arbitrary",)),
    )(y, target)


def _down(x, s):
    return x if s == 0 else pltpu.roll(x, s, 0)


def _up(x, s):
    return x if s == 0 else pltpu.roll(x, x.shape[0] - s, 0)


def _halo(t, r, hh, tc, col):
    q = r // hh
    last = t // hh - 1
    tile = pl.BlockSpec((r, tc), lambda j, i: (i, col(j)))
    prev = pl.BlockSpec((hh, tc), lambda j, i: (jnp.maximum(i * q - 1, 0), col(j)))
    nxt = pl.BlockSpec((hh, tc), lambda j, i: (jnp.minimum((i + 1) * q, last), col(j)))
    return tile, prev, nxt


def _sig(x):
    return 1.0 / (1.0 + jnp.exp(-x))


def _dsilu(x, s):
    return s * (1.0 + x * (1.0 - s))


def _dn_pre_fwd(p, conv_w, name):
    t = p.shape[0]
    r = _rows(t)

    def body(x_ref, xp_ref, w_ref, o_ref):
        j, i = pl.program_id(0), pl.program_id(1)
        xe = jnp.concatenate([jnp.where(i == 0, 0.0, xp_ref[...]), x_ref[...]], axis=0)
        c = sum(_down(xe, DN_CONV - 1 - k) * w_ref[pl.ds(k, 1), :] for k in range(DN_CONV))[8:]
        a = c * _sig(c)
        for h in range(DN_HEADS):
            ah = a[:, h * 128:(h + 1) * 128]
            fac = lax.rsqrt(jnp.sum(ah * ah, axis=-1, keepdims=True) + NORM_EPS)
            o_ref[:, h * 128:(h + 1) * 128] = ah * jnp.where(j == 0, fac * HEAD_DIM ** -0.5, jnp.where(j == 1, fac, 1.0))

    tile, prev, _ = _halo(t, r, 8, DN_W, lambda j: j)
    return pl.pallas_call(
        body, name=name, grid=(3, t // r),
        in_specs=[tile, prev, pl.BlockSpec((DN_CONV, DN_W), lambda j, i: (0, j))],
        out_specs=tile, out_shape=jax.ShapeDtypeStruct((t, 3 * DN_W), F32),
        compiler_params=_cp(("parallel", "parallel")),
    )(p, p, conv_w)


def _dn_pre_bwd(p, conv_w, dqkv, name):
    t = p.shape[0]
    r = _rows(t)
    ni = t // r

    def body(x_ref, xp_ref, xn_ref, w_ref, d_ref, dn_ref, dx_ref, dw_ref):
        j, i = pl.program_id(0), pl.program_id(1)
        xe = jnp.concatenate([jnp.where(i == 0, 0.0, xp_ref[...]), x_ref[...], xn_ref[...]], axis=0)
        de = jnp.concatenate([jnp.zeros((8, DN_W), F32), d_ref[...], jnp.where(i == ni - 1, 0.0, dn_ref[...])], axis=0)
        xs = [_down(xe, DN_CONV - 1 - k) for k in range(DN_CONV)]
        c = sum(xs[k] * w_ref[pl.ds(k, 1), :] for k in range(DN_CONV))
        s = _sig(c)
        a = c * s
        das = []
        for h in range(DN_HEADS):
            ah, dh = a[:, h * 128:(h + 1) * 128], de[:, h * 128:(h + 1) * 128]
            fac = lax.rsqrt(jnp.sum(ah * ah, axis=-1, keepdims=True) + NORM_EPS)
            dnorm = fac * dh - ah * (fac * fac * fac) * jnp.sum(dh * ah, axis=-1, keepdims=True)
            das.append(jnp.where(j == 0, dnorm * HEAD_DIM ** -0.5, jnp.where(j == 1, dnorm, dh)))
        dc = jnp.concatenate(das, axis=1) * _dsilu(c, s)
        dx_ref[...] = sum(_up(dc, DN_CONV - 1 - k) * w_ref[pl.ds(k, 1), :] for k in range(DN_CONV))[8:8 + r]

        @pl.when(i == 0)
        def _():
            dw_ref[...] = jnp.zeros_like(dw_ref)

        for k in range(DN_CONV):
            dw_ref[pl.ds(k, 1), :] += jnp.sum((dc * xs[k])[8:8 + r], axis=0, keepdims=True)

    tile, prev, nxt = _halo(t, r, 8, DN_W, lambda j: j)
    wspec = pl.BlockSpec((DN_CONV, DN_W), lambda j, i: (0, j))
    return pl.pallas_call(
        body, name=name, grid=(3, ni),
        in_specs=[tile, prev, nxt, wspec, tile, nxt],
        out_specs=[tile, wspec],
        out_shape=[jax.ShapeDtypeStruct((t, 3 * DN_W), F32), jax.ShapeDtypeStruct((DN_CONV, 3 * DN_W), F32)],
        compiler_params=_cp(("parallel", "arbitrary")),
    )(p, p, p, conv_w, dqkv, dqkv)


def _ffn_act_fwd(up, cw, cb, name):
    t, f2 = up.shape
    f = f2 // 2
    r = _rows(t)
    tc = _tile(f, (512, 256, 128))
    nj = f // tc

    def body(a_ref, ap_ref, b_ref, bp_ref, wa_ref, wb_ref, ca_ref, cb_ref, o_ref):
        i = pl.program_id(1)

        def conv(x_ref, xp_ref, w_ref, c_ref):
            xe = jnp.concatenate([jnp.where(i == 0, 0.0, xp_ref[...]), x_ref[...]], axis=0)
            return sum(_down(xe, FFN_CONV - 1 - k) * w_ref[pl.ds(k, 1), :] for k in range(FFN_CONV))[8:] + c_ref[...]

        ua = conv(a_ref, ap_ref, wa_ref, ca_ref)
        ub = conv(b_ref, bp_ref, wb_ref, cb_ref)
        o_ref[...] = (ua * _sig(ua) * ub).astype(o_ref.dtype)

    ta, pa, _ = _halo(t, r, 8, tc, lambda j: j)
    tb, pb, _ = _halo(t, r, 8, tc, lambda j: j + nj)
    wa = pl.BlockSpec((FFN_CONV, tc), lambda j, i: (0, j))
    wb = pl.BlockSpec((FFN_CONV, tc), lambda j, i: (0, j + nj))
    ca = pl.BlockSpec((1, tc), lambda j, i: (0, j))
    cbs = pl.BlockSpec((1, tc), lambda j, i: (0, j + nj))
    return pl.pallas_call(
        body, name=name, grid=(nj, t // r),
        in_specs=[ta, pa, tb, pb, wa, wb, ca, cbs], out_specs=ta,
        out_shape=jax.ShapeDtypeStruct((t, f), BF16), compiler_params=_cp(("parallel", "parallel")),
    )(up, up, up, up, cw, cw, cb, cb)


def _ffn_act_bwd(up, cw, cb, dact, name):
    t, f2 = up.shape
    f = f2 // 2
    r = _rows(t)
    ni = t // r
    tc = _tile(f, (512, 256, 128))
    nj = f // tc

    def body(a_ref, ap_ref, an_ref, b_ref, bp_ref, bn_ref, wa_ref, wb_ref, ca_ref, cb_ref, d_ref, dn_ref,
             dua_ref, dub_ref, dwa_ref, dwb_ref, dca_ref, dcb_ref):
        i = pl.program_id(1)

        def ext(x_ref, xp_ref, xn_ref):
            return jnp.concatenate([jnp.where(i == 0, 0.0, xp_ref[...]), x_ref[...], xn_ref[...]], axis=0)

        ae, be = ext(a_ref, ap_ref, an_ref), ext(b_ref, bp_ref, bn_ref)
        as_ = [_down(ae, FFN_CONV - 1 - k) for k in range(FFN_CONV)]
        bs_ = [_down(be, FFN_CONV - 1 - k) for k in range(FFN_CONV)]
        ua = sum(as_[k] * wa_ref[pl.ds(k, 1), :] for k in range(FFN_CONV)) + ca_ref[...]
        ub = sum(bs_[k] * wb_ref[pl.ds(k, 1), :] for k in range(FFN_CONV)) + cb_ref[...]
        de = jnp.concatenate([jnp.zeros((8, tc), F32), d_ref[...].astype(F32),
                              jnp.where(i == ni - 1, 0.0, dn_ref[...].astype(F32))], axis=0)
        s = _sig(ua)
        dua = de * ub * _dsilu(ua, s)
        dub = de * ua * s
        dua_ref[...] = sum(_up(dua, FFN_CONV - 1 - k) * wa_ref[pl.ds(k, 1), :] for k in range(FFN_CONV))[8:8 + r].astype(dua_ref.dtype)
        dub_ref[...] = sum(_up(dub, FFN_CONV - 1 - k) * wb_ref[pl.ds(k, 1), :] for k in range(FFN_CONV))[8:8 + r].astype(dub_ref.dtype)

        @pl.when(i == 0)
        def _():
            dwa_ref[...] = jnp.zeros_like(dwa_ref)
            dwb_ref[...] = jnp.zeros_like(dwb_ref)
            dca_ref[...] = jnp.zeros_like(dca_ref)
            dcb_ref[...] = jnp.zeros_like(dcb_ref)

        for k in range(FFN_CONV):
            dwa_ref[pl.ds(k, 1), :] += jnp.sum((dua * as_[k])[8:8 + r], axis=0, keepdims=True)
            dwb_ref[pl.ds(k, 1), :] += jnp.sum((dub * bs_[k])[8:8 + r], axis=0, keepdims=True)
        dca_ref[...] += jnp.sum(dua[8:8 + r], axis=0, keepdims=True)
        dcb_ref[...] += jnp.sum(dub[8:8 + r], axis=0, keepdims=True)

    ta, pa, na = _halo(t, r, 8, tc, lambda j: j)
    tb, pb, nb = _halo(t, r, 8, tc, lambda j: j + nj)
    wa = pl.BlockSpec((FFN_CONV, tc), lambda j, i: (0, j))
    wb = pl.BlockSpec((FFN_CONV, tc), lambda j, i: (0, j + nj))
    ca = pl.BlockSpec((1, tc), lambda j, i: (0, j))
    cbs = pl.BlockSpec((1, tc), lambda j, i: (0, j + nj))
    return pl.pallas_call(
        body, name=name, grid=(nj, ni),
        in_specs=[ta, pa, na, tb, pb, nb, wa, wb, ca, cbs, ta, na],
        out_specs=[ta, ta, wa, wa, ca, ca],
        out_shape=[jax.ShapeDtypeStruct((t, f), BF16), jax.ShapeDtypeStruct((t, f), BF16),
                   jax.ShapeDtypeStruct((FFN_CONV, f), F32), jax.ShapeDtypeStruct((FFN_CONV, f), F32),
                   jax.ShapeDtypeStruct((1, f), F32), jax.ShapeDtypeStruct((1, f), F32)],
        compiler_params=_cp(("parallel", "arbitrary")),
    )(up, up, up, up, up, up, cw, cw, cb, cb, dact, dact)


def _pool_pick(g, vals):
    return jnp.where(g == 0, vals[0], jnp.where(g == 1, vals[1], jnp.where(g == 2, vals[2], vals[3])))


def _pool_pre(xe, g, t0):
    s1 = xe + _down(xe, 1)
    s2 = s1 + _down(s1, 2)
    s3 = s2 + _down(s2, 4)
    s4 = s3 + _down(s3, 8)
    r = xe.shape[0] - 16
    pos = (t0 + lax.broadcasted_iota(jnp.int32, (r, 1), 0)).astype(F32)
    cnt = jnp.minimum(pos + 1.0, _pool_pick(g, (2.0, 4.0, 8.0, 16.0)))
    return _pool_pick(g, (s1, s2, s3, s4))[16:] / cnt - xe[16:]


def _pool_fwd(p, pool_w, scale, name):
    t = p.shape[0]
    r = _tile(t, (1024, 256, 128))

    def body(x_ref, xp_ref, w_ref, sc_ref, o_ref):
        g, i = pl.program_id(0), pl.program_id(1)
        xe = jnp.concatenate([jnp.where(i == 0, 0.0, xp_ref[...]), x_ref[...]], axis=0)
        pre = _pool_pre(xe, g, i * r)
        o_ref[...] = jnp.dot(pre, w_ref[0], preferred_element_type=F32) * sc_ref[...]

    tile, prev, _ = _halo(t, r, 16, 128, lambda j: CB_POOL + j)
    return pl.pallas_call(
        body, name=name, grid=(POOL_GROUPS, t // r),
        in_specs=[tile, prev, pl.BlockSpec((1, 128, 128), lambda j, i: (j, 0, 0)), pl.BlockSpec((1, 128), lambda j, i: (0, j))],
        out_specs=pl.BlockSpec((r, 128), lambda j, i: (i, j)),
        out_shape=jax.ShapeDtypeStruct((t, POOL_W), F32), compiler_params=_cp(("parallel", "parallel")),
    )(p, p, pool_w, scale)


def _pool_bwd(p, pool_w, scale, dycat, name):
    t = p.shape[0]
    r = _tile(t, (1024, 256, 128))
    ni = t // r

    def body(x_ref, xp_ref, w_ref, sc_ref, d_ref, dn_ref, dx_ref, dw_ref, dsc_ref):
        g, i = pl.program_id(0), pl.program_id(1)
        xe = jnp.concatenate([jnp.where(i == 0, 0.0, xp_ref[...]), x_ref[...]], axis=0)
        pre = _pool_pre(xe, g, i * r)
        w = w_ref[0]
        dy = d_ref[...]
        dye = jnp.concatenate([dy, jnp.where(i == ni - 1, 0.0, dn_ref[...])], axis=0)
        dpre = lax.dot_general(dye * sc_ref[...], w, _DN["nt"], preferred_element_type=F32)
        pos = (i * r + lax.broadcasted_iota(jnp.int32, (r + 16, 1), 0)).astype(F32)
        dm = dpre / jnp.minimum(pos + 1.0, _pool_pick(g, (2.0, 4.0, 8.0, 16.0)))
        a1 = dm + _up(dm, 1)
        a2 = a1 + _up(a1, 2)
        a3 = a2 + _up(a2, 4)
        a4 = a3 + _up(a3, 8)
        dx_ref[...] = (_pool_pick(g, (a1, a2, a3, a4)) - dpre)[:r]

        @pl.when(i == 0)
        def _():
            dw_ref[...] = jnp.zeros_like(dw_ref)
            dsc_ref[...] = jnp.zeros_like(dsc_ref)

        dw_ref[0] += lax.dot_general(pre, dy * sc_ref[...], _DN["tn"], preferred_element_type=F32)
        dsc_ref[...] += jnp.sum(dy * jnp.dot(pre, w, preferred_element_type=F32), axis=0, keepdims=True)

    tile, prev, _ = _halo(t, r, 16, 128, lambda j: CB_POOL + j)
    dtile, _, dnxt = _halo(t, r, 16, 128, lambda j: DN_W // 128 + j)
    wspec = pl.BlockSpec((1, 128, 128), lambda j, i: (j, 0, 0))
    sspec = pl.BlockSpec((1, 128), lambda j, i: (0, j))
    return pl.pallas_call(
        body, name=name, grid=(POOL_GROUPS, ni),
        in_specs=[tile, prev, wspec, sspec, dtile, dnxt],
        out_specs=[pl.BlockSpec((r, 128), lambda j, i: (i, j)), wspec, sspec],
        out_shape=[jax.ShapeDtypeStruct((t, POOL_W), F32), jax.ShapeDtypeStruct((POOL_GROUPS, 128, 128), F32),
                   jax.ShapeDtypeStruct((1, POOL_W), F32)],
        compiler_params=_cp(("parallel", "arbitrary")),
    )(p, p, pool_w, scale, dycat, dycat)


_DNB = {"nn": (((2,), (1,)), ((0,), (0,))), "nt": (((2,), (2,)), ((0,), (0,))), "tn": (((1,), (1,)), ((0,), (0,)))}


def _dot(a, b, mode="nn", precision=None):
    dn = _DNB[mode] if a.ndim == 3 else _DN[mode]
    return lax.dot_general(a, b, dn, precision=precision, preferred_element_type=F32)


@functools.partial(jax.custom_vjp, nondiff_argnums=(2,))
def _bdot(a, b, mode):
    return _dot(a.astype(BF16), b.astype(BF16), mode)


def _bdot_fwd(a, b, mode):
    return _bdot(a, b, mode), (a, b)


def _bdot_bwd(mode, res, g):
    a, b = res
    if mode == "nn":
        return _bdot(g, b, "nt"), _bdot(a, g, "tn")
    if mode == "nt":
        return _bdot(g, b, "nn"), _bdot(g, a, "tn")
    return _bdot(b, g, "nt"), _bdot(a, g, "nn")


_bdot.defvjp(_bdot_fwd, _bdot_bwd)


def _dn_consts():
    c = DN_CHUNK
    ii = lax.broadcasted_iota(jnp.int32, (c, c), 0)
    jj = lax.broadcasted_iota(jnp.int32, (c, c), 1)
    one, zero = jnp.ones((c, c), F32), jnp.zeros((c, c), F32)
    return dict(ltri=jnp.where(ii >= jj, one, zero), utri=jnp.where(ii <= jj, one, zero), ones=one,
                causal=ii >= jj, strict=ii > jj, eye=jnp.where(ii == jj, one, zero))


def _dn_chunk(q, k, v, z, bcol, acol, s_in, alog, dtb, nw, cs):
    c = DN_CHUNK
    hh = q.shape[0]
    per_head = lambda m: jnp.broadcast_to(m, (hh, c, c))
    beta = _sig(bcol)
    xa = acol + dtb
    g = -jnp.exp(alog) * (jnp.maximum(xa, 0.0) + jnp.log(1.0 + jnp.exp(-jnp.abs(xa))))
    gb = jnp.broadcast_to(g, (hh, c, HEAD_DIM))
    gbc = jnp.broadcast_to(g, (hh, c, c))
    gc = _dot(per_head(cs["ltri"]), gb, precision=HIGH)
    gcol = _dot(per_head(cs["ltri"]), gbc, precision=HIGH)
    grow = _dot(per_head(cs["ones"]), gbc * cs["utri"], precision=HIGH)
    decay = jnp.exp(jnp.where(cs["causal"], gcol - grow, -1e30))
    kb = k * beta
    vb = v * beta
    nil = -jnp.where(cs["strict"], _bdot(kb, k, "nt") * decay, 0.0)
    inv = cs["eye"] + nil
    powk = nil
    for _ in range(int(math.log2(c)) - 1):
        powk = _dot(powk, powk)
        inv = _dot(inv, cs["eye"] + powk)
    eg = jnp.exp(gc)
    u = _bdot(inv, vb, "nn")
    w = _bdot(inv, kb * eg, "nn")
    a = _bdot(q, k, "nt") * decay
    v_new = u - _bdot(w, s_in, "nn")
    o = _bdot(q * eg, s_in, "nn") + _bdot(a, v_new, "nn")
    glast = jnp.sum(gb, axis=1, keepdims=True)
    s_out = s_in * jnp.exp(glast) + _bdot(k * jnp.exp(glast - gc), v_new, "tn")
    on = o * lax.rsqrt(jnp.mean(o * o, axis=-1, keepdims=True) + NORM_EPS) * nw
    return on * (z * _sig(z)), s_out


def _lane_pick(x, lane, idx):
    return jnp.sum(jnp.where(lane == idx, x, 0.0), axis=1, keepdims=True)


def _dn_load(q_ref, k_ref, v_ref, z_ref, bd_ref, al_ref, dt_ref, nw_ref, s_in):
    lane = lax.broadcasted_iota(jnp.int32, (1, 128), 1)
    bd, al, dt = bd_ref[...], al_ref[...], dt_ref[...]
    heads = range(DN_HEADS)
    wide = lambda ref: jnp.stack([ref[:, h * 128:(h + 1) * 128] for h in heads], axis=0)
    col = lambda x, off: jnp.stack([_lane_pick(x, lane, off + h) for h in heads], axis=0)
    return (wide(q_ref), wide(k_ref), wide(v_ref), wide(z_ref), col(bd, 0), col(bd, DN_HEADS), s_in,
            col(al, 0), col(dt, 0), nw_ref[...])


def _dn_fwd(qkv, p, alog, dtb, nw, name):
    t = qkv.shape[0]
    c = DN_CHUNK
    n = t // c

    def body(q_ref, k_ref, v_ref, z_ref, bd_ref, al_ref, dt_ref, nw_ref, y_ref, ss_ref, s_scr):
        @pl.when(pl.program_id(0) == 0)
        def _():
            s_scr[...] = jnp.zeros_like(s_scr)

        s_in = s_scr[...]
        y, s_out = _dn_chunk(*_dn_load(q_ref, k_ref, v_ref, z_ref, bd_ref, al_ref, dt_ref, nw_ref, s_in), _dn_consts())
        ss_ref[0] = s_in
        s_scr[...] = s_out
        for h in range(DN_HEADS):
            y_ref[:, h * 128:(h + 1) * 128] = y[h]

    wide = lambda j: pl.BlockSpec((c, DN_W), lambda i: (i, j))
    vec = pl.BlockSpec((1, 128), lambda i: (0, 0))
    return pl.pallas_call(
        body, name=name, grid=(n,),
        in_specs=[wide(0), wide(1), wide(2), wide(3), pl.BlockSpec((c, 128), lambda i: (i, CB_BD)), vec, vec, vec],
        out_specs=[wide(0), pl.BlockSpec((1, DN_HEADS, 128, 128), lambda i: (i, 0, 0, 0))],
        out_shape=[jax.ShapeDtypeStruct((t, DN_W), F32), jax.ShapeDtypeStruct((n, DN_HEADS, 128, 128), F32)],
        scratch_shapes=[pltpu.VMEM((DN_HEADS, 128, 128), F32)],
        compiler_params=_cp(("arbitrary",)),
    )(qkv, qkv, qkv, p, p, alog, dtb, nw)


def _dn_bwd(qkv, p, alog, dtb, nw, states, dycat, name):
    t = qkv.shape[0]
    c = DN_CHUNK
    n = t // c

    def body(q_ref, k_ref, v_ref, z_ref, bd_ref, al_ref, dt_ref, nw_ref, ss_ref, dy_ref,
             dqkv_ref, dz_ref, dbd_ref, dal_ref, ddt_ref, dnw_ref, ds_scr):
        @pl.when(pl.program_id(0) == 0)
        def _():
            ds_scr[...] = jnp.zeros_like(ds_scr)
            dal_ref[...] = jnp.zeros_like(dal_ref)
            ddt_ref[...] = jnp.zeros_like(ddt_ref)
            dnw_ref[...] = jnp.zeros_like(dnw_ref)

        lane = lax.broadcasted_iota(jnp.int32, (1, 128), 1)
        args = _dn_load(q_ref, k_ref, v_ref, z_ref, bd_ref, al_ref, dt_ref, nw_ref, ss_ref[0])
        dy = jnp.stack([dy_ref[:, h * 128:(h + 1) * 128] for h in range(DN_HEADS)], axis=0)
        _, vjp = jax.vjp(functools.partial(_dn_chunk, cs=_dn_consts()), *args)
        gq, gk, gv, gz, gb, ga, gs, gal, gdt, gnw = vjp((dy, ds_scr[...]))
        ds_scr[...] = gs
        dbd = jnp.zeros((c, 128), F32)
        dal = jnp.zeros((1, 128), F32)
        ddt = jnp.zeros((1, 128), F32)
        for h in range(DN_HEADS):
            sl = slice(h * 128, (h + 1) * 128)
            dqkv_ref[:, sl] = gq[h]
            dqkv_ref[:, DN_W + h * 128:DN_W + (h + 1) * 128] = gk[h]
            dqkv_ref[:, 2 * DN_W + h * 128:2 * DN_W + (h + 1) * 128] = gv[h]
            dz_ref[:, sl] = gz[h]
            dbd = dbd + jnp.where(lane == h, gb[h], 0.0) + jnp.where(lane == DN_HEADS + h, ga[h], 0.0)
            dal = dal + jnp.where(lane == h, gal[h], 0.0)
            ddt = ddt + jnp.where(lane == h, gdt[h], 0.0)
        dbd_ref[...] = dbd
        dal_ref[...] += dal
        ddt_ref[...] += ddt
        dnw_ref[...] += gnw

    rev = lambda i: n - 1 - i
    wide = lambda j: pl.BlockSpec((c, DN_W), lambda i: (rev(i), j))
    vec = pl.BlockSpec((1, 128), lambda i: (0, 0))
    return pl.pallas_call(
        body, name=name, grid=(n,),
        in_specs=[wide(0), wide(1), wide(2), wide(3), pl.BlockSpec((c, 128), lambda i: (rev(i), CB_BD)), vec, vec, vec,
                  pl.BlockSpec((1, DN_HEADS, 128, 128), lambda i: (rev(i), 0, 0, 0)), wide(0)],
        out_specs=[pl.BlockSpec((c, 3 * DN_W), lambda i: (rev(i), 0)), wide(0),
                   pl.BlockSpec((c, 128), lambda i: (rev(i), 0)), vec, vec, vec],
        out_shape=[jax.ShapeDtypeStruct((t, 3 * DN_W), F32), jax.ShapeDtypeStruct((t, DN_W), F32),
                   jax.ShapeDtypeStruct((t, 128), F32), jax.ShapeDtypeStruct((1, 128), F32),
                   jax.ShapeDtypeStruct((1, 128), F32), jax.ShapeDtypeStruct((1, 128), F32)],
        scratch_shapes=[pltpu.VMEM((DN_HEADS, 128, 128), F32)],
        compiler_params=_cp(("arbitrary",)),
    )(qkv, qkv, qkv, p, p, alog, dtb, nw, states, dycat)


def _rope(x, cosf, sins):
    return x * cosf + pltpu.roll(x, HEAD_DIM // 2, 1) * sins


def _rope_t(d, cosf, sins):
    return d * cosf + pltpu.roll(d * sins, HEAD_DIM // 2, 1)


def _swa_masks():
    b = SWA_BLOCK
    i = lax.broadcasted_iota(jnp.int32, (SWA_GROUP * b, b), 0) & (b - 1)
    j = lax.broadcasted_iota(jnp.int32, (SWA_GROUP * b, b), 1)
    return j > i, j <= i


def _swa_sink_col(sinks_ref, h):
    b = SWA_BLOCK
    r = lax.broadcasted_iota(jnp.int32, (SWA_GROUP * b, 1), 0)
    s = [sinks_ref[h * SWA_GROUP + g] for g in range(SWA_GROUP)]
    return jnp.where(r < b, s[0], jnp.where(r < 2 * b, s[1], s[2]))


def _swa_specs(t, h_first):
    nb = t // SWA_BLOCK

    def at(col, off):
        def imap(h, n):
            return (jnp.clip(n + off, 0, nb - 1), col(h))
        return pl.BlockSpec((SWA_BLOCK, 128), imap)
    return at


def _swa_fwd(p, cosf, sins, sinks, name):
    t = p.shape[0]
    b = SWA_BLOCK
    nb = t // b
    at = _swa_specs(t, None)
    scale = HEAD_DIM ** -0.5

    def body(q0, q1, q2, kp, kc, vp, vc, cc, sc, cp, sp, sinks_ref, o_ref, lse_ref):
        h, n = pl.program_id(0), pl.program_id(1)
        qs = jnp.concatenate([_rope(q[...], cc[...], sc[...]) for q in (q0, q1, q2)], axis=0)
        ks = jnp.concatenate([_rope(kp[...], cp[...], sp[...]), _rope(kc[...], cc[...], sc[...])], axis=0)
        vs = jnp.concatenate([vp[...], vc[...]], axis=0)
        mp, mc = _swa_masks()
        mask = jnp.concatenate([mp & (n > 0), mc], axis=1)
        s = jnp.where(mask, _dot(qs, ks, "nt") * scale, -1e30)
        sink = _swa_sink_col(sinks_ref, h)
        m = jnp.maximum(jnp.max(s, axis=1, keepdims=True), sink)
        e = jnp.exp(s - m)
        l = jnp.sum(e, axis=1, keepdims=True) + jnp.exp(sink - m)
        o = _dot(e, vs) / l
        lse = m + jnp.log(l)
        lane = lax.broadcasted_iota(jnp.int32, (1, 128), 1)
        tile = jnp.zeros((b, 128), F32)
        for g in range(SWA_GROUP):
            o_ref[:, g * 128:(g + 1) * 128] = o[g * b:(g + 1) * b]
            tile = tile + jnp.where(lane == g, lse[g * b:(g + 1) * b], 0.0)
        lse_ref[0] = tile

    qcol = lambda g: (lambda h: CB_SQ + h * SWA_GROUP + g)
    kcol, vcol, one = (lambda h: CB_SK + h), (lambda h: CB_SV + h), (lambda h: 0)
    in_specs = [at(qcol(0), 0), at(qcol(1), 0), at(qcol(2), 0), at(kcol, -1), at(kcol, 0), at(vcol, -1), at(vcol, 0),
                at(one, 0), at(one, 0), at(one, -1), at(one, -1), pl.BlockSpec(memory_space=pltpu.SMEM)]
    return pl.pallas_call(
        body, name=name, grid=(SWA_KV_HEADS, nb), in_specs=in_specs,
        out_specs=[pl.BlockSpec((b, SWA_GROUP * 128), lambda h, n: (n, h)), pl.BlockSpec((1, b, 128), lambda h, n: (h, n, 0))],
        out_shape=[jax.ShapeDtypeStruct((t, SWA_W), F32), jax.ShapeDtypeStruct((SWA_KV_HEADS, t, 128), F32)],
        compiler_params=_cp(("parallel", "parallel")),
    )(p, p, p, p, p, p, p, cosf, sins, cosf, sins, sinks)


def _swa_bwd(p, cosf, sins, sinks, o, lse, dycat, name):
    t = p.shape[0]
    b = SWA_BLOCK
    nb = t // b
    at = _swa_specs(t, None)
    scale = HEAD_DIM ** -0.5
    gb = SWA_GROUP * b

    def body(q0, q1, q2, r0, r1, r2, kp, kc, vp, vc, cc, sc, cp, sp, cn, sn, d0, d1, d2, e0, e1, e2,
             oc_ref, on_ref, lc_ref, ln_ref, sinks_ref, dq_ref, dk_ref, dv_ref, dsk_ref):
        h, n = pl.program_id(0), pl.program_id(1)
        lane = lax.broadcasted_iota(jnp.int32, (1, 128), 1)
        stack = lambda refs: jnp.concatenate([x[...] for x in refs], axis=0)
        q_c = jnp.concatenate([_rope(q[...], cc[...], sc[...]) for q in (q0, q1, q2)], axis=0)
        q_n = jnp.concatenate([_rope(q[...], cn[...], sn[...]) for q in (r0, r1, r2)], axis=0)
        k_p = _rope(kp[...], cp[...], sp[...])
        k_c = _rope(kc[...], cc[...], sc[...])
        do_c, do_n = stack((d0, d1, d2)), stack((e0, e1, e2))
        o_c = jnp.concatenate([oc_ref[:, g * 128:(g + 1) * 128] for g in range(SWA_GROUP)], axis=0)
        o_n = jnp.concatenate([on_ref[:, g * 128:(g + 1) * 128] for g in range(SWA_GROUP)], axis=0)
        lse_c = jnp.concatenate([_lane_pick(lc_ref[0], lane, g) for g in range(SWA_GROUP)], axis=0)
        lse_n = jnp.concatenate([_lane_pick(ln_ref[0], lane, g) for g in range(SWA_GROUP)], axis=0)
        dl_c = jnp.sum(do_c * o_c, axis=1, keepdims=True)
        dl_n = jnp.sum(do_n * o_n, axis=1, keepdims=True)
        mp, mc = _swa_masks()

        def pair(qr, kr, v, do, lse_, dl, mask):
            s = _dot(qr, kr, "nt") * scale
            pr = jnp.where(mask, jnp.exp(s - lse_), 0.0)
            ds = pr * (_dot(do, v, "nt") - dl) * scale
            return _dot(ds, kr), _dot(ds, qr, "tn"), _dot(pr, do, "tn")

        dq_a, _, _ = pair(q_c, k_p, vp[...], do_c, lse_c, dl_c, mp & (n > 0))
        dq_b, dk_b, dv_b = pair(q_c, k_c, vc[...], do_c, lse_c, dl_c, mc)
        _, dk_n, dv_n = pair(q_n, k_c, vc[...], do_n, lse_n, dl_n, mp & (n < nb - 1))
        dq = dq_a + dq_b
        for g in range(SWA_GROUP):
            dq_ref[:, g * 128:(g + 1) * 128] = _rope_t(dq[g * b:(g + 1) * b], cc[...], sc[...])
        dk_ref[...] = _rope_t(dk_b + dk_n, cc[...], sc[...])
        dv_ref[...] = dv_b + dv_n

        @pl.when(n == 0)
        def _():
            dsk_ref[...] = jnp.zeros_like(dsk_ref)

        w = -jnp.exp(_swa_sink_col(sinks_ref, h) - lse_c) * dl_c
        acc = jnp.zeros((1, 128), F32)
        for g in range(SWA_GROUP):
            acc = acc + jnp.where(lane == g, jnp.sum(w[g * b:(g + 1) * b], axis=0, keepdims=True), 0.0)
        dsk_ref[0] += jnp.broadcast_to(acc, (8, 128))

    qcol = lambda g: (lambda h: CB_SQ + h * SWA_GROUP + g)
    dcol = lambda g: (lambda h: (DN_W + POOL_W) // 128 + h * SWA_GROUP + g)
    kcol, vcol, one = (lambda h: CB_SK + h), (lambda h: CB_SV + h), (lambda h: 0)
    wide = lambda off: pl.BlockSpec((b, SWA_GROUP * 128), lambda h, n: (jnp.clip(n + off, 0, nb - 1), h))
    lspec = lambda off: pl.BlockSpec((1, b, 128), lambda h, n: (h, jnp.clip(n + off, 0, nb - 1), 0))
    in_specs = ([at(qcol(g), 0) for g in range(3)] + [at(qcol(g), 1) for g in range(3)]
                + [at(kcol, -1), at(kcol, 0), at(vcol, -1), at(vcol, 0)]
                + [at(one, 0), at(one, 0), at(one, -1), at(one, -1), at(one, 1), at(one, 1)]
                + [at(dcol(g), 0) for g in range(3)] + [at(dcol(g), 1) for g in range(3)]
                + [wide(0), wide(1), lspec(0), lspec(1), pl.BlockSpec(memory_space=pltpu.SMEM)])
    kv_out = pl.BlockSpec((b, 128), lambda h, n: (n, h))
    return pl.pallas_call(
        body, name=name, grid=(SWA_KV_HEADS, nb), in_specs=in_specs,
        out_specs=[wide(0), kv_out, kv_out, pl.BlockSpec((1, 8, 128), lambda h, n: (h, 0, 0))],
        out_shape=[jax.ShapeDtypeStruct((t, SWA_W), F32), jax.ShapeDtypeStruct((t, SWA_KV_W), F32),
                   jax.ShapeDtypeStruct((t, SWA_KV_W), F32), jax.ShapeDtypeStruct((SWA_KV_HEADS, 8, 128), F32)],
        compiler_params=_cp(("parallel", "arbitrary")),
    )(*([p] * 10), cosf, sins, cosf, sins, cosf, sins, *([dycat] * 6), o, o, lse, lse, sinks)


def _adam_math(w, g, m, v):
    m = ADAM_B1 * m + (1.0 - ADAM_B1) * g
    v = ADAM_B2 * v + (1.0 - ADAM_B2) * (g * g)
    m_hat = m / (1.0 - ADAM_B1 ** ADAM_STEP)
    v_hat = v / (1.0 - ADAM_B2 ** ADAM_STEP)
    return -ADAM_LR * (m_hat / (jnp.sqrt(v_hat) + ADAM_EPS) + ADAM_WD * w), m, v


def _adamw(w, g, m, v, name):
    shape = w.shape
    cols = shape[-1]
    rows = math.prod(shape[:-1])
    flat = lambda a: a.reshape(rows, cols)
    r = rows
    for cand in (512, 256, 128, 64, 32, 16, 8):
        if rows % cand == 0 and cand * cols * 4 <= (1 << 20):
            r = cand
            break

    def body(w_ref, g_ref, m_ref, v_ref, d_ref, nm_ref, nv_ref):
        d_ref[...], nm_ref[...], nv_ref[...] = _adam_math(w_ref[...], g_ref[...], m_ref[...], v_ref[...])

    spec = pl.BlockSpec((r, cols), lambda i: (i, 0))
    outs = pl.pallas_call(
        body, name=name, grid=(rows // r,), in_specs=[spec] * 4, out_specs=[spec] * 3,
        out_shape=[jax.ShapeDtypeStruct((rows, cols), F32)] * 3, compiler_params=_cp(("parallel",)),
    )(flat(w), flat(g), flat(m), flat(v))
    return tuple(o.reshape(shape) for o in outs)


BIG = ("w_in", "w_out", "ffn_w_up", "ffn_w_down")
CONV = ("dn_conv_w", "ffn_conv_w")
SHARD_AXIS = {"w_in": 2, "w_out": 1, "ffn_w_up": 2, "ffn_w_down": 1}
SMALL = ("norm_mix_pre", "dn_a_log", "dn_dt_bias", "dn_norm_w", "pool_w", "pool_scale", "swa_sinks",
         "norm_mix_post", "norm_ffn_pre", "ffn_conv_b", "norm_ffn_post")
WEIGHTS = ("norm_mix_pre", "w_in", "dn_conv_w", "dn_a_log", "dn_dt_bias", "dn_norm_w", "pool_w", "pool_scale",
           "swa_sinks", "w_out", "norm_mix_post", "norm_ffn_pre", "ffn_w_up", "ffn_conv_w", "ffn_conv_b",
           "ffn_w_down", "norm_ffn_post")


def _pad_in(w):
    z = lambda n: jnp.zeros(w.shape[:-1] + (n,), w.dtype)
    return jnp.concatenate([w[..., :GATE_END], z(CB_POOL * 128 - GATE_END), w[..., GATE_END:],
                            z(IN_PAD - CB_POOL * 128 - (IN_TRUE - GATE_END))], axis=-1)


def _unpad_in(g):
    return jnp.concatenate([g[..., :GATE_END], g[..., CB_POOL * 128:CB_POOL * 128 + IN_TRUE - GATE_END]], axis=-1)


def _lanes(v):
    return jnp.zeros((1, 128), F32).at[0, :v.shape[0]].set(v)


def _rope_tables(positions):
    inv_freq = 1.0 / (ROPE_THETA ** (jnp.arange(0, HEAD_DIM, 2, dtype=F32) / HEAD_DIM))
    ang = positions.astype(F32)[:, None] * inv_freq
    cos, sin = jnp.cos(ang), jnp.sin(ang)
    return jnp.concatenate([cos, cos], axis=-1), jnp.concatenate([-sin, sin], axis=-1)


def _local_step(x, positions, target, w):
    depth = w["w_out"].shape[0]
    t = x.shape[0]
    cosf, sins = _rope_tables(positions)
    saved = []
    for l in range(depth):
        nm = f"l{l}_"
        n1, n2, n3, n4 = (w[k][l][None] for k in ("norm_mix_pre", "norm_mix_post", "norm_ffn_pre", "norm_ffn_post"))
        alog, dtb, dnw = _lanes(w["dn_a_log"][l]), _lanes(w["dn_dt_bias"][l]), w["dn_norm_w"][l][None]
        psc, cb = w["pool_scale"][l][None], w["ffn_conv_b"][l][None]
        h = _norm_fwd(x, n1, nm + "norm1")
        p = _mm(h, w["w_in"][l], "nn", F32, nm + "in_proj")
        qkv = _dn_pre_fwd(p, w["dn_conv_w"][l], nm + "dn_pre")
        y_dn, st = _dn_fwd(qkv, p, alog, dtb, dnw, nm + "dn")
        y_pool = _pool_fwd(p, w["pool_w"][l], psc, nm + "pool")
        y_swa, lse = _swa_fwd(p, cosf, sins, w["swa_sinks"][l], nm + "swa")
        ycat = jnp.concatenate([y_dn, y_pool, y_swa], axis=1).astype(BF16)
        mix = _mm(ycat, w["w_out"][l], "nn", F32, nm + "out_proj")
        x1 = _resnorm_fwd(x, mix, n2, nm + "res1")
        h2 = _norm_fwd(x1, n3, nm + "norm3")
        up = _mm(h2, w["ffn_w_up"][l], "nn", F32, nm + "ffn_up")
        act = _ffn_act_fwd(up, w["ffn_conv_w"][l], cb, nm + "ffn_act")
        f = _mm(act, w["ffn_w_down"][l], "nn", F32, nm + "ffn_down")
        x2 = _resnorm_fwd(x1, f, n4, nm + "res2")
        saved.append(dict(x=x, h=h, p=p, qkv=qkv, st=st, y_swa=y_swa, lse=lse, ycat=ycat, mix=mix, x1=x1, h2=h2,
                          up=up, act=act, f=f, n=(n1, n2, n3, n4), alog=alog, dtb=dtb, dnw=dnw, psc=psc, cb=cb))
        x = x2
    loss, dx = _loss_head(x, target, "loss_head")
    grads = {k: [None] * depth for k in WEIGHTS}
    for l in reversed(range(depth)):
        nm, s = f"l{l}_b_", saved[l]
        n1, n2, n3, n4 = s["n"]
        df, g4 = _norm_bwd(s["f"], n4, dx, None, BF16, nm + "res2")
        dact = _mm(df, w["ffn_w_down"][l], "nt", F32, nm + "ffn_down_dx")
        grads["ffn_w_down"][l] = _mm(s["act"], df, "tn", F32, nm + "ffn_down_dw")
        dua, dub, dwa, dwb, dca, dcb = _ffn_act_bwd(s["up"], w["ffn_conv_w"][l], s["cb"], dact, nm + "ffn_act")
        dup = jnp.concatenate([dua, dub], axis=1)
        grads["ffn_conv_w"][l] = jnp.concatenate([dwa, dwb], axis=1)
        grads["ffn_conv_b"][l] = jnp.concatenate([dca, dcb], axis=1)[0]
        grads["ffn_w_up"][l] = _mm(s["h2"], dup, "tn", F32, nm + "ffn_up_dw")
        dh2 = _mm(dup, w["ffn_w_up"][l], "nt", BF16, nm + "ffn_up_dx")
        dx1, g3 = _norm_bwd(s["x1"], n3, dh2, dx, F32, nm + "norm3")
        dmix, g2 = _norm_bwd(s["mix"], n2, dx1, None, BF16, nm + "res1")
        grads["w_out"][l] = _mm(s["ycat"], dmix, "tn", F32, nm + "out_proj_dw")
        dycat = _mm(dmix, w["w_out"][l], "nt", F32, nm + "out_proj_dx")
        dqkv, dz, dbd, gal, gdt, gnw = _dn_bwd(s["qkv"], s["p"], s["alog"], s["dtb"], s["dnw"], s["st"], dycat, nm + "dn")
        dpq, gconv = _dn_pre_bwd(s["p"], w["dn_conv_w"][l], dqkv, nm + "dn_pre")
        dpool, gpw, gpsc = _pool_bwd(s["p"], w["pool_w"][l], s["psc"], dycat, nm + "pool")
        dsq, dsk, dsv, gsk = _swa_bwd(s["p"], cosf, sins, w["swa_sinks"][l], s["y_swa"], s["lse"], dycat, nm + "swa")
        dp = jnp.concatenate([dpq, dz, dbd, dpool, dsq, dsk, dsv, jnp.zeros((t, 128), F32)], axis=1).astype(BF16)
        grads["w_in"][l] = _unpad_in(_mm(s["h"], dp, "tn", F32, nm + "in_proj_dw"))
        dh = _mm(dp, w["w_in"][l], "nt", BF16, nm + "in_proj_dx")
        dx, g1 = _norm_bwd(s["x"], n1, dh, dx1, F32, nm + "norm1")
        grads["norm_mix_pre"][l], grads["norm_mix_post"][l] = g1[0], g2[0]
        grads["norm_ffn_pre"][l], grads["norm_ffn_post"][l] = g3[0], g4[0]
        grads["dn_conv_w"][l] = gconv
        grads["dn_a_log"][l], grads["dn_dt_bias"][l], grads["dn_norm_w"][l] = gal[0, :DN_HEADS], gdt[0, :DN_HEADS], gnw[0]
        grads["pool_w"][l], grads["pool_scale"][l] = gpw, gpsc[0]
        grads["swa_sinks"][l] = gsk[:, 0, :SWA_GROUP].reshape(SWA_HEADS)
    return loss, dx, grads


def _flat2(a):
    return a.reshape(math.prod(a.shape[:-1]), a.shape[-1])


def _ew_rows(rows, cols, n_arrays):
    for cand in (512, 256, 128, 64, 32, 16):
        if rows % cand == 0 and cand * cols * 4 * n_arrays <= (8 << 20):
            return cand
    return rows


def _cast_into_slot(a, slot, dtype, name):
    a2 = _flat2(a)
    rows, cols = a2.shape
    r = _ew_rows(rows, cols, 2)
    nb = rows // r

    def body(s_ref, a_ref, o_ref):
        o_ref[...] = a_ref[...].astype(o_ref.dtype)

    return pl.pallas_call(
        body, name=name,
        grid_spec=pltpu.PrefetchScalarGridSpec(
            num_scalar_prefetch=1, grid=(nb,),
            in_specs=[pl.BlockSpec((r, cols), lambda i, s: (i, 0))],
            out_specs=pl.BlockSpec((r, cols), lambda i, s: (s[0] * nb + i, 0))),
        out_shape=jax.ShapeDtypeStruct((4 * rows, cols), dtype), compiler_params=_cp(("parallel",)),
    )(slot, a2).reshape((4,) + a.shape)


def _add2(a, b, dtype, name):
    a2, b2 = _flat2(a), _flat2(b)
    rows, cols = a2.shape
    r = _ew_rows(rows, cols, 3)

    def body(a_ref, b_ref, o_ref):
        o_ref[...] = (a_ref[...].astype(F32) + b_ref[...].astype(F32)).astype(o_ref.dtype)

    spec = pl.BlockSpec((r, cols), lambda i: (i, 0))
    return pl.pallas_call(body, name=name, grid=(rows // r,), in_specs=[spec, spec], out_specs=spec,
                          out_shape=jax.ShapeDtypeStruct((rows, cols), dtype), compiler_params=_cp(("parallel",)),
                          )(a2, b2).reshape(a.shape)


def _sum_slots(a, name):
    s = a.shape[0]
    a3 = a.reshape(s, math.prod(a.shape[1:-1]), a.shape[-1])
    _, rows, cols = a3.shape
    r = _ew_rows(rows, cols, s + 1)

    def body(a_ref, o_ref):
        acc = a_ref[0].astype(F32)
        for k in range(1, s):
            acc = acc + a_ref[k].astype(F32)
        o_ref[...] = acc

    return pl.pallas_call(body, name=name, grid=(rows // r,),
                          in_specs=[pl.BlockSpec((s, r, cols), lambda i: (0, i, 0))],
                          out_specs=pl.BlockSpec((r, cols), lambda i: (i, 0)),
                          out_shape=jax.ShapeDtypeStruct((rows, cols), F32), compiler_params=_cp(("parallel",)),
                          )(a3).reshape(a.shape[1:])


def _owner_sum(own, got, place, name):
    l2 = own.shape[1]
    rows, cols = l2 * math.prod(own.shape[2:-1]), own.shape[-1]
    r = _ew_rows(rows, cols, 6)
    nb = rows // r

    def body(s_ref, own_ref, got_ref, o_ref):
        acc = own_ref[0].astype(F32)
        for k in range(3):
            acc = acc + got_ref[k].astype(F32)
        o_ref[...] = acc

    return pl.pallas_call(
        body, name=name,
        grid_spec=pltpu.PrefetchScalarGridSpec(
            num_scalar_prefetch=1, grid=(nb,),
            in_specs=[pl.BlockSpec((1, r, cols), lambda i, s: (s[0], i, 0)), pl.BlockSpec((3, r, cols), lambda i, s: (0, i, 0))],
            out_specs=pl.BlockSpec((r, cols), lambda i, s: (s[1] * nb + i, 0))),
        out_shape=jax.ShapeDtypeStruct((2 * rows, cols), F32), compiler_params=_cp(("parallel",)),
    )(place, own.reshape(4, rows, cols), got.reshape(3, rows, cols)).reshape((2 * l2,) + own.shape[2:])


MESH = pl.DeviceIdType.MESH
ANY = pl.BlockSpec(memory_space=pl.ANY)


def _place():
    x, y, c = lax.axis_index("x"), lax.axis_index("y"), lax.axis_index("c")
    chips = [(1 - x, y), (x, 1 - y), (1 - x, 1 - y)]
    return x, y, c, chips


def _gather_chips(arrs, name):
    na = len(arrs)
    l2 = arrs[0].shape[1] // 2

    def body(*refs):
        outs = refs[na:2 * na]
        send1, recv1, send2, recv2 = refs[2 * na:]
        x, y, c, chips = _place()
        me = 2 * x + y
        half = pl.ds(c * l2, l2)
        other = pl.ds((1 - c) * l2, l2)

        def first(k, j):
            px, py = chips[j]
            mine = outs[k].at[me, half]
            return pltpu.make_async_remote_copy(mine, mine, send1.at[k, j], recv1.at[k, j],
                                                device_id=(px, py, c), device_id_type=MESH)

        def landed(k, j):
            px, py = chips[j]
            return outs[k].at[2 * px + py, half]

        def passed(k, j):
            return pltpu.make_async_remote_copy(landed(k, j), landed(k, j), send2.at[k, j], recv2.at[k, j],
                                                device_id=(x, y, 1 - c), device_id_type=MESH)

        def from_sibling(k, j):
            px, py = chips[j]
            dst = outs[k].at[2 * px + py, other]
            return pltpu.make_async_remote_copy(dst, dst, send2.at[k, j], recv2.at[k, j],
                                                device_id=(x, y, 1 - c), device_id_type=MESH)

        for k in range(na):
            for j in range(3):
                first(k, j).start()
        for k in range(na):
            for j in range(3):
                first(k, j).wait_recv()
                passed(k, j).start()
        for k in range(na):
            for j in range(3):
                from_sibling(k, j).wait_recv()
        for k in range(na):
            for j in range(3):
                first(k, j).wait_send()
                passed(k, j).wait_send()

    return pl.pallas_call(
        body, name=name, in_specs=[ANY] * na, out_specs=[ANY] * na,
        out_shape=[jax.ShapeDtypeStruct(a.shape, a.dtype) for a in arrs],
        input_output_aliases={k: k for k in range(na)},
        scratch_shapes=[pltpu.SemaphoreType.DMA((na, 3))] * 4,
    )(*arrs)


def _swap_sibling(arrs, name):
    na = len(arrs)

    def body(*refs):
        ins, outs, send, recv = refs[:na], refs[na:2 * na], refs[2 * na], refs[2 * na + 1]
        x, y, c, _ = _place()
        cps = [pltpu.make_async_remote_copy(ins[k], outs[k], send.at[k], recv.at[k], device_id=(x, y, 1 - c),
                                            device_id_type=MESH) for k in range(na)]
        for cp in cps:
            cp.start()
        for cp in cps:
            cp.wait()

    return pl.pallas_call(
        body, name=name, in_specs=[ANY] * na, out_specs=[ANY] * na,
        out_shape=[jax.ShapeDtypeStruct(a.shape, a.dtype) for a in arrs],
        scratch_shapes=[pltpu.SemaphoreType.DMA((na,))] * 2,
    )(*arrs)


def _scatter_chips(arrs, name):
    na = len(arrs)

    def body(*refs):
        ins, outs, send, recv = refs[:na], refs[na:2 * na], refs[2 * na], refs[2 * na + 1]
        x, y, c, chips = _place()
        cps = []
        for k in range(na):
            for j, (px, py) in enumerate(chips):
                cps.append(pltpu.make_async_remote_copy(ins[k].at[2 * px + py], outs[k].at[j], send.at[k, j], recv.at[k, j],
                                                        device_id=(px, py, c), device_id_type=MESH))
        for cp in cps:
            cp.start()
        for cp in cps:
            cp.wait()

    return pl.pallas_call(
        body, name=name, in_specs=[ANY] * na, out_specs=[ANY] * na,
        out_shape=[jax.ShapeDtypeStruct((3,) + a.shape[1:], a.dtype) for a in arrs],
        scratch_shapes=[pltpu.SemaphoreType.DMA((na, 3))] * 2,
    )(*arrs)


def _join_halves(arrs, name):
    na = len(arrs)
    l2 = arrs[0].shape[0] // 2

    def body(*refs):
        outs, send, recv = refs[na:2 * na], refs[2 * na], refs[2 * na + 1]
        x, y, c, _ = _place()
        mine = pl.ds(c * l2, l2)
        theirs = pl.ds((1 - c) * l2, l2)
        cps = [pltpu.make_async_remote_copy(outs[k].at[mine], outs[k].at[mine], send.at[k], recv.at[k],
                                            device_id=(x, y, 1 - c), device_id_type=MESH) for k in range(na)]
        for cp in cps:
            cp.start()
        for k, cp in enumerate(cps):
            cp.wait_send()
            pltpu.make_async_remote_copy(outs[k].at[theirs], outs[k].at[theirs], send.at[k], recv.at[k],
                                         device_id=(x, y, 1 - c), device_id_type=MESH).wait_recv()

    return pl.pallas_call(
        body, name=name, in_specs=[ANY] * na, out_specs=[ANY] * na,
        out_shape=[jax.ShapeDtypeStruct(a.shape, a.dtype) for a in arrs],
        input_output_aliases={k: k for k in range(na)},
        scratch_shapes=[pltpu.SemaphoreType.DMA((na,))] * 2,
    )(*arrs)


def _gather_all(a, name):
    def body(a_ref, o_ref, send, recv, local):
        x, y, c, _ = _place()
        me = 4 * x + 2 * y + c
        mine = pltpu.make_async_copy(a_ref, o_ref.at[me], local)
        mine.start()
        cps = []
        for j in range(1, 8):
            peer = (x ^ (j >> 2), y ^ ((j >> 1) & 1), c ^ (j & 1))
            cps.append(pltpu.make_async_remote_copy(a_ref, o_ref.at[me], send.at[j - 1], recv.at[j - 1],
                                                    device_id=peer, device_id_type=MESH))
        for cp in cps:
            cp.start()
        for cp in cps:
            cp.wait()
        mine.wait()

    return pl.pallas_call(
        body, name=name, in_specs=[ANY], out_specs=ANY,
        out_shape=jax.ShapeDtypeStruct((8,) + a.shape, a.dtype),
        scratch_shapes=[pltpu.SemaphoreType.DMA((7,)), pltpu.SemaphoreType.DMA((7,)), pltpu.SemaphoreType.DMA],
    )(a)


def _to_full(name, g):
    ax = SHARD_AXIS[name]
    g = jnp.moveaxis(g, 0, ax)
    return g.reshape(g.shape[:ax] + (4 * g.shape[ax + 1],) + g.shape[ax + 2:])


def _to_pieces(name, g):
    ax = SHARD_AXIS[name]
    g = g.reshape(g.shape[:ax] + (4, g.shape[ax] // 4) + g.shape[ax + 1:])
    return jnp.moveaxis(g, ax, 0)


def _pack(parts):
    flat = jnp.concatenate([p.reshape(-1) for p in parts])
    n = flat.shape[0]
    rows = -(-n // 1024) * 8
    return jnp.pad(flat, (0, rows * 128 - n)).reshape(rows, 128)


def _unpack(buf, like):
    flat, out, off = buf.reshape(-1), [], 0
    for p in like:
        out.append(flat[off:off + p.size].reshape(p.shape))
        off += p.size
    return out


def kernel(x, positions, norm_mix_pre, w_in, dn_conv_w, dn_a_log, dn_dt_bias, dn_norm_w, pool_w, pool_scale, swa_sinks, w_out, norm_mix_post, norm_ffn_pre, ffn_w_up, ffn_conv_w, ffn_conv_b, ffn_w_down, norm_ffn_post, loss_target, m_norm_mix_pre, m_w_in, m_dn_conv_w, m_dn_a_log, m_dn_dt_bias, m_dn_norm_w, m_pool_w, m_pool_scale, m_swa_sinks, m_w_out, m_norm_mix_post, m_norm_ffn_pre, m_ffn_w_up, m_ffn_conv_w, m_ffn_conv_b, m_ffn_w_down, m_norm_ffn_post, v_norm_mix_pre, v_w_in, v_dn_conv_w, v_dn_a_log, v_dn_dt_bias, v_dn_norm_w, v_pool_w, v_pool_scale, v_swa_sinks, v_w_out, v_norm_mix_post, v_norm_ffn_pre, v_ffn_w_up, v_ffn_conv_w, v_ffn_conv_b, v_ffn_w_down, v_norm_ffn_post):
    wts = dict(zip(WEIGHTS, (norm_mix_pre, w_in, dn_conv_w, dn_a_log, dn_dt_bias, dn_norm_w, pool_w, pool_scale, swa_sinks,
                             w_out, norm_mix_post, norm_ffn_pre, ffn_w_up, ffn_conv_w, ffn_conv_b, ffn_w_down, norm_ffn_post)))
    mom = dict(zip(WEIGHTS, (m_norm_mix_pre, m_w_in, m_dn_conv_w, m_dn_a_log, m_dn_dt_bias, m_dn_norm_w, m_pool_w, m_pool_scale,
                             m_swa_sinks, m_w_out, m_norm_mix_post, m_norm_ffn_pre, m_ffn_w_up, m_ffn_conv_w, m_ffn_conv_b,
                             m_ffn_w_down, m_norm_ffn_post)))
    var = dict(zip(WEIGHTS, (v_norm_mix_pre, v_w_in, v_dn_conv_w, v_dn_a_log, v_dn_dt_bias, v_dn_norm_w, v_pool_w, v_pool_scale,
                             v_swa_sinks, v_w_out, v_norm_mix_post, v_norm_ffn_pre, v_ffn_w_up, v_ffn_conv_w, v_ffn_conv_b,
                             v_ffn_w_down, v_norm_ffn_post)))
    depth = w_out.shape[0]
    l2 = depth // 2
    c = lax.axis_index("c")
    chip = 2 * lax.axis_index("x") + lax.axis_index("y")
    slot = jnp.stack([chip]).astype(jnp.int32)
    place = jnp.stack([chip, c]).astype(jnp.int32)

    shards = [_cast_into_slot(wts[k], slot, BF16, "cast_" + k) for k in BIG]
    w = dict(wts)
    w.update({k: _to_full(k, g) for k, g in zip(BIG, _gather_chips(shards, "gather_weights"))})
    w["w_in"] = _pad_in(w["w_in"])
    conv_like = [wts[k] for k in CONV]
    conv_all = _gather_all(_pack(conv_like), "gather_conv")
    for i, k in enumerate(CONV):
        w[k] = jnp.concatenate([_unpack(conv_all[2 * j], conv_like)[i] for j in range(4)], axis=2)

    loss, dx, grads = _local_step(x[0], positions[0], loss_target[0], w)
    loss = lax.psum(loss[0, 0], ("x", "y", "c"))

    stacked = {k: jnp.stack(grads[k]) for k in BIG}
    mine = [_to_pieces(k, lax.dynamic_slice_in_dim(stacked[k], c * l2, l2, 0)).astype(BF16) for k in BIG]
    sibs = [_to_pieces(k, lax.dynamic_slice_in_dim(stacked[k], (1 - c) * l2, l2, 0)).astype(BF16) for k in BIG]
    got = _swap_sibling(sibs, "grads_to_sibling")
    chip_sum = [_add2(a, b, BF16, "chip_sum_" + k) for k, a, b in zip(BIG, mine, got)]
    arrived = _scatter_chips(chip_sum, "grads_to_owner")
    halves = [_owner_sum(a, b, place, "owner_sum_" + k) for k, a, b in zip(BIG, chip_sum, arrived)]
    g_big = dict(zip(BIG, _join_halves(halves, "grads_join")))

    small_like = [wts[k] for k in SMALL]
    full_like = small_like + [w[k] for k in CONV]
    g_buf = _sum_slots(_gather_all(_pack([jnp.stack(grads[k]) for k in SMALL + CONV]), "gather_small"), "sum_small")
    g_small = dict(zip(SMALL + CONV, _unpack(g_buf, full_like)))
    for k in CONV:
        width = wts[k].shape[2]
        g_small[k] = lax.dynamic_slice_in_dim(g_small[k], chip * width, width, 2)
    pk = lambda d: _pack([d[k] for k in SMALL + CONV])
    upd = _adamw(pk(wts), pk(g_small), pk(mom), pk(var), "adam_small")
    upd_small = [dict(zip(SMALL + CONV, _unpack(b, small_like + conv_like))) for b in upd]

    g_all, d_all, m_all, v_all = {}, {}, {}, {}
    for k in WEIGHTS:
        if k in BIG:
            g_all[k] = g_big[k]
            d_all[k], m_all[k], v_all[k] = _adamw(wts[k], g_big[k], mom[k], var[k], "adam_" + k)
        else:
            g_all[k], d_all[k], m_all[k], v_all[k] = g_small[k], upd_small[0][k], upd_small[1][k], upd_small[2][k]
    return (loss, dx[None], *[g_all[k] for k in WEIGHTS], *[d_all[k] for k in WEIGHTS],
            *[m_all[k] for k in WEIGHTS], *[v_all[k] for k in WEIGHTS])
```

```python
import functools
import math

import jax
import jax.numpy as jnp
from jax import lax
from jax.experimental import pallas as pl
from jax.experimental.pallas import tpu as pltpu

F32 = jnp.float32
BF16 = jnp.bfloat16

HEAD_DIM = 128
DN_HEADS = 6
DN_CONV = 4
DN_CHUNK = 64
POOL_GROUPS = 4
SWA_HEADS = 6
SWA_KV_HEADS = 2
SWA_GROUP = SWA_HEADS // SWA_KV_HEADS
SWA_BLOCK = 128
ROPE_THETA = 10000.0
FFN_CONV = 3
NORM_EPS = 1e-6
DN_W = DN_HEADS * HEAD_DIM
POOL_W = POOL_GROUPS * HEAD_DIM
SWA_W = SWA_HEADS * HEAD_DIM
SWA_KV_W = SWA_KV_HEADS * HEAD_DIM
MIX_W = DN_W + POOL_W + SWA_W
IN_TRUE = 3 * DN_W + DN_W + 2 * DN_HEADS + POOL_W + SWA_W + 2 * SWA_KV_W
GATE_END = 4 * DN_W + 2 * DN_HEADS
CB_Z = 18
CB_BD = 24
CB_POOL = 25
CB_SQ = 29
CB_SK = 35
CB_SV = 37
IN_PAD = 40 * 128
ADAM_LR, ADAM_B1, ADAM_B2, ADAM_EPS, ADAM_WD, ADAM_STEP = 0.001, 0.9, 0.999, 1e-08, 0.01, 10

VMEM_LIMIT = 48 * 1024 * 1024
MM_TK_MAX = 2816
HIGH = lax.Precision.HIGHEST


def _cp(sem):
    return pltpu.CompilerParams(dimension_semantics=sem, vmem_limit_bytes=VMEM_LIMIT)


def _tile(n, prefs):
    for p in prefs:
        if n % p == 0:
            return p
    return n


def _rows(t):
    return _tile(t, (256, 128))


_DN = {"nn": (((1,), (0,)), ((), ())), "nt": (((1,), (1,)), ((), ())), "tn": (((0,), (0,)), ((), ()))}


def _mm_operand(arr, pick, block, idx):
    if pick is None:
        return pl.BlockSpec(block, idx)
    if pick == "split":
        per = arr.shape[2] // block[1]

        def split_idx(i, j, kk):
            r, c = idx(i, j, kk)
            return lax.div(c, per), r, lax.rem(c, per)

        return pl.BlockSpec((None,) + block, split_idx)
    slab = pick[1]
    return pl.BlockSpec((None,) + block, lambda i, j, kk: (slab,) + idx(i, j, kk))


def _mm(a, b, mode, out_dtype, name, a_pick=None, b_pick=None, into=None):
    def dims(arr, pick):
        r, c = arr.shape[-2:]
        return (r, c * arr.shape[0]) if pick == "split" else (r, c)

    (a0, a1), (b0, b1) = dims(a, a_pick), dims(b, b_pick)
    k, m = (a0, a1) if mode == "tn" else (a1, a0)
    n = b0 if mode == "nt" else b1
    lim = lambda arr, pick, is_last, full: arr.shape[2] if (pick == "split" and is_last) else full
    tm = _tile(lim(a, a_pick, mode == "tn", m), (1024, 512, 256, 128))
    tn = _tile(lim(b, b_pick, mode != "nt", n), (1024, 512, 256, 128))
    k_lim = min(lim(a, a_pick, mode != "tn", k), lim(b, b_pick, mode == "nt", k))
    tk = max([d for d in range(128, min(k_lim, MM_TK_MAX) + 1, 128) if k_lim % d == 0], default=k_lim)
    nk = k // tk

    def body(a_ref, b_ref, *rest):
        o_ref, scratch = (rest[1], rest[2:]) if into is not None else (rest[0], rest[1:])
        part = lax.dot_general(a_ref[...], b_ref[...], _DN[mode], preferred_element_type=F32)
        if nk == 1:
            o_ref[...] = part.astype(o_ref.dtype)
            return
        acc_ref, = scratch
        kk = pl.program_id(2)

        @pl.when(kk == 0)
        def _():
            acc_ref[...] = part

        @pl.when(kk > 0)
        def _():
            acc_ref[...] += part

        @pl.when(kk == nk - 1)
        def _():
            o_ref[...] = acc_ref[...].astype(o_ref.dtype)

    if mode == "tn":
        a_spec = _mm_operand(a, a_pick, (tk, tm), lambda i, j, kk: (kk, i))
    else:
        a_spec = _mm_operand(a, a_pick, (tm, tk), lambda i, j, kk: (i, kk))
    if mode == "nt":
        b_spec = _mm_operand(b, b_pick, (tn, tk), lambda i, j, kk: (j, kk))
    else:
        b_spec = _mm_operand(b, b_pick, (tk, tn), lambda i, j, kk: (kk, j))
    scratch = [pltpu.VMEM((tm, tn), F32)] if nk > 1 else []
    params = _cp(("parallel", "parallel", "arbitrary"))
    if into is None:
        return pl.pallas_call(
            body, name=name, grid=(m // tm, n // tn, nk),
            in_specs=[a_spec, b_spec], out_specs=pl.BlockSpec((tm, tn), lambda i, j, kk: (i, j)),
            out_shape=jax.ShapeDtypeStruct((m, n), out_dtype), scratch_shapes=scratch, compiler_params=params,
        )(a, b)
    buf, slab = into
    assert buf.shape[1:] == (m, n) and buf.dtype == out_dtype
    return pl.pallas_call(
        body, name=name, grid=(m // tm, n // tn, nk),
        in_specs=[a_spec, b_spec, pl.BlockSpec(memory_space=pl.ANY)],
        out_specs=pl.BlockSpec((None, tm, tn), lambda i, j, kk: (slab, i, j)),
        out_shape=jax.ShapeDtypeStruct(buf.shape, buf.dtype), input_output_aliases={2: 0},
        scratch_shapes=scratch, compiler_params=params,
    )(a, b, buf)


def _rms(x, w):
    return x * lax.rsqrt(jnp.mean(x * x, axis=-1, keepdims=True) + NORM_EPS) * w


def _norm_fwd(x, w, name):
    t, d = x.shape
    r = _rows(t)

    def body(x_ref, w_ref, h_ref):
        h_ref[...] = _rms(x_ref[...], w_ref[...]).astype(h_ref.dtype)

    return pl.pallas_call(
        body, name=name, grid=(t // r,),
        in_specs=[pl.BlockSpec((r, d), lambda i: (i, 0)), pl.BlockSpec((1, d), lambda i: (0, 0))],
        out_specs=pl.BlockSpec((r, d), lambda i: (i, 0)),
        out_shape=jax.ShapeDtypeStruct((t, d), BF16), compiler_params=_cp(("parallel",)),
    )(x, w)


def _resnorm_fwd(x, y, w, name):
    t, d = x.shape
    r = _rows(t)

    def body(x_ref, y_ref, w_ref, o_ref):
        o_ref[...] = x_ref[...] + _rms(y_ref[...], w_ref[...])

    return pl.pallas_call(
        body, name=name, grid=(t // r,),
        in_specs=[pl.BlockSpec((r, d), lambda i: (i, 0)), pl.BlockSpec((r, d), lambda i: (i, 0)),
                  pl.BlockSpec((1, d), lambda i: (0, 0))],
        out_specs=pl.BlockSpec((r, d), lambda i: (i, 0)),
        out_shape=jax.ShapeDtypeStruct((t, d), F32), compiler_params=_cp(("parallel",)),
    )(x, y, w)


def _norm_bwd(x, w, dh, add, out_dtype, name):
    t, d = x.shape
    r = _rows(t)
    has_add = add is not None

    def body(*refs):
        if has_add:
            x_ref, w_ref, dh_ref, add_ref, dx_ref, dw_ref = refs
        else:
            x_ref, w_ref, dh_ref, dx_ref, dw_ref = refs
        xv = x_ref[...]
        g = dh_ref[...].astype(F32)
        rs = lax.rsqrt(jnp.mean(xv * xv, axis=-1, keepdims=True) + NORM_EPS)
        xh = xv * rs
        gw = g * w_ref[...]
        dx = rs * (gw - xh * jnp.mean(gw * xh, axis=-1, keepdims=True))
        if has_add:
            dx = dx + add_ref[...]
        dx_ref[...] = dx.astype(dx_ref.dtype)

        @pl.when(pl.program_id(0) == 0)
        def _():
            dw_ref[...] = jnp.zeros_like(dw_ref)

        dw_ref[...] += jnp.sum(g * xh, axis=0, keepdims=True)

    row = pl.BlockSpec((r, d), lambda i: (i, 0))
    vec = pl.BlockSpec((1, d), lambda i: (0, 0))
    ins = [x, w, dh] + ([add] if has_add else [])
    return pl.pallas_call(
        body, name=name, grid=(t // r,),
        in_specs=[row, vec, row] + ([row] if has_add else []),
        out_specs=[row, vec],
        out_shape=[jax.ShapeDtypeStruct((t, d), out_dtype), jax.ShapeDtypeStruct((1, d), F32)],
        compiler_params=_cp(("arbitrary",)),
    )(*ins)


def _loss_head(y, target, name):
    t, d = y.shape
    r = _rows(t)

    def body(y_ref, t_ref, l_ref, g_ref):
        e = y_ref[...] - t_ref[...]
        g_ref[...] = e * (1.0 / d)

        @pl.when(pl.program_id(0) == 0)
        def _():
            l_ref[...] = jnp.zeros_like(l_ref)

        l_ref[...] += jnp.sum(e * e) * (0.5 / d)

    row = pl.BlockSpec((r, d), lambda i: (i, 0))
    return pl.pallas_call(
        body, name=name, grid=(t // r,), in_specs=[row, row],
        out_specs=[pl.BlockSpec((1, 128), lambda i: (0, 0)), row],
        out_shape=[jax.ShapeDtypeStruct((1, 128), F32), jax.ShapeDtypeStruct((t, d), F32)],
        compiler_params=_cp(("arbitrary",)),
    )(y, target)


def _down(x, s):
    return x if s == 0 else pltpu.roll(x, s, 0)


def _up(x, s):
    return x if s == 0 else pltpu.roll(x, x.shape[0] - s, 0)


def _halo(t, r, hh, tc, col):
    q = r // hh
    last = t // hh - 1
    tile = pl.BlockSpec((r, tc), lambda j, i: (i, col(j)))
    prev = pl.BlockSpec((hh, tc), lambda j, i: (jnp.maximum(i * q - 1, 0), col(j)))
    nxt = pl.BlockSpec((hh, tc), lambda j, i: (jnp.minimum((i + 1) * q, last), col(j)))
    return tile, prev, nxt


def _sig(x):
    return 1.0 / (1.0 + jnp.exp(-x))


def _dsilu(x, s):
    return s * (1.0 + x * (1.0 - s))


def _dn_pre_fwd(p, conv_w, name):
    t = p.shape[0]
    r = _rows(t)

    def body(x_ref, xp_ref, w_ref, o_ref):
        j, i = pl.program_id(0), pl.program_id(1)
        xe = jnp.concatenate([jnp.where(i == 0, 0.0, xp_ref[...]), x_ref[...]], axis=0)
        c = sum(_down(xe, DN_CONV - 1 - k) * w_ref[pl.ds(k, 1), :] for k in range(DN_CONV))[8:]
        a = c * _sig(c)
        for h in range(DN_HEADS):
            ah = a[:, h * 128:(h + 1) * 128]
            fac = lax.rsqrt(jnp.sum(ah * ah, axis=-1, keepdims=True) + NORM_EPS)
            o_ref[:, h * 128:(h + 1) * 128] = ah * jnp.where(j == 0, fac * HEAD_DIM ** -0.5, jnp.where(j == 1, fac, 1.0))

    tile, prev, _ = _halo(t, r, 8, DN_W, lambda j: j)
    return pl.pallas_call(
        body, name=name, grid=(3, t // r),
        in_specs=[tile, prev, pl.BlockSpec((DN_CONV, DN_W), lambda j, i: (0, j))],
        out_specs=tile, out_shape=jax.ShapeDtypeStruct((t, 3 * DN_W), F32),
        compiler_params=_cp(("parallel", "parallel")),
    )(p, p, conv_w)


def _dn_pre_bwd(p, conv_w, dqkv, name):
    t = p.shape[0]
    r = _rows(t)
    ni = t // r

    def body(x_ref, xp_ref, xn_ref, w_ref, d_ref, dn_ref, dx_ref, dw_ref):
        j, i = pl.program_id(0), pl.program_id(1)
        xe = jnp.concatenate([jnp.where(i == 0, 0.0, xp_ref[...]), x_ref[...], xn_ref[...]], axis=0)
        de = jnp.concatenate([jnp.zeros((8, DN_W), F32), d_ref[...], jnp.where(i == ni - 1, 0.0, dn_ref[...])], axis=0)
        xs = [_down(xe, DN_CONV - 1 - k) for k in range(DN_CONV)]
        c = sum(xs[k] * w_ref[pl.ds(k, 1), :] for k in range(DN_CONV))
        s = _sig(c)
        a = c * s
        das = []
        for h in range(DN_HEADS):
            ah, dh = a[:, h * 128:(h + 1) * 128], de[:, h * 128:(h + 1) * 128]
            fac = lax.rsqrt(jnp.sum(ah * ah, axis=-1, keepdims=True) + NORM_EPS)
            dnorm = fac * dh - ah * (fac * fac * fac) * jnp.sum(dh * ah, axis=-1, keepdims=True)
            das.append(jnp.where(j == 0, dnorm * HEAD_DIM ** -0.5, jnp.where(j == 1, dnorm, dh)))
        dc = jnp.concatenate(das, axis=1) * _dsilu(c, s)
        dx_ref[...] = sum(_up(dc, DN_CONV - 1 - k) * w_ref[pl.ds(k, 1), :] for k in range(DN_CONV))[8:8 + r]

        @pl.when(i == 0)
        def _():
            dw_ref[...] = jnp.zeros_like(dw_ref)

        for k in range(DN_CONV):
            dw_ref[pl.ds(k, 1), :] += jnp.sum((dc * xs[k])[8:8 + r], axis=0, keepdims=True)

    tile, prev, nxt = _halo(t, r, 8, DN_W, lambda j: j)
    wspec = pl.BlockSpec((DN_CONV, DN_W), lambda j, i: (0, j))
    return pl.pallas_call(
        body, name=name, grid=(3, ni),
        in_specs=[tile, prev, nxt, wspec, tile, nxt],
        out_specs=[tile, wspec],
        out_shape=[jax.ShapeDtypeStruct((t, 3 * DN_W), F32), jax.ShapeDtypeStruct((DN_CONV, 3 * DN_W), F32)],
        compiler_params=_cp(("parallel", "arbitrary")),
    )(p, p, p, conv_w, dqkv, dqkv)


def _ffn_act_fwd(up, cw, cb, name):
    t, f2 = up.shape
    f = f2 // 2
    r = _rows(t)
    tc = _tile(f, (512, 256, 128))
    nj = f // tc

    def body(a_ref, ap_ref, b_ref, bp_ref, wa_ref, wb_ref, ca_ref, cb_ref, o_ref):
        i = pl.program_id(1)

        def conv(x_ref, xp_ref, w_ref, c_ref):
            xe = jnp.concatenate([jnp.where(i == 0, 0.0, xp_ref[...]), x_ref[...]], axis=0)
            return sum(_down(xe, FFN_CONV - 1 - k) * w_ref[pl.ds(k, 1), :] for k in range(FFN_CONV))[8:] + c_ref[...]

        ua = conv(a_ref, ap_ref, wa_ref, ca_ref)
        ub = conv(b_ref, bp_ref, wb_ref, cb_ref)
        o_ref[...] = (ua * _sig(ua) * ub).astype(o_ref.dtype)

    ta, pa, _ = _halo(t, r, 8, tc, lambda j: j)
    tb, pb, _ = _halo(t, r, 8, tc, lambda j: j + nj)
    wa = pl.BlockSpec((FFN_CONV, tc), lambda j, i: (0, j))
    wb = pl.BlockSpec((FFN_CONV, tc), lambda j, i: (0, j + nj))
    ca = pl.BlockSpec((1, tc), lambda j, i: (0, j))
    cbs = pl.BlockSpec((1, tc), lambda j, i: (0, j + nj))
    return pl.pallas_call(
        body, name=name, grid=(nj, t // r),
        in_specs=[ta, pa, tb, pb, wa, wb, ca, cbs], out_specs=ta,
        out_shape=jax.ShapeDtypeStruct((t, f), BF16), compiler_params=_cp(("parallel", "parallel")),
    )(up, up, up, up, cw, cw, cb, cb)


def _ffn_act_bwd(up, cw, cb, dact, name):
    t, f2 = up.shape
    f = f2 // 2
    r = _rows(t)
    ni = t // r
    tc = _tile(f, (512, 256, 128))
    nj = f // tc

    def body(a_ref, ap_ref, an_ref, b_ref, bp_ref, bn_ref, wa_ref, wb_ref, ca_ref, cb_ref, d_ref, dn_ref,
             du_ref, dw_ref, dc_ref):
        i = pl.program_id(1)
        dua_ref, dub_ref, dwa_ref, dwb_ref, dca_ref, dcb_ref = (du_ref.at[0], du_ref.at[1], dw_ref.at[0], dw_ref.at[1],
                                                                  dc_ref.at[0], dc_ref.at[1])

        def ext(x_ref, xp_ref, xn_ref):
            return jnp.concatenate([jnp.where(i == 0, 0.0, xp_ref[...]), x_ref[...], xn_ref[...]], axis=0)

        ae, be = ext(a_ref, ap_ref, an_ref), ext(b_ref, bp_ref, bn_ref)
        as_ = [_down(ae, FFN_CONV - 1 - k) for k in range(FFN_CONV)]
        bs_ = [_down(be, FFN_CONV - 1 - k) for k in range(FFN_CONV)]
        ua = sum(as_[k] * wa_ref[pl.ds(k, 1), :] for k in range(FFN_CONV)) + ca_ref[...]
        ub = sum(bs_[k] * wb_ref[pl.ds(k, 1), :] for k in range(FFN_CONV)) + cb_ref[...]
        de = jnp.concatenate([jnp.zeros((8, tc), F32), d_ref[...].astype(F32),
                              jnp.where(i == ni - 1, 0.0, dn_ref[...].astype(F32))], axis=0)
        s = _sig(ua)
        dua = de * ub * _dsilu(ua, s)
        dub = de * ua * s
        dua_ref[...] = sum(_up(dua, FFN_CONV - 1 - k) * wa_ref[pl.ds(k, 1), :] for k in range(FFN_CONV))[8:8 + r].astype(dua_ref.dtype)
        dub_ref[...] = sum(_up(dub, FFN_CONV - 1 - k) * wb_ref[pl.ds(k, 1), :] for k in range(FFN_CONV))[8:8 + r].astype(dub_ref.dtype)

        @pl.when(i == 0)
        def _():
            dw_ref[...] = jnp.zeros_like(dw_ref)
            dc_ref[...] = jnp.zeros_like(dc_ref)

        for k in range(FFN_CONV):
            dwa_ref[pl.ds(k, 1), :] += jnp.sum((dua * as_[k])[8:8 + r], axis=0, keepdims=True)
            dwb_ref[pl.ds(k, 1), :] += jnp.sum((dub * bs_[k])[8:8 + r], axis=0, keepdims=True)
        dca_ref[...] += jnp.sum(dua[8:8 + r], axis=0, keepdims=True)
        dcb_ref[...] += jnp.sum(dub[8:8 + r], axis=0, keepdims=True)

    ta, pa, na = _halo(t, r, 8, tc, lambda j: j)
    tb, pb, nb = _halo(t, r, 8, tc, lambda j: j + nj)
    wa = pl.BlockSpec((FFN_CONV, tc), lambda j, i: (0, j))
    wb = pl.BlockSpec((FFN_CONV, tc), lambda j, i: (0, j + nj))
    ca = pl.BlockSpec((1, tc), lambda j, i: (0, j))
    cbs = pl.BlockSpec((1, tc), lambda j, i: (0, j + nj))
    return pl.pallas_call(
        body, name=name, grid=(nj, ni),
        in_specs=[ta, pa, na, tb, pb, nb, wa, wb, ca, cbs, ta, na],
        out_specs=[pl.BlockSpec((2, r, tc), lambda j, i: (0, i, j)), pl.BlockSpec((2, FFN_CONV, tc), lambda j, i: (0, 0, j)),
                   pl.BlockSpec((2, 1, tc), lambda j, i: (0, 0, j))],
        out_shape=[jax.ShapeDtypeStruct((2, t, f), BF16), jax.ShapeDtypeStruct((2, FFN_CONV, f), F32),
                   jax.ShapeDtypeStruct((2, 1, f), F32)],
        compiler_params=_cp(("parallel", "arbitrary")),
    )(up, up, up, up, up, up, cw, cw, cb, cb, dact, dact)


def _pool_pick(g, vals):
    return jnp.where(g == 0, vals[0], jnp.where(g == 1, vals[1], jnp.where(g == 2, vals[2], vals[3])))


def _pool_pre(xe, g, t0):
    s1 = xe + _down(xe, 1)
    s2 = s1 + _down(s1, 2)
    s3 = s2 + _down(s2, 4)
    s4 = s3 + _down(s3, 8)
    r = xe.shape[0] - 16
    pos = (t0 + lax.broadcasted_iota(jnp.int32, (r, 1), 0)).astype(F32)
    cnt = jnp.minimum(pos + 1.0, _pool_pick(g, (2.0, 4.0, 8.0, 16.0)))
    return _pool_pick(g, (s1, s2, s3, s4))[16:] / cnt - xe[16:]


def _pool_fwd(p, pool_w, scale, name):
    t = p.shape[0]
    r = _tile(t, (1024, 256, 128))

    def body(x_ref, xp_ref, w_ref, sc_ref, o_ref):
        g, i = pl.program_id(0), pl.program_id(1)
        xe = jnp.concatenate([jnp.where(i == 0, 0.0, xp_ref[...]), x_ref[...]], axis=0)
        pre = _pool_pre(xe, g, i * r)
        o_ref[...] = jnp.dot(pre, w_ref[0], preferred_element_type=F32) * sc_ref[...]

    tile, prev, _ = _halo(t, r, 16, 128, lambda j: CB_POOL + j)
    return pl.pallas_call(
        body, name=name, grid=(POOL_GROUPS, t // r),
        in_specs=[tile, prev, pl.BlockSpec((1, 128, 128), lambda j, i: (j, 0, 0)), pl.BlockSpec((1, 128), lambda j, i: (0, j))],
        out_specs=pl.BlockSpec((r, 128), lambda j, i: (i, j)),
        out_shape=jax.ShapeDtypeStruct((t, POOL_W), F32), compiler_params=_cp(("parallel", "parallel")),
    )(p, p, pool_w, scale)


def _pool_bwd(p, pool_w, scale, dycat, name):
    t = p.shape[0]
    r = _tile(t, (1024, 256, 128))
    ni = t // r

    def body(x_ref, xp_ref, w_ref, sc_ref, d_ref, dn_ref, dx_ref, dw_ref, dsc_ref):
        g, i = pl.program_id(0), pl.program_id(1)
        xe = jnp.concatenate([jnp.where(i == 0, 0.0, xp_ref[...]), x_ref[...]], axis=0)
        pre = _pool_pre(xe, g, i * r)
        w = w_ref[0]
        dy = d_ref[...]
        dye = jnp.concatenate([dy, jnp.where(i == ni - 1, 0.0, dn_ref[...])], axis=0)
        dpre = lax.dot_general(dye * sc_ref[...], w, _DN["nt"], preferred_element_type=F32)
        pos = (i * r + lax.broadcasted_iota(jnp.int32, (r + 16, 1), 0)).astype(F32)
        dm = dpre / jnp.minimum(pos + 1.0, _pool_pick(g, (2.0, 4.0, 8.0, 16.0)))
        a1 = dm + _up(dm, 1)
        a2 = a1 + _up(a1, 2)
        a3 = a2 + _up(a2, 4)
        a4 = a3 + _up(a3, 8)
        dx_ref[...] = (_pool_pick(g, (a1, a2, a3, a4)) - dpre)[:r]

        @pl.when(i == 0)
        def _():
            dw_ref[...] = jnp.zeros_like(dw_ref)
            dsc_ref[...] = jnp.zeros_like(dsc_ref)

        dw_ref[0] += lax.dot_general(pre, dy * sc_ref[...], _DN["tn"], preferred_element_type=F32)
        dsc_ref[...] += jnp.sum(dy * jnp.dot(pre, w, preferred_element_type=F32), axis=0, keepdims=True)

    tile, prev, _ = _halo(t, r, 16, 128, lambda j: CB_POOL + j)
    dtile, _, dnxt = _halo(t, r, 16, 128, lambda j: DN_W // 128 + j)
    wspec = pl.BlockSpec((1, 128, 128), lambda j, i: (j, 0, 0))
    sspec = pl.BlockSpec((1, 128), lambda j, i: (0, j))
    return pl.pallas_call(
        body, name=name, grid=(POOL_GROUPS, ni),
        in_specs=[tile, prev, wspec, sspec, dtile, dnxt],
        out_specs=[pl.BlockSpec((r, 128), lambda j, i: (i, j)), wspec, sspec],
        out_shape=[jax.ShapeDtypeStruct((t, POOL_W), F32), jax.ShapeDtypeStruct((POOL_GROUPS, 128, 128), F32),
                   jax.ShapeDtypeStruct((1, POOL_W), F32)],
        compiler_params=_cp(("parallel", "arbitrary")),
    )(p, p, pool_w, scale, dycat, dycat)


_DNB = {"nn": (((2,), (1,)), ((0,), (0,))), "nt": (((2,), (2,)), ((0,), (0,))), "tn": (((1,), (1,)), ((0,), (0,)))}


def _dot(a, b, mode="nn", precision=None):
    dn = _DNB[mode] if a.ndim == 3 else _DN[mode]
    return lax.dot_general(a, b, dn, precision=precision, preferred_element_type=F32)


@functools.partial(jax.custom_vjp, nondiff_argnums=(2,))
def _bdot(a, b, mode):
    return _dot(a.astype(BF16), b.astype(BF16), mode)


def _bdot_fwd(a, b, mode):
    return _bdot(a, b, mode), (a, b)


def _bdot_bwd(mode, res, g):
    a, b = res
    if mode == "nn":
        return _bdot(g, b, "nt"), _bdot(a, g, "tn")
    if mode == "nt":
        return _bdot(g, b, "nn"), _bdot(g, a, "tn")
    return _bdot(b, g, "nt"), _bdot(a, g, "nn")


_bdot.defvjp(_bdot_fwd, _bdot_bwd)


def _dn_consts():
    c = DN_CHUNK
    ii = lax.broadcasted_iota(jnp.int32, (c, c), 0)
    jj = lax.broadcasted_iota(jnp.int32, (c, c), 1)
    one, zero = jnp.ones((c, c), F32), jnp.zeros((c, c), F32)
    return dict(ltri=jnp.where(ii >= jj, one, zero), utri=jnp.where(ii <= jj, one, zero), ones=one,
                causal=ii >= jj, strict=ii > jj, eye=jnp.where(ii == jj, one, zero))


def _dn_chunk(q, k, v, z, bcol, acol, s_in, alog, dtb, nw, cs):
    c = DN_CHUNK
    hh = q.shape[0]
    per_head = lambda m: jnp.broadcast_to(m, (hh, c, c))
    beta = _sig(bcol)
    xa = acol + dtb
    g = -jnp.exp(alog) * (jnp.maximum(xa, 0.0) + jnp.log(1.0 + jnp.exp(-jnp.abs(xa))))
    gb = jnp.broadcast_to(g, (hh, c, HEAD_DIM))
    gbc = jnp.broadcast_to(g, (hh, c, c))
    gc = _dot(per_head(cs["ltri"]), gb, precision=HIGH)
    gcol = _dot(per_head(cs["ltri"]), gbc, precision=HIGH)
    grow = _dot(per_head(cs["ones"]), gbc * cs["utri"], precision=HIGH)
    decay = jnp.exp(jnp.where(cs["causal"], gcol - grow, -1e30))
    kb = k * beta
    vb = v * beta
    nil = -jnp.where(cs["strict"], _bdot(kb, k, "nt") * decay, 0.0)
    inv = cs["eye"] + nil
    powk = nil
    for _ in range(int(math.log2(c)) - 1):
        powk = _dot(powk, powk)
        inv = _dot(inv, cs["eye"] + powk)
    eg = jnp.exp(gc)
    u = _bdot(inv, vb, "nn")
    w = _bdot(inv, kb * eg, "nn")
    a = _bdot(q, k, "nt") * decay
    v_new = u - _bdot(w, s_in, "nn")
    o = _bdot(q * eg, s_in, "nn") + _bdot(a, v_new, "nn")
    glast = jnp.sum(gb, axis=1, keepdims=True)
    s_out = s_in * jnp.exp(glast) + _bdot(k * jnp.exp(glast - gc), v_new, "tn")
    on = o * lax.rsqrt(jnp.mean(o * o, axis=-1, keepdims=True) + NORM_EPS) * nw
    return on * (z * _sig(z)), s_out


def _lane_pick(x, lane, idx):
    return jnp.sum(jnp.where(lane == idx, x, 0.0), axis=1, keepdims=True)


def _dn_load(q_ref, k_ref, v_ref, z_ref, bd_ref, al_ref, dt_ref, nw_ref, s_in):
    lane = lax.broadcasted_iota(jnp.int32, (1, 128), 1)
    bd, al, dt = bd_ref[...], al_ref[...], dt_ref[...]
    heads = range(DN_HEADS)
    wide = lambda ref: jnp.stack([ref[:, h * 128:(h + 1) * 128] for h in heads], axis=0)
    col = lambda x, off: jnp.stack([_lane_pick(x, lane, off + h) for h in heads], axis=0)
    return (wide(q_ref), wide(k_ref), wide(v_ref), wide(z_ref), col(bd, 0), col(bd, DN_HEADS), s_in,
            col(al, 0), col(dt, 0), nw_ref[...])


def _dn_fwd(qkv, p, alog, dtb, nw, name):
    t = qkv.shape[0]
    c = DN_CHUNK
    n = t // c

    def body(q_ref, k_ref, v_ref, z_ref, bd_ref, al_ref, dt_ref, nw_ref, y_ref, ss_ref, s_scr):
        @pl.when(pl.program_id(0) == 0)
        def _():
            s_scr[...] = jnp.zeros_like(s_scr)

        s_in = s_scr[...]
        y, s_out = _dn_chunk(*_dn_load(q_ref, k_ref, v_ref, z_ref, bd_ref, al_ref, dt_ref, nw_ref, s_in), _dn_consts())
        ss_ref[0] = s_in
        s_scr[...] = s_out
        for h in range(DN_HEADS):
            y_ref[:, h * 128:(h + 1) * 128] = y[h]

    wide = lambda j: pl.BlockSpec((c, DN_W), lambda i: (i, j))
    vec = pl.BlockSpec((1, 128), lambda i: (0, 0))
    return pl.pallas_call(
        body, name=name, grid=(n,),
        in_specs=[wide(0), wide(1), wide(2), wide(3), pl.BlockSpec((c, 128), lambda i: (i, CB_BD)), vec, vec, vec],
        out_specs=[wide(0), pl.BlockSpec((1, DN_HEADS, 128, 128), lambda i: (i, 0, 0, 0))],
        out_shape=[jax.ShapeDtypeStruct((t, DN_W), F32), jax.ShapeDtypeStruct((n, DN_HEADS, 128, 128), F32)],
        scratch_shapes=[pltpu.VMEM((DN_HEADS, 128, 128), F32)],
        compiler_params=_cp(("arbitrary",)),
    )(qkv, qkv, qkv, p, p, alog, dtb, nw)


def _dn_bwd(qkv, p, alog, dtb, nw, states, dycat, name):
    t = qkv.shape[0]
    c = DN_CHUNK
    n = t // c

    def body(q_ref, k_ref, v_ref, z_ref, bd_ref, al_ref, dt_ref, nw_ref, ss_ref, dy_ref,
             dqkv_ref, dz_ref, dbd_ref, dal_ref, ddt_ref, dnw_ref, ds_scr):
        @pl.when(pl.program_id(0) == 0)
        def _():
            ds_scr[...] = jnp.zeros_like(ds_scr)
            dal_ref[...] = jnp.zeros_like(dal_ref)
            ddt_ref[...] = jnp.zeros_like(ddt_ref)
            dnw_ref[...] = jnp.zeros_like(dnw_ref)

        lane = lax.broadcasted_iota(jnp.int32, (1, 128), 1)
        args = _dn_load(q_ref, k_ref, v_ref, z_ref, bd_ref, al_ref, dt_ref, nw_ref, ss_ref[0])
        dy = jnp.stack([dy_ref[:, h * 128:(h + 1) * 128] for h in range(DN_HEADS)], axis=0)
        _, vjp = jax.vjp(functools.partial(_dn_chunk, cs=_dn_consts()), *args)
        gq, gk, gv, gz, gb, ga, gs, gal, gdt, gnw = vjp((dy, ds_scr[...]))
        ds_scr[...] = gs
        dbd = jnp.zeros((c, 128), F32)
        dal = jnp.zeros((1, 128), F32)
        ddt = jnp.zeros((1, 128), F32)
        for h in range(DN_HEADS):
            sl = slice(h * 128, (h + 1) * 128)
            dqkv_ref[:, sl] = gq[h]
            dqkv_ref[:, DN_W + h * 128:DN_W + (h + 1) * 128] = gk[h]
            dqkv_ref[:, 2 * DN_W + h * 128:2 * DN_W + (h + 1) * 128] = gv[h]
            dz_ref[:, sl] = gz[h]
            dbd = dbd + jnp.where(lane == h, gb[h], 0.0) + jnp.where(lane == DN_HEADS + h, ga[h], 0.0)
            dal = dal + jnp.where(lane == h, gal[h], 0.0)
            ddt = ddt + jnp.where(lane == h, gdt[h], 0.0)
        dbd_ref[...] = dbd
        dal_ref[...] += dal
        ddt_ref[...] += ddt
        dnw_ref[...] += gnw

    rev = lambda i: n - 1 - i
    wide = lambda j: pl.BlockSpec((c, DN_W), lambda i: (rev(i), j))
    vec = pl.BlockSpec((1, 128), lambda i: (0, 0))
    return pl.pallas_call(
        body, name=name, grid=(n,),
        in_specs=[wide(0), wide(1), wide(2), wide(3), pl.BlockSpec((c, 128), lambda i: (rev(i), CB_BD)), vec, vec, vec,
                  pl.BlockSpec((1, DN_HEADS, 128, 128), lambda i: (rev(i), 0, 0, 0)), wide(0)],
        out_specs=[pl.BlockSpec((c, 3 * DN_W), lambda i: (rev(i), 0)), wide(0),
                   pl.BlockSpec((c, 128), lambda i: (rev(i), 0)), vec, vec, vec],
        out_shape=[jax.ShapeDtypeStruct((t, 3 * DN_W), F32), jax.ShapeDtypeStruct((t, DN_W), F32),
                   jax.ShapeDtypeStruct((t, 128), F32), jax.ShapeDtypeStruct((1, 128), F32),
                   jax.ShapeDtypeStruct((1, 128), F32), jax.ShapeDtypeStruct((1, 128), F32)],
        scratch_shapes=[pltpu.VMEM((DN_HEADS, 128, 128), F32)],
        compiler_params=_cp(("arbitrary",)),
    )(qkv, qkv, qkv, p, p, alog, dtb, nw, states, dycat)


def _rope(x, cosf, sins):
    return x * cosf + pltpu.roll(x, HEAD_DIM // 2, 1) * sins


def _rope_t(d, cosf, sins):
    return d * cosf + pltpu.roll(d * sins, HEAD_DIM // 2, 1)


def _swa_masks():
    b = SWA_BLOCK
    i = lax.broadcasted_iota(jnp.int32, (SWA_GROUP * b, b), 0) & (b - 1)
    j = lax.broadcasted_iota(jnp.int32, (SWA_GROUP * b, b), 1)
    return j > i, j <= i


def _swa_sink_col(sinks_ref, h):
    b = SWA_BLOCK
    r = lax.broadcasted_iota(jnp.int32, (SWA_GROUP * b, 1), 0)
    s = [sinks_ref[h * SWA_GROUP + g] for g in range(SWA_GROUP)]
    return jnp.where(r < b, s[0], jnp.where(r < 2 * b, s[1], s[2]))


def _swa_specs(t, h_first):
    nb = t // SWA_BLOCK

    def at(col, off):
        def imap(h, n):
            return (jnp.clip(n + off, 0, nb - 1), col(h))
        return pl.BlockSpec((SWA_BLOCK, 128), imap)
    return at


def _swa_fwd(p, cosf, sins, sinks, name):
    t = p.shape[0]
    b = SWA_BLOCK
    nb = t // b
    at = _swa_specs(t, None)
    scale = HEAD_DIM ** -0.5

    def body(q0, q1, q2, kp, kc, vp, vc, cc, sc, cp, sp, sinks_ref, o_ref, lse_ref):
        h, n = pl.program_id(0), pl.program_id(1)
        qs = jnp.concatenate([_rope(q[...], cc[...], sc[...]) for q in (q0, q1, q2)], axis=0)
        ks = jnp.concatenate([_rope(kp[...], cp[...], sp[...]), _rope(kc[...], cc[...], sc[...])], axis=0)
        vs = jnp.concatenate([vp[...], vc[...]], axis=0)
        mp, mc = _swa_masks()
        mask = jnp.concatenate([mp & (n > 0), mc], axis=1)
        s = jnp.where(mask, _dot(qs, ks, "nt") * scale, -1e30)
        sink = _swa_sink_col(sinks_ref, h)
        m = jnp.maximum(jnp.max(s, axis=1, keepdims=True), sink)
        e = jnp.exp(s - m)
        l = jnp.sum(e, axis=1, keepdims=True) + jnp.exp(sink - m)
        o = _dot(e, vs) / l
        lse = m + jnp.log(l)
        lane = lax.broadcasted_iota(jnp.int32, (1, 128), 1)
        tile = jnp.zeros((b, 128), F32)
        for g in range(SWA_GROUP):
            o_ref[:, g * 128:(g + 1) * 128] = o[g * b:(g + 1) * b]
            tile = tile + jnp.where(lane == g, lse[g * b:(g + 1) * b], 0.0)
        lse_ref[0] = tile

    qcol = lambda g: (lambda h: CB_SQ + h * SWA_GROUP + g)
    kcol, vcol, one = (lambda h: CB_SK + h), (lambda h: CB_SV + h), (lambda h: 0)
    in_specs = [at(qcol(0), 0), at(qcol(1), 0), at(qcol(2), 0), at(kcol, -1), at(kcol, 0), at(vcol, -1), at(vcol, 0),
                at(one, 0), at(one, 0), at(one, -1), at(one, -1), pl.BlockSpec(memory_space=pltpu.SMEM)]
    return pl.pallas_call(
        body, name=name, grid=(SWA_KV_HEADS, nb), in_specs=in_specs,
        out_specs=[pl.BlockSpec((b, SWA_GROUP * 128), lambda h, n: (n, h)), pl.BlockSpec((1, b, 128), lambda h, n: (h, n, 0))],
        out_shape=[jax.ShapeDtypeStruct((t, SWA_W), F32), jax.ShapeDtypeStruct((SWA_KV_HEADS, t, 128), F32)],
        compiler_params=_cp(("parallel", "parallel")),
    )(p, p, p, p, p, p, p, cosf, sins, cosf, sins, sinks)


def _swa_bwd(p, cosf, sins, sinks, o, lse, dycat, name):
    t = p.shape[0]
    b = SWA_BLOCK
    nb = t // b
    at = _swa_specs(t, None)
    scale = HEAD_DIM ** -0.5
    gb = SWA_GROUP * b

    def body(q0, q1, q2, r0, r1, r2, kp, kc, vp, vc, cc, sc, cp, sp, cn, sn, d0, d1, d2, e0, e1, e2,
             oc_ref, on_ref, lc_ref, ln_ref, sinks_ref, dq_ref, dk_ref, dv_ref, dsk_ref):
        h, n = pl.program_id(0), pl.program_id(1)
        lane = lax.broadcasted_iota(jnp.int32, (1, 128), 1)
        stack = lambda refs: jnp.concatenate([x[...] for x in refs], axis=0)
        q_c = jnp.concatenate([_rope(q[...], cc[...], sc[...]) for q in (q0, q1, q2)], axis=0)
        q_n = jnp.concatenate([_rope(q[...], cn[...], sn[...]) for q in (r0, r1, r2)], axis=0)
        k_p = _rope(kp[...], cp[...], sp[...])
        k_c = _rope(kc[...], cc[...], sc[...])
        do_c, do_n = stack((d0, d1, d2)), stack((e0, e1, e2))
        o_c = jnp.concatenate([oc_ref[:, g * 128:(g + 1) * 128] for g in range(SWA_GROUP)], axis=0)
        o_n = jnp.concatenate([on_ref[:, g * 128:(g + 1) * 128] for g in range(SWA_GROUP)], axis=0)
        lse_c = jnp.concatenate([_lane_pick(lc_ref[0], lane, g) for g in range(SWA_GROUP)], axis=0)
        lse_n = jnp.concatenate([_lane_pick(ln_ref[0], lane, g) for g in range(SWA_GROUP)], axis=0)
        dl_c = jnp.sum(do_c * o_c, axis=1, keepdims=True)
        dl_n = jnp.sum(do_n * o_n, axis=1, keepdims=True)
        mp, mc = _swa_masks()

        def pair(qr, kr, v, do, lse_, dl, mask):
            s = _dot(qr, kr, "nt") * scale
            pr = jnp.where(mask, jnp.exp(s - lse_), 0.0)
            ds = pr * (_dot(do, v, "nt") - dl) * scale
            return _dot(ds, kr), _dot(ds, qr, "tn"), _dot(pr, do, "tn")

        dq_a, _, _ = pair(q_c, k_p, vp[...], do_c, lse_c, dl_c, mp & (n > 0))
        dq_b, dk_b, dv_b = pair(q_c, k_c, vc[...], do_c, lse_c, dl_c, mc)
        _, dk_n, dv_n = pair(q_n, k_c, vc[...], do_n, lse_n, dl_n, mp & (n < nb - 1))
        dq = dq_a + dq_b
        for g in range(SWA_GROUP):
            dq_ref[:, g * 128:(g + 1) * 128] = _rope_t(dq[g * b:(g + 1) * b], cc[...], sc[...])
        dk_ref[...] = _rope_t(dk_b + dk_n, cc[...], sc[...])
        dv_ref[...] = dv_b + dv_n

        @pl.when(n == 0)
        def _():
            dsk_ref[...] = jnp.zeros_like(dsk_ref)

        w = -jnp.exp(_swa_sink_col(sinks_ref, h) - lse_c) * dl_c
        acc = jnp.zeros((1, 128), F32)
        for g in range(SWA_GROUP):
            acc = acc + jnp.where(lane == g, jnp.sum(w[g * b:(g + 1) * b], axis=0, keepdims=True), 0.0)
        dsk_ref[0] += jnp.broadcast_to(acc, (8, 128))

    qcol = lambda g: (lambda h: CB_SQ + h * SWA_GROUP + g)
    dcol = lambda g: (lambda h: (DN_W + POOL_W) // 128 + h * SWA_GROUP + g)
    kcol, vcol, one = (lambda h: CB_SK + h), (lambda h: CB_SV + h), (lambda h: 0)
    wide = lambda off: pl.BlockSpec((b, SWA_GROUP * 128), lambda h, n: (jnp.clip(n + off, 0, nb - 1), h))
    lspec = lambda off: pl.BlockSpec((1, b, 128), lambda h, n: (h, jnp.clip(n + off, 0, nb - 1), 0))
    in_specs = ([at(qcol(g), 0) for g in range(3)] + [at(qcol(g), 1) for g in range(3)]
                + [at(kcol, -1), at(kcol, 0), at(vcol, -1), at(vcol, 0)]
                + [at(one, 0), at(one, 0), at(one, -1), at(one, -1), at(one, 1), at(one, 1)]
                + [at(dcol(g), 0) for g in range(3)] + [at(dcol(g), 1) for g in range(3)]
                + [wide(0), wide(1), lspec(0), lspec(1), pl.BlockSpec(memory_space=pltpu.SMEM)])
    kv_out = pl.BlockSpec((b, 128), lambda h, n: (n, h))
    return pl.pallas_call(
        body, name=name, grid=(SWA_KV_HEADS, nb), in_specs=in_specs,
        out_specs=[wide(0), kv_out, kv_out, pl.BlockSpec((1, 8, 128), lambda h, n: (h, 0, 0))],
        out_shape=[jax.ShapeDtypeStruct((t, SWA_W), F32), jax.ShapeDtypeStruct((t, SWA_KV_W), F32),
                   jax.ShapeDtypeStruct((t, SWA_KV_W), F32), jax.ShapeDtypeStruct((SWA_KV_HEADS, 8, 128), F32)],
        compiler_params=_cp(("parallel", "arbitrary")),
    )(*([p] * 10), cosf, sins, cosf, sins, cosf, sins, *([dycat] * 6), o, o, lse, lse, sinks)


def _adam_math(w, g, m, v):
    m = ADAM_B1 * m + (1.0 - ADAM_B1) * g
    v = ADAM_B2 * v + (1.0 - ADAM_B2) * (g * g)
    m_hat = m / (1.0 - ADAM_B1 ** ADAM_STEP)
    v_hat = v / (1.0 - ADAM_B2 ** ADAM_STEP)
    return -ADAM_LR * (m_hat / (jnp.sqrt(v_hat) + ADAM_EPS) + ADAM_WD * w), m, v


def _adamw(w, g, m, v, name):
    shape = w.shape
    cols = shape[-1]
    rows = math.prod(shape[:-1])
    flat = lambda a: a.reshape(rows, cols)
    r = rows
    for cand in (512, 256, 128, 64, 32, 16, 8):
        if rows % cand == 0 and cand * cols * 4 <= (1 << 20):
            r = cand
            break

    def body(w_ref, g_ref, m_ref, v_ref, d_ref, nm_ref, nv_ref):
        d_ref[...], nm_ref[...], nv_ref[...] = _adam_math(w_ref[...], g_ref[...], m_ref[...], v_ref[...])

    spec = pl.BlockSpec((r, cols), lambda i: (i, 0))
    outs = pl.pallas_call(
        body, name=name, grid=(rows // r,), in_specs=[spec] * 4, out_specs=[spec] * 3,
        out_shape=[jax.ShapeDtypeStruct((rows, cols), F32)] * 3, compiler_params=_cp(("parallel",)),
    )(flat(w), flat(g), flat(m), flat(v))
    return tuple(o.reshape(shape) for o in outs)


BIG = ("w_in", "w_out", "ffn_w_up", "ffn_w_down")
CONV = ("dn_conv_w", "ffn_conv_w")
KIND = {"w_in": "col", "w_out": "row", "ffn_w_up": "col", "ffn_w_down": "row"}
SMALL = ("norm_mix_pre", "dn_a_log", "dn_dt_bias", "dn_norm_w", "pool_w", "pool_scale", "swa_sinks",
         "norm_mix_post", "norm_ffn_pre", "ffn_conv_b", "norm_ffn_post")
WEIGHTS = ("norm_mix_pre", "w_in", "dn_conv_w", "dn_a_log", "dn_dt_bias", "dn_norm_w", "pool_w", "pool_scale",
           "swa_sinks", "w_out", "norm_mix_post", "norm_ffn_pre", "ffn_w_up", "ffn_conv_w", "ffn_conv_b",
           "ffn_w_down", "norm_ffn_post")


def _pad_in(w):
    z = lambda n: jnp.zeros(w.shape[:-1] + (n,), w.dtype)
    return jnp.concatenate([w[..., :GATE_END], z(CB_POOL * 128 - GATE_END), w[..., GATE_END:],
                            z(IN_PAD - CB_POOL * 128 - (IN_TRUE - GATE_END))], axis=-1)


def _unpad_in(g):
    return jnp.concatenate([g[..., :GATE_END], g[..., CB_POOL * 128:CB_POOL * 128 + IN_TRUE - GATE_END]], axis=-1)


def _lanes(v):
    return jnp.zeros((1, 128), F32).at[0, :v.shape[0]].set(v)


def _rope_tables(positions):
    inv_freq = 1.0 / (ROPE_THETA ** (jnp.arange(0, HEAD_DIM, 2, dtype=F32) / HEAD_DIM))
    ang = positions.astype(F32)[:, None] * inv_freq
    cos, sin = jnp.cos(ang), jnp.sin(ang)
    return jnp.concatenate([cos, cos], axis=-1), jnp.concatenate([-sin, sin], axis=-1)


def _local_step(x, positions, target, w):
    depth = w["w_out"].shape[0]
    t = x.shape[0]
    cosf, sins = _rope_tables(positions)
    saved = []
    for l in range(depth):
        nm = f"l{l}_"
        n1, n2, n3, n4 = (w[k][l][None] for k in ("norm_mix_pre", "norm_mix_post", "norm_ffn_pre", "norm_ffn_post"))
        alog, dtb, dnw = _lanes(w["dn_a_log"][l]), _lanes(w["dn_dt_bias"][l]), w["dn_norm_w"][l][None]
        psc, cb = w["pool_scale"][l][None], w["ffn_conv_b"][l][None]
        h = _norm_fwd(x, n1, nm + "norm1")
        p = _mm(h, w["w_in"], "nn", F32, nm + "in_proj", b_pick=("slab", l))
        qkv = _dn_pre_fwd(p, w["dn_conv_w"][l], nm + "dn_pre")
        y_dn, st = _dn_fwd(qkv, p, alog, dtb, dnw, nm + "dn")
        y_pool = _pool_fwd(p, w["pool_w"][l], psc, nm + "pool")
        y_swa, lse = _swa_fwd(p, cosf, sins, w["swa_sinks"][l], nm + "swa")
        ycat = jnp.concatenate([y_dn, y_pool, y_swa], axis=1).astype(BF16)
        mix = _mm(ycat, w["w_out"], "nn", F32, nm + "out_proj", b_pick=("slab", l))
        x1 = _resnorm_fwd(x, mix, n2, nm + "res1")
        h2 = _norm_fwd(x1, n3, nm + "norm3")
        up = _mm(h2, w["ffn_w_up"], "nn", F32, nm + "ffn_up", b_pick=("slab", l))
        act = _ffn_act_fwd(up, w["ffn_conv_w"][l], cb, nm + "ffn_act")
        f = _mm(act, w["ffn_w_down"], "nn", F32, nm + "ffn_down", b_pick=("slab", l))
        x2 = _resnorm_fwd(x1, f, n4, nm + "res2")
        saved.append(dict(x=x, h=h, p=p, qkv=qkv, st=st, y_swa=y_swa, lse=lse, ycat=ycat, mix=mix, x1=x1, h2=h2,
                          up=up, act=act, f=f, n=(n1, n2, n3, n4), alog=alog, dtb=dtb, dnw=dnw, psc=psc, cb=cb))
        x = x2
    loss, dx = _loss_head(x, target, "loss_head")
    grads = {k: [None] * depth for k in WEIGHTS}
    for k in BIG:
        grads[k] = lax.empty(w[k].shape, BF16)
    for l in reversed(range(depth)):
        nm, s = f"l{l}_b_", saved[l]
        n1, n2, n3, n4 = s["n"]
        df, g4 = _norm_bwd(s["f"], n4, dx, None, BF16, nm + "res2")
        dact = _mm(df, w["ffn_w_down"], "nt", F32, nm + "ffn_down_dx", b_pick=("slab", l))
        grads["ffn_w_down"] = _mm(s["act"], df, "tn", BF16, nm + "ffn_down_dw", into=(grads["ffn_w_down"], l))
        dup, dcw, dcb = _ffn_act_bwd(s["up"], w["ffn_conv_w"][l], s["cb"], dact, nm + "ffn_act")
        grads["ffn_conv_w"][l] = jnp.concatenate([dcw[0], dcw[1]], axis=1)
        grads["ffn_conv_b"][l] = jnp.concatenate([dcb[0], dcb[1]], axis=1)[0]
        grads["ffn_w_up"] = _mm(s["h2"], dup, "tn", BF16, nm + "ffn_up_dw", b_pick="split", into=(grads["ffn_w_up"], l))
        dh2 = _mm(dup, w["ffn_w_up"], "nt", BF16, nm + "ffn_up_dx", a_pick="split", b_pick=("slab", l))
        dx1, g3 = _norm_bwd(s["x1"], n3, dh2, dx, F32, nm + "norm3")
        dmix, g2 = _norm_bwd(s["mix"], n2, dx1, None, BF16, nm + "res1")
        grads["w_out"] = _mm(s["ycat"], dmix, "tn", BF16, nm + "out_proj_dw", into=(grads["w_out"], l))
        dycat = _mm(dmix, w["w_out"], "nt", F32, nm + "out_proj_dx", b_pick=("slab", l))
        dqkv, dz, dbd, gal, gdt, gnw = _dn_bwd(s["qkv"], s["p"], s["alog"], s["dtb"], s["dnw"], s["st"], dycat, nm + "dn")
        dpq, gconv = _dn_pre_bwd(s["p"], w["dn_conv_w"][l], dqkv, nm + "dn_pre")
        dpool, gpw, gpsc = _pool_bwd(s["p"], w["pool_w"][l], s["psc"], dycat, nm + "pool")
        dsq, dsk, dsv, gsk = _swa_bwd(s["p"], cosf, sins, w["swa_sinks"][l], s["y_swa"], s["lse"], dycat, nm + "swa")
        dp = jnp.concatenate([dpq, dz, dbd, dpool, dsq, dsk, dsv, jnp.zeros((t, 128), F32)], axis=1).astype(BF16)
        grads["w_in"] = _mm(s["h"], dp, "tn", BF16, nm + "in_proj_dw", into=(grads["w_in"], l))
        dh = _mm(dp, w["w_in"], "nt", BF16, nm + "in_proj_dx", b_pick=("slab", l))
        dx, g1 = _norm_bwd(s["x"], n1, dh, dx1, F32, nm + "norm1")
        grads["norm_mix_pre"][l], grads["norm_mix_post"][l] = g1[0], g2[0]
        grads["norm_ffn_pre"][l], grads["norm_ffn_post"][l] = g3[0], g4[0]
        grads["dn_conv_w"][l] = gconv
        grads["dn_a_log"][l], grads["dn_dt_bias"][l], grads["dn_norm_w"][l] = gal[0, :DN_HEADS], gdt[0, :DN_HEADS], gnw[0]
        grads["pool_w"][l], grads["pool_scale"][l] = gpw, gpsc[0]
        grads["swa_sinks"][l] = gsk[:, 0, :SWA_GROUP].reshape(SWA_HEADS)
    return loss, dx, grads


def _flat2(a):
    return a.reshape(math.prod(a.shape[:-1]), a.shape[-1])


def _ew_rows(rows, cols, n_arrays):
    for cand in (512, 256, 128, 64, 32, 16):
        if rows % cand == 0 and cand * cols * 4 * n_arrays <= (8 << 20):
            return cand
    return rows


def _spread_shard(a, kind, place, dtype, name):
    depth, rows, cols = a.shape
    r = _ew_rows(rows, cols, 2)
    nb = rows // r

    def body(s_ref, a_ref, o_ref):
        o_ref[...] = a_ref[...].astype(o_ref.dtype)

    if kind == "row":
        out_spec = pl.BlockSpec((None, r, cols), lambda l, i, s: (l, s[0] * nb + i, 0))
        out_shape = (depth, 4 * rows, cols)
    else:
        out_spec = pl.BlockSpec((None, r, cols), lambda l, i, s: (l, i, s[0]))
        out_shape = (depth, rows, 4 * cols)
    return pl.pallas_call(
        body, name=name,
        grid_spec=pltpu.PrefetchScalarGridSpec(
            num_scalar_prefetch=1, grid=(depth, nb),
            in_specs=[pl.BlockSpec((None, r, cols), lambda l, i, s: (l, i, 0))], out_specs=out_spec),
        out_shape=jax.ShapeDtypeStruct(out_shape, dtype), compiler_params=_cp(("parallel", "parallel")),
    )(place, a)


def _chip_sum(mine, sib, place, name):
    l2, rows, cols = sib.shape
    r = _ew_rows(rows, cols, 3)

    def body(s_ref, a_ref, b_ref, o_ref):
        o_ref[...] = (a_ref[...].astype(F32) + b_ref[...].astype(F32)).astype(o_ref.dtype)

    spec = pl.BlockSpec((None, r, cols), lambda l, i, s: (l, i, 0))
    return pl.pallas_call(
        body, name=name,
        grid_spec=pltpu.PrefetchScalarGridSpec(
            num_scalar_prefetch=1, grid=(l2, rows // r),
            in_specs=[pl.BlockSpec((None, r, cols), lambda l, i, s: (s[1] * l2 + l, i, 0)), spec], out_specs=spec),
        out_shape=jax.ShapeDtypeStruct(sib.shape, BF16), compiler_params=_cp(("parallel", "parallel")),
    )(place, mine, sib)


def _sum_slots(a, name):
    s = a.shape[0]
    a3 = a.reshape(s, math.prod(a.shape[1:-1]), a.shape[-1])
    _, rows, cols = a3.shape
    r = _ew_rows(rows, cols, s + 1)

    def body(a_ref, o_ref):
        acc = a_ref[0].astype(F32)
        for k in range(1, s):
            acc = acc + a_ref[k].astype(F32)
        o_ref[...] = acc

    return pl.pallas_call(body, name=name, grid=(rows // r,),
                          in_specs=[pl.BlockSpec((s, r, cols), lambda i: (0, i, 0))],
                          out_specs=pl.BlockSpec((r, cols), lambda i: (i, 0)),
                          out_shape=jax.ShapeDtypeStruct((rows, cols), F32), compiler_params=_cp(("parallel",)),
                          )(a3).reshape(a.shape[1:])


def _owner_sum(own, got, kind, place, name):
    _, l2, rows, cols = got.shape
    r = _ew_rows(rows, cols, 6)
    nb = rows // r

    def body(s_ref, own_ref, got_ref, o_ref):
        acc = own_ref[...].astype(F32)
        for k in range(3):
            acc = acc + got_ref[k].astype(F32)
        o_ref[...] = acc

    if kind == "row":
        own_spec = pl.BlockSpec((None, r, cols), lambda l, i, s: (l, s[0] * nb + i, 0))
    else:
        own_spec = pl.BlockSpec((None, r, cols), lambda l, i, s: (l, i, s[0]))
    return pl.pallas_call(
        body, name=name,
        grid_spec=pltpu.PrefetchScalarGridSpec(
            num_scalar_prefetch=1, grid=(l2, nb),
            in_specs=[own_spec, pl.BlockSpec((3, None, r, cols), lambda l, i, s: (0, l, i, 0))],
            out_specs=pl.BlockSpec((None, r, cols), lambda l, i, s: (s[1] * l2 + l, i, 0))),
        out_shape=jax.ShapeDtypeStruct((2 * l2, rows, cols), F32), compiler_params=_cp(("parallel", "parallel")),
    )(place, own, got)


MESH = pl.DeviceIdType.MESH
ANY = pl.BlockSpec(memory_space=pl.ANY)


def _place():
    x, y, c = lax.axis_index("x"), lax.axis_index("y"), lax.axis_index("c")
    chips = [(1 - x, y), (x, 1 - y), (1 - x, 1 - y)]
    return x, y, c, chips


def _part(ref, kind, chip, layers):
    if kind == "row":
        width = ref.shape[1] // 4
        return ref.at[layers, pl.ds(pl.multiple_of(chip * width, 16), width), :]
    width = ref.shape[2] // 4
    return ref.at[layers, :, pl.ds(pl.multiple_of(chip * width, 128), width)]


def _gather_chips(arrs, kinds, name):
    na = len(arrs)
    l2 = arrs[0].shape[0] // 2

    def body(*refs):
        outs = refs[na:2 * na]
        send1, recv1, send2, recv2 = refs[2 * na:]
        x, y, c, chips = _place()
        me = 2 * x + y
        half = pl.ds(c * l2, l2)
        other = pl.ds((1 - c) * l2, l2)

        def first(k, j):
            px, py = chips[j]
            mine = _part(outs[k], kinds[k], me, half)
            return pltpu.make_async_remote_copy(mine, mine, send1.at[k, j], recv1.at[k, j],
                                                device_id=(px, py, c), device_id_type=MESH)

        def landed(k, j):
            px, py = chips[j]
            return _part(outs[k], kinds[k], 2 * px + py, half)

        def passed(k, j):
            return pltpu.make_async_remote_copy(landed(k, j), landed(k, j), send2.at[k, j], recv2.at[k, j],
                                                device_id=(x, y, 1 - c), device_id_type=MESH)

        def from_sibling(k, j):
            px, py = chips[j]
            dst = _part(outs[k], kinds[k], 2 * px + py, other)
            return pltpu.make_async_remote_copy(dst, dst, send2.at[k, j], recv2.at[k, j],
                                                device_id=(x, y, 1 - c), device_id_type=MESH)

        for k in range(na):
            for j in range(3):
                first(k, j).start()
        for k in range(na):
            for j in range(3):
                first(k, j).wait_recv()
                passed(k, j).start()
        for k in range(na):
            for j in range(3):
                from_sibling(k, j).wait_recv()
        for k in range(na):
            for j in range(3):
                first(k, j).wait_send()
                passed(k, j).wait_send()

    return pl.pallas_call(
        body, name=name, in_specs=[ANY] * na, out_specs=[ANY] * na,
        out_shape=[jax.ShapeDtypeStruct(a.shape, a.dtype) for a in arrs],
        input_output_aliases={k: k for k in range(na)},
        scratch_shapes=[pltpu.SemaphoreType.DMA((na, 3))] * 4,
    )(*arrs)


def _swap_sibling(arrs, name):
    na = len(arrs)
    l2 = arrs[0].shape[0] // 2

    def body(*refs):
        ins, outs, send, recv = refs[:na], refs[na:2 * na], refs[2 * na], refs[2 * na + 1]
        x, y, c, _ = _place()
        cps = [pltpu.make_async_remote_copy(ins[k].at[pl.ds((1 - c) * l2, l2)], outs[k], send.at[k], recv.at[k],
                                            device_id=(x, y, 1 - c), device_id_type=MESH) for k in range(na)]
        for cp in cps:
            cp.start()
        for cp in cps:
            cp.wait()

    return pl.pallas_call(
        body, name=name, in_specs=[ANY] * na, out_specs=[ANY] * na,
        out_shape=[jax.ShapeDtypeStruct((l2,) + a.shape[1:], a.dtype) for a in arrs],
        scratch_shapes=[pltpu.SemaphoreType.DMA((na,))] * 2,
    )(*arrs)


def _scatter_chips(arrs, kinds, name):
    na = len(arrs)
    l2 = arrs[0].shape[0]
    part_shape = lambda a, kind: (a.shape[1] // 4, a.shape[2]) if kind == "row" else (a.shape[1], a.shape[2] // 4)

    def body(*refs):
        ins, outs, send, recv = refs[:na], refs[na:2 * na], refs[2 * na], refs[2 * na + 1]
        x, y, c, chips = _place()
        cps = []
        for k in range(na):
            for j, (px, py) in enumerate(chips):
                src = _part(ins[k], kinds[k], 2 * px + py, pl.ds(0, l2))
                cps.append(pltpu.make_async_remote_copy(src, outs[k].at[j], send.at[k, j], recv.at[k, j],
                                                        device_id=(px, py, c), device_id_type=MESH))
        for cp in cps:
            cp.start()
        for cp in cps:
            cp.wait()

    return pl.pallas_call(
        body, name=name, in_specs=[ANY] * na, out_specs=[ANY] * na,
        out_shape=[jax.ShapeDtypeStruct((3, l2) + part_shape(a, kind), a.dtype) for a, kind in zip(arrs, kinds)],
        scratch_shapes=[pltpu.SemaphoreType.DMA((na, 3))] * 2,
    )(*arrs)


def _join_halves(arrs, name):
    na = len(arrs)
    l2 = arrs[0].shape[0] // 2

    def body(*refs):
        outs, send, recv = refs[na:2 * na], refs[2 * na], refs[2 * na + 1]
        x, y, c, _ = _place()
        mine = pl.ds(c * l2, l2)
        theirs = pl.ds((1 - c) * l2, l2)
        cps = [pltpu.make_async_remote_copy(outs[k].at[mine], outs[k].at[mine], send.at[k], recv.at[k],
                                            device_id=(x, y, 1 - c), device_id_type=MESH) for k in range(na)]
        for cp in cps:
            cp.start()
        for k, cp in enumerate(cps):
            cp.wait_send()
            pltpu.make_async_remote_copy(outs[k].at[theirs], outs[k].at[theirs], send.at[k], recv.at[k],
                                         device_id=(x, y, 1 - c), device_id_type=MESH).wait_recv()

    return pl.pallas_call(
        body, name=name, in_specs=[ANY] * na, out_specs=[ANY] * na,
        out_shape=[jax.ShapeDtypeStruct(a.shape, a.dtype) for a in arrs],
        input_output_aliases={k: k for k in range(na)},
        scratch_shapes=[pltpu.SemaphoreType.DMA((na,))] * 2,
    )(*arrs)


def _gather_all(a, name):
    def body(a_ref, o_ref, send, recv, local):
        x, y, c, _ = _place()
        me = 4 * x + 2 * y + c
        mine = pltpu.make_async_copy(a_ref, o_ref.at[me], local)
        mine.start()
        cps = []
        for j in range(1, 8):
            peer = (x ^ (j >> 2), y ^ ((j >> 1) & 1), c ^ (j & 1))
            cps.append(pltpu.make_async_remote_copy(a_ref, o_ref.at[me], send.at[j - 1], recv.at[j - 1],
                                                    device_id=peer, device_id_type=MESH))
        for cp in cps:
            cp.start()
        for cp in cps:
            cp.wait()
        mine.wait()

    return pl.pallas_call(
        body, name=name, in_specs=[ANY], out_specs=ANY,
        out_shape=jax.ShapeDtypeStruct((8,) + a.shape, a.dtype),
        scratch_shapes=[pltpu.SemaphoreType.DMA((7,)), pltpu.SemaphoreType.DMA((7,)), pltpu.SemaphoreType.DMA],
    )(a)


def _pack(parts):
    flat = jnp.concatenate([p.reshape(-1) for p in parts])
    n = flat.shape[0]
    rows = -(-n // 1024) * 8
    return jnp.pad(flat, (0, rows * 128 - n)).reshape(rows, 128)


def _unpack(buf, like):
    flat, out, off = buf.reshape(-1), [], 0
    for p in like:
        out.append(flat[off:off + p.size].reshape(p.shape))
        off += p.size
    return out


def kernel(x, positions, norm_mix_pre, w_in, dn_conv_w, dn_a_log, dn_dt_bias, dn_norm_w, pool_w, pool_scale, swa_sinks, w_out, norm_mix_post, norm_ffn_pre, ffn_w_up, ffn_conv_w, ffn_conv_b, ffn_w_down, norm_ffn_post, loss_target, m_norm_mix_pre, m_w_in, m_dn_conv_w, m_dn_a_log, m_dn_dt_bias, m_dn_norm_w, m_pool_w, m_pool_scale, m_swa_sinks, m_w_out, m_norm_mix_post, m_norm_ffn_pre, m_ffn_w_up, m_ffn_conv_w, m_ffn_conv_b, m_ffn_w_down, m_norm_ffn_post, v_norm_mix_pre, v_w_in, v_dn_conv_w, v_dn_a_log, v_dn_dt_bias, v_dn_norm_w, v_pool_w, v_pool_scale, v_swa_sinks, v_w_out, v_norm_mix_post, v_norm_ffn_pre, v_ffn_w_up, v_ffn_conv_w, v_ffn_conv_b, v_ffn_w_down, v_norm_ffn_post):
    wts = dict(zip(WEIGHTS, (norm_mix_pre, w_in, dn_conv_w, dn_a_log, dn_dt_bias, dn_norm_w, pool_w, pool_scale, swa_sinks,
                             w_out, norm_mix_post, norm_ffn_pre, ffn_w_up, ffn_conv_w, ffn_conv_b, ffn_w_down, norm_ffn_post)))
    mom = dict(zip(WEIGHTS, (m_norm_mix_pre, m_w_in, m_dn_conv_w, m_dn_a_log, m_dn_dt_bias, m_dn_norm_w, m_pool_w, m_pool_scale,
                             m_swa_sinks, m_w_out, m_norm_mix_post, m_norm_ffn_pre, m_ffn_w_up, m_ffn_conv_w, m_ffn_conv_b,
                             m_ffn_w_down, m_norm_ffn_post)))
    var = dict(zip(WEIGHTS, (v_norm_mix_pre, v_w_in, v_dn_conv_w, v_dn_a_log, v_dn_dt_bias, v_dn_norm_w, v_pool_w, v_pool_scale,
                             v_swa_sinks, v_w_out, v_norm_mix_post, v_norm_ffn_pre, v_ffn_w_up, v_ffn_conv_w, v_ffn_conv_b,
                             v_ffn_w_down, v_norm_ffn_post)))
    c = lax.axis_index("c")
    chip = 2 * lax.axis_index("x") + lax.axis_index("y")
    place = jnp.stack([chip, c]).astype(jnp.int32)
    kinds = [KIND[k] for k in BIG]
    in_cols = w_in.shape[2]
    in_cols_pad = -(-in_cols // 128) * 128

    shards = dict(wts, w_in=jnp.pad(w_in, ((0, 0), (0, 0), (0, in_cols_pad - in_cols))))
    spread = [_spread_shard(shards[k], KIND[k], place, BF16, "cast_" + k) for k in BIG]
    w = dict(wts)
    w.update(zip(BIG, _gather_chips(spread, kinds, "gather_weights")))
    by_chip = w["w_in"].reshape(w["w_in"].shape[:2] + (4, in_cols_pad))[..., :in_cols]
    w["w_in"] = _pad_in(by_chip.reshape(by_chip.shape[:2] + (4 * in_cols,)))
    conv_like = [wts[k] for k in CONV]
    conv_all = _gather_all(_pack(conv_like), "gather_conv")
    for i, k in enumerate(CONV):
        w[k] = jnp.concatenate([_unpack(conv_all[2 * j], conv_like)[i] for j in range(4)], axis=2)

    loss, dx, grads = _local_step(x[0], positions[0], loss_target[0], w)
    loss = lax.psum(loss[0, 0], ("x", "y", "c"))

    g_in = _unpad_in(grads["w_in"])
    g_in = jnp.pad(g_in.reshape(g_in.shape[:2] + (4, in_cols)), ((0, 0), (0, 0), (0, 0), (0, in_cols_pad - in_cols)))
    mine = [g_in.reshape(g_in.shape[:2] + (4 * in_cols_pad,)) if k == "w_in" else grads[k] for k in BIG]
    got = _swap_sibling(mine, "grads_to_sibling")
    chip_sum = [_chip_sum(a, b, place, "chip_sum_" + k) for k, a, b in zip(BIG, mine, got)]
    arrived = _scatter_chips(chip_sum, kinds, "grads_to_owner")
    halves = [_owner_sum(a, b, KIND[k], place, "owner_sum_" + k) for k, a, b in zip(BIG, chip_sum, arrived)]
    g_big = dict(zip(BIG, _join_halves(halves, "grads_join")))
    g_big["w_in"] = g_big["w_in"][..., :in_cols]

    small_like = [wts[k] for k in SMALL]
    full_like = small_like + [w[k] for k in CONV]
    g_buf = _sum_slots(_gather_all(_pack([jnp.stack(grads[k]) for k in SMALL + CONV]), "gather_small"), "sum_small")
    g_small = dict(zip(SMALL + CONV, _unpack(g_buf, full_like)))
    for k in CONV:
        width = wts[k].shape[2]
        g_small[k] = lax.dynamic_slice_in_dim(g_small[k], chip * width, width, 2)
    pk = lambda d: _pack([d[k] for k in SMALL + CONV])
    upd = _adamw(pk(wts), pk(g_small), pk(mom), pk(var), "adam_small")
    upd_small = [dict(zip(SMALL + CONV, _unpack(b, small_like + conv_like))) for b in upd]

    g_all, d_all, m_all, v_all = {}, {}, {}, {}
    for k in WEIGHTS:
        if k in BIG:
            g_all[k] = g_big[k]
            d_all[k], m_all[k], v_all[k] = _adamw(wts[k], g_big[k], mom[k], var[k], "adam_" + k)
        else:
            g_all[k], d_all[k], m_all[k], v_all[k] = g_small[k], upd_small[0][k], upd_small[1][k], upd_small[2][k]
    return (loss, dx[None], *[g_all[k] for k in WEIGHTS], *[d_all[k] for k in WEIGHTS],
            *[m_all[k] for k in WEIGHTS], *[v_all[k] for k in WEIGHTS])
```

```python
import functools
import math

import jax
import jax.numpy as jnp
from jax import lax
from jax.experimental import pallas as pl
from jax.experimental.pallas import tpu as pltpu

F32 = jnp.float32
BF16 = jnp.bfloat16

HEAD_DIM = 128
DN_HEADS = 6
DN_CONV = 4
DN_CHUNK = 64
POOL_GROUPS = 4
SWA_HEADS = 6
SWA_KV_HEADS = 2
SWA_GROUP = SWA_HEADS // SWA_KV_HEADS
SWA_BLOCK = 128
ROPE_THETA = 10000.0
FFN_CONV = 3
NORM_EPS = 1e-6
DN_W = DN_HEADS * HEAD_DIM
POOL_W = POOL_GROUPS * HEAD_DIM
SWA_W = SWA_HEADS * HEAD_DIM
SWA_KV_W = SWA_KV_HEADS * HEAD_DIM
MIX_W = DN_W + POOL_W + SWA_W
IN_TRUE = 3 * DN_W + DN_W + 2 * DN_HEADS + POOL_W + SWA_W + 2 * SWA_KV_W
GATE_END = 4 * DN_W + 2 * DN_HEADS
CB_Z = 18
CB_BD = 24
CB_POOL = 25
CB_SQ = 29
CB_SK = 35
CB_SV = 37
IN_PAD = 40 * 128
ADAM_LR, ADAM_B1, ADAM_B2, ADAM_EPS, ADAM_WD, ADAM_STEP = 0.001, 0.9, 0.999, 1e-08, 0.01, 10

VMEM_LIMIT = 48 * 1024 * 1024
MM_TK_MAX = 2816
HIGH = lax.Precision.HIGHEST


def _cp(sem):
    return pltpu.CompilerParams(dimension_semantics=sem, vmem_limit_bytes=VMEM_LIMIT)


def _tile(n, prefs):
    for p in prefs:
        if n % p == 0:
            return p
    return n


def _rows(t):
    return _tile(t, (256, 128))


_DN = {"nn": (((1,), (0,)), ((), ())), "nt": (((1,), (1,)), ((), ())), "tn": (((0,), (0,)), ((), ()))}


def _mm_operand(arr, pick, block, idx):
    if pick is None:
        return pl.BlockSpec(block, idx)
    if pick == "split":
        per = arr.shape[2] // block[1]

        def split_idx(i, j, kk):
            r, c = idx(i, j, kk)
            return lax.div(c, per), r, lax.rem(c, per)

        return pl.BlockSpec((None,) + block, split_idx)
    slab = pick[1]
    return pl.BlockSpec((None,) + block, lambda i, j, kk: (slab,) + idx(i, j, kk))


def _mm(a, b, mode, out_dtype, name, a_pick=None, b_pick=None):
    def dims(arr, pick):
        r, c = arr.shape[-2:]
        return (r, c * arr.shape[0]) if pick == "split" else (r, c)

    (a0, a1), (b0, b1) = dims(a, a_pick), dims(b, b_pick)
    k, m = (a0, a1) if mode == "tn" else (a1, a0)
    n = b0 if mode == "nt" else b1
    lim = lambda arr, pick, is_last, full: arr.shape[2] if (pick == "split" and is_last) else full
    tm = _tile(lim(a, a_pick, mode == "tn", m), (1024, 512, 256, 128))
    tn = _tile(lim(b, b_pick, mode != "nt", n), (1024, 512, 256, 128))
    k_lim = min(lim(a, a_pick, mode != "tn", k), lim(b, b_pick, mode == "nt", k))
    tk = max([d for d in range(128, min(k_lim, MM_TK_MAX) + 1, 128) if k_lim % d == 0], default=k_lim)
    nk = k // tk

    def body(a_ref, b_ref, o_ref, *scratch):
        part = lax.dot_general(a_ref[...], b_ref[...], _DN[mode], preferred_element_type=F32)
        if nk == 1:
            o_ref[...] = part.astype(o_ref.dtype)
            return
        acc_ref, = scratch
        kk = pl.program_id(2)

        @pl.when(kk == 0)
        def _():
            acc_ref[...] = part

        @pl.when(kk > 0)
        def _():
            acc_ref[...] += part

        @pl.when(kk == nk - 1)
        def _():
            o_ref[...] = acc_ref[...].astype(o_ref.dtype)

    if mode == "tn":
        a_spec = _mm_operand(a, a_pick, (tk, tm), lambda i, j, kk: (kk, i))
    else:
        a_spec = _mm_operand(a, a_pick, (tm, tk), lambda i, j, kk: (i, kk))
    if mode == "nt":
        b_spec = _mm_operand(b, b_pick, (tn, tk), lambda i, j, kk: (j, kk))
    else:
        b_spec = _mm_operand(b, b_pick, (tk, tn), lambda i, j, kk: (kk, j))
    return pl.pallas_call(
        body, name=name, grid=(m // tm, n // tn, nk),
        in_specs=[a_spec, b_spec], out_specs=pl.BlockSpec((tm, tn), lambda i, j, kk: (i, j)),
        out_shape=jax.ShapeDtypeStruct((m, n), out_dtype),
        scratch_shapes=[pltpu.VMEM((tm, tn), F32)] if nk > 1 else [],
        compiler_params=_cp(("parallel", "parallel", "arbitrary")),
    )(a, b)


def _rms(x, w):
    return x * lax.rsqrt(jnp.mean(x * x, axis=-1, keepdims=True) + NORM_EPS) * w


def _norm_fwd(x, w, name):
    t, d = x.shape
    r = _rows(t)

    def body(x_ref, w_ref, h_ref):
        h_ref[...] = _rms(x_ref[...], w_ref[...]).astype(h_ref.dtype)

    return pl.pallas_call(
        body, name=name, grid=(t // r,),
        in_specs=[pl.BlockSpec((r, d), lambda i: (i, 0)), pl.BlockSpec((1, d), lambda i: (0, 0))],
        out_specs=pl.BlockSpec((r, d), lambda i: (i, 0)),
        out_shape=jax.ShapeDtypeStruct((t, d), BF16), compiler_params=_cp(("parallel",)),
    )(x, w)


def _resnorm_fwd(x, y, w, name):
    t, d = x.shape
    r = _rows(t)

    def body(x_ref, y_ref, w_ref, o_ref):
        o_ref[...] = x_ref[...] + _rms(y_ref[...], w_ref[...])

    return pl.pallas_call(
        body, name=name, grid=(t // r,),
        in_specs=[pl.BlockSpec((r, d), lambda i: (i, 0)), pl.BlockSpec((r, d), lambda i: (i, 0)),
                  pl.BlockSpec((1, d), lambda i: (0, 0))],
        out_specs=pl.BlockSpec((r, d), lambda i: (i, 0)),
        out_shape=jax.ShapeDtypeStruct((t, d), F32), compiler_params=_cp(("parallel",)),
    )(x, y, w)


def _norm_bwd(x, w, dh, add, out_dtype, name):
    t, d = x.shape
    r = _rows(t)
    has_add = add is not None

    def body(*refs):
        if has_add:
            x_ref, w_ref, dh_ref, add_ref, dx_ref, dw_ref = refs
        else:
            x_ref, w_ref, dh_ref, dx_ref, dw_ref = refs
        xv = x_ref[...]
        g = dh_ref[...].astype(F32)
        rs = lax.rsqrt(jnp.mean(xv * xv, axis=-1, keepdims=True) + NORM_EPS)
        xh = xv * rs
        gw = g * w_ref[...]
        dx = rs * (gw - xh * jnp.mean(gw * xh, axis=-1, keepdims=True))
        if has_add:
            dx = dx + add_ref[...]
        dx_ref[...] = dx.astype(dx_ref.dtype)

        @pl.when(pl.program_id(0) == 0)
        def _():
            dw_ref[...] = jnp.zeros_like(dw_ref)

        dw_ref[...] += jnp.sum(g * xh, axis=0, keepdims=True)

    row = pl.BlockSpec((r, d), lambda i: (i, 0))
    vec = pl.BlockSpec((1, d), lambda i: (0, 0))
    ins = [x, w, dh] + ([add] if has_add else [])
    return pl.pallas_call(
        body, name=name, grid=(t // r,),
        in_specs=[row, vec, row] + ([row] if has_add else []),
        out_specs=[row, vec],
        out_shape=[jax.ShapeDtypeStruct((t, d), out_dtype), jax.ShapeDtypeStruct((1, d), F32)],
        compiler_params=_cp(("arbitrary",)),
    )(*ins)


def _loss_head(y, target, name):
    t, d = y.shape
    r = _rows(t)

    def body(y_ref, t_ref, l_ref, g_ref):
        e = y_ref[...] - t_ref[...]
        g_ref[...] = e * (1.0 / d)

        @pl.when(pl.program_id(0) == 0)
        def _():
            l_ref[...] = jnp.zeros_like(l_ref)

        l_ref[...] += jnp.sum(e * e) * (0.5 / d)

    row = pl.BlockSpec((r, d), lambda i: (i, 0))
    return pl.pallas_call(
        body, name=name, grid=(t // r,), in_specs=[row, row],
        out_specs=[pl.BlockSpec((1, 128), lambda i: (0, 0)), row],
        out_shape=[jax.ShapeDtypeStruct((1, 128), F32), jax.ShapeDtypeStruct((t, d), F32)],
        compiler_params=_cp(("arbitrary",)),
    )(y, target)


def _down(x, s):
    return x if s == 0 else pltpu.roll(x, s, 0)


def _up(x, s):
    return x if s == 0 else pltpu.roll(x, x.shape[0] - s, 0)


def _halo(t, r, hh, tc, col):
    q = r // hh
    last = t // hh - 1
    tile = pl.BlockSpec((r, tc), lambda j, i: (i, col(j)))
    prev = pl.BlockSpec((hh, tc), lambda j, i: (jnp.maximum(i * q - 1, 0), col(j)))
    nxt = pl.BlockSpec((hh, tc), lambda j, i: (jnp.minimum((i + 1) * q, last), col(j)))
    return tile, prev, nxt


def _sig(x):
    return 1.0 / (1.0 + jnp.exp(-x))


def _dsilu(x, s):
    return s * (1.0 + x * (1.0 - s))


def _dn_pre_fwd(p, conv_w, name):
    t = p.shape[0]
    r = _rows(t)

    def body(x_ref, xp_ref, w_ref, o_ref):
        j, i = pl.program_id(0), pl.program_id(1)
        xe = jnp.concatenate([jnp.where(i == 0, 0.0, xp_ref[...]), x_ref[...]], axis=0)
        c = sum(_down(xe, DN_CONV - 1 - k) * w_ref[pl.ds(k, 1), :] for k in range(DN_CONV))[8:]
        a = c * _sig(c)
        for h in range(DN_HEADS):
            ah = a[:, h * 128:(h + 1) * 128]
            fac = lax.rsqrt(jnp.sum(ah * ah, axis=-1, keepdims=True) + NORM_EPS)
            o_ref[:, h * 128:(h + 1) * 128] = ah * jnp.where(j == 0, fac * HEAD_DIM ** -0.5, jnp.where(j == 1, fac, 1.0))

    tile, prev, _ = _halo(t, r, 8, DN_W, lambda j: j)
    return pl.pallas_call(
        body, name=name, grid=(3, t // r),
        in_specs=[tile, prev, pl.BlockSpec((DN_CONV, DN_W), lambda j, i: (0, j))],
        out_specs=tile, out_shape=jax.ShapeDtypeStruct((t, 3 * DN_W), F32),
        compiler_params=_cp(("parallel", "parallel")),
    )(p, p, conv_w)


def _dn_pre_bwd(p, conv_w, dqkv, name):
    t = p.shape[0]
    r = _rows(t)
    ni = t // r

    def body(x_ref, xp_ref, xn_ref, w_ref, d_ref, dn_ref, dx_ref, dw_ref):
        j, i = pl.program_id(0), pl.program_id(1)
        xe = jnp.concatenate([jnp.where(i == 0, 0.0, xp_ref[...]), x_ref[...], xn_ref[...]], axis=0)
        de = jnp.concatenate([jnp.zeros((8, DN_W), F32), d_ref[...], jnp.where(i == ni - 1, 0.0, dn_ref[...])], axis=0)
        xs = [_down(xe, DN_CONV - 1 - k) for k in range(DN_CONV)]
        c = sum(xs[k] * w_ref[pl.ds(k, 1), :] for k in range(DN_CONV))
        s = _sig(c)
        a = c * s
        das = []
        for h in range(DN_HEADS):
            ah, dh = a[:, h * 128:(h + 1) * 128], de[:, h * 128:(h + 1) * 128]
            fac = lax.rsqrt(jnp.sum(ah * ah, axis=-1, keepdims=True) + NORM_EPS)
            dnorm = fac * dh - ah * (fac * fac * fac) * jnp.sum(dh * ah, axis=-1, keepdims=True)
            das.append(jnp.where(j == 0, dnorm * HEAD_DIM ** -0.5, jnp.where(j == 1, dnorm, dh)))
        dc = jnp.concatenate(das, axis=1) * _dsilu(c, s)
        dx_ref[...] = sum(_up(dc, DN_CONV - 1 - k) * w_ref[pl.ds(k, 1), :] for k in range(DN_CONV))[8:8 + r]

        @pl.when(i == 0)
        def _():
            dw_ref[...] = jnp.zeros_like(dw_ref)

        for k in range(DN_CONV):
            dw_ref[pl.ds(k, 1), :] += jnp.sum((dc * xs[k])[8:8 + r], axis=0, keepdims=True)

    tile, prev, nxt = _halo(t, r, 8, DN_W, lambda j: j)
    wspec = pl.BlockSpec((DN_CONV, DN_W), lambda j, i: (0, j))
    return pl.pallas_call(
        body, name=name, grid=(3, ni),
        in_specs=[tile, prev, nxt, wspec, tile, nxt],
        out_specs=[tile, wspec],
        out_shape=[jax.ShapeDtypeStruct((t, 3 * DN_W), F32), jax.ShapeDtypeStruct((DN_CONV, 3 * DN_W), F32)],
        compiler_params=_cp(("parallel", "arbitrary")),
    )(p, p, p, conv_w, dqkv, dqkv)


def _ffn_act_fwd(up, cw, cb, name):
    t, f2 = up.shape
    f = f2 // 2
    r = _rows(t)
    tc = _tile(f, (512, 256, 128))
    nj = f // tc

    def body(a_ref, ap_ref, b_ref, bp_ref, wa_ref, wb_ref, ca_ref, cb_ref, o_ref):
        i = pl.program_id(1)

        def conv(x_ref, xp_ref, w_ref, c_ref):
            xe = jnp.concatenate([jnp.where(i == 0, 0.0, xp_ref[...]), x_ref[...]], axis=0)
            return sum(_down(xe, FFN_CONV - 1 - k) * w_ref[pl.ds(k, 1), :] for k in range(FFN_CONV))[8:] + c_ref[...]

        ua = conv(a_ref, ap_ref, wa_ref, ca_ref)
        ub = conv(b_ref, bp_ref, wb_ref, cb_ref)
        o_ref[...] = (ua * _sig(ua) * ub).astype(o_ref.dtype)

    ta, pa, _ = _halo(t, r, 8, tc, lambda j: j)
    tb, pb, _ = _halo(t, r, 8, tc, lambda j: j + nj)
    wa = pl.BlockSpec((FFN_CONV, tc), lambda j, i: (0, j))
    wb = pl.BlockSpec((FFN_CONV, tc), lambda j, i: (0, j + nj))
    ca = pl.BlockSpec((1, tc), lambda j, i: (0, j))
    cbs = pl.BlockSpec((1, tc), lambda j, i: (0, j + nj))
    return pl.pallas_call(
        body, name=name, grid=(nj, t // r),
        in_specs=[ta, pa, tb, pb, wa, wb, ca, cbs], out_specs=ta,
        out_shape=jax.ShapeDtypeStruct((t, f), BF16), compiler_params=_cp(("parallel", "parallel")),
    )(up, up, up, up, cw, cw, cb, cb)


def _ffn_act_bwd(up, cw, cb, dact, name):
    t, f2 = up.shape
    f = f2 // 2
    r = _rows(t)
    ni = t // r
    tc = _tile(f, (512, 256, 128))
    nj = f // tc

    def body(a_ref, ap_ref, an_ref, b_ref, bp_ref, bn_ref, wa_ref, wb_ref, ca_ref, cb_ref, d_ref, dn_ref,
             du_ref, dw_ref, dc_ref):
        i = pl.program_id(1)
        dua_ref, dub_ref, dwa_ref, dwb_ref, dca_ref, dcb_ref = (du_ref.at[0], du_ref.at[1], dw_ref.at[0], dw_ref.at[1],
                                                                  dc_ref.at[0], dc_ref.at[1])

        def ext(x_ref, xp_ref, xn_ref):
            return jnp.concatenate([jnp.where(i == 0, 0.0, xp_ref[...]), x_ref[...], xn_ref[...]], axis=0)

        ae, be = ext(a_ref, ap_ref, an_ref), ext(b_ref, bp_ref, bn_ref)
        as_ = [_down(ae, FFN_CONV - 1 - k) for k in range(FFN_CONV)]
        bs_ = [_down(be, FFN_CONV - 1 - k) for k in range(FFN_CONV)]
        ua = sum(as_[k] * wa_ref[pl.ds(k, 1), :] for k in range(FFN_CONV)) + ca_ref[...]
        ub = sum(bs_[k] * wb_ref[pl.ds(k, 1), :] for k in range(FFN_CONV)) + cb_ref[...]
        de = jnp.concatenate([jnp.zeros((8, tc), F32), d_ref[...].astype(F32),
                              jnp.where(i == ni - 1, 0.0, dn_ref[...].astype(F32))], axis=0)
        s = _sig(ua)
        dua = de * ub * _dsilu(ua, s)
        dub = de * ua * s
        dua_ref[...] = sum(_up(dua, FFN_CONV - 1 - k) * wa_ref[pl.ds(k, 1), :] for k in range(FFN_CONV))[8:8 + r].astype(dua_ref.dtype)
        dub_ref[...] = sum(_up(dub, FFN_CONV - 1 - k) * wb_ref[pl.ds(k, 1), :] for k in range(FFN_CONV))[8:8 + r].astype(dub_ref.dtype)

        @pl.when(i == 0)
        def _():
            dw_ref[...] = jnp.zeros_like(dw_ref)
            dc_ref[...] = jnp.zeros_like(dc_ref)

        for k in range(FFN_CONV):
            dwa_ref[pl.ds(k, 1), :] += jnp.sum((dua * as_[k])[8:8 + r], axis=0, keepdims=True)
            dwb_ref[pl.ds(k, 1), :] += jnp.sum((dub * bs_[k])[8:8 + r], axis=0, keepdims=True)
        dca_ref[...] += jnp.sum(dua[8:8 + r], axis=0, keepdims=True)
        dcb_ref[...] += jnp.sum(dub[8:8 + r], axis=0, keepdims=True)

    ta, pa, na = _halo(t, r, 8, tc, lambda j: j)
    tb, pb, nb = _halo(t, r, 8, tc, lambda j: j + nj)
    wa = pl.BlockSpec((FFN_CONV, tc), lambda j, i: (0, j))
    wb = pl.BlockSpec((FFN_CONV, tc), lambda j, i: (0, j + nj))
    ca = pl.BlockSpec((1, tc), lambda j, i: (0, j))
    cbs = pl.BlockSpec((1, tc), lambda j, i: (0, j + nj))
    return pl.pallas_call(
        body, name=name, grid=(nj, ni),
        in_specs=[ta, pa, na, tb, pb, nb, wa, wb, ca, cbs, ta, na],
        out_specs=[pl.BlockSpec((2, r, tc), lambda j, i: (0, i, j)), pl.BlockSpec((2, FFN_CONV, tc), lambda j, i: (0, 0, j)),
                   pl.BlockSpec((2, 1, tc), lambda j, i: (0, 0, j))],
        out_shape=[jax.ShapeDtypeStruct((2, t, f), BF16), jax.ShapeDtypeStruct((2, FFN_CONV, f), F32),
                   jax.ShapeDtypeStruct((2, 1, f), F32)],
        compiler_params=_cp(("parallel", "arbitrary")),
    )(up, up, up, up, up, up, cw, cw, cb, cb, dact, dact)


def _pool_pick(g, vals):
    return jnp.where(g == 0, vals[0], jnp.where(g == 1, vals[1], jnp.where(g == 2, vals[2], vals[3])))


def _pool_pre(xe, g, t0):
    s1 = xe + _down(xe, 1)
    s2 = s1 + _down(s1, 2)
    s3 = s2 + _down(s2, 4)
    s4 = s3 + _down(s3, 8)
    r = xe.shape[0] - 16
    pos = (t0 + lax.broadcasted_iota(jnp.int32, (r, 1), 0)).astype(F32)
    cnt = jnp.minimum(pos + 1.0, _pool_pick(g, (2.0, 4.0, 8.0, 16.0)))
    return _pool_pick(g, (s1, s2, s3, s4))[16:] / cnt - xe[16:]


def _pool_fwd(p, pool_w, scale, name):
    t = p.shape[0]
    r = _tile(t, (1024, 256, 128))

    def body(x_ref, xp_ref, w_ref, sc_ref, o_ref):
        g, i = pl.program_id(0), pl.program_id(1)
        xe = jnp.concatenate([jnp.where(i == 0, 0.0, xp_ref[...]), x_ref[...]], axis=0)
        pre = _pool_pre(xe, g, i * r)
        o_ref[...] = jnp.dot(pre, w_ref[0], preferred_element_type=F32) * sc_ref[...]

    tile, prev, _ = _halo(t, r, 16, 128, lambda j: CB_POOL + j)
    return pl.pallas_call(
        body, name=name, grid=(POOL_GROUPS, t // r),
        in_specs=[tile, prev, pl.BlockSpec((1, 128, 128), lambda j, i: (j, 0, 0)), pl.BlockSpec((1, 128), lambda j, i: (0, j))],
        out_specs=pl.BlockSpec((r, 128), lambda j, i: (i, j)),
        out_shape=jax.ShapeDtypeStruct((t, POOL_W), F32), compiler_params=_cp(("parallel", "parallel")),
    )(p, p, pool_w, scale)


def _pool_bwd(p, pool_w, scale, dycat, name):
    t = p.shape[0]
    r = _tile(t, (1024, 256, 128))
    ni = t // r

    def body(x_ref, xp_ref, w_ref, sc_ref, d_ref, dn_ref, dx_ref, dw_ref, dsc_ref):
        g, i = pl.program_id(0), pl.program_id(1)
        xe = jnp.concatenate([jnp.where(i == 0, 0.0, xp_ref[...]), x_ref[...]], axis=0)
        pre = _pool_pre(xe, g, i * r)
        w = w_ref[0]
        dy = d_ref[...]
        dye = jnp.concatenate([dy, jnp.where(i == ni - 1, 0.0, dn_ref[...])], axis=0)
        dpre = lax.dot_general(dye * sc_ref[...], w, _DN["nt"], preferred_element_type=F32)
        pos = (i * r + lax.broadcasted_iota(jnp.int32, (r + 16, 1), 0)).astype(F32)
        dm = dpre / jnp.minimum(pos + 1.0, _pool_pick(g, (2.0, 4.0, 8.0, 16.0)))
        a1 = dm + _up(dm, 1)
        a2 = a1 + _up(a1, 2)
        a3 = a2 + _up(a2, 4)
        a4 = a3 + _up(a3, 8)
        dx_ref[...] = (_pool_pick(g, (a1, a2, a3, a4)) - dpre)[:r]

        @pl.when(i == 0)
        def _():
            dw_ref[...] = jnp.zeros_like(dw_ref)
            dsc_ref[...] = jnp.zeros_like(dsc_ref)

        dw_ref[0] += lax.dot_general(pre, dy * sc_ref[...], _DN["tn"], preferred_element_type=F32)
        dsc_ref[...] += jnp.sum(dy * jnp.dot(pre, w, preferred_element_type=F32), axis=0, keepdims=True)

    tile, prev, _ = _halo(t, r, 16, 128, lambda j: CB_POOL + j)
    dtile, _, dnxt = _halo(t, r, 16, 128, lambda j: DN_W // 128 + j)
    wspec = pl.BlockSpec((1, 128, 128), lambda j, i: (j, 0, 0))
    sspec = pl.BlockSpec((1, 128), lambda j, i: (0, j))
    return pl.pallas_call(
        body, name=name, grid=(POOL_GROUPS, ni),
        in_specs=[tile, prev, wspec, sspec, dtile, dnxt],
        out_specs=[pl.BlockSpec((r, 128), lambda j, i: (i, j)), wspec, sspec],
        out_shape=[jax.ShapeDtypeStruct((t, POOL_W), F32), jax.ShapeDtypeStruct((POOL_GROUPS, 128, 128), F32),
                   jax.ShapeDtypeStruct((1, POOL_W), F32)],
        compiler_params=_cp(("parallel", "arbitrary")),
    )(p, p, pool_w, scale, dycat, dycat)


_DNB = {"nn": (((2,), (1,)), ((0,), (0,))), "nt": (((2,), (2,)), ((0,), (0,))), "tn": (((1,), (1,)), ((0,), (0,)))}


def _dot(a, b, mode="nn", precision=None):
    dn = _DNB[mode] if a.ndim == 3 else _DN[mode]
    return lax.dot_general(a, b, dn, precision=precision, preferred_element_type=F32)


@functools.partial(jax.custom_vjp, nondiff_argnums=(2,))
def _bdot(a, b, mode):
    return _dot(a.astype(BF16), b.astype(BF16), mode)


def _bdot_fwd(a, b, mode):
    return _bdot(a, b, mode), (a, b)


def _bdot_bwd(mode, res, g):
    a, b = res
    if mode == "nn":
        return _bdot(g, b, "nt"), _bdot(a, g, "tn")
    if mode == "nt":
        return _bdot(g, b, "nn"), _bdot(g, a, "tn")
    return _bdot(b, g, "nt"), _bdot(a, g, "nn")


_bdot.defvjp(_bdot_fwd, _bdot_bwd)


def _dn_consts():
    c = DN_CHUNK
    ii = lax.broadcasted_iota(jnp.int32, (c, c), 0)
    jj = lax.broadcasted_iota(jnp.int32, (c, c), 1)
    one, zero = jnp.ones((c, c), F32), jnp.zeros((c, c), F32)
    return dict(ltri=jnp.where(ii >= jj, one, zero), utri=jnp.where(ii <= jj, one, zero), ones=one,
                causal=ii >= jj, strict=ii > jj, eye=jnp.where(ii == jj, one, zero))


def _dn_chunk(q, k, v, z, bcol, acol, s_in, alog, dtb, nw, cs):
    c = DN_CHUNK
    hh = q.shape[0]
    per_head = lambda m: jnp.broadcast_to(m, (hh, c, c))
    beta = _sig(bcol)
    xa = acol + dtb
    g = -jnp.exp(alog) * (jnp.maximum(xa, 0.0) + jnp.log(1.0 + jnp.exp(-jnp.abs(xa))))
    gb = jnp.broadcast_to(g, (hh, c, HEAD_DIM))
    gbc = jnp.broadcast_to(g, (hh, c, c))
    gc = _dot(per_head(cs["ltri"]), gb, precision=HIGH)
    gcol = _dot(per_head(cs["ltri"]), gbc, precision=HIGH)
    grow = _dot(per_head(cs["ones"]), gbc * cs["utri"], precision=HIGH)
    decay = jnp.exp(jnp.where(cs["causal"], gcol - grow, -1e30))
    kb = k * beta
    vb = v * beta
    nil = -jnp.where(cs["strict"], _bdot(kb, k, "nt") * decay, 0.0)
    inv = cs["eye"] + nil
    powk = nil
    for _ in range(int(math.log2(c)) - 1):
        powk = _dot(powk, powk)
        inv = _dot(inv, cs["eye"] + powk)
    eg = jnp.exp(gc)
    u = _bdot(inv, vb, "nn")
    w = _bdot(inv, kb * eg, "nn")
    a = _bdot(q, k, "nt") * decay
    v_new = u - _bdot(w, s_in, "nn")
    o = _bdot(q * eg, s_in, "nn") + _bdot(a, v_new, "nn")
    glast = jnp.sum(gb, axis=1, keepdims=True)
    s_out = s_in * jnp.exp(glast) + _bdot(k * jnp.exp(glast - gc), v_new, "tn")
    on = o * lax.rsqrt(jnp.mean(o * o, axis=-1, keepdims=True) + NORM_EPS) * nw
    return on * (z * _sig(z)), s_out


def _lane_pick(x, lane, idx):
    return jnp.sum(jnp.where(lane == idx, x, 0.0), axis=1, keepdims=True)


def _dn_load(q_ref, k_ref, v_ref, z_ref, bd_ref, al_ref, dt_ref, nw_ref, s_in):
    lane = lax.broadcasted_iota(jnp.int32, (1, 128), 1)
    bd, al, dt = bd_ref[...], al_ref[...], dt_ref[...]
    heads = range(DN_HEADS)
    wide = lambda ref: jnp.stack([ref[:, h * 128:(h + 1) * 128] for h in heads], axis=0)
    col = lambda x, off: jnp.stack([_lane_pick(x, lane, off + h) for h in heads], axis=0)
    return (wide(q_ref), wide(k_ref), wide(v_ref), wide(z_ref), col(bd, 0), col(bd, DN_HEADS), s_in,
            col(al, 0), col(dt, 0), nw_ref[...])


def _dn_fwd(qkv, p, alog, dtb, nw, name):
    t = qkv.shape[0]
    c = DN_CHUNK
    n = t // c

    def body(q_ref, k_ref, v_ref, z_ref, bd_ref, al_ref, dt_ref, nw_ref, y_ref, ss_ref, s_scr):
        @pl.when(pl.program_id(0) == 0)
        def _():
            s_scr[...] = jnp.zeros_like(s_scr)

        s_in = s_scr[...]
        y, s_out = _dn_chunk(*_dn_load(q_ref, k_ref, v_ref, z_ref, bd_ref, al_ref, dt_ref, nw_ref, s_in), _dn_consts())
        ss_ref[0] = s_in
        s_scr[...] = s_out
        for h in range(DN_HEADS):
            y_ref[:, h * 128:(h + 1) * 128] = y[h]

    wide = lambda j: pl.BlockSpec((c, DN_W), lambda i: (i, j))
    vec = pl.BlockSpec((1, 128), lambda i: (0, 0))
    return pl.pallas_call(
        body, name=name, grid=(n,),
        in_specs=[wide(0), wide(1), wide(2), wide(3), pl.BlockSpec((c, 128), lambda i: (i, CB_BD)), vec, vec, vec],
        out_specs=[wide(0), pl.BlockSpec((1, DN_HEADS, 128, 128), lambda i: (i, 0, 0, 0))],
        out_shape=[jax.ShapeDtypeStruct((t, DN_W), F32), jax.ShapeDtypeStruct((n, DN_HEADS, 128, 128), F32)],
        scratch_shapes=[pltpu.VMEM((DN_HEADS, 128, 128), F32)],
        compiler_params=_cp(("arbitrary",)),
    )(qkv, qkv, qkv, p, p, alog, dtb, nw)


def _dn_bwd(qkv, p, alog, dtb, nw, states, dycat, name, carry=None):
    t = qkv.shape[0]
    c = DN_CHUNK
    n = t // c
    sums, kinds = carry if carry is not None else ((), ())
    na = len(sums)

    def body(*refs):
        q_ref, k_ref, v_ref, z_ref, bd_ref, al_ref, dt_ref, nw_ref, ss_ref, dy_ref = refs[:10]
        dqkv_ref, dz_ref, dbd_ref, dal_ref, ddt_ref, dnw_ref = refs[10 + na:16 + na]
        ds_scr = refs[16 + 2 * na]
        if na:
            copies = _scatter_copies(refs[10:10 + na], refs[16 + na:16 + 2 * na], kinds, *refs[17 + 2 * na:])

        @pl.when(pl.program_id(0) == 0)
        def _():
            ds_scr[...] = jnp.zeros_like(ds_scr)
            dal_ref[...] = jnp.zeros_like(dal_ref)
            ddt_ref[...] = jnp.zeros_like(ddt_ref)
            dnw_ref[...] = jnp.zeros_like(dnw_ref)
            if na:
                for cp in copies:
                    cp.start()

        lane = lax.broadcasted_iota(jnp.int32, (1, 128), 1)
        args = _dn_load(q_ref, k_ref, v_ref, z_ref, bd_ref, al_ref, dt_ref, nw_ref, ss_ref[0])
        dy = jnp.stack([dy_ref[:, h * 128:(h + 1) * 128] for h in range(DN_HEADS)], axis=0)
        _, vjp = jax.vjp(functools.partial(_dn_chunk, cs=_dn_consts()), *args)
        gq, gk, gv, gz, gb, ga, gs, gal, gdt, gnw = vjp((dy, ds_scr[...]))
        ds_scr[...] = gs
        dbd = jnp.zeros((c, 128), F32)
        dal = jnp.zeros((1, 128), F32)
        ddt = jnp.zeros((1, 128), F32)
        for h in range(DN_HEADS):
            sl = slice(h * 128, (h + 1) * 128)
            dqkv_ref[:, sl] = gq[h]
            dqkv_ref[:, DN_W + h * 128:DN_W + (h + 1) * 128] = gk[h]
            dqkv_ref[:, 2 * DN_W + h * 128:2 * DN_W + (h + 1) * 128] = gv[h]
            dz_ref[:, sl] = gz[h]
            dbd = dbd + jnp.where(lane == h, gb[h], 0.0) + jnp.where(lane == DN_HEADS + h, ga[h], 0.0)
            dal = dal + jnp.where(lane == h, gal[h], 0.0)
            ddt = ddt + jnp.where(lane == h, gdt[h], 0.0)
        dbd_ref[...] = dbd
        dal_ref[...] += dal
        ddt_ref[...] += ddt
        dnw_ref[...] += gnw

        if na:
            @pl.when(pl.program_id(0) == n - 1)
            def _():
                for cp in copies:
                    cp.wait()

    rev = lambda i: n - 1 - i
    wide = lambda j: pl.BlockSpec((c, DN_W), lambda i: (rev(i), j))
    vec = pl.BlockSpec((1, 128), lambda i: (0, 0))
    any_space = pl.BlockSpec(memory_space=pl.ANY)
    return pl.pallas_call(
        body, name=name, grid=(n,),
        in_specs=[wide(0), wide(1), wide(2), wide(3), pl.BlockSpec((c, 128), lambda i: (rev(i), CB_BD)), vec, vec, vec,
                  pl.BlockSpec((1, DN_HEADS, 128, 128), lambda i: (rev(i), 0, 0, 0)), wide(0)] + [any_space] * na,
        out_specs=[pl.BlockSpec((c, 3 * DN_W), lambda i: (rev(i), 0)), wide(0),
                   pl.BlockSpec((c, 128), lambda i: (rev(i), 0)), vec, vec, vec] + [any_space] * na,
        out_shape=[jax.ShapeDtypeStruct((t, 3 * DN_W), F32), jax.ShapeDtypeStruct((t, DN_W), F32),
                   jax.ShapeDtypeStruct((t, 128), F32), jax.ShapeDtypeStruct((1, 128), F32),
                   jax.ShapeDtypeStruct((1, 128), F32), jax.ShapeDtypeStruct((1, 128), F32)] + _scatter_shapes(sums, kinds),
        scratch_shapes=[pltpu.VMEM((DN_HEADS, 128, 128), F32)] + ([pltpu.SemaphoreType.DMA((na, 3))] * 2 if na else []),
        compiler_params=_cp(("arbitrary",)),
    )(qkv, qkv, qkv, p, p, alog, dtb, nw, states, dycat, *sums)


def _rope(x, cosf, sins):
    return x * cosf + pltpu.roll(x, HEAD_DIM // 2, 1) * sins


def _rope_t(d, cosf, sins):
    return d * cosf + pltpu.roll(d * sins, HEAD_DIM // 2, 1)


def _swa_masks():
    b = SWA_BLOCK
    i = lax.broadcasted_iota(jnp.int32, (SWA_GROUP * b, b), 0) & (b - 1)
    j = lax.broadcasted_iota(jnp.int32, (SWA_GROUP * b, b), 1)
    return j > i, j <= i


def _swa_sink_col(sinks_ref, h):
    b = SWA_BLOCK
    r = lax.broadcasted_iota(jnp.int32, (SWA_GROUP * b, 1), 0)
    s = [sinks_ref[h * SWA_GROUP + g] for g in range(SWA_GROUP)]
    return jnp.where(r < b, s[0], jnp.where(r < 2 * b, s[1], s[2]))


def _swa_specs(t, h_first):
    nb = t // SWA_BLOCK

    def at(col, off):
        def imap(h, n):
            return (jnp.clip(n + off, 0, nb - 1), col(h))
        return pl.BlockSpec((SWA_BLOCK, 128), imap)
    return at


def _swa_fwd(p, cosf, sins, sinks, name):
    t = p.shape[0]
    b = SWA_BLOCK
    nb = t // b
    at = _swa_specs(t, None)
    scale = HEAD_DIM ** -0.5

    def body(q0, q1, q2, kp, kc, vp, vc, cc, sc, cp, sp, sinks_ref, o_ref, lse_ref):
        h, n = pl.program_id(0), pl.program_id(1)
        qs = jnp.concatenate([_rope(q[...], cc[...], sc[...]) for q in (q0, q1, q2)], axis=0)
        ks = jnp.concatenate([_rope(kp[...], cp[...], sp[...]), _rope(kc[...], cc[...], sc[...])], axis=0)
        vs = jnp.concatenate([vp[...], vc[...]], axis=0)
        mp, mc = _swa_masks()
        mask = jnp.concatenate([mp & (n > 0), mc], axis=1)
        s = jnp.where(mask, _dot(qs, ks, "nt") * scale, -1e30)
        sink = _swa_sink_col(sinks_ref, h)
        m = jnp.maximum(jnp.max(s, axis=1, keepdims=True), sink)
        e = jnp.exp(s - m)
        l = jnp.sum(e, axis=1, keepdims=True) + jnp.exp(sink - m)
        o = _dot(e, vs) / l
        lse = m + jnp.log(l)
        lane = lax.broadcasted_iota(jnp.int32, (1, 128), 1)
        tile = jnp.zeros((b, 128), F32)
        for g in range(SWA_GROUP):
            o_ref[:, g * 128:(g + 1) * 128] = o[g * b:(g + 1) * b]
            tile = tile + jnp.where(lane == g, lse[g * b:(g + 1) * b], 0.0)
        lse_ref[0] = tile

    qcol = lambda g: (lambda h: CB_SQ + h * SWA_GROUP + g)
    kcol, vcol, one = (lambda h: CB_SK + h), (lambda h: CB_SV + h), (lambda h: 0)
    in_specs = [at(qcol(0), 0), at(qcol(1), 0), at(qcol(2), 0), at(kcol, -1), at(kcol, 0), at(vcol, -1), at(vcol, 0),
                at(one, 0), at(one, 0), at(one, -1), at(one, -1), pl.BlockSpec(memory_space=pltpu.SMEM)]
    return pl.pallas_call(
        body, name=name, grid=(SWA_KV_HEADS, nb), in_specs=in_specs,
        out_specs=[pl.BlockSpec((b, SWA_GROUP * 128), lambda h, n: (n, h)), pl.BlockSpec((1, b, 128), lambda h, n: (h, n, 0))],
        out_shape=[jax.ShapeDtypeStruct((t, SWA_W), F32), jax.ShapeDtypeStruct((SWA_KV_HEADS, t, 128), F32)],
        compiler_params=_cp(("parallel", "parallel")),
    )(p, p, p, p, p, p, p, cosf, sins, cosf, sins, sinks)


def _swa_bwd(p, cosf, sins, sinks, o, lse, dycat, name):
    t = p.shape[0]
    b = SWA_BLOCK
    nb = t // b
    at = _swa_specs(t, None)
    scale = HEAD_DIM ** -0.5
    gb = SWA_GROUP * b

    def body(q0, q1, q2, r0, r1, r2, kp, kc, vp, vc, cc, sc, cp, sp, cn, sn, d0, d1, d2, e0, e1, e2,
             oc_ref, on_ref, lc_ref, ln_ref, sinks_ref, dq_ref, dk_ref, dv_ref, dsk_ref):
        h, n = pl.program_id(0), pl.program_id(1)
        lane = lax.broadcasted_iota(jnp.int32, (1, 128), 1)
        stack = lambda refs: jnp.concatenate([x[...] for x in refs], axis=0)
        q_c = jnp.concatenate([_rope(q[...], cc[...], sc[...]) for q in (q0, q1, q2)], axis=0)
        q_n = jnp.concatenate([_rope(q[...], cn[...], sn[...]) for q in (r0, r1, r2)], axis=0)
        k_p = _rope(kp[...], cp[...], sp[...])
        k_c = _rope(kc[...], cc[...], sc[...])
        do_c, do_n = stack((d0, d1, d2)), stack((e0, e1, e2))
        o_c = jnp.concatenate([oc_ref[:, g * 128:(g + 1) * 128] for g in range(SWA_GROUP)], axis=0)
        o_n = jnp.concatenate([on_ref[:, g * 128:(g + 1) * 128] for g in range(SWA_GROUP)], axis=0)
        lse_c = jnp.concatenate([_lane_pick(lc_ref[0], lane, g) for g in range(SWA_GROUP)], axis=0)
        lse_n = jnp.concatenate([_lane_pick(ln_ref[0], lane, g) for g in range(SWA_GROUP)], axis=0)
        dl_c = jnp.sum(do_c * o_c, axis=1, keepdims=True)
        dl_n = jnp.sum(do_n * o_n, axis=1, keepdims=True)
        mp, mc = _swa_masks()

        def pair(qr, kr, v, do, lse_, dl, mask):
            s = _dot(qr, kr, "nt") * scale
            pr = jnp.where(mask, jnp.exp(s - lse_), 0.0)
            ds = pr * (_dot(do, v, "nt") - dl) * scale
            return _dot(ds, kr), _dot(ds, qr, "tn"), _dot(pr, do, "tn")

        dq_a, _, _ = pair(q_c, k_p, vp[...], do_c, lse_c, dl_c, mp & (n > 0))
        dq_b, dk_b, dv_b = pair(q_c, k_c, vc[...], do_c, lse_c, dl_c, mc)
        _, dk_n, dv_n = pair(q_n, k_c, vc[...], do_n, lse_n, dl_n, mp & (n < nb - 1))
        dq = dq_a + dq_b
        for g in range(SWA_GROUP):
            dq_ref[:, g * 128:(g + 1) * 128] = _rope_t(dq[g * b:(g + 1) * b], cc[...], sc[...])
        dk_ref[...] = _rope_t(dk_b + dk_n, cc[...], sc[...])
        dv_ref[...] = dv_b + dv_n

        @pl.when(n == 0)
        def _():
            dsk_ref[...] = jnp.zeros_like(dsk_ref)

        w = -jnp.exp(_swa_sink_col(sinks_ref, h) - lse_c) * dl_c
        acc = jnp.zeros((1, 128), F32)
        for g in range(SWA_GROUP):
            acc = acc + jnp.where(lane == g, jnp.sum(w[g * b:(g + 1) * b], axis=0, keepdims=True), 0.0)
        dsk_ref[0] += jnp.broadcast_to(acc, (8, 128))

    qcol = lambda g: (lambda h: CB_SQ + h * SWA_GROUP + g)
    dcol = lambda g: (lambda h: (DN_W + POOL_W) // 128 + h * SWA_GROUP + g)
    kcol, vcol, one = (lambda h: CB_SK + h), (lambda h: CB_SV + h), (lambda h: 0)
    wide = lambda off: pl.BlockSpec((b, SWA_GROUP * 128), lambda h, n: (jnp.clip(n + off, 0, nb - 1), h))
    lspec = lambda off: pl.BlockSpec((1, b, 128), lambda h, n: (h, jnp.clip(n + off, 0, nb - 1), 0))
    in_specs = ([at(qcol(g), 0) for g in range(3)] + [at(qcol(g), 1) for g in range(3)]
                + [at(kcol, -1), at(kcol, 0), at(vcol, -1), at(vcol, 0)]
                + [at(one, 0), at(one, 0), at(one, -1), at(one, -1), at(one, 1), at(one, 1)]
                + [at(dcol(g), 0) for g in range(3)] + [at(dcol(g), 1) for g in range(3)]
                + [wide(0), wide(1), lspec(0), lspec(1), pl.BlockSpec(memory_space=pltpu.SMEM)])
    kv_out = pl.BlockSpec((b, 128), lambda h, n: (n, h))
    return pl.pallas_call(
        body, name=name, grid=(SWA_KV_HEADS, nb), in_specs=in_specs,
        out_specs=[wide(0), kv_out, kv_out, pl.BlockSpec((1, 8, 128), lambda h, n: (h, 0, 0))],
        out_shape=[jax.ShapeDtypeStruct((t, SWA_W), F32), jax.ShapeDtypeStruct((t, SWA_KV_W), F32),
                   jax.ShapeDtypeStruct((t, SWA_KV_W), F32), jax.ShapeDtypeStruct((SWA_KV_HEADS, 8, 128), F32)],
        compiler_params=_cp(("parallel", "arbitrary")),
    )(*([p] * 10), cosf, sins, cosf, sins, cosf, sins, *([dycat] * 6), o, o, lse, lse, sinks)


def _adam_math(w, g, m, v):
    m = ADAM_B1 * m + (1.0 - ADAM_B1) * g
    v = ADAM_B2 * v + (1.0 - ADAM_B2) * (g * g)
    m_hat = m / (1.0 - ADAM_B1 ** ADAM_STEP)
    v_hat = v / (1.0 - ADAM_B2 ** ADAM_STEP)
    return -ADAM_LR * (m_hat / (jnp.sqrt(v_hat) + ADAM_EPS) + ADAM_WD * w), m, v


def _adamw(w, g, m, v, name):
    shape = w.shape
    cols = shape[-1]
    rows = math.prod(shape[:-1])
    flat = lambda a: a.reshape(rows, cols)
    r = rows
    for cand in (512, 256, 128, 64, 32, 16, 8):
        if rows % cand == 0 and cand * cols * 4 <= (1 << 20):
            r = cand
            break

    def body(w_ref, g_ref, m_ref, v_ref, d_ref, nm_ref, nv_ref):
        d_ref[...], nm_ref[...], nv_ref[...] = _adam_math(w_ref[...], g_ref[...], m_ref[...], v_ref[...])

    spec = pl.BlockSpec((r, cols), lambda i: (i, 0))
    outs = pl.pallas_call(
        body, name=name, grid=(rows // r,), in_specs=[spec] * 4, out_specs=[spec] * 3,
        out_shape=[jax.ShapeDtypeStruct((rows, cols), F32)] * 3, compiler_params=_cp(("parallel",)),
    )(flat(w), flat(g), flat(m), flat(v))
    return tuple(o.reshape(shape) for o in outs)


BIG = ("w_in", "w_out", "ffn_w_up", "ffn_w_down")
CONV = ("dn_conv_w", "ffn_conv_w")
KIND = {"w_in": "col", "w_out": "row", "ffn_w_up": "col", "ffn_w_down": "row"}
SMALL = ("norm_mix_pre", "dn_a_log", "dn_dt_bias", "dn_norm_w", "pool_w", "pool_scale", "swa_sinks",
         "norm_mix_post", "norm_ffn_pre", "ffn_conv_b", "norm_ffn_post")
WEIGHTS = ("norm_mix_pre", "w_in", "dn_conv_w", "dn_a_log", "dn_dt_bias", "dn_norm_w", "pool_w", "pool_scale",
           "swa_sinks", "w_out", "norm_mix_post", "norm_ffn_pre", "ffn_w_up", "ffn_conv_w", "ffn_conv_b",
           "ffn_w_down", "norm_ffn_post")


def _pad_in(w):
    z = lambda n: jnp.zeros(w.shape[:-1] + (n,), w.dtype)
    return jnp.concatenate([w[..., :GATE_END], z(CB_POOL * 128 - GATE_END), w[..., GATE_END:],
                            z(IN_PAD - CB_POOL * 128 - (IN_TRUE - GATE_END))], axis=-1)


def _unpad_in(g):
    return jnp.concatenate([g[..., :GATE_END], g[..., CB_POOL * 128:CB_POOL * 128 + IN_TRUE - GATE_END]], axis=-1)


IN_SHARD = IN_TRUE // 4
IN_SHARD_PAD = -(-IN_SHARD // 128) * 128


def _chip_cols_to_true(w):
    by_chip = w.reshape(w.shape[:-1] + (4, IN_SHARD_PAD))[..., :IN_SHARD]
    return by_chip.reshape(w.shape[:-1] + (IN_TRUE,))


def _true_to_chip_cols(g):
    by_chip = g.reshape(g.shape[:-1] + (4, IN_SHARD))
    by_chip = jnp.pad(by_chip, [(0, 0)] * (by_chip.ndim - 1) + [(0, IN_SHARD_PAD - IN_SHARD)])
    return by_chip.reshape(g.shape[:-1] + (4 * IN_SHARD_PAD,))


def _lanes(v):
    return jnp.zeros((1, 128), F32).at[0, :v.shape[0]].set(v)


def _rope_tables(positions):
    inv_freq = 1.0 / (ROPE_THETA ** (jnp.arange(0, HEAD_DIM, 2, dtype=F32) / HEAD_DIM))
    ang = positions.astype(F32)[:, None] * inv_freq
    cos, sin = jnp.cos(ang), jnp.sin(ang)
    return jnp.concatenate([cos, cos], axis=-1), jnp.concatenate([-sin, sin], axis=-1)


class _GradReduce:
    def __init__(self, place, shard_shapes):
        self.place = place
        self.out = {k: lax.empty(shard_shapes[k], F32) for k in BIG}
        self.pending = None

    def layer_done(self, l, dw):
        parts = {"w_in": _true_to_chip_cols(_unpad_in(dw["w_in"]))[None], "ffn_w_up": dw["ffn_w_up"][None],
                 "w_out": dw["w_out"].reshape(4, -1, dw["w_out"].shape[1]),
                 "ffn_w_down": dw["ffn_w_down"].reshape(4, -1, dw["ffn_w_down"].shape[1])}
        mine = [parts[k] for k in BIG]
        got = _swap_sibling(mine, f"l{l}_grads_to_sibling")
        self.pending = (l, [_chip_sum(a, b, self.place, f"l{l}_chip_sum_{k}") for k, a, b in zip(BIG, mine, got)])

    def carry(self):
        return (self.pending[1], [KIND[k] for k in BIG]) if self.pending is not None else None

    def arrived(self, got):
        l, sums = self.pending
        self.pending = None
        for k, own, g in zip(BIG, sums, got):
            self.out[k] = _owner_sum(own, g, KIND[k], self.place, (self.out[k], l), f"l{l}_owner_sum_{k}")

    def finish(self):
        l, sums = self.pending
        self.arrived(_scatter_chips(sums, [KIND[k] for k in BIG], f"l{l}_grads_to_owner"))
        return dict(zip(BIG, _join_halves([self.out[k] for k in BIG], "grads_join")))


def _local_step(x, positions, target, w, reduce=None):
    depth = w["w_out"].shape[0]
    t = x.shape[0]
    cosf, sins = _rope_tables(positions)
    saved = []
    for l in range(depth):
        nm = f"l{l}_"
        n1, n2, n3, n4 = (w[k][l][None] for k in ("norm_mix_pre", "norm_mix_post", "norm_ffn_pre", "norm_ffn_post"))
        alog, dtb, dnw = _lanes(w["dn_a_log"][l]), _lanes(w["dn_dt_bias"][l]), w["dn_norm_w"][l][None]
        psc, cb = w["pool_scale"][l][None], w["ffn_conv_b"][l][None]
        h = _norm_fwd(x, n1, nm + "norm1")
        p = _mm(h, w["w_in"], "nn", F32, nm + "in_proj", b_pick=("slab", l))
        qkv = _dn_pre_fwd(p, w["dn_conv_w"][l], nm + "dn_pre")
        y_dn, st = _dn_fwd(qkv, p, alog, dtb, dnw, nm + "dn")
        y_pool = _pool_fwd(p, w["pool_w"][l], psc, nm + "pool")
        y_swa, lse = _swa_fwd(p, cosf, sins, w["swa_sinks"][l], nm + "swa")
        ycat = jnp.concatenate([y_dn, y_pool, y_swa], axis=1).astype(BF16)
        mix = _mm(ycat, w["w_out"], "nn", F32, nm + "out_proj", b_pick=("slab", l))
        x1 = _resnorm_fwd(x, mix, n2, nm + "res1")
        h2 = _norm_fwd(x1, n3, nm + "norm3")
        up = _mm(h2, w["ffn_w_up"], "nn", F32, nm + "ffn_up", b_pick=("slab", l))
        act = _ffn_act_fwd(up, w["ffn_conv_w"][l], cb, nm + "ffn_act")
        f = _mm(act, w["ffn_w_down"], "nn", F32, nm + "ffn_down", b_pick=("slab", l))
        x2 = _resnorm_fwd(x1, f, n4, nm + "res2")
        saved.append(dict(x=x, h=h, p=p, qkv=qkv, st=st, y_swa=y_swa, lse=lse, ycat=ycat, mix=mix, x1=x1, h2=h2,
                          up=up, act=act, f=f, n=(n1, n2, n3, n4), alog=alog, dtb=dtb, dnw=dnw, psc=psc, cb=cb))
        x = x2
    loss, dx = _loss_head(x, target, "loss_head")
    grads = {k: [None] * depth for k in WEIGHTS}
    for l in reversed(range(depth)):
        nm, s = f"l{l}_b_", saved[l]
        n1, n2, n3, n4 = s["n"]
        df, g4 = _norm_bwd(s["f"], n4, dx, None, BF16, nm + "res2")
        dact = _mm(df, w["ffn_w_down"], "nt", F32, nm + "ffn_down_dx", b_pick=("slab", l))
        grads["ffn_w_down"][l] = _mm(s["act"], df, "tn", BF16, nm + "ffn_down_dw")
        dup, dcw, dcb = _ffn_act_bwd(s["up"], w["ffn_conv_w"][l], s["cb"], dact, nm + "ffn_act")
        grads["ffn_conv_w"][l] = jnp.concatenate([dcw[0], dcw[1]], axis=1)
        grads["ffn_conv_b"][l] = jnp.concatenate([dcb[0], dcb[1]], axis=1)[0]
        grads["ffn_w_up"][l] = _mm(s["h2"], dup, "tn", BF16, nm + "ffn_up_dw", b_pick="split")
        dh2 = _mm(dup, w["ffn_w_up"], "nt", BF16, nm + "ffn_up_dx", a_pick="split", b_pick=("slab", l))
        dx1, g3 = _norm_bwd(s["x1"], n3, dh2, dx, F32, nm + "norm3")
        dmix, g2 = _norm_bwd(s["mix"], n2, dx1, None, BF16, nm + "res1")
        grads["w_out"][l] = _mm(s["ycat"], dmix, "tn", BF16, nm + "out_proj_dw")
        dycat = _mm(dmix, w["w_out"], "nt", F32, nm + "out_proj_dx", b_pick=("slab", l))
        carry = reduce.carry() if reduce is not None else None
        res = _dn_bwd(s["qkv"], s["p"], s["alog"], s["dtb"], s["dnw"], s["st"], dycat, nm + "dn", carry=carry)
        dqkv, dz, dbd, gal, gdt, gnw = res[:6]
        if carry is not None:
            reduce.arrived(res[6:])
        dpq, gconv = _dn_pre_bwd(s["p"], w["dn_conv_w"][l], dqkv, nm + "dn_pre")
        dpool, gpw, gpsc = _pool_bwd(s["p"], w["pool_w"][l], s["psc"], dycat, nm + "pool")
        dsq, dsk, dsv, gsk = _swa_bwd(s["p"], cosf, sins, w["swa_sinks"][l], s["y_swa"], s["lse"], dycat, nm + "swa")
        dp = jnp.concatenate([dpq, dz, dbd, dpool, dsq, dsk, dsv, jnp.zeros((t, 128), F32)], axis=1).astype(BF16)
        grads["w_in"][l] = _mm(s["h"], dp, "tn", BF16, nm + "in_proj_dw")
        dh = _mm(dp, w["w_in"], "nt", BF16, nm + "in_proj_dx", b_pick=("slab", l))
        dx, g1 = _norm_bwd(s["x"], n1, dh, dx1, F32, nm + "norm1")
        if reduce is not None:
            reduce.layer_done(l, {k: grads[k][l] for k in BIG})
        grads["norm_mix_pre"][l], grads["norm_mix_post"][l] = g1[0], g2[0]
        grads["norm_ffn_pre"][l], grads["norm_ffn_post"][l] = g3[0], g4[0]
        grads["dn_conv_w"][l] = gconv
        grads["dn_a_log"][l], grads["dn_dt_bias"][l], grads["dn_norm_w"][l] = gal[0, :DN_HEADS], gdt[0, :DN_HEADS], gnw[0]
        grads["pool_w"][l], grads["pool_scale"][l] = gpw, gpsc[0]
        grads["swa_sinks"][l] = gsk[:, 0, :SWA_GROUP].reshape(SWA_HEADS)
    return loss, dx, grads


def _flat2(a):
    return a.reshape(math.prod(a.shape[:-1]), a.shape[-1])


def _ew_rows(rows, cols, n_arrays):
    for cand in (512, 256, 128, 64, 32, 16):
        if rows % cand == 0 and cand * cols * 4 * n_arrays <= (8 << 20):
            return cand
    return rows


def _spread_shard(a, kind, place, dtype, name):
    depth, rows, cols = a.shape
    r = _ew_rows(rows, cols, 2)
    nb = rows // r

    def body(s_ref, a_ref, o_ref):
        o_ref[...] = a_ref[...].astype(o_ref.dtype)

    if kind == "row":
        out_spec = pl.BlockSpec((None, r, cols), lambda l, i, s: (l, s[0] * nb + i, 0))
        out_shape = (depth, 4 * rows, cols)
    else:
        out_spec = pl.BlockSpec((None, r, cols), lambda l, i, s: (l, i, s[0]))
        out_shape = (depth, rows, 4 * cols)
    return pl.pallas_call(
        body, name=name,
        grid_spec=pltpu.PrefetchScalarGridSpec(
            num_scalar_prefetch=1, grid=(depth, nb),
            in_specs=[pl.BlockSpec((None, r, cols), lambda l, i, s: (l, i, 0))], out_specs=out_spec),
        out_shape=jax.ShapeDtypeStruct(out_shape, dtype), compiler_params=_cp(("parallel", "parallel")),
    )(place, a)


def _chip_sum(mine, sib, place, name):
    parts, rows, cols = sib.shape
    r = _ew_rows(rows, cols, 3)
    nb = rows // r

    def body(s_ref, a_ref, b_ref, o_ref):
        o_ref[...] = (a_ref[...].astype(F32) + b_ref[...].astype(F32)).astype(o_ref.dtype)

    spec = pl.BlockSpec((None, r, cols), lambda j, i, s: (j, i, 0))
    return pl.pallas_call(
        body, name=name,
        grid_spec=pltpu.PrefetchScalarGridSpec(
            num_scalar_prefetch=1, grid=(parts, nb),
            in_specs=[pl.BlockSpec((None, r, cols), lambda j, i, s: (j, s[1] * nb + i, 0)), spec], out_specs=spec),
        out_shape=jax.ShapeDtypeStruct(sib.shape, BF16), compiler_params=_cp(("parallel", "parallel")),
    )(place, mine, sib)


def _sum_slots(a, name):
    s = a.shape[0]
    a3 = a.reshape(s, math.prod(a.shape[1:-1]), a.shape[-1])
    _, rows, cols = a3.shape
    r = _ew_rows(rows, cols, s + 1)

    def body(a_ref, o_ref):
        acc = a_ref[0].astype(F32)
        for k in range(1, s):
            acc = acc + a_ref[k].astype(F32)
        o_ref[...] = acc

    return pl.pallas_call(body, name=name, grid=(rows // r,),
                          in_specs=[pl.BlockSpec((s, r, cols), lambda i: (0, i, 0))],
                          out_specs=pl.BlockSpec((r, cols), lambda i: (i, 0)),
                          out_shape=jax.ShapeDtypeStruct((rows, cols), F32), compiler_params=_cp(("parallel",)),
                          )(a3).reshape(a.shape[1:])


def _owner_sum(own, got, kind, place, into, name):
    buf, slab = into
    _, rows, cols = got.shape
    r = _ew_rows(rows, cols, 6)
    nb = rows // r

    def body(s_ref, own_ref, got_ref, buf_ref, o_ref):
        acc = own_ref[...].astype(F32)
        for k in range(3):
            acc = acc + got_ref[k].astype(F32)
        o_ref[...] = acc

    if kind == "row":
        own_spec = pl.BlockSpec((None, r, cols), lambda i, s: (s[0], i, 0))
    else:
        own_spec = pl.BlockSpec((None, r, cols), lambda i, s: (0, i, s[0]))
    return pl.pallas_call(
        body, name=name,
        grid_spec=pltpu.PrefetchScalarGridSpec(
            num_scalar_prefetch=1, grid=(nb,),
            in_specs=[own_spec, pl.BlockSpec((3, r, cols), lambda i, s: (0, i, 0)), pl.BlockSpec(memory_space=pl.ANY)],
            out_specs=pl.BlockSpec((None, r, cols), lambda i, s: (slab, s[1] * nb + i, 0))),
        out_shape=jax.ShapeDtypeStruct(buf.shape, buf.dtype), input_output_aliases={3: 0},
        compiler_params=_cp(("parallel",)),
    )(place, own, got, buf)


MESH = pl.DeviceIdType.MESH
ANY = pl.BlockSpec(memory_space=pl.ANY)


def _place():
    x, y, c = lax.axis_index("x"), lax.axis_index("y"), lax.axis_index("c")
    chips = [(1 - x, y), (x, 1 - y), (1 - x, 1 - y)]
    return x, y, c, chips


def _part(ref, kind, chip, layers):
    if kind == "row":
        width = ref.shape[1] // 4
        return ref.at[layers, pl.ds(pl.multiple_of(chip * width, 16), width), :]
    width = ref.shape[2] // 4
    return ref.at[layers, :, pl.ds(pl.multiple_of(chip * width, 128), width)]


def _gather_chips(arrs, kinds, name):
    na = len(arrs)
    l2 = arrs[0].shape[0] // 2

    def body(*refs):
        outs = refs[na:2 * na]
        send1, recv1, send2, recv2 = refs[2 * na:]
        x, y, c, chips = _place()
        me = 2 * x + y
        half = pl.ds(c * l2, l2)
        other = pl.ds((1 - c) * l2, l2)

        def first(k, j):
            px, py = chips[j]
            mine = _part(outs[k], kinds[k], me, half)
            return pltpu.make_async_remote_copy(mine, mine, send1.at[k, j], recv1.at[k, j],
                                                device_id=(px, py, c), device_id_type=MESH)

        def landed(k, j):
            px, py = chips[j]
            return _part(outs[k], kinds[k], 2 * px + py, half)

        def passed(k, j):
            return pltpu.make_async_remote_copy(landed(k, j), landed(k, j), send2.at[k, j], recv2.at[k, j],
                                                device_id=(x, y, 1 - c), device_id_type=MESH)

        def from_sibling(k, j):
            px, py = chips[j]
            dst = _part(outs[k], kinds[k], 2 * px + py, other)
            return pltpu.make_async_remote_copy(dst, dst, send2.at[k, j], recv2.at[k, j],
                                                device_id=(x, y, 1 - c), device_id_type=MESH)

        for k in range(na):
            for j in range(3):
                first(k, j).start()
        for k in range(na):
            for j in range(3):
                first(k, j).wait_recv()
                passed(k, j).start()
        for k in range(na):
            for j in range(3):
                from_sibling(k, j).wait_recv()
        for k in range(na):
            for j in range(3):
                first(k, j).wait_send()
                passed(k, j).wait_send()

    return pl.pallas_call(
        body, name=name, in_specs=[ANY] * na, out_specs=[ANY] * na,
        out_shape=[jax.ShapeDtypeStruct(a.shape, a.dtype) for a in arrs],
        input_output_aliases={k: k for k in range(na)},
        scratch_shapes=[pltpu.SemaphoreType.DMA((na, 3))] * 4,
    )(*arrs)


def _swap_sibling(arrs, name):
    na = len(arrs)

    def body(*refs):
        ins, outs, send, recv = refs[:na], refs[na:2 * na], refs[2 * na], refs[2 * na + 1]
        x, y, c, _ = _place()
        cps = []
        for k in range(na):
            h = ins[k].shape[1] // 2
            cps.append(pltpu.make_async_remote_copy(ins[k].at[:, pl.ds(pl.multiple_of((1 - c) * h, 16), h), :], outs[k],
                                                    send.at[k], recv.at[k], device_id=(x, y, 1 - c), device_id_type=MESH))
        for cp in cps:
            cp.start()
        for cp in cps:
            cp.wait()

    return pl.pallas_call(
        body, name=name, in_specs=[ANY] * na, out_specs=[ANY] * na,
        out_shape=[jax.ShapeDtypeStruct((a.shape[0], a.shape[1] // 2, a.shape[2]), a.dtype) for a in arrs],
        scratch_shapes=[pltpu.SemaphoreType.DMA((na,))] * 2,
    )(*arrs)


def _scatter_shapes(sums, kinds):
    return [jax.ShapeDtypeStruct((3, a.shape[1], a.shape[2] if kind == "row" else a.shape[2] // 4), a.dtype)
            for a, kind in zip(sums, kinds)]


def _scatter_copies(srcs, dsts, kinds, send, recv):
    x, y, c, chips = _place()
    cps = []
    for k, (src, dst) in enumerate(zip(srcs, dsts)):
        for j, (px, py) in enumerate(chips):
            chip = 2 * px + py
            if kinds[k] == "row":
                part = src.at[chip]
            else:
                width = src.shape[2] // 4
                part = src.at[0, :, pl.ds(pl.multiple_of(chip * width, 128), width)]
            cps.append(pltpu.make_async_remote_copy(part, dst.at[j], send.at[k, j], recv.at[k, j],
                                                    device_id=(px, py, c), device_id_type=MESH))
    return cps


def _scatter_chips(sums, kinds, name):
    na = len(sums)

    def body(*refs):
        cps = _scatter_copies(refs[:na], refs[na:2 * na], kinds, refs[2 * na], refs[2 * na + 1])
        for cp in cps:
            cp.start()
        for cp in cps:
            cp.wait()

    return pl.pallas_call(
        body, name=name, in_specs=[ANY] * na, out_specs=[ANY] * na, out_shape=_scatter_shapes(sums, kinds),
        scratch_shapes=[pltpu.SemaphoreType.DMA((na, 3))] * 2,
    )(*sums)


def _join_halves(arrs, name):
    na = len(arrs)

    def body(*refs):
        outs, send, recv = refs[na:2 * na], refs[2 * na], refs[2 * na + 1]
        x, y, c, _ = _place()
        halves = [a.shape[1] // 2 for a in arrs]
        mine = [outs[k].at[:, pl.ds(pl.multiple_of(c * h, 8), h), :] for k, h in enumerate(halves)]
        theirs = [outs[k].at[:, pl.ds(pl.multiple_of((1 - c) * h, 8), h), :] for k, h in enumerate(halves)]
        cps = [pltpu.make_async_remote_copy(mine[k], mine[k], send.at[k], recv.at[k],
                                            device_id=(x, y, 1 - c), device_id_type=MESH) for k in range(na)]
        for cp in cps:
            cp.start()
        for k, cp in enumerate(cps):
            cp.wait_send()
            pltpu.make_async_remote_copy(theirs[k], theirs[k], send.at[k], recv.at[k],
                                         device_id=(x, y, 1 - c), device_id_type=MESH).wait_recv()

    return pl.pallas_call(
        body, name=name, in_specs=[ANY] * na, out_specs=[ANY] * na,
        out_shape=[jax.ShapeDtypeStruct(a.shape, a.dtype) for a in arrs],
        input_output_aliases={k: k for k in range(na)},
        scratch_shapes=[pltpu.SemaphoreType.DMA((na,))] * 2,
    )(*arrs)


def _gather_all(a, name):
    def body(a_ref, o_ref, send, recv, local):
        x, y, c, _ = _place()
        me = 4 * x + 2 * y + c
        mine = pltpu.make_async_copy(a_ref, o_ref.at[me], local)
        mine.start()
        cps = []
        for j in range(1, 8):
            peer = (x ^ (j >> 2), y ^ ((j >> 1) & 1), c ^ (j & 1))
            cps.append(pltpu.make_async_remote_copy(a_ref, o_ref.at[me], send.at[j - 1], recv.at[j - 1],
                                                    device_id=peer, device_id_type=MESH))
        for cp in cps:
            cp.start()
        for cp in cps:
            cp.wait()
        mine.wait()

    return pl.pallas_call(
        body, name=name, in_specs=[ANY], out_specs=ANY,
        out_shape=jax.ShapeDtypeStruct((8,) + a.shape, a.dtype),
        scratch_shapes=[pltpu.SemaphoreType.DMA((7,)), pltpu.SemaphoreType.DMA((7,)), pltpu.SemaphoreType.DMA],
    )(a)


def _pack(parts):
    flat = jnp.concatenate([p.reshape(-1) for p in parts])
    n = flat.shape[0]
    rows = -(-n // 1024) * 8
    return jnp.pad(flat, (0, rows * 128 - n)).reshape(rows, 128)


def _unpack(buf, like):
    flat, out, off = buf.reshape(-1), [], 0
    for p in like:
        out.append(flat[off:off + p.size].reshape(p.shape))
        off += p.size
    return out


def kernel(x, positions, norm_mix_pre, w_in, dn_conv_w, dn_a_log, dn_dt_bias, dn_norm_w, pool_w, pool_scale, swa_sinks, w_out, norm_mix_post, norm_ffn_pre, ffn_w_up, ffn_conv_w, ffn_conv_b, ffn_w_down, norm_ffn_post, loss_target, m_norm_mix_pre, m_w_in, m_dn_conv_w, m_dn_a_log, m_dn_dt_bias, m_dn_norm_w, m_pool_w, m_pool_scale, m_swa_sinks, m_w_out, m_norm_mix_post, m_norm_ffn_pre, m_ffn_w_up, m_ffn_conv_w, m_ffn_conv_b, m_ffn_w_down, m_norm_ffn_post, v_norm_mix_pre, v_w_in, v_dn_conv_w, v_dn_a_log, v_dn_dt_bias, v_dn_norm_w, v_pool_w, v_pool_scale, v_swa_sinks, v_w_out, v_norm_mix_post, v_norm_ffn_pre, v_ffn_w_up, v_ffn_conv_w, v_ffn_conv_b, v_ffn_w_down, v_norm_ffn_post):
    wts = dict(zip(WEIGHTS, (norm_mix_pre, w_in, dn_conv_w, dn_a_log, dn_dt_bias, dn_norm_w, pool_w, pool_scale, swa_sinks,
                             w_out, norm_mix_post, norm_ffn_pre, ffn_w_up, ffn_conv_w, ffn_conv_b, ffn_w_down, norm_ffn_post)))
    mom = dict(zip(WEIGHTS, (m_norm_mix_pre, m_w_in, m_dn_conv_w, m_dn_a_log, m_dn_dt_bias, m_dn_norm_w, m_pool_w, m_pool_scale,
                             m_swa_sinks, m_w_out, m_norm_mix_post, m_norm_ffn_pre, m_ffn_w_up, m_ffn_conv_w, m_ffn_conv_b,
                             m_ffn_w_down, m_norm_ffn_post)))
    var = dict(zip(WEIGHTS, (v_norm_mix_pre, v_w_in, v_dn_conv_w, v_dn_a_log, v_dn_dt_bias, v_dn_norm_w, v_pool_w, v_pool_scale,
                             v_swa_sinks, v_w_out, v_norm_mix_post, v_norm_ffn_pre, v_ffn_w_up, v_ffn_conv_w, v_ffn_conv_b,
                             v_ffn_w_down, v_norm_ffn_post)))
    c = lax.axis_index("c")
    chip = 2 * lax.axis_index("x") + lax.axis_index("y")
    place = jnp.stack([chip, c]).astype(jnp.int32)
    kinds = [KIND[k] for k in BIG]

    shards = dict(wts, w_in=jnp.pad(w_in, ((0, 0), (0, 0), (0, IN_SHARD_PAD - IN_SHARD))))
    spread = [_spread_shard(shards[k], KIND[k], place, BF16, "cast_" + k) for k in BIG]
    w = dict(wts)
    w.update(zip(BIG, _gather_chips(spread, kinds, "gather_weights")))
    w["w_in"] = _pad_in(_chip_cols_to_true(w["w_in"]))
    conv_like = [wts[k] for k in CONV]
    conv_all = _gather_all(_pack(conv_like), "gather_conv")
    for i, k in enumerate(CONV):
        w[k] = jnp.concatenate([_unpack(conv_all[2 * j], conv_like)[i] for j in range(4)], axis=2)

    reduce = _GradReduce(place, {k: shards[k].shape for k in BIG})
    loss, dx, grads = _local_step(x[0], positions[0], loss_target[0], w, reduce)
    loss = lax.psum(loss[0, 0], ("x", "y", "c"))
    g_big = reduce.finish()
    g_big["w_in"] = g_big["w_in"][..., :IN_SHARD]

    small_like = [wts[k] for k in SMALL]
    full_like = small_like + [w[k] for k in CONV]
    g_buf = _sum_slots(_gather_all(_pack([jnp.stack(grads[k]) for k in SMALL + CONV]), "gather_small"), "sum_small")
    g_small = dict(zip(SMALL + CONV, _unpack(g_buf, full_like)))
    for k in CONV:
        width = wts[k].shape[2]
        g_small[k] = lax.dynamic_slice_in_dim(g_small[k], chip * width, width, 2)
    pk = lambda d: _pack([d[k] for k in SMALL + CONV])
    upd = _adamw(pk(wts), pk(g_small), pk(mom), pk(var), "adam_small")
    upd_small = [dict(zip(SMALL + CONV, _unpack(b, small_like + conv_like))) for b in upd]

    g_all, d_all, m_all, v_all = {}, {}, {}, {}
    for k in WEIGHTS:
        if k in BIG:
            g_all[k] = g_big[k]
            d_all[k], m_all[k], v_all[k] = _adamw(wts[k], g_big[k], mom[k], var[k], "adam_" + k)
        else:
            g_all[k], d_all[k], m_all[k], v_all[k] = g_small[k], upd_small[0][k], upd_small[1][k], upd_small[2][k]
    return (loss, dx[None], *[g_all[k] for k in WEIGHTS], *[d_all[k] for k in WEIGHTS],
            *[m_all[k] for k in WEIGHTS], *[v_all[k] for k in WEIGHTS])
```

```python
import functools
import math

import jax
import jax.numpy as jnp
from jax import lax
from jax.experimental import pallas as pl
from jax.experimental.pallas import tpu as pltpu

F32 = jnp.float32
BF16 = jnp.bfloat16

HEAD_DIM = 128
DN_HEADS = 6
DN_CONV = 4
DN_CHUNK = 64
POOL_GROUPS = 4
SWA_HEADS = 6
SWA_KV_HEADS = 2
SWA_GROUP = SWA_HEADS // SWA_KV_HEADS
SWA_BLOCK = 128
ROPE_THETA = 10000.0
FFN_CONV = 3
NORM_EPS = 1e-6
DN_W = DN_HEADS * HEAD_DIM
POOL_W = POOL_GROUPS * HEAD_DIM
SWA_W = SWA_HEADS * HEAD_DIM
SWA_KV_W = SWA_KV_HEADS * HEAD_DIM
MIX_W = DN_W + POOL_W + SWA_W
IN_TRUE = 3 * DN_W + DN_W + 2 * DN_HEADS + POOL_W + SWA_W + 2 * SWA_KV_W
GATE_END = 4 * DN_W + 2 * DN_HEADS
CB_Z = 18
CB_BD = 24
CB_POOL = 25
CB_SQ = 29
CB_SK = 35
CB_SV = 37
IN_PAD = 40 * 128
ADAM_LR, ADAM_B1, ADAM_B2, ADAM_EPS, ADAM_WD, ADAM_STEP = 0.001, 0.9, 0.999, 1e-08, 0.01, 10

VMEM_LIMIT = 48 * 1024 * 1024
MM_TK_MAX = 2816
HIGH = lax.Precision.HIGHEST


def _cp(sem):
    return pltpu.CompilerParams(dimension_semantics=sem, vmem_limit_bytes=VMEM_LIMIT)


def _tile(n, prefs):
    for p in prefs:
        if n % p == 0:
            return p
    return n


def _rows(t):
    return _tile(t, (256, 128))


_DN = {"nn": (((1,), (0,)), ((), ())), "nt": (((1,), (1,)), ((), ())), "tn": (((0,), (0,)), ((), ()))}


def _mm_operand(arr, pick, block, idx):
    if pick is None:
        return pl.BlockSpec(block, idx)
    if pick == "split":
        per = arr.shape[2] // block[1]

        def split_idx(i, j, kk):
            r, c = idx(i, j, kk)
            return lax.div(c, per), r, lax.rem(c, per)

        return pl.BlockSpec((None,) + block, split_idx)
    slab = pick[1]
    return pl.BlockSpec((None,) + block, lambda i, j, kk: (slab,) + idx(i, j, kk))


def _mm(a, b, mode, out_dtype, name, a_pick=None, b_pick=None, carry=None):
    def dims(arr, pick):
        r, c = arr.shape[-2:]
        return (r, c * arr.shape[0]) if pick == "split" else (r, c)

    (a0, a1), (b0, b1) = dims(a, a_pick), dims(b, b_pick)
    k, m = (a0, a1) if mode == "tn" else (a1, a0)
    n = b0 if mode == "nt" else b1
    lim = lambda arr, pick, is_last, full: arr.shape[2] if (pick == "split" and is_last) else full
    tm = _tile(lim(a, a_pick, mode == "tn", m), (1024, 512, 256, 128))
    tn = _tile(lim(b, b_pick, mode != "nt", n), (1024, 512, 256, 128))
    k_lim = min(lim(a, a_pick, mode != "tn", k), lim(b, b_pick, mode == "nt", k))
    tk = max([d for d in range(128, min(k_lim, MM_TK_MAX) + 1, 128) if k_lim % d == 0], default=k_lim)
    nk = k // tk

    grid = (m // tm, n // tn, nk)

    def body(a_ref, b_ref, *rest):
        if carry is not None:
            _, o_ref, w_ref, *scratch = rest
            send, recv = scratch[-2:]
            scratch = scratch[:-2]
            copies = _gather_copies(w_ref, carry[1], send, recv)
            step = (pl.program_id(0) * grid[1] + pl.program_id(1)) * grid[2] + pl.program_id(2)

            @pl.when(step == 0)
            def _():
                for cp in copies:
                    cp.start()
        else:
            o_ref, *scratch = rest
        part = lax.dot_general(a_ref[...], b_ref[...], _DN[mode], preferred_element_type=F32)
        if nk == 1:
            o_ref[...] = part.astype(o_ref.dtype)
        else:
            acc_ref, = scratch
            kk = pl.program_id(2)

            @pl.when(kk == 0)
            def _():
                acc_ref[...] = part

            @pl.when(kk > 0)
            def _():
                acc_ref[...] += part

            @pl.when(kk == nk - 1)
            def _():
                o_ref[...] = acc_ref[...].astype(o_ref.dtype)
        if carry is not None:
            @pl.when(step == grid[0] * grid[1] * grid[2] - 1)
            def _():
                for cp in copies:
                    cp.wait()

    if mode == "tn":
        a_spec = _mm_operand(a, a_pick, (tk, tm), lambda i, j, kk: (kk, i))
    else:
        a_spec = _mm_operand(a, a_pick, (tm, tk), lambda i, j, kk: (i, kk))
    if mode == "nt":
        b_spec = _mm_operand(b, b_pick, (tn, tk), lambda i, j, kk: (j, kk))
    else:
        b_spec = _mm_operand(b, b_pick, (tk, tn), lambda i, j, kk: (kk, j))
    scratch = [pltpu.VMEM((tm, tn), F32)] if nk > 1 else []
    out_spec = pl.BlockSpec((tm, tn), lambda i, j, kk: (i, j))
    out_shape = jax.ShapeDtypeStruct((m, n), out_dtype)
    if carry is None:
        return pl.pallas_call(
            body, name=name, grid=grid, in_specs=[a_spec, b_spec], out_specs=out_spec, out_shape=out_shape,
            scratch_shapes=scratch, compiler_params=_cp(("parallel", "parallel", "arbitrary")),
        )(a, b)
    any_space = pl.BlockSpec(memory_space=pl.ANY)
    return pl.pallas_call(
        body, name=name, grid=grid, in_specs=[a_spec, b_spec, any_space], out_specs=[out_spec, any_space],
        out_shape=[out_shape, jax.ShapeDtypeStruct(carry[0].shape, carry[0].dtype)], input_output_aliases={2: 1},
        scratch_shapes=scratch + [pltpu.SemaphoreType.DMA((3,))] * 2,
        compiler_params=_cp(("arbitrary", "arbitrary", "arbitrary")),
    )(a, b, carry[0])


def _rms(x, w):
    return x * lax.rsqrt(jnp.mean(x * x, axis=-1, keepdims=True) + NORM_EPS) * w


def _norm_fwd(x, w, name):
    t, d = x.shape
    r = _rows(t)

    def body(x_ref, w_ref, h_ref):
        h_ref[...] = _rms(x_ref[...], w_ref[...]).astype(h_ref.dtype)

    return pl.pallas_call(
        body, name=name, grid=(t // r,),
        in_specs=[pl.BlockSpec((r, d), lambda i: (i, 0)), pl.BlockSpec((1, d), lambda i: (0, 0))],
        out_specs=pl.BlockSpec((r, d), lambda i: (i, 0)),
        out_shape=jax.ShapeDtypeStruct((t, d), BF16), compiler_params=_cp(("parallel",)),
    )(x, w)


def _resnorm_fwd(x, y, w, name):
    t, d = x.shape
    r = _rows(t)

    def body(x_ref, y_ref, w_ref, o_ref):
        o_ref[...] = x_ref[...] + _rms(y_ref[...], w_ref[...])

    return pl.pallas_call(
        body, name=name, grid=(t // r,),
        in_specs=[pl.BlockSpec((r, d), lambda i: (i, 0)), pl.BlockSpec((r, d), lambda i: (i, 0)),
                  pl.BlockSpec((1, d), lambda i: (0, 0))],
        out_specs=pl.BlockSpec((r, d), lambda i: (i, 0)),
        out_shape=jax.ShapeDtypeStruct((t, d), F32), compiler_params=_cp(("parallel",)),
    )(x, y, w)


def _norm_bwd(x, w, dh, add, out_dtype, name):
    t, d = x.shape
    r = _rows(t)
    has_add = add is not None

    def body(*refs):
        if has_add:
            x_ref, w_ref, dh_ref, add_ref, dx_ref, dw_ref = refs
        else:
            x_ref, w_ref, dh_ref, dx_ref, dw_ref = refs
        xv = x_ref[...]
        g = dh_ref[...].astype(F32)
        rs = lax.rsqrt(jnp.mean(xv * xv, axis=-1, keepdims=True) + NORM_EPS)
        xh = xv * rs
        gw = g * w_ref[...]
        dx = rs * (gw - xh * jnp.mean(gw * xh, axis=-1, keepdims=True))
        if has_add:
            dx = dx + add_ref[...]
        dx_ref[...] = dx.astype(dx_ref.dtype)

        @pl.when(pl.program_id(0) == 0)
        def _():
            dw_ref[...] = jnp.zeros_like(dw_ref)

        dw_ref[...] += jnp.sum(g * xh, axis=0, keepdims=True)

    row = pl.BlockSpec((r, d), lambda i: (i, 0))
    vec = pl.BlockSpec((1, d), lambda i: (0, 0))
    ins = [x, w, dh] + ([add] if has_add else [])
    return pl.pallas_call(
        body, name=name, grid=(t // r,),
        in_specs=[row, vec, row] + ([row] if has_add else []),
        out_specs=[row, vec],
        out_shape=[jax.ShapeDtypeStruct((t, d), out_dtype), jax.ShapeDtypeStruct((1, d), F32)],
        compiler_params=_cp(("arbitrary",)),
    )(*ins)


def _loss_head(y, target, name):
    t, d = y.shape
    r = _rows(t)

    def body(y_ref, t_ref, l_ref, g_ref):
        e = y_ref[...] - t_ref[...]
        g_ref[...] = e * (1.0 / d)

        @pl.when(pl.program_id(0) == 0)
        def _():
            l_ref[...] = jnp.zeros_like(l_ref)

        l_ref[...] += jnp.sum(e * e) * (0.5 / d)

    row = pl.BlockSpec((r, d), lambda i: (i, 0))
    return pl.pallas_call(
        body, name=name, grid=(t // r,), in_specs=[row, row],
        out_specs=[pl.BlockSpec((1, 128), lambda i: (0, 0)), row],
        out_shape=[jax.ShapeDtypeStruct((1, 128), F32), jax.ShapeDtypeStruct((t, d), F32)],
        compiler_params=_cp(("arbitrary",)),
    )(y, target)


def _down(x, s):
    return x if s == 0 else pltpu.roll(x, s, 0)


def _up(x, s):
    return x if s == 0 else pltpu.roll(x, x.shape[0] - s, 0)


def _halo(t, r, hh, tc, col):
    q = r // hh
    last = t // hh - 1
    tile = pl.BlockSpec((r, tc), lambda j, i: (i, col(j)))
    prev = pl.BlockSpec((hh, tc), lambda j, i: (jnp.maximum(i * q - 1, 0), col(j)))
    nxt = pl.BlockSpec((hh, tc), lambda j, i: (jnp.minimum((i + 1) * q, last), col(j)))
    return tile, prev, nxt


def _sig(x):
    return 1.0 / (1.0 + jnp.exp(-x))


def _dsilu(x, s):
    return s * (1.0 + x * (1.0 - s))


def _dn_pre_fwd(p, conv_w, name):
    t = p.shape[0]
    r = _rows(t)

    def body(x_ref, xp_ref, w_ref, o_ref):
        j, i = pl.program_id(0), pl.program_id(1)
        xe = jnp.concatenate([jnp.where(i == 0, 0.0, xp_ref[...]), x_ref[...]], axis=0)
        c = sum(_down(xe, DN_CONV - 1 - k) * w_ref[pl.ds(k, 1), :] for k in range(DN_CONV))[8:]
        a = c * _sig(c)
        for h in range(DN_HEADS):
            ah = a[:, h * 128:(h + 1) * 128]
            fac = lax.rsqrt(jnp.sum(ah * ah, axis=-1, keepdims=True) + NORM_EPS)
            o_ref[:, h * 128:(h + 1) * 128] = ah * jnp.where(j == 0, fac * HEAD_DIM ** -0.5, jnp.where(j == 1, fac, 1.0))

    tile, prev, _ = _halo(t, r, 8, DN_W, lambda j: j)
    return pl.pallas_call(
        body, name=name, grid=(3, t // r),
        in_specs=[tile, prev, pl.BlockSpec((DN_CONV, DN_W), lambda j, i: (0, j))],
        out_specs=tile, out_shape=jax.ShapeDtypeStruct((t, 3 * DN_W), F32),
        compiler_params=_cp(("parallel", "parallel")),
    )(p, p, conv_w)


def _dn_pre_bwd(p, conv_w, dqkv, name):
    t = p.shape[0]
    r = _rows(t)
    ni = t // r

    def body(x_ref, xp_ref, xn_ref, w_ref, d_ref, dn_ref, dx_ref, dw_ref):
        j, i = pl.program_id(0), pl.program_id(1)
        xe = jnp.concatenate([jnp.where(i == 0, 0.0, xp_ref[...]), x_ref[...], xn_ref[...]], axis=0)
        de = jnp.concatenate([jnp.zeros((8, DN_W), F32), d_ref[...], jnp.where(i == ni - 1, 0.0, dn_ref[...])], axis=0)
        xs = [_down(xe, DN_CONV - 1 - k) for k in range(DN_CONV)]
        c = sum(xs[k] * w_ref[pl.ds(k, 1), :] for k in range(DN_CONV))
        s = _sig(c)
        a = c * s
        das = []
        for h in range(DN_HEADS):
            ah, dh = a[:, h * 128:(h + 1) * 128], de[:, h * 128:(h + 1) * 128]
            fac = lax.rsqrt(jnp.sum(ah * ah, axis=-1, keepdims=True) + NORM_EPS)
            dnorm = fac * dh - ah * (fac * fac * fac) * jnp.sum(dh * ah, axis=-1, keepdims=True)
            das.append(jnp.where(j == 0, dnorm * HEAD_DIM ** -0.5, jnp.where(j == 1, dnorm, dh)))
        dc = jnp.concatenate(das, axis=1) * _dsilu(c, s)
        dx_ref[...] = sum(_up(dc, DN_CONV - 1 - k) * w_ref[pl.ds(k, 1), :] for k in range(DN_CONV))[8:8 + r]

        @pl.when(i == 0)
        def _():
            dw_ref[...] = jnp.zeros_like(dw_ref)

        for k in range(DN_CONV):
            dw_ref[pl.ds(k, 1), :] += jnp.sum((dc * xs[k])[8:8 + r], axis=0, keepdims=True)

    tile, prev, nxt = _halo(t, r, 8, DN_W, lambda j: j)
    wspec = pl.BlockSpec((DN_CONV, DN_W), lambda j, i: (0, j))
    return pl.pallas_call(
        body, name=name, grid=(3, ni),
        in_specs=[tile, prev, nxt, wspec, tile, nxt],
        out_specs=[tile, wspec],
        out_shape=[jax.ShapeDtypeStruct((t, 3 * DN_W), F32), jax.ShapeDtypeStruct((DN_CONV, 3 * DN_W), F32)],
        compiler_params=_cp(("parallel", "arbitrary")),
    )(p, p, p, conv_w, dqkv, dqkv)


def _ffn_act_fwd(up, cw, cb, name):
    t, f2 = up.shape
    f = f2 // 2
    r = _rows(t)
    tc = _tile(f, (512, 256, 128))
    nj = f // tc

    def body(a_ref, ap_ref, b_ref, bp_ref, wa_ref, wb_ref, ca_ref, cb_ref, o_ref):
        i = pl.program_id(1)

        def conv(x_ref, xp_ref, w_ref, c_ref):
            xe = jnp.concatenate([jnp.where(i == 0, 0.0, xp_ref[...]), x_ref[...]], axis=0)
            return sum(_down(xe, FFN_CONV - 1 - k) * w_ref[pl.ds(k, 1), :] for k in range(FFN_CONV))[8:] + c_ref[...]

        ua = conv(a_ref, ap_ref, wa_ref, ca_ref)
        ub = conv(b_ref, bp_ref, wb_ref, cb_ref)
        o_ref[...] = (ua * _sig(ua) * ub).astype(o_ref.dtype)

    ta, pa, _ = _halo(t, r, 8, tc, lambda j: j)
    tb, pb, _ = _halo(t, r, 8, tc, lambda j: j + nj)
    wa = pl.BlockSpec((FFN_CONV, tc), lambda j, i: (0, j))
    wb = pl.BlockSpec((FFN_CONV, tc), lambda j, i: (0, j + nj))
    ca = pl.BlockSpec((1, tc), lambda j, i: (0, j))
    cbs = pl.BlockSpec((1, tc), lambda j, i: (0, j + nj))
    return pl.pallas_call(
        body, name=name, grid=(nj, t // r),
        in_specs=[ta, pa, tb, pb, wa, wb, ca, cbs], out_specs=ta,
        out_shape=jax.ShapeDtypeStruct((t, f), BF16), compiler_params=_cp(("parallel", "parallel")),
    )(up, up, up, up, cw, cw, cb, cb)


def _ffn_act_bwd(up, cw, cb, dact, name):
    t, f2 = up.shape
    f = f2 // 2
    r = _rows(t)
    ni = t // r
    tc = _tile(f, (512, 256, 128))
    nj = f // tc

    def body(a_ref, ap_ref, an_ref, b_ref, bp_ref, bn_ref, wa_ref, wb_ref, ca_ref, cb_ref, d_ref, dn_ref,
             du_ref, dw_ref, dc_ref):
        i = pl.program_id(1)
        dua_ref, dub_ref, dwa_ref, dwb_ref, dca_ref, dcb_ref = (du_ref.at[0], du_ref.at[1], dw_ref.at[0], dw_ref.at[1],
                                                                  dc_ref.at[0], dc_ref.at[1])

        def ext(x_ref, xp_ref, xn_ref):
            return jnp.concatenate([jnp.where(i == 0, 0.0, xp_ref[...]), x_ref[...], xn_ref[...]], axis=0)

        ae, be = ext(a_ref, ap_ref, an_ref), ext(b_ref, bp_ref, bn_ref)
        as_ = [_down(ae, FFN_CONV - 1 - k) for k in range(FFN_CONV)]
        bs_ = [_down(be, FFN_CONV - 1 - k) for k in range(FFN_CONV)]
        ua = sum(as_[k] * wa_ref[pl.ds(k, 1), :] for k in range(FFN_CONV)) + ca_ref[...]
        ub = sum(bs_[k] * wb_ref[pl.ds(k, 1), :] for k in range(FFN_CONV)) + cb_ref[...]
        de = jnp.concatenate([jnp.zeros((8, tc), F32), d_ref[...].astype(F32),
                              jnp.where(i == ni - 1, 0.0, dn_ref[...].astype(F32))], axis=0)
        s = _sig(ua)
        dua = de * ub * _dsilu(ua, s)
        dub = de * ua * s
        dua_ref[...] = sum(_up(dua, FFN_CONV - 1 - k) * wa_ref[pl.ds(k, 1), :] for k in range(FFN_CONV))[8:8 + r].astype(dua_ref.dtype)
        dub_ref[...] = sum(_up(dub, FFN_CONV - 1 - k) * wb_ref[pl.ds(k, 1), :] for k in range(FFN_CONV))[8:8 + r].astype(dub_ref.dtype)

        @pl.when(i == 0)
        def _():
            dw_ref[...] = jnp.zeros_like(dw_ref)
            dc_ref[...] = jnp.zeros_like(dc_ref)

        for k in range(FFN_CONV):
            dwa_ref[pl.ds(k, 1), :] += jnp.sum((dua * as_[k])[8:8 + r], axis=0, keepdims=True)
            dwb_ref[pl.ds(k, 1), :] += jnp.sum((dub * bs_[k])[8:8 + r], axis=0, keepdims=True)
        dca_ref[...] += jnp.sum(dua[8:8 + r], axis=0, keepdims=True)
        dcb_ref[...] += jnp.sum(dub[8:8 + r], axis=0, keepdims=True)

    ta, pa, na = _halo(t, r, 8, tc, lambda j: j)
    tb, pb, nb = _halo(t, r, 8, tc, lambda j: j + nj)
    wa = pl.BlockSpec((FFN_CONV, tc), lambda j, i: (0, j))
    wb = pl.BlockSpec((FFN_CONV, tc), lambda j, i: (0, j + nj))
    ca = pl.BlockSpec((1, tc), lambda j, i: (0, j))
    cbs = pl.BlockSpec((1, tc), lambda j, i: (0, j + nj))
    return pl.pallas_call(
        body, name=name, grid=(nj, ni),
        in_specs=[ta, pa, na, tb, pb, nb, wa, wb, ca, cbs, ta, na],
        out_specs=[pl.BlockSpec((2, r, tc), lambda j, i: (0, i, j)), pl.BlockSpec((2, FFN_CONV, tc), lambda j, i: (0, 0, j)),
                   pl.BlockSpec((2, 1, tc), lambda j, i: (0, 0, j))],
        out_shape=[jax.ShapeDtypeStruct((2, t, f), BF16), jax.ShapeDtypeStruct((2, FFN_CONV, f), F32),
                   jax.ShapeDtypeStruct((2, 1, f), F32)],
        compiler_params=_cp(("parallel", "arbitrary")),
    )(up, up, up, up, up, up, cw, cw, cb, cb, dact, dact)


def _pool_pick(g, vals):
    return jnp.where(g == 0, vals[0], jnp.where(g == 1, vals[1], jnp.where(g == 2, vals[2], vals[3])))


def _pool_pre(xe, g, t0):
    s1 = xe + _down(xe, 1)
    s2 = s1 + _down(s1, 2)
    s3 = s2 + _down(s2, 4)
    s4 = s3 + _down(s3, 8)
    r = xe.shape[0] - 16
    pos = (t0 + lax.broadcasted_iota(jnp.int32, (r, 1), 0)).astype(F32)
    cnt = jnp.minimum(pos + 1.0, _pool_pick(g, (2.0, 4.0, 8.0, 16.0)))
    return _pool_pick(g, (s1, s2, s3, s4))[16:] / cnt - xe[16:]


def _pool_fwd(p, pool_w, scale, name):
    t = p.shape[0]
    r = _tile(t, (1024, 256, 128))

    def body(x_ref, xp_ref, w_ref, sc_ref, o_ref):
        g, i = pl.program_id(0), pl.program_id(1)
        xe = jnp.concatenate([jnp.where(i == 0, 0.0, xp_ref[...]), x_ref[...]], axis=0)
        pre = _pool_pre(xe, g, i * r)
        o_ref[...] = jnp.dot(pre, w_ref[0], preferred_element_type=F32) * sc_ref[...]

    tile, prev, _ = _halo(t, r, 16, 128, lambda j: CB_POOL + j)
    return pl.pallas_call(
        body, name=name, grid=(POOL_GROUPS, t // r),
        in_specs=[tile, prev, pl.BlockSpec((1, 128, 128), lambda j, i: (j, 0, 0)), pl.BlockSpec((1, 128), lambda j, i: (0, j))],
        out_specs=pl.BlockSpec((r, 128), lambda j, i: (i, j)),
        out_shape=jax.ShapeDtypeStruct((t, POOL_W), F32), compiler_params=_cp(("parallel", "parallel")),
    )(p, p, pool_w, scale)


def _pool_bwd(p, pool_w, scale, dycat, name):
    t = p.shape[0]
    r = _tile(t, (1024, 256, 128))
    ni = t // r

    def body(x_ref, xp_ref, w_ref, sc_ref, d_ref, dn_ref, dx_ref, dw_ref, dsc_ref):
        g, i = pl.program_id(0), pl.program_id(1)
        xe = jnp.concatenate([jnp.where(i == 0, 0.0, xp_ref[...]), x_ref[...]], axis=0)
        pre = _pool_pre(xe, g, i * r)
        w = w_ref[0]
        dy = d_ref[...]
        dye = jnp.concatenate([dy, jnp.where(i == ni - 1, 0.0, dn_ref[...])], axis=0)
        dpre = lax.dot_general(dye * sc_ref[...], w, _DN["nt"], preferred_element_type=F32)
        pos = (i * r + lax.broadcasted_iota(jnp.int32, (r + 16, 1), 0)).astype(F32)
        dm = dpre / jnp.minimum(pos + 1.0, _pool_pick(g, (2.0, 4.0, 8.0, 16.0)))
        a1 = dm + _up(dm, 1)
        a2 = a1 + _up(a1, 2)
        a3 = a2 + _up(a2, 4)
        a4 = a3 + _up(a3, 8)
        dx_ref[...] = (_pool_pick(g, (a1, a2, a3, a4)) - dpre)[:r]

        @pl.when(i == 0)
        def _():
            dw_ref[...] = jnp.zeros_like(dw_ref)
            dsc_ref[...] = jnp.zeros_like(dsc_ref)

        dw_ref[0] += lax.dot_general(pre, dy * sc_ref[...], _DN["tn"], preferred_element_type=F32)
        dsc_ref[...] += jnp.sum(dy * jnp.dot(pre, w, preferred_element_type=F32), axis=0, keepdims=True)

    tile, prev, _ = _halo(t, r, 16, 128, lambda j: CB_POOL + j)
    dtile, _, dnxt = _halo(t, r, 16, 128, lambda j: DN_W // 128 + j)
    wspec = pl.BlockSpec((1, 128, 128), lambda j, i: (j, 0, 0))
    sspec = pl.BlockSpec((1, 128), lambda j, i: (0, j))
    return pl.pallas_call(
        body, name=name, grid=(POOL_GROUPS, ni),
        in_specs=[tile, prev, wspec, sspec, dtile, dnxt],
        out_specs=[pl.BlockSpec((r, 128), lambda j, i: (i, j)), wspec, sspec],
        out_shape=[jax.ShapeDtypeStruct((t, POOL_W), F32), jax.ShapeDtypeStruct((POOL_GROUPS, 128, 128), F32),
                   jax.ShapeDtypeStruct((1, POOL_W), F32)],
        compiler_params=_cp(("parallel", "arbitrary")),
    )(p, p, pool_w, scale, dycat, dycat)


_DNB = {"nn": (((2,), (1,)), ((0,), (0,))), "nt": (((2,), (2,)), ((0,), (0,))), "tn": (((1,), (1,)), ((0,), (0,)))}


def _dot(a, b, mode="nn", precision=None):
    dn = _DNB[mode] if a.ndim == 3 else _DN[mode]
    return lax.dot_general(a, b, dn, precision=precision, preferred_element_type=F32)


@functools.partial(jax.custom_vjp, nondiff_argnums=(2,))
def _bdot(a, b, mode):
    return _dot(a.astype(BF16), b.astype(BF16), mode)


def _bdot_fwd(a, b, mode):
    return _bdot(a, b, mode), (a, b)


def _bdot_bwd(mode, res, g):
    a, b = res
    if mode == "nn":
        return _bdot(g, b, "nt"), _bdot(a, g, "tn")
    if mode == "nt":
        return _bdot(g, b, "nn"), _bdot(g, a, "tn")
    return _bdot(b, g, "nt"), _bdot(a, g, "nn")


_bdot.defvjp(_bdot_fwd, _bdot_bwd)


def _dn_consts():
    c = DN_CHUNK
    ii = lax.broadcasted_iota(jnp.int32, (c, c), 0)
    jj = lax.broadcasted_iota(jnp.int32, (c, c), 1)
    one, zero = jnp.ones((c, c), F32), jnp.zeros((c, c), F32)
    return dict(ltri=jnp.where(ii >= jj, one, zero), utri=jnp.where(ii <= jj, one, zero), ones=one,
                causal=ii >= jj, strict=ii > jj, eye=jnp.where(ii == jj, one, zero))


def _dn_chunk(q, k, v, z, bcol, acol, s_in, alog, dtb, nw, cs):
    c = DN_CHUNK
    hh = q.shape[0]
    per_head = lambda m: jnp.broadcast_to(m, (hh, c, c))
    beta = _sig(bcol)
    xa = acol + dtb
    g = -jnp.exp(alog) * (jnp.maximum(xa, 0.0) + jnp.log(1.0 + jnp.exp(-jnp.abs(xa))))
    gb = jnp.broadcast_to(g, (hh, c, HEAD_DIM))
    gbc = jnp.broadcast_to(g, (hh, c, c))
    gc = _dot(per_head(cs["ltri"]), gb, precision=HIGH)
    gcol = _dot(per_head(cs["ltri"]), gbc, precision=HIGH)
    grow = _dot(per_head(cs["ones"]), gbc * cs["utri"], precision=HIGH)
    decay = jnp.exp(jnp.where(cs["causal"], gcol - grow, -1e30))
    kb = k * beta
    vb = v * beta
    nil = -jnp.where(cs["strict"], _bdot(kb, k, "nt") * decay, 0.0)
    inv = cs["eye"] + nil
    powk = nil
    for _ in range(int(math.log2(c)) - 1):
        powk = _dot(powk, powk)
        inv = _dot(inv, cs["eye"] + powk)
    eg = jnp.exp(gc)
    u = _bdot(inv, vb, "nn")
    w = _bdot(inv, kb * eg, "nn")
    a = _bdot(q, k, "nt") * decay
    v_new = u - _bdot(w, s_in, "nn")
    o = _bdot(q * eg, s_in, "nn") + _bdot(a, v_new, "nn")
    glast = jnp.sum(gb, axis=1, keepdims=True)
    s_out = s_in * jnp.exp(glast) + _bdot(k * jnp.exp(glast - gc), v_new, "tn")
    on = o * lax.rsqrt(jnp.mean(o * o, axis=-1, keepdims=True) + NORM_EPS) * nw
    return on * (z * _sig(z)), s_out


def _lane_pick(x, lane, idx):
    return jnp.sum(jnp.where(lane == idx, x, 0.0), axis=1, keepdims=True)


def _dn_load(q_ref, k_ref, v_ref, z_ref, bd_ref, al_ref, dt_ref, nw_ref, s_in):
    lane = lax.broadcasted_iota(jnp.int32, (1, 128), 1)
    bd, al, dt = bd_ref[...], al_ref[...], dt_ref[...]
    heads = range(DN_HEADS)
    wide = lambda ref: jnp.stack([ref[:, h * 128:(h + 1) * 128] for h in heads], axis=0)
    col = lambda x, off: jnp.stack([_lane_pick(x, lane, off + h) for h in heads], axis=0)
    return (wide(q_ref), wide(k_ref), wide(v_ref), wide(z_ref), col(bd, 0), col(bd, DN_HEADS), s_in,
            col(al, 0), col(dt, 0), nw_ref[...])


def _dn_fwd(qkv, p, alog, dtb, nw, name):
    t = qkv.shape[0]
    c = DN_CHUNK
    n = t // c

    def body(q_ref, k_ref, v_ref, z_ref, bd_ref, al_ref, dt_ref, nw_ref, y_ref, ss_ref, s_scr):
        @pl.when(pl.program_id(0) == 0)
        def _():
            s_scr[...] = jnp.zeros_like(s_scr)

        s_in = s_scr[...]
        y, s_out = _dn_chunk(*_dn_load(q_ref, k_ref, v_ref, z_ref, bd_ref, al_ref, dt_ref, nw_ref, s_in), _dn_consts())
        ss_ref[0] = s_in
        s_scr[...] = s_out
        for h in range(DN_HEADS):
            y_ref[:, h * 128:(h + 1) * 128] = y[h]

    wide = lambda j: pl.BlockSpec((c, DN_W), lambda i: (i, j))
    vec = pl.BlockSpec((1, 128), lambda i: (0, 0))
    return pl.pallas_call(
        body, name=name, grid=(n,),
        in_specs=[wide(0), wide(1), wide(2), wide(3), pl.BlockSpec((c, 128), lambda i: (i, CB_BD)), vec, vec, vec],
        out_specs=[wide(0), pl.BlockSpec((1, DN_HEADS, 128, 128), lambda i: (i, 0, 0, 0))],
        out_shape=[jax.ShapeDtypeStruct((t, DN_W), F32), jax.ShapeDtypeStruct((n, DN_HEADS, 128, 128), F32)],
        scratch_shapes=[pltpu.VMEM((DN_HEADS, 128, 128), F32)],
        compiler_params=_cp(("arbitrary",)),
    )(qkv, qkv, qkv, p, p, alog, dtb, nw)


def _dn_bwd(qkv, p, alog, dtb, nw, states, dycat, name, carry=None):
    t = qkv.shape[0]
    c = DN_CHUNK
    n = t // c
    sums, kinds = carry if carry is not None else ((), ())
    na = len(sums)

    def body(*refs):
        q_ref, k_ref, v_ref, z_ref, bd_ref, al_ref, dt_ref, nw_ref, ss_ref, dy_ref = refs[:10]
        dqkv_ref, dz_ref, dbd_ref, dal_ref, ddt_ref, dnw_ref = refs[10 + na:16 + na]
        ds_scr = refs[16 + 2 * na]
        if na:
            copies = _scatter_copies(refs[10:10 + na], refs[16 + na:16 + 2 * na], kinds, *refs[17 + 2 * na:])

        @pl.when(pl.program_id(0) == 0)
        def _():
            ds_scr[...] = jnp.zeros_like(ds_scr)
            dal_ref[...] = jnp.zeros_like(dal_ref)
            ddt_ref[...] = jnp.zeros_like(ddt_ref)
            dnw_ref[...] = jnp.zeros_like(dnw_ref)
            if na:
                for cp in copies:
                    cp.start()

        lane = lax.broadcasted_iota(jnp.int32, (1, 128), 1)
        args = _dn_load(q_ref, k_ref, v_ref, z_ref, bd_ref, al_ref, dt_ref, nw_ref, ss_ref[0])
        dy = jnp.stack([dy_ref[:, h * 128:(h + 1) * 128] for h in range(DN_HEADS)], axis=0)
        _, vjp = jax.vjp(functools.partial(_dn_chunk, cs=_dn_consts()), *args)
        gq, gk, gv, gz, gb, ga, gs, gal, gdt, gnw = vjp((dy, ds_scr[...]))
        ds_scr[...] = gs
        dbd = jnp.zeros((c, 128), F32)
        dal = jnp.zeros((1, 128), F32)
        ddt = jnp.zeros((1, 128), F32)
        for h in range(DN_HEADS):
            sl = slice(h * 128, (h + 1) * 128)
            dqkv_ref[:, sl] = gq[h]
            dqkv_ref[:, DN_W + h * 128:DN_W + (h + 1) * 128] = gk[h]
            dqkv_ref[:, 2 * DN_W + h * 128:2 * DN_W + (h + 1) * 128] = gv[h]
            dz_ref[:, sl] = gz[h]
            dbd = dbd + jnp.where(lane == h, gb[h], 0.0) + jnp.where(lane == DN_HEADS + h, ga[h], 0.0)
            dal = dal + jnp.where(lane == h, gal[h], 0.0)
            ddt = ddt + jnp.where(lane == h, gdt[h], 0.0)
        dbd_ref[...] = dbd
        dal_ref[...] += dal
        ddt_ref[...] += ddt
        dnw_ref[...] += gnw

        if na:
            @pl.when(pl.program_id(0) == n - 1)
            def _():
                for cp in copies:
                    cp.wait()

    rev = lambda i: n - 1 - i
    wide = lambda j: pl.BlockSpec((c, DN_W), lambda i: (rev(i), j))
    vec = pl.BlockSpec((1, 128), lambda i: (0, 0))
    any_space = pl.BlockSpec(memory_space=pl.ANY)
    return pl.pallas_call(
        body, name=name, grid=(n,),
        in_specs=[wide(0), wide(1), wide(2), wide(3), pl.BlockSpec((c, 128), lambda i: (rev(i), CB_BD)), vec, vec, vec,
                  pl.BlockSpec((1, DN_HEADS, 128, 128), lambda i: (rev(i), 0, 0, 0)), wide(0)] + [any_space] * na,
        out_specs=[pl.BlockSpec((c, 3 * DN_W), lambda i: (rev(i), 0)), wide(0),
                   pl.BlockSpec((c, 128), lambda i: (rev(i), 0)), vec, vec, vec] + [any_space] * na,
        out_shape=[jax.ShapeDtypeStruct((t, 3 * DN_W), F32), jax.ShapeDtypeStruct((t, DN_W), F32),
                   jax.ShapeDtypeStruct((t, 128), F32), jax.ShapeDtypeStruct((1, 128), F32),
                   jax.ShapeDtypeStruct((1, 128), F32), jax.ShapeDtypeStruct((1, 128), F32)] + _scatter_shapes(sums, kinds),
        scratch_shapes=[pltpu.VMEM((DN_HEADS, 128, 128), F32)] + ([pltpu.SemaphoreType.DMA((na, 3))] * 2 if na else []),
        compiler_params=_cp(("arbitrary",)),
    )(qkv, qkv, qkv, p, p, alog, dtb, nw, states, dycat, *sums)


def _rope(x, cosf, sins):
    return x * cosf + pltpu.roll(x, HEAD_DIM // 2, 1) * sins


def _rope_t(d, cosf, sins):
    return d * cosf + pltpu.roll(d * sins, HEAD_DIM // 2, 1)


def _swa_masks():
    b = SWA_BLOCK
    i = lax.broadcasted_iota(jnp.int32, (SWA_GROUP * b, b), 0) & (b - 1)
    j = lax.broadcasted_iota(jnp.int32, (SWA_GROUP * b, b), 1)
    return j > i, j <= i


def _swa_sink_col(sinks_ref, h):
    b = SWA_BLOCK
    r = lax.broadcasted_iota(jnp.int32, (SWA_GROUP * b, 1), 0)
    s = [sinks_ref[h * SWA_GROUP + g] for g in range(SWA_GROUP)]
    return jnp.where(r < b, s[0], jnp.where(r < 2 * b, s[1], s[2]))


def _swa_specs(t, h_first):
    nb = t // SWA_BLOCK

    def at(col, off):
        def imap(h, n):
            return (jnp.clip(n + off, 0, nb - 1), col(h))
        return pl.BlockSpec((SWA_BLOCK, 128), imap)
    return at


def _swa_fwd(p, cosf, sins, sinks, name):
    t = p.shape[0]
    b = SWA_BLOCK
    nb = t // b
    at = _swa_specs(t, None)
    scale = HEAD_DIM ** -0.5

    def body(q0, q1, q2, kp, kc, vp, vc, cc, sc, cp, sp, sinks_ref, o_ref, lse_ref):
        h, n = pl.program_id(0), pl.program_id(1)
        qs = jnp.concatenate([_rope(q[...], cc[...], sc[...]) for q in (q0, q1, q2)], axis=0)
        ks = jnp.concatenate([_rope(kp[...], cp[...], sp[...]), _rope(kc[...], cc[...], sc[...])], axis=0)
        vs = jnp.concatenate([vp[...], vc[...]], axis=0)
        mp, mc = _swa_masks()
        mask = jnp.concatenate([mp & (n > 0), mc], axis=1)
        s = jnp.where(mask, _dot(qs, ks, "nt") * scale, -1e30)
        sink = _swa_sink_col(sinks_ref, h)
        m = jnp.maximum(jnp.max(s, axis=1, keepdims=True), sink)
        e = jnp.exp(s - m)
        l = jnp.sum(e, axis=1, keepdims=True) + jnp.exp(sink - m)
        o = _dot(e, vs) / l
        lse = m + jnp.log(l)
        lane = lax.broadcasted_iota(jnp.int32, (1, 128), 1)
        tile = jnp.zeros((b, 128), F32)
        for g in range(SWA_GROUP):
            o_ref[:, g * 128:(g + 1) * 128] = o[g * b:(g + 1) * b]
            tile = tile + jnp.where(lane == g, lse[g * b:(g + 1) * b], 0.0)
        lse_ref[0] = tile

    qcol = lambda g: (lambda h: CB_SQ + h * SWA_GROUP + g)
    kcol, vcol, one = (lambda h: CB_SK + h), (lambda h: CB_SV + h), (lambda h: 0)
    in_specs = [at(qcol(0), 0), at(qcol(1), 0), at(qcol(2), 0), at(kcol, -1), at(kcol, 0), at(vcol, -1), at(vcol, 0),
                at(one, 0), at(one, 0), at(one, -1), at(one, -1), pl.BlockSpec(memory_space=pltpu.SMEM)]
    return pl.pallas_call(
        body, name=name, grid=(SWA_KV_HEADS, nb), in_specs=in_specs,
        out_specs=[pl.BlockSpec((b, SWA_GROUP * 128), lambda h, n: (n, h)), pl.BlockSpec((1, b, 128), lambda h, n: (h, n, 0))],
        out_shape=[jax.ShapeDtypeStruct((t, SWA_W), F32), jax.ShapeDtypeStruct((SWA_KV_HEADS, t, 128), F32)],
        compiler_params=_cp(("parallel", "parallel")),
    )(p, p, p, p, p, p, p, cosf, sins, cosf, sins, sinks)


def _swa_bwd(p, cosf, sins, sinks, o, lse, dycat, name):
    t = p.shape[0]
    b = SWA_BLOCK
    nb = t // b
    at = _swa_specs(t, None)
    scale = HEAD_DIM ** -0.5
    gb = SWA_GROUP * b

    def body(q0, q1, q2, r0, r1, r2, kp, kc, vp, vc, cc, sc, cp, sp, cn, sn, d0, d1, d2, e0, e1, e2,
             oc_ref, on_ref, lc_ref, ln_ref, sinks_ref, dq_ref, dk_ref, dv_ref, dsk_ref):
        h, n = pl.program_id(0), pl.program_id(1)
        lane = lax.broadcasted_iota(jnp.int32, (1, 128), 1)
        stack = lambda refs: jnp.concatenate([x[...] for x in refs], axis=0)
        q_c = jnp.concatenate([_rope(q[...], cc[...], sc[...]) for q in (q0, q1, q2)], axis=0)
        q_n = jnp.concatenate([_rope(q[...], cn[...], sn[...]) for q in (r0, r1, r2)], axis=0)
        k_p = _rope(kp[...], cp[...], sp[...])
        k_c = _rope(kc[...], cc[...], sc[...])
        do_c, do_n = stack((d0, d1, d2)), stack((e0, e1, e2))
        o_c = jnp.concatenate([oc_ref[:, g * 128:(g + 1) * 128] for g in range(SWA_GROUP)], axis=0)
        o_n = jnp.concatenate([on_ref[:, g * 128:(g + 1) * 128] for g in range(SWA_GROUP)], axis=0)
        lse_c = jnp.concatenate([_lane_pick(lc_ref[0], lane, g) for g in range(SWA_GROUP)], axis=0)
        lse_n = jnp.concatenate([_lane_pick(ln_ref[0], lane, g) for g in range(SWA_GROUP)], axis=0)
        dl_c = jnp.sum(do_c * o_c, axis=1, keepdims=True)
        dl_n = jnp.sum(do_n * o_n, axis=1, keepdims=True)
        mp, mc = _swa_masks()

        def pair(qr, kr, v, do, lse_, dl, mask):
            s = _dot(qr, kr, "nt") * scale
            pr = jnp.where(mask, jnp.exp(s - lse_), 0.0)
            ds = pr * (_dot(do, v, "nt") - dl) * scale
            return _dot(ds, kr), _dot(ds, qr, "tn"), _dot(pr, do, "tn")

        dq_a, _, _ = pair(q_c, k_p, vp[...], do_c, lse_c, dl_c, mp & (n > 0))
        dq_b, dk_b, dv_b = pair(q_c, k_c, vc[...], do_c, lse_c, dl_c, mc)
        _, dk_n, dv_n = pair(q_n, k_c, vc[...], do_n, lse_n, dl_n, mp & (n < nb - 1))
        dq = dq_a + dq_b
        for g in range(SWA_GROUP):
            dq_ref[:, g * 128:(g + 1) * 128] = _rope_t(dq[g * b:(g + 1) * b], cc[...], sc[...])
        dk_ref[...] = _rope_t(dk_b + dk_n, cc[...], sc[...])
        dv_ref[...] = dv_b + dv_n

        @pl.when(n == 0)
        def _():
            dsk_ref[...] = jnp.zeros_like(dsk_ref)

        w = -jnp.exp(_swa_sink_col(sinks_ref, h) - lse_c) * dl_c
        acc = jnp.zeros((1, 128), F32)
        for g in range(SWA_GROUP):
            acc = acc + jnp.where(lane == g, jnp.sum(w[g * b:(g + 1) * b], axis=0, keepdims=True), 0.0)
        dsk_ref[0] += jnp.broadcast_to(acc, (8, 128))

    qcol = lambda g: (lambda h: CB_SQ + h * SWA_GROUP + g)
    dcol = lambda g: (lambda h: (DN_W + POOL_W) // 128 + h * SWA_GROUP + g)
    kcol, vcol, one = (lambda h: CB_SK + h), (lambda h: CB_SV + h), (lambda h: 0)
    wide = lambda off: pl.BlockSpec((b, SWA_GROUP * 128), lambda h, n: (jnp.clip(n + off, 0, nb - 1), h))
    lspec = lambda off: pl.BlockSpec((1, b, 128), lambda h, n: (h, jnp.clip(n + off, 0, nb - 1), 0))
    in_specs = ([at(qcol(g), 0) for g in range(3)] + [at(qcol(g), 1) for g in range(3)]
                + [at(kcol, -1), at(kcol, 0), at(vcol, -1), at(vcol, 0)]
                + [at(one, 0), at(one, 0), at(one, -1), at(one, -1), at(one, 1), at(one, 1)]
                + [at(dcol(g), 0) for g in range(3)] + [at(dcol(g), 1) for g in range(3)]
                + [wide(0), wide(1), lspec(0), lspec(1), pl.BlockSpec(memory_space=pltpu.SMEM)])
    kv_out = pl.BlockSpec((b, 128), lambda h, n: (n, h))
    return pl.pallas_call(
        body, name=name, grid=(SWA_KV_HEADS, nb), in_specs=in_specs,
        out_specs=[wide(0), kv_out, kv_out, pl.BlockSpec((1, 8, 128), lambda h, n: (h, 0, 0))],
        out_shape=[jax.ShapeDtypeStruct((t, SWA_W), F32), jax.ShapeDtypeStruct((t, SWA_KV_W), F32),
                   jax.ShapeDtypeStruct((t, SWA_KV_W), F32), jax.ShapeDtypeStruct((SWA_KV_HEADS, 8, 128), F32)],
        compiler_params=_cp(("parallel", "arbitrary")),
    )(*([p] * 10), cosf, sins, cosf, sins, cosf, sins, *([dycat] * 6), o, o, lse, lse, sinks)


def _adam_math(w, g, m, v):
    m = ADAM_B1 * m + (1.0 - ADAM_B1) * g
    v = ADAM_B2 * v + (1.0 - ADAM_B2) * (g * g)
    m_hat = m / (1.0 - ADAM_B1 ** ADAM_STEP)
    v_hat = v / (1.0 - ADAM_B2 ** ADAM_STEP)
    return -ADAM_LR * (m_hat / (jnp.sqrt(v_hat) + ADAM_EPS) + ADAM_WD * w), m, v


def _adamw(w, g, m, v, name):
    shape = w.shape
    cols = shape[-1]
    rows = math.prod(shape[:-1])
    flat = lambda a: a.reshape(rows, cols)
    r = rows
    for cand in (512, 256, 128, 64, 32, 16, 8):
        if rows % cand == 0 and cand * cols * 4 <= (1 << 20):
            r = cand
            break

    def body(w_ref, g_ref, m_ref, v_ref, d_ref, nm_ref, nv_ref):
        d_ref[...], nm_ref[...], nv_ref[...] = _adam_math(w_ref[...], g_ref[...], m_ref[...], v_ref[...])

    spec = pl.BlockSpec((r, cols), lambda i: (i, 0))
    outs = pl.pallas_call(
        body, name=name, grid=(rows // r,), in_specs=[spec] * 4, out_specs=[spec] * 3,
        out_shape=[jax.ShapeDtypeStruct((rows, cols), F32)] * 3, compiler_params=_cp(("parallel",)),
    )(flat(w), flat(g), flat(m), flat(v))
    return tuple(o.reshape(shape) for o in outs)


BIG = ("w_in", "w_out", "ffn_w_up", "ffn_w_down")
CONV = ("dn_conv_w", "ffn_conv_w")
KIND = {"w_in": "col", "w_out": "row", "ffn_w_up": "col", "ffn_w_down": "row"}
SMALL = ("norm_mix_pre", "dn_a_log", "dn_dt_bias", "dn_norm_w", "pool_w", "pool_scale", "swa_sinks",
         "norm_mix_post", "norm_ffn_pre", "ffn_conv_b", "norm_ffn_post")
WEIGHTS = ("norm_mix_pre", "w_in", "dn_conv_w", "dn_a_log", "dn_dt_bias", "dn_norm_w", "pool_w", "pool_scale",
           "swa_sinks", "w_out", "norm_mix_post", "norm_ffn_pre", "ffn_w_up", "ffn_conv_w", "ffn_conv_b",
           "ffn_w_down", "norm_ffn_post")


def _pad_in(w):
    z = lambda n: jnp.zeros(w.shape[:-1] + (n,), w.dtype)
    return jnp.concatenate([w[..., :GATE_END], z(CB_POOL * 128 - GATE_END), w[..., GATE_END:],
                            z(IN_PAD - CB_POOL * 128 - (IN_TRUE - GATE_END))], axis=-1)


def _unpad_in(g):
    return jnp.concatenate([g[..., :GATE_END], g[..., CB_POOL * 128:CB_POOL * 128 + IN_TRUE - GATE_END]], axis=-1)


IN_SHARD = IN_TRUE // 4
IN_SHARD_PAD = -(-IN_SHARD // 128) * 128


def _chip_cols_to_true(w):
    by_chip = w.reshape(w.shape[:-1] + (4, IN_SHARD_PAD))[..., :IN_SHARD]
    return by_chip.reshape(w.shape[:-1] + (IN_TRUE,))


def _true_to_chip_cols(g):
    by_chip = g.reshape(g.shape[:-1] + (4, IN_SHARD))
    by_chip = jnp.pad(by_chip, [(0, 0)] * (by_chip.ndim - 1) + [(0, IN_SHARD_PAD - IN_SHARD)])
    return by_chip.reshape(g.shape[:-1] + (4 * IN_SHARD_PAD,))


def _lanes(v):
    return jnp.zeros((1, 128), F32).at[0, :v.shape[0]].set(v)


def _rope_tables(positions):
    inv_freq = 1.0 / (ROPE_THETA ** (jnp.arange(0, HEAD_DIM, 2, dtype=F32) / HEAD_DIM))
    ang = positions.astype(F32)[:, None] * inv_freq
    cos, sin = jnp.cos(ang), jnp.sin(ang)
    return jnp.concatenate([cos, cos], axis=-1), jnp.concatenate([-sin, sin], axis=-1)


class _GradReduce:
    def __init__(self, place, shard_shapes):
        self.place = place
        self.out = {k: lax.empty(shard_shapes[k], F32) for k in BIG}
        self.pending = []

    def submit(self, l, dw):
        def parts(k, g):
            if KIND[k] == "row":
                return g.reshape(4, -1, g.shape[1])
            return (_true_to_chip_cols(_unpad_in(g)) if k == "w_in" else g)[None]

        names = [k for k in BIG if k in dw]
        tag = f"l{l}_" + "_".join(names)
        mine = [parts(k, dw[k]) for k in names]
        got = _swap_sibling(mine, tag + "_to_sibling")
        self.pending += [(l, k, _chip_sum(a, b, self.place, f"l{l}_chip_sum_{k}")) for k, a, b in zip(names, mine, got)]

    def carry(self):
        return ([s for _, _, s in self.pending], [KIND[k] for _, k, _ in self.pending]) if self.pending else None

    def arrived(self, got):
        for (l, k, own), g in zip(self.pending, got):
            self.out[k] = _owner_sum(own, g, KIND[k], self.place, (self.out[k], l), f"l{l}_owner_sum_{k}")
        self.pending = []

    def finish(self):
        sums, kinds = self.carry()
        self.arrived(_scatter_chips(sums, kinds, "last_grads_to_owner"))
        return dict(zip(BIG, _join_halves([self.out[k] for k in BIG], "grads_join")))


class _LayerWeights:
    def __init__(self, layers):
        self.layers = layers

    def layer(self, l):
        return self.layers[l]

    def carry(self, l, k):
        return None


class _WeightGather(_LayerWeights):
    def __init__(self, shards, place):
        depth = shards["w_out"].shape[0]
        self.kinds = [KIND[k] for k in BIG]
        self.raw = [{k: _spread_shard(shards[k], l, KIND[k], place, BF16, f"l{l}_cast_{k}") for k in BIG} for l in range(depth)]
        first = _gather_ici([self.raw[0][k] for k in BIG], self.kinds, "l0_gather")
        self.layers = {0: self._passed(0, first)}

    def _passed(self, l, arrs):
        full = dict(zip(BIG, _gather_pass(arrs, self.kinds, f"l{l}_gather_pass")))
        full["w_in"] = _pad_in(_chip_cols_to_true(full["w_in"]))
        return full

    def carry(self, l, k):
        return (self.raw[l + 1][k], KIND[k]) if l + 1 < len(self.raw) else None

    def carried(self, l, landed):
        self.layers[l + 1] = self._passed(l + 1, [landed[k] for k in BIG])


def _local_step(x, positions, target, w, mats, reduce=None):
    depth = w["norm_mix_pre"].shape[0]
    t = x.shape[0]
    cosf, sins = _rope_tables(positions)
    saved = []
    for l in range(depth):
        nm = f"l{l}_"
        n1, n2, n3, n4 = (w[k][l][None] for k in ("norm_mix_pre", "norm_mix_post", "norm_ffn_pre", "norm_ffn_post"))
        alog, dtb, dnw = _lanes(w["dn_a_log"][l]), _lanes(w["dn_dt_bias"][l]), w["dn_norm_w"][l][None]
        psc, cb = w["pool_scale"][l][None], w["ffn_conv_b"][l][None]
        big = mats.layer(l)
        landed = {}

        def project(a, k, name):
            riding = mats.carry(l, k)
            if riding is None:
                return _mm(a, big[k], "nn", F32, name)
            out, landed[k] = _mm(a, big[k], "nn", F32, name, carry=riding)
            return out

        h = _norm_fwd(x, n1, nm + "norm1")
        p = project(h, "w_in", nm + "in_proj")
        qkv = _dn_pre_fwd(p, w["dn_conv_w"][l], nm + "dn_pre")
        y_dn, st = _dn_fwd(qkv, p, alog, dtb, dnw, nm + "dn")
        y_pool = _pool_fwd(p, w["pool_w"][l], psc, nm + "pool")
        y_swa, lse = _swa_fwd(p, cosf, sins, w["swa_sinks"][l], nm + "swa")
        ycat = jnp.concatenate([y_dn, y_pool, y_swa], axis=1).astype(BF16)
        mix = project(ycat, "w_out", nm + "out_proj")
        x1 = _resnorm_fwd(x, mix, n2, nm + "res1")
        h2 = _norm_fwd(x1, n3, nm + "norm3")
        up = project(h2, "ffn_w_up", nm + "ffn_up")
        act = _ffn_act_fwd(up, w["ffn_conv_w"][l], cb, nm + "ffn_act")
        f = project(act, "ffn_w_down", nm + "ffn_down")
        if landed:
            mats.carried(l, landed)
        x2 = _resnorm_fwd(x1, f, n4, nm + "res2")
        saved.append(dict(x=x, h=h, p=p, qkv=qkv, st=st, y_swa=y_swa, lse=lse, ycat=ycat, mix=mix, x1=x1, h2=h2,
                          up=up, act=act, f=f, n=(n1, n2, n3, n4), alog=alog, dtb=dtb, dnw=dnw, psc=psc, cb=cb))
        x = x2
    loss, dx = _loss_head(x, target, "loss_head")
    grads = {k: [None] * depth for k in WEIGHTS}
    for l in reversed(range(depth)):
        nm, s = f"l{l}_b_", saved[l]
        n1, n2, n3, n4 = s["n"]
        big = mats.layer(l)
        df, g4 = _norm_bwd(s["f"], n4, dx, None, BF16, nm + "res2")
        dact = _mm(df, big["ffn_w_down"], "nt", F32, nm + "ffn_down_dx")
        grads["ffn_w_down"][l] = _mm(s["act"], df, "tn", BF16, nm + "ffn_down_dw")
        dup, dcw, dcb = _ffn_act_bwd(s["up"], w["ffn_conv_w"][l], s["cb"], dact, nm + "ffn_act")
        grads["ffn_conv_w"][l] = jnp.concatenate([dcw[0], dcw[1]], axis=1)
        grads["ffn_conv_b"][l] = jnp.concatenate([dcb[0], dcb[1]], axis=1)[0]
        grads["ffn_w_up"][l] = _mm(s["h2"], dup, "tn", BF16, nm + "ffn_up_dw", b_pick="split")
        dh2 = _mm(dup, big["ffn_w_up"], "nt", BF16, nm + "ffn_up_dx", a_pick="split")
        dx1, g3 = _norm_bwd(s["x1"], n3, dh2, dx, F32, nm + "norm3")
        dmix, g2 = _norm_bwd(s["mix"], n2, dx1, None, BF16, nm + "res1")
        grads["w_out"][l] = _mm(s["ycat"], dmix, "tn", BF16, nm + "out_proj_dw")
        dycat = _mm(dmix, big["w_out"], "nt", F32, nm + "out_proj_dx")
        if reduce is not None:
            reduce.submit(l, {k: grads[k][l] for k in ("ffn_w_down", "ffn_w_up", "w_out")})
        carry = reduce.carry() if reduce is not None else None
        res = _dn_bwd(s["qkv"], s["p"], s["alog"], s["dtb"], s["dnw"], s["st"], dycat, nm + "dn", carry=carry)
        dqkv, dz, dbd, gal, gdt, gnw = res[:6]
        if carry is not None:
            reduce.arrived(res[6:])
        dpq, gconv = _dn_pre_bwd(s["p"], w["dn_conv_w"][l], dqkv, nm + "dn_pre")
        dpool, gpw, gpsc = _pool_bwd(s["p"], w["pool_w"][l], s["psc"], dycat, nm + "pool")
        dsq, dsk, dsv, gsk = _swa_bwd(s["p"], cosf, sins, w["swa_sinks"][l], s["y_swa"], s["lse"], dycat, nm + "swa")
        dp = jnp.concatenate([dpq, dz, dbd, dpool, dsq, dsk, dsv, jnp.zeros((t, 128), F32)], axis=1).astype(BF16)
        grads["w_in"][l] = _mm(s["h"], dp, "tn", BF16, nm + "in_proj_dw")
        dh = _mm(dp, big["w_in"], "nt", BF16, nm + "in_proj_dx")
        dx, g1 = _norm_bwd(s["x"], n1, dh, dx1, F32, nm + "norm1")
        if reduce is not None:
            reduce.submit(l, {"w_in": grads["w_in"][l]})
        grads["norm_mix_pre"][l], grads["norm_mix_post"][l] = g1[0], g2[0]
        grads["norm_ffn_pre"][l], grads["norm_ffn_post"][l] = g3[0], g4[0]
        grads["dn_conv_w"][l] = gconv
        grads["dn_a_log"][l], grads["dn_dt_bias"][l], grads["dn_norm_w"][l] = gal[0, :DN_HEADS], gdt[0, :DN_HEADS], gnw[0]
        grads["pool_w"][l], grads["pool_scale"][l] = gpw, gpsc[0]
        grads["swa_sinks"][l] = gsk[:, 0, :SWA_GROUP].reshape(SWA_HEADS)
    return loss, dx, grads


def _flat2(a):
    return a.reshape(math.prod(a.shape[:-1]), a.shape[-1])


def _ew_rows(rows, cols, n_arrays):
    for cand in (512, 256, 128, 64, 32, 16):
        if rows % cand == 0 and cand * cols * 4 * n_arrays <= (8 << 20):
            return cand
    return rows


def _spread_shard(a, layer, kind, place, dtype, name):
    _, rows, cols = a.shape
    r = _ew_rows(rows, cols, 2)
    nb = rows // r

    def body(s_ref, a_ref, o_ref):
        o_ref[...] = a_ref[...].astype(o_ref.dtype)

    if kind == "row":
        out_spec = pl.BlockSpec((r, cols), lambda i, s: (s[0] * nb + i, 0))
        out_shape = (4 * rows, cols)
    else:
        out_spec = pl.BlockSpec((r, cols), lambda i, s: (i, s[0]))
        out_shape = (rows, 4 * cols)
    return pl.pallas_call(
        body, name=name,
        grid_spec=pltpu.PrefetchScalarGridSpec(
            num_scalar_prefetch=1, grid=(nb,),
            in_specs=[pl.BlockSpec((None, r, cols), lambda i, s: (layer, i, 0))], out_specs=out_spec),
        out_shape=jax.ShapeDtypeStruct(out_shape, dtype), compiler_params=_cp(("parallel",)),
    )(place, a)


def _chip_sum(mine, sib, place, name):
    parts, rows, cols = sib.shape
    r = _ew_rows(rows, cols, 3)
    nb = rows // r

    def body(s_ref, a_ref, b_ref, o_ref):
        o_ref[...] = (a_ref[...].astype(F32) + b_ref[...].astype(F32)).astype(o_ref.dtype)

    spec = pl.BlockSpec((None, r, cols), lambda j, i, s: (j, i, 0))
    return pl.pallas_call(
        body, name=name,
        grid_spec=pltpu.PrefetchScalarGridSpec(
            num_scalar_prefetch=1, grid=(parts, nb),
            in_specs=[pl.BlockSpec((None, r, cols), lambda j, i, s: (j, s[1] * nb + i, 0)), spec], out_specs=spec),
        out_shape=jax.ShapeDtypeStruct(sib.shape, BF16), compiler_params=_cp(("parallel", "parallel")),
    )(place, mine, sib)


def _sum_slots(a, name):
    s = a.shape[0]
    a3 = a.reshape(s, math.prod(a.shape[1:-1]), a.shape[-1])
    _, rows, cols = a3.shape
    r = _ew_rows(rows, cols, s + 1)

    def body(a_ref, o_ref):
        acc = a_ref[0].astype(F32)
        for k in range(1, s):
            acc = acc + a_ref[k].astype(F32)
        o_ref[...] = acc

    return pl.pallas_call(body, name=name, grid=(rows // r,),
                          in_specs=[pl.BlockSpec((s, r, cols), lambda i: (0, i, 0))],
                          out_specs=pl.BlockSpec((r, cols), lambda i: (i, 0)),
                          out_shape=jax.ShapeDtypeStruct((rows, cols), F32), compiler_params=_cp(("parallel",)),
                          )(a3).reshape(a.shape[1:])


def _owner_sum(own, got, kind, place, into, name):
    buf, slab = into
    _, rows, cols = got.shape
    r = _ew_rows(rows, cols, 6)
    nb = rows // r

    def body(s_ref, own_ref, got_ref, buf_ref, o_ref):
        acc = own_ref[...].astype(F32)
        for k in range(3):
            acc = acc + got_ref[k].astype(F32)
        o_ref[...] = acc

    if kind == "row":
        own_spec = pl.BlockSpec((None, r, cols), lambda i, s: (s[0], i, 0))
    else:
        own_spec = pl.BlockSpec((None, r, cols), lambda i, s: (0, i, s[0]))
    return pl.pallas_call(
        body, name=name,
        grid_spec=pltpu.PrefetchScalarGridSpec(
            num_scalar_prefetch=1, grid=(nb,),
            in_specs=[own_spec, pl.BlockSpec((3, r, cols), lambda i, s: (0, i, 0)), pl.BlockSpec(memory_space=pl.ANY)],
            out_specs=pl.BlockSpec((None, r, cols), lambda i, s: (slab, s[1] * nb + i, 0))),
        out_shape=jax.ShapeDtypeStruct(buf.shape, buf.dtype), input_output_aliases={3: 0},
        compiler_params=_cp(("parallel",)),
    )(place, own, got, buf)


MESH = pl.DeviceIdType.MESH
ANY = pl.BlockSpec(memory_space=pl.ANY)


def _place():
    x, y, c = lax.axis_index("x"), lax.axis_index("y"), lax.axis_index("c")
    chips = [(1 - x, y), (x, 1 - y), (1 - x, 1 - y)]
    return x, y, c, chips


def _half_part(ref, kind, chip, half):
    if kind == "row":
        h = ref.shape[0] // 8
        return ref.at[pl.ds(pl.multiple_of((2 * chip + half) * h, 16), h), :]
    h, width = ref.shape[0] // 2, ref.shape[1] // 4
    return ref.at[pl.ds(pl.multiple_of(half * h, 16), h), pl.ds(pl.multiple_of(chip * width, 128), width)]


def _gather_copies(w_ref, kind, send, recv):
    x, y, c, chips = _place()
    mine = _half_part(w_ref, kind, 2 * x + y, c)
    return [pltpu.make_async_remote_copy(mine, mine, send.at[j], recv.at[j], device_id=(px, py, c), device_id_type=MESH)
            for j, (px, py) in enumerate(chips)]


def _gather_ici(arrs, kinds, name):
    na = len(arrs)

    def body(*refs):
        outs, send, recv = refs[na:2 * na], refs[2 * na], refs[2 * na + 1]
        cps = [cp for k in range(na) for cp in _gather_copies(outs[k], kinds[k], send.at[k], recv.at[k])]
        for cp in cps:
            cp.start()
        for cp in cps:
            cp.wait()

    return pl.pallas_call(
        body, name=name, in_specs=[ANY] * na, out_specs=[ANY] * na,
        out_shape=[jax.ShapeDtypeStruct(a.shape, a.dtype) for a in arrs],
        input_output_aliases={k: k for k in range(na)},
        scratch_shapes=[pltpu.SemaphoreType.DMA((na, 3))] * 2,
    )(*arrs)


def _gather_pass(arrs, kinds, name):
    na = len(arrs)

    def body(*refs):
        outs, send, recv = refs[na:2 * na], refs[2 * na], refs[2 * na + 1]
        x, y, c, chips = _place()
        cps, arrivals = [], []
        for k in range(na):
            for j, (px, py) in enumerate(chips):
                mine = _half_part(outs[k], kinds[k], 2 * px + py, c)
                theirs = _half_part(outs[k], kinds[k], 2 * px + py, 1 - c)
                cps.append(pltpu.make_async_remote_copy(mine, mine, send.at[k, j], recv.at[k, j],
                                                        device_id=(x, y, 1 - c), device_id_type=MESH))
                arrivals.append(pltpu.make_async_remote_copy(theirs, theirs, send.at[k, j], recv.at[k, j],
                                                             device_id=(x, y, 1 - c), device_id_type=MESH))
        for cp in cps:
            cp.start()
        for cp, arrival in zip(cps, arrivals):
            cp.wait_send()
            arrival.wait_recv()

    return pl.pallas_call(
        body, name=name, in_specs=[ANY] * na, out_specs=[ANY] * na,
        out_shape=[jax.ShapeDtypeStruct(a.shape, a.dtype) for a in arrs],
        input_output_aliases={k: k for k in range(na)},
        scratch_shapes=[pltpu.SemaphoreType.DMA((na, 3))] * 2,
    )(*arrs)


def _swap_sibling(arrs, name):
    na = len(arrs)

    def body(*refs):
        ins, outs, send, recv = refs[:na], refs[na:2 * na], refs[2 * na], refs[2 * na + 1]
        x, y, c, _ = _place()
        cps = []
        for k in range(na):
            h = ins[k].shape[1] // 2
            cps.append(pltpu.make_async_remote_copy(ins[k].at[:, pl.ds(pl.multiple_of((1 - c) * h, 16), h), :], outs[k],
                                                    send.at[k], recv.at[k], device_id=(x, y, 1 - c), device_id_type=MESH))
        for cp in cps:
            cp.start()
        for cp in cps:
            cp.wait()

    return pl.pallas_call(
        body, name=name, in_specs=[ANY] * na, out_specs=[ANY] * na,
        out_shape=[jax.ShapeDtypeStruct((a.shape[0], a.shape[1] // 2, a.shape[2]), a.dtype) for a in arrs],
        scratch_shapes=[pltpu.SemaphoreType.DMA((na,))] * 2,
    )(*arrs)


def _scatter_shapes(sums, kinds):
    return [jax.ShapeDtypeStruct((3, a.shape[1], a.shape[2] if kind == "row" else a.shape[2] // 4), a.dtype)
            for a, kind in zip(sums, kinds)]


def _scatter_copies(srcs, dsts, kinds, send, recv):
    x, y, c, chips = _place()
    cps = []
    for k, (src, dst) in enumerate(zip(srcs, dsts)):
        for j, (px, py) in enumerate(chips):
            chip = 2 * px + py
            if kinds[k] == "row":
                part = src.at[chip]
            else:
                width = src.shape[2] // 4
                part = src.at[0, :, pl.ds(pl.multiple_of(chip * width, 128), width)]
            cps.append(pltpu.make_async_remote_copy(part, dst.at[j], send.at[k, j], recv.at[k, j],
                                                    device_id=(px, py, c), device_id_type=MESH))
    return cps


def _scatter_chips(sums, kinds, name):
    na = len(sums)

    def body(*refs):
        cps = _scatter_copies(refs[:na], refs[na:2 * na], kinds, refs[2 * na], refs[2 * na + 1])
        for cp in cps:
            cp.start()
        for cp in cps:
            cp.wait()

    return pl.pallas_call(
        body, name=name, in_specs=[ANY] * na, out_specs=[ANY] * na, out_shape=_scatter_shapes(sums, kinds),
        scratch_shapes=[pltpu.SemaphoreType.DMA((na, 3))] * 2,
    )(*sums)


def _join_halves(arrs, name):
    na = len(arrs)

    def body(*refs):
        outs, send, recv = refs[na:2 * na], refs[2 * na], refs[2 * na + 1]
        x, y, c, _ = _place()
        halves = [a.shape[1] // 2 for a in arrs]
        mine = [outs[k].at[:, pl.ds(pl.multiple_of(c * h, 8), h), :] for k, h in enumerate(halves)]
        theirs = [outs[k].at[:, pl.ds(pl.multiple_of((1 - c) * h, 8), h), :] for k, h in enumerate(halves)]
        cps = [pltpu.make_async_remote_copy(mine[k], mine[k], send.at[k], recv.at[k],
                                            device_id=(x, y, 1 - c), device_id_type=MESH) for k in range(na)]
        for cp in cps:
            cp.start()
        for k, cp in enumerate(cps):
            cp.wait_send()
            pltpu.make_async_remote_copy(theirs[k], theirs[k], send.at[k], recv.at[k],
                                         device_id=(x, y, 1 - c), device_id_type=MESH).wait_recv()

    return pl.pallas_call(
        body, name=name, in_specs=[ANY] * na, out_specs=[ANY] * na,
        out_shape=[jax.ShapeDtypeStruct(a.shape, a.dtype) for a in arrs],
        input_output_aliases={k: k for k in range(na)},
        scratch_shapes=[pltpu.SemaphoreType.DMA((na,))] * 2,
    )(*arrs)


def _gather_all(a, name):
    def body(a_ref, o_ref, send, recv, local):
        x, y, c, _ = _place()
        me = 4 * x + 2 * y + c
        mine = pltpu.make_async_copy(a_ref, o_ref.at[me], local)
        mine.start()
        cps = []
        for j in range(1, 8):
            peer = (x ^ (j >> 2), y ^ ((j >> 1) & 1), c ^ (j & 1))
            cps.append(pltpu.make_async_remote_copy(a_ref, o_ref.at[me], send.at[j - 1], recv.at[j - 1],
                                                    device_id=peer, device_id_type=MESH))
        for cp in cps:
            cp.start()
        for cp in cps:
            cp.wait()
        mine.wait()

    return pl.pallas_call(
        body, name=name, in_specs=[ANY], out_specs=ANY,
        out_shape=jax.ShapeDtypeStruct((8,) + a.shape, a.dtype),
        scratch_shapes=[pltpu.SemaphoreType.DMA((7,)), pltpu.SemaphoreType.DMA((7,)), pltpu.SemaphoreType.DMA],
    )(a)


def _pack(parts):
    flat = jnp.concatenate([p.reshape(-1) for p in parts])
    n = flat.shape[0]
    rows = -(-n // 1024) * 8
    return jnp.pad(flat, (0, rows * 128 - n)).reshape(rows, 128)


def _unpack(buf, like):
    flat, out, off = buf.reshape(-1), [], 0
    for p in like:
        out.append(flat[off:off + p.size].reshape(p.shape))
        off += p.size
    return out


def kernel(x, positions, norm_mix_pre, w_in, dn_conv_w, dn_a_log, dn_dt_bias, dn_norm_w, pool_w, pool_scale, swa_sinks, w_out, norm_mix_post, norm_ffn_pre, ffn_w_up, ffn_conv_w, ffn_conv_b, ffn_w_down, norm_ffn_post, loss_target, m_norm_mix_pre, m_w_in, m_dn_conv_w, m_dn_a_log, m_dn_dt_bias, m_dn_norm_w, m_pool_w, m_pool_scale, m_swa_sinks, m_w_out, m_norm_mix_post, m_norm_ffn_pre, m_ffn_w_up, m_ffn_conv_w, m_ffn_conv_b, m_ffn_w_down, m_norm_ffn_post, v_norm_mix_pre, v_w_in, v_dn_conv_w, v_dn_a_log, v_dn_dt_bias, v_dn_norm_w, v_pool_w, v_pool_scale, v_swa_sinks, v_w_out, v_norm_mix_post, v_norm_ffn_pre, v_ffn_w_up, v_ffn_conv_w, v_ffn_conv_b, v_ffn_w_down, v_norm_ffn_post):
    wts = dict(zip(WEIGHTS, (norm_mix_pre, w_in, dn_conv_w, dn_a_log, dn_dt_bias, dn_norm_w, pool_w, pool_scale, swa_sinks,
                             w_out, norm_mix_post, norm_ffn_pre, ffn_w_up, ffn_conv_w, ffn_conv_b, ffn_w_down, norm_ffn_post)))
    mom = dict(zip(WEIGHTS, (m_norm_mix_pre, m_w_in, m_dn_conv_w, m_dn_a_log, m_dn_dt_bias, m_dn_norm_w, m_pool_w, m_pool_scale,
                             m_swa_sinks, m_w_out, m_norm_mix_post, m_norm_ffn_pre, m_ffn_w_up, m_ffn_conv_w, m_ffn_conv_b,
                             m_ffn_w_down, m_norm_ffn_post)))
    var = dict(zip(WEIGHTS, (v_norm_mix_pre, v_w_in, v_dn_conv_w, v_dn_a_log, v_dn_dt_bias, v_dn_norm_w, v_pool_w, v_pool_scale,
                             v_swa_sinks, v_w_out, v_norm_mix_post, v_norm_ffn_pre, v_ffn_w_up, v_ffn_conv_w, v_ffn_conv_b,
                             v_ffn_w_down, v_norm_ffn_post)))
    c = lax.axis_index("c")
    chip = 2 * lax.axis_index("x") + lax.axis_index("y")
    place = jnp.stack([chip, c]).astype(jnp.int32)
    shards = dict(wts, w_in=jnp.pad(w_in, ((0, 0), (0, 0), (0, IN_SHARD_PAD - IN_SHARD))))
    mats = _WeightGather(shards, place)
    w = dict(wts)
    conv_like = [wts[k] for k in CONV]
    conv_all = _gather_all(_pack(conv_like), "gather_conv")
    for i, k in enumerate(CONV):
        w[k] = jnp.concatenate([_unpack(conv_all[2 * j], conv_like)[i] for j in range(4)], axis=2)

    reduce = _GradReduce(place, {k: shards[k].shape for k in BIG})
    loss, dx, grads = _local_step(x[0], positions[0], loss_target[0], w, mats, reduce)
    loss = lax.psum(loss[0, 0], ("x", "y", "c"))
    g_big = reduce.finish()
    g_big["w_in"] = g_big["w_in"][..., :IN_SHARD]

    small_like = [wts[k] for k in SMALL]
    full_like = small_like + [w[k] for k in CONV]
    g_buf = _sum_slots(_gather_all(_pack([jnp.stack(grads[k]) for k in SMALL + CONV]), "gather_small"), "sum_small")
    g_small = dict(zip(SMALL + CONV, _unpack(g_buf, full_like)))
    for k in CONV:
        width = wts[k].shape[2]
        g_small[k] = lax.dynamic_slice_in_dim(g_small[k], chip * width, width, 2)
    pk = lambda d: _pack([d[k] for k in SMALL + CONV])
    upd = _adamw(pk(wts), pk(g_small), pk(mom), pk(var), "adam_small")
    upd_small = [dict(zip(SMALL + CONV, _unpack(b, small_like + conv_like))) for b in upd]

    g_all, d_all, m_all, v_all = {}, {}, {}, {}
    for k in WEIGHTS:
        if k in BIG:
            g_all[k] = g_big[k]
            d_all[k], m_all[k], v_all[k] = _adamw(wts[k], g_big[k], mom[k], var[k], "adam_" + k)
        else:
            g_all[k], d_all[k], m_all[k], v_all[k] = g_small[k], upd_small[0][k], upd_small[1][k], upd_small[2][k]
    return (loss, dx[None], *[g_all[k] for k in WEIGHTS], *[d_all[k] for k in WEIGHTS],
            *[m_all[k] for k in WEIGHTS], *[v_all[k] for k in WEIGHTS])
```

```python
import functools
import math

import jax
import jax.numpy as jnp
from jax import lax
from jax.experimental import pallas as pl
from jax.experimental.pallas import tpu as pltpu

F32 = jnp.float32
BF16 = jnp.bfloat16

HEAD_DIM = 128
DN_HEADS = 6
DN_CONV = 4
DN_CHUNK = 64
POOL_GROUPS = 4
SWA_HEADS = 6
SWA_KV_HEADS = 2
SWA_GROUP = SWA_HEADS // SWA_KV_HEADS
SWA_BLOCK = 128
ROPE_THETA = 10000.0
FFN_CONV = 3
NORM_EPS = 1e-6
DN_W = DN_HEADS * HEAD_DIM
POOL_W = POOL_GROUPS * HEAD_DIM
SWA_W = SWA_HEADS * HEAD_DIM
SWA_KV_W = SWA_KV_HEADS * HEAD_DIM
MIX_W = DN_W + POOL_W + SWA_W
IN_TRUE = 3 * DN_W + DN_W + 2 * DN_HEADS + POOL_W + SWA_W + 2 * SWA_KV_W
GATE_END = 4 * DN_W + 2 * DN_HEADS
CB_Z = 18
CB_BD = 24
CB_POOL = 25
CB_SQ = 29
CB_SK = 35
CB_SV = 37
IN_PAD = 40 * 128
ADAM_LR, ADAM_B1, ADAM_B2, ADAM_EPS, ADAM_WD, ADAM_STEP = 0.001, 0.9, 0.999, 1e-08, 0.01, 10

VMEM_LIMIT = 48 * 1024 * 1024
PACK_ROWS = 512
MM_TK_MAX = 2816
HIGH = lax.Precision.HIGHEST


def _cp(sem):
    return pltpu.CompilerParams(dimension_semantics=sem, vmem_limit_bytes=VMEM_LIMIT)


def _tile(n, prefs):
    for p in prefs:
        if n % p == 0:
            return p
    return n


def _rows(t):
    return _tile(t, (256, 128))


_DN = {"nn": (((1,), (0,)), ((), ())), "nt": (((1,), (1,)), ((), ())), "tn": (((0,), (0,)), ((), ()))}


def _mm_operand(arr, pick, block, idx):
    if pick is None:
        return pl.BlockSpec(block, idx)
    if pick == "split":
        per = arr.shape[2] // block[1]

        def split_idx(i, j, kk):
            r, c = idx(i, j, kk)
            return lax.div(c, per), r, lax.rem(c, per)

        return pl.BlockSpec((None,) + block, split_idx)
    slab = pick[1]
    return pl.BlockSpec((None,) + block, lambda i, j, kk: (slab,) + idx(i, j, kk))


def _mm(a, b, mode, out_dtype, name, a_pick=None, b_pick=None, carry=None):
    def dims(arr, pick):
        r, c = arr.shape[-2:]
        return (r, c * arr.shape[0]) if pick == "split" else (r, c)

    (a0, a1), (b0, b1) = dims(a, a_pick), dims(b, b_pick)
    k, m = (a0, a1) if mode == "tn" else (a1, a0)
    n = b0 if mode == "nt" else b1
    lim = lambda arr, pick, is_last, full: arr.shape[2] if (pick == "split" and is_last) else full
    tm = _tile(lim(a, a_pick, mode == "tn", m), (1024, 512, 256, 128))
    tn = _tile(lim(b, b_pick, mode != "nt", n), (1408, 1280, 1024, 512, 256, 128))
    k_lim = min(lim(a, a_pick, mode != "tn", k), lim(b, b_pick, mode == "nt", k))
    tk = max([d for d in range(128, min(k_lim, MM_TK_MAX) + 1, 128) if k_lim % d == 0], default=k_lim)
    nk = k // tk

    grid = (m // tm, n // tn, nk)

    def body(a_ref, b_ref, *rest):
        if carry is not None:
            _, o_ref, w_ref, *scratch = rest
            send, recv = scratch[-2:]
            scratch = scratch[:-2]
            copies = _gather_copies(w_ref, carry[1], send, recv)
            step = (pl.program_id(0) * grid[1] + pl.program_id(1)) * grid[2] + pl.program_id(2)

            @pl.when(step == 0)
            def _():
                for cp in copies:
                    cp.start()
        else:
            o_ref, *scratch = rest
        part = lax.dot_general(a_ref[...], b_ref[...], _DN[mode], preferred_element_type=F32)
        if nk == 1:
            o_ref[...] = part.astype(o_ref.dtype)
        else:
            acc_ref, = scratch
            kk = pl.program_id(2)

            @pl.when(kk == 0)
            def _():
                acc_ref[...] = part

            @pl.when(kk > 0)
            def _():
                acc_ref[...] += part

            @pl.when(kk == nk - 1)
            def _():
                o_ref[...] = acc_ref[...].astype(o_ref.dtype)
        if carry is not None:
            @pl.when(step == grid[0] * grid[1] * grid[2] - 1)
            def _():
                for cp in copies:
                    cp.wait()

    if mode == "tn":
        a_spec = _mm_operand(a, a_pick, (tk, tm), lambda i, j, kk: (kk, i))
    else:
        a_spec = _mm_operand(a, a_pick, (tm, tk), lambda i, j, kk: (i, kk))
    if mode == "nt":
        b_spec = _mm_operand(b, b_pick, (tn, tk), lambda i, j, kk: (j, kk))
    else:
        b_spec = _mm_operand(b, b_pick, (tk, tn), lambda i, j, kk: (kk, j))
    scratch = [pltpu.VMEM((tm, tn), F32)] if nk > 1 else []
    out_spec = pl.BlockSpec((tm, tn), lambda i, j, kk: (i, j))
    out_shape = jax.ShapeDtypeStruct((m, n), out_dtype)
    if carry is None:
        return pl.pallas_call(
            body, name=name, grid=grid, in_specs=[a_spec, b_spec], out_specs=out_spec, out_shape=out_shape,
            scratch_shapes=scratch, compiler_params=_cp(("parallel", "parallel", "arbitrary")),
        )(a, b)
    any_space = pl.BlockSpec(memory_space=pl.ANY)
    return pl.pallas_call(
        body, name=name, grid=grid, in_specs=[a_spec, b_spec, any_space], out_specs=[out_spec, any_space],
        out_shape=[out_shape, jax.ShapeDtypeStruct(carry[0].shape, carry[0].dtype)], input_output_aliases={2: 1},
        scratch_shapes=scratch + [pltpu.SemaphoreType.DMA((3,))] * 2,
        compiler_params=_cp(("arbitrary", "arbitrary", "arbitrary")),
    )(a, b, carry[0])


def _rms(x, w):
    return x * lax.rsqrt(jnp.mean(x * x, axis=-1, keepdims=True) + NORM_EPS) * w


def _norm_fwd(x, w, name):
    t, d = x.shape
    r = _rows(t)

    def body(x_ref, w_ref, h_ref):
        h_ref[...] = _rms(x_ref[...], w_ref[...]).astype(h_ref.dtype)

    return pl.pallas_call(
        body, name=name, grid=(t // r,),
        in_specs=[pl.BlockSpec((r, d), lambda i: (i, 0)), pl.BlockSpec((1, d), lambda i: (0, 0))],
        out_specs=pl.BlockSpec((r, d), lambda i: (i, 0)),
        out_shape=jax.ShapeDtypeStruct((t, d), BF16), compiler_params=_cp(("parallel",)),
    )(x, w)


def _resnorm_fwd(x, y, w, name):
    t, d = x.shape
    r = _rows(t)

    def body(x_ref, y_ref, w_ref, o_ref):
        o_ref[...] = x_ref[...] + _rms(y_ref[...], w_ref[...])

    return pl.pallas_call(
        body, name=name, grid=(t // r,),
        in_specs=[pl.BlockSpec((r, d), lambda i: (i, 0)), pl.BlockSpec((r, d), lambda i: (i, 0)),
                  pl.BlockSpec((1, d), lambda i: (0, 0))],
        out_specs=pl.BlockSpec((r, d), lambda i: (i, 0)),
        out_shape=jax.ShapeDtypeStruct((t, d), F32), compiler_params=_cp(("parallel",)),
    )(x, y, w)


def _norm_bwd(x, w, dh, add, out_dtype, name):
    t, d = x.shape
    r = _rows(t)
    has_add = add is not None

    def body(*refs):
        if has_add:
            x_ref, w_ref, dh_ref, add_ref, dx_ref, dw_ref = refs
        else:
            x_ref, w_ref, dh_ref, dx_ref, dw_ref = refs
        xv = x_ref[...]
        g = dh_ref[...].astype(F32)
        rs = lax.rsqrt(jnp.mean(xv * xv, axis=-1, keepdims=True) + NORM_EPS)
        xh = xv * rs
        gw = g * w_ref[...]
        dx = rs * (gw - xh * jnp.mean(gw * xh, axis=-1, keepdims=True))
        if has_add:
            dx = dx + add_ref[...]
        dx_ref[...] = dx.astype(dx_ref.dtype)

        @pl.when(pl.program_id(0) == 0)
        def _():
            dw_ref[...] = jnp.zeros_like(dw_ref)

        dw_ref[...] += jnp.sum(g * xh, axis=0, keepdims=True)

    row = pl.BlockSpec((r, d), lambda i: (i, 0))
    vec = pl.BlockSpec((1, d), lambda i: (0, 0))
    ins = [x, w, dh] + ([add] if has_add else [])
    return pl.pallas_call(
        body, name=name, grid=(t // r,),
        in_specs=[row, vec, row] + ([row] if has_add else []),
        out_specs=[row, vec],
        out_shape=[jax.ShapeDtypeStruct((t, d), out_dtype), jax.ShapeDtypeStruct((1, d), F32)],
        compiler_params=_cp(("arbitrary",)),
    )(*ins)


def _loss_head(y, target, name):
    t, d = y.shape
    r = _rows(t)

    def body(y_ref, t_ref, l_ref, g_ref):
        e = y_ref[...] - t_ref[...]
        g_ref[...] = e * (1.0 / d)

        @pl.when(pl.program_id(0) == 0)
        def _():
            l_ref[...] = jnp.zeros_like(l_ref)

        l_ref[...] += jnp.sum(e * e) * (0.5 / d)

    row = pl.BlockSpec((r, d), lambda i: (i, 0))
    return pl.pallas_call(
        body, name=name, grid=(t // r,), in_specs=[row, row],
        out_specs=[pl.BlockSpec((1, 128), lambda i: (0, 0)), row],
        out_shape=[jax.ShapeDtypeStruct((1, 128), F32), jax.ShapeDtypeStruct((t, d), F32)],
        compiler_params=_cp(("arbitrary",)),
    )(y, target)


def _down(x, s):
    return x if s == 0 else pltpu.roll(x, s, 0)


def _up(x, s):
    return x if s == 0 else pltpu.roll(x, x.shape[0] - s, 0)


def _halo(t, r, hh, tc, col):
    q = r // hh
    last = t // hh - 1
    tile = pl.BlockSpec((r, tc), lambda j, i: (i, col(j)))
    prev = pl.BlockSpec((hh, tc), lambda j, i: (jnp.maximum(i * q - 1, 0), col(j)))
    nxt = pl.BlockSpec((hh, tc), lambda j, i: (jnp.minimum((i + 1) * q, last), col(j)))
    return tile, prev, nxt


def _sig(x):
    return 1.0 / (1.0 + jnp.exp(-x))


def _dsilu(x, s):
    return s * (1.0 + x * (1.0 - s))


def _dn_pre_fwd(p, conv_w, name):
    t = p.shape[0]
    r = _rows(t)

    def body(x_ref, xp_ref, w_ref, o_ref):
        j, i = pl.program_id(0), pl.program_id(1)
        xe = jnp.concatenate([jnp.where(i == 0, 0.0, xp_ref[...]), x_ref[...]], axis=0)
        c = sum(_down(xe, DN_CONV - 1 - k) * w_ref[pl.ds(k, 1), :] for k in range(DN_CONV))[8:]
        a = c * _sig(c)
        for h in range(DN_HEADS):
            ah = a[:, h * 128:(h + 1) * 128]
            fac = lax.rsqrt(jnp.sum(ah * ah, axis=-1, keepdims=True) + NORM_EPS)
            o_ref[:, h * 128:(h + 1) * 128] = ah * jnp.where(j == 0, fac * HEAD_DIM ** -0.5, jnp.where(j == 1, fac, 1.0))

    tile, prev, _ = _halo(t, r, 8, DN_W, lambda j: j)
    return pl.pallas_call(
        body, name=name, grid=(3, t // r),
        in_specs=[tile, prev, pl.BlockSpec((DN_CONV, DN_W), lambda j, i: (0, j))],
        out_specs=tile, out_shape=jax.ShapeDtypeStruct((t, 3 * DN_W), F32),
        compiler_params=_cp(("parallel", "parallel")),
    )(p, p, conv_w)


def _dn_pre_bwd(p, conv_w, dqkv, name):
    t = p.shape[0]
    r = _rows(t)
    ni = t // r

    def body(x_ref, xp_ref, xn_ref, w_ref, d_ref, dn_ref, dx_ref, dw_ref):
        j, i = pl.program_id(0), pl.program_id(1)
        xe = jnp.concatenate([jnp.where(i == 0, 0.0, xp_ref[...]), x_ref[...], xn_ref[...]], axis=0)
        de = jnp.concatenate([jnp.zeros((8, DN_W), F32), d_ref[...], jnp.where(i == ni - 1, 0.0, dn_ref[...])], axis=0)
        xs = [_down(xe, DN_CONV - 1 - k) for k in range(DN_CONV)]
        c = sum(xs[k] * w_ref[pl.ds(k, 1), :] for k in range(DN_CONV))
        s = _sig(c)
        a = c * s
        das = []
        for h in range(DN_HEADS):
            ah, dh = a[:, h * 128:(h + 1) * 128], de[:, h * 128:(h + 1) * 128]
            fac = lax.rsqrt(jnp.sum(ah * ah, axis=-1, keepdims=True) + NORM_EPS)
            dnorm = fac * dh - ah * (fac * fac * fac) * jnp.sum(dh * ah, axis=-1, keepdims=True)
            das.append(jnp.where(j == 0, dnorm * HEAD_DIM ** -0.5, jnp.where(j == 1, dnorm, dh)))
        dc = jnp.concatenate(das, axis=1) * _dsilu(c, s)
        dx_ref[...] = sum(_up(dc, DN_CONV - 1 - k) * w_ref[pl.ds(k, 1), :] for k in range(DN_CONV))[8:8 + r]

        @pl.when(i == 0)
        def _():
            dw_ref[...] = jnp.zeros_like(dw_ref)

        for k in range(DN_CONV):
            dw_ref[pl.ds(k, 1), :] += jnp.sum((dc * xs[k])[8:8 + r], axis=0, keepdims=True)

    tile, prev, nxt = _halo(t, r, 8, DN_W, lambda j: j)
    wspec = pl.BlockSpec((DN_CONV, DN_W), lambda j, i: (0, j))
    return pl.pallas_call(
        body, name=name, grid=(3, ni),
        in_specs=[tile, prev, nxt, wspec, tile, nxt],
        out_specs=[tile, wspec],
        out_shape=[jax.ShapeDtypeStruct((t, 3 * DN_W), F32), jax.ShapeDtypeStruct((DN_CONV, 3 * DN_W), F32)],
        compiler_params=_cp(("parallel", "arbitrary")),
    )(p, p, p, conv_w, dqkv, dqkv)


def _ffn_act_fwd(up, cw, cb, name):
    t, f2 = up.shape
    f = f2 // 2
    r = _tile(t, (512, 256, 128))
    tc = _tile(f, (512, 256, 128))
    nj = f // tc

    def body(a_ref, ap_ref, b_ref, bp_ref, wa_ref, wb_ref, ca_ref, cb_ref, o_ref):
        i = pl.program_id(1)

        def conv(x_ref, xp_ref, w_ref, c_ref):
            xe = jnp.concatenate([jnp.where(i == 0, 0.0, xp_ref[...]), x_ref[...]], axis=0)
            return sum(_down(xe, FFN_CONV - 1 - k) * w_ref[pl.ds(k, 1), :] for k in range(FFN_CONV))[8:] + c_ref[...]

        ua = conv(a_ref, ap_ref, wa_ref, ca_ref)
        ub = conv(b_ref, bp_ref, wb_ref, cb_ref)
        o_ref[...] = (ua * _sig(ua) * ub).astype(o_ref.dtype)

    ta, pa, _ = _halo(t, r, 8, tc, lambda j: j)
    tb, pb, _ = _halo(t, r, 8, tc, lambda j: j + nj)
    wa = pl.BlockSpec((FFN_CONV, tc), lambda j, i: (0, j))
    wb = pl.BlockSpec((FFN_CONV, tc), lambda j, i: (0, j + nj))
    ca = pl.BlockSpec((1, tc), lambda j, i: (0, j))
    cbs = pl.BlockSpec((1, tc), lambda j, i: (0, j + nj))
    return pl.pallas_call(
        body, name=name, grid=(nj, t // r),
        in_specs=[ta, pa, tb, pb, wa, wb, ca, cbs], out_specs=ta,
        out_shape=jax.ShapeDtypeStruct((t, f), BF16), compiler_params=_cp(("parallel", "parallel")),
    )(up, up, up, up, cw, cw, cb, cb)


def _ffn_act_bwd(up, cw, cb, dact, name):
    t, f2 = up.shape
    f = f2 // 2
    r = _tile(t, (512, 256, 128))
    ni = t // r
    tc = _tile(f, (512, 256, 128))
    nj = f // tc

    def body(a_ref, ap_ref, an_ref, b_ref, bp_ref, bn_ref, wa_ref, wb_ref, ca_ref, cb_ref, d_ref, dn_ref,
             du_ref, dw_ref, dc_ref):
        i = pl.program_id(1)
        dua_ref, dub_ref, dwa_ref, dwb_ref, dca_ref, dcb_ref = (du_ref.at[0], du_ref.at[1], dw_ref.at[0], dw_ref.at[1],
                                                                  dc_ref.at[0], dc_ref.at[1])

        def ext(x_ref, xp_ref, xn_ref):
            return jnp.concatenate([jnp.where(i == 0, 0.0, xp_ref[...]), x_ref[...], xn_ref[...]], axis=0)

        ae, be = ext(a_ref, ap_ref, an_ref), ext(b_ref, bp_ref, bn_ref)
        as_ = [_down(ae, FFN_CONV - 1 - k) for k in range(FFN_CONV)]
        bs_ = [_down(be, FFN_CONV - 1 - k) for k in range(FFN_CONV)]
        ua = sum(as_[k] * wa_ref[pl.ds(k, 1), :] for k in range(FFN_CONV)) + ca_ref[...]
        ub = sum(bs_[k] * wb_ref[pl.ds(k, 1), :] for k in range(FFN_CONV)) + cb_ref[...]
        de = jnp.concatenate([jnp.zeros((8, tc), F32), d_ref[...].astype(F32),
                              jnp.where(i == ni - 1, 0.0, dn_ref[...].astype(F32))], axis=0)
        s = _sig(ua)
        dua = de * ub * _dsilu(ua, s)
        dub = de * ua * s
        dua_ref[...] = sum(_up(dua, FFN_CONV - 1 - k) * wa_ref[pl.ds(k, 1), :] for k in range(FFN_CONV))[8:8 + r].astype(dua_ref.dtype)
        dub_ref[...] = sum(_up(dub, FFN_CONV - 1 - k) * wb_ref[pl.ds(k, 1), :] for k in range(FFN_CONV))[8:8 + r].astype(dub_ref.dtype)

        @pl.when(i == 0)
        def _():
            dw_ref[...] = jnp.zeros_like(dw_ref)
            dc_ref[...] = jnp.zeros_like(dc_ref)

        for k in range(FFN_CONV):
            dwa_ref[pl.ds(k, 1), :] += jnp.sum((dua * as_[k])[8:8 + r], axis=0, keepdims=True)
            dwb_ref[pl.ds(k, 1), :] += jnp.sum((dub * bs_[k])[8:8 + r], axis=0, keepdims=True)
        dca_ref[...] += jnp.sum(dua[8:8 + r], axis=0, keepdims=True)
        dcb_ref[...] += jnp.sum(dub[8:8 + r], axis=0, keepdims=True)

    ta, pa, na = _halo(t, r, 8, tc, lambda j: j)
    tb, pb, nb = _halo(t, r, 8, tc, lambda j: j + nj)
    wa = pl.BlockSpec((FFN_CONV, tc), lambda j, i: (0, j))
    wb = pl.BlockSpec((FFN_CONV, tc), lambda j, i: (0, j + nj))
    ca = pl.BlockSpec((1, tc), lambda j, i: (0, j))
    cbs = pl.BlockSpec((1, tc), lambda j, i: (0, j + nj))
    return pl.pallas_call(
        body, name=name, grid=(nj, ni),
        in_specs=[ta, pa, na, tb, pb, nb, wa, wb, ca, cbs, ta, na],
        out_specs=[pl.BlockSpec((2, r, tc), lambda j, i: (0, i, j)), pl.BlockSpec((2, FFN_CONV, tc), lambda j, i: (0, 0, j)),
                   pl.BlockSpec((2, 1, tc), lambda j, i: (0, 0, j))],
        out_shape=[jax.ShapeDtypeStruct((2, t, f), BF16), jax.ShapeDtypeStruct((2, FFN_CONV, f), F32),
                   jax.ShapeDtypeStruct((2, 1, f), F32)],
        compiler_params=_cp(("parallel", "arbitrary")),
    )(up, up, up, up, up, up, cw, cw, cb, cb, dact, dact)


def _pool_pick(g, vals):
    return jnp.where(g == 0, vals[0], jnp.where(g == 1, vals[1], jnp.where(g == 2, vals[2], vals[3])))


def _pool_pre(xe, g, t0):
    s1 = xe + _down(xe, 1)
    s2 = s1 + _down(s1, 2)
    s3 = s2 + _down(s2, 4)
    s4 = s3 + _down(s3, 8)
    r = xe.shape[0] - 16
    pos = (t0 + lax.broadcasted_iota(jnp.int32, (r, 1), 0)).astype(F32)
    cnt = jnp.minimum(pos + 1.0, _pool_pick(g, (2.0, 4.0, 8.0, 16.0)))
    return _pool_pick(g, (s1, s2, s3, s4))[16:] / cnt - xe[16:]


def _pool_fwd(p, pool_w, scale, name):
    t = p.shape[0]
    r = _tile(t, (1024, 256, 128))

    def body(x_ref, xp_ref, w_ref, sc_ref, o_ref):
        g, i = pl.program_id(0), pl.program_id(1)
        xe = jnp.concatenate([jnp.where(i == 0, 0.0, xp_ref[...]), x_ref[...]], axis=0)
        pre = _pool_pre(xe, g, i * r)
        o_ref[...] = jnp.dot(pre, w_ref[0], preferred_element_type=F32) * sc_ref[...]

    tile, prev, _ = _halo(t, r, 16, 128, lambda j: CB_POOL + j)
    return pl.pallas_call(
        body, name=name, grid=(POOL_GROUPS, t // r),
        in_specs=[tile, prev, pl.BlockSpec((1, 128, 128), lambda j, i: (j, 0, 0)), pl.BlockSpec((1, 128), lambda j, i: (0, j))],
        out_specs=pl.BlockSpec((r, 128), lambda j, i: (i, j)),
        out_shape=jax.ShapeDtypeStruct((t, POOL_W), F32), compiler_params=_cp(("parallel", "parallel")),
    )(p, p, pool_w, scale)


def _pool_bwd(p, pool_w, scale, dycat, name):
    t = p.shape[0]
    r = _tile(t, (1024, 256, 128))
    ni = t // r

    def body(x_ref, xp_ref, w_ref, sc_ref, d_ref, dn_ref, dx_ref, dw_ref, dsc_ref):
        g, i = pl.program_id(0), pl.program_id(1)
        xe = jnp.concatenate([jnp.where(i == 0, 0.0, xp_ref[...]), x_ref[...]], axis=0)
        pre = _pool_pre(xe, g, i * r)
        w = w_ref[0]
        dy = d_ref[...]
        dye = jnp.concatenate([dy, jnp.where(i == ni - 1, 0.0, dn_ref[...])], axis=0)
        dpre = lax.dot_general(dye * sc_ref[...], w, _DN["nt"], preferred_element_type=F32)
        pos = (i * r + lax.broadcasted_iota(jnp.int32, (r + 16, 1), 0)).astype(F32)
        dm = dpre / jnp.minimum(pos + 1.0, _pool_pick(g, (2.0, 4.0, 8.0, 16.0)))
        a1 = dm + _up(dm, 1)
        a2 = a1 + _up(a1, 2)
        a3 = a2 + _up(a2, 4)
        a4 = a3 + _up(a3, 8)
        dx_ref[...] = (_pool_pick(g, (a1, a2, a3, a4)) - dpre)[:r]

        @pl.when(i == 0)
        def _():
            dw_ref[...] = jnp.zeros_like(dw_ref)
            dsc_ref[...] = jnp.zeros_like(dsc_ref)

        dw_ref[0] += lax.dot_general(pre, dy * sc_ref[...], _DN["tn"], preferred_element_type=F32)
        dsc_ref[...] += jnp.sum(dy * jnp.dot(pre, w, preferred_element_type=F32), axis=0, keepdims=True)

    tile, prev, _ = _halo(t, r, 16, 128, lambda j: CB_POOL + j)
    dtile, _, dnxt = _halo(t, r, 16, 128, lambda j: DN_W // 128 + j)
    wspec = pl.BlockSpec((1, 128, 128), lambda j, i: (j, 0, 0))
    sspec = pl.BlockSpec((1, 128), lambda j, i: (0, j))
    return pl.pallas_call(
        body, name=name, grid=(POOL_GROUPS, ni),
        in_specs=[tile, prev, wspec, sspec, dtile, dnxt],
        out_specs=[pl.BlockSpec((r, 128), lambda j, i: (i, j)), wspec, sspec],
        out_shape=[jax.ShapeDtypeStruct((t, POOL_W), F32), jax.ShapeDtypeStruct((POOL_GROUPS, 128, 128), F32),
                   jax.ShapeDtypeStruct((1, POOL_W), F32)],
        compiler_params=_cp(("parallel", "arbitrary")),
    )(p, p, pool_w, scale, dycat, dycat)


_DNB = {"nn": (((2,), (1,)), ((0,), (0,))), "nt": (((2,), (2,)), ((0,), (0,))), "tn": (((1,), (1,)), ((0,), (0,)))}


def _dot(a, b, mode="nn", precision=None):
    dn = _DNB[mode] if a.ndim == 3 else _DN[mode]
    return lax.dot_general(a, b, dn, precision=precision, preferred_element_type=F32)


@functools.partial(jax.custom_vjp, nondiff_argnums=(2,))
def _bdot(a, b, mode):
    return _dot(a.astype(BF16), b.astype(BF16), mode)


def _bdot_fwd(a, b, mode):
    return _bdot(a, b, mode), (a, b)


def _bdot_bwd(mode, res, g):
    a, b = res
    if mode == "nn":
        return _bdot(g, b, "nt"), _bdot(a, g, "tn")
    if mode == "nt":
        return _bdot(g, b, "nn"), _bdot(g, a, "tn")
    return _bdot(b, g, "nt"), _bdot(a, g, "nn")


_bdot.defvjp(_bdot_fwd, _bdot_bwd)


def _dn_consts():
    c = DN_CHUNK
    ii = lax.broadcasted_iota(jnp.int32, (c, c), 0)
    jj = lax.broadcasted_iota(jnp.int32, (c, c), 1)
    one, zero = jnp.ones((c, c), F32), jnp.zeros((c, c), F32)
    return dict(ltri=jnp.where(ii >= jj, one, zero), utri=jnp.where(ii <= jj, one, zero), ones=one,
                causal=ii >= jj, strict=ii > jj, eye=jnp.where(ii == jj, one, zero))


def _dn_chunk(q, k, v, z, bcol, acol, s_in, alog, dtb, nw, cs):
    c = DN_CHUNK
    hh = q.shape[0]
    per_head = lambda m: jnp.broadcast_to(m, (hh, c, c))
    beta = _sig(bcol)
    xa = acol + dtb
    g = -jnp.exp(alog) * (jnp.maximum(xa, 0.0) + jnp.log(1.0 + jnp.exp(-jnp.abs(xa))))
    gb = jnp.broadcast_to(g, (hh, c, HEAD_DIM))
    gbc = jnp.broadcast_to(g, (hh, c, c))
    gc = _dot(per_head(cs["ltri"]), gb, precision=HIGH)
    gcol = gc[:, :, :c]
    grow = jnp.swapaxes(gcol, 1, 2)
    decay = jnp.exp(jnp.where(cs["causal"], gcol - grow, -1e30))
    kb = k * beta
    vb = v * beta
    nil = -jnp.where(cs["strict"], _bdot(kb, k, "nt") * decay, 0.0)
    inv = cs["eye"] + nil
    powk = nil
    for _ in range(int(math.log2(c)) - 1):
        powk = _bdot(powk, powk, "nn")
        inv = _bdot(inv, cs["eye"] + powk, "nn")
    eg = jnp.exp(gc)
    u = _bdot(inv, vb, "nn")
    w = _bdot(inv, kb * eg, "nn")
    a = _bdot(q, k, "nt") * decay
    v_new = u - _bdot(w, s_in, "nn")
    o = _bdot(q * eg, s_in, "nn") + _bdot(a, v_new, "nn")
    glast = jnp.sum(gb, axis=1, keepdims=True)
    s_out = s_in * jnp.exp(glast) + _bdot(k * jnp.exp(glast - gc), v_new, "tn")
    on = o * lax.rsqrt(jnp.mean(o * o, axis=-1, keepdims=True) + NORM_EPS) * nw
    return on * (z * _sig(z)), s_out


def _lane_pick(x, lane, idx):
    return jnp.sum(jnp.where(lane == idx, x, 0.0), axis=1, keepdims=True)


def _dn_load(q_ref, k_ref, v_ref, z_ref, bd_ref, al_ref, dt_ref, nw_ref, s_in):
    lane = lax.broadcasted_iota(jnp.int32, (1, 128), 1)
    bd, al, dt = bd_ref[...], al_ref[...], dt_ref[...]
    heads = range(DN_HEADS)
    wide = lambda ref: jnp.stack([ref[:, h * 128:(h + 1) * 128] for h in heads], axis=0)
    col = lambda x, off: jnp.stack([_lane_pick(x, lane, off + h) for h in heads], axis=0)
    return (wide(q_ref), wide(k_ref), wide(v_ref), wide(z_ref), col(bd, 0), col(bd, DN_HEADS), s_in,
            col(al, 0), col(dt, 0), nw_ref[...])


def _dn_fwd(qkv, p, alog, dtb, nw, name):
    t = qkv.shape[0]
    c = DN_CHUNK
    n = t // c

    def body(q_ref, k_ref, v_ref, z_ref, bd_ref, al_ref, dt_ref, nw_ref, y_ref, ss_ref, s_scr):
        @pl.when(pl.program_id(0) == 0)
        def _():
            s_scr[...] = jnp.zeros_like(s_scr)

        s_in = s_scr[...]
        y, s_out = _dn_chunk(*_dn_load(q_ref, k_ref, v_ref, z_ref, bd_ref, al_ref, dt_ref, nw_ref, s_in), _dn_consts())
        ss_ref[0] = s_in
        s_scr[...] = s_out
        for h in range(DN_HEADS):
            y_ref[:, h * 128:(h + 1) * 128] = y[h]

    wide = lambda j: pl.BlockSpec((c, DN_W), lambda i: (i, j))
    vec = pl.BlockSpec((1, 128), lambda i: (0, 0))
    return pl.pallas_call(
        body, name=name, grid=(n,),
        in_specs=[wide(0), wide(1), wide(2), wide(3), pl.BlockSpec((c, 128), lambda i: (i, CB_BD)), vec, vec, vec],
        out_specs=[wide(0), pl.BlockSpec((1, DN_HEADS, 128, 128), lambda i: (i, 0, 0, 0))],
        out_shape=[jax.ShapeDtypeStruct((t, DN_W), F32), jax.ShapeDtypeStruct((n, DN_HEADS, 128, 128), F32)],
        scratch_shapes=[pltpu.VMEM((DN_HEADS, 128, 128), F32)],
        compiler_params=_cp(("arbitrary",)),
    )(qkv, qkv, qkv, p, p, alog, dtb, nw)


def _dn_bwd(qkv, p, alog, dtb, nw, states, dycat, name, carry=None):
    t = qkv.shape[0]
    c = DN_CHUNK
    n = t // c
    sums, kinds = carry if carry is not None else ((), ())
    na = len(sums)

    def body(*refs):
        q_ref, k_ref, v_ref, z_ref, bd_ref, al_ref, dt_ref, nw_ref, ss_ref, dy_ref = refs[:10]
        dqkv_ref, dz_ref, dbd_ref, dal_ref, ddt_ref, dnw_ref = refs[10 + na:16 + na]
        ds_scr = refs[16 + 2 * na]
        if na:
            copies = _scatter_copies(refs[10:10 + na], refs[16 + na:16 + 2 * na], kinds, *refs[17 + 2 * na:])

        @pl.when(pl.program_id(0) == 0)
        def _():
            ds_scr[...] = jnp.zeros_like(ds_scr)
            dal_ref[...] = jnp.zeros_like(dal_ref)
            ddt_ref[...] = jnp.zeros_like(ddt_ref)
            dnw_ref[...] = jnp.zeros_like(dnw_ref)
            if na:
                for cp in copies:
                    cp.start()

        lane = lax.broadcasted_iota(jnp.int32, (1, 128), 1)
        args = _dn_load(q_ref, k_ref, v_ref, z_ref, bd_ref, al_ref, dt_ref, nw_ref, ss_ref[0])
        dy = jnp.stack([dy_ref[:, h * 128:(h + 1) * 128] for h in range(DN_HEADS)], axis=0)
        _, vjp = jax.vjp(functools.partial(_dn_chunk, cs=_dn_consts()), *args)
        gq, gk, gv, gz, gb, ga, gs, gal, gdt, gnw = vjp((dy, ds_scr[...]))
        ds_scr[...] = gs
        dbd = jnp.zeros((c, 128), F32)
        dal = jnp.zeros((1, 128), F32)
        ddt = jnp.zeros((1, 128), F32)
        for h in range(DN_HEADS):
            sl = slice(h * 128, (h + 1) * 128)
            dqkv_ref[:, sl] = gq[h]
            dqkv_ref[:, DN_W + h * 128:DN_W + (h + 1) * 128] = gk[h]
            dqkv_ref[:, 2 * DN_W + h * 128:2 * DN_W + (h + 1) * 128] = gv[h]
            dz_ref[:, sl] = gz[h]
            dbd = dbd + jnp.where(lane == h, gb[h], 0.0) + jnp.where(lane == DN_HEADS + h, ga[h], 0.0)
            dal = dal + jnp.where(lane == h, gal[h], 0.0)
            ddt = ddt + jnp.where(lane == h, gdt[h], 0.0)
        dbd_ref[...] = dbd
        dal_ref[...] += dal
        ddt_ref[...] += ddt
        dnw_ref[...] += gnw

        if na:
            @pl.when(pl.program_id(0) == n - 1)
            def _():
                for cp in copies:
                    cp.wait()

    rev = lambda i: n - 1 - i
    wide = lambda j: pl.BlockSpec((c, DN_W), lambda i: (rev(i), j))
    vec = pl.BlockSpec((1, 128), lambda i: (0, 0))
    any_space = pl.BlockSpec(memory_space=pl.ANY)
    return pl.pallas_call(
        body, name=name, grid=(n,),
        in_specs=[wide(0), wide(1), wide(2), wide(3), pl.BlockSpec((c, 128), lambda i: (rev(i), CB_BD)), vec, vec, vec,
                  pl.BlockSpec((1, DN_HEADS, 128, 128), lambda i: (rev(i), 0, 0, 0)), wide(0)] + [any_space] * na,
        out_specs=[pl.BlockSpec((c, 3 * DN_W), lambda i: (rev(i), 0)), wide(0),
                   pl.BlockSpec((c, 128), lambda i: (rev(i), 0)), vec, vec, vec] + [any_space] * na,
        out_shape=[jax.ShapeDtypeStruct((t, 3 * DN_W), F32), jax.ShapeDtypeStruct((t, DN_W), F32),
                   jax.ShapeDtypeStruct((t, 128), F32), jax.ShapeDtypeStruct((1, 128), F32),
                   jax.ShapeDtypeStruct((1, 128), F32), jax.ShapeDtypeStruct((1, 128), F32)] + _scatter_shapes(sums, kinds),
        scratch_shapes=[pltpu.VMEM((DN_HEADS, 128, 128), F32)] + ([pltpu.SemaphoreType.DMA((na, 3))] * 2 if na else []),
        compiler_params=_cp(("arbitrary",)),
    )(qkv, qkv, qkv, p, p, alog, dtb, nw, states, dycat, *sums)


def _rope(x, cosf, sins):
    return x * cosf + pltpu.roll(x, HEAD_DIM // 2, 1) * sins


def _rope_t(d, cosf, sins):
    return d * cosf + pltpu.roll(d * sins, HEAD_DIM // 2, 1)


def _swa_masks():
    b = SWA_BLOCK
    i = lax.broadcasted_iota(jnp.int32, (SWA_GROUP * b, b), 0) & (b - 1)
    j = lax.broadcasted_iota(jnp.int32, (SWA_GROUP * b, b), 1)
    return j > i, j <= i


def _swa_sink_col(sinks_ref, h):
    b = SWA_BLOCK
    r = lax.broadcasted_iota(jnp.int32, (SWA_GROUP * b, 1), 0)
    s = [sinks_ref[h * SWA_GROUP + g] for g in range(SWA_GROUP)]
    return jnp.where(r < b, s[0], jnp.where(r < 2 * b, s[1], s[2]))


def _swa_specs(t, h_first):
    nb = t // SWA_BLOCK

    def at(col, off):
        def imap(h, n):
            return (jnp.clip(n + off, 0, nb - 1), col(h))
        return pl.BlockSpec((SWA_BLOCK, 128), imap)
    return at


def _swa_fwd(p, cosf, sins, sinks, name):
    t = p.shape[0]
    b = SWA_BLOCK
    nb = t // b
    at = _swa_specs(t, None)
    scale = HEAD_DIM ** -0.5

    def body(q0, q1, q2, kp, kc, vp, vc, cc, sc, cp, sp, sinks_ref, o_ref, lse_ref):
        h, n = pl.program_id(0), pl.program_id(1)
        qs = jnp.concatenate([_rope(q[...], cc[...], sc[...]) for q in (q0, q1, q2)], axis=0)
        ks = jnp.concatenate([_rope(kp[...], cp[...], sp[...]), _rope(kc[...], cc[...], sc[...])], axis=0)
        vs = jnp.concatenate([vp[...], vc[...]], axis=0)
        mp, mc = _swa_masks()
        mask = jnp.concatenate([mp & (n > 0), mc], axis=1)
        s = jnp.where(mask, _dot(qs, ks, "nt") * scale, -1e30)
        sink = _swa_sink_col(sinks_ref, h)
        m = jnp.maximum(jnp.max(s, axis=1, keepdims=True), sink)
        e = jnp.exp(s - m)
        l = jnp.sum(e, axis=1, keepdims=True) + jnp.exp(sink - m)
        o = _dot(e, vs) / l
        lse = m + jnp.log(l)
        lane = lax.broadcasted_iota(jnp.int32, (1, 128), 1)
        tile = jnp.zeros((b, 128), F32)
        for g in range(SWA_GROUP):
            o_ref[:, g * 128:(g + 1) * 128] = o[g * b:(g + 1) * b]
            tile = tile + jnp.where(lane == g, lse[g * b:(g + 1) * b], 0.0)
        lse_ref[0] = tile

    qcol = lambda g: (lambda h: CB_SQ + h * SWA_GROUP + g)
    kcol, vcol, one = (lambda h: CB_SK + h), (lambda h: CB_SV + h), (lambda h: 0)
    in_specs = [at(qcol(0), 0), at(qcol(1), 0), at(qcol(2), 0), at(kcol, -1), at(kcol, 0), at(vcol, -1), at(vcol, 0),
                at(one, 0), at(one, 0), at(one, -1), at(one, -1), pl.BlockSpec(memory_space=pltpu.SMEM)]
    return pl.pallas_call(
        body, name=name, grid=(SWA_KV_HEADS, nb), in_specs=in_specs,
        out_specs=[pl.BlockSpec((b, SWA_GROUP * 128), lambda h, n: (n, h)), pl.BlockSpec((1, b, 128), lambda h, n: (h, n, 0))],
        out_shape=[jax.ShapeDtypeStruct((t, SWA_W), F32), jax.ShapeDtypeStruct((SWA_KV_HEADS, t, 128), F32)],
        compiler_params=_cp(("parallel", "parallel")),
    )(p, p, p, p, p, p, p, cosf, sins, cosf, sins, sinks)


def _swa_bwd(p, cosf, sins, sinks, o, lse, dycat, name):
    t = p.shape[0]
    b = SWA_BLOCK
    nb = t // b
    at = _swa_specs(t, None)
    scale = HEAD_DIM ** -0.5
    gb = SWA_GROUP * b

    def body(q0, q1, q2, r0, r1, r2, kp, kc, vp, vc, cc, sc, cp, sp, cn, sn, d0, d1, d2, e0, e1, e2,
             oc_ref, on_ref, lc_ref, ln_ref, sinks_ref, dq_ref, dk_ref, dv_ref, dsk_ref):
        h, n = pl.program_id(0), pl.program_id(1)
        lane = lax.broadcasted_iota(jnp.int32, (1, 128), 1)
        stack = lambda refs: jnp.concatenate([x[...] for x in refs], axis=0)
        q_c = jnp.concatenate([_rope(q[...], cc[...], sc[...]) for q in (q0, q1, q2)], axis=0)
        q_n = jnp.concatenate([_rope(q[...], cn[...], sn[...]) for q in (r0, r1, r2)], axis=0)
        k_p = _rope(kp[...], cp[...], sp[...])
        k_c = _rope(kc[...], cc[...], sc[...])
        do_c, do_n = stack((d0, d1, d2)), stack((e0, e1, e2))
        o_c = jnp.concatenate([oc_ref[:, g * 128:(g + 1) * 128] for g in range(SWA_GROUP)], axis=0)
        o_n = jnp.concatenate([on_ref[:, g * 128:(g + 1) * 128] for g in range(SWA_GROUP)], axis=0)
        lse_c = jnp.concatenate([_lane_pick(lc_ref[0], lane, g) for g in range(SWA_GROUP)], axis=0)
        lse_n = jnp.concatenate([_lane_pick(ln_ref[0], lane, g) for g in range(SWA_GROUP)], axis=0)
        dl_c = jnp.sum(do_c * o_c, axis=1, keepdims=True)
        dl_n = jnp.sum(do_n * o_n, axis=1, keepdims=True)
        mp, mc = _swa_masks()

        def pair(qr, kr, v, do, lse_, dl, mask):
            s = _dot(qr, kr, "nt") * scale
            pr = jnp.where(mask, jnp.exp(s - lse_), 0.0)
            ds = pr * (_dot(do, v, "nt") - dl) * scale
            return _dot(ds, kr), _dot(ds, qr, "tn"), _dot(pr, do, "tn")

        dq_a, _, _ = pair(q_c, k_p, vp[...], do_c, lse_c, dl_c, mp & (n > 0))
        dq_b, dk_b, dv_b = pair(q_c, k_c, vc[...], do_c, lse_c, dl_c, mc)
        _, dk_n, dv_n = pair(q_n, k_c, vc[...], do_n, lse_n, dl_n, mp & (n < nb - 1))
        dq = dq_a + dq_b
        for g in range(SWA_GROUP):
            dq_ref[:, g * 128:(g + 1) * 128] = _rope_t(dq[g * b:(g + 1) * b], cc[...], sc[...])
        dk_ref[...] = _rope_t(dk_b + dk_n, cc[...], sc[...])
        dv_ref[...] = dv_b + dv_n

        @pl.when(n == 0)
        def _():
            dsk_ref[...] = jnp.zeros_like(dsk_ref)

        w = -jnp.exp(_swa_sink_col(sinks_ref, h) - lse_c) * dl_c
        acc = jnp.zeros((1, 128), F32)
        for g in range(SWA_GROUP):
            acc = acc + jnp.where(lane == g, jnp.sum(w[g * b:(g + 1) * b], axis=0, keepdims=True), 0.0)
        dsk_ref[0] += jnp.broadcast_to(acc, (8, 128))

    qcol = lambda g: (lambda h: CB_SQ + h * SWA_GROUP + g)
    dcol = lambda g: (lambda h: (DN_W + POOL_W) // 128 + h * SWA_GROUP + g)
    kcol, vcol, one = (lambda h: CB_SK + h), (lambda h: CB_SV + h), (lambda h: 0)
    wide = lambda off: pl.BlockSpec((b, SWA_GROUP * 128), lambda h, n: (jnp.clip(n + off, 0, nb - 1), h))
    lspec = lambda off: pl.BlockSpec((1, b, 128), lambda h, n: (h, jnp.clip(n + off, 0, nb - 1), 0))
    in_specs = ([at(qcol(g), 0) for g in range(3)] + [at(qcol(g), 1) for g in range(3)]
                + [at(kcol, -1), at(kcol, 0), at(vcol, -1), at(vcol, 0)]
                + [at(one, 0), at(one, 0), at(one, -1), at(one, -1), at(one, 1), at(one, 1)]
                + [at(dcol(g), 0) for g in range(3)] + [at(dcol(g), 1) for g in range(3)]
                + [wide(0), wide(1), lspec(0), lspec(1), pl.BlockSpec(memory_space=pltpu.SMEM)])
    kv_out = pl.BlockSpec((b, 128), lambda h, n: (n, h))
    return pl.pallas_call(
        body, name=name, grid=(SWA_KV_HEADS, nb), in_specs=in_specs,
        out_specs=[wide(0), kv_out, kv_out, pl.BlockSpec((1, 8, 128), lambda h, n: (h, 0, 0))],
        out_shape=[jax.ShapeDtypeStruct((t, SWA_W), F32), jax.ShapeDtypeStruct((t, SWA_KV_W), F32),
                   jax.ShapeDtypeStruct((t, SWA_KV_W), F32), jax.ShapeDtypeStruct((SWA_KV_HEADS, 8, 128), F32)],
        compiler_params=_cp(("parallel", "arbitrary")),
    )(*([p] * 10), cosf, sins, cosf, sins, cosf, sins, *([dycat] * 6), o, o, lse, lse, sinks)


def _adam_math(w, g, m, v):
    m = ADAM_B1 * m + (1.0 - ADAM_B1) * g
    v = ADAM_B2 * v + (1.0 - ADAM_B2) * (g * g)
    m_hat = m / (1.0 - ADAM_B1 ** ADAM_STEP)
    v_hat = v / (1.0 - ADAM_B2 ** ADAM_STEP)
    return -ADAM_LR * (m_hat / (jnp.sqrt(v_hat) + ADAM_EPS) + ADAM_WD * w), m, v


def _adamw(w, g, m, v, name):
    shape = w.shape
    cols = shape[-1]
    rows = math.prod(shape[:-1])
    flat = lambda a: a.reshape(rows, cols)
    r = rows
    for cand in (512, 256, 128, 64, 32, 16, 8):
        if rows % cand == 0 and cand * cols * 4 <= (1 << 20):
            r = cand
            break

    def body(w_ref, g_ref, m_ref, v_ref, d_ref, nm_ref, nv_ref):
        d_ref[...], nm_ref[...], nv_ref[...] = _adam_math(w_ref[...], g_ref[...], m_ref[...], v_ref[...])

    spec = pl.BlockSpec((r, cols), lambda i: (i, 0))
    outs = pl.pallas_call(
        body, name=name, grid=(rows // r,), in_specs=[spec] * 4, out_specs=[spec] * 3,
        out_shape=[jax.ShapeDtypeStruct((rows, cols), F32)] * 3, compiler_params=_cp(("parallel",)),
    )(flat(w), flat(g), flat(m), flat(v))
    return tuple(o.reshape(shape) for o in outs)


BIG = ("w_in", "w_out", "ffn_w_up", "ffn_w_down")
CONV = ("dn_conv_w", "ffn_conv_w")
KIND = {"w_in": "col", "w_out": "row", "ffn_w_up": "col", "ffn_w_down": "row"}
SMALL = ("norm_mix_pre", "dn_a_log", "dn_dt_bias", "dn_norm_w", "pool_w", "pool_scale", "swa_sinks",
         "norm_mix_post", "norm_ffn_pre", "ffn_conv_b", "norm_ffn_post")
WEIGHTS = ("norm_mix_pre", "w_in", "dn_conv_w", "dn_a_log", "dn_dt_bias", "dn_norm_w", "pool_w", "pool_scale",
           "swa_sinks", "w_out", "norm_mix_post", "norm_ffn_pre", "ffn_w_up", "ffn_conv_w", "ffn_conv_b",
           "ffn_w_down", "norm_ffn_post")


def _pad_in(w):
    z = lambda n: jnp.zeros(w.shape[:-1] + (n,), w.dtype)
    return jnp.concatenate([w[..., :GATE_END], z(CB_POOL * 128 - GATE_END), w[..., GATE_END:],
                            z(IN_PAD - CB_POOL * 128 - (IN_TRUE - GATE_END))], axis=-1)


def _unpad_in(g):
    return jnp.concatenate([g[..., :GATE_END], g[..., CB_POOL * 128:CB_POOL * 128 + IN_TRUE - GATE_END]], axis=-1)


IN_SHARD = IN_TRUE // 4
IN_SHARD_PAD = -(-IN_SHARD // 128) * 128


def _chip_cols_to_true(w):
    by_chip = w.reshape(w.shape[:-1] + (4, IN_SHARD_PAD))[..., :IN_SHARD]
    return by_chip.reshape(w.shape[:-1] + (IN_TRUE,))


def _true_to_chip_cols(g):
    by_chip = g.reshape(g.shape[:-1] + (4, IN_SHARD))
    by_chip = jnp.pad(by_chip, [(0, 0)] * (by_chip.ndim - 1) + [(0, IN_SHARD_PAD - IN_SHARD)])
    return by_chip.reshape(g.shape[:-1] + (4 * IN_SHARD_PAD,))


def _lanes(v):
    return jnp.zeros((1, 128), F32).at[0, :v.shape[0]].set(v)


def _rope_tables(positions):
    inv_freq = 1.0 / (ROPE_THETA ** (jnp.arange(0, HEAD_DIM, 2, dtype=F32) / HEAD_DIM))
    ang = positions.astype(F32)[:, None] * inv_freq
    cos, sin = jnp.cos(ang), jnp.sin(ang)
    return jnp.concatenate([cos, cos], axis=-1), jnp.concatenate([-sin, sin], axis=-1)


class _GradReduce:
    def __init__(self, place, shard_shapes):
        self.place = place
        self.out = {k: lax.empty(shard_shapes[k], F32) for k in BIG}
        self.pending = []

    def submit(self, l, dw):
        def parts(k, g):
            if KIND[k] == "row":
                return g.reshape(4, -1, g.shape[1])
            return (_true_to_chip_cols(_unpad_in(g)) if k == "w_in" else g)[None]

        names = [k for k in BIG if k in dw]
        tag = f"l{l}_" + "_".join(names)
        mine = [parts(k, dw[k]) for k in names]
        got = _swap_sibling(mine, tag + "_to_sibling")
        self.pending += [(l, k, _chip_sum(a, b, self.place, f"l{l}_chip_sum_{k}")) for k, a, b in zip(names, mine, got)]

    def carry(self):
        return ([s for _, _, s in self.pending], [KIND[k] for _, k, _ in self.pending]) if self.pending else None

    def arrived(self, got):
        for (l, k, own), g in zip(self.pending, got):
            self.out[k] = _owner_sum(own, g, KIND[k], self.place, (self.out[k], l), f"l{l}_owner_sum_{k}")
        self.pending = []

    def finish(self):
        sums, kinds = self.carry()
        self.arrived(_scatter_chips(sums, kinds, "last_grads_to_owner"))
        return dict(zip(BIG, _join_halves([self.out[k] for k in BIG], "grads_join")))


class _LayerWeights:
    def __init__(self, layers):
        self.layers = layers

    def layer(self, l):
        return self.layers[l]

    def carry(self, l, k):
        return None


class _WeightGather(_LayerWeights):
    def __init__(self, shards, place):
        depth = shards["w_out"].shape[0]
        self.kinds = [KIND[k] for k in BIG]
        self.raw = [{k: _spread_shard(shards[k], l, KIND[k], place, BF16, f"l{l}_cast_{k}") for k in BIG} for l in range(depth)]
        first = _gather_ici([self.raw[0][k] for k in BIG], self.kinds, "l0_gather")
        self.layers = {0: self._passed(0, first)}

    def _passed(self, l, arrs):
        full = dict(zip(BIG, _gather_pass(arrs, self.kinds, f"l{l}_gather_pass")))
        full["w_in"] = _pad_in(_chip_cols_to_true(full["w_in"]))
        return full

    def carry(self, l, k):
        return (self.raw[l + 1][k], KIND[k]) if l + 1 < len(self.raw) else None

    def carried(self, l, landed):
        self.layers[l + 1] = self._passed(l + 1, [landed[k] for k in BIG])


def _local_step(x, positions, target, w, mats, reduce=None):
    depth = w["norm_mix_pre"].shape[0]
    t = x.shape[0]
    cosf, sins = _rope_tables(positions)
    saved = []
    for l in range(depth):
        nm = f"l{l}_"
        n1, n2, n3, n4 = (w[k][l][None] for k in ("norm_mix_pre", "norm_mix_post", "norm_ffn_pre", "norm_ffn_post"))
        alog, dtb, dnw = _lanes(w["dn_a_log"][l]), _lanes(w["dn_dt_bias"][l]), w["dn_norm_w"][l][None]
        psc, cb = w["pool_scale"][l][None], w["ffn_conv_b"][l][None]
        big = mats.layer(l)
        landed = {}

        def project(a, k, name):
            riding = mats.carry(l, k)
            if riding is None:
                return _mm(a, big[k], "nn", F32, name)
            out, landed[k] = _mm(a, big[k], "nn", F32, name, carry=riding)
            return out

        h = _norm_fwd(x, n1, nm + "norm1")
        p = project(h, "w_in", nm + "in_proj")
        qkv = _dn_pre_fwd(p, w["dn_conv_w"][l], nm + "dn_pre")
        y_dn, st = _dn_fwd(qkv, p, alog, dtb, dnw, nm + "dn")
        y_pool = _pool_fwd(p, w["pool_w"][l], psc, nm + "pool")
        y_swa, lse = _swa_fwd(p, cosf, sins, w["swa_sinks"][l], nm + "swa")
        ycat = jnp.concatenate([y_dn, y_pool, y_swa], axis=1).astype(BF16)
        mix = project(ycat, "w_out", nm + "out_proj")
        x1 = _resnorm_fwd(x, mix, n2, nm + "res1")
        h2 = _norm_fwd(x1, n3, nm + "norm3")
        up = project(h2, "ffn_w_up", nm + "ffn_up")
        act = _ffn_act_fwd(up, w["ffn_conv_w"][l], cb, nm + "ffn_act")
        f = project(act, "ffn_w_down", nm + "ffn_down")
        if landed:
            mats.carried(l, landed)
        x2 = _resnorm_fwd(x1, f, n4, nm + "res2")
        saved.append(dict(x=x, h=h, p=p, qkv=qkv, st=st, y_swa=y_swa, lse=lse, ycat=ycat, mix=mix, x1=x1, h2=h2,
                          up=up, act=act, f=f, n=(n1, n2, n3, n4), alog=alog, dtb=dtb, dnw=dnw, psc=psc, cb=cb))
        x = x2
    loss, dx = _loss_head(x, target, "loss_head")
    grads = {k: [None] * depth for k in WEIGHTS}
    for l in reversed(range(depth)):
        nm, s = f"l{l}_b_", saved[l]
        n1, n2, n3, n4 = s["n"]
        big = mats.layer(l)
        df, g4 = _norm_bwd(s["f"], n4, dx, None, BF16, nm + "res2")
        dact = _mm(df, big["ffn_w_down"], "nt", F32, nm + "ffn_down_dx")
        grads["ffn_w_down"][l] = _mm(s["act"], df, "tn", BF16, nm + "ffn_down_dw")
        dup, dcw, dcb = _ffn_act_bwd(s["up"], w["ffn_conv_w"][l], s["cb"], dact, nm + "ffn_act")
        grads["ffn_conv_w"][l] = jnp.concatenate([dcw[0], dcw[1]], axis=1)
        grads["ffn_conv_b"][l] = jnp.concatenate([dcb[0], dcb[1]], axis=1)[0]
        grads["ffn_w_up"][l] = _mm(s["h2"], dup, "tn", BF16, nm + "ffn_up_dw", b_pick="split")
        dh2 = _mm(dup, big["ffn_w_up"], "nt", BF16, nm + "ffn_up_dx", a_pick="split")
        dx1, g3 = _norm_bwd(s["x1"], n3, dh2, dx, F32, nm + "norm3")
        dmix, g2 = _norm_bwd(s["mix"], n2, dx1, None, BF16, nm + "res1")
        grads["w_out"][l] = _mm(s["ycat"], dmix, "tn", BF16, nm + "out_proj_dw")
        dycat = _mm(dmix, big["w_out"], "nt", F32, nm + "out_proj_dx")
        if reduce is not None:
            reduce.submit(l, {k: grads[k][l] for k in ("ffn_w_down", "ffn_w_up", "w_out")})
        carry = reduce.carry() if reduce is not None else None
        res = _dn_bwd(s["qkv"], s["p"], s["alog"], s["dtb"], s["dnw"], s["st"], dycat, nm + "dn", carry=carry)
        dqkv, dz, dbd, gal, gdt, gnw = res[:6]
        if carry is not None:
            reduce.arrived(res[6:])
        dpq, gconv = _dn_pre_bwd(s["p"], w["dn_conv_w"][l], dqkv, nm + "dn_pre")
        dpool, gpw, gpsc = _pool_bwd(s["p"], w["pool_w"][l], s["psc"], dycat, nm + "pool")
        dsq, dsk, dsv, gsk = _swa_bwd(s["p"], cosf, sins, w["swa_sinks"][l], s["y_swa"], s["lse"], dycat, nm + "swa")
        dp = jnp.concatenate([dpq, dz, dbd, dpool, dsq, dsk, dsv, jnp.zeros((t, 128), F32)], axis=1).astype(BF16)
        grads["w_in"][l] = _mm(s["h"], dp, "tn", BF16, nm + "in_proj_dw")
        dh = _mm(dp, big["w_in"], "nt", BF16, nm + "in_proj_dx")
        dx, g1 = _norm_bwd(s["x"], n1, dh, dx1, F32, nm + "norm1")
        if reduce is not None:
            reduce.submit(l, {"w_in": grads["w_in"][l]})
        grads["norm_mix_pre"][l], grads["norm_mix_post"][l] = g1[0], g2[0]
        grads["norm_ffn_pre"][l], grads["norm_ffn_post"][l] = g3[0], g4[0]
        grads["dn_conv_w"][l] = gconv
        grads["dn_a_log"][l], grads["dn_dt_bias"][l], grads["dn_norm_w"][l] = gal[0, :DN_HEADS], gdt[0, :DN_HEADS], gnw[0]
        grads["pool_w"][l], grads["pool_scale"][l] = gpw, gpsc[0]
        grads["swa_sinks"][l] = gsk[:, 0, :SWA_GROUP].reshape(SWA_HEADS)
    return loss, dx, grads


def _flat2(a):
    return a.reshape(math.prod(a.shape[:-1]), a.shape[-1])


def _ew_rows(rows, cols, n_arrays):
    for cand in (512, 256, 128, 64, 32, 16):
        if rows % cand == 0 and cand * cols * 4 * n_arrays <= (8 << 20):
            return cand
    return rows


def _spread_shard(a, layer, kind, place, dtype, name):
    _, rows, cols = a.shape
    r = _ew_rows(rows, cols, 2)
    nb = rows // r

    def body(s_ref, a_ref, o_ref):
        o_ref[...] = a_ref[...].astype(o_ref.dtype)

    if kind == "row":
        out_spec = pl.BlockSpec((r, cols), lambda i, s: (s[0] * nb + i, 0))
        out_shape = (4 * rows, cols)
    else:
        out_spec = pl.BlockSpec((r, cols), lambda i, s: (i, s[0]))
        out_shape = (rows, 4 * cols)
    return pl.pallas_call(
        body, name=name,
        grid_spec=pltpu.PrefetchScalarGridSpec(
            num_scalar_prefetch=1, grid=(nb,),
            in_specs=[pl.BlockSpec((None, r, cols), lambda i, s: (layer, i, 0))], out_specs=out_spec),
        out_shape=jax.ShapeDtypeStruct(out_shape, dtype), compiler_params=_cp(("parallel",)),
    )(place, a)


def _chip_sum(mine, sib, place, name):
    parts, rows, cols = sib.shape
    r = _ew_rows(rows, cols, 3)
    nb = rows // r

    def body(s_ref, a_ref, b_ref, o_ref):
        o_ref[...] = (a_ref[...].astype(F32) + b_ref[...].astype(F32)).astype(o_ref.dtype)

    spec = pl.BlockSpec((None, r, cols), lambda j, i, s: (j, i, 0))
    return pl.pallas_call(
        body, name=name,
        grid_spec=pltpu.PrefetchScalarGridSpec(
            num_scalar_prefetch=1, grid=(parts, nb),
            in_specs=[pl.BlockSpec((None, r, cols), lambda j, i, s: (j, s[1] * nb + i, 0)), spec], out_specs=spec),
        out_shape=jax.ShapeDtypeStruct(sib.shape, BF16), compiler_params=_cp(("parallel", "parallel")),
    )(place, mine, sib)


def _sum_slots(a, name):
    s = a.shape[0]
    a3 = a.reshape(s, math.prod(a.shape[1:-1]), a.shape[-1])
    _, rows, cols = a3.shape
    r = _ew_rows(rows, cols, s + 1)

    def body(a_ref, o_ref):
        acc = a_ref[0].astype(F32)
        for k in range(1, s):
            acc = acc + a_ref[k].astype(F32)
        o_ref[...] = acc

    return pl.pallas_call(body, name=name, grid=(rows // r,),
                          in_specs=[pl.BlockSpec((s, r, cols), lambda i: (0, i, 0))],
                          out_specs=pl.BlockSpec((r, cols), lambda i: (i, 0)),
                          out_shape=jax.ShapeDtypeStruct((rows, cols), F32), compiler_params=_cp(("parallel",)),
                          )(a3).reshape(a.shape[1:])


def _owner_sum(own, got, kind, place, into, name):
    buf, slab = into
    _, rows, cols = got.shape
    r = _ew_rows(rows, cols, 6)
    nb = rows // r

    def body(s_ref, own_ref, got_ref, buf_ref, o_ref):
        acc = own_ref[...].astype(F32)
        for k in range(3):
            acc = acc + got_ref[k].astype(F32)
        o_ref[...] = acc

    if kind == "row":
        own_spec = pl.BlockSpec((None, r, cols), lambda i, s: (s[0], i, 0))
    else:
        own_spec = pl.BlockSpec((None, r, cols), lambda i, s: (0, i, s[0]))
    return pl.pallas_call(
        body, name=name,
        grid_spec=pltpu.PrefetchScalarGridSpec(
            num_scalar_prefetch=1, grid=(nb,),
            in_specs=[own_spec, pl.BlockSpec((3, r, cols), lambda i, s: (0, i, 0)), pl.BlockSpec(memory_space=pl.ANY)],
            out_specs=pl.BlockSpec((None, r, cols), lambda i, s: (slab, s[1] * nb + i, 0))),
        out_shape=jax.ShapeDtypeStruct(buf.shape, buf.dtype), input_output_aliases={3: 0},
        compiler_params=_cp(("parallel",)),
    )(place, own, got, buf)


MESH = pl.DeviceIdType.MESH
ANY = pl.BlockSpec(memory_space=pl.ANY)


def _place():
    x, y, c = lax.axis_index("x"), lax.axis_index("y"), lax.axis_index("c")
    chips = [(1 - x, y), (x, 1 - y), (1 - x, 1 - y)]
    return x, y, c, chips


def _half_part(ref, kind, chip, half):
    if kind == "row":
        h = ref.shape[0] // 8
        return ref.at[pl.ds(pl.multiple_of((2 * chip + half) * h, 16), h), :]
    h, width = ref.shape[0] // 2, ref.shape[1] // 4
    return ref.at[pl.ds(pl.multiple_of(half * h, 16), h), pl.ds(pl.multiple_of(chip * width, 128), width)]


def _gather_copies(w_ref, kind, send, recv):
    x, y, c, chips = _place()
    mine = _half_part(w_ref, kind, 2 * x + y, c)
    return [pltpu.make_async_remote_copy(mine, mine, send.at[j], recv.at[j], device_id=(px, py, c), device_id_type=MESH)
            for j, (px, py) in enumerate(chips)]


def _gather_ici(arrs, kinds, name):
    na = len(arrs)

    def body(*refs):
        outs, send, recv = refs[na:2 * na], refs[2 * na], refs[2 * na + 1]
        cps = [cp for k in range(na) for cp in _gather_copies(outs[k], kinds[k], send.at[k], recv.at[k])]
        for cp in cps:
            cp.start()
        for cp in cps:
            cp.wait()

    return pl.pallas_call(
        body, name=name, in_specs=[ANY] * na, out_specs=[ANY] * na,
        out_shape=[jax.ShapeDtypeStruct(a.shape, a.dtype) for a in arrs],
        input_output_aliases={k: k for k in range(na)},
        scratch_shapes=[pltpu.SemaphoreType.DMA((na, 3))] * 2,
    )(*arrs)


def _gather_pass(arrs, kinds, name):
    na = len(arrs)

    def body(*refs):
        outs, send, recv = refs[na:2 * na], refs[2 * na], refs[2 * na + 1]
        x, y, c, chips = _place()
        cps, arrivals = [], []
        for k in range(na):
            for j, (px, py) in enumerate(chips):
                mine = _half_part(outs[k], kinds[k], 2 * px + py, c)
                theirs = _half_part(outs[k], kinds[k], 2 * px + py, 1 - c)
                cps.append(pltpu.make_async_remote_copy(mine, mine, send.at[k, j], recv.at[k, j],
                                                        device_id=(x, y, 1 - c), device_id_type=MESH))
                arrivals.append(pltpu.make_async_remote_copy(theirs, theirs, send.at[k, j], recv.at[k, j],
                                                             device_id=(x, y, 1 - c), device_id_type=MESH))
        for cp in cps:
            cp.start()
        for cp, arrival in zip(cps, arrivals):
            cp.wait_send()
            arrival.wait_recv()

    return pl.pallas_call(
        body, name=name, in_specs=[ANY] * na, out_specs=[ANY] * na,
        out_shape=[jax.ShapeDtypeStruct(a.shape, a.dtype) for a in arrs],
        input_output_aliases={k: k for k in range(na)},
        scratch_shapes=[pltpu.SemaphoreType.DMA((na, 3))] * 2,
    )(*arrs)


def _swap_sibling(arrs, name):
    na = len(arrs)

    def body(*refs):
        ins, outs, send, recv = refs[:na], refs[na:2 * na], refs[2 * na], refs[2 * na + 1]
        x, y, c, _ = _place()
        cps = []
        for k in range(na):
            h = ins[k].shape[1] // 2
            cps.append(pltpu.make_async_remote_copy(ins[k].at[:, pl.ds(pl.multiple_of((1 - c) * h, 16), h), :], outs[k],
                                                    send.at[k], recv.at[k], device_id=(x, y, 1 - c), device_id_type=MESH))
        for cp in cps:
            cp.start()
        for cp in cps:
            cp.wait()

    return pl.pallas_call(
        body, name=name, in_specs=[ANY] * na, out_specs=[ANY] * na,
        out_shape=[jax.ShapeDtypeStruct((a.shape[0], a.shape[1] // 2, a.shape[2]), a.dtype) for a in arrs],
        scratch_shapes=[pltpu.SemaphoreType.DMA((na,))] * 2,
    )(*arrs)


def _scatter_shapes(sums, kinds):
    return [jax.ShapeDtypeStruct((3, a.shape[1], a.shape[2] if kind == "row" else a.shape[2] // 4), a.dtype)
            for a, kind in zip(sums, kinds)]


def _scatter_copies(srcs, dsts, kinds, send, recv):
    x, y, c, chips = _place()
    cps = []
    for k, (src, dst) in enumerate(zip(srcs, dsts)):
        for j, (px, py) in enumerate(chips):
            chip = 2 * px + py
            if kinds[k] == "row":
                part = src.at[chip]
            else:
                width = src.shape[2] // 4
                part = src.at[0, :, pl.ds(pl.multiple_of(chip * width, 128), width)]
            cps.append(pltpu.make_async_remote_copy(part, dst.at[j], send.at[k, j], recv.at[k, j],
                                                    device_id=(px, py, c), device_id_type=MESH))
    return cps


def _scatter_chips(sums, kinds, name):
    na = len(sums)

    def body(*refs):
        cps = _scatter_copies(refs[:na], refs[na:2 * na], kinds, refs[2 * na], refs[2 * na + 1])
        for cp in cps:
            cp.start()
        for cp in cps:
            cp.wait()

    return pl.pallas_call(
        body, name=name, in_specs=[ANY] * na, out_specs=[ANY] * na, out_shape=_scatter_shapes(sums, kinds),
        scratch_shapes=[pltpu.SemaphoreType.DMA((na, 3))] * 2,
    )(*sums)


def _join_halves(arrs, name):
    na = len(arrs)

    def body(*refs):
        outs, send, recv = refs[na:2 * na], refs[2 * na], refs[2 * na + 1]
        x, y, c, _ = _place()
        halves = [a.shape[1] // 2 for a in arrs]
        mine = [outs[k].at[:, pl.ds(pl.multiple_of(c * h, 8), h), :] for k, h in enumerate(halves)]
        theirs = [outs[k].at[:, pl.ds(pl.multiple_of((1 - c) * h, 8), h), :] for k, h in enumerate(halves)]
        cps = [pltpu.make_async_remote_copy(mine[k], mine[k], send.at[k], recv.at[k],
                                            device_id=(x, y, 1 - c), device_id_type=MESH) for k in range(na)]
        for cp in cps:
            cp.start()
        for k, cp in enumerate(cps):
            cp.wait_send()
            pltpu.make_async_remote_copy(theirs[k], theirs[k], send.at[k], recv.at[k],
                                         device_id=(x, y, 1 - c), device_id_type=MESH).wait_recv()

    return pl.pallas_call(
        body, name=name, in_specs=[ANY] * na, out_specs=[ANY] * na,
        out_shape=[jax.ShapeDtypeStruct(a.shape, a.dtype) for a in arrs],
        input_output_aliases={k: k for k in range(na)},
        scratch_shapes=[pltpu.SemaphoreType.DMA((na,))] * 2,
    )(*arrs)


def _gather_all(a, name):
    def body(a_ref, o_ref, send, recv, local):
        x, y, c, _ = _place()
        me = 4 * x + 2 * y + c
        mine = pltpu.make_async_copy(a_ref, o_ref.at[me], local)
        mine.start()
        cps = []
        for j in range(1, 8):
            peer = (x ^ (j >> 2), y ^ ((j >> 1) & 1), c ^ (j & 1))
            cps.append(pltpu.make_async_remote_copy(a_ref, o_ref.at[me], send.at[j - 1], recv.at[j - 1],
                                                    device_id=peer, device_id_type=MESH))
        for cp in cps:
            cp.start()
        for cp in cps:
            cp.wait()
        mine.wait()

    return pl.pallas_call(
        body, name=name, in_specs=[ANY], out_specs=ANY,
        out_shape=jax.ShapeDtypeStruct((8,) + a.shape, a.dtype),
        scratch_shapes=[pltpu.SemaphoreType.DMA((7,)), pltpu.SemaphoreType.DMA((7,)), pltpu.SemaphoreType.DMA],
    )(a)


def _pack(parts):
    flat = jnp.concatenate([p.reshape(-1) for p in parts])
    n = flat.shape[0]
    rows = -(-n // (PACK_ROWS * 128)) * PACK_ROWS
    return jnp.pad(flat, (0, rows * 128 - n)).reshape(rows, 128)


def _unpack(buf, like):
    flat, out, off = buf.reshape(-1), [], 0
    for p in like:
        out.append(flat[off:off + p.size].reshape(p.shape))
        off += p.size
    return out


def kernel(x, positions, norm_mix_pre, w_in, dn_conv_w, dn_a_log, dn_dt_bias, dn_norm_w, pool_w, pool_scale, swa_sinks, w_out, norm_mix_post, norm_ffn_pre, ffn_w_up, ffn_conv_w, ffn_conv_b, ffn_w_down, norm_ffn_post, loss_target, m_norm_mix_pre, m_w_in, m_dn_conv_w, m_dn_a_log, m_dn_dt_bias, m_dn_norm_w, m_pool_w, m_pool_scale, m_swa_sinks, m_w_out, m_norm_mix_post, m_norm_ffn_pre, m_ffn_w_up, m_ffn_conv_w, m_ffn_conv_b, m_ffn_w_down, m_norm_ffn_post, v_norm_mix_pre, v_w_in, v_dn_conv_w, v_dn_a_log, v_dn_dt_bias, v_dn_norm_w, v_pool_w, v_pool_scale, v_swa_sinks, v_w_out, v_norm_mix_post, v_norm_ffn_pre, v_ffn_w_up, v_ffn_conv_w, v_ffn_conv_b, v_ffn_w_down, v_norm_ffn_post):
    wts = dict(zip(WEIGHTS, (norm_mix_pre, w_in, dn_conv_w, dn_a_log, dn_dt_bias, dn_norm_w, pool_w, pool_scale, swa_sinks,
                             w_out, norm_mix_post, norm_ffn_pre, ffn_w_up, ffn_conv_w, ffn_conv_b, ffn_w_down, norm_ffn_post)))
    mom = dict(zip(WEIGHTS, (m_norm_mix_pre, m_w_in, m_dn_conv_w, m_dn_a_log, m_dn_dt_bias, m_dn_norm_w, m_pool_w, m_pool_scale,
                             m_swa_sinks, m_w_out, m_norm_mix_post, m_norm_ffn_pre, m_ffn_w_up, m_ffn_conv_w, m_ffn_conv_b,
                             m_ffn_w_down, m_norm_ffn_post)))
    var = dict(zip(WEIGHTS, (v_norm_mix_pre, v_w_in, v_dn_conv_w, v_dn_a_log, v_dn_dt_bias, v_dn_norm_w, v_pool_w, v_pool_scale,
                             v_swa_sinks, v_w_out, v_norm_mix_post, v_norm_ffn_pre, v_ffn_w_up, v_ffn_conv_w, v_ffn_conv_b,
                             v_ffn_w_down, v_norm_ffn_post)))
    c = lax.axis_index("c")
    chip = 2 * lax.axis_index("x") + lax.axis_index("y")
    place = jnp.stack([chip, c]).astype(jnp.int32)
    shards = dict(wts, w_in=jnp.pad(w_in, ((0, 0), (0, 0), (0, IN_SHARD_PAD - IN_SHARD))))
    mats = _WeightGather(shards, place)
    w = dict(wts)
    conv_like = [wts[k] for k in CONV]
    conv_all = _gather_all(_pack(conv_like), "gather_conv")
    for i, k in enumerate(CONV):
        w[k] = jnp.concatenate([_unpack(conv_all[2 * j], conv_like)[i] for j in range(4)], axis=2)

    reduce = _GradReduce(place, {k: shards[k].shape for k in BIG})
    loss, dx, grads = _local_step(x[0], positions[0], loss_target[0], w, mats, reduce)
    loss = lax.psum(loss[0, 0], ("x", "y", "c"))
    g_big = reduce.finish()
    g_big["w_in"] = g_big["w_in"][..., :IN_SHARD]

    small_like = [wts[k] for k in SMALL]
    full_like = small_like + [w[k] for k in CONV]
    g_buf = _sum_slots(_gather_all(_pack([jnp.stack(grads[k]) for k in SMALL + CONV]), "gather_small"), "sum_small")
    g_small = dict(zip(SMALL + CONV, _unpack(g_buf, full_like)))
    for k in CONV:
        width = wts[k].shape[2]
        g_small[k] = lax.dynamic_slice_in_dim(g_small[k], chip * width, width, 2)
    pk = lambda d: _pack([d[k] for k in SMALL + CONV])
    upd = _adamw(pk(wts), pk(g_small), pk(mom), pk(var), "adam_small")
    upd_small = [dict(zip(SMALL + CONV, _unpack(b, small_like + conv_like))) for b in upd]

    g_all, d_all, m_all, v_all = {}, {}, {}, {}
    for k in WEIGHTS:
        if k in BIG:
            g_all[k] = g_big[k]
            d_all[k], m_all[k], v_all[k] = _adamw(wts[k], g_big[k], mom[k], var[k], "adam_" + k)
        else:
            g_all[k], d_all[k], m_all[k], v_all[k] = g_small[k], upd_small[0][k], upd_small[1][k], upd_small[2][k]
    return (loss, dx[None], *[g_all[k] for k in WEIGHTS], *[d_all[k] for k in WEIGHTS],
            *[m_all[k] for k in WEIGHTS], *[v_all[k] for k in WEIGHTS])
```

```python
import functools
import math

import jax
import jax.numpy as jnp
from jax import lax
from jax.experimental import pallas as pl
from jax.experimental.pallas import tpu as pltpu

F32 = jnp.float32
BF16 = jnp.bfloat16

HEAD_DIM = 128
DN_HEADS = 6
DN_CONV = 4
DN_CHUNK = 64
POOL_GROUPS = 4
SWA_HEADS = 6
SWA_KV_HEADS = 2
SWA_GROUP = SWA_HEADS // SWA_KV_HEADS
SWA_BLOCK = 128
ROPE_THETA = 10000.0
FFN_CONV = 3
NORM_EPS = 1e-6
DN_W = DN_HEADS * HEAD_DIM
POOL_W = POOL_GROUPS * HEAD_DIM
SWA_W = SWA_HEADS * HEAD_DIM
SWA_KV_W = SWA_KV_HEADS * HEAD_DIM
MIX_W = DN_W + POOL_W + SWA_W
IN_TRUE = 3 * DN_W + DN_W + 2 * DN_HEADS + POOL_W + SWA_W + 2 * SWA_KV_W
GATE_END = 4 * DN_W + 2 * DN_HEADS
CB_Z = 18
CB_BD = 24
CB_POOL = 25
CB_SQ = 29
CB_SK = 35
CB_SV = 37
IN_PAD = 40 * 128
ADAM_LR, ADAM_B1, ADAM_B2, ADAM_EPS, ADAM_WD, ADAM_STEP = 0.001, 0.9, 0.999, 1e-08, 0.01, 10

VMEM_LIMIT = 48 * 1024 * 1024
PACK_ROWS = 512
MM_TK_MAX = 2816
HIGH = lax.Precision.HIGHEST


def _cp(sem):
    return pltpu.CompilerParams(dimension_semantics=sem, vmem_limit_bytes=VMEM_LIMIT)


def _tile(n, prefs):
    for p in prefs:
        if n % p == 0:
            return p
    return n


def _rows(t):
    return _tile(t, (256, 128))


_DN = {"nn": (((1,), (0,)), ((), ())), "nt": (((1,), (1,)), ((), ())), "tn": (((0,), (0,)), ((), ()))}


def _mm_operand(arr, pick, block, idx):
    if pick is None:
        return pl.BlockSpec(block, idx)
    if pick == "split":
        per = arr.shape[2] // block[1]

        def split_idx(i, j, kk):
            r, c = idx(i, j, kk)
            return lax.div(c, per), r, lax.rem(c, per)

        return pl.BlockSpec((None,) + block, split_idx)
    slab = pick[1]
    return pl.BlockSpec((None,) + block, lambda i, j, kk: (slab,) + idx(i, j, kk))


def _mm(a, b, mode, out_dtype, name, a_pick=None, b_pick=None, carry=None, scatter=None):
    def dims(arr, pick):
        r, c = arr.shape[-2:]
        return (r, c * arr.shape[0]) if pick == "split" else (r, c)

    (a0, a1), (b0, b1) = dims(a, a_pick), dims(b, b_pick)
    k, m = (a0, a1) if mode == "tn" else (a1, a0)
    n = b0 if mode == "nt" else b1
    lim = lambda arr, pick, is_last, full: arr.shape[2] if (pick == "split" and is_last) else full
    tm = _tile(lim(a, a_pick, mode == "tn", m), (1024, 512, 256, 128))
    tn = _tile(lim(b, b_pick, mode != "nt", n), (1408, 1280, 1024, 512, 256, 128))
    k_lim = min(lim(a, a_pick, mode != "tn", k), lim(b, b_pick, mode == "nt", k))
    tk = max([d for d in range(128, min(k_lim, MM_TK_MAX) + 1, 128) if k_lim % d == 0], default=k_lim)
    nk = k // tk

    grid = (m // tm, n // tn, nk)

    riding = carry is not None or scatter is not None
    ns = len(scatter[0]) if scatter is not None else 0

    def body(a_ref, b_ref, *rest):
        if carry is not None:
            _, o_ref, w_ref, *scratch = rest
            copies = _gather_copies(w_ref, carry[1], *scratch[-2:])
        elif scatter is not None:
            o_ref, scratch = rest[ns], rest[2 * ns + 1:]
            copies = _scatter_copies(rest[:ns], rest[ns + 1:2 * ns + 1], scatter[1], *scratch[-2:])
        else:
            o_ref, *scratch = rest
        if riding:
            scratch = scratch[:-2]
            step = (pl.program_id(0) * grid[1] + pl.program_id(1)) * grid[2] + pl.program_id(2)

            @pl.when(step == 0)
            def _():
                for cp in copies:
                    cp.start()
        part = lax.dot_general(a_ref[...], b_ref[...], _DN[mode], preferred_element_type=F32)
        if nk == 1:
            o_ref[...] = part.astype(o_ref.dtype)
        else:
            acc_ref, = scratch
            kk = pl.program_id(2)

            @pl.when(kk == 0)
            def _():
                acc_ref[...] = part

            @pl.when(kk > 0)
            def _():
                acc_ref[...] += part

            @pl.when(kk == nk - 1)
            def _():
                o_ref[...] = acc_ref[...].astype(o_ref.dtype)
        if riding:
            @pl.when(step == grid[0] * grid[1] * grid[2] - 1)
            def _():
                for cp in copies:
                    cp.wait()

    if mode == "tn":
        a_spec = _mm_operand(a, a_pick, (tk, tm), lambda i, j, kk: (kk, i))
    else:
        a_spec = _mm_operand(a, a_pick, (tm, tk), lambda i, j, kk: (i, kk))
    if mode == "nt":
        b_spec = _mm_operand(b, b_pick, (tn, tk), lambda i, j, kk: (j, kk))
    else:
        b_spec = _mm_operand(b, b_pick, (tk, tn), lambda i, j, kk: (kk, j))
    scratch = [pltpu.VMEM((tm, tn), F32)] if nk > 1 else []
    out_spec = pl.BlockSpec((tm, tn), lambda i, j, kk: (i, j))
    out_shape = jax.ShapeDtypeStruct((m, n), out_dtype)
    if not riding:
        return pl.pallas_call(
            body, name=name, grid=grid, in_specs=[a_spec, b_spec], out_specs=out_spec, out_shape=out_shape,
            scratch_shapes=scratch, compiler_params=_cp(("parallel", "parallel", "arbitrary")),
        )(a, b)
    any_space = pl.BlockSpec(memory_space=pl.ANY)
    in_order = _cp(("arbitrary", "arbitrary", "arbitrary"))
    if carry is not None:
        return pl.pallas_call(
            body, name=name, grid=grid, in_specs=[a_spec, b_spec, any_space], out_specs=[out_spec, any_space],
            out_shape=[out_shape, jax.ShapeDtypeStruct(carry[0].shape, carry[0].dtype)], input_output_aliases={2: 1},
            scratch_shapes=scratch + [pltpu.SemaphoreType.DMA((3,))] * 2, compiler_params=in_order,
        )(a, b, carry[0])
    return pl.pallas_call(
        body, name=name, grid=grid, in_specs=[a_spec, b_spec] + [any_space] * ns, out_specs=[out_spec] + [any_space] * ns,
        out_shape=[out_shape] + _scatter_shapes(*scatter),
        scratch_shapes=scratch + [pltpu.SemaphoreType.DMA((ns, 3))] * 2, compiler_params=in_order,
    )(a, b, *scatter[0])


def _rms(x, w):
    return x * lax.rsqrt(jnp.mean(x * x, axis=-1, keepdims=True) + NORM_EPS) * w


def _norm_fwd(x, w, name):
    t, d = x.shape
    r = _rows(t)

    def body(x_ref, w_ref, h_ref):
        h_ref[...] = _rms(x_ref[...], w_ref[...]).astype(h_ref.dtype)

    return pl.pallas_call(
        body, name=name, grid=(t // r,),
        in_specs=[pl.BlockSpec((r, d), lambda i: (i, 0)), pl.BlockSpec((1, d), lambda i: (0, 0))],
        out_specs=pl.BlockSpec((r, d), lambda i: (i, 0)),
        out_shape=jax.ShapeDtypeStruct((t, d), BF16), compiler_params=_cp(("parallel",)),
    )(x, w)


def _resnorm_fwd(x, y, w, name):
    t, d = x.shape
    r = _rows(t)

    def body(x_ref, y_ref, w_ref, o_ref):
        o_ref[...] = x_ref[...] + _rms(y_ref[...], w_ref[...])

    return pl.pallas_call(
        body, name=name, grid=(t // r,),
        in_specs=[pl.BlockSpec((r, d), lambda i: (i, 0)), pl.BlockSpec((r, d), lambda i: (i, 0)),
                  pl.BlockSpec((1, d), lambda i: (0, 0))],
        out_specs=pl.BlockSpec((r, d), lambda i: (i, 0)),
        out_shape=jax.ShapeDtypeStruct((t, d), F32), compiler_params=_cp(("parallel",)),
    )(x, y, w)


def _norm_bwd(x, w, dh, add, out_dtype, name):
    t, d = x.shape
    r = _rows(t)
    has_add = add is not None

    def body(*refs):
        if has_add:
            x_ref, w_ref, dh_ref, add_ref, dx_ref, dw_ref = refs
        else:
            x_ref, w_ref, dh_ref, dx_ref, dw_ref = refs
        xv = x_ref[...]
        g = dh_ref[...].astype(F32)
        rs = lax.rsqrt(jnp.mean(xv * xv, axis=-1, keepdims=True) + NORM_EPS)
        xh = xv * rs
        gw = g * w_ref[...]
        dx = rs * (gw - xh * jnp.mean(gw * xh, axis=-1, keepdims=True))
        if has_add:
            dx = dx + add_ref[...]
        dx_ref[...] = dx.astype(dx_ref.dtype)

        @pl.when(pl.program_id(0) == 0)
        def _():
            dw_ref[...] = jnp.zeros_like(dw_ref)

        dw_ref[...] += jnp.sum(g * xh, axis=0, keepdims=True)

    row = pl.BlockSpec((r, d), lambda i: (i, 0))
    vec = pl.BlockSpec((1, d), lambda i: (0, 0))
    ins = [x, w, dh] + ([add] if has_add else [])
    return pl.pallas_call(
        body, name=name, grid=(t // r,),
        in_specs=[row, vec, row] + ([row] if has_add else []),
        out_specs=[row, vec],
        out_shape=[jax.ShapeDtypeStruct((t, d), out_dtype), jax.ShapeDtypeStruct((1, d), F32)],
        compiler_params=_cp(("arbitrary",)),
    )(*ins)


def _loss_head(y, target, name):
    t, d = y.shape
    r = _rows(t)

    def body(y_ref, t_ref, l_ref, g_ref):
        e = y_ref[...] - t_ref[...]
        g_ref[...] = e * (1.0 / d)

        @pl.when(pl.program_id(0) == 0)
        def _():
            l_ref[...] = jnp.zeros_like(l_ref)

        l_ref[...] += jnp.sum(e * e) * (0.5 / d)

    row = pl.BlockSpec((r, d), lambda i: (i, 0))
    return pl.pallas_call(
        body, name=name, grid=(t // r,), in_specs=[row, row],
        out_specs=[pl.BlockSpec((1, 128), lambda i: (0, 0)), row],
        out_shape=[jax.ShapeDtypeStruct((1, 128), F32), jax.ShapeDtypeStruct((t, d), F32)],
        compiler_params=_cp(("arbitrary",)),
    )(y, target)


def _down(x, s):
    return x if s == 0 else pltpu.roll(x, s, 0)


def _up(x, s):
    return x if s == 0 else pltpu.roll(x, x.shape[0] - s, 0)


def _halo(t, r, hh, tc, col):
    q = r // hh
    last = t // hh - 1
    tile = pl.BlockSpec((r, tc), lambda j, i: (i, col(j)))
    prev = pl.BlockSpec((hh, tc), lambda j, i: (jnp.maximum(i * q - 1, 0), col(j)))
    nxt = pl.BlockSpec((hh, tc), lambda j, i: (jnp.minimum((i + 1) * q, last), col(j)))
    return tile, prev, nxt


def _sig(x):
    return 1.0 / (1.0 + jnp.exp(-x))


def _dsilu(x, s):
    return s * (1.0 + x * (1.0 - s))


def _dn_pre_fwd(p, conv_w, name):
    t = p.shape[0]
    r = _rows(t)

    def body(x_ref, xp_ref, w_ref, o_ref):
        j, i = pl.program_id(0), pl.program_id(1)
        xe = jnp.concatenate([jnp.where(i == 0, 0.0, xp_ref[...]), x_ref[...]], axis=0)
        c = sum(_down(xe, DN_CONV - 1 - k) * w_ref[pl.ds(k, 1), :] for k in range(DN_CONV))[8:]
        a = c * _sig(c)
        for h in range(DN_HEADS):
            ah = a[:, h * 128:(h + 1) * 128]
            fac = lax.rsqrt(jnp.sum(ah * ah, axis=-1, keepdims=True) + NORM_EPS)
            o_ref[:, h * 128:(h + 1) * 128] = ah * jnp.where(j == 0, fac * HEAD_DIM ** -0.5, jnp.where(j == 1, fac, 1.0))

    tile, prev, _ = _halo(t, r, 8, DN_W, lambda j: j)
    return pl.pallas_call(
        body, name=name, grid=(3, t // r),
        in_specs=[tile, prev, pl.BlockSpec((DN_CONV, DN_W), lambda j, i: (0, j))],
        out_specs=tile, out_shape=jax.ShapeDtypeStruct((t, 3 * DN_W), F32),
        compiler_params=_cp(("parallel", "parallel")),
    )(p, p, conv_w)


def _dn_pre_bwd(p, conv_w, dqkv, name):
    t = p.shape[0]
    r = _rows(t)
    ni = t // r

    def body(x_ref, xp_ref, xn_ref, w_ref, d_ref, dn_ref, dx_ref, dw_ref):
        j, i = pl.program_id(0), pl.program_id(1)
        xe = jnp.concatenate([jnp.where(i == 0, 0.0, xp_ref[...]), x_ref[...], xn_ref[...]], axis=0)
        de = jnp.concatenate([jnp.zeros((8, DN_W), F32), d_ref[...], jnp.where(i == ni - 1, 0.0, dn_ref[...])], axis=0)
        xs = [_down(xe, DN_CONV - 1 - k) for k in range(DN_CONV)]
        c = sum(xs[k] * w_ref[pl.ds(k, 1), :] for k in range(DN_CONV))
        s = _sig(c)
        a = c * s
        das = []
        for h in range(DN_HEADS):
            ah, dh = a[:, h * 128:(h + 1) * 128], de[:, h * 128:(h + 1) * 128]
            fac = lax.rsqrt(jnp.sum(ah * ah, axis=-1, keepdims=True) + NORM_EPS)
            dnorm = fac * dh - ah * (fac * fac * fac) * jnp.sum(dh * ah, axis=-1, keepdims=True)
            das.append(jnp.where(j == 0, dnorm * HEAD_DIM ** -0.5, jnp.where(j == 1, dnorm, dh)))
        dc = jnp.concatenate(das, axis=1) * _dsilu(c, s)
        dx_ref[...] = sum(_up(dc, DN_CONV - 1 - k) * w_ref[pl.ds(k, 1), :] for k in range(DN_CONV))[8:8 + r]

        @pl.when(i == 0)
        def _():
            dw_ref[...] = jnp.zeros_like(dw_ref)

        for k in range(DN_CONV):
            dw_ref[pl.ds(k, 1), :] += jnp.sum((dc * xs[k])[8:8 + r], axis=0, keepdims=True)

    tile, prev, nxt = _halo(t, r, 8, DN_W, lambda j: j)
    wspec = pl.BlockSpec((DN_CONV, DN_W), lambda j, i: (0, j))
    return pl.pallas_call(
        body, name=name, grid=(3, ni),
        in_specs=[tile, prev, nxt, wspec, tile, nxt],
        out_specs=[tile, wspec],
        out_shape=[jax.ShapeDtypeStruct((t, 3 * DN_W), F32), jax.ShapeDtypeStruct((DN_CONV, 3 * DN_W), F32)],
        compiler_params=_cp(("parallel", "arbitrary")),
    )(p, p, p, conv_w, dqkv, dqkv)


def _ffn_act_fwd(up, cw, cb, name):
    t, f2 = up.shape
    f = f2 // 2
    r = _tile(t, (512, 256, 128))
    tc = _tile(f, (512, 256, 128))
    nj = f // tc

    def body(a_ref, ap_ref, b_ref, bp_ref, wa_ref, wb_ref, ca_ref, cb_ref, o_ref):
        i = pl.program_id(1)

        def conv(x_ref, xp_ref, w_ref, c_ref):
            xe = jnp.concatenate([jnp.where(i == 0, 0.0, xp_ref[...]), x_ref[...]], axis=0)
            return sum(_down(xe, FFN_CONV - 1 - k) * w_ref[pl.ds(k, 1), :] for k in range(FFN_CONV))[8:] + c_ref[...]

        ua = conv(a_ref, ap_ref, wa_ref, ca_ref)
        ub = conv(b_ref, bp_ref, wb_ref, cb_ref)
        o_ref[...] = (ua * _sig(ua) * ub).astype(o_ref.dtype)

    ta, pa, _ = _halo(t, r, 8, tc, lambda j: j)
    tb, pb, _ = _halo(t, r, 8, tc, lambda j: j + nj)
    wa = pl.BlockSpec((FFN_CONV, tc), lambda j, i: (0, j))
    wb = pl.BlockSpec((FFN_CONV, tc), lambda j, i: (0, j + nj))
    ca = pl.BlockSpec((1, tc), lambda j, i: (0, j))
    cbs = pl.BlockSpec((1, tc), lambda j, i: (0, j + nj))
    return pl.pallas_call(
        body, name=name, grid=(nj, t // r),
        in_specs=[ta, pa, tb, pb, wa, wb, ca, cbs], out_specs=ta,
        out_shape=jax.ShapeDtypeStruct((t, f), BF16), compiler_params=_cp(("parallel", "parallel")),
    )(up, up, up, up, cw, cw, cb, cb)


def _ffn_act_bwd(up, cw, cb, dact, name):
    t, f2 = up.shape
    f = f2 // 2
    r = _tile(t, (512, 256, 128))
    ni = t // r
    tc = _tile(f, (512, 256, 128))
    nj = f // tc

    def body(a_ref, ap_ref, an_ref, b_ref, bp_ref, bn_ref, wa_ref, wb_ref, ca_ref, cb_ref, d_ref, dn_ref,
             du_ref, dw_ref, dc_ref):
        i = pl.program_id(1)
        dua_ref, dub_ref, dwa_ref, dwb_ref, dca_ref, dcb_ref = (du_ref.at[0], du_ref.at[1], dw_ref.at[0], dw_ref.at[1],
                                                                  dc_ref.at[0], dc_ref.at[1])

        def ext(x_ref, xp_ref, xn_ref):
            return jnp.concatenate([jnp.where(i == 0, 0.0, xp_ref[...]), x_ref[...], xn_ref[...]], axis=0)

        ae, be = ext(a_ref, ap_ref, an_ref), ext(b_ref, bp_ref, bn_ref)
        as_ = [_down(ae, FFN_CONV - 1 - k) for k in range(FFN_CONV)]
        bs_ = [_down(be, FFN_CONV - 1 - k) for k in range(FFN_CONV)]
        ua = sum(as_[k] * wa_ref[pl.ds(k, 1), :] for k in range(FFN_CONV)) + ca_ref[...]
        ub = sum(bs_[k] * wb_ref[pl.ds(k, 1), :] for k in range(FFN_CONV)) + cb_ref[...]
        de = jnp.concatenate([jnp.zeros((8, tc), F32), d_ref[...].astype(F32),
                              jnp.where(i == ni - 1, 0.0, dn_ref[...].astype(F32))], axis=0)
        s = _sig(ua)
        dua = de * ub * _dsilu(ua, s)
        dub = de * ua * s
        dua_ref[...] = sum(_up(dua, FFN_CONV - 1 - k) * wa_ref[pl.ds(k, 1), :] for k in range(FFN_CONV))[8:8 + r].astype(dua_ref.dtype)
        dub_ref[...] = sum(_up(dub, FFN_CONV - 1 - k) * wb_ref[pl.ds(k, 1), :] for k in range(FFN_CONV))[8:8 + r].astype(dub_ref.dtype)

        @pl.when(i == 0)
        def _():
            dw_ref[...] = jnp.zeros_like(dw_ref)
            dc_ref[...] = jnp.zeros_like(dc_ref)

        for k in range(FFN_CONV):
            dwa_ref[pl.ds(k, 1), :] += jnp.sum((dua * as_[k])[8:8 + r], axis=0, keepdims=True)
            dwb_ref[pl.ds(k, 1), :] += jnp.sum((dub * bs_[k])[8:8 + r], axis=0, keepdims=True)
        dca_ref[...] += jnp.sum(dua[8:8 + r], axis=0, keepdims=True)
        dcb_ref[...] += jnp.sum(dub[8:8 + r], axis=0, keepdims=True)

    ta, pa, na = _halo(t, r, 8, tc, lambda j: j)
    tb, pb, nb = _halo(t, r, 8, tc, lambda j: j + nj)
    wa = pl.BlockSpec((FFN_CONV, tc), lambda j, i: (0, j))
    wb = pl.BlockSpec((FFN_CONV, tc), lambda j, i: (0, j + nj))
    ca = pl.BlockSpec((1, tc), lambda j, i: (0, j))
    cbs = pl.BlockSpec((1, tc), lambda j, i: (0, j + nj))
    return pl.pallas_call(
        body, name=name, grid=(nj, ni),
        in_specs=[ta, pa, na, tb, pb, nb, wa, wb, ca, cbs, ta, na],
        out_specs=[pl.BlockSpec((2, r, tc), lambda j, i: (0, i, j)), pl.BlockSpec((2, FFN_CONV, tc), lambda j, i: (0, 0, j)),
                   pl.BlockSpec((2, 1, tc), lambda j, i: (0, 0, j))],
        out_shape=[jax.ShapeDtypeStruct((2, t, f), BF16), jax.ShapeDtypeStruct((2, FFN_CONV, f), F32),
                   jax.ShapeDtypeStruct((2, 1, f), F32)],
        compiler_params=_cp(("parallel", "arbitrary")),
    )(up, up, up, up, up, up, cw, cw, cb, cb, dact, dact)


def _pool_pick(g, vals):
    return jnp.where(g == 0, vals[0], jnp.where(g == 1, vals[1], jnp.where(g == 2, vals[2], vals[3])))


def _pool_pre(xe, g, t0):
    s1 = xe + _down(xe, 1)
    s2 = s1 + _down(s1, 2)
    s3 = s2 + _down(s2, 4)
    s4 = s3 + _down(s3, 8)
    r = xe.shape[0] - 16
    pos = (t0 + lax.broadcasted_iota(jnp.int32, (r, 1), 0)).astype(F32)
    cnt = jnp.minimum(pos + 1.0, _pool_pick(g, (2.0, 4.0, 8.0, 16.0)))
    return _pool_pick(g, (s1, s2, s3, s4))[16:] / cnt - xe[16:]


def _pool_fwd(p, pool_w, scale, name):
    t = p.shape[0]
    r = _tile(t, (1024, 256, 128))

    def body(x_ref, xp_ref, w_ref, sc_ref, o_ref):
        g, i = pl.program_id(0), pl.program_id(1)
        xe = jnp.concatenate([jnp.where(i == 0, 0.0, xp_ref[...]), x_ref[...]], axis=0)
        pre = _pool_pre(xe, g, i * r)
        o_ref[...] = jnp.dot(pre, w_ref[0], preferred_element_type=F32) * sc_ref[...]

    tile, prev, _ = _halo(t, r, 16, 128, lambda j: CB_POOL + j)
    return pl.pallas_call(
        body, name=name, grid=(POOL_GROUPS, t // r),
        in_specs=[tile, prev, pl.BlockSpec((1, 128, 128), lambda j, i: (j, 0, 0)), pl.BlockSpec((1, 128), lambda j, i: (0, j))],
        out_specs=pl.BlockSpec((r, 128), lambda j, i: (i, j)),
        out_shape=jax.ShapeDtypeStruct((t, POOL_W), F32), compiler_params=_cp(("parallel", "parallel")),
    )(p, p, pool_w, scale)


def _pool_bwd(p, pool_w, scale, dycat, name):
    t = p.shape[0]
    r = _tile(t, (1024, 256, 128))
    ni = t // r

    def body(x_ref, xp_ref, w_ref, sc_ref, d_ref, dn_ref, dx_ref, dw_ref, dsc_ref):
        g, i = pl.program_id(0), pl.program_id(1)
        xe = jnp.concatenate([jnp.where(i == 0, 0.0, xp_ref[...]), x_ref[...]], axis=0)
        pre = _pool_pre(xe, g, i * r)
        w = w_ref[0]
        dy = d_ref[...]
        dye = jnp.concatenate([dy, jnp.where(i == ni - 1, 0.0, dn_ref[...])], axis=0)
        dpre = lax.dot_general(dye * sc_ref[...], w, _DN["nt"], preferred_element_type=F32)
        pos = (i * r + lax.broadcasted_iota(jnp.int32, (r + 16, 1), 0)).astype(F32)
        dm = dpre / jnp.minimum(pos + 1.0, _pool_pick(g, (2.0, 4.0, 8.0, 16.0)))
        a1 = dm + _up(dm, 1)
        a2 = a1 + _up(a1, 2)
        a3 = a2 + _up(a2, 4)
        a4 = a3 + _up(a3, 8)
        dx_ref[...] = (_pool_pick(g, (a1, a2, a3, a4)) - dpre)[:r]

        @pl.when(i == 0)
        def _():
            dw_ref[...] = jnp.zeros_like(dw_ref)
            dsc_ref[...] = jnp.zeros_like(dsc_ref)

        dw_ref[0] += lax.dot_general(pre, dy * sc_ref[...], _DN["tn"], preferred_element_type=F32)
        dsc_ref[...] += jnp.sum(dy * jnp.dot(pre, w, preferred_element_type=F32), axis=0, keepdims=True)

    tile, prev, _ = _halo(t, r, 16, 128, lambda j: CB_POOL + j)
    dtile, _, dnxt = _halo(t, r, 16, 128, lambda j: DN_W // 128 + j)
    wspec = pl.BlockSpec((1, 128, 128), lambda j, i: (j, 0, 0))
    sspec = pl.BlockSpec((1, 128), lambda j, i: (0, j))
    return pl.pallas_call(
        body, name=name, grid=(POOL_GROUPS, ni),
        in_specs=[tile, prev, wspec, sspec, dtile, dnxt],
        out_specs=[pl.BlockSpec((r, 128), lambda j, i: (i, j)), wspec, sspec],
        out_shape=[jax.ShapeDtypeStruct((t, POOL_W), F32), jax.ShapeDtypeStruct((POOL_GROUPS, 128, 128), F32),
                   jax.ShapeDtypeStruct((1, POOL_W), F32)],
        compiler_params=_cp(("parallel", "arbitrary")),
    )(p, p, pool_w, scale, dycat, dycat)


_DNB = {"nn": (((2,), (1,)), ((0,), (0,))), "nt": (((2,), (2,)), ((0,), (0,))), "tn": (((1,), (1,)), ((0,), (0,)))}


def _dot(a, b, mode="nn", precision=None):
    dn = _DNB[mode] if a.ndim == 3 else _DN[mode]
    return lax.dot_general(a, b, dn, precision=precision, preferred_element_type=F32)


@functools.partial(jax.custom_vjp, nondiff_argnums=(2,))
def _bdot(a, b, mode):
    return _dot(a.astype(BF16), b.astype(BF16), mode)


def _bdot_fwd(a, b, mode):
    return _bdot(a, b, mode), (a, b)


def _bdot_bwd(mode, res, g):
    a, b = res
    if mode == "nn":
        return _bdot(g, b, "nt"), _bdot(a, g, "tn")
    if mode == "nt":
        return _bdot(g, b, "nn"), _bdot(g, a, "tn")
    return _bdot(b, g, "nt"), _bdot(a, g, "nn")


_bdot.defvjp(_bdot_fwd, _bdot_bwd)


def _dn_consts():
    c = DN_CHUNK
    ii = lax.broadcasted_iota(jnp.int32, (c, c), 0)
    jj = lax.broadcasted_iota(jnp.int32, (c, c), 1)
    one, zero = jnp.ones((c, c), F32), jnp.zeros((c, c), F32)
    return dict(ltri=jnp.where(ii >= jj, one, zero), utri=jnp.where(ii <= jj, one, zero), ones=one,
                causal=ii >= jj, strict=ii > jj, eye=jnp.where(ii == jj, one, zero))


def _dn_chunk(q, k, v, z, bcol, acol, s_in, alog, dtb, nw, cs):
    c = DN_CHUNK
    hh = q.shape[0]
    per_head = lambda m: jnp.broadcast_to(m, (hh, c, c))
    beta = _sig(bcol)
    xa = acol + dtb
    g = -jnp.exp(alog) * (jnp.maximum(xa, 0.0) + jnp.log(1.0 + jnp.exp(-jnp.abs(xa))))
    gb = jnp.broadcast_to(g, (hh, c, HEAD_DIM))
    gbc = jnp.broadcast_to(g, (hh, c, c))
    gc = _dot(per_head(cs["ltri"]), gb, precision=HIGH)
    gcol = gc[:, :, :c]
    grow = jnp.swapaxes(gcol, 1, 2)
    decay = jnp.exp(jnp.where(cs["causal"], gcol - grow, -1e30))
    kb = k * beta
    vb = v * beta
    nil = -jnp.where(cs["strict"], _bdot(kb, k, "nt") * decay, 0.0)
    inv = cs["eye"] + nil
    powk = nil
    for _ in range(int(math.log2(c)) - 1):
        powk = _bdot(powk, powk, "nn")
        inv = _bdot(inv, cs["eye"] + powk, "nn")
    eg = jnp.exp(gc)
    u = _bdot(inv, vb, "nn")
    w = _bdot(inv, kb * eg, "nn")
    a = _bdot(q, k, "nt") * decay
    v_new = u - _bdot(w, s_in, "nn")
    o = _bdot(q * eg, s_in, "nn") + _bdot(a, v_new, "nn")
    glast = jnp.sum(gb, axis=1, keepdims=True)
    s_out = s_in * jnp.exp(glast) + _bdot(k * jnp.exp(glast - gc), v_new, "tn")
    on = o * lax.rsqrt(jnp.mean(o * o, axis=-1, keepdims=True) + NORM_EPS) * nw
    return on * (z * _sig(z)), s_out


def _lane_pick(x, lane, idx):
    return jnp.sum(jnp.where(lane == idx, x, 0.0), axis=1, keepdims=True)


def _dn_load(q_ref, k_ref, v_ref, z_ref, bd_ref, al_ref, dt_ref, nw_ref, s_in):
    lane = lax.broadcasted_iota(jnp.int32, (1, 128), 1)
    bd, al, dt = bd_ref[...], al_ref[...], dt_ref[...]
    heads = range(DN_HEADS)
    wide = lambda ref: jnp.stack([ref[:, h * 128:(h + 1) * 128] for h in heads], axis=0)
    col = lambda x, off: jnp.stack([_lane_pick(x, lane, off + h) for h in heads], axis=0)
    return (wide(q_ref), wide(k_ref), wide(v_ref), wide(z_ref), col(bd, 0), col(bd, DN_HEADS), s_in,
            col(al, 0), col(dt, 0), nw_ref[...])


def _dn_fwd(qkv, p, alog, dtb, nw, name):
    t = qkv.shape[0]
    c = DN_CHUNK
    n = t // c

    def body(q_ref, k_ref, v_ref, z_ref, bd_ref, al_ref, dt_ref, nw_ref, y_ref, ss_ref, s_scr):
        @pl.when(pl.program_id(0) == 0)
        def _():
            s_scr[...] = jnp.zeros_like(s_scr)

        s_in = s_scr[...]
        y, s_out = _dn_chunk(*_dn_load(q_ref, k_ref, v_ref, z_ref, bd_ref, al_ref, dt_ref, nw_ref, s_in), _dn_consts())
        ss_ref[0] = s_in
        s_scr[...] = s_out
        for h in range(DN_HEADS):
            y_ref[:, h * 128:(h + 1) * 128] = y[h]

    wide = lambda j: pl.BlockSpec((c, DN_W), lambda i: (i, j))
    vec = pl.BlockSpec((1, 128), lambda i: (0, 0))
    return pl.pallas_call(
        body, name=name, grid=(n,),
        in_specs=[wide(0), wide(1), wide(2), wide(3), pl.BlockSpec((c, 128), lambda i: (i, CB_BD)), vec, vec, vec],
        out_specs=[wide(0), pl.BlockSpec((1, DN_HEADS, 128, 128), lambda i: (i, 0, 0, 0))],
        out_shape=[jax.ShapeDtypeStruct((t, DN_W), F32), jax.ShapeDtypeStruct((n, DN_HEADS, 128, 128), F32)],
        scratch_shapes=[pltpu.VMEM((DN_HEADS, 128, 128), F32)],
        compiler_params=_cp(("arbitrary",)),
    )(qkv, qkv, qkv, p, p, alog, dtb, nw)


def _dn_bwd(qkv, p, alog, dtb, nw, states, dycat, name, carry=None):
    t = qkv.shape[0]
    c = DN_CHUNK
    n = t // c
    sums, kinds = carry if carry is not None else ((), ())
    na = len(sums)

    def body(*refs):
        q_ref, k_ref, v_ref, z_ref, bd_ref, al_ref, dt_ref, nw_ref, ss_ref, dy_ref = refs[:10]
        dqkv_ref, dz_ref, dbd_ref, dal_ref, ddt_ref, dnw_ref = refs[10 + na:16 + na]
        ds_scr = refs[16 + 2 * na]
        if na:
            copies = _scatter_copies(refs[10:10 + na], refs[16 + na:16 + 2 * na], kinds, *refs[17 + 2 * na:])

        @pl.when(pl.program_id(0) == 0)
        def _():
            ds_scr[...] = jnp.zeros_like(ds_scr)
            dal_ref[...] = jnp.zeros_like(dal_ref)
            ddt_ref[...] = jnp.zeros_like(ddt_ref)
            dnw_ref[...] = jnp.zeros_like(dnw_ref)
            if na:
                for cp in copies:
                    cp.start()

        lane = lax.broadcasted_iota(jnp.int32, (1, 128), 1)
        args = _dn_load(q_ref, k_ref, v_ref, z_ref, bd_ref, al_ref, dt_ref, nw_ref, ss_ref[0])
        dy = jnp.stack([dy_ref[:, h * 128:(h + 1) * 128] for h in range(DN_HEADS)], axis=0)
        _, vjp = jax.vjp(functools.partial(_dn_chunk, cs=_dn_consts()), *args)
        gq, gk, gv, gz, gb, ga, gs, gal, gdt, gnw = vjp((dy, ds_scr[...]))
        ds_scr[...] = gs
        dbd = jnp.zeros((c, 128), F32)
        dal = jnp.zeros((1, 128), F32)
        ddt = jnp.zeros((1, 128), F32)
        for h in range(DN_HEADS):
            sl = slice(h * 128, (h + 1) * 128)
            dqkv_ref[:, sl] = gq[h]
            dqkv_ref[:, DN_W + h * 128:DN_W + (h + 1) * 128] = gk[h]
            dqkv_ref[:, 2 * DN_W + h * 128:2 * DN_W + (h + 1) * 128] = gv[h]
            dz_ref[:, sl] = gz[h]
            dbd = dbd + jnp.where(lane == h, gb[h], 0.0) + jnp.where(lane == DN_HEADS + h, ga[h], 0.0)
            dal = dal + jnp.where(lane == h, gal[h], 0.0)
            ddt = ddt + jnp.where(lane == h, gdt[h], 0.0)
        dbd_ref[...] = dbd
        dal_ref[...] += dal
        ddt_ref[...] += ddt
        dnw_ref[...] += gnw

        if na:
            @pl.when(pl.program_id(0) == n - 1)
            def _():
                for cp in copies:
                    cp.wait()

    rev = lambda i: n - 1 - i
    wide = lambda j: pl.BlockSpec((c, DN_W), lambda i: (rev(i), j))
    vec = pl.BlockSpec((1, 128), lambda i: (0, 0))
    any_space = pl.BlockSpec(memory_space=pl.ANY)
    return pl.pallas_call(
        body, name=name, grid=(n,),
        in_specs=[wide(0), wide(1), wide(2), wide(3), pl.BlockSpec((c, 128), lambda i: (rev(i), CB_BD)), vec, vec, vec,
                  pl.BlockSpec((1, DN_HEADS, 128, 128), lambda i: (rev(i), 0, 0, 0)), wide(0)] + [any_space] * na,
        out_specs=[pl.BlockSpec((c, 3 * DN_W), lambda i: (rev(i), 0)), wide(0),
                   pl.BlockSpec((c, 128), lambda i: (rev(i), 0)), vec, vec, vec] + [any_space] * na,
        out_shape=[jax.ShapeDtypeStruct((t, 3 * DN_W), F32), jax.ShapeDtypeStruct((t, DN_W), F32),
                   jax.ShapeDtypeStruct((t, 128), F32), jax.ShapeDtypeStruct((1, 128), F32),
                   jax.ShapeDtypeStruct((1, 128), F32), jax.ShapeDtypeStruct((1, 128), F32)] + _scatter_shapes(sums, kinds),
        scratch_shapes=[pltpu.VMEM((DN_HEADS, 128, 128), F32)] + ([pltpu.SemaphoreType.DMA((na, 3))] * 2 if na else []),
        compiler_params=_cp(("arbitrary",)),
    )(qkv, qkv, qkv, p, p, alog, dtb, nw, states, dycat, *sums)


def _rope(x, cosf, sins):
    return x * cosf + pltpu.roll(x, HEAD_DIM // 2, 1) * sins


def _rope_t(d, cosf, sins):
    return d * cosf + pltpu.roll(d * sins, HEAD_DIM // 2, 1)


def _swa_masks():
    b = SWA_BLOCK
    i = lax.broadcasted_iota(jnp.int32, (SWA_GROUP * b, b), 0) & (b - 1)
    j = lax.broadcasted_iota(jnp.int32, (SWA_GROUP * b, b), 1)
    return j > i, j <= i


def _swa_sink_col(sinks_ref, h):
    b = SWA_BLOCK
    r = lax.broadcasted_iota(jnp.int32, (SWA_GROUP * b, 1), 0)
    s = [sinks_ref[h * SWA_GROUP + g] for g in range(SWA_GROUP)]
    return jnp.where(r < b, s[0], jnp.where(r < 2 * b, s[1], s[2]))


def _swa_specs(t, h_first):
    nb = t // SWA_BLOCK

    def at(col, off):
        def imap(h, n):
            return (jnp.clip(n + off, 0, nb - 1), col(h))
        return pl.BlockSpec((SWA_BLOCK, 128), imap)
    return at


def _swa_fwd(p, cosf, sins, sinks, name):
    t = p.shape[0]
    b = SWA_BLOCK
    nb = t // b
    at = _swa_specs(t, None)
    scale = HEAD_DIM ** -0.5

    def body(q0, q1, q2, kp, kc, vp, vc, cc, sc, cp, sp, sinks_ref, o_ref, lse_ref):
        h, n = pl.program_id(0), pl.program_id(1)
        qs = jnp.concatenate([_rope(q[...], cc[...], sc[...]) for q in (q0, q1, q2)], axis=0)
        ks = jnp.concatenate([_rope(kp[...], cp[...], sp[...]), _rope(kc[...], cc[...], sc[...])], axis=0)
        vs = jnp.concatenate([vp[...], vc[...]], axis=0)
        mp, mc = _swa_masks()
        mask = jnp.concatenate([mp & (n > 0), mc], axis=1)
        s = jnp.where(mask, _dot(qs, ks, "nt") * scale, -1e30)
        sink = _swa_sink_col(sinks_ref, h)
        m = jnp.maximum(jnp.max(s, axis=1, keepdims=True), sink)
        e = jnp.exp(s - m)
        l = jnp.sum(e, axis=1, keepdims=True) + jnp.exp(sink - m)
        o = _dot(e, vs) / l
        lse = m + jnp.log(l)
        lane = lax.broadcasted_iota(jnp.int32, (1, 128), 1)
        tile = jnp.zeros((b, 128), F32)
        for g in range(SWA_GROUP):
            o_ref[:, g * 128:(g + 1) * 128] = o[g * b:(g + 1) * b]
            tile = tile + jnp.where(lane == g, lse[g * b:(g + 1) * b], 0.0)
        lse_ref[0] = tile

    qcol = lambda g: (lambda h: CB_SQ + h * SWA_GROUP + g)
    kcol, vcol, one = (lambda h: CB_SK + h), (lambda h: CB_SV + h), (lambda h: 0)
    in_specs = [at(qcol(0), 0), at(qcol(1), 0), at(qcol(2), 0), at(kcol, -1), at(kcol, 0), at(vcol, -1), at(vcol, 0),
                at(one, 0), at(one, 0), at(one, -1), at(one, -1), pl.BlockSpec(memory_space=pltpu.SMEM)]
    return pl.pallas_call(
        body, name=name, grid=(SWA_KV_HEADS, nb), in_specs=in_specs,
        out_specs=[pl.BlockSpec((b, SWA_GROUP * 128), lambda h, n: (n, h)), pl.BlockSpec((1, b, 128), lambda h, n: (h, n, 0))],
        out_shape=[jax.ShapeDtypeStruct((t, SWA_W), F32), jax.ShapeDtypeStruct((SWA_KV_HEADS, t, 128), F32)],
        compiler_params=_cp(("parallel", "parallel")),
    )(p, p, p, p, p, p, p, cosf, sins, cosf, sins, sinks)


def _swa_bwd(p, cosf, sins, sinks, o, lse, dycat, name):
    t = p.shape[0]
    b = SWA_BLOCK
    nb = t // b
    at = _swa_specs(t, None)
    scale = HEAD_DIM ** -0.5
    gb = SWA_GROUP * b

    def body(q0, q1, q2, r0, r1, r2, kp, kc, vp, vc, cc, sc, cp, sp, cn, sn, d0, d1, d2, e0, e1, e2,
             oc_ref, on_ref, lc_ref, ln_ref, sinks_ref, dq_ref, dk_ref, dv_ref, dsk_ref):
        h, n = pl.program_id(0), pl.program_id(1)
        lane = lax.broadcasted_iota(jnp.int32, (1, 128), 1)
        stack = lambda refs: jnp.concatenate([x[...] for x in refs], axis=0)
        q_c = jnp.concatenate([_rope(q[...], cc[...], sc[...]) for q in (q0, q1, q2)], axis=0)
        q_n = jnp.concatenate([_rope(q[...], cn[...], sn[...]) for q in (r0, r1, r2)], axis=0)
        k_p = _rope(kp[...], cp[...], sp[...])
        k_c = _rope(kc[...], cc[...], sc[...])
        do_c, do_n = stack((d0, d1, d2)), stack((e0, e1, e2))
        o_c = jnp.concatenate([oc_ref[:, g * 128:(g + 1) * 128] for g in range(SWA_GROUP)], axis=0)
        o_n = jnp.concatenate([on_ref[:, g * 128:(g + 1) * 128] for g in range(SWA_GROUP)], axis=0)
        lse_c = jnp.concatenate([_lane_pick(lc_ref[0], lane, g) for g in range(SWA_GROUP)], axis=0)
        lse_n = jnp.concatenate([_lane_pick(ln_ref[0], lane, g) for g in range(SWA_GROUP)], axis=0)
        dl_c = jnp.sum(do_c * o_c, axis=1, keepdims=True)
        dl_n = jnp.sum(do_n * o_n, axis=1, keepdims=True)
        mp, mc = _swa_masks()

        def pair(qr, kr, v, do, lse_, dl, mask):
            s = _dot(qr, kr, "nt") * scale
            pr = jnp.where(mask, jnp.exp(s - lse_), 0.0)
            ds = pr * (_dot(do, v, "nt") - dl) * scale
            return _dot(ds, kr), _dot(ds, qr, "tn"), _dot(pr, do, "tn")

        dq_a, _, _ = pair(q_c, k_p, vp[...], do_c, lse_c, dl_c, mp & (n > 0))
        dq_b, dk_b, dv_b = pair(q_c, k_c, vc[...], do_c, lse_c, dl_c, mc)
        _, dk_n, dv_n = pair(q_n, k_c, vc[...], do_n, lse_n, dl_n, mp & (n < nb - 1))
        dq = dq_a + dq_b
        for g in range(SWA_GROUP):
            dq_ref[:, g * 128:(g + 1) * 128] = _rope_t(dq[g * b:(g + 1) * b], cc[...], sc[...])
        dk_ref[...] = _rope_t(dk_b + dk_n, cc[...], sc[...])
        dv_ref[...] = dv_b + dv_n

        @pl.when(n == 0)
        def _():
            dsk_ref[...] = jnp.zeros_like(dsk_ref)

        w = -jnp.exp(_swa_sink_col(sinks_ref, h) - lse_c) * dl_c
        acc = jnp.zeros((1, 128), F32)
        for g in range(SWA_GROUP):
            acc = acc + jnp.where(lane == g, jnp.sum(w[g * b:(g + 1) * b], axis=0, keepdims=True), 0.0)
        dsk_ref[0] += jnp.broadcast_to(acc, (8, 128))

    qcol = lambda g: (lambda h: CB_SQ + h * SWA_GROUP + g)
    dcol = lambda g: (lambda h: (DN_W + POOL_W) // 128 + h * SWA_GROUP + g)
    kcol, vcol, one = (lambda h: CB_SK + h), (lambda h: CB_SV + h), (lambda h: 0)
    wide = lambda off: pl.BlockSpec((b, SWA_GROUP * 128), lambda h, n: (jnp.clip(n + off, 0, nb - 1), h))
    lspec = lambda off: pl.BlockSpec((1, b, 128), lambda h, n: (h, jnp.clip(n + off, 0, nb - 1), 0))
    in_specs = ([at(qcol(g), 0) for g in range(3)] + [at(qcol(g), 1) for g in range(3)]
                + [at(kcol, -1), at(kcol, 0), at(vcol, -1), at(vcol, 0)]
                + [at(one, 0), at(one, 0), at(one, -1), at(one, -1), at(one, 1), at(one, 1)]
                + [at(dcol(g), 0) for g in range(3)] + [at(dcol(g), 1) for g in range(3)]
                + [wide(0), wide(1), lspec(0), lspec(1), pl.BlockSpec(memory_space=pltpu.SMEM)])
    kv_out = pl.BlockSpec((b, 128), lambda h, n: (n, h))
    return pl.pallas_call(
        body, name=name, grid=(SWA_KV_HEADS, nb), in_specs=in_specs,
        out_specs=[wide(0), kv_out, kv_out, pl.BlockSpec((1, 8, 128), lambda h, n: (h, 0, 0))],
        out_shape=[jax.ShapeDtypeStruct((t, SWA_W), F32), jax.ShapeDtypeStruct((t, SWA_KV_W), F32),
                   jax.ShapeDtypeStruct((t, SWA_KV_W), F32), jax.ShapeDtypeStruct((SWA_KV_HEADS, 8, 128), F32)],
        compiler_params=_cp(("parallel", "arbitrary")),
    )(*([p] * 10), cosf, sins, cosf, sins, cosf, sins, *([dycat] * 6), o, o, lse, lse, sinks)


def _adam_math(w, g, m, v):
    m = ADAM_B1 * m + (1.0 - ADAM_B1) * g
    v = ADAM_B2 * v + (1.0 - ADAM_B2) * (g * g)
    m_hat = m / (1.0 - ADAM_B1 ** ADAM_STEP)
    v_hat = v / (1.0 - ADAM_B2 ** ADAM_STEP)
    return -ADAM_LR * (m_hat / (jnp.sqrt(v_hat) + ADAM_EPS) + ADAM_WD * w), m, v


def _adamw(w, g, m, v, name):
    shape = w.shape
    cols = shape[-1]
    rows = math.prod(shape[:-1])
    flat = lambda a: a.reshape(rows, cols)
    r = rows
    for cand in (512, 256, 128, 64, 32, 16, 8):
        if rows % cand == 0 and cand * cols * 4 <= (1 << 20):
            r = cand
            break

    def body(w_ref, g_ref, m_ref, v_ref, d_ref, nm_ref, nv_ref):
        d_ref[...], nm_ref[...], nv_ref[...] = _adam_math(w_ref[...], g_ref[...], m_ref[...], v_ref[...])

    spec = pl.BlockSpec((r, cols), lambda i: (i, 0))
    outs = pl.pallas_call(
        body, name=name, grid=(rows // r,), in_specs=[spec] * 4, out_specs=[spec] * 3,
        out_shape=[jax.ShapeDtypeStruct((rows, cols), F32)] * 3, compiler_params=_cp(("parallel",)),
    )(flat(w), flat(g), flat(m), flat(v))
    return tuple(o.reshape(shape) for o in outs)


BIG = ("w_in", "w_out", "ffn_w_up", "ffn_w_down")
CONV = ("dn_conv_w", "ffn_conv_w")
KIND = {"w_in": "col", "w_out": "row", "ffn_w_up": "col", "ffn_w_down": "row"}
SMALL = ("norm_mix_pre", "dn_a_log", "dn_dt_bias", "dn_norm_w", "pool_w", "pool_scale", "swa_sinks",
         "norm_mix_post", "norm_ffn_pre", "ffn_conv_b", "norm_ffn_post")
WEIGHTS = ("norm_mix_pre", "w_in", "dn_conv_w", "dn_a_log", "dn_dt_bias", "dn_norm_w", "pool_w", "pool_scale",
           "swa_sinks", "w_out", "norm_mix_post", "norm_ffn_pre", "ffn_w_up", "ffn_conv_w", "ffn_conv_b",
           "ffn_w_down", "norm_ffn_post")


def _pad_in(w):
    z = lambda n: jnp.zeros(w.shape[:-1] + (n,), w.dtype)
    return jnp.concatenate([w[..., :GATE_END], z(CB_POOL * 128 - GATE_END), w[..., GATE_END:],
                            z(IN_PAD - CB_POOL * 128 - (IN_TRUE - GATE_END))], axis=-1)


def _unpad_in(g):
    return jnp.concatenate([g[..., :GATE_END], g[..., CB_POOL * 128:CB_POOL * 128 + IN_TRUE - GATE_END]], axis=-1)


IN_SHARD = IN_TRUE // 4
IN_SHARD_PAD = -(-IN_SHARD // 128) * 128


def _chip_cols_to_true(w):
    by_chip = w.reshape(w.shape[:-1] + (4, IN_SHARD_PAD))[..., :IN_SHARD]
    return by_chip.reshape(w.shape[:-1] + (IN_TRUE,))


def _true_to_chip_cols(g):
    by_chip = g.reshape(g.shape[:-1] + (4, IN_SHARD))
    by_chip = jnp.pad(by_chip, [(0, 0)] * (by_chip.ndim - 1) + [(0, IN_SHARD_PAD - IN_SHARD)])
    return by_chip.reshape(g.shape[:-1] + (4 * IN_SHARD_PAD,))


def _lanes(v):
    return jnp.zeros((1, 128), F32).at[0, :v.shape[0]].set(v)


def _rope_tables(positions):
    inv_freq = 1.0 / (ROPE_THETA ** (jnp.arange(0, HEAD_DIM, 2, dtype=F32) / HEAD_DIM))
    ang = positions.astype(F32)[:, None] * inv_freq
    cos, sin = jnp.cos(ang), jnp.sin(ang)
    return jnp.concatenate([cos, cos], axis=-1), jnp.concatenate([-sin, sin], axis=-1)


class _GradReduce:
    def __init__(self, place, shard_shapes):
        self.place = place
        self.out = {k: lax.empty(shard_shapes[k], F32) for k in BIG}
        self.pending = []

    def submit(self, l, dw):
        def parts(k, g):
            if KIND[k] == "row":
                return g.reshape(4, -1, g.shape[1])
            return (_true_to_chip_cols(_unpad_in(g)) if k == "w_in" else g)[None]

        names = [k for k in BIG if k in dw]
        tag = f"l{l}_" + "_".join(names)
        mine = [parts(k, dw[k]) for k in names]
        got = _swap_sibling(mine, tag + "_to_sibling")
        self.pending += [(l, k, _chip_sum(a, b, self.place, f"l{l}_chip_sum_{k}")) for k, a, b in zip(names, mine, got)]

    def take(self, names):
        entries = [e for e in self.pending if e[1] in names]
        if not entries:
            return None, None
        self.pending = [e for e in self.pending if e[1] not in names]
        return entries, ([s for _, _, s in entries], [KIND[k] for _, k, _ in entries])

    def arrived(self, entries, got):
        for (l, k, own), g in zip(entries, got):
            self.out[k] = _owner_sum(own, g, KIND[k], self.place, (self.out[k], l), f"l{l}_owner_sum_{k}")

    def finish(self):
        entries, (sums, kinds) = self.take(BIG)
        self.arrived(entries, _scatter_chips(sums, kinds, "last_grads_to_owner"))
        return dict(zip(BIG, _join_halves([self.out[k] for k in BIG], "grads_join")))


class _LayerWeights:
    def __init__(self, layers):
        self.layers = layers

    def layer(self, l):
        return self.layers[l]

    def carry(self, l, k):
        return None


class _WeightGather(_LayerWeights):
    def __init__(self, shards, place):
        depth = shards["w_out"].shape[0]
        self.kinds = [KIND[k] for k in BIG]
        self.raw = [{k: _spread_shard(shards[k], l, KIND[k], place, BF16, f"l{l}_cast_{k}") for k in BIG} for l in range(depth)]
        first = _gather_ici([self.raw[0][k] for k in BIG], self.kinds, "l0_gather")
        self.layers = {0: self._passed(0, first)}

    def _passed(self, l, arrs):
        full = dict(zip(BIG, _gather_pass(arrs, self.kinds, f"l{l}_gather_pass")))
        full["w_in"] = _pad_in(_chip_cols_to_true(full["w_in"]))
        return full

    def carry(self, l, k):
        return (self.raw[l + 1][k], KIND[k]) if l + 1 < len(self.raw) else None

    def carried(self, l, landed):
        self.layers[l + 1] = self._passed(l + 1, [landed[k] for k in BIG])


def _local_step(x, positions, target, w, mats, reduce=None):
    depth = w["norm_mix_pre"].shape[0]
    t = x.shape[0]
    cosf, sins = _rope_tables(positions)
    saved = []
    for l in range(depth):
        nm = f"l{l}_"
        n1, n2, n3, n4 = (w[k][l][None] for k in ("norm_mix_pre", "norm_mix_post", "norm_ffn_pre", "norm_ffn_post"))
        alog, dtb, dnw = _lanes(w["dn_a_log"][l]), _lanes(w["dn_dt_bias"][l]), w["dn_norm_w"][l][None]
        psc, cb = w["pool_scale"][l][None], w["ffn_conv_b"][l][None]
        big = mats.layer(l)
        landed = {}

        def project(a, k, name):
            riding = mats.carry(l, k)
            if riding is None:
                return _mm(a, big[k], "nn", F32, name)
            out, landed[k] = _mm(a, big[k], "nn", F32, name, carry=riding)
            return out

        h = _norm_fwd(x, n1, nm + "norm1")
        p = project(h, "w_in", nm + "in_proj")
        qkv = _dn_pre_fwd(p, w["dn_conv_w"][l], nm + "dn_pre")
        y_dn, st = _dn_fwd(qkv, p, alog, dtb, dnw, nm + "dn")
        y_pool = _pool_fwd(p, w["pool_w"][l], psc, nm + "pool")
        y_swa, lse = _swa_fwd(p, cosf, sins, w["swa_sinks"][l], nm + "swa")
        ycat = jnp.concatenate([y_dn, y_pool, y_swa], axis=1).astype(BF16)
        mix = project(ycat, "w_out", nm + "out_proj")
        x1 = _resnorm_fwd(x, mix, n2, nm + "res1")
        h2 = _norm_fwd(x1, n3, nm + "norm3")
        up = project(h2, "ffn_w_up", nm + "ffn_up")
        act = _ffn_act_fwd(up, w["ffn_conv_w"][l], cb, nm + "ffn_act")
        f = project(act, "ffn_w_down", nm + "ffn_down")
        if landed:
            mats.carried(l, landed)
        x2 = _resnorm_fwd(x1, f, n4, nm + "res2")
        saved.append(dict(x=x, h=h, p=p, qkv=qkv, st=st, y_swa=y_swa, lse=lse, ycat=ycat, mix=mix, x1=x1, h2=h2,
                          up=up, act=act, f=f, n=(n1, n2, n3, n4), alog=alog, dtb=dtb, dnw=dnw, psc=psc, cb=cb))
        x = x2
    loss, dx = _loss_head(x, target, "loss_head")
    grads = {k: [None] * depth for k in WEIGHTS}
    for l in reversed(range(depth)):
        nm, s = f"l{l}_b_", saved[l]
        n1, n2, n3, n4 = s["n"]
        big = mats.layer(l)
        df, g4 = _norm_bwd(s["f"], n4, dx, None, BF16, nm + "res2")
        dact = _mm(df, big["ffn_w_down"], "nt", F32, nm + "ffn_down_dx")
        grads["ffn_w_down"][l] = _mm(s["act"], df, "tn", BF16, nm + "ffn_down_dw")
        dup, dcw, dcb = _ffn_act_bwd(s["up"], w["ffn_conv_w"][l], s["cb"], dact, nm + "ffn_act")
        grads["ffn_conv_w"][l] = jnp.concatenate([dcw[0], dcw[1]], axis=1)
        grads["ffn_conv_b"][l] = jnp.concatenate([dcb[0], dcb[1]], axis=1)[0]
        grads["ffn_w_up"][l] = _mm(s["h2"], dup, "tn", BF16, nm + "ffn_up_dw", b_pick="split")
        dh2 = _mm(dup, big["ffn_w_up"], "nt", BF16, nm + "ffn_up_dx", a_pick="split")
        dx1, g3 = _norm_bwd(s["x1"], n3, dh2, dx, F32, nm + "norm3")
        dmix, g2 = _norm_bwd(s["mix"], n2, dx1, None, BF16, nm + "res1")
        grads["w_out"][l] = _mm(s["ycat"], dmix, "tn", BF16, nm + "out_proj_dw")
        dycat = _mm(dmix, big["w_out"], "nt", F32, nm + "out_proj_dx")
        riders = (lambda names: reduce.take(names)) if reduce is not None else (lambda names: (None, None))
        if reduce is not None:
            reduce.submit(l, {k: grads[k][l] for k in ("ffn_w_down", "ffn_w_up", "w_out")})
        entries, riding = riders(("ffn_w_down", "ffn_w_up"))
        res = _dn_bwd(s["qkv"], s["p"], s["alog"], s["dtb"], s["dnw"], s["st"], dycat, nm + "dn", carry=riding)
        dqkv, dz, dbd, gal, gdt, gnw = res[:6]
        if entries:
            reduce.arrived(entries, res[6:])
        dpq, gconv = _dn_pre_bwd(s["p"], w["dn_conv_w"][l], dqkv, nm + "dn_pre")
        dpool, gpw, gpsc = _pool_bwd(s["p"], w["pool_w"][l], s["psc"], dycat, nm + "pool")
        dsq, dsk, dsv, gsk = _swa_bwd(s["p"], cosf, sins, w["swa_sinks"][l], s["y_swa"], s["lse"], dycat, nm + "swa")
        dp = jnp.concatenate([dpq, dz, dbd, dpool, dsq, dsk, dsv, jnp.zeros((t, 128), F32)], axis=1).astype(BF16)
        entries, riding = riders(("w_in",))
        res = _mm(s["h"], dp, "tn", BF16, nm + "in_proj_dw", scatter=riding)
        grads["w_in"][l] = res[0] if entries else res
        if entries:
            reduce.arrived(entries, res[1:])
        entries, riding = riders(("w_out",))
        res = _mm(dp, big["w_in"], "nt", BF16, nm + "in_proj_dx", scatter=riding)
        dh = res[0] if entries else res
        if entries:
            reduce.arrived(entries, res[1:])
        dx, g1 = _norm_bwd(s["x"], n1, dh, dx1, F32, nm + "norm1")
        if reduce is not None:
            reduce.submit(l, {"w_in": grads["w_in"][l]})
        grads["norm_mix_pre"][l], grads["norm_mix_post"][l] = g1[0], g2[0]
        grads["norm_ffn_pre"][l], grads["norm_ffn_post"][l] = g3[0], g4[0]
        grads["dn_conv_w"][l] = gconv
        grads["dn_a_log"][l], grads["dn_dt_bias"][l], grads["dn_norm_w"][l] = gal[0, :DN_HEADS], gdt[0, :DN_HEADS], gnw[0]
        grads["pool_w"][l], grads["pool_scale"][l] = gpw, gpsc[0]
        grads["swa_sinks"][l] = gsk[:, 0, :SWA_GROUP].reshape(SWA_HEADS)
    return loss, dx, grads


def _flat2(a):
    return a.reshape(math.prod(a.shape[:-1]), a.shape[-1])


def _ew_rows(rows, cols, n_arrays):
    for cand in (512, 256, 128, 64, 32, 16):
        if rows % cand == 0 and cand * cols * 4 * n_arrays <= (8 << 20):
            return cand
    return rows


def _spread_shard(a, layer, kind, place, dtype, name):
    _, rows, cols = a.shape
    r = _ew_rows(rows, cols, 2)
    nb = rows // r

    def body(s_ref, a_ref, o_ref):
        o_ref[...] = a_ref[...].astype(o_ref.dtype)

    if kind == "row":
        out_spec = pl.BlockSpec((r, cols), lambda i, s: (s[0] * nb + i, 0))
        out_shape = (4 * rows, cols)
    else:
        out_spec = pl.BlockSpec((r, cols), lambda i, s: (i, s[0]))
        out_shape = (rows, 4 * cols)
    return pl.pallas_call(
        body, name=name,
        grid_spec=pltpu.PrefetchScalarGridSpec(
            num_scalar_prefetch=1, grid=(nb,),
            in_specs=[pl.BlockSpec((None, r, cols), lambda i, s: (layer, i, 0))], out_specs=out_spec),
        out_shape=jax.ShapeDtypeStruct(out_shape, dtype), compiler_params=_cp(("parallel",)),
    )(place, a)


def _chip_sum(mine, sib, place, name):
    parts, rows, cols = sib.shape
    r = _ew_rows(rows, cols, 3)
    nb = rows // r

    def body(s_ref, a_ref, b_ref, o_ref):
        o_ref[...] = (a_ref[...].astype(F32) + b_ref[...].astype(F32)).astype(o_ref.dtype)

    spec = pl.BlockSpec((None, r, cols), lambda j, i, s: (j, i, 0))
    return pl.pallas_call(
        body, name=name,
        grid_spec=pltpu.PrefetchScalarGridSpec(
            num_scalar_prefetch=1, grid=(parts, nb),
            in_specs=[pl.BlockSpec((None, r, cols), lambda j, i, s: (j, s[1] * nb + i, 0)), spec], out_specs=spec),
        out_shape=jax.ShapeDtypeStruct(sib.shape, BF16), compiler_params=_cp(("parallel", "parallel")),
    )(place, mine, sib)


def _sum_slots(a, name):
    s = a.shape[0]
    a3 = a.reshape(s, math.prod(a.shape[1:-1]), a.shape[-1])
    _, rows, cols = a3.shape
    r = _ew_rows(rows, cols, s + 1)

    def body(a_ref, o_ref):
        acc = a_ref[0].astype(F32)
        for k in range(1, s):
            acc = acc + a_ref[k].astype(F32)
        o_ref[...] = acc

    return pl.pallas_call(body, name=name, grid=(rows // r,),
                          in_specs=[pl.BlockSpec((s, r, cols), lambda i: (0, i, 0))],
                          out_specs=pl.BlockSpec((r, cols), lambda i: (i, 0)),
                          out_shape=jax.ShapeDtypeStruct((rows, cols), F32), compiler_params=_cp(("parallel",)),
                          )(a3).reshape(a.shape[1:])


def _owner_sum(own, got, kind, place, into, name):
    buf, slab = into
    _, rows, cols = got.shape
    r = _ew_rows(rows, cols, 6)
    nb = rows // r

    def body(s_ref, own_ref, got_ref, buf_ref, o_ref):
        acc = own_ref[...].astype(F32)
        for k in range(3):
            acc = acc + got_ref[k].astype(F32)
        o_ref[...] = acc

    if kind == "row":
        own_spec = pl.BlockSpec((None, r, cols), lambda i, s: (s[0], i, 0))
    else:
        own_spec = pl.BlockSpec((None, r, cols), lambda i, s: (0, i, s[0]))
    return pl.pallas_call(
        body, name=name,
        grid_spec=pltpu.PrefetchScalarGridSpec(
            num_scalar_prefetch=1, grid=(nb,),
            in_specs=[own_spec, pl.BlockSpec((3, r, cols), lambda i, s: (0, i, 0)), pl.BlockSpec(memory_space=pl.ANY)],
            out_specs=pl.BlockSpec((None, r, cols), lambda i, s: (slab, s[1] * nb + i, 0))),
        out_shape=jax.ShapeDtypeStruct(buf.shape, buf.dtype), input_output_aliases={3: 0},
        compiler_params=_cp(("parallel",)),
    )(place, own, got, buf)


MESH = pl.DeviceIdType.MESH
ANY = pl.BlockSpec(memory_space=pl.ANY)


def _place():
    x, y, c = lax.axis_index("x"), lax.axis_index("y"), lax.axis_index("c")
    chips = [(1 - x, y), (x, 1 - y), (1 - x, 1 - y)]
    return x, y, c, chips


def _half_part(ref, kind, chip, half):
    if kind == "row":
        h = ref.shape[0] // 8
        return ref.at[pl.ds(pl.multiple_of((2 * chip + half) * h, 16), h), :]
    h, width = ref.shape[0] // 2, ref.shape[1] // 4
    return ref.at[pl.ds(pl.multiple_of(half * h, 16), h), pl.ds(pl.multiple_of(chip * width, 128), width)]


def _gather_copies(w_ref, kind, send, recv):
    x, y, c, chips = _place()
    mine = _half_part(w_ref, kind, 2 * x + y, c)
    return [pltpu.make_async_remote_copy(mine, mine, send.at[j], recv.at[j], device_id=(px, py, c), device_id_type=MESH)
            for j, (px, py) in enumerate(chips)]


def _gather_ici(arrs, kinds, name):
    na = len(arrs)

    def body(*refs):
        outs, send, recv = refs[na:2 * na], refs[2 * na], refs[2 * na + 1]
        cps = [cp for k in range(na) for cp in _gather_copies(outs[k], kinds[k], send.at[k], recv.at[k])]
        for cp in cps:
            cp.start()
        for cp in cps:
            cp.wait()

    return pl.pallas_call(
        body, name=name, in_specs=[ANY] * na, out_specs=[ANY] * na,
        out_shape=[jax.ShapeDtypeStruct(a.shape, a.dtype) for a in arrs],
        input_output_aliases={k: k for k in range(na)},
        scratch_shapes=[pltpu.SemaphoreType.DMA((na, 3))] * 2,
    )(*arrs)


def _gather_pass(arrs, kinds, name):
    na = len(arrs)

    def body(*refs):
        outs, send, recv = refs[na:2 * na], refs[2 * na], refs[2 * na + 1]
        x, y, c, chips = _place()
        cps, arrivals = [], []
        for k in range(na):
            for j, (px, py) in enumerate(chips):
                mine = _half_part(outs[k], kinds[k], 2 * px + py, c)
                theirs = _half_part(outs[k], kinds[k], 2 * px + py, 1 - c)
                cps.append(pltpu.make_async_remote_copy(mine, mine, send.at[k, j], recv.at[k, j],
                                                        device_id=(x, y, 1 - c), device_id_type=MESH))
                arrivals.append(pltpu.make_async_remote_copy(theirs, theirs, send.at[k, j], recv.at[k, j],
                                                             device_id=(x, y, 1 - c), device_id_type=MESH))
        for cp in cps:
            cp.start()
        for cp, arrival in zip(cps, arrivals):
            cp.wait_send()
            arrival.wait_recv()

    return pl.pallas_call(
        body, name=name, in_specs=[ANY] * na, out_specs=[ANY] * na,
        out_shape=[jax.ShapeDtypeStruct(a.shape, a.dtype) for a in arrs],
        input_output_aliases={k: k for k in range(na)},
        scratch_shapes=[pltpu.SemaphoreType.DMA((na, 3))] * 2,
    )(*arrs)


def _swap_sibling(arrs, name):
    na = len(arrs)

    def body(*refs):
        ins, outs, send, recv = refs[:na], refs[na:2 * na], refs[2 * na], refs[2 * na + 1]
        x, y, c, _ = _place()
        cps = []
        for k in range(na):
            h = ins[k].shape[1] // 2
            cps.append(pltpu.make_async_remote_copy(ins[k].at[:, pl.ds(pl.multiple_of((1 - c) * h, 16), h), :], outs[k],
                                                    send.at[k], recv.at[k], device_id=(x, y, 1 - c), device_id_type=MESH))
        for cp in cps:
            cp.start()
        for cp in cps:
            cp.wait()

    return pl.pallas_call(
        body, name=name, in_specs=[ANY] * na, out_specs=[ANY] * na,
        out_shape=[jax.ShapeDtypeStruct((a.shape[0], a.shape[1] // 2, a.shape[2]), a.dtype) for a in arrs],
        scratch_shapes=[pltpu.SemaphoreType.DMA((na,))] * 2,
    )(*arrs)


def _scatter_shapes(sums, kinds):
    return [jax.ShapeDtypeStruct((3, a.shape[1], a.shape[2] if kind == "row" else a.shape[2] // 4), a.dtype)
            for a, kind in zip(sums, kinds)]


def _scatter_copies(srcs, dsts, kinds, send, recv):
    x, y, c, chips = _place()
    cps = []
    for k, (src, dst) in enumerate(zip(srcs, dsts)):
        for j, (px, py) in enumerate(chips):
            chip = 2 * px + py
            if kinds[k] == "row":
                part = src.at[chip]
            else:
                width = src.shape[2] // 4
                part = src.at[0, :, pl.ds(pl.multiple_of(chip * width, 128), width)]
            cps.append(pltpu.make_async_remote_copy(part, dst.at[j], send.at[k, j], recv.at[k, j],
                                                    device_id=(px, py, c), device_id_type=MESH))
    return cps


def _scatter_chips(sums, kinds, name):
    na = len(sums)

    def body(*refs):
        cps = _scatter_copies(refs[:na], refs[na:2 * na], kinds, refs[2 * na], refs[2 * na + 1])
        for cp in cps:
            cp.start()
        for cp in cps:
            cp.wait()

    return pl.pallas_call(
        body, name=name, in_specs=[ANY] * na, out_specs=[ANY] * na, out_shape=_scatter_shapes(sums, kinds),
        scratch_shapes=[pltpu.SemaphoreType.DMA((na, 3))] * 2,
    )(*sums)


def _join_halves(arrs, name):
    na = len(arrs)

    def body(*refs):
        outs, send, recv = refs[na:2 * na], refs[2 * na], refs[2 * na + 1]
        x, y, c, _ = _place()
        halves = [a.shape[1] // 2 for a in arrs]
        mine = [outs[k].at[:, pl.ds(pl.multiple_of(c * h, 8), h), :] for k, h in enumerate(halves)]
        theirs = [outs[k].at[:, pl.ds(pl.multiple_of((1 - c) * h, 8), h), :] for k, h in enumerate(halves)]
        cps = [pltpu.make_async_remote_copy(mine[k], mine[k], send.at[k], recv.at[k],
                                            device_id=(x, y, 1 - c), device_id_type=MESH) for k in range(na)]
        for cp in cps:
            cp.start()
        for k, cp in enumerate(cps):
            cp.wait_send()
            pltpu.make_async_remote_copy(theirs[k], theirs[k], send.at[k], recv.at[k],
                                         device_id=(x, y, 1 - c), device_id_type=MESH).wait_recv()

    return pl.pallas_call(
        body, name=name, in_specs=[ANY] * na, out_specs=[ANY] * na,
        out_shape=[jax.ShapeDtypeStruct(a.shape, a.dtype) for a in arrs],
        input_output_aliases={k: k for k in range(na)},
        scratch_shapes=[pltpu.SemaphoreType.DMA((na,))] * 2,
    )(*arrs)


def _gather_all(a, name):
    def body(a_ref, o_ref, send, recv, local):
        x, y, c, _ = _place()
        me = 4 * x + 2 * y + c
        mine = pltpu.make_async_copy(a_ref, o_ref.at[me], local)
        mine.start()
        cps = []
        for j in range(1, 8):
            peer = (x ^ (j >> 2), y ^ ((j >> 1) & 1), c ^ (j & 1))
            cps.append(pltpu.make_async_remote_copy(a_ref, o_ref.at[me], send.at[j - 1], recv.at[j - 1],
                                                    device_id=peer, device_id_type=MESH))
        for cp in cps:
            cp.start()
        for cp in cps:
            cp.wait()
        mine.wait()

    return pl.pallas_call(
        body, name=name, in_specs=[ANY], out_specs=ANY,
        out_shape=jax.ShapeDtypeStruct((8,) + a.shape, a.dtype),
        scratch_shapes=[pltpu.SemaphoreType.DMA((7,)), pltpu.SemaphoreType.DMA((7,)), pltpu.SemaphoreType.DMA],
    )(a)


def _pack(parts):
    flat = jnp.concatenate([p.reshape(-1) for p in parts])
    n = flat.shape[0]
    rows = -(-n // (PACK_ROWS * 128)) * PACK_ROWS
    return jnp.pad(flat, (0, rows * 128 - n)).reshape(rows, 128)


def _unpack(buf, like):
    flat, out, off = buf.reshape(-1), [], 0
    for p in like:
        out.append(flat[off:off + p.size].reshape(p.shape))
        off += p.size
    return out


def kernel(x, positions, norm_mix_pre, w_in, dn_conv_w, dn_a_log, dn_dt_bias, dn_norm_w, pool_w, pool_scale, swa_sinks, w_out, norm_mix_post, norm_ffn_pre, ffn_w_up, ffn_conv_w, ffn_conv_b, ffn_w_down, norm_ffn_post, loss_target, m_norm_mix_pre, m_w_in, m_dn_conv_w, m_dn_a_log, m_dn_dt_bias, m_dn_norm_w, m_pool_w, m_pool_scale, m_swa_sinks, m_w_out, m_norm_mix_post, m_norm_ffn_pre, m_ffn_w_up, m_ffn_conv_w, m_ffn_conv_b, m_ffn_w_down, m_norm_ffn_post, v_norm_mix_pre, v_w_in, v_dn_conv_w, v_dn_a_log, v_dn_dt_bias, v_dn_norm_w, v_pool_w, v_pool_scale, v_swa_sinks, v_w_out, v_norm_mix_post, v_norm_ffn_pre, v_ffn_w_up, v_ffn_conv_w, v_ffn_conv_b, v_ffn_w_down, v_norm_ffn_post):
    wts = dict(zip(WEIGHTS, (norm_mix_pre, w_in, dn_conv_w, dn_a_log, dn_dt_bias, dn_norm_w, pool_w, pool_scale, swa_sinks,
                             w_out, norm_mix_post, norm_ffn_pre, ffn_w_up, ffn_conv_w, ffn_conv_b, ffn_w_down, norm_ffn_post)))
    mom = dict(zip(WEIGHTS, (m_norm_mix_pre, m_w_in, m_dn_conv_w, m_dn_a_log, m_dn_dt_bias, m_dn_norm_w, m_pool_w, m_pool_scale,
                             m_swa_sinks, m_w_out, m_norm_mix_post, m_norm_ffn_pre, m_ffn_w_up, m_ffn_conv_w, m_ffn_conv_b,
                             m_ffn_w_down, m_norm_ffn_post)))
    var = dict(zip(WEIGHTS, (v_norm_mix_pre, v_w_in, v_dn_conv_w, v_dn_a_log, v_dn_dt_bias, v_dn_norm_w, v_pool_w, v_pool_scale,
                             v_swa_sinks, v_w_out, v_norm_mix_post, v_norm_ffn_pre, v_ffn_w_up, v_ffn_conv_w, v_ffn_conv_b,
                             v_ffn_w_down, v_norm_ffn_post)))
    c = lax.axis_index("c")
    chip = 2 * lax.axis_index("x") + lax.axis_index("y")
    place = jnp.stack([chip, c]).astype(jnp.int32)
    shards = dict(wts, w_in=jnp.pad(w_in, ((0, 0), (0, 0), (0, IN_SHARD_PAD - IN_SHARD))))
    mats = _WeightGather(shards, place)
    w = dict(wts)
    conv_like = [wts[k] for k in CONV]
    conv_all = _gather_all(_pack(conv_like), "gather_conv")
    for i, k in enumerate(CONV):
        w[k] = jnp.concatenate([_unpack(conv_all[2 * j], conv_like)[i] for j in range(4)], axis=2)

    reduce = _GradReduce(place, {k: shards[k].shape for k in BIG})
    loss, dx, grads = _local_step(x[0], positions[0], loss_target[0], w, mats, reduce)
    loss = lax.psum(loss[0, 0], ("x", "y", "c"))
    g_big = reduce.finish()
    g_big["w_in"] = g_big["w_in"][..., :IN_SHARD]

    small_like = [wts[k] for k in SMALL]
    full_like = small_like + [w[k] for k in CONV]
    g_buf = _sum_slots(_gather_all(_pack([jnp.stack(grads[k]) for k in SMALL + CONV]), "gather_small"), "sum_small")
    g_small = dict(zip(SMALL + CONV, _unpack(g_buf, full_like)))
    for k in CONV:
        width = wts[k].shape[2]
        g_small[k] = lax.dynamic_slice_in_dim(g_small[k], chip * width, width, 2)
    pk = lambda d: _pack([d[k] for k in SMALL + CONV])
    upd = _adamw(pk(wts), pk(g_small), pk(mom), pk(var), "adam_small")
    upd_small = [dict(zip(SMALL + CONV, _unpack(b, small_like + conv_like))) for b in upd]

    g_all, d_all, m_all, v_all = {}, {}, {}, {}
    for k in WEIGHTS:
        if k in BIG:
            g_all[k] = g_big[k]
            d_all[k], m_all[k], v_all[k] = _adamw(wts[k], g_big[k], mom[k], var[k], "adam_" + k)
        else:
            g_all[k], d_all[k], m_all[k], v_all[k] = g_small[k], upd_small[0][k], upd_small[1][k], upd_small[2][k]
    return (loss, dx[None], *[g_all[k] for k in WEIGHTS], *[d_all[k] for k in WEIGHTS],
            *[m_all[k] for k in WEIGHTS], *[v_all[k] for k in WEIGHTS])
```

```python
import functools
import math

import jax
import jax.numpy as jnp
from jax import lax
from jax.experimental import pallas as pl
from jax.experimental.pallas import tpu as pltpu

F32 = jnp.float32
BF16 = jnp.bfloat16

HEAD_DIM = 128
DN_HEADS = 6
DN_CONV = 4
DN_CHUNK = 64
POOL_GROUPS = 4
SWA_HEADS = 6
SWA_KV_HEADS = 2
SWA_GROUP = SWA_HEADS // SWA_KV_HEADS
SWA_BLOCK = 128
ROPE_THETA = 10000.0
FFN_CONV = 3
NORM_EPS = 1e-6
DN_W = DN_HEADS * HEAD_DIM
POOL_W = POOL_GROUPS * HEAD_DIM
SWA_W = SWA_HEADS * HEAD_DIM
SWA_KV_W = SWA_KV_HEADS * HEAD_DIM
MIX_W = DN_W + POOL_W + SWA_W
IN_TRUE = 3 * DN_W + DN_W + 2 * DN_HEADS + POOL_W + SWA_W + 2 * SWA_KV_W
GATE_END = 4 * DN_W + 2 * DN_HEADS
CB_Z = 18
CB_BD = 24
CB_POOL = 25
CB_SQ = 29
CB_SK = 35
CB_SV = 37
IN_PAD = 40 * 128
ADAM_LR, ADAM_B1, ADAM_B2, ADAM_EPS, ADAM_WD, ADAM_STEP = 0.001, 0.9, 0.999, 1e-08, 0.01, 10

VMEM_LIMIT = 48 * 1024 * 1024
PACK_ROWS = 512
MM_TK_MAX = 2816
HIGH = lax.Precision.HIGHEST


def _cp(sem):
    return pltpu.CompilerParams(dimension_semantics=sem, vmem_limit_bytes=VMEM_LIMIT)


def _tile(n, prefs):
    for p in prefs:
        if n % p == 0:
            return p
    return n


def _rows(t):
    return _tile(t, (256, 128))


_DN = {"nn": (((1,), (0,)), ((), ())), "nt": (((1,), (1,)), ((), ())), "tn": (((0,), (0,)), ((), ()))}


def _mm_operand(arr, pick, block, idx):
    if pick is None:
        return pl.BlockSpec(block, idx)
    if pick == "split":
        per = arr.shape[2] // block[1]

        def split_idx(i, j, kk):
            r, c = idx(i, j, kk)
            return lax.div(c, per), r, lax.rem(c, per)

        return pl.BlockSpec((None,) + block, split_idx)
    slab = pick[1]
    return pl.BlockSpec((None,) + block, lambda i, j, kk: (slab,) + idx(i, j, kk))


def _mm(a, b, mode, out_dtype, name, a_pick=None, b_pick=None, carry=None, scatter=None):
    def dims(arr, pick):
        r, c = arr.shape[-2:]
        return (r, c * arr.shape[0]) if pick == "split" else (r, c)

    (a0, a1), (b0, b1) = dims(a, a_pick), dims(b, b_pick)
    k, m = (a0, a1) if mode == "tn" else (a1, a0)
    n = b0 if mode == "nt" else b1
    lim = lambda arr, pick, is_last, full: arr.shape[2] if (pick == "split" and is_last) else full
    tm = _tile(lim(a, a_pick, mode == "tn", m), (1024, 512, 256, 128))
    tn = _tile(lim(b, b_pick, mode != "nt", n), (1408, 1280, 1024, 512, 256, 128))
    k_lim = min(lim(a, a_pick, mode != "tn", k), lim(b, b_pick, mode == "nt", k))
    tk = max([d for d in range(128, min(k_lim, MM_TK_MAX) + 1, 128) if k_lim % d == 0], default=k_lim)
    nk = k // tk

    grid = (m // tm, n // tn, nk)

    riding = carry is not None or scatter is not None
    ns = len(scatter[0]) if scatter is not None else 0

    def body(a_ref, b_ref, *rest):
        if carry is not None:
            _, o_ref, w_ref, *scratch = rest
            copies = _gather_copies(w_ref, carry[1], *scratch[-2:])
        elif scatter is not None:
            o_ref, scratch = rest[ns], rest[2 * ns + 1:]
            copies = _scatter_copies(rest[:ns], rest[ns + 1:2 * ns + 1], scatter[1], *scratch[-2:])
        else:
            o_ref, *scratch = rest
        if riding:
            scratch = scratch[:-2]
            step = (pl.program_id(0) * grid[1] + pl.program_id(1)) * grid[2] + pl.program_id(2)

            @pl.when(step == 0)
            def _():
                for cp in copies:
                    cp.start()
        part = lax.dot_general(a_ref[...], b_ref[...], _DN[mode], preferred_element_type=F32)
        if nk == 1:
            o_ref[...] = part.astype(o_ref.dtype)
        else:
            acc_ref, = scratch
            kk = pl.program_id(2)

            @pl.when(kk == 0)
            def _():
                acc_ref[...] = part

            @pl.when(kk > 0)
            def _():
                acc_ref[...] += part

            @pl.when(kk == nk - 1)
            def _():
                o_ref[...] = acc_ref[...].astype(o_ref.dtype)
        if riding:
            @pl.when(step == grid[0] * grid[1] * grid[2] - 1)
            def _():
                for cp in copies:
                    cp.wait()

    if mode == "tn":
        a_spec = _mm_operand(a, a_pick, (tk, tm), lambda i, j, kk: (kk, i))
    else:
        a_spec = _mm_operand(a, a_pick, (tm, tk), lambda i, j, kk: (i, kk))
    if mode == "nt":
        b_spec = _mm_operand(b, b_pick, (tn, tk), lambda i, j, kk: (j, kk))
    else:
        b_spec = _mm_operand(b, b_pick, (tk, tn), lambda i, j, kk: (kk, j))
    scratch = [pltpu.VMEM((tm, tn), F32)] if nk > 1 else []
    out_spec = pl.BlockSpec((tm, tn), lambda i, j, kk: (i, j))
    out_shape = jax.ShapeDtypeStruct((m, n), out_dtype)
    if not riding:
        return pl.pallas_call(
            body, name=name, grid=grid, in_specs=[a_spec, b_spec], out_specs=out_spec, out_shape=out_shape,
            scratch_shapes=scratch, compiler_params=_cp(("parallel", "parallel", "arbitrary")),
        )(a, b)
    any_space = pl.BlockSpec(memory_space=pl.ANY)
    in_order = _cp(("arbitrary", "arbitrary", "arbitrary"))
    if carry is not None:
        return pl.pallas_call(
            body, name=name, grid=grid, in_specs=[a_spec, b_spec, any_space], out_specs=[out_spec, any_space],
            out_shape=[out_shape, jax.ShapeDtypeStruct(carry[0].shape, carry[0].dtype)], input_output_aliases={2: 1},
            scratch_shapes=scratch + [pltpu.SemaphoreType.DMA((3,))] * 2, compiler_params=in_order,
        )(a, b, carry[0])
    return pl.pallas_call(
        body, name=name, grid=grid, in_specs=[a_spec, b_spec] + [any_space] * ns, out_specs=[out_spec] + [any_space] * ns,
        out_shape=[out_shape] + _scatter_shapes(*scatter),
        scratch_shapes=scratch + [pltpu.SemaphoreType.DMA((ns, 3))] * 2, compiler_params=in_order,
    )(a, b, *scatter[0])


def _rms(x, w):
    return x * lax.rsqrt(jnp.mean(x * x, axis=-1, keepdims=True) + NORM_EPS) * w


def _norm_fwd(x, w, name):
    t, d = x.shape
    r = _rows(t)

    def body(x_ref, w_ref, h_ref):
        h_ref[...] = _rms(x_ref[...], w_ref[...]).astype(h_ref.dtype)

    return pl.pallas_call(
        body, name=name, grid=(t // r,),
        in_specs=[pl.BlockSpec((r, d), lambda i: (i, 0)), pl.BlockSpec((1, d), lambda i: (0, 0))],
        out_specs=pl.BlockSpec((r, d), lambda i: (i, 0)),
        out_shape=jax.ShapeDtypeStruct((t, d), BF16), compiler_params=_cp(("parallel",)),
    )(x, w)


def _resnorm_fwd(x, y, w, name):
    t, d = x.shape
    r = _rows(t)

    def body(x_ref, y_ref, w_ref, o_ref):
        o_ref[...] = x_ref[...] + _rms(y_ref[...], w_ref[...])

    return pl.pallas_call(
        body, name=name, grid=(t // r,),
        in_specs=[pl.BlockSpec((r, d), lambda i: (i, 0)), pl.BlockSpec((r, d), lambda i: (i, 0)),
                  pl.BlockSpec((1, d), lambda i: (0, 0))],
        out_specs=pl.BlockSpec((r, d), lambda i: (i, 0)),
        out_shape=jax.ShapeDtypeStruct((t, d), F32), compiler_params=_cp(("parallel",)),
    )(x, y, w)


def _norm_bwd(x, w, dh, add, out_dtype, name):
    t, d = x.shape
    r = _rows(t)
    has_add = add is not None

    def body(*refs):
        if has_add:
            x_ref, w_ref, dh_ref, add_ref, dx_ref, dw_ref = refs
        else:
            x_ref, w_ref, dh_ref, dx_ref, dw_ref = refs
        xv = x_ref[...]
        g = dh_ref[...].astype(F32)
        rs = lax.rsqrt(jnp.mean(xv * xv, axis=-1, keepdims=True) + NORM_EPS)
        xh = xv * rs
        gw = g * w_ref[...]
        dx = rs * (gw - xh * jnp.mean(gw * xh, axis=-1, keepdims=True))
        if has_add:
            dx = dx + add_ref[...]
        dx_ref[...] = dx.astype(dx_ref.dtype)

        @pl.when(pl.program_id(0) == 0)
        def _():
            dw_ref[...] = jnp.zeros_like(dw_ref)

        dw_ref[...] += jnp.sum(g * xh, axis=0, keepdims=True)

    row = pl.BlockSpec((r, d), lambda i: (i, 0))
    vec = pl.BlockSpec((1, d), lambda i: (0, 0))
    ins = [x, w, dh] + ([add] if has_add else [])
    return pl.pallas_call(
        body, name=name, grid=(t // r,),
        in_specs=[row, vec, row] + ([row] if has_add else []),
        out_specs=[row, vec],
        out_shape=[jax.ShapeDtypeStruct((t, d), out_dtype), jax.ShapeDtypeStruct((1, d), F32)],
        compiler_params=_cp(("arbitrary",)),
    )(*ins)


def _loss_head(y, target, name):
    t, d = y.shape
    r = _rows(t)

    def body(y_ref, t_ref, l_ref, g_ref):
        e = y_ref[...] - t_ref[...]
        g_ref[...] = e * (1.0 / d)

        @pl.when(pl.program_id(0) == 0)
        def _():
            l_ref[...] = jnp.zeros_like(l_ref)

        l_ref[...] += jnp.sum(e * e) * (0.5 / d)

    row = pl.BlockSpec((r, d), lambda i: (i, 0))
    return pl.pallas_call(
        body, name=name, grid=(t // r,), in_specs=[row, row],
        out_specs=[pl.BlockSpec((1, 128), lambda i: (0, 0)), row],
        out_shape=[jax.ShapeDtypeStruct((1, 128), F32), jax.ShapeDtypeStruct((t, d), F32)],
        compiler_params=_cp(("arbitrary",)),
    )(y, target)


def _down(x, s):
    return x if s == 0 else pltpu.roll(x, s, 0)


def _up(x, s):
    return x if s == 0 else pltpu.roll(x, x.shape[0] - s, 0)


def _halo(t, r, hh, tc, col):
    q = r // hh
    last = t // hh - 1
    tile = pl.BlockSpec((r, tc), lambda j, i: (i, col(j)))
    prev = pl.BlockSpec((hh, tc), lambda j, i: (jnp.maximum(i * q - 1, 0), col(j)))
    nxt = pl.BlockSpec((hh, tc), lambda j, i: (jnp.minimum((i + 1) * q, last), col(j)))
    return tile, prev, nxt


def _sig(x):
    return 1.0 / (1.0 + jnp.exp(-x))


def _dsilu(x, s):
    return s * (1.0 + x * (1.0 - s))


def _dn_pre_fwd(p, conv_w, name):
    t = p.shape[0]
    r = _rows(t)

    def body(x_ref, xp_ref, w_ref, o_ref):
        j, i = pl.program_id(0), pl.program_id(1)
        xe = jnp.concatenate([jnp.where(i == 0, 0.0, xp_ref[...]), x_ref[...]], axis=0)
        c = sum(_down(xe, DN_CONV - 1 - k) * w_ref[pl.ds(k, 1), :] for k in range(DN_CONV))[8:]
        a = c * _sig(c)
        for h in range(DN_HEADS):
            ah = a[:, h * 128:(h + 1) * 128]
            fac = lax.rsqrt(jnp.sum(ah * ah, axis=-1, keepdims=True) + NORM_EPS)
            o_ref[:, h * 128:(h + 1) * 128] = ah * jnp.where(j == 0, fac * HEAD_DIM ** -0.5, jnp.where(j == 1, fac, 1.0))

    tile, prev, _ = _halo(t, r, 8, DN_W, lambda j: j)
    return pl.pallas_call(
        body, name=name, grid=(3, t // r),
        in_specs=[tile, prev, pl.BlockSpec((DN_CONV, DN_W), lambda j, i: (0, j))],
        out_specs=tile, out_shape=jax.ShapeDtypeStruct((t, 3 * DN_W), F32),
        compiler_params=_cp(("parallel", "parallel")),
    )(p, p, conv_w)


def _dn_pre_bwd(p, conv_w, dqkv, name):
    t = p.shape[0]
    r = _rows(t)
    ni = t // r

    def body(x_ref, xp_ref, xn_ref, w_ref, d_ref, dn_ref, dx_ref, dw_ref):
        j, i = pl.program_id(0), pl.program_id(1)
        xe = jnp.concatenate([jnp.where(i == 0, 0.0, xp_ref[...]), x_ref[...], xn_ref[...]], axis=0)
        de = jnp.concatenate([jnp.zeros((8, DN_W), F32), d_ref[...], jnp.where(i == ni - 1, 0.0, dn_ref[...])], axis=0)
        xs = [_down(xe, DN_CONV - 1 - k) for k in range(DN_CONV)]
        c = sum(xs[k] * w_ref[pl.ds(k, 1), :] for k in range(DN_CONV))
        s = _sig(c)
        a = c * s
        das = []
        for h in range(DN_HEADS):
            ah, dh = a[:, h * 128:(h + 1) * 128], de[:, h * 128:(h + 1) * 128]
            fac = lax.rsqrt(jnp.sum(ah * ah, axis=-1, keepdims=True) + NORM_EPS)
            dnorm = fac * dh - ah * (fac * fac * fac) * jnp.sum(dh * ah, axis=-1, keepdims=True)
            das.append(jnp.where(j == 0, dnorm * HEAD_DIM ** -0.5, jnp.where(j == 1, dnorm, dh)))
        dc = jnp.concatenate(das, axis=1) * _dsilu(c, s)
        dx_ref[...] = sum(_up(dc, DN_CONV - 1 - k) * w_ref[pl.ds(k, 1), :] for k in range(DN_CONV))[8:8 + r]

        @pl.when(i == 0)
        def _():
            dw_ref[...] = jnp.zeros_like(dw_ref)

        for k in range(DN_CONV):
            dw_ref[pl.ds(k, 1), :] += jnp.sum((dc * xs[k])[8:8 + r], axis=0, keepdims=True)

    tile, prev, nxt = _halo(t, r, 8, DN_W, lambda j: j)
    wspec = pl.BlockSpec((DN_CONV, DN_W), lambda j, i: (0, j))
    return pl.pallas_call(
        body, name=name, grid=(3, ni),
        in_specs=[tile, prev, nxt, wspec, tile, nxt],
        out_specs=[tile, wspec],
        out_shape=[jax.ShapeDtypeStruct((t, 3 * DN_W), F32), jax.ShapeDtypeStruct((DN_CONV, 3 * DN_W), F32)],
        compiler_params=_cp(("parallel", "arbitrary")),
    )(p, p, p, conv_w, dqkv, dqkv)


def _ffn_act_fwd(up, cw, cb, name):
    t, f2 = up.shape
    f = f2 // 2
    r = _tile(t, (512, 256, 128))
    tc = _tile(f, (512, 256, 128))
    nj = f // tc

    def body(a_ref, ap_ref, b_ref, bp_ref, wa_ref, wb_ref, ca_ref, cb_ref, o_ref):
        i = pl.program_id(1)

        def conv(x_ref, xp_ref, w_ref, c_ref):
            xe = jnp.concatenate([jnp.where(i == 0, 0.0, xp_ref[...]), x_ref[...]], axis=0)
            return sum(_down(xe, FFN_CONV - 1 - k) * w_ref[pl.ds(k, 1), :] for k in range(FFN_CONV))[8:] + c_ref[...]

        ua = conv(a_ref, ap_ref, wa_ref, ca_ref)
        ub = conv(b_ref, bp_ref, wb_ref, cb_ref)
        o_ref[...] = (ua * _sig(ua) * ub).astype(o_ref.dtype)

    ta, pa, _ = _halo(t, r, 8, tc, lambda j: j)
    tb, pb, _ = _halo(t, r, 8, tc, lambda j: j + nj)
    wa = pl.BlockSpec((FFN_CONV, tc), lambda j, i: (0, j))
    wb = pl.BlockSpec((FFN_CONV, tc), lambda j, i: (0, j + nj))
    ca = pl.BlockSpec((1, tc), lambda j, i: (0, j))
    cbs = pl.BlockSpec((1, tc), lambda j, i: (0, j + nj))
    return pl.pallas_call(
        body, name=name, grid=(nj, t // r),
        in_specs=[ta, pa, tb, pb, wa, wb, ca, cbs], out_specs=ta,
        out_shape=jax.ShapeDtypeStruct((t, f), BF16), compiler_params=_cp(("parallel", "parallel")),
    )(up, up, up, up, cw, cw, cb, cb)


def _ffn_act_bwd(up, cw, cb, dact, name):
    t, f2 = up.shape
    f = f2 // 2
    r = _tile(t, (512, 256, 128))
    ni = t // r
    tc = _tile(f, (512, 256, 128))
    nj = f // tc

    def body(a_ref, ap_ref, an_ref, b_ref, bp_ref, bn_ref, wa_ref, wb_ref, ca_ref, cb_ref, d_ref, dn_ref,
             du_ref, dw_ref, dc_ref):
        i = pl.program_id(1)
        dua_ref, dub_ref, dwa_ref, dwb_ref, dca_ref, dcb_ref = (du_ref.at[0], du_ref.at[1], dw_ref.at[0], dw_ref.at[1],
                                                                  dc_ref.at[0], dc_ref.at[1])

        def ext(x_ref, xp_ref, xn_ref):
            return jnp.concatenate([jnp.where(i == 0, 0.0, xp_ref[...]), x_ref[...], xn_ref[...]], axis=0)

        ae, be = ext(a_ref, ap_ref, an_ref), ext(b_ref, bp_ref, bn_ref)
        as_ = [_down(ae, FFN_CONV - 1 - k) for k in range(FFN_CONV)]
        bs_ = [_down(be, FFN_CONV - 1 - k) for k in range(FFN_CONV)]
        ua = sum(as_[k] * wa_ref[pl.ds(k, 1), :] for k in range(FFN_CONV)) + ca_ref[...]
        ub = sum(bs_[k] * wb_ref[pl.ds(k, 1), :] for k in range(FFN_CONV)) + cb_ref[...]
        de = jnp.concatenate([jnp.zeros((8, tc), F32), d_ref[...].astype(F32),
                              jnp.where(i == ni - 1, 0.0, dn_ref[...].astype(F32))], axis=0)
        s = _sig(ua)
        dua = de * ub * _dsilu(ua, s)
        dub = de * ua * s
        dua_ref[...] = sum(_up(dua, FFN_CONV - 1 - k) * wa_ref[pl.ds(k, 1), :] for k in range(FFN_CONV))[8:8 + r].astype(dua_ref.dtype)
        dub_ref[...] = sum(_up(dub, FFN_CONV - 1 - k) * wb_ref[pl.ds(k, 1), :] for k in range(FFN_CONV))[8:8 + r].astype(dub_ref.dtype)

        @pl.when(i == 0)
        def _():
            dw_ref[...] = jnp.zeros_like(dw_ref)
            dc_ref[...] = jnp.zeros_like(dc_ref)

        for k in range(FFN_CONV):
            dwa_ref[pl.ds(k, 1), :] += jnp.sum((dua * as_[k])[8:8 + r], axis=0, keepdims=True)
            dwb_ref[pl.ds(k, 1), :] += jnp.sum((dub * bs_[k])[8:8 + r], axis=0, keepdims=True)
        dca_ref[...] += jnp.sum(dua[8:8 + r], axis=0, keepdims=True)
        dcb_ref[...] += jnp.sum(dub[8:8 + r], axis=0, keepdims=True)

    ta, pa, na = _halo(t, r, 8, tc, lambda j: j)
    tb, pb, nb = _halo(t, r, 8, tc, lambda j: j + nj)
    wa = pl.BlockSpec((FFN_CONV, tc), lambda j, i: (0, j))
    wb = pl.BlockSpec((FFN_CONV, tc), lambda j, i: (0, j + nj))
    ca = pl.BlockSpec((1, tc), lambda j, i: (0, j))
    cbs = pl.BlockSpec((1, tc), lambda j, i: (0, j + nj))
    return pl.pallas_call(
        body, name=name, grid=(nj, ni),
        in_specs=[ta, pa, na, tb, pb, nb, wa, wb, ca, cbs, ta, na],
        out_specs=[pl.BlockSpec((2, r, tc), lambda j, i: (0, i, j)), pl.BlockSpec((2, FFN_CONV, tc), lambda j, i: (0, 0, j)),
                   pl.BlockSpec((2, 1, tc), lambda j, i: (0, 0, j))],
        out_shape=[jax.ShapeDtypeStruct((2, t, f), BF16), jax.ShapeDtypeStruct((2, FFN_CONV, f), F32),
                   jax.ShapeDtypeStruct((2, 1, f), F32)],
        compiler_params=_cp(("parallel", "arbitrary")),
    )(up, up, up, up, up, up, cw, cw, cb, cb, dact, dact)


def _pool_pick(g, vals):
    return jnp.where(g == 0, vals[0], jnp.where(g == 1, vals[1], jnp.where(g == 2, vals[2], vals[3])))


def _pool_pre(xe, g, t0):
    s1 = xe + _down(xe, 1)
    s2 = s1 + _down(s1, 2)
    s3 = s2 + _down(s2, 4)
    s4 = s3 + _down(s3, 8)
    r = xe.shape[0] - 16
    pos = (t0 + lax.broadcasted_iota(jnp.int32, (r, 1), 0)).astype(F32)
    cnt = jnp.minimum(pos + 1.0, _pool_pick(g, (2.0, 4.0, 8.0, 16.0)))
    return _pool_pick(g, (s1, s2, s3, s4))[16:] / cnt - xe[16:]


def _pool_fwd(p, pool_w, scale, name):
    t = p.shape[0]
    r = _tile(t, (1024, 256, 128))

    def body(x_ref, xp_ref, w_ref, sc_ref, o_ref):
        g, i = pl.program_id(0), pl.program_id(1)
        xe = jnp.concatenate([jnp.where(i == 0, 0.0, xp_ref[...]), x_ref[...]], axis=0)
        pre = _pool_pre(xe, g, i * r)
        o_ref[...] = jnp.dot(pre, w_ref[0], preferred_element_type=F32) * sc_ref[...]

    tile, prev, _ = _halo(t, r, 16, 128, lambda j: CB_POOL + j)
    return pl.pallas_call(
        body, name=name, grid=(POOL_GROUPS, t // r),
        in_specs=[tile, prev, pl.BlockSpec((1, 128, 128), lambda j, i: (j, 0, 0)), pl.BlockSpec((1, 128), lambda j, i: (0, j))],
        out_specs=pl.BlockSpec((r, 128), lambda j, i: (i, j)),
        out_shape=jax.ShapeDtypeStruct((t, POOL_W), F32), compiler_params=_cp(("parallel", "parallel")),
    )(p, p, pool_w, scale)


def _pool_bwd(p, pool_w, scale, dycat, name):
    t = p.shape[0]
    r = _tile(t, (1024, 256, 128))
    ni = t // r

    def body(x_ref, xp_ref, w_ref, sc_ref, d_ref, dn_ref, dx_ref, dw_ref, dsc_ref):
        g, i = pl.program_id(0), pl.program_id(1)
        xe = jnp.concatenate([jnp.where(i == 0, 0.0, xp_ref[...]), x_ref[...]], axis=0)
        pre = _pool_pre(xe, g, i * r)
        w = w_ref[0]
        dy = d_ref[...]
        dye = jnp.concatenate([dy, jnp.where(i == ni - 1, 0.0, dn_ref[...])], axis=0)
        dpre = lax.dot_general(dye * sc_ref[...], w, _DN["nt"], preferred_element_type=F32)
        pos = (i * r + lax.broadcasted_iota(jnp.int32, (r + 16, 1), 0)).astype(F32)
        dm = dpre / jnp.minimum(pos + 1.0, _pool_pick(g, (2.0, 4.0, 8.0, 16.0)))
        a1 = dm + _up(dm, 1)
        a2 = a1 + _up(a1, 2)
        a3 = a2 + _up(a2, 4)
        a4 = a3 + _up(a3, 8)
        dx_ref[...] = (_pool_pick(g, (a1, a2, a3, a4)) - dpre)[:r]

        @pl.when(i == 0)
        def _():
            dw_ref[...] = jnp.zeros_like(dw_ref)
            dsc_ref[...] = jnp.zeros_like(dsc_ref)

        dw_ref[0] += lax.dot_general(pre, dy * sc_ref[...], _DN["tn"], preferred_element_type=F32)
        dsc_ref[...] += jnp.sum(dy * jnp.dot(pre, w, preferred_element_type=F32), axis=0, keepdims=True)

    tile, prev, _ = _halo(t, r, 16, 128, lambda j: CB_POOL + j)
    dtile, _, dnxt = _halo(t, r, 16, 128, lambda j: DN_W // 128 + j)
    wspec = pl.BlockSpec((1, 128, 128), lambda j, i: (j, 0, 0))
    sspec = pl.BlockSpec((1, 128), lambda j, i: (0, j))
    return pl.pallas_call(
        body, name=name, grid=(POOL_GROUPS, ni),
        in_specs=[tile, prev, wspec, sspec, dtile, dnxt],
        out_specs=[pl.BlockSpec((r, 128), lambda j, i: (i, j)), wspec, sspec],
        out_shape=[jax.ShapeDtypeStruct((t, POOL_W), F32), jax.ShapeDtypeStruct((POOL_GROUPS, 128, 128), F32),
                   jax.ShapeDtypeStruct((1, POOL_W), F32)],
        compiler_params=_cp(("parallel", "arbitrary")),
    )(p, p, pool_w, scale, dycat, dycat)


_DNB = {"nn": (((2,), (1,)), ((0,), (0,))), "nt": (((2,), (2,)), ((0,), (0,))), "tn": (((1,), (1,)), ((0,), (0,)))}


def _dot(a, b, mode="nn", precision=None):
    dn = _DNB[mode] if a.ndim == 3 else _DN[mode]
    return lax.dot_general(a, b, dn, precision=precision, preferred_element_type=F32)


@functools.partial(jax.custom_vjp, nondiff_argnums=(2,))
def _bdot(a, b, mode):
    return _dot(a.astype(BF16), b.astype(BF16), mode)


def _bdot_fwd(a, b, mode):
    return _bdot(a, b, mode), (a, b)


def _bdot_bwd(mode, res, g):
    a, b = res
    if mode == "nn":
        return _bdot(g, b, "nt"), _bdot(a, g, "tn")
    if mode == "nt":
        return _bdot(g, b, "nn"), _bdot(g, a, "tn")
    return _bdot(b, g, "nt"), _bdot(a, g, "nn")


_bdot.defvjp(_bdot_fwd, _bdot_bwd)


def _dn_consts():
    c = DN_CHUNK
    ii = lax.broadcasted_iota(jnp.int32, (c, c), 0)
    jj = lax.broadcasted_iota(jnp.int32, (c, c), 1)
    one, zero = jnp.ones((c, c), F32), jnp.zeros((c, c), F32)
    return dict(ltri=jnp.where(ii >= jj, one, zero), utri=jnp.where(ii <= jj, one, zero), ones=one,
                causal=ii >= jj, strict=ii > jj, eye=jnp.where(ii == jj, one, zero))


def _dn_chunk(q, k, v, z, bcol, acol, s_in, alog, dtb, nw, cs):
    c = DN_CHUNK
    hh = q.shape[0]
    per_head = lambda m: jnp.broadcast_to(m, (hh, c, c))
    beta = _sig(bcol)
    xa = acol + dtb
    g = -jnp.exp(alog) * (jnp.maximum(xa, 0.0) + jnp.log(1.0 + jnp.exp(-jnp.abs(xa))))
    gb = jnp.broadcast_to(g, (hh, c, HEAD_DIM))
    gbc = jnp.broadcast_to(g, (hh, c, c))
    gc = _dot(per_head(cs["ltri"]), gb, precision=HIGH)
    gcol = gc[:, :, :c]
    grow = jnp.swapaxes(gcol, 1, 2)
    decay = jnp.exp(jnp.where(cs["causal"], gcol - grow, -1e30))
    kb = k * beta
    vb = v * beta
    nil = -jnp.where(cs["strict"], _bdot(kb, k, "nt") * decay, 0.0)
    inv = cs["eye"] + nil
    powk = nil
    for _ in range(int(math.log2(c)) - 1):
        powk = _bdot(powk, powk, "nn")
        inv = _bdot(inv, cs["eye"] + powk, "nn")
    eg = jnp.exp(gc)
    u = _bdot(inv, vb, "nn")
    w = _bdot(inv, kb * eg, "nn")
    a = _bdot(q, k, "nt") * decay
    v_new = u - _bdot(w, s_in, "nn")
    o = _bdot(q * eg, s_in, "nn") + _bdot(a, v_new, "nn")
    glast = jnp.sum(gb, axis=1, keepdims=True)
    s_out = s_in * jnp.exp(glast) + _bdot(k * jnp.exp(glast - gc), v_new, "tn")
    on = o * lax.rsqrt(jnp.mean(o * o, axis=-1, keepdims=True) + NORM_EPS) * nw
    return on * (z * _sig(z)), s_out


def _lane_pick(x, lane, idx):
    return jnp.sum(jnp.where(lane == idx, x, 0.0), axis=1, keepdims=True)


def _dn_load(q_ref, k_ref, v_ref, z_ref, bd_ref, al_ref, dt_ref, nw_ref, s_in):
    lane = lax.broadcasted_iota(jnp.int32, (1, 128), 1)
    bd, al, dt = bd_ref[...], al_ref[...], dt_ref[...]
    heads = range(DN_HEADS)
    wide = lambda ref: jnp.stack([ref[:, h * 128:(h + 1) * 128] for h in heads], axis=0)
    col = lambda x, off: jnp.stack([_lane_pick(x, lane, off + h) for h in heads], axis=0)
    return (wide(q_ref), wide(k_ref), wide(v_ref), wide(z_ref), col(bd, 0), col(bd, DN_HEADS), s_in,
            col(al, 0), col(dt, 0), nw_ref[...])


def _dn_fwd(qkv, p, alog, dtb, nw, name):
    t = qkv.shape[0]
    c = DN_CHUNK
    n = t // c

    def body(q_ref, k_ref, v_ref, z_ref, bd_ref, al_ref, dt_ref, nw_ref, y_ref, ss_ref, s_scr):
        @pl.when(pl.program_id(0) == 0)
        def _():
            s_scr[...] = jnp.zeros_like(s_scr)

        s_in = s_scr[...]
        y, s_out = _dn_chunk(*_dn_load(q_ref, k_ref, v_ref, z_ref, bd_ref, al_ref, dt_ref, nw_ref, s_in), _dn_consts())
        ss_ref[0] = s_in
        s_scr[...] = s_out
        for h in range(DN_HEADS):
            y_ref[:, h * 128:(h + 1) * 128] = y[h]

    wide = lambda j: pl.BlockSpec((c, DN_W), lambda i: (i, j))
    vec = pl.BlockSpec((1, 128), lambda i: (0, 0))
    return pl.pallas_call(
        body, name=name, grid=(n,),
        in_specs=[wide(0), wide(1), wide(2), wide(3), pl.BlockSpec((c, 128), lambda i: (i, CB_BD)), vec, vec, vec],
        out_specs=[wide(0), pl.BlockSpec((1, DN_HEADS, 128, 128), lambda i: (i, 0, 0, 0))],
        out_shape=[jax.ShapeDtypeStruct((t, DN_W), F32), jax.ShapeDtypeStruct((n, DN_HEADS, 128, 128), F32)],
        scratch_shapes=[pltpu.VMEM((DN_HEADS, 128, 128), F32)],
        compiler_params=_cp(("arbitrary",)),
    )(qkv, qkv, qkv, p, p, alog, dtb, nw)


def _dn_bwd(qkv, p, alog, dtb, nw, states, dycat, name, carry=None):
    t = qkv.shape[0]
    c = DN_CHUNK
    n = t // c
    sums, kinds = carry if carry is not None else ((), ())
    na = len(sums)

    def body(*refs):
        q_ref, k_ref, v_ref, z_ref, bd_ref, al_ref, dt_ref, nw_ref, ss_ref, dy_ref = refs[:10]
        dqkv_ref, dz_ref, dbd_ref, dal_ref, ddt_ref, dnw_ref = refs[10 + na:16 + na]
        ds_scr = refs[16 + 2 * na]
        if na:
            copies = _scatter_copies(refs[10:10 + na], refs[16 + na:16 + 2 * na], kinds, *refs[17 + 2 * na:])

        @pl.when(pl.program_id(0) == 0)
        def _():
            ds_scr[...] = jnp.zeros_like(ds_scr)
            dal_ref[...] = jnp.zeros_like(dal_ref)
            ddt_ref[...] = jnp.zeros_like(ddt_ref)
            dnw_ref[...] = jnp.zeros_like(dnw_ref)
            if na:
                for cp in copies:
                    cp.start()

        lane = lax.broadcasted_iota(jnp.int32, (1, 128), 1)
        args = _dn_load(q_ref, k_ref, v_ref, z_ref, bd_ref, al_ref, dt_ref, nw_ref, ss_ref[0])
        dy = jnp.stack([dy_ref[:, h * 128:(h + 1) * 128] for h in range(DN_HEADS)], axis=0)
        _, vjp = jax.vjp(functools.partial(_dn_chunk, cs=_dn_consts()), *args)
        gq, gk, gv, gz, gb, ga, gs, gal, gdt, gnw = vjp((dy, ds_scr[...]))
        ds_scr[...] = gs
        dbd = jnp.zeros((c, 128), F32)
        dal = jnp.zeros((1, 128), F32)
        ddt = jnp.zeros((1, 128), F32)
        for h in range(DN_HEADS):
            sl = slice(h * 128, (h + 1) * 128)
            dqkv_ref[:, sl] = gq[h]
            dqkv_ref[:, DN_W + h * 128:DN_W + (h + 1) * 128] = gk[h]
            dqkv_ref[:, 2 * DN_W + h * 128:2 * DN_W + (h + 1) * 128] = gv[h]
            dz_ref[:, sl] = gz[h]
            dbd = dbd + jnp.where(lane == h, gb[h], 0.0) + jnp.where(lane == DN_HEADS + h, ga[h], 0.0)
            dal = dal + jnp.where(lane == h, gal[h], 0.0)
            ddt = ddt + jnp.where(lane == h, gdt[h], 0.0)
        dbd_ref[...] = dbd
        dal_ref[...] += dal
        ddt_ref[...] += ddt
        dnw_ref[...] += gnw

        if na:
            @pl.when(pl.program_id(0) == n - 1)
            def _():
                for cp in copies:
                    cp.wait()

    rev = lambda i: n - 1 - i
    wide = lambda j: pl.BlockSpec((c, DN_W), lambda i: (rev(i), j))
    vec = pl.BlockSpec((1, 128), lambda i: (0, 0))
    any_space = pl.BlockSpec(memory_space=pl.ANY)
    return pl.pallas_call(
        body, name=name, grid=(n,),
        in_specs=[wide(0), wide(1), wide(2), wide(3), pl.BlockSpec((c, 128), lambda i: (rev(i), CB_BD)), vec, vec, vec,
                  pl.BlockSpec((1, DN_HEADS, 128, 128), lambda i: (rev(i), 0, 0, 0)), wide(0)] + [any_space] * na,
        out_specs=[pl.BlockSpec((c, 3 * DN_W), lambda i: (rev(i), 0)), wide(0),
                   pl.BlockSpec((c, 128), lambda i: (rev(i), 0)), vec, vec, vec] + [any_space] * na,
        out_shape=[jax.ShapeDtypeStruct((t, 3 * DN_W), F32), jax.ShapeDtypeStruct((t, DN_W), F32),
                   jax.ShapeDtypeStruct((t, 128), F32), jax.ShapeDtypeStruct((1, 128), F32),
                   jax.ShapeDtypeStruct((1, 128), F32), jax.ShapeDtypeStruct((1, 128), F32)] + _scatter_shapes(sums, kinds),
        scratch_shapes=[pltpu.VMEM((DN_HEADS, 128, 128), F32)] + ([pltpu.SemaphoreType.DMA((na, 3))] * 2 if na else []),
        compiler_params=_cp(("arbitrary",)),
    )(qkv, qkv, qkv, p, p, alog, dtb, nw, states, dycat, *sums)


def _rope(x, cosf, sins):
    return x * cosf + pltpu.roll(x, HEAD_DIM // 2, 1) * sins


def _rope_t(d, cosf, sins):
    return d * cosf + pltpu.roll(d * sins, HEAD_DIM // 2, 1)


def _swa_masks():
    b = SWA_BLOCK
    i = lax.broadcasted_iota(jnp.int32, (SWA_GROUP * b, b), 0) & (b - 1)
    j = lax.broadcasted_iota(jnp.int32, (SWA_GROUP * b, b), 1)
    return j > i, j <= i


def _swa_sink_col(sinks_ref, h):
    b = SWA_BLOCK
    r = lax.broadcasted_iota(jnp.int32, (SWA_GROUP * b, 1), 0)
    s = [sinks_ref[h * SWA_GROUP + g] for g in range(SWA_GROUP)]
    return jnp.where(r < b, s[0], jnp.where(r < 2 * b, s[1], s[2]))


def _swa_specs(t, h_first):
    nb = t // SWA_BLOCK

    def at(col, off):
        def imap(h, n):
            return (jnp.clip(n + off, 0, nb - 1), col(h))
        return pl.BlockSpec((SWA_BLOCK, 128), imap)
    return at


def _swa_fwd(p, cosf, sins, sinks, name):
    t = p.shape[0]
    b = SWA_BLOCK
    nb = t // b
    at = _swa_specs(t, None)
    scale = HEAD_DIM ** -0.5

    def body(q0, q1, q2, kp, kc, vp, vc, cc, sc, cp, sp, sinks_ref, o_ref, lse_ref):
        h, n = pl.program_id(0), pl.program_id(1)
        qs = jnp.concatenate([_rope(q[...], cc[...], sc[...]) for q in (q0, q1, q2)], axis=0)
        ks = jnp.concatenate([_rope(kp[...], cp[...], sp[...]), _rope(kc[...], cc[...], sc[...])], axis=0)
        vs = jnp.concatenate([vp[...], vc[...]], axis=0)
        mp, mc = _swa_masks()
        mask = jnp.concatenate([mp & (n > 0), mc], axis=1)
        s = jnp.where(mask, _dot(qs, ks, "nt") * scale, -1e30)
        sink = _swa_sink_col(sinks_ref, h)
        m = jnp.maximum(jnp.max(s, axis=1, keepdims=True), sink)
        e = jnp.exp(s - m)
        l = jnp.sum(e, axis=1, keepdims=True) + jnp.exp(sink - m)
        o = _dot(e, vs) / l
        lse = m + jnp.log(l)
        lane = lax.broadcasted_iota(jnp.int32, (1, 128), 1)
        tile = jnp.zeros((b, 128), F32)
        for g in range(SWA_GROUP):
            o_ref[:, g * 128:(g + 1) * 128] = o[g * b:(g + 1) * b]
            tile = tile + jnp.where(lane == g, lse[g * b:(g + 1) * b], 0.0)
        lse_ref[0] = tile

    qcol = lambda g: (lambda h: CB_SQ + h * SWA_GROUP + g)
    kcol, vcol, one = (lambda h: CB_SK + h), (lambda h: CB_SV + h), (lambda h: 0)
    in_specs = [at(qcol(0), 0), at(qcol(1), 0), at(qcol(2), 0), at(kcol, -1), at(kcol, 0), at(vcol, -1), at(vcol, 0),
                at(one, 0), at(one, 0), at(one, -1), at(one, -1), pl.BlockSpec(memory_space=pltpu.SMEM)]
    return pl.pallas_call(
        body, name=name, grid=(SWA_KV_HEADS, nb), in_specs=in_specs,
        out_specs=[pl.BlockSpec((b, SWA_GROUP * 128), lambda h, n: (n, h)), pl.BlockSpec((1, b, 128), lambda h, n: (h, n, 0))],
        out_shape=[jax.ShapeDtypeStruct((t, SWA_W), F32), jax.ShapeDtypeStruct((SWA_KV_HEADS, t, 128), F32)],
        compiler_params=_cp(("parallel", "parallel")),
    )(p, p, p, p, p, p, p, cosf, sins, cosf, sins, sinks)


def _swa_bwd(p, cosf, sins, sinks, o, lse, dycat, name):
    t = p.shape[0]
    b = SWA_BLOCK
    nb = t // b
    at = _swa_specs(t, None)
    scale = HEAD_DIM ** -0.5
    gb = SWA_GROUP * b

    def body(q0, q1, q2, r0, r1, r2, kp, kc, vp, vc, cc, sc, cp, sp, cn, sn, d0, d1, d2, e0, e1, e2,
             oc_ref, on_ref, lc_ref, ln_ref, sinks_ref, dq_ref, dk_ref, dv_ref, dsk_ref):
        h, n = pl.program_id(0), pl.program_id(1)
        lane = lax.broadcasted_iota(jnp.int32, (1, 128), 1)
        stack = lambda refs: jnp.concatenate([x[...] for x in refs], axis=0)
        q_c = jnp.concatenate([_rope(q[...], cc[...], sc[...]) for q in (q0, q1, q2)], axis=0)
        q_n = jnp.concatenate([_rope(q[...], cn[...], sn[...]) for q in (r0, r1, r2)], axis=0)
        k_p = _rope(kp[...], cp[...], sp[...])
        k_c = _rope(kc[...], cc[...], sc[...])
        do_c, do_n = stack((d0, d1, d2)), stack((e0, e1, e2))
        o_c = jnp.concatenate([oc_ref[:, g * 128:(g + 1) * 128] for g in range(SWA_GROUP)], axis=0)
        o_n = jnp.concatenate([on_ref[:, g * 128:(g + 1) * 128] for g in range(SWA_GROUP)], axis=0)
        lse_c = jnp.concatenate([_lane_pick(lc_ref[0], lane, g) for g in range(SWA_GROUP)], axis=0)
        lse_n = jnp.concatenate([_lane_pick(ln_ref[0], lane, g) for g in range(SWA_GROUP)], axis=0)
        dl_c = jnp.sum(do_c * o_c, axis=1, keepdims=True)
        dl_n = jnp.sum(do_n * o_n, axis=1, keepdims=True)
        mp, mc = _swa_masks()

        def pair(qr, kr, v, do, lse_, dl, mask):
            s = _dot(qr, kr, "nt") * scale
            pr = jnp.where(mask, jnp.exp(s - lse_), 0.0)
            ds = pr * (_dot(do, v, "nt") - dl) * scale
            return _dot(ds, kr), _dot(ds, qr, "tn"), _dot(pr, do, "tn")

        dq_a, _, _ = pair(q_c, k_p, vp[...], do_c, lse_c, dl_c, mp & (n > 0))
        dq_b, dk_b, dv_b = pair(q_c, k_c, vc[...], do_c, lse_c, dl_c, mc)
        _, dk_n, dv_n = pair(q_n, k_c, vc[...], do_n, lse_n, dl_n, mp & (n < nb - 1))
        dq = dq_a + dq_b
        for g in range(SWA_GROUP):
            dq_ref[:, g * 128:(g + 1) * 128] = _rope_t(dq[g * b:(g + 1) * b], cc[...], sc[...])
        dk_ref[...] = _rope_t(dk_b + dk_n, cc[...], sc[...])
        dv_ref[...] = dv_b + dv_n

        @pl.when(n == 0)
        def _():
            dsk_ref[...] = jnp.zeros_like(dsk_ref)

        w = -jnp.exp(_swa_sink_col(sinks_ref, h) - lse_c) * dl_c
        acc = jnp.zeros((1, 128), F32)
        for g in range(SWA_GROUP):
            acc = acc + jnp.where(lane == g, jnp.sum(w[g * b:(g + 1) * b], axis=0, keepdims=True), 0.0)
        dsk_ref[0] += jnp.broadcast_to(acc, (8, 128))

    qcol = lambda g: (lambda h: CB_SQ + h * SWA_GROUP + g)
    dcol = lambda g: (lambda h: (DN_W + POOL_W) // 128 + h * SWA_GROUP + g)
    kcol, vcol, one = (lambda h: CB_SK + h), (lambda h: CB_SV + h), (lambda h: 0)
    wide = lambda off: pl.BlockSpec((b, SWA_GROUP * 128), lambda h, n: (jnp.clip(n + off, 0, nb - 1), h))
    lspec = lambda off: pl.BlockSpec((1, b, 128), lambda h, n: (h, jnp.clip(n + off, 0, nb - 1), 0))
    in_specs = ([at(qcol(g), 0) for g in range(3)] + [at(qcol(g), 1) for g in range(3)]
                + [at(kcol, -1), at(kcol, 0), at(vcol, -1), at(vcol, 0)]
                + [at(one, 0), at(one, 0), at(one, -1), at(one, -1), at(one, 1), at(one, 1)]
                + [at(dcol(g), 0) for g in range(3)] + [at(dcol(g), 1) for g in range(3)]
                + [wide(0), wide(1), lspec(0), lspec(1), pl.BlockSpec(memory_space=pltpu.SMEM)])
    kv_out = pl.BlockSpec((b, 128), lambda h, n: (n, h))
    return pl.pallas_call(
        body, name=name, grid=(SWA_KV_HEADS, nb), in_specs=in_specs,
        out_specs=[wide(0), kv_out, kv_out, pl.BlockSpec((1, 8, 128), lambda h, n: (h, 0, 0))],
        out_shape=[jax.ShapeDtypeStruct((t, SWA_W), F32), jax.ShapeDtypeStruct((t, SWA_KV_W), F32),
                   jax.ShapeDtypeStruct((t, SWA_KV_W), F32), jax.ShapeDtypeStruct((SWA_KV_HEADS, 8, 128), F32)],
        compiler_params=_cp(("parallel", "arbitrary")),
    )(*([p] * 10), cosf, sins, cosf, sins, cosf, sins, *([dycat] * 6), o, o, lse, lse, sinks)


def _adam_math(w, g, m, v):
    m = ADAM_B1 * m + (1.0 - ADAM_B1) * g
    v = ADAM_B2 * v + (1.0 - ADAM_B2) * (g * g)
    m_hat = m / (1.0 - ADAM_B1 ** ADAM_STEP)
    v_hat = v / (1.0 - ADAM_B2 ** ADAM_STEP)
    return -ADAM_LR * (m_hat / (jnp.sqrt(v_hat) + ADAM_EPS) + ADAM_WD * w), m, v


def _adamw(w, g, m, v, name):
    shape = w.shape
    cols = shape[-1]
    rows = math.prod(shape[:-1])
    flat = lambda a: a.reshape(rows, cols)
    r = rows
    for cand in (512, 256, 128, 64, 32, 16, 8):
        if rows % cand == 0 and cand * cols * 4 <= (1 << 20):
            r = cand
            break

    def body(w_ref, g_ref, m_ref, v_ref, d_ref, nm_ref, nv_ref):
        d_ref[...], nm_ref[...], nv_ref[...] = _adam_math(w_ref[...], g_ref[...], m_ref[...], v_ref[...])

    spec = pl.BlockSpec((r, cols), lambda i: (i, 0))
    outs = pl.pallas_call(
        body, name=name, grid=(rows // r,), in_specs=[spec] * 4, out_specs=[spec] * 3,
        out_shape=[jax.ShapeDtypeStruct((rows, cols), F32)] * 3, compiler_params=_cp(("parallel",)),
    )(flat(w), flat(g), flat(m), flat(v))
    return tuple(o.reshape(shape) for o in outs)


BIG = ("w_in", "w_out", "ffn_w_up", "ffn_w_down")
CONV = ("dn_conv_w", "ffn_conv_w")
KIND = {"w_in": "col", "w_out": "row", "ffn_w_up": "col", "ffn_w_down": "row"}
SMALL = ("norm_mix_pre", "dn_a_log", "dn_dt_bias", "dn_norm_w", "pool_w", "pool_scale", "swa_sinks",
         "norm_mix_post", "norm_ffn_pre", "ffn_conv_b", "norm_ffn_post")
WEIGHTS = ("norm_mix_pre", "w_in", "dn_conv_w", "dn_a_log", "dn_dt_bias", "dn_norm_w", "pool_w", "pool_scale",
           "swa_sinks", "w_out", "norm_mix_post", "norm_ffn_pre", "ffn_w_up", "ffn_conv_w", "ffn_conv_b",
           "ffn_w_down", "norm_ffn_post")


def _pad_in(w):
    z = lambda n: jnp.zeros(w.shape[:-1] + (n,), w.dtype)
    return jnp.concatenate([w[..., :GATE_END], z(CB_POOL * 128 - GATE_END), w[..., GATE_END:],
                            z(IN_PAD - CB_POOL * 128 - (IN_TRUE - GATE_END))], axis=-1)


def _unpad_in(g):
    return jnp.concatenate([g[..., :GATE_END], g[..., CB_POOL * 128:CB_POOL * 128 + IN_TRUE - GATE_END]], axis=-1)


IN_SHARD = IN_TRUE // 4
IN_SHARD_PAD = -(-IN_SHARD // 128) * 128


def _chip_cols_to_true(w):
    by_chip = w.reshape(w.shape[:-1] + (4, IN_SHARD_PAD))[..., :IN_SHARD]
    return by_chip.reshape(w.shape[:-1] + (IN_TRUE,))


def _true_to_chip_cols(g):
    by_chip = g.reshape(g.shape[:-1] + (4, IN_SHARD))
    by_chip = jnp.pad(by_chip, [(0, 0)] * (by_chip.ndim - 1) + [(0, IN_SHARD_PAD - IN_SHARD)])
    return by_chip.reshape(g.shape[:-1] + (4 * IN_SHARD_PAD,))


def _lanes(v):
    return jnp.zeros((1, 128), F32).at[0, :v.shape[0]].set(v)


def _rope_tables(positions):
    inv_freq = 1.0 / (ROPE_THETA ** (jnp.arange(0, HEAD_DIM, 2, dtype=F32) / HEAD_DIM))
    ang = positions.astype(F32)[:, None] * inv_freq
    cos, sin = jnp.cos(ang), jnp.sin(ang)
    return jnp.concatenate([cos, cos], axis=-1), jnp.concatenate([-sin, sin], axis=-1)


class _GradReduce:
    def __init__(self, place, shard_shapes):
        self.place = place
        self.out = {k: lax.empty(shard_shapes[k], F32) for k in BIG}
        self.pending = []

    def submit(self, l, dw):
        def parts(k, g):
            if KIND[k] == "row":
                return g.reshape(4, -1, g.shape[1])
            return (_true_to_chip_cols(_unpad_in(g)) if k == "w_in" else g)[None]

        names = [k for k in BIG if k in dw]
        tag = f"l{l}_" + "_".join(names)
        mine = [parts(k, dw[k]) for k in names]
        got = _swap_sibling(mine, tag + "_to_sibling")
        self.pending += [(l, k, _chip_sum(a, b, self.place, f"l{l}_chip_sum_{k}")) for k, a, b in zip(names, mine, got)]

    def take(self, names):
        entries = [e for e in self.pending if e[1] in names]
        if not entries:
            return None, None
        self.pending = [e for e in self.pending if e[1] not in names]
        return entries, ([s for _, _, s in entries], [KIND[k] for _, k, _ in entries])

    def arrived(self, entries, got):
        for (l, k, own), g in zip(entries, got):
            self.out[k] = _owner_sum(own, g, KIND[k], self.place, (self.out[k], l), f"l{l}_owner_sum_{k}")

    def finish(self):
        entries, (sums, kinds) = self.take(BIG)
        self.arrived(entries, _scatter_chips(sums, kinds, "last_grads_to_owner"))
        return dict(zip(BIG, _join_halves([self.out[k] for k in BIG], "grads_join")))


class _LayerWeights:
    def __init__(self, layers):
        self.layers = layers

    def layer(self, l):
        return self.layers[l]

    def carry(self, l, k):
        return None


class _WeightGather(_LayerWeights):
    def __init__(self, shards, place):
        depth = shards["w_out"].shape[0]
        self.kinds = [KIND[k] for k in BIG]
        self.raw = [{k: _spread_shard(shards[k], l, KIND[k], place, BF16, f"l{l}_cast_{k}") for k in BIG} for l in range(depth)]
        first = _gather_ici([self.raw[0][k] for k in BIG], self.kinds, "l0_gather")
        self.layers = {0: self._passed(0, first)}

    def _passed(self, l, arrs):
        full = dict(zip(BIG, _gather_pass(arrs, self.kinds, f"l{l}_gather_pass")))
        full["w_in"] = _pad_in(_chip_cols_to_true(full["w_in"]))
        return full

    def carry(self, l, k):
        return (self.raw[l + 1][k], KIND[k]) if l + 1 < len(self.raw) else None

    def carried(self, l, landed):
        self.layers[l + 1] = self._passed(l + 1, [landed[k] for k in BIG])


def _local_step(x, positions, target, w, mats, reduce=None):
    depth = w["norm_mix_pre"].shape[0]
    t = x.shape[0]
    cosf, sins = _rope_tables(positions)
    saved = []
    for l in range(depth):
        nm = f"l{l}_"
        n1, n2, n3, n4 = (w[k][l][None] for k in ("norm_mix_pre", "norm_mix_post", "norm_ffn_pre", "norm_ffn_post"))
        alog, dtb, dnw = _lanes(w["dn_a_log"][l]), _lanes(w["dn_dt_bias"][l]), w["dn_norm_w"][l][None]
        psc, cb = w["pool_scale"][l][None], w["ffn_conv_b"][l][None]
        big = mats.layer(l)
        landed = {}

        def project(a, k, name):
            riding = mats.carry(l, k)
            if riding is None:
                return _mm(a, big[k], "nn", F32, name)
            out, landed[k] = _mm(a, big[k], "nn", F32, name, carry=riding)
            return out

        h = _norm_fwd(x, n1, nm + "norm1")
        p = project(h, "w_in", nm + "in_proj")
        qkv = _dn_pre_fwd(p, w["dn_conv_w"][l], nm + "dn_pre")
        y_dn, st = _dn_fwd(qkv, p, alog, dtb, dnw, nm + "dn")
        y_pool = _pool_fwd(p, w["pool_w"][l], psc, nm + "pool")
        y_swa, lse = _swa_fwd(p, cosf, sins, w["swa_sinks"][l], nm + "swa")
        ycat = jnp.concatenate([y_dn, y_pool, y_swa], axis=1).astype(BF16)
        mix = project(ycat, "w_out", nm + "out_proj")
        x1 = _resnorm_fwd(x, mix, n2, nm + "res1")
        h2 = _norm_fwd(x1, n3, nm + "norm3")
        up = project(h2, "ffn_w_up", nm + "ffn_up")
        act = _ffn_act_fwd(up, w["ffn_conv_w"][l], cb, nm + "ffn_act")
        f = project(act, "ffn_w_down", nm + "ffn_down")
        if landed:
            mats.carried(l, landed)
        x2 = _resnorm_fwd(x1, f, n4, nm + "res2")
        saved.append(dict(x=x, h=h, p=p, qkv=qkv, st=st, y_swa=y_swa, lse=lse, ycat=ycat, mix=mix, x1=x1, h2=h2,
                          up=up, act=act, f=f, n=(n1, n2, n3, n4), alog=alog, dtb=dtb, dnw=dnw, psc=psc, cb=cb))
        x = x2
    loss, dx = _loss_head(x, target, "loss_head")
    grads = {k: [None] * depth for k in WEIGHTS}
    held = None
    for l in reversed(range(depth)):
        nm, s = f"l{l}_b_", saved[l]
        n1, n2, n3, n4 = s["n"]
        big = mats.layer(l)
        df, g4 = _norm_bwd(s["f"], n4, dx, None, BF16, nm + "res2")
        dact = _mm(df, big["ffn_w_down"], "nt", F32, nm + "ffn_down_dx")
        grads["ffn_w_down"][l] = _mm(s["act"], df, "tn", BF16, nm + "ffn_down_dw")
        dup, dcw, dcb = _ffn_act_bwd(s["up"], w["ffn_conv_w"][l], s["cb"], dact, nm + "ffn_act")
        grads["ffn_conv_w"][l] = jnp.concatenate([dcw[0], dcw[1]], axis=1)
        grads["ffn_conv_b"][l] = jnp.concatenate([dcb[0], dcb[1]], axis=1)[0]
        grads["ffn_w_up"][l] = _mm(s["h2"], dup, "tn", BF16, nm + "ffn_up_dw", b_pick="split")
        dh2 = _mm(dup, big["ffn_w_up"], "nt", BF16, nm + "ffn_up_dx", a_pick="split")
        dx1, g3 = _norm_bwd(s["x1"], n3, dh2, dx, F32, nm + "norm3")
        dmix, g2 = _norm_bwd(s["mix"], n2, dx1, None, BF16, nm + "res1")
        grads["w_out"][l] = _mm(s["ycat"], dmix, "tn", BF16, nm + "out_proj_dw")
        dycat = _mm(dmix, big["w_out"], "nt", F32, nm + "out_proj_dx")
        riders = (lambda names: reduce.take(names)) if reduce is not None else (lambda names: (None, None))
        if reduce is not None:
            reduce.submit(l, {k: grads[k][l] for k in ("ffn_w_down", "ffn_w_up", "w_out")})
        entries, riding = riders(("ffn_w_down", "ffn_w_up"))
        res = _dn_bwd(s["qkv"], s["p"], s["alog"], s["dtb"], s["dnw"], s["st"], dycat, nm + "dn", carry=riding)
        dqkv, dz, dbd, gal, gdt, gnw = res[:6]
        if entries:
            reduce.arrived(entries, res[6:])
        dpq, gconv = _dn_pre_bwd(s["p"], w["dn_conv_w"][l], dqkv, nm + "dn_pre")
        dpool, gpw, gpsc = _pool_bwd(s["p"], w["pool_w"][l], s["psc"], dycat, nm + "pool")
        dsq, dsk, dsv, gsk = _swa_bwd(s["p"], cosf, sins, w["swa_sinks"][l], s["y_swa"], s["lse"], dycat, nm + "swa")
        dp = jnp.concatenate([dpq, dz, dbd, dpool, dsq, dsk, dsv, jnp.zeros((t, 128), F32)], axis=1).astype(BF16)
        if held is not None:
            reduce.submit(*held)
        entries, riding = riders(("w_in",))
        res = _mm(s["h"], dp, "tn", BF16, nm + "in_proj_dw", scatter=riding)
        grads["w_in"][l] = res[0] if entries else res
        if entries:
            reduce.arrived(entries, res[1:])
        entries, riding = riders(("w_out",))
        res = _mm(dp, big["w_in"], "nt", BF16, nm + "in_proj_dx", scatter=riding)
        dh = res[0] if entries else res
        if entries:
            reduce.arrived(entries, res[1:])
        dx, g1 = _norm_bwd(s["x"], n1, dh, dx1, F32, nm + "norm1")
        held = (l, {"w_in": grads["w_in"][l]}) if reduce is not None else None
        grads["norm_mix_pre"][l], grads["norm_mix_post"][l] = g1[0], g2[0]
        grads["norm_ffn_pre"][l], grads["norm_ffn_post"][l] = g3[0], g4[0]
        grads["dn_conv_w"][l] = gconv
        grads["dn_a_log"][l], grads["dn_dt_bias"][l], grads["dn_norm_w"][l] = gal[0, :DN_HEADS], gdt[0, :DN_HEADS], gnw[0]
        grads["pool_w"][l], grads["pool_scale"][l] = gpw, gpsc[0]
        grads["swa_sinks"][l] = gsk[:, 0, :SWA_GROUP].reshape(SWA_HEADS)
    if held is not None:
        reduce.submit(*held)
    return loss, dx, grads


def _flat2(a):
    return a.reshape(math.prod(a.shape[:-1]), a.shape[-1])


def _ew_rows(rows, cols, n_arrays):
    for cand in (512, 256, 128, 64, 32, 16):
        if rows % cand == 0 and cand * cols * 4 * n_arrays <= (8 << 20):
            return cand
    return rows


def _spread_shard(a, layer, kind, place, dtype, name):
    _, rows, cols = a.shape
    r = _ew_rows(rows, cols, 2)
    nb = rows // r

    def body(s_ref, a_ref, o_ref):
        o_ref[...] = a_ref[...].astype(o_ref.dtype)

    if kind == "row":
        out_spec = pl.BlockSpec((r, cols), lambda i, s: (s[0] * nb + i, 0))
        out_shape = (4 * rows, cols)
    else:
        out_spec = pl.BlockSpec((r, cols), lambda i, s: (i, s[0]))
        out_shape = (rows, 4 * cols)
    return pl.pallas_call(
        body, name=name,
        grid_spec=pltpu.PrefetchScalarGridSpec(
            num_scalar_prefetch=1, grid=(nb,),
            in_specs=[pl.BlockSpec((None, r, cols), lambda i, s: (layer, i, 0))], out_specs=out_spec),
        out_shape=jax.ShapeDtypeStruct(out_shape, dtype), compiler_params=_cp(("parallel",)),
    )(place, a)


def _chip_sum(mine, sib, place, name):
    parts, rows, cols = sib.shape
    r = _ew_rows(rows, cols, 3)
    nb = rows // r

    def body(s_ref, a_ref, b_ref, o_ref):
        o_ref[...] = (a_ref[...].astype(F32) + b_ref[...].astype(F32)).astype(o_ref.dtype)

    spec = pl.BlockSpec((None, r, cols), lambda j, i, s: (j, i, 0))
    return pl.pallas_call(
        body, name=name,
        grid_spec=pltpu.PrefetchScalarGridSpec(
            num_scalar_prefetch=1, grid=(parts, nb),
            in_specs=[pl.BlockSpec((None, r, cols), lambda j, i, s: (j, s[1] * nb + i, 0)), spec], out_specs=spec),
        out_shape=jax.ShapeDtypeStruct(sib.shape, BF16), compiler_params=_cp(("parallel", "parallel")),
    )(place, mine, sib)


def _sum_slots(a, name):
    s = a.shape[0]
    a3 = a.reshape(s, math.prod(a.shape[1:-1]), a.shape[-1])
    _, rows, cols = a3.shape
    r = _ew_rows(rows, cols, s + 1)

    def body(a_ref, o_ref):
        acc = a_ref[0].astype(F32)
        for k in range(1, s):
            acc = acc + a_ref[k].astype(F32)
        o_ref[...] = acc

    return pl.pallas_call(body, name=name, grid=(rows // r,),
                          in_specs=[pl.BlockSpec((s, r, cols), lambda i: (0, i, 0))],
                          out_specs=pl.BlockSpec((r, cols), lambda i: (i, 0)),
                          out_shape=jax.ShapeDtypeStruct((rows, cols), F32), compiler_params=_cp(("parallel",)),
                          )(a3).reshape(a.shape[1:])


def _owner_sum(own, got, kind, place, into, name):
    buf, slab = into
    _, rows, cols = got.shape
    r = _ew_rows(rows, cols, 6)
    nb = rows // r

    def body(s_ref, own_ref, got_ref, buf_ref, o_ref):
        acc = own_ref[...].astype(F32)
        for k in range(3):
            acc = acc + got_ref[k].astype(F32)
        o_ref[...] = acc

    if kind == "row":
        own_spec = pl.BlockSpec((None, r, cols), lambda i, s: (s[0], i, 0))
    else:
        own_spec = pl.BlockSpec((None, r, cols), lambda i, s: (0, i, s[0]))
    return pl.pallas_call(
        body, name=name,
        grid_spec=pltpu.PrefetchScalarGridSpec(
            num_scalar_prefetch=1, grid=(nb,),
            in_specs=[own_spec, pl.BlockSpec((3, r, cols), lambda i, s: (0, i, 0)), pl.BlockSpec(memory_space=pl.ANY)],
            out_specs=pl.BlockSpec((None, r, cols), lambda i, s: (slab, s[1] * nb + i, 0))),
        out_shape=jax.ShapeDtypeStruct(buf.shape, buf.dtype), input_output_aliases={3: 0},
        compiler_params=_cp(("parallel",)),
    )(place, own, got, buf)


MESH = pl.DeviceIdType.MESH
ANY = pl.BlockSpec(memory_space=pl.ANY)


def _place():
    x, y, c = lax.axis_index("x"), lax.axis_index("y"), lax.axis_index("c")
    chips = [(1 - x, y), (x, 1 - y), (1 - x, 1 - y)]
    return x, y, c, chips


def _half_part(ref, kind, chip, half):
    if kind == "row":
        h = ref.shape[0] // 8
        return ref.at[pl.ds(pl.multiple_of((2 * chip + half) * h, 16), h), :]
    h, width = ref.shape[0] // 2, ref.shape[1] // 4
    return ref.at[pl.ds(pl.multiple_of(half * h, 16), h), pl.ds(pl.multiple_of(chip * width, 128), width)]


def _gather_copies(w_ref, kind, send, recv):
    x, y, c, chips = _place()
    mine = _half_part(w_ref, kind, 2 * x + y, c)
    return [pltpu.make_async_remote_copy(mine, mine, send.at[j], recv.at[j], device_id=(px, py, c), device_id_type=MESH)
            for j, (px, py) in enumerate(chips)]


def _gather_ici(arrs, kinds, name):
    na = len(arrs)

    def body(*refs):
        outs, send, recv = refs[na:2 * na], refs[2 * na], refs[2 * na + 1]
        cps = [cp for k in range(na) for cp in _gather_copies(outs[k], kinds[k], send.at[k], recv.at[k])]
        for cp in cps:
            cp.start()
        for cp in cps:
            cp.wait()

    return pl.pallas_call(
        body, name=name, in_specs=[ANY] * na, out_specs=[ANY] * na,
        out_shape=[jax.ShapeDtypeStruct(a.shape, a.dtype) for a in arrs],
        input_output_aliases={k: k for k in range(na)},
        scratch_shapes=[pltpu.SemaphoreType.DMA((na, 3))] * 2,
    )(*arrs)


def _gather_pass(arrs, kinds, name):
    na = len(arrs)

    def body(*refs):
        outs, send, recv = refs[na:2 * na], refs[2 * na], refs[2 * na + 1]
        x, y, c, chips = _place()
        cps, arrivals = [], []
        for k in range(na):
            for j, (px, py) in enumerate(chips):
                mine = _half_part(outs[k], kinds[k], 2 * px + py, c)
                theirs = _half_part(outs[k], kinds[k], 2 * px + py, 1 - c)
                cps.append(pltpu.make_async_remote_copy(mine, mine, send.at[k, j], recv.at[k, j],
                                                        device_id=(x, y, 1 - c), device_id_type=MESH))
                arrivals.append(pltpu.make_async_remote_copy(theirs, theirs, send.at[k, j], recv.at[k, j],
                                                             device_id=(x, y, 1 - c), device_id_type=MESH))
        for cp in cps:
            cp.start()
        for cp, arrival in zip(cps, arrivals):
            cp.wait_send()
            arrival.wait_recv()

    return pl.pallas_call(
        body, name=name, in_specs=[ANY] * na, out_specs=[ANY] * na,
        out_shape=[jax.ShapeDtypeStruct(a.shape, a.dtype) for a in arrs],
        input_output_aliases={k: k for k in range(na)},
        scratch_shapes=[pltpu.SemaphoreType.DMA((na, 3))] * 2,
    )(*arrs)


def _swap_sibling(arrs, name):
    na = len(arrs)

    def body(*refs):
        ins, outs, send, recv = refs[:na], refs[na:2 * na], refs[2 * na], refs[2 * na + 1]
        x, y, c, _ = _place()
        cps = []
        for k in range(na):
            h = ins[k].shape[1] // 2
            cps.append(pltpu.make_async_remote_copy(ins[k].at[:, pl.ds(pl.multiple_of((1 - c) * h, 16), h), :], outs[k],
                                                    send.at[k], recv.at[k], device_id=(x, y, 1 - c), device_id_type=MESH))
        for cp in cps:
            cp.start()
        for cp in cps:
            cp.wait()

    return pl.pallas_call(
        body, name=name, in_specs=[ANY] * na, out_specs=[ANY] * na,
        out_shape=[jax.ShapeDtypeStruct((a.shape[0], a.shape[1] // 2, a.shape[2]), a.dtype) for a in arrs],
        scratch_shapes=[pltpu.SemaphoreType.DMA((na,))] * 2,
    )(*arrs)


def _scatter_shapes(sums, kinds):
    return [jax.ShapeDtypeStruct((3, a.shape[1], a.shape[2] if kind == "row" else a.shape[2] // 4), a.dtype)
            for a, kind in zip(sums, kinds)]


def _scatter_copies(srcs, dsts, kinds, send, recv):
    x, y, c, chips = _place()
    cps = []
    for k, (src, dst) in enumerate(zip(srcs, dsts)):
        for j, (px, py) in enumerate(chips):
            chip = 2 * px + py
            if kinds[k] == "row":
                part = src.at[chip]
            else:
                width = src.shape[2] // 4
                part = src.at[0, :, pl.ds(pl.multiple_of(chip * width, 128), width)]
            cps.append(pltpu.make_async_remote_copy(part, dst.at[j], send.at[k, j], recv.at[k, j],
                                                    device_id=(px, py, c), device_id_type=MESH))
    return cps


def _scatter_chips(sums, kinds, name):
    na = len(sums)

    def body(*refs):
        cps = _scatter_copies(refs[:na], refs[na:2 * na], kinds, refs[2 * na], refs[2 * na + 1])
        for cp in cps:
            cp.start()
        for cp in cps:
            cp.wait()

    return pl.pallas_call(
        body, name=name, in_specs=[ANY] * na, out_specs=[ANY] * na, out_shape=_scatter_shapes(sums, kinds),
        scratch_shapes=[pltpu.SemaphoreType.DMA((na, 3))] * 2,
    )(*sums)


def _join_halves(arrs, name):
    na = len(arrs)

    def body(*refs):
        outs, send, recv = refs[na:2 * na], refs[2 * na], refs[2 * na + 1]
        x, y, c, _ = _place()
        halves = [a.shape[1] // 2 for a in arrs]
        mine = [outs[k].at[:, pl.ds(pl.multiple_of(c * h, 8), h), :] for k, h in enumerate(halves)]
        theirs = [outs[k].at[:, pl.ds(pl.multiple_of((1 - c) * h, 8), h), :] for k, h in enumerate(halves)]
        cps = [pltpu.make_async_remote_copy(mine[k], mine[k], send.at[k], recv.at[k],
                                            device_id=(x, y, 1 - c), device_id_type=MESH) for k in range(na)]
        for cp in cps:
            cp.start()
        for k, cp in enumerate(cps):
            cp.wait_send()
            pltpu.make_async_remote_copy(theirs[k], theirs[k], send.at[k], recv.at[k],
                                         device_id=(x, y, 1 - c), device_id_type=MESH).wait_recv()

    return pl.pallas_call(
        body, name=name, in_specs=[ANY] * na, out_specs=[ANY] * na,
        out_shape=[jax.ShapeDtypeStruct(a.shape, a.dtype) for a in arrs],
        input_output_aliases={k: k for k in range(na)},
        scratch_shapes=[pltpu.SemaphoreType.DMA((na,))] * 2,
    )(*arrs)


def _gather_all(a, name):
    def body(a_ref, o_ref, send, recv, local):
        x, y, c, _ = _place()
        me = 4 * x + 2 * y + c
        mine = pltpu.make_async_copy(a_ref, o_ref.at[me], local)
        mine.start()
        cps = []
        for j in range(1, 8):
            peer = (x ^ (j >> 2), y ^ ((j >> 1) & 1), c ^ (j & 1))
            cps.append(pltpu.make_async_remote_copy(a_ref, o_ref.at[me], send.at[j - 1], recv.at[j - 1],
                                                    device_id=peer, device_id_type=MESH))
        for cp in cps:
            cp.start()
        for cp in cps:
            cp.wait()
        mine.wait()

    return pl.pallas_call(
        body, name=name, in_specs=[ANY], out_specs=ANY,
        out_shape=jax.ShapeDtypeStruct((8,) + a.shape, a.dtype),
        scratch_shapes=[pltpu.SemaphoreType.DMA((7,)), pltpu.SemaphoreType.DMA((7,)), pltpu.SemaphoreType.DMA],
    )(a)


def _pack(parts):
    flat = jnp.concatenate([p.reshape(-1) for p in parts])
    n = flat.shape[0]
    rows = -(-n // (PACK_ROWS * 128)) * PACK_ROWS
    return jnp.pad(flat, (0, rows * 128 - n)).reshape(rows, 128)


def _unpack(buf, like):
    flat, out, off = buf.reshape(-1), [], 0
    for p in like:
        out.append(flat[off:off + p.size].reshape(p.shape))
        off += p.size
    return out


def kernel(x, positions, norm_mix_pre, w_in, dn_conv_w, dn_a_log, dn_dt_bias, dn_norm_w, pool_w, pool_scale, swa_sinks, w_out, norm_mix_post, norm_ffn_pre, ffn_w_up, ffn_conv_w, ffn_conv_b, ffn_w_down, norm_ffn_post, loss_target, m_norm_mix_pre, m_w_in, m_dn_conv_w, m_dn_a_log, m_dn_dt_bias, m_dn_norm_w, m_pool_w, m_pool_scale, m_swa_sinks, m_w_out, m_norm_mix_post, m_norm_ffn_pre, m_ffn_w_up, m_ffn_conv_w, m_ffn_conv_b, m_ffn_w_down, m_norm_ffn_post, v_norm_mix_pre, v_w_in, v_dn_conv_w, v_dn_a_log, v_dn_dt_bias, v_dn_norm_w, v_pool_w, v_pool_scale, v_swa_sinks, v_w_out, v_norm_mix_post, v_norm_ffn_pre, v_ffn_w_up, v_ffn_conv_w, v_ffn_conv_b, v_ffn_w_down, v_norm_ffn_post):
    wts = dict(zip(WEIGHTS, (norm_mix_pre, w_in, dn_conv_w, dn_a_log, dn_dt_bias, dn_norm_w, pool_w, pool_scale, swa_sinks,
                             w_out, norm_mix_post, norm_ffn_pre, ffn_w_up, ffn_conv_w, ffn_conv_b, ffn_w_down, norm_ffn_post)))
    mom = dict(zip(WEIGHTS, (m_norm_mix_pre, m_w_in, m_dn_conv_w, m_dn_a_log, m_dn_dt_bias, m_dn_norm_w, m_pool_w, m_pool_scale,
                             m_swa_sinks, m_w_out, m_norm_mix_post, m_norm_ffn_pre, m_ffn_w_up, m_ffn_conv_w, m_ffn_conv_b,
                             m_ffn_w_down, m_norm_ffn_post)))
    var = dict(zip(WEIGHTS, (v_norm_mix_pre, v_w_in, v_dn_conv_w, v_dn_a_log, v_dn_dt_bias, v_dn_norm_w, v_pool_w, v_pool_scale,
                             v_swa_sinks, v_w_out, v_norm_mix_post, v_norm_ffn_pre, v_ffn_w_up, v_ffn_conv_w, v_ffn_conv_b,
                             v_ffn_w_down, v_norm_ffn_post)))
    c = lax.axis_index("c")
    chip = 2 * lax.axis_index("x") + lax.axis_index("y")
    place = jnp.stack([chip, c]).astype(jnp.int32)
    shards = dict(wts, w_in=jnp.pad(w_in, ((0, 0), (0, 0), (0, IN_SHARD_PAD - IN_SHARD))))
    mats = _WeightGather(shards, place)
    w = dict(wts)
    conv_like = [wts[k] for k in CONV]
    conv_all = _gather_all(_pack(conv_like), "gather_conv")
    for i, k in enumerate(CONV):
        w[k] = jnp.concatenate([_unpack(conv_all[2 * j], conv_like)[i] for j in range(4)], axis=2)

    reduce = _GradReduce(place, {k: shards[k].shape for k in BIG})
    loss, dx, grads = _local_step(x[0], positions[0], loss_target[0], w, mats, reduce)
    loss = lax.psum(loss[0, 0], ("x", "y", "c"))
    g_big = reduce.finish()
    g_big["w_in"] = g_big["w_in"][..., :IN_SHARD]

    small_like = [wts[k] for k in SMALL]
    full_like = small_like + [w[k] for k in CONV]
    g_buf = _sum_slots(_gather_all(_pack([jnp.stack(grads[k]) for k in SMALL + CONV]), "gather_small"), "sum_small")
    g_small = dict(zip(SMALL + CONV, _unpack(g_buf, full_like)))
    for k in CONV:
        width = wts[k].shape[2]
        g_small[k] = lax.dynamic_slice_in_dim(g_small[k], chip * width, width, 2)
    pk = lambda d: _pack([d[k] for k in SMALL + CONV])
    upd = _adamw(pk(wts), pk(g_small), pk(mom), pk(var), "adam_small")
    upd_small = [dict(zip(SMALL + CONV, _unpack(b, small_like + conv_like))) for b in upd]

    g_all, d_all, m_all, v_all = {}, {}, {}, {}
    for k in WEIGHTS:
        if k in BIG:
            g_all[k] = g_big[k]
            d_all[k], m_all[k], v_all[k] = _adamw(wts[k], g_big[k], mom[k], var[k], "adam_" + k)
        else:
            g_all[k], d_all[k], m_all[k], v_all[k] = g_small[k], upd_small[0][k], upd_small[1][k], upd_small[2][k]
    return (loss, dx[None], *[g_all[k] for k in WEIGHTS], *[d_all[k] for k in WEIGHTS],
            *[m_all[k] for k in WEIGHTS], *[v_all[k] for k in WEIGHTS])
```

```python
import functools
import math

import jax
import jax.numpy as jnp
from jax import lax
from jax.experimental import pallas as pl
from jax.experimental.pallas import tpu as pltpu

F32 = jnp.float32
BF16 = jnp.bfloat16

HEAD_DIM = 128
DN_HEADS = 6
DN_CONV = 4
DN_CHUNK = 64
POOL_GROUPS = 4
SWA_HEADS = 6
SWA_KV_HEADS = 2
SWA_GROUP = SWA_HEADS // SWA_KV_HEADS
SWA_BLOCK = 128
ROPE_THETA = 10000.0
FFN_CONV = 3
NORM_EPS = 1e-6
DN_W = DN_HEADS * HEAD_DIM
POOL_W = POOL_GROUPS * HEAD_DIM
SWA_W = SWA_HEADS * HEAD_DIM
SWA_KV_W = SWA_KV_HEADS * HEAD_DIM
MIX_W = DN_W + POOL_W + SWA_W
IN_TRUE = 3 * DN_W + DN_W + 2 * DN_HEADS + POOL_W + SWA_W + 2 * SWA_KV_W
GATE_END = 4 * DN_W + 2 * DN_HEADS
CB_Z = 18
CB_BD = 24
CB_POOL = 25
CB_SQ = 29
CB_SK = 35
CB_SV = 37
IN_PAD = 40 * 128
ADAM_LR, ADAM_B1, ADAM_B2, ADAM_EPS, ADAM_WD, ADAM_STEP = 0.001, 0.9, 0.999, 1e-08, 0.01, 10

VMEM_LIMIT = 48 * 1024 * 1024
PACK_ROWS = 512
MM_TK_MAX = 2816
HIGH = lax.Precision.HIGHEST


def _cp(sem):
    return pltpu.CompilerParams(dimension_semantics=sem, vmem_limit_bytes=VMEM_LIMIT)


def _tile(n, prefs):
    for p in prefs:
        if n % p == 0:
            return p
    return n


def _rows(t):
    return _tile(t, (256, 128))


_DN = {"nn": (((1,), (0,)), ((), ())), "nt": (((1,), (1,)), ((), ())), "tn": (((0,), (0,)), ((), ()))}


def _mm_operand(arr, pick, block, idx):
    if pick is None:
        return pl.BlockSpec(block, idx)
    if pick == "split":
        per = arr.shape[2] // block[1]

        def split_idx(i, j, kk):
            r, c = idx(i, j, kk)
            return lax.div(c, per), r, lax.rem(c, per)

        return pl.BlockSpec((None,) + block, split_idx)
    slab = pick[1]
    return pl.BlockSpec((None,) + block, lambda i, j, kk: (slab,) + idx(i, j, kk))


def _mm(a, b, mode, out_dtype, name, a_pick=None, b_pick=None, carry=None, scatter=None):
    def dims(arr, pick):
        r, c = arr.shape[-2:]
        return (r, c * arr.shape[0]) if pick == "split" else (r, c)

    (a0, a1), (b0, b1) = dims(a, a_pick), dims(b, b_pick)
    k, m = (a0, a1) if mode == "tn" else (a1, a0)
    n = b0 if mode == "nt" else b1
    lim = lambda arr, pick, is_last, full: arr.shape[2] if (pick == "split" and is_last) else full
    tm = _tile(lim(a, a_pick, mode == "tn", m), (1024, 512, 256, 128))
    tn = _tile(lim(b, b_pick, mode != "nt", n), (1408, 1280, 1024, 512, 256, 128))
    k_lim = min(lim(a, a_pick, mode != "tn", k), lim(b, b_pick, mode == "nt", k))
    tk = max([d for d in range(128, min(k_lim, MM_TK_MAX) + 1, 128) if k_lim % d == 0], default=k_lim)
    nk = k // tk

    grid = (m // tm, n // tn, nk)

    riding = carry is not None or scatter is not None
    ns = len(scatter[0]) if scatter is not None else 0

    def body(a_ref, b_ref, *rest):
        if carry is not None:
            _, o_ref, w_ref, *scratch = rest
            copies = _gather_copies(w_ref, carry[1], *scratch[-2:])
        elif scatter is not None:
            o_ref, scratch = rest[ns], rest[2 * ns + 1:]
            copies = _scatter_copies(rest[:ns], rest[ns + 1:2 * ns + 1], scatter[1], *scratch[-2:])
        else:
            o_ref, *scratch = rest
        if riding:
            scratch = scratch[:-2]
            step = (pl.program_id(0) * grid[1] + pl.program_id(1)) * grid[2] + pl.program_id(2)

            @pl.when(step == 0)
            def _():
                for cp in copies:
                    cp.start()
        part = lax.dot_general(a_ref[...], b_ref[...], _DN[mode], preferred_element_type=F32)
        if nk == 1:
            o_ref[...] = part.astype(o_ref.dtype)
        else:
            acc_ref, = scratch
            kk = pl.program_id(2)

            @pl.when(kk == 0)
            def _():
                acc_ref[...] = part

            @pl.when(kk > 0)
            def _():
                acc_ref[...] += part

            @pl.when(kk == nk - 1)
            def _():
                o_ref[...] = acc_ref[...].astype(o_ref.dtype)
        if riding:
            @pl.when(step == grid[0] * grid[1] * grid[2] - 1)
            def _():
                for cp in copies:
                    cp.wait()

    if mode == "tn":
        a_spec = _mm_operand(a, a_pick, (tk, tm), lambda i, j, kk: (kk, i))
    else:
        a_spec = _mm_operand(a, a_pick, (tm, tk), lambda i, j, kk: (i, kk))
    if mode == "nt":
        b_spec = _mm_operand(b, b_pick, (tn, tk), lambda i, j, kk: (j, kk))
    else:
        b_spec = _mm_operand(b, b_pick, (tk, tn), lambda i, j, kk: (kk, j))
    scratch = [pltpu.VMEM((tm, tn), F32)] if nk > 1 else []
    out_spec = pl.BlockSpec((tm, tn), lambda i, j, kk: (i, j))
    out_shape = jax.ShapeDtypeStruct((m, n), out_dtype)
    if not riding:
        return pl.pallas_call(
            body, name=name, grid=grid, in_specs=[a_spec, b_spec], out_specs=out_spec, out_shape=out_shape,
            scratch_shapes=scratch, compiler_params=_cp(("parallel", "parallel", "arbitrary")),
        )(a, b)
    any_space = pl.BlockSpec(memory_space=pl.ANY)
    in_order = _cp(("arbitrary", "arbitrary", "arbitrary"))
    if carry is not None:
        return pl.pallas_call(
            body, name=name, grid=grid, in_specs=[a_spec, b_spec, any_space], out_specs=[out_spec, any_space],
            out_shape=[out_shape, jax.ShapeDtypeStruct(carry[0].shape, carry[0].dtype)], input_output_aliases={2: 1},
            scratch_shapes=scratch + [pltpu.SemaphoreType.DMA((3,))] * 2, compiler_params=in_order,
        )(a, b, carry[0])
    return pl.pallas_call(
        body, name=name, grid=grid, in_specs=[a_spec, b_spec] + [any_space] * ns, out_specs=[out_spec] + [any_space] * ns,
        out_shape=[out_shape] + _scatter_shapes(*scatter),
        scratch_shapes=scratch + [pltpu.SemaphoreType.DMA((ns, 3))] * 2, compiler_params=in_order,
    )(a, b, *scatter[0])


def _rms(x, w):
    return x * lax.rsqrt(jnp.mean(x * x, axis=-1, keepdims=True) + NORM_EPS) * w


def _norm_fwd(x, w, name):
    t, d = x.shape
    r = _rows(t)

    def body(x_ref, w_ref, h_ref):
        h_ref[...] = _rms(x_ref[...], w_ref[...]).astype(h_ref.dtype)

    return pl.pallas_call(
        body, name=name, grid=(t // r,),
        in_specs=[pl.BlockSpec((r, d), lambda i: (i, 0)), pl.BlockSpec((1, d), lambda i: (0, 0))],
        out_specs=pl.BlockSpec((r, d), lambda i: (i, 0)),
        out_shape=jax.ShapeDtypeStruct((t, d), BF16), compiler_params=_cp(("parallel",)),
    )(x, w)


def _resnorm_fwd(x, y, w, name):
    t, d = x.shape
    r = _rows(t)

    def body(x_ref, y_ref, w_ref, o_ref):
        o_ref[...] = x_ref[...] + _rms(y_ref[...], w_ref[...])

    return pl.pallas_call(
        body, name=name, grid=(t // r,),
        in_specs=[pl.BlockSpec((r, d), lambda i: (i, 0)), pl.BlockSpec((r, d), lambda i: (i, 0)),
                  pl.BlockSpec((1, d), lambda i: (0, 0))],
        out_specs=pl.BlockSpec((r, d), lambda i: (i, 0)),
        out_shape=jax.ShapeDtypeStruct((t, d), F32), compiler_params=_cp(("parallel",)),
    )(x, y, w)


def _norm_bwd(x, w, dh, add, out_dtype, name):
    t, d = x.shape
    r = _rows(t)
    has_add = add is not None

    def body(*refs):
        if has_add:
            x_ref, w_ref, dh_ref, add_ref, dx_ref, dw_ref = refs
        else:
            x_ref, w_ref, dh_ref, dx_ref, dw_ref = refs
        xv = x_ref[...]
        g = dh_ref[...].astype(F32)
        rs = lax.rsqrt(jnp.mean(xv * xv, axis=-1, keepdims=True) + NORM_EPS)
        xh = xv * rs
        gw = g * w_ref[...]
        dx = rs * (gw - xh * jnp.mean(gw * xh, axis=-1, keepdims=True))
        if has_add:
            dx = dx + add_ref[...]
        dx_ref[...] = dx.astype(dx_ref.dtype)

        @pl.when(pl.program_id(0) == 0)
        def _():
            dw_ref[...] = jnp.zeros_like(dw_ref)

        dw_ref[...] += jnp.sum(g * xh, axis=0, keepdims=True)

    row = pl.BlockSpec((r, d), lambda i: (i, 0))
    vec = pl.BlockSpec((1, d), lambda i: (0, 0))
    ins = [x, w, dh] + ([add] if has_add else [])
    return pl.pallas_call(
        body, name=name, grid=(t // r,),
        in_specs=[row, vec, row] + ([row] if has_add else []),
        out_specs=[row, vec],
        out_shape=[jax.ShapeDtypeStruct((t, d), out_dtype), jax.ShapeDtypeStruct((1, d), F32)],
        compiler_params=_cp(("arbitrary",)),
    )(*ins)


def _loss_head(y, target, name):
    t, d = y.shape
    r = _rows(t)

    def body(y_ref, t_ref, l_ref, g_ref):
        e = y_ref[...] - t_ref[...]
        g_ref[...] = e * (1.0 / d)

        @pl.when(pl.program_id(0) == 0)
        def _():
            l_ref[...] = jnp.zeros_like(l_ref)

        l_ref[...] += jnp.sum(e * e) * (0.5 / d)

    row = pl.BlockSpec((r, d), lambda i: (i, 0))
    return pl.pallas_call(
        body, name=name, grid=(t // r,), in_specs=[row, row],
        out_specs=[pl.BlockSpec((1, 128), lambda i: (0, 0)), row],
        out_shape=[jax.ShapeDtypeStruct((1, 128), F32), jax.ShapeDtypeStruct((t, d), F32)],
        compiler_params=_cp(("arbitrary",)),
    )(y, target)


def _down(x, s):
    return x if s == 0 else pltpu.roll(x, s, 0)


def _up(x, s):
    return x if s == 0 else pltpu.roll(x, x.shape[0] - s, 0)


def _halo(t, r, hh, tc, col):
    q = r // hh
    last = t // hh - 1
    tile = pl.BlockSpec((r, tc), lambda j, i: (i, col(j)))
    prev = pl.BlockSpec((hh, tc), lambda j, i: (jnp.maximum(i * q - 1, 0), col(j)))
    nxt = pl.BlockSpec((hh, tc), lambda j, i: (jnp.minimum((i + 1) * q, last), col(j)))
    return tile, prev, nxt


def _sig(x):
    return 1.0 / (1.0 + jnp.exp(-x))


def _dsilu(x, s):
    return s * (1.0 + x * (1.0 - s))


def _dn_pre_fwd(p, conv_w, name):
    t = p.shape[0]
    r = _rows(t)

    def body(x_ref, xp_ref, w_ref, o_ref):
        j, i = pl.program_id(0), pl.program_id(1)
        xe = jnp.concatenate([jnp.where(i == 0, 0.0, xp_ref[...]), x_ref[...]], axis=0)
        c = sum(_down(xe, DN_CONV - 1 - k) * w_ref[pl.ds(k, 1), :] for k in range(DN_CONV))[8:]
        a = c * _sig(c)
        for h in range(DN_HEADS):
            ah = a[:, h * 128:(h + 1) * 128]
            fac = lax.rsqrt(jnp.sum(ah * ah, axis=-1, keepdims=True) + NORM_EPS)
            o_ref[:, h * 128:(h + 1) * 128] = ah * jnp.where(j == 0, fac * HEAD_DIM ** -0.5, jnp.where(j == 1, fac, 1.0))

    tile, prev, _ = _halo(t, r, 8, DN_W, lambda j: j)
    return pl.pallas_call(
        body, name=name, grid=(3, t // r),
        in_specs=[tile, prev, pl.BlockSpec((DN_CONV, DN_W), lambda j, i: (0, j))],
        out_specs=tile, out_shape=jax.ShapeDtypeStruct((t, 3 * DN_W), F32),
        compiler_params=_cp(("parallel", "parallel")),
    )(p, p, conv_w)


def _dn_pre_bwd(p, conv_w, dqkv, name):
    t = p.shape[0]
    r = _rows(t)
    ni = t // r

    def body(x_ref, xp_ref, xn_ref, w_ref, d_ref, dn_ref, dx_ref, dw_ref):
        j, i = pl.program_id(0), pl.program_id(1)
        xe = jnp.concatenate([jnp.where(i == 0, 0.0, xp_ref[...]), x_ref[...], xn_ref[...]], axis=0)
        de = jnp.concatenate([jnp.zeros((8, DN_W), F32), d_ref[...], jnp.where(i == ni - 1, 0.0, dn_ref[...])], axis=0)
        xs = [_down(xe, DN_CONV - 1 - k) for k in range(DN_CONV)]
        c = sum(xs[k] * w_ref[pl.ds(k, 1), :] for k in range(DN_CONV))
        s = _sig(c)
        a = c * s
        das = []
        for h in range(DN_HEADS):
            ah, dh = a[:, h * 128:(h + 1) * 128], de[:, h * 128:(h + 1) * 128]
            fac = lax.rsqrt(jnp.sum(ah * ah, axis=-1, keepdims=True) + NORM_EPS)
            dnorm = fac * dh - ah * (fac * fac * fac) * jnp.sum(dh * ah, axis=-1, keepdims=True)
            das.append(jnp.where(j == 0, dnorm * HEAD_DIM ** -0.5, jnp.where(j == 1, dnorm, dh)))
        dc = jnp.concatenate(das, axis=1) * _dsilu(c, s)
        dx_ref[...] = sum(_up(dc, DN_CONV - 1 - k) * w_ref[pl.ds(k, 1), :] for k in range(DN_CONV))[8:8 + r]

        @pl.when(i == 0)
        def _():
            dw_ref[...] = jnp.zeros_like(dw_ref)

        for k in range(DN_CONV):
            dw_ref[pl.ds(k, 1), :] += jnp.sum((dc * xs[k])[8:8 + r], axis=0, keepdims=True)

    tile, prev, nxt = _halo(t, r, 8, DN_W, lambda j: j)
    wspec = pl.BlockSpec((DN_CONV, DN_W), lambda j, i: (0, j))
    return pl.pallas_call(
        body, name=name, grid=(3, ni),
        in_specs=[tile, prev, nxt, wspec, tile, nxt],
        out_specs=[tile, wspec],
        out_shape=[jax.ShapeDtypeStruct((t, 3 * DN_W), F32), jax.ShapeDtypeStruct((DN_CONV, 3 * DN_W), F32)],
        compiler_params=_cp(("parallel", "arbitrary")),
    )(p, p, p, conv_w, dqkv, dqkv)


def _ffn_act_fwd(up, cw, cb, name):
    t, f2 = up.shape
    f = f2 // 2
    r = _tile(t, (512, 256, 128))
    tc = _tile(f, (512, 256, 128))
    nj = f // tc

    def body(a_ref, ap_ref, b_ref, bp_ref, wa_ref, wb_ref, ca_ref, cb_ref, o_ref):
        i = pl.program_id(1)

        def conv(x_ref, xp_ref, w_ref, c_ref):
            xe = jnp.concatenate([jnp.where(i == 0, 0.0, xp_ref[...]), x_ref[...]], axis=0)
            return sum(_down(xe, FFN_CONV - 1 - k) * w_ref[pl.ds(k, 1), :] for k in range(FFN_CONV))[8:] + c_ref[...]

        ua = conv(a_ref, ap_ref, wa_ref, ca_ref)
        ub = conv(b_ref, bp_ref, wb_ref, cb_ref)
        o_ref[...] = (ua * _sig(ua) * ub).astype(o_ref.dtype)

    ta, pa, _ = _halo(t, r, 8, tc, lambda j: j)
    tb, pb, _ = _halo(t, r, 8, tc, lambda j: j + nj)
    wa = pl.BlockSpec((FFN_CONV, tc), lambda j, i: (0, j))
    wb = pl.BlockSpec((FFN_CONV, tc), lambda j, i: (0, j + nj))
    ca = pl.BlockSpec((1, tc), lambda j, i: (0, j))
    cbs = pl.BlockSpec((1, tc), lambda j, i: (0, j + nj))
    return pl.pallas_call(
        body, name=name, grid=(nj, t // r),
        in_specs=[ta, pa, tb, pb, wa, wb, ca, cbs], out_specs=ta,
        out_shape=jax.ShapeDtypeStruct((t, f), BF16), compiler_params=_cp(("parallel", "parallel")),
    )(up, up, up, up, cw, cw, cb, cb)


def _ffn_act_bwd(up, cw, cb, dact, name):
    t, f2 = up.shape
    f = f2 // 2
    r = _tile(t, (512, 256, 128))
    ni = t // r
    tc = _tile(f, (512, 256, 128))
    nj = f // tc

    def body(a_ref, ap_ref, an_ref, b_ref, bp_ref, bn_ref, wa_ref, wb_ref, ca_ref, cb_ref, d_ref, dn_ref,
             du_ref, dw_ref, dc_ref):
        i = pl.program_id(1)
        dua_ref, dub_ref, dwa_ref, dwb_ref, dca_ref, dcb_ref = (du_ref.at[0], du_ref.at[1], dw_ref.at[0], dw_ref.at[1],
                                                                  dc_ref.at[0], dc_ref.at[1])

        def ext(x_ref, xp_ref, xn_ref):
            return jnp.concatenate([jnp.where(i == 0, 0.0, xp_ref[...]), x_ref[...], xn_ref[...]], axis=0)

        ae, be = ext(a_ref, ap_ref, an_ref), ext(b_ref, bp_ref, bn_ref)
        as_ = [_down(ae, FFN_CONV - 1 - k) for k in range(FFN_CONV)]
        bs_ = [_down(be, FFN_CONV - 1 - k) for k in range(FFN_CONV)]
        ua = sum(as_[k] * wa_ref[pl.ds(k, 1), :] for k in range(FFN_CONV)) + ca_ref[...]
        ub = sum(bs_[k] * wb_ref[pl.ds(k, 1), :] for k in range(FFN_CONV)) + cb_ref[...]
        de = jnp.concatenate([jnp.zeros((8, tc), F32), d_ref[...].astype(F32),
                              jnp.where(i == ni - 1, 0.0, dn_ref[...].astype(F32))], axis=0)
        s = _sig(ua)
        dua = de * ub * _dsilu(ua, s)
        dub = de * ua * s
        dua_ref[...] = sum(_up(dua, FFN_CONV - 1 - k) * wa_ref[pl.ds(k, 1), :] for k in range(FFN_CONV))[8:8 + r].astype(dua_ref.dtype)
        dub_ref[...] = sum(_up(dub, FFN_CONV - 1 - k) * wb_ref[pl.ds(k, 1), :] for k in range(FFN_CONV))[8:8 + r].astype(dub_ref.dtype)

        @pl.when(i == 0)
        def _():
            dw_ref[...] = jnp.zeros_like(dw_ref)
            dc_ref[...] = jnp.zeros_like(dc_ref)

        for k in range(FFN_CONV):
            dwa_ref[pl.ds(k, 1), :] += jnp.sum((dua * as_[k])[8:8 + r], axis=0, keepdims=True)
            dwb_ref[pl.ds(k, 1), :] += jnp.sum((dub * bs_[k])[8:8 + r], axis=0, keepdims=True)
        dca_ref[...] += jnp.sum(dua[8:8 + r], axis=0, keepdims=True)
        dcb_ref[...] += jnp.sum(dub[8:8 + r], axis=0, keepdims=True)

    ta, pa, na = _halo(t, r, 8, tc, lambda j: j)
    tb, pb, nb = _halo(t, r, 8, tc, lambda j: j + nj)
    wa = pl.BlockSpec((FFN_CONV, tc), lambda j, i: (0, j))
    wb = pl.BlockSpec((FFN_CONV, tc), lambda j, i: (0, j + nj))
    ca = pl.BlockSpec((1, tc), lambda j, i: (0, j))
    cbs = pl.BlockSpec((1, tc), lambda j, i: (0, j + nj))
    return pl.pallas_call(
        body, name=name, grid=(nj, ni),
        in_specs=[ta, pa, na, tb, pb, nb, wa, wb, ca, cbs, ta, na],
        out_specs=[pl.BlockSpec((2, r, tc), lambda j, i: (0, i, j)), pl.BlockSpec((2, FFN_CONV, tc), lambda j, i: (0, 0, j)),
                   pl.BlockSpec((2, 1, tc), lambda j, i: (0, 0, j))],
        out_shape=[jax.ShapeDtypeStruct((2, t, f), BF16), jax.ShapeDtypeStruct((2, FFN_CONV, f), F32),
                   jax.ShapeDtypeStruct((2, 1, f), F32)],
        compiler_params=_cp(("parallel", "arbitrary")),
    )(up, up, up, up, up, up, cw, cw, cb, cb, dact, dact)


def _pool_pick(g, vals):
    return jnp.where(g == 0, vals[0], jnp.where(g == 1, vals[1], jnp.where(g == 2, vals[2], vals[3])))


def _pool_pre(xe, g, t0):
    s1 = xe + _down(xe, 1)
    s2 = s1 + _down(s1, 2)
    s3 = s2 + _down(s2, 4)
    s4 = s3 + _down(s3, 8)
    r = xe.shape[0] - 16
    pos = (t0 + lax.broadcasted_iota(jnp.int32, (r, 1), 0)).astype(F32)
    cnt = jnp.minimum(pos + 1.0, _pool_pick(g, (2.0, 4.0, 8.0, 16.0)))
    return _pool_pick(g, (s1, s2, s3, s4))[16:] / cnt - xe[16:]


def _pool_fwd(p, pool_w, scale, name):
    t = p.shape[0]
    r = _tile(t, (1024, 256, 128))

    def body(x_ref, xp_ref, w_ref, sc_ref, o_ref):
        g, i = pl.program_id(0), pl.program_id(1)
        xe = jnp.concatenate([jnp.where(i == 0, 0.0, xp_ref[...]), x_ref[...]], axis=0)
        pre = _pool_pre(xe, g, i * r)
        o_ref[...] = jnp.dot(pre, w_ref[0], preferred_element_type=F32) * sc_ref[...]

    tile, prev, _ = _halo(t, r, 16, 128, lambda j: CB_POOL + j)
    return pl.pallas_call(
        body, name=name, grid=(POOL_GROUPS, t // r),
        in_specs=[tile, prev, pl.BlockSpec((1, 128, 128), lambda j, i: (j, 0, 0)), pl.BlockSpec((1, 128), lambda j, i: (0, j))],
        out_specs=pl.BlockSpec((r, 128), lambda j, i: (i, j)),
        out_shape=jax.ShapeDtypeStruct((t, POOL_W), F32), compiler_params=_cp(("parallel", "parallel")),
    )(p, p, pool_w, scale)


def _pool_bwd(p, pool_w, scale, dycat, name):
    t = p.shape[0]
    r = _tile(t, (1024, 256, 128))
    ni = t // r

    def body(x_ref, xp_ref, w_ref, sc_ref, d_ref, dn_ref, dx_ref, dw_ref, dsc_ref):
        g, i = pl.program_id(0), pl.program_id(1)
        xe = jnp.concatenate([jnp.where(i == 0, 0.0, xp_ref[...]), x_ref[...]], axis=0)
        pre = _pool_pre(xe, g, i * r)
        w = w_ref[0]
        dy = d_ref[...]
        dye = jnp.concatenate([dy, jnp.where(i == ni - 1, 0.0, dn_ref[...])], axis=0)
        dpre = lax.dot_general(dye * sc_ref[...], w, _DN["nt"], preferred_element_type=F32)
        pos = (i * r + lax.broadcasted_iota(jnp.int32, (r + 16, 1), 0)).astype(F32)
        dm = dpre / jnp.minimum(pos + 1.0, _pool_pick(g, (2.0, 4.0, 8.0, 16.0)))
        a1 = dm + _up(dm, 1)
        a2 = a1 + _up(a1, 2)
        a3 = a2 + _up(a2, 4)
        a4 = a3 + _up(a3, 8)
        dx_ref[...] = (_pool_pick(g, (a1, a2, a3, a4)) - dpre)[:r]

        @pl.when(i == 0)
        def _():
            dw_ref[...] = jnp.zeros_like(dw_ref)
            dsc_ref[...] = jnp.zeros_like(dsc_ref)

        dw_ref[0] += lax.dot_general(pre, dy * sc_ref[...], _DN["tn"], preferred_element_type=F32)
        dsc_ref[...] += jnp.sum(dy * jnp.dot(pre, w, preferred_element_type=F32), axis=0, keepdims=True)

    tile, prev, _ = _halo(t, r, 16, 128, lambda j: CB_POOL + j)
    dtile, _, dnxt = _halo(t, r, 16, 128, lambda j: DN_W // 128 + j)
    wspec = pl.BlockSpec((1, 128, 128), lambda j, i: (j, 0, 0))
    sspec = pl.BlockSpec((1, 128), lambda j, i: (0, j))
    return pl.pallas_call(
        body, name=name, grid=(POOL_GROUPS, ni),
        in_specs=[tile, prev, wspec, sspec, dtile, dnxt],
        out_specs=[pl.BlockSpec((r, 128), lambda j, i: (i, j)), wspec, sspec],
        out_shape=[jax.ShapeDtypeStruct((t, POOL_W), F32), jax.ShapeDtypeStruct((POOL_GROUPS, 128, 128), F32),
                   jax.ShapeDtypeStruct((1, POOL_W), F32)],
        compiler_params=_cp(("parallel", "arbitrary")),
    )(p, p, pool_w, scale, dycat, dycat)


_DNB = {"nn": (((2,), (1,)), ((0,), (0,))), "nt": (((2,), (2,)), ((0,), (0,))), "tn": (((1,), (1,)), ((0,), (0,)))}


def _dot(a, b, mode="nn", precision=None):
    dn = _DNB[mode] if a.ndim == 3 else _DN[mode]
    return lax.dot_general(a, b, dn, precision=precision, preferred_element_type=F32)


@functools.partial(jax.custom_vjp, nondiff_argnums=(2,))
def _bdot(a, b, mode):
    return _dot(a.astype(BF16), b.astype(BF16), mode)


def _bdot_fwd(a, b, mode):
    return _bdot(a, b, mode), (a, b)


def _bdot_bwd(mode, res, g):
    a, b = res
    if mode == "nn":
        return _bdot(g, b, "nt"), _bdot(a, g, "tn")
    if mode == "nt":
        return _bdot(g, b, "nn"), _bdot(g, a, "tn")
    return _bdot(b, g, "nt"), _bdot(a, g, "nn")


_bdot.defvjp(_bdot_fwd, _bdot_bwd)


def _dn_consts():
    c = DN_CHUNK
    ii = lax.broadcasted_iota(jnp.int32, (c, c), 0)
    jj = lax.broadcasted_iota(jnp.int32, (c, c), 1)
    one, zero = jnp.ones((c, c), F32), jnp.zeros((c, c), F32)
    return dict(ltri=jnp.where(ii >= jj, one, zero), utri=jnp.where(ii <= jj, one, zero), ones=one,
                causal=ii >= jj, strict=ii > jj, eye=jnp.where(ii == jj, one, zero))


def _dn_chunk(q, k, v, z, bcol, acol, s_in, alog, dtb, nw, cs):
    c = DN_CHUNK
    hh = q.shape[0]
    per_head = lambda m: jnp.broadcast_to(m, (hh, c, c))
    beta = _sig(bcol)
    xa = acol + dtb
    g = -jnp.exp(alog) * (jnp.maximum(xa, 0.0) + jnp.log(1.0 + jnp.exp(-jnp.abs(xa))))
    gb = jnp.broadcast_to(g, (hh, c, HEAD_DIM))
    gbc = jnp.broadcast_to(g, (hh, c, c))
    gc = _dot(per_head(cs["ltri"]), gb, precision=HIGH)
    gcol = gc[:, :, :c]
    grow = jnp.swapaxes(gcol, 1, 2)
    decay = jnp.exp(jnp.where(cs["causal"], gcol - grow, -1e30))
    kb = k * beta
    vb = v * beta
    nil = -jnp.where(cs["strict"], _bdot(kb, k, "nt") * decay, 0.0)
    inv = cs["eye"] + nil
    powk = nil
    for _ in range(int(math.log2(c)) - 1):
        powk = _bdot(powk, powk, "nn")
        inv = _bdot(inv, cs["eye"] + powk, "nn")
    eg = jnp.exp(gc)
    u = _bdot(inv, vb, "nn")
    w = _bdot(inv, kb * eg, "nn")
    a = _bdot(q, k, "nt") * decay
    v_new = u - _bdot(w, s_in, "nn")
    o = _bdot(q * eg, s_in, "nn") + _bdot(a, v_new, "nn")
    glast = jnp.sum(gb, axis=1, keepdims=True)
    s_out = s_in * jnp.exp(glast) + _bdot(k * jnp.exp(glast - gc), v_new, "tn")
    on = o * lax.rsqrt(jnp.mean(o * o, axis=-1, keepdims=True) + NORM_EPS) * nw
    return on * (z * _sig(z)), s_out


def _lane_pick(x, lane, idx):
    return jnp.sum(jnp.where(lane == idx, x, 0.0), axis=1, keepdims=True)


def _dn_load(q_ref, k_ref, v_ref, z_ref, bd_ref, al_ref, dt_ref, nw_ref, s_in):
    lane = lax.broadcasted_iota(jnp.int32, (1, 128), 1)
    bd, al, dt = bd_ref[...], al_ref[...], dt_ref[...]
    heads = range(DN_HEADS)
    wide = lambda ref: jnp.stack([ref[:, h * 128:(h + 1) * 128] for h in heads], axis=0)
    col = lambda x, off: jnp.stack([_lane_pick(x, lane, off + h) for h in heads], axis=0)
    return (wide(q_ref), wide(k_ref), wide(v_ref), wide(z_ref), col(bd, 0), col(bd, DN_HEADS), s_in,
            col(al, 0), col(dt, 0), nw_ref[...])


def _dn_fwd(qkv, p, alog, dtb, nw, name):
    t = qkv.shape[0]
    c = DN_CHUNK
    n = t // c

    def body(q_ref, k_ref, v_ref, z_ref, bd_ref, al_ref, dt_ref, nw_ref, y_ref, ss_ref, s_scr):
        @pl.when(pl.program_id(0) == 0)
        def _():
            s_scr[...] = jnp.zeros_like(s_scr)

        s_in = s_scr[...]
        y, s_out = _dn_chunk(*_dn_load(q_ref, k_ref, v_ref, z_ref, bd_ref, al_ref, dt_ref, nw_ref, s_in), _dn_consts())
        ss_ref[0] = s_in
        s_scr[...] = s_out
        for h in range(DN_HEADS):
            y_ref[:, h * 128:(h + 1) * 128] = y[h]

    wide = lambda j: pl.BlockSpec((c, DN_W), lambda i: (i, j))
    vec = pl.BlockSpec((1, 128), lambda i: (0, 0))
    return pl.pallas_call(
        body, name=name, grid=(n,),
        in_specs=[wide(0), wide(1), wide(2), wide(3), pl.BlockSpec((c, 128), lambda i: (i, CB_BD)), vec, vec, vec],
        out_specs=[wide(0), pl.BlockSpec((1, DN_HEADS, 128, 128), lambda i: (i, 0, 0, 0))],
        out_shape=[jax.ShapeDtypeStruct((t, DN_W), F32), jax.ShapeDtypeStruct((n, DN_HEADS, 128, 128), F32)],
        scratch_shapes=[pltpu.VMEM((DN_HEADS, 128, 128), F32)],
        compiler_params=_cp(("arbitrary",)),
    )(qkv, qkv, qkv, p, p, alog, dtb, nw)


def _dn_bwd(qkv, p, alog, dtb, nw, states, dycat, name, carry=None):
    t = qkv.shape[0]
    c = DN_CHUNK
    n = t // c
    sums, kinds = carry if carry is not None else ((), ())
    na = len(sums)

    def body(*refs):
        q_ref, k_ref, v_ref, z_ref, bd_ref, al_ref, dt_ref, nw_ref, ss_ref, dy_ref = refs[:10]
        dqkv_ref, dz_ref, dbd_ref, dal_ref, ddt_ref, dnw_ref = refs[10 + na:16 + na]
        ds_scr = refs[16 + 2 * na]
        if na:
            copies = _scatter_copies(refs[10:10 + na], refs[16 + na:16 + 2 * na], kinds, *refs[17 + 2 * na:])

        @pl.when(pl.program_id(0) == 0)
        def _():
            ds_scr[...] = jnp.zeros_like(ds_scr)
            dal_ref[...] = jnp.zeros_like(dal_ref)
            ddt_ref[...] = jnp.zeros_like(ddt_ref)
            dnw_ref[...] = jnp.zeros_like(dnw_ref)
            if na:
                for cp in copies:
                    cp.start()

        lane = lax.broadcasted_iota(jnp.int32, (1, 128), 1)
        args = _dn_load(q_ref, k_ref, v_ref, z_ref, bd_ref, al_ref, dt_ref, nw_ref, ss_ref[0])
        dy = jnp.stack([dy_ref[:, h * 128:(h + 1) * 128] for h in range(DN_HEADS)], axis=0)
        _, vjp = jax.vjp(functools.partial(_dn_chunk, cs=_dn_consts()), *args)
        gq, gk, gv, gz, gb, ga, gs, gal, gdt, gnw = vjp((dy, ds_scr[...]))
        ds_scr[...] = gs
        dbd = jnp.zeros((c, 128), F32)
        dal = jnp.zeros((1, 128), F32)
        ddt = jnp.zeros((1, 128), F32)
        for h in range(DN_HEADS):
            sl = slice(h * 128, (h + 1) * 128)
            dqkv_ref[:, sl] = gq[h]
            dqkv_ref[:, DN_W + h * 128:DN_W + (h + 1) * 128] = gk[h]
            dqkv_ref[:, 2 * DN_W + h * 128:2 * DN_W + (h + 1) * 128] = gv[h]
            dz_ref[:, sl] = gz[h]
            dbd = dbd + jnp.where(lane == h, gb[h], 0.0) + jnp.where(lane == DN_HEADS + h, ga[h], 0.0)
            dal = dal + jnp.where(lane == h, gal[h], 0.0)
            ddt = ddt + jnp.where(lane == h, gdt[h], 0.0)
        dbd_ref[...] = dbd
        dal_ref[...] += dal
        ddt_ref[...] += ddt
        dnw_ref[...] += gnw

        if na:
            @pl.when(pl.program_id(0) == n - 1)
            def _():
                for cp in copies:
                    cp.wait()

    rev = lambda i: n - 1 - i
    wide = lambda j: pl.BlockSpec((c, DN_W), lambda i: (rev(i), j))
    vec = pl.BlockSpec((1, 128), lambda i: (0, 0))
    any_space = pl.BlockSpec(memory_space=pl.ANY)
    return pl.pallas_call(
        body, name=name, grid=(n,),
        in_specs=[wide(0), wide(1), wide(2), wide(3), pl.BlockSpec((c, 128), lambda i: (rev(i), CB_BD)), vec, vec, vec,
                  pl.BlockSpec((1, DN_HEADS, 128, 128), lambda i: (rev(i), 0, 0, 0)), wide(0)] + [any_space] * na,
        out_specs=[pl.BlockSpec((c, 3 * DN_W), lambda i: (rev(i), 0)), wide(0),
                   pl.BlockSpec((c, 128), lambda i: (rev(i), 0)), vec, vec, vec] + [any_space] * na,
        out_shape=[jax.ShapeDtypeStruct((t, 3 * DN_W), F32), jax.ShapeDtypeStruct((t, DN_W), F32),
                   jax.ShapeDtypeStruct((t, 128), F32), jax.ShapeDtypeStruct((1, 128), F32),
                   jax.ShapeDtypeStruct((1, 128), F32), jax.ShapeDtypeStruct((1, 128), F32)] + _scatter_shapes(sums, kinds),
        scratch_shapes=[pltpu.VMEM((DN_HEADS, 128, 128), F32)] + ([pltpu.SemaphoreType.DMA((na, 3))] * 2 if na else []),
        compiler_params=_cp(("arbitrary",)),
    )(qkv, qkv, qkv, p, p, alog, dtb, nw, states, dycat, *sums)


def _rope(x, cosf, sins):
    return x * cosf + pltpu.roll(x, HEAD_DIM // 2, 1) * sins


def _rope_t(d, cosf, sins):
    return d * cosf + pltpu.roll(d * sins, HEAD_DIM // 2, 1)


def _swa_masks():
    b = SWA_BLOCK
    i = lax.broadcasted_iota(jnp.int32, (SWA_GROUP * b, b), 0) & (b - 1)
    j = lax.broadcasted_iota(jnp.int32, (SWA_GROUP * b, b), 1)
    return j > i, j <= i


def _swa_sink_col(sinks_ref, h):
    b = SWA_BLOCK
    r = lax.broadcasted_iota(jnp.int32, (SWA_GROUP * b, 1), 0)
    s = [sinks_ref[h * SWA_GROUP + g] for g in range(SWA_GROUP)]
    return jnp.where(r < b, s[0], jnp.where(r < 2 * b, s[1], s[2]))


def _swa_specs(t, h_first):
    nb = t // SWA_BLOCK

    def at(col, off):
        def imap(h, n):
            return (jnp.clip(n + off, 0, nb - 1), col(h))
        return pl.BlockSpec((SWA_BLOCK, 128), imap)
    return at


def _swa_fwd(p, cosf, sins, sinks, name):
    t = p.shape[0]
    b = SWA_BLOCK
    nb = t // b
    at = _swa_specs(t, None)
    scale = HEAD_DIM ** -0.5

    def body(q0, q1, q2, kp, kc, vp, vc, cc, sc, cp, sp, sinks_ref, o_ref, lse_ref):
        h, n = pl.program_id(0), pl.program_id(1)
        qs = jnp.concatenate([_rope(q[...], cc[...], sc[...]) for q in (q0, q1, q2)], axis=0)
        ks = jnp.concatenate([_rope(kp[...], cp[...], sp[...]), _rope(kc[...], cc[...], sc[...])], axis=0)
        vs = jnp.concatenate([vp[...], vc[...]], axis=0)
        mp, mc = _swa_masks()
        mask = jnp.concatenate([mp & (n > 0), mc], axis=1)
        s = jnp.where(mask, _dot(qs, ks, "nt") * scale, -1e30)
        sink = _swa_sink_col(sinks_ref, h)
        m = jnp.maximum(jnp.max(s, axis=1, keepdims=True), sink)
        e = jnp.exp(s - m)
        l = jnp.sum(e, axis=1, keepdims=True) + jnp.exp(sink - m)
        o = _dot(e, vs) / l
        lse = m + jnp.log(l)
        lane = lax.broadcasted_iota(jnp.int32, (1, 128), 1)
        tile = jnp.zeros((b, 128), F32)
        for g in range(SWA_GROUP):
            o_ref[:, g * 128:(g + 1) * 128] = o[g * b:(g + 1) * b]
            tile = tile + jnp.where(lane == g, lse[g * b:(g + 1) * b], 0.0)
        lse_ref[0] = tile

    qcol = lambda g: (lambda h: CB_SQ + h * SWA_GROUP + g)
    kcol, vcol, one = (lambda h: CB_SK + h), (lambda h: CB_SV + h), (lambda h: 0)
    in_specs = [at(qcol(0), 0), at(qcol(1), 0), at(qcol(2), 0), at(kcol, -1), at(kcol, 0), at(vcol, -1), at(vcol, 0),
                at(one, 0), at(one, 0), at(one, -1), at(one, -1), pl.BlockSpec(memory_space=pltpu.SMEM)]
    return pl.pallas_call(
        body, name=name, grid=(SWA_KV_HEADS, nb), in_specs=in_specs,
        out_specs=[pl.BlockSpec((b, SWA_GROUP * 128), lambda h, n: (n, h)), pl.BlockSpec((1, b, 128), lambda h, n: (h, n, 0))],
        out_shape=[jax.ShapeDtypeStruct((t, SWA_W), F32), jax.ShapeDtypeStruct((SWA_KV_HEADS, t, 128), F32)],
        compiler_params=_cp(("parallel", "parallel")),
    )(p, p, p, p, p, p, p, cosf, sins, cosf, sins, sinks)


def _swa_bwd(p, cosf, sins, sinks, o, lse, dycat, name):
    t = p.shape[0]
    b = SWA_BLOCK
    nb = t // b
    at = _swa_specs(t, None)
    scale = HEAD_DIM ** -0.5
    gb = SWA_GROUP * b

    def body(q0, q1, q2, r0, r1, r2, kp, kc, vp, vc, cc, sc, cp, sp, cn, sn, d0, d1, d2, e0, e1, e2,
             oc_ref, on_ref, lc_ref, ln_ref, sinks_ref, dq_ref, dk_ref, dv_ref, dsk_ref):
        h, n = pl.program_id(0), pl.program_id(1)
        lane = lax.broadcasted_iota(jnp.int32, (1, 128), 1)
        stack = lambda refs: jnp.concatenate([x[...] for x in refs], axis=0)
        q_c = jnp.concatenate([_rope(q[...], cc[...], sc[...]) for q in (q0, q1, q2)], axis=0)
        q_n = jnp.concatenate([_rope(q[...], cn[...], sn[...]) for q in (r0, r1, r2)], axis=0)
        k_p = _rope(kp[...], cp[...], sp[...])
        k_c = _rope(kc[...], cc[...], sc[...])
        do_c, do_n = stack((d0, d1, d2)), stack((e0, e1, e2))
        o_c = jnp.concatenate([oc_ref[:, g * 128:(g + 1) * 128] for g in range(SWA_GROUP)], axis=0)
        o_n = jnp.concatenate([on_ref[:, g * 128:(g + 1) * 128] for g in range(SWA_GROUP)], axis=0)
        lse_c = jnp.concatenate([_lane_pick(lc_ref[0], lane, g) for g in range(SWA_GROUP)], axis=0)
        lse_n = jnp.concatenate([_lane_pick(ln_ref[0], lane, g) for g in range(SWA_GROUP)], axis=0)
        dl_c = jnp.sum(do_c * o_c, axis=1, keepdims=True)
        dl_n = jnp.sum(do_n * o_n, axis=1, keepdims=True)
        mp, mc = _swa_masks()

        def pair(qr, kr, v, do, lse_, dl, mask):
            s = _dot(qr, kr, "nt") * scale
            pr = jnp.where(mask, jnp.exp(s - lse_), 0.0)
            ds = pr * (_dot(do, v, "nt") - dl) * scale
            return _dot(ds, kr), _dot(ds, qr, "tn"), _dot(pr, do, "tn")

        dq_a, _, _ = pair(q_c, k_p, vp[...], do_c, lse_c, dl_c, mp & (n > 0))
        dq_b, dk_b, dv_b = pair(q_c, k_c, vc[...], do_c, lse_c, dl_c, mc)
        _, dk_n, dv_n = pair(q_n, k_c, vc[...], do_n, lse_n, dl_n, mp & (n < nb - 1))
        dq = dq_a + dq_b
        for g in range(SWA_GROUP):
            dq_ref[:, g * 128:(g + 1) * 128] = _rope_t(dq[g * b:(g + 1) * b], cc[...], sc[...])
        dk_ref[...] = _rope_t(dk_b + dk_n, cc[...], sc[...])
        dv_ref[...] = dv_b + dv_n

        @pl.when(n == 0)
        def _():
            dsk_ref[...] = jnp.zeros_like(dsk_ref)

        w = -jnp.exp(_swa_sink_col(sinks_ref, h) - lse_c) * dl_c
        acc = jnp.zeros((1, 128), F32)
        for g in range(SWA_GROUP):
            acc = acc + jnp.where(lane == g, jnp.sum(w[g * b:(g + 1) * b], axis=0, keepdims=True), 0.0)
        dsk_ref[0] += jnp.broadcast_to(acc, (8, 128))

    qcol = lambda g: (lambda h: CB_SQ + h * SWA_GROUP + g)
    dcol = lambda g: (lambda h: (DN_W + POOL_W) // 128 + h * SWA_GROUP + g)
    kcol, vcol, one = (lambda h: CB_SK + h), (lambda h: CB_SV + h), (lambda h: 0)
    wide = lambda off: pl.BlockSpec((b, SWA_GROUP * 128), lambda h, n: (jnp.clip(n + off, 0, nb - 1), h))
    lspec = lambda off: pl.BlockSpec((1, b, 128), lambda h, n: (h, jnp.clip(n + off, 0, nb - 1), 0))
    in_specs = ([at(qcol(g), 0) for g in range(3)] + [at(qcol(g), 1) for g in range(3)]
                + [at(kcol, -1), at(kcol, 0), at(vcol, -1), at(vcol, 0)]
                + [at(one, 0), at(one, 0), at(one, -1), at(one, -1), at(one, 1), at(one, 1)]
                + [at(dcol(g), 0) for g in range(3)] + [at(dcol(g), 1) for g in range(3)]
                + [wide(0), wide(1), lspec(0), lspec(1), pl.BlockSpec(memory_space=pltpu.SMEM)])
    kv_out = pl.BlockSpec((b, 128), lambda h, n: (n, h))
    return pl.pallas_call(
        body, name=name, grid=(SWA_KV_HEADS, nb), in_specs=in_specs,
        out_specs=[wide(0), kv_out, kv_out, pl.BlockSpec((1, 8, 128), lambda h, n: (h, 0, 0))],
        out_shape=[jax.ShapeDtypeStruct((t, SWA_W), F32), jax.ShapeDtypeStruct((t, SWA_KV_W), F32),
                   jax.ShapeDtypeStruct((t, SWA_KV_W), F32), jax.ShapeDtypeStruct((SWA_KV_HEADS, 8, 128), F32)],
        compiler_params=_cp(("parallel", "arbitrary")),
    )(*([p] * 10), cosf, sins, cosf, sins, cosf, sins, *([dycat] * 6), o, o, lse, lse, sinks)


def _adam_math(w, g, m, v):
    m = ADAM_B1 * m + (1.0 - ADAM_B1) * g
    v = ADAM_B2 * v + (1.0 - ADAM_B2) * (g * g)
    m_hat = m / (1.0 - ADAM_B1 ** ADAM_STEP)
    v_hat = v / (1.0 - ADAM_B2 ** ADAM_STEP)
    return -ADAM_LR * (m_hat / (jnp.sqrt(v_hat) + ADAM_EPS) + ADAM_WD * w), m, v


def _adamw(w, g, m, v, name):
    shape = w.shape
    cols = shape[-1]
    rows = math.prod(shape[:-1])
    flat = lambda a: a.reshape(rows, cols)
    r = rows
    for cand in (512, 256, 128, 64, 32, 16, 8):
        if rows % cand == 0 and cand * cols * 4 <= (1 << 20):
            r = cand
            break

    def body(w_ref, g_ref, m_ref, v_ref, d_ref, nm_ref, nv_ref):
        d_ref[...], nm_ref[...], nv_ref[...] = _adam_math(w_ref[...], g_ref[...], m_ref[...], v_ref[...])

    spec = pl.BlockSpec((r, cols), lambda i: (i, 0))
    outs = pl.pallas_call(
        body, name=name, grid=(rows // r,), in_specs=[spec] * 4, out_specs=[spec] * 3,
        out_shape=[jax.ShapeDtypeStruct((rows, cols), F32)] * 3, compiler_params=_cp(("parallel",)),
    )(flat(w), flat(g), flat(m), flat(v))
    return tuple(o.reshape(shape) for o in outs)


BIG = ("w_in", "w_out", "ffn_w_up", "ffn_w_down")
CONV = ("dn_conv_w", "ffn_conv_w")
KIND = {"w_in": "col", "w_out": "row", "ffn_w_up": "col", "ffn_w_down": "row"}
SMALL = ("norm_mix_pre", "dn_a_log", "dn_dt_bias", "dn_norm_w", "pool_w", "pool_scale", "swa_sinks",
         "norm_mix_post", "norm_ffn_pre", "ffn_conv_b", "norm_ffn_post")
WEIGHTS = ("norm_mix_pre", "w_in", "dn_conv_w", "dn_a_log", "dn_dt_bias", "dn_norm_w", "pool_w", "pool_scale",
           "swa_sinks", "w_out", "norm_mix_post", "norm_ffn_pre", "ffn_w_up", "ffn_conv_w", "ffn_conv_b",
           "ffn_w_down", "norm_ffn_post")


def _pad_in(w):
    z = lambda n: jnp.zeros(w.shape[:-1] + (n,), w.dtype)
    return jnp.concatenate([w[..., :GATE_END], z(CB_POOL * 128 - GATE_END), w[..., GATE_END:],
                            z(IN_PAD - CB_POOL * 128 - (IN_TRUE - GATE_END))], axis=-1)


def _unpad_in(g):
    return jnp.concatenate([g[..., :GATE_END], g[..., CB_POOL * 128:CB_POOL * 128 + IN_TRUE - GATE_END]], axis=-1)


IN_SHARD = IN_TRUE // 4
IN_SHARD_PAD = -(-IN_SHARD // 128) * 128


def _in_runs():
    cuts = sorted({0, IN_TRUE, GATE_END} | {j * IN_SHARD for j in range(4)})
    runs = []
    for t0, t1 in zip(cuts[:-1], cuts[1:]):
        chip = t0 // IN_SHARD
        runs.append((chip * IN_SHARD_PAD + t0 - chip * IN_SHARD, t0 if t0 < GATE_END else t0 + CB_POOL * 128 - GATE_END, t1 - t0))
    return runs


def _remap_cols(x, runs, out_cols, name):
    rows, cols = x.shape
    r = _rows(rows)
    pieces = {}
    for src, dst, n in runs:
        while n > 0:
            step = min(n, 128 - src % 128, 128 - dst % 128)
            pieces.setdefault(dst // 128, []).append((src // 128, src % 128, dst % 128, step))
            src, dst, n = src + step, dst + step, n - step

    def body(x_ref, o_ref):
        lane = lax.broadcasted_iota(jnp.int32, (1, 128), 1)
        for ob in range(out_cols // 128):
            acc = jnp.zeros((r, 128), F32)
            for ib, ls, ld, n in pieces.get(ob, ()):
                blk = x_ref[:, ib * 128:(ib + 1) * 128].astype(F32)
                moved = blk if ls == ld else pltpu.roll(blk, (ld - ls) % 128, 1)
                acc = jnp.where((lane >= ld) & (lane < ld + n), moved, acc)
            o_ref[:, ob * 128:(ob + 1) * 128] = acc.astype(o_ref.dtype)

    return pl.pallas_call(
        body, name=name, grid=(rows // r,), in_specs=[pl.BlockSpec((r, cols), lambda i: (i, 0))],
        out_specs=pl.BlockSpec((r, out_cols), lambda i: (i, 0)),
        out_shape=jax.ShapeDtypeStruct((rows, out_cols), x.dtype), compiler_params=_cp(("parallel",)),
    )(x)


def _lanes(v):
    return jnp.zeros((1, 128), F32).at[0, :v.shape[0]].set(v)


def _rope_tables(positions):
    inv_freq = 1.0 / (ROPE_THETA ** (jnp.arange(0, HEAD_DIM, 2, dtype=F32) / HEAD_DIM))
    ang = positions.astype(F32)[:, None] * inv_freq
    cos, sin = jnp.cos(ang), jnp.sin(ang)
    return jnp.concatenate([cos, cos], axis=-1), jnp.concatenate([-sin, sin], axis=-1)


class _GradReduce:
    def __init__(self, place, shard_shapes):
        self.place = place
        self.out = {k: lax.empty(shard_shapes[k], F32) for k in BIG}
        self.pending = []

    def submit(self, l, dw):
        def parts(k, g):
            if KIND[k] == "row":
                return g.reshape(4, -1, g.shape[1])
            if k == "w_in":
                g = _remap_cols(g, [(seg, chip, n) for chip, seg, n in _in_runs()], 4 * IN_SHARD_PAD, f"l{l}_w_in_grad_layout")
            return g[None]

        names = [k for k in BIG if k in dw]
        tag = f"l{l}_" + "_".join(names)
        mine = [parts(k, dw[k]) for k in names]
        got = _swap_sibling(mine, tag + "_to_sibling")
        self.pending += [(l, k, _chip_sum(a, b, self.place, f"l{l}_chip_sum_{k}")) for k, a, b in zip(names, mine, got)]

    def take(self, names):
        entries = [e for e in self.pending if e[1] in names]
        if not entries:
            return None, None
        self.pending = [e for e in self.pending if e[1] not in names]
        return entries, ([s for _, _, s in entries], [KIND[k] for _, k, _ in entries])

    def arrived(self, entries, got):
        for (l, k, own), g in zip(entries, got):
            self.out[k] = _owner_sum(own, g, KIND[k], self.place, (self.out[k], l), f"l{l}_owner_sum_{k}")

    def finish(self):
        entries, (sums, kinds) = self.take(BIG)
        self.arrived(entries, _scatter_chips(sums, kinds, "last_grads_to_owner"))
        return dict(zip(BIG, _join_halves([self.out[k] for k in BIG], "grads_join")))


class _LayerWeights:
    def __init__(self, layers):
        self.layers = layers

    def layer(self, l):
        return self.layers[l]

    def carry(self, l, k):
        return None


class _WeightGather(_LayerWeights):
    def __init__(self, shards, place):
        depth = shards["w_out"].shape[0]
        self.kinds = [KIND[k] for k in BIG]
        self.raw = [{k: _spread_shard(shards[k], l, KIND[k], place, BF16, f"l{l}_cast_{k}") for k in BIG} for l in range(depth)]
        first = _gather_ici([self.raw[0][k] for k in BIG], self.kinds, "l0_gather")
        self.layers = {0: self._passed(0, first)}

    def _passed(self, l, arrs):
        full = dict(zip(BIG, _gather_pass(arrs, self.kinds, f"l{l}_gather_pass")))
        full["w_in"] = _remap_cols(full["w_in"], _in_runs(), IN_PAD, f"l{l}_w_in_layout")
        return full

    def carry(self, l, k):
        return (self.raw[l + 1][k], KIND[k]) if l + 1 < len(self.raw) else None

    def carried(self, l, landed):
        self.layers[l + 1] = self._passed(l + 1, [landed[k] for k in BIG])


def _local_step(x, positions, target, w, mats, reduce=None):
    depth = w["norm_mix_pre"].shape[0]
    t = x.shape[0]
    cosf, sins = _rope_tables(positions)
    saved = []
    for l in range(depth):
        nm = f"l{l}_"
        n1, n2, n3, n4 = (w[k][l][None] for k in ("norm_mix_pre", "norm_mix_post", "norm_ffn_pre", "norm_ffn_post"))
        alog, dtb, dnw = _lanes(w["dn_a_log"][l]), _lanes(w["dn_dt_bias"][l]), w["dn_norm_w"][l][None]
        psc, cb = w["pool_scale"][l][None], w["ffn_conv_b"][l][None]
        big = mats.layer(l)
        landed = {}

        def project(a, k, name):
            riding = mats.carry(l, k)
            if riding is None:
                return _mm(a, big[k], "nn", F32, name)
            out, landed[k] = _mm(a, big[k], "nn", F32, name, carry=riding)
            return out

        h = _norm_fwd(x, n1, nm + "norm1")
        p = project(h, "w_in", nm + "in_proj")
        qkv = _dn_pre_fwd(p, w["dn_conv_w"][l], nm + "dn_pre")
        y_dn, st = _dn_fwd(qkv, p, alog, dtb, dnw, nm + "dn")
        y_pool = _pool_fwd(p, w["pool_w"][l], psc, nm + "pool")
        y_swa, lse = _swa_fwd(p, cosf, sins, w["swa_sinks"][l], nm + "swa")
        ycat = jnp.concatenate([y_dn, y_pool, y_swa], axis=1).astype(BF16)
        mix = project(ycat, "w_out", nm + "out_proj")
        x1 = _resnorm_fwd(x, mix, n2, nm + "res1")
        h2 = _norm_fwd(x1, n3, nm + "norm3")
        up = project(h2, "ffn_w_up", nm + "ffn_up")
        act = _ffn_act_fwd(up, w["ffn_conv_w"][l], cb, nm + "ffn_act")
        f = project(act, "ffn_w_down", nm + "ffn_down")
        if landed:
            mats.carried(l, landed)
        x2 = _resnorm_fwd(x1, f, n4, nm + "res2")
        saved.append(dict(x=x, h=h, p=p, qkv=qkv, st=st, y_swa=y_swa, lse=lse, ycat=ycat, mix=mix, x1=x1, h2=h2,
                          up=up, act=act, f=f, n=(n1, n2, n3, n4), alog=alog, dtb=dtb, dnw=dnw, psc=psc, cb=cb))
        x = x2
    loss, dx = _loss_head(x, target, "loss_head")
    grads = {k: [None] * depth for k in WEIGHTS}
    held = None
    for l in reversed(range(depth)):
        nm, s = f"l{l}_b_", saved[l]
        n1, n2, n3, n4 = s["n"]
        big = mats.layer(l)
        df, g4 = _norm_bwd(s["f"], n4, dx, None, BF16, nm + "res2")
        dact = _mm(df, big["ffn_w_down"], "nt", F32, nm + "ffn_down_dx")
        grads["ffn_w_down"][l] = _mm(s["act"], df, "tn", BF16, nm + "ffn_down_dw")
        dup, dcw, dcb = _ffn_act_bwd(s["up"], w["ffn_conv_w"][l], s["cb"], dact, nm + "ffn_act")
        grads["ffn_conv_w"][l] = jnp.concatenate([dcw[0], dcw[1]], axis=1)
        grads["ffn_conv_b"][l] = jnp.concatenate([dcb[0], dcb[1]], axis=1)[0]
        grads["ffn_w_up"][l] = _mm(s["h2"], dup, "tn", BF16, nm + "ffn_up_dw", b_pick="split")
        dh2 = _mm(dup, big["ffn_w_up"], "nt", BF16, nm + "ffn_up_dx", a_pick="split")
        dx1, g3 = _norm_bwd(s["x1"], n3, dh2, dx, F32, nm + "norm3")
        dmix, g2 = _norm_bwd(s["mix"], n2, dx1, None, BF16, nm + "res1")
        grads["w_out"][l] = _mm(s["ycat"], dmix, "tn", BF16, nm + "out_proj_dw")
        dycat = _mm(dmix, big["w_out"], "nt", F32, nm + "out_proj_dx")
        riders = (lambda names: reduce.take(names)) if reduce is not None else (lambda names: (None, None))
        if reduce is not None:
            reduce.submit(l, {k: grads[k][l] for k in ("ffn_w_down", "ffn_w_up", "w_out")})
        entries, riding = riders(("ffn_w_down", "ffn_w_up"))
        res = _dn_bwd(s["qkv"], s["p"], s["alog"], s["dtb"], s["dnw"], s["st"], dycat, nm + "dn", carry=riding)
        dqkv, dz, dbd, gal, gdt, gnw = res[:6]
        if entries:
            reduce.arrived(entries, res[6:])
        dpq, gconv = _dn_pre_bwd(s["p"], w["dn_conv_w"][l], dqkv, nm + "dn_pre")
        dpool, gpw, gpsc = _pool_bwd(s["p"], w["pool_w"][l], s["psc"], dycat, nm + "pool")
        dsq, dsk, dsv, gsk = _swa_bwd(s["p"], cosf, sins, w["swa_sinks"][l], s["y_swa"], s["lse"], dycat, nm + "swa")
        dp = jnp.concatenate([dpq, dz, dbd, dpool, dsq, dsk, dsv, jnp.zeros((t, 128), F32)], axis=1).astype(BF16)
        if held is not None:
            reduce.submit(*held)
        entries, riding = riders(("w_in",))
        res = _mm(s["h"], dp, "tn", BF16, nm + "in_proj_dw", scatter=riding)
        grads["w_in"][l] = res[0] if entries else res
        if entries:
            reduce.arrived(entries, res[1:])
        entries, riding = riders(("w_out",))
        res = _mm(dp, big["w_in"], "nt", BF16, nm + "in_proj_dx", scatter=riding)
        dh = res[0] if entries else res
        if entries:
            reduce.arrived(entries, res[1:])
        dx, g1 = _norm_bwd(s["x"], n1, dh, dx1, F32, nm + "norm1")
        held = (l, {"w_in": grads["w_in"][l]}) if reduce is not None else None
        grads["norm_mix_pre"][l], grads["norm_mix_post"][l] = g1[0], g2[0]
        grads["norm_ffn_pre"][l], grads["norm_ffn_post"][l] = g3[0], g4[0]
        grads["dn_conv_w"][l] = gconv
        grads["dn_a_log"][l], grads["dn_dt_bias"][l], grads["dn_norm_w"][l] = gal[0, :DN_HEADS], gdt[0, :DN_HEADS], gnw[0]
        grads["pool_w"][l], grads["pool_scale"][l] = gpw, gpsc[0]
        grads["swa_sinks"][l] = gsk[:, 0, :SWA_GROUP].reshape(SWA_HEADS)
    if held is not None:
        reduce.submit(*held)
    return loss, dx, grads


def _flat2(a):
    return a.reshape(math.prod(a.shape[:-1]), a.shape[-1])


def _ew_rows(rows, cols, n_arrays):
    for cand in (512, 256, 128, 64, 32, 16):
        if rows % cand == 0 and cand * cols * 4 * n_arrays <= (8 << 20):
            return cand
    return rows


def _spread_shard(a, layer, kind, place, dtype, name):
    _, rows, cols = a.shape
    r = _ew_rows(rows, cols, 2)
    nb = rows // r

    def body(s_ref, a_ref, o_ref):
        o_ref[...] = a_ref[...].astype(o_ref.dtype)

    if kind == "row":
        out_spec = pl.BlockSpec((r, cols), lambda i, s: (s[0] * nb + i, 0))
        out_shape = (4 * rows, cols)
    else:
        out_spec = pl.BlockSpec((r, cols), lambda i, s: (i, s[0]))
        out_shape = (rows, 4 * cols)
    return pl.pallas_call(
        body, name=name,
        grid_spec=pltpu.PrefetchScalarGridSpec(
            num_scalar_prefetch=1, grid=(nb,),
            in_specs=[pl.BlockSpec((None, r, cols), lambda i, s: (layer, i, 0))], out_specs=out_spec),
        out_shape=jax.ShapeDtypeStruct(out_shape, dtype), compiler_params=_cp(("parallel",)),
    )(place, a)


def _chip_sum(mine, sib, place, name):
    parts, rows, cols = sib.shape
    r = _ew_rows(rows, cols, 3)
    nb = rows // r

    def body(s_ref, a_ref, b_ref, o_ref):
        o_ref[...] = (a_ref[...].astype(F32) + b_ref[...].astype(F32)).astype(o_ref.dtype)

    spec = pl.BlockSpec((None, r, cols), lambda j, i, s: (j, i, 0))
    return pl.pallas_call(
        body, name=name,
        grid_spec=pltpu.PrefetchScalarGridSpec(
            num_scalar_prefetch=1, grid=(parts, nb),
            in_specs=[pl.BlockSpec((None, r, cols), lambda j, i, s: (j, s[1] * nb + i, 0)), spec], out_specs=spec),
        out_shape=jax.ShapeDtypeStruct(sib.shape, BF16), compiler_params=_cp(("parallel", "parallel")),
    )(place, mine, sib)


def _sum_slots(a, name):
    s = a.shape[0]
    a3 = a.reshape(s, math.prod(a.shape[1:-1]), a.shape[-1])
    _, rows, cols = a3.shape
    r = _ew_rows(rows, cols, s + 1)

    def body(a_ref, o_ref):
        acc = a_ref[0].astype(F32)
        for k in range(1, s):
            acc = acc + a_ref[k].astype(F32)
        o_ref[...] = acc

    return pl.pallas_call(body, name=name, grid=(rows // r,),
                          in_specs=[pl.BlockSpec((s, r, cols), lambda i: (0, i, 0))],
                          out_specs=pl.BlockSpec((r, cols), lambda i: (i, 0)),
                          out_shape=jax.ShapeDtypeStruct((rows, cols), F32), compiler_params=_cp(("parallel",)),
                          )(a3).reshape(a.shape[1:])


def _owner_sum(own, got, kind, place, into, name):
    buf, slab = into
    _, rows, cols = got.shape
    r = _ew_rows(rows, cols, 6)
    nb = rows // r

    def body(s_ref, own_ref, got_ref, buf_ref, o_ref):
        acc = own_ref[...].astype(F32)
        for k in range(3):
            acc = acc + got_ref[k].astype(F32)
        o_ref[...] = acc

    if kind == "row":
        own_spec = pl.BlockSpec((None, r, cols), lambda i, s: (s[0], i, 0))
    else:
        own_spec = pl.BlockSpec((None, r, cols), lambda i, s: (0, i, s[0]))
    return pl.pallas_call(
        body, name=name,
        grid_spec=pltpu.PrefetchScalarGridSpec(
            num_scalar_prefetch=1, grid=(nb,),
            in_specs=[own_spec, pl.BlockSpec((3, r, cols), lambda i, s: (0, i, 0)), pl.BlockSpec(memory_space=pl.ANY)],
            out_specs=pl.BlockSpec((None, r, cols), lambda i, s: (slab, s[1] * nb + i, 0))),
        out_shape=jax.ShapeDtypeStruct(buf.shape, buf.dtype), input_output_aliases={3: 0},
        compiler_params=_cp(("parallel",)),
    )(place, own, got, buf)


MESH = pl.DeviceIdType.MESH
ANY = pl.BlockSpec(memory_space=pl.ANY)


def _place():
    x, y, c = lax.axis_index("x"), lax.axis_index("y"), lax.axis_index("c")
    chips = [(1 - x, y), (x, 1 - y), (1 - x, 1 - y)]
    return x, y, c, chips


def _half_part(ref, kind, chip, half):
    if kind == "row":
        h = ref.shape[0] // 8
        return ref.at[pl.ds(pl.multiple_of((2 * chip + half) * h, 16), h), :]
    h, width = ref.shape[0] // 2, ref.shape[1] // 4
    return ref.at[pl.ds(pl.multiple_of(half * h, 16), h), pl.ds(pl.multiple_of(chip * width, 128), width)]


def _gather_copies(w_ref, kind, send, recv):
    x, y, c, chips = _place()
    mine = _half_part(w_ref, kind, 2 * x + y, c)
    return [pltpu.make_async_remote_copy(mine, mine, send.at[j], recv.at[j], device_id=(px, py, c), device_id_type=MESH)
            for j, (px, py) in enumerate(chips)]


def _gather_ici(arrs, kinds, name):
    na = len(arrs)

    def body(*refs):
        outs, send, recv = refs[na:2 * na], refs[2 * na], refs[2 * na + 1]
        cps = [cp for k in range(na) for cp in _gather_copies(outs[k], kinds[k], send.at[k], recv.at[k])]
        for cp in cps:
            cp.start()
        for cp in cps:
            cp.wait()

    return pl.pallas_call(
        body, name=name, in_specs=[ANY] * na, out_specs=[ANY] * na,
        out_shape=[jax.ShapeDtypeStruct(a.shape, a.dtype) for a in arrs],
        input_output_aliases={k: k for k in range(na)},
        scratch_shapes=[pltpu.SemaphoreType.DMA((na, 3))] * 2,
    )(*arrs)


def _gather_pass(arrs, kinds, name):
    na = len(arrs)

    def body(*refs):
        outs, send, recv = refs[na:2 * na], refs[2 * na], refs[2 * na + 1]
        x, y, c, chips = _place()
        cps, arrivals = [], []
        for k in range(na):
            for j, (px, py) in enumerate(chips):
                mine = _half_part(outs[k], kinds[k], 2 * px + py, c)
                theirs = _half_part(outs[k], kinds[k], 2 * px + py, 1 - c)
                cps.append(pltpu.make_async_remote_copy(mine, mine, send.at[k, j], recv.at[k, j],
                                                        device_id=(x, y, 1 - c), device_id_type=MESH))
                arrivals.append(pltpu.make_async_remote_copy(theirs, theirs, send.at[k, j], recv.at[k, j],
                                                             device_id=(x, y, 1 - c), device_id_type=MESH))
        for cp in cps:
            cp.start()
        for cp, arrival in zip(cps, arrivals):
            cp.wait_send()
            arrival.wait_recv()

    return pl.pallas_call(
        body, name=name, in_specs=[ANY] * na, out_specs=[ANY] * na,
        out_shape=[jax.ShapeDtypeStruct(a.shape, a.dtype) for a in arrs],
        input_output_aliases={k: k for k in range(na)},
        scratch_shapes=[pltpu.SemaphoreType.DMA((na, 3))] * 2,
    )(*arrs)


def _swap_sibling(arrs, name):
    na = len(arrs)

    def body(*refs):
        ins, outs, send, recv = refs[:na], refs[na:2 * na], refs[2 * na], refs[2 * na + 1]
        x, y, c, _ = _place()
        cps = []
        for k in range(na):
            h = ins[k].shape[1] // 2
            cps.append(pltpu.make_async_remote_copy(ins[k].at[:, pl.ds(pl.multiple_of((1 - c) * h, 16), h), :], outs[k],
                                                    send.at[k], recv.at[k], device_id=(x, y, 1 - c), device_id_type=MESH))
        for cp in cps:
            cp.start()
        for cp in cps:
            cp.wait()

    return pl.pallas_call(
        body, name=name, in_specs=[ANY] * na, out_specs=[ANY] * na,
        out_shape=[jax.ShapeDtypeStruct((a.shape[0], a.shape[1] // 2, a.shape[2]), a.dtype) for a in arrs],
        scratch_shapes=[pltpu.SemaphoreType.DMA((na,))] * 2,
    )(*arrs)


def _scatter_shapes(sums, kinds):
    return [jax.ShapeDtypeStruct((3, a.shape[1], a.shape[2] if kind == "row" else a.shape[2] // 4), a.dtype)
            for a, kind in zip(sums, kinds)]


def _scatter_copies(srcs, dsts, kinds, send, recv):
    x, y, c, chips = _place()
    cps = []
    for k, (src, dst) in enumerate(zip(srcs, dsts)):
        for j, (px, py) in enumerate(chips):
            chip = 2 * px + py
            if kinds[k] == "row":
                part = src.at[chip]
            else:
                width = src.shape[2] // 4
                part = src.at[0, :, pl.ds(pl.multiple_of(chip * width, 128), width)]
            cps.append(pltpu.make_async_remote_copy(part, dst.at[j], send.at[k, j], recv.at[k, j],
                                                    device_id=(px, py, c), device_id_type=MESH))
    return cps


def _scatter_chips(sums, kinds, name):
    na = len(sums)

    def body(*refs):
        cps = _scatter_copies(refs[:na], refs[na:2 * na], kinds, refs[2 * na], refs[2 * na + 1])
        for cp in cps:
            cp.start()
        for cp in cps:
            cp.wait()

    return pl.pallas_call(
        body, name=name, in_specs=[ANY] * na, out_specs=[ANY] * na, out_shape=_scatter_shapes(sums, kinds),
        scratch_shapes=[pltpu.SemaphoreType.DMA((na, 3))] * 2,
    )(*sums)


def _join_halves(arrs, name):
    na = len(arrs)

    def body(*refs):
        outs, send, recv = refs[na:2 * na], refs[2 * na], refs[2 * na + 1]
        x, y, c, _ = _place()
        halves = [a.shape[1] // 2 for a in arrs]
        mine = [outs[k].at[:, pl.ds(pl.multiple_of(c * h, 8), h), :] for k, h in enumerate(halves)]
        theirs = [outs[k].at[:, pl.ds(pl.multiple_of((1 - c) * h, 8), h), :] for k, h in enumerate(halves)]
        cps = [pltpu.make_async_remote_copy(mine[k], mine[k], send.at[k], recv.at[k],
                                            device_id=(x, y, 1 - c), device_id_type=MESH) for k in range(na)]
        for cp in cps:
            cp.start()
        for k, cp in enumerate(cps):
            cp.wait_send()
            pltpu.make_async_remote_copy(theirs[k], theirs[k], send.at[k], recv.at[k],
                                         device_id=(x, y, 1 - c), device_id_type=MESH).wait_recv()

    return pl.pallas_call(
        body, name=name, in_specs=[ANY] * na, out_specs=[ANY] * na,
        out_shape=[jax.ShapeDtypeStruct(a.shape, a.dtype) for a in arrs],
        input_output_aliases={k: k for k in range(na)},
        scratch_shapes=[pltpu.SemaphoreType.DMA((na,))] * 2,
    )(*arrs)


def _gather_all(a, name):
    def body(a_ref, o_ref, send, recv, local):
        x, y, c, _ = _place()
        me = 4 * x + 2 * y + c
        mine = pltpu.make_async_copy(a_ref, o_ref.at[me], local)
        mine.start()
        cps = []
        for j in range(1, 8):
            peer = (x ^ (j >> 2), y ^ ((j >> 1) & 1), c ^ (j & 1))
            cps.append(pltpu.make_async_remote_copy(a_ref, o_ref.at[me], send.at[j - 1], recv.at[j - 1],
                                                    device_id=peer, device_id_type=MESH))
        for cp in cps:
            cp.start()
        for cp in cps:
            cp.wait()
        mine.wait()

    return pl.pallas_call(
        body, name=name, in_specs=[ANY], out_specs=ANY,
        out_shape=jax.ShapeDtypeStruct((8,) + a.shape, a.dtype),
        scratch_shapes=[pltpu.SemaphoreType.DMA((7,)), pltpu.SemaphoreType.DMA((7,)), pltpu.SemaphoreType.DMA],
    )(a)


def _pack(parts):
    flat = jnp.concatenate([p.reshape(-1) for p in parts])
    n = flat.shape[0]
    rows = -(-n // (PACK_ROWS * 128)) * PACK_ROWS
    return jnp.pad(flat, (0, rows * 128 - n)).reshape(rows, 128)


def _unpack(buf, like):
    flat, out, off = buf.reshape(-1), [], 0
    for p in like:
        out.append(flat[off:off + p.size].reshape(p.shape))
        off += p.size
    return out


def kernel(x, positions, norm_mix_pre, w_in, dn_conv_w, dn_a_log, dn_dt_bias, dn_norm_w, pool_w, pool_scale, swa_sinks, w_out, norm_mix_post, norm_ffn_pre, ffn_w_up, ffn_conv_w, ffn_conv_b, ffn_w_down, norm_ffn_post, loss_target, m_norm_mix_pre, m_w_in, m_dn_conv_w, m_dn_a_log, m_dn_dt_bias, m_dn_norm_w, m_pool_w, m_pool_scale, m_swa_sinks, m_w_out, m_norm_mix_post, m_norm_ffn_pre, m_ffn_w_up, m_ffn_conv_w, m_ffn_conv_b, m_ffn_w_down, m_norm_ffn_post, v_norm_mix_pre, v_w_in, v_dn_conv_w, v_dn_a_log, v_dn_dt_bias, v_dn_norm_w, v_pool_w, v_pool_scale, v_swa_sinks, v_w_out, v_norm_mix_post, v_norm_ffn_pre, v_ffn_w_up, v_ffn_conv_w, v_ffn_conv_b, v_ffn_w_down, v_norm_ffn_post):
    wts = dict(zip(WEIGHTS, (norm_mix_pre, w_in, dn_conv_w, dn_a_log, dn_dt_bias, dn_norm_w, pool_w, pool_scale, swa_sinks,
                             w_out, norm_mix_post, norm_ffn_pre, ffn_w_up, ffn_conv_w, ffn_conv_b, ffn_w_down, norm_ffn_post)))
    mom = dict(zip(WEIGHTS, (m_norm_mix_pre, m_w_in, m_dn_conv_w, m_dn_a_log, m_dn_dt_bias, m_dn_norm_w, m_pool_w, m_pool_scale,
                             m_swa_sinks, m_w_out, m_norm_mix_post, m_norm_ffn_pre, m_ffn_w_up, m_ffn_conv_w, m_ffn_conv_b,
                             m_ffn_w_down, m_norm_ffn_post)))
    var = dict(zip(WEIGHTS, (v_norm_mix_pre, v_w_in, v_dn_conv_w, v_dn_a_log, v_dn_dt_bias, v_dn_norm_w, v_pool_w, v_pool_scale,
                             v_swa_sinks, v_w_out, v_norm_mix_post, v_norm_ffn_pre, v_ffn_w_up, v_ffn_conv_w, v_ffn_conv_b,
                             v_ffn_w_down, v_norm_ffn_post)))
    c = lax.axis_index("c")
    chip = 2 * lax.axis_index("x") + lax.axis_index("y")
    place = jnp.stack([chip, c]).astype(jnp.int32)
    shards = dict(wts, w_in=jnp.pad(w_in, ((0, 0), (0, 0), (0, IN_SHARD_PAD - IN_SHARD))))
    mats = _WeightGather(shards, place)
    w = dict(wts)
    conv_like = [wts[k] for k in CONV]
    conv_all = _gather_all(_pack(conv_like), "gather_conv")
    for i, k in enumerate(CONV):
        w[k] = jnp.concatenate([_unpack(conv_all[2 * j], conv_like)[i] for j in range(4)], axis=2)

    reduce = _GradReduce(place, {k: shards[k].shape for k in BIG})
    loss, dx, grads = _local_step(x[0], positions[0], loss_target[0], w, mats, reduce)
    loss = lax.psum(loss[0, 0], ("x", "y", "c"))
    g_big = reduce.finish()
    g_big["w_in"] = g_big["w_in"][..., :IN_SHARD]

    small_like = [wts[k] for k in SMALL]
    full_like = small_like + [w[k] for k in CONV]
    g_buf = _sum_slots(_gather_all(_pack([jnp.stack(grads[k]) for k in SMALL + CONV]), "gather_small"), "sum_small")
    g_small = dict(zip(SMALL + CONV, _unpack(g_buf, full_like)))
    for k in CONV:
        width = wts[k].shape[2]
        g_small[k] = lax.dynamic_slice_in_dim(g_small[k], chip * width, width, 2)
    pk = lambda d: _pack([d[k] for k in SMALL + CONV])
    upd = _adamw(pk(wts), pk(g_small), pk(mom), pk(var), "adam_small")
    upd_small = [dict(zip(SMALL + CONV, _unpack(b, small_like + conv_like))) for b in upd]

    g_all, d_all, m_all, v_all = {}, {}, {}, {}
    for k in WEIGHTS:
        if k in BIG:
            g_all[k] = g_big[k]
            d_all[k], m_all[k], v_all[k] = _adamw(wts[k], g_big[k], mom[k], var[k], "adam_" + k)
        else:
            g_all[k], d_all[k], m_all[k], v_all[k] = g_small[k], upd_small[0][k], upd_small[1][k], upd_small[2][k]
    return (loss, dx[None], *[g_all[k] for k in WEIGHTS], *[d_all[k] for k in WEIGHTS],
            *[m_all[k] for k in WEIGHTS], *[v_all[k] for k in WEIGHTS])
```

```python
import functools
import math

import jax
import jax.numpy as jnp
from jax import lax
from jax.experimental import pallas as pl
from jax.experimental.pallas import tpu as pltpu

F32 = jnp.float32
BF16 = jnp.bfloat16

HEAD_DIM = 128
DN_HEADS = 6
DN_CONV = 4
DN_CHUNK = 64
POOL_GROUPS = 4
SWA_HEADS = 6
SWA_KV_HEADS = 2
SWA_GROUP = SWA_HEADS // SWA_KV_HEADS
SWA_BLOCK = 128
ROPE_THETA = 10000.0
FFN_CONV = 3
NORM_EPS = 1e-6
DN_W = DN_HEADS * HEAD_DIM
POOL_W = POOL_GROUPS * HEAD_DIM
SWA_W = SWA_HEADS * HEAD_DIM
SWA_KV_W = SWA_KV_HEADS * HEAD_DIM
MIX_W = DN_W + POOL_W + SWA_W
IN_TRUE = 3 * DN_W + DN_W + 2 * DN_HEADS + POOL_W + SWA_W + 2 * SWA_KV_W
GATE_END = 4 * DN_W + 2 * DN_HEADS
CB_Z = 18
CB_BD = 24
CB_POOL = 25
CB_SQ = 29
CB_SK = 35
CB_SV = 37
IN_PAD = 40 * 128
ADAM_LR, ADAM_B1, ADAM_B2, ADAM_EPS, ADAM_WD, ADAM_STEP = 0.001, 0.9, 0.999, 1e-08, 0.01, 10

VMEM_LIMIT = 48 * 1024 * 1024
PACK_ROWS = 512
MM_TK_MAX = 2816
HIGH = lax.Precision.HIGHEST


def _cp(sem):
    return pltpu.CompilerParams(dimension_semantics=sem, vmem_limit_bytes=VMEM_LIMIT)


def _tile(n, prefs):
    for p in prefs:
        if n % p == 0:
            return p
    return n


def _rows(t):
    return _tile(t, (256, 128))


_DN = {"nn": (((1,), (0,)), ((), ())), "nt": (((1,), (1,)), ((), ())), "tn": (((0,), (0,)), ((), ()))}


def _mm_operand(arr, pick, block, idx):
    if pick is None:
        return pl.BlockSpec(block, idx)
    if pick == "split":
        per = arr.shape[2] // block[1]

        def split_idx(i, j, kk):
            r, c = idx(i, j, kk)
            return lax.div(c, per), r, lax.rem(c, per)

        return pl.BlockSpec((None,) + block, split_idx)
    slab = pick[1]
    return pl.BlockSpec((None,) + block, lambda i, j, kk: (slab,) + idx(i, j, kk))


def _mm(a, b, mode, out_dtype, name, a_pick=None, b_pick=None, carry=None, scatter=None):
    def dims(arr, pick):
        r, c = arr.shape[-2:]
        return (r, c * arr.shape[0]) if pick == "split" else (r, c)

    (a0, a1), (b0, b1) = dims(a, a_pick), dims(b, b_pick)
    k, m = (a0, a1) if mode == "tn" else (a1, a0)
    n = b0 if mode == "nt" else b1
    lim = lambda arr, pick, is_last, full: arr.shape[2] if (pick == "split" and is_last) else full
    tm = _tile(lim(a, a_pick, mode == "tn", m), (1024, 512, 256, 128))
    tn = _tile(lim(b, b_pick, mode != "nt", n), (1408, 1280, 1024, 512, 256, 128))
    k_lim = min(lim(a, a_pick, mode != "tn", k), lim(b, b_pick, mode == "nt", k))
    tk = max([d for d in range(128, min(k_lim, MM_TK_MAX) + 1, 128) if k_lim % d == 0], default=k_lim)
    nk = k // tk

    grid = (m // tm, n // tn, nk)

    riding = carry is not None or scatter is not None
    ns = len(scatter[0]) if scatter is not None else 0

    def body(a_ref, b_ref, *rest):
        if carry is not None:
            _, o_ref, w_ref, *scratch = rest
            copies = _gather_copies(w_ref, carry[1], *scratch[-2:])
        elif scatter is not None:
            o_ref, scratch = rest[ns], rest[2 * ns + 1:]
            if scatter[1] is None:
                copies = _sibling_copies(rest[:ns], rest[ns + 1:2 * ns + 1], *scratch[-2:])
            else:
                copies = _scatter_copies(rest[:ns], rest[ns + 1:2 * ns + 1], scatter[1], *scratch[-2:])
        else:
            o_ref, *scratch = rest
        if riding:
            scratch = scratch[:-2]
            step = (pl.program_id(0) * grid[1] + pl.program_id(1)) * grid[2] + pl.program_id(2)

            @pl.when(step == 0)
            def _():
                for cp in copies:
                    cp.start()
        part = lax.dot_general(a_ref[...], b_ref[...], _DN[mode], preferred_element_type=F32)
        if nk == 1:
            o_ref[...] = part.astype(o_ref.dtype)
        else:
            acc_ref, = scratch
            kk = pl.program_id(2)

            @pl.when(kk == 0)
            def _():
                acc_ref[...] = part

            @pl.when(kk > 0)
            def _():
                acc_ref[...] += part

            @pl.when(kk == nk - 1)
            def _():
                o_ref[...] = acc_ref[...].astype(o_ref.dtype)
        if riding:
            @pl.when(step == grid[0] * grid[1] * grid[2] - 1)
            def _():
                for cp in copies:
                    cp.wait()

    if mode == "tn":
        a_spec = _mm_operand(a, a_pick, (tk, tm), lambda i, j, kk: (kk, i))
    else:
        a_spec = _mm_operand(a, a_pick, (tm, tk), lambda i, j, kk: (i, kk))
    if mode == "nt":
        b_spec = _mm_operand(b, b_pick, (tn, tk), lambda i, j, kk: (j, kk))
    else:
        b_spec = _mm_operand(b, b_pick, (tk, tn), lambda i, j, kk: (kk, j))
    scratch = [pltpu.VMEM((tm, tn), F32)] if nk > 1 else []
    out_spec = pl.BlockSpec((tm, tn), lambda i, j, kk: (i, j))
    out_shape = jax.ShapeDtypeStruct((m, n), out_dtype)
    if not riding:
        return pl.pallas_call(
            body, name=name, grid=grid, in_specs=[a_spec, b_spec], out_specs=out_spec, out_shape=out_shape,
            scratch_shapes=scratch, compiler_params=_cp(("parallel", "parallel", "arbitrary")),
        )(a, b)
    any_space = pl.BlockSpec(memory_space=pl.ANY)
    in_order = _cp(("arbitrary", "arbitrary", "arbitrary"))
    if carry is not None:
        return pl.pallas_call(
            body, name=name, grid=grid, in_specs=[a_spec, b_spec, any_space], out_specs=[out_spec, any_space],
            out_shape=[out_shape, jax.ShapeDtypeStruct(carry[0].shape, carry[0].dtype)], input_output_aliases={2: 1},
            scratch_shapes=scratch + [pltpu.SemaphoreType.DMA((3,))] * 2, compiler_params=in_order,
        )(a, b, carry[0])
    return pl.pallas_call(
        body, name=name, grid=grid, in_specs=[a_spec, b_spec] + [any_space] * ns, out_specs=[out_spec] + [any_space] * ns,
        out_shape=[out_shape] + (_sibling_shapes(scatter[0]) if scatter[1] is None else _scatter_shapes(*scatter)),
        scratch_shapes=scratch + [pltpu.SemaphoreType.DMA((ns, 3))] * 2, compiler_params=in_order,
    )(a, b, *scatter[0])


def _rms(x, w):
    return x * lax.rsqrt(jnp.mean(x * x, axis=-1, keepdims=True) + NORM_EPS) * w


def _norm_fwd(x, w, name):
    t, d = x.shape
    r = _rows(t)

    def body(x_ref, w_ref, h_ref):
        h_ref[...] = _rms(x_ref[...], w_ref[...]).astype(h_ref.dtype)

    return pl.pallas_call(
        body, name=name, grid=(t // r,),
        in_specs=[pl.BlockSpec((r, d), lambda i: (i, 0)), pl.BlockSpec((1, d), lambda i: (0, 0))],
        out_specs=pl.BlockSpec((r, d), lambda i: (i, 0)),
        out_shape=jax.ShapeDtypeStruct((t, d), BF16), compiler_params=_cp(("parallel",)),
    )(x, w)


def _resnorm_fwd(x, y, w, name):
    t, d = x.shape
    r = _rows(t)

    def body(x_ref, y_ref, w_ref, o_ref):
        o_ref[...] = x_ref[...] + _rms(y_ref[...], w_ref[...])

    return pl.pallas_call(
        body, name=name, grid=(t // r,),
        in_specs=[pl.BlockSpec((r, d), lambda i: (i, 0)), pl.BlockSpec((r, d), lambda i: (i, 0)),
                  pl.BlockSpec((1, d), lambda i: (0, 0))],
        out_specs=pl.BlockSpec((r, d), lambda i: (i, 0)),
        out_shape=jax.ShapeDtypeStruct((t, d), F32), compiler_params=_cp(("parallel",)),
    )(x, y, w)


def _norm_bwd(x, w, dh, add, out_dtype, name):
    t, d = x.shape
    r = _rows(t)
    has_add = add is not None

    def body(*refs):
        if has_add:
            x_ref, w_ref, dh_ref, add_ref, dx_ref, dw_ref = refs
        else:
            x_ref, w_ref, dh_ref, dx_ref, dw_ref = refs
        xv = x_ref[...]
        g = dh_ref[...].astype(F32)
        rs = lax.rsqrt(jnp.mean(xv * xv, axis=-1, keepdims=True) + NORM_EPS)
        xh = xv * rs
        gw = g * w_ref[...]
        dx = rs * (gw - xh * jnp.mean(gw * xh, axis=-1, keepdims=True))
        if has_add:
            dx = dx + add_ref[...]
        dx_ref[...] = dx.astype(dx_ref.dtype)

        @pl.when(pl.program_id(0) == 0)
        def _():
            dw_ref[...] = jnp.zeros_like(dw_ref)

        dw_ref[...] += jnp.sum(g * xh, axis=0, keepdims=True)

    row = pl.BlockSpec((r, d), lambda i: (i, 0))
    vec = pl.BlockSpec((1, d), lambda i: (0, 0))
    ins = [x, w, dh] + ([add] if has_add else [])
    return pl.pallas_call(
        body, name=name, grid=(t // r,),
        in_specs=[row, vec, row] + ([row] if has_add else []),
        out_specs=[row, vec],
        out_shape=[jax.ShapeDtypeStruct((t, d), out_dtype), jax.ShapeDtypeStruct((1, d), F32)],
        compiler_params=_cp(("arbitrary",)),
    )(*ins)


def _loss_head(y, target, name):
    t, d = y.shape
    r = _rows(t)

    def body(y_ref, t_ref, l_ref, g_ref):
        e = y_ref[...] - t_ref[...]
        g_ref[...] = e * (1.0 / d)

        @pl.when(pl.program_id(0) == 0)
        def _():
            l_ref[...] = jnp.zeros_like(l_ref)

        l_ref[...] += jnp.sum(e * e) * (0.5 / d)

    row = pl.BlockSpec((r, d), lambda i: (i, 0))
    return pl.pallas_call(
        body, name=name, grid=(t // r,), in_specs=[row, row],
        out_specs=[pl.BlockSpec((1, 128), lambda i: (0, 0)), row],
        out_shape=[jax.ShapeDtypeStruct((1, 128), F32), jax.ShapeDtypeStruct((t, d), F32)],
        compiler_params=_cp(("arbitrary",)),
    )(y, target)


def _down(x, s):
    return x if s == 0 else pltpu.roll(x, s, 0)


def _up(x, s):
    return x if s == 0 else pltpu.roll(x, x.shape[0] - s, 0)


def _halo(t, r, hh, tc, col):
    q = r // hh
    last = t // hh - 1
    tile = pl.BlockSpec((r, tc), lambda j, i: (i, col(j)))
    prev = pl.BlockSpec((hh, tc), lambda j, i: (jnp.maximum(i * q - 1, 0), col(j)))
    nxt = pl.BlockSpec((hh, tc), lambda j, i: (jnp.minimum((i + 1) * q, last), col(j)))
    return tile, prev, nxt


def _sig(x):
    return 1.0 / (1.0 + jnp.exp(-x))


def _dsilu(x, s):
    return s * (1.0 + x * (1.0 - s))


def _dn_pre_fwd(p, conv_w, name):
    t = p.shape[0]
    r = _rows(t)

    def body(x_ref, xp_ref, w_ref, o_ref):
        j, i = pl.program_id(0), pl.program_id(1)
        xe = jnp.concatenate([jnp.where(i == 0, 0.0, xp_ref[...]), x_ref[...]], axis=0)
        c = sum(_down(xe, DN_CONV - 1 - k) * w_ref[pl.ds(k, 1), :] for k in range(DN_CONV))[8:]
        a = c * _sig(c)
        for h in range(DN_HEADS):
            ah = a[:, h * 128:(h + 1) * 128]
            fac = lax.rsqrt(jnp.sum(ah * ah, axis=-1, keepdims=True) + NORM_EPS)
            o_ref[:, h * 128:(h + 1) * 128] = ah * jnp.where(j == 0, fac * HEAD_DIM ** -0.5, jnp.where(j == 1, fac, 1.0))

    tile, prev, _ = _halo(t, r, 8, DN_W, lambda j: j)
    return pl.pallas_call(
        body, name=name, grid=(3, t // r),
        in_specs=[tile, prev, pl.BlockSpec((DN_CONV, DN_W), lambda j, i: (0, j))],
        out_specs=tile, out_shape=jax.ShapeDtypeStruct((t, 3 * DN_W), F32),
        compiler_params=_cp(("parallel", "parallel")),
    )(p, p, conv_w)


def _dn_pre_bwd(p, conv_w, dqkv, name):
    t = p.shape[0]
    r = _rows(t)
    ni = t // r

    def body(x_ref, xp_ref, xn_ref, w_ref, d_ref, dn_ref, dx_ref, dw_ref):
        j, i = pl.program_id(0), pl.program_id(1)
        xe = jnp.concatenate([jnp.where(i == 0, 0.0, xp_ref[...]), x_ref[...], xn_ref[...]], axis=0)
        de = jnp.concatenate([jnp.zeros((8, DN_W), F32), d_ref[...], jnp.where(i == ni - 1, 0.0, dn_ref[...])], axis=0)
        xs = [_down(xe, DN_CONV - 1 - k) for k in range(DN_CONV)]
        c = sum(xs[k] * w_ref[pl.ds(k, 1), :] for k in range(DN_CONV))
        s = _sig(c)
        a = c * s
        das = []
        for h in range(DN_HEADS):
            ah, dh = a[:, h * 128:(h + 1) * 128], de[:, h * 128:(h + 1) * 128]
            fac = lax.rsqrt(jnp.sum(ah * ah, axis=-1, keepdims=True) + NORM_EPS)
            dnorm = fac * dh - ah * (fac * fac * fac) * jnp.sum(dh * ah, axis=-1, keepdims=True)
            das.append(jnp.where(j == 0, dnorm * HEAD_DIM ** -0.5, jnp.where(j == 1, dnorm, dh)))
        dc = jnp.concatenate(das, axis=1) * _dsilu(c, s)
        dx_ref[...] = sum(_up(dc, DN_CONV - 1 - k) * w_ref[pl.ds(k, 1), :] for k in range(DN_CONV))[8:8 + r]

        @pl.when(i == 0)
        def _():
            dw_ref[...] = jnp.zeros_like(dw_ref)

        for k in range(DN_CONV):
            dw_ref[pl.ds(k, 1), :] += jnp.sum((dc * xs[k])[8:8 + r], axis=0, keepdims=True)

    tile, prev, nxt = _halo(t, r, 8, DN_W, lambda j: j)
    wspec = pl.BlockSpec((DN_CONV, DN_W), lambda j, i: (0, j))
    return pl.pallas_call(
        body, name=name, grid=(3, ni),
        in_specs=[tile, prev, nxt, wspec, tile, nxt],
        out_specs=[tile, wspec],
        out_shape=[jax.ShapeDtypeStruct((t, 3 * DN_W), F32), jax.ShapeDtypeStruct((DN_CONV, 3 * DN_W), F32)],
        compiler_params=_cp(("parallel", "arbitrary")),
    )(p, p, p, conv_w, dqkv, dqkv)


def _ffn_act_fwd(up, cw, cb, name):
    t, f2 = up.shape
    f = f2 // 2
    r = _tile(t, (512, 256, 128))
    tc = _tile(f, (512, 256, 128))
    nj = f // tc

    def body(a_ref, ap_ref, b_ref, bp_ref, wa_ref, wb_ref, ca_ref, cb_ref, o_ref):
        i = pl.program_id(1)

        def conv(x_ref, xp_ref, w_ref, c_ref):
            xe = jnp.concatenate([jnp.where(i == 0, 0.0, xp_ref[...]), x_ref[...]], axis=0)
            return sum(_down(xe, FFN_CONV - 1 - k) * w_ref[pl.ds(k, 1), :] for k in range(FFN_CONV))[8:] + c_ref[...]

        ua = conv(a_ref, ap_ref, wa_ref, ca_ref)
        ub = conv(b_ref, bp_ref, wb_ref, cb_ref)
        o_ref[...] = (ua * _sig(ua) * ub).astype(o_ref.dtype)

    ta, pa, _ = _halo(t, r, 8, tc, lambda j: j)
    tb, pb, _ = _halo(t, r, 8, tc, lambda j: j + nj)
    wa = pl.BlockSpec((FFN_CONV, tc), lambda j, i: (0, j))
    wb = pl.BlockSpec((FFN_CONV, tc), lambda j, i: (0, j + nj))
    ca = pl.BlockSpec((1, tc), lambda j, i: (0, j))
    cbs = pl.BlockSpec((1, tc), lambda j, i: (0, j + nj))
    return pl.pallas_call(
        body, name=name, grid=(nj, t // r),
        in_specs=[ta, pa, tb, pb, wa, wb, ca, cbs], out_specs=ta,
        out_shape=jax.ShapeDtypeStruct((t, f), BF16), compiler_params=_cp(("parallel", "parallel")),
    )(up, up, up, up, cw, cw, cb, cb)


def _ffn_act_bwd(up, cw, cb, dact, name):
    t, f2 = up.shape
    f = f2 // 2
    r = _tile(t, (512, 256, 128))
    ni = t // r
    tc = _tile(f, (512, 256, 128))
    nj = f // tc

    def body(a_ref, ap_ref, an_ref, b_ref, bp_ref, bn_ref, wa_ref, wb_ref, ca_ref, cb_ref, d_ref, dn_ref,
             du_ref, dw_ref, dc_ref):
        i = pl.program_id(1)
        dua_ref, dub_ref, dwa_ref, dwb_ref, dca_ref, dcb_ref = (du_ref.at[0], du_ref.at[1], dw_ref.at[0], dw_ref.at[1],
                                                                  dc_ref.at[0], dc_ref.at[1])

        def ext(x_ref, xp_ref, xn_ref):
            return jnp.concatenate([jnp.where(i == 0, 0.0, xp_ref[...]), x_ref[...], xn_ref[...]], axis=0)

        ae, be = ext(a_ref, ap_ref, an_ref), ext(b_ref, bp_ref, bn_ref)
        as_ = [_down(ae, FFN_CONV - 1 - k) for k in range(FFN_CONV)]
        bs_ = [_down(be, FFN_CONV - 1 - k) for k in range(FFN_CONV)]
        ua = sum(as_[k] * wa_ref[pl.ds(k, 1), :] for k in range(FFN_CONV)) + ca_ref[...]
        ub = sum(bs_[k] * wb_ref[pl.ds(k, 1), :] for k in range(FFN_CONV)) + cb_ref[...]
        de = jnp.concatenate([jnp.zeros((8, tc), F32), d_ref[...].astype(F32),
                              jnp.where(i == ni - 1, 0.0, dn_ref[...].astype(F32))], axis=0)
        s = _sig(ua)
        dua = de * ub * _dsilu(ua, s)
        dub = de * ua * s
        dua_ref[...] = sum(_up(dua, FFN_CONV - 1 - k) * wa_ref[pl.ds(k, 1), :] for k in range(FFN_CONV))[8:8 + r].astype(dua_ref.dtype)
        dub_ref[...] = sum(_up(dub, FFN_CONV - 1 - k) * wb_ref[pl.ds(k, 1), :] for k in range(FFN_CONV))[8:8 + r].astype(dub_ref.dtype)

        @pl.when(i == 0)
        def _():
            dw_ref[...] = jnp.zeros_like(dw_ref)
            dc_ref[...] = jnp.zeros_like(dc_ref)

        for k in range(FFN_CONV):
            dwa_ref[pl.ds(k, 1), :] += jnp.sum((dua * as_[k])[8:8 + r], axis=0, keepdims=True)
            dwb_ref[pl.ds(k, 1), :] += jnp.sum((dub * bs_[k])[8:8 + r], axis=0, keepdims=True)
        dca_ref[...] += jnp.sum(dua[8:8 + r], axis=0, keepdims=True)
        dcb_ref[...] += jnp.sum(dub[8:8 + r], axis=0, keepdims=True)

    ta, pa, na = _halo(t, r, 8, tc, lambda j: j)
    tb, pb, nb = _halo(t, r, 8, tc, lambda j: j + nj)
    wa = pl.BlockSpec((FFN_CONV, tc), lambda j, i: (0, j))
    wb = pl.BlockSpec((FFN_CONV, tc), lambda j, i: (0, j + nj))
    ca = pl.BlockSpec((1, tc), lambda j, i: (0, j))
    cbs = pl.BlockSpec((1, tc), lambda j, i: (0, j + nj))
    return pl.pallas_call(
        body, name=name, grid=(nj, ni),
        in_specs=[ta, pa, na, tb, pb, nb, wa, wb, ca, cbs, ta, na],
        out_specs=[pl.BlockSpec((2, r, tc), lambda j, i: (0, i, j)), pl.BlockSpec((2, FFN_CONV, tc), lambda j, i: (0, 0, j)),
                   pl.BlockSpec((2, 1, tc), lambda j, i: (0, 0, j))],
        out_shape=[jax.ShapeDtypeStruct((2, t, f), BF16), jax.ShapeDtypeStruct((2, FFN_CONV, f), F32),
                   jax.ShapeDtypeStruct((2, 1, f), F32)],
        compiler_params=_cp(("parallel", "arbitrary")),
    )(up, up, up, up, up, up, cw, cw, cb, cb, dact, dact)


def _pool_pick(g, vals):
    return jnp.where(g == 0, vals[0], jnp.where(g == 1, vals[1], jnp.where(g == 2, vals[2], vals[3])))


def _pool_pre(xe, g, t0):
    s1 = xe + _down(xe, 1)
    s2 = s1 + _down(s1, 2)
    s3 = s2 + _down(s2, 4)
    s4 = s3 + _down(s3, 8)
    r = xe.shape[0] - 16
    pos = (t0 + lax.broadcasted_iota(jnp.int32, (r, 1), 0)).astype(F32)
    cnt = jnp.minimum(pos + 1.0, _pool_pick(g, (2.0, 4.0, 8.0, 16.0)))
    return _pool_pick(g, (s1, s2, s3, s4))[16:] / cnt - xe[16:]


def _pool_fwd(p, pool_w, scale, name):
    t = p.shape[0]
    r = _tile(t, (1024, 256, 128))

    def body(x_ref, xp_ref, w_ref, sc_ref, o_ref):
        g, i = pl.program_id(0), pl.program_id(1)
        xe = jnp.concatenate([jnp.where(i == 0, 0.0, xp_ref[...]), x_ref[...]], axis=0)
        pre = _pool_pre(xe, g, i * r)
        o_ref[...] = jnp.dot(pre, w_ref[0], preferred_element_type=F32) * sc_ref[...]

    tile, prev, _ = _halo(t, r, 16, 128, lambda j: CB_POOL + j)
    return pl.pallas_call(
        body, name=name, grid=(POOL_GROUPS, t // r),
        in_specs=[tile, prev, pl.BlockSpec((1, 128, 128), lambda j, i: (j, 0, 0)), pl.BlockSpec((1, 128), lambda j, i: (0, j))],
        out_specs=pl.BlockSpec((r, 128), lambda j, i: (i, j)),
        out_shape=jax.ShapeDtypeStruct((t, POOL_W), F32), compiler_params=_cp(("parallel", "parallel")),
    )(p, p, pool_w, scale)


def _pool_bwd(p, pool_w, scale, dycat, name):
    t = p.shape[0]
    r = _tile(t, (1024, 256, 128))
    ni = t // r

    def body(x_ref, xp_ref, w_ref, sc_ref, d_ref, dn_ref, dx_ref, dw_ref, dsc_ref):
        g, i = pl.program_id(0), pl.program_id(1)
        xe = jnp.concatenate([jnp.where(i == 0, 0.0, xp_ref[...]), x_ref[...]], axis=0)
        pre = _pool_pre(xe, g, i * r)
        w = w_ref[0]
        dy = d_ref[...]
        dye = jnp.concatenate([dy, jnp.where(i == ni - 1, 0.0, dn_ref[...])], axis=0)
        dpre = lax.dot_general(dye * sc_ref[...], w, _DN["nt"], preferred_element_type=F32)
        pos = (i * r + lax.broadcasted_iota(jnp.int32, (r + 16, 1), 0)).astype(F32)
        dm = dpre / jnp.minimum(pos + 1.0, _pool_pick(g, (2.0, 4.0, 8.0, 16.0)))
        a1 = dm + _up(dm, 1)
        a2 = a1 + _up(a1, 2)
        a3 = a2 + _up(a2, 4)
        a4 = a3 + _up(a3, 8)
        dx_ref[...] = (_pool_pick(g, (a1, a2, a3, a4)) - dpre)[:r]

        @pl.when(i == 0)
        def _():
            dw_ref[...] = jnp.zeros_like(dw_ref)
            dsc_ref[...] = jnp.zeros_like(dsc_ref)

        dw_ref[0] += lax.dot_general(pre, dy * sc_ref[...], _DN["tn"], preferred_element_type=F32)
        dsc_ref[...] += jnp.sum(dy * jnp.dot(pre, w, preferred_element_type=F32), axis=0, keepdims=True)

    tile, prev, _ = _halo(t, r, 16, 128, lambda j: CB_POOL + j)
    dtile, _, dnxt = _halo(t, r, 16, 128, lambda j: DN_W // 128 + j)
    wspec = pl.BlockSpec((1, 128, 128), lambda j, i: (j, 0, 0))
    sspec = pl.BlockSpec((1, 128), lambda j, i: (0, j))
    return pl.pallas_call(
        body, name=name, grid=(POOL_GROUPS, ni),
        in_specs=[tile, prev, wspec, sspec, dtile, dnxt],
        out_specs=[pl.BlockSpec((r, 128), lambda j, i: (i, j)), wspec, sspec],
        out_shape=[jax.ShapeDtypeStruct((t, POOL_W), F32), jax.ShapeDtypeStruct((POOL_GROUPS, 128, 128), F32),
                   jax.ShapeDtypeStruct((1, POOL_W), F32)],
        compiler_params=_cp(("parallel", "arbitrary")),
    )(p, p, pool_w, scale, dycat, dycat)


_DNB = {"nn": (((2,), (1,)), ((0,), (0,))), "nt": (((2,), (2,)), ((0,), (0,))), "tn": (((1,), (1,)), ((0,), (0,)))}


def _dot(a, b, mode="nn", precision=None):
    dn = _DNB[mode] if a.ndim == 3 else _DN[mode]
    return lax.dot_general(a, b, dn, precision=precision, preferred_element_type=F32)


@functools.partial(jax.custom_vjp, nondiff_argnums=(2,))
def _bdot(a, b, mode):
    return _dot(a.astype(BF16), b.astype(BF16), mode)


def _bdot_fwd(a, b, mode):
    return _bdot(a, b, mode), (a, b)


def _bdot_bwd(mode, res, g):
    a, b = res
    if mode == "nn":
        return _bdot(g, b, "nt"), _bdot(a, g, "tn")
    if mode == "nt":
        return _bdot(g, b, "nn"), _bdot(g, a, "tn")
    return _bdot(b, g, "nt"), _bdot(a, g, "nn")


_bdot.defvjp(_bdot_fwd, _bdot_bwd)


def _dn_consts():
    c = DN_CHUNK
    ii = lax.broadcasted_iota(jnp.int32, (c, c), 0)
    jj = lax.broadcasted_iota(jnp.int32, (c, c), 1)
    one, zero = jnp.ones((c, c), F32), jnp.zeros((c, c), F32)
    return dict(ltri=jnp.where(ii >= jj, one, zero), utri=jnp.where(ii <= jj, one, zero), ones=one,
                causal=ii >= jj, strict=ii > jj, eye=jnp.where(ii == jj, one, zero))


def _dn_chunk(q, k, v, z, bcol, acol, s_in, alog, dtb, nw, cs):
    c = DN_CHUNK
    hh = q.shape[0]
    per_head = lambda m: jnp.broadcast_to(m, (hh, c, c))
    beta = _sig(bcol)
    xa = acol + dtb
    g = -jnp.exp(alog) * (jnp.maximum(xa, 0.0) + jnp.log(1.0 + jnp.exp(-jnp.abs(xa))))
    gb = jnp.broadcast_to(g, (hh, c, HEAD_DIM))
    gbc = jnp.broadcast_to(g, (hh, c, c))
    gc = _dot(per_head(cs["ltri"]), gb, precision=HIGH)
    gcol = gc[:, :, :c]
    grow = jnp.swapaxes(gcol, 1, 2)
    decay = jnp.exp(jnp.where(cs["causal"], gcol - grow, -1e30))
    kb = k * beta
    vb = v * beta
    nil = -jnp.where(cs["strict"], _bdot(kb, k, "nt") * decay, 0.0)
    inv = cs["eye"] + nil
    powk = nil
    for _ in range(int(math.log2(c)) - 1):
        powk = _bdot(powk, powk, "nn")
        inv = _bdot(inv, cs["eye"] + powk, "nn")
    eg = jnp.exp(gc)
    u = _bdot(inv, vb, "nn")
    w = _bdot(inv, kb * eg, "nn")
    a = _bdot(q, k, "nt") * decay
    v_new = u - _bdot(w, s_in, "nn")
    o = _bdot(q * eg, s_in, "nn") + _bdot(a, v_new, "nn")
    glast = jnp.sum(gb, axis=1, keepdims=True)
    s_out = s_in * jnp.exp(glast) + _bdot(k * jnp.exp(glast - gc), v_new, "tn")
    on = o * lax.rsqrt(jnp.mean(o * o, axis=-1, keepdims=True) + NORM_EPS) * nw
    return on * (z * _sig(z)), s_out


def _lane_pick(x, lane, idx):
    return jnp.sum(jnp.where(lane == idx, x, 0.0), axis=1, keepdims=True)


def _dn_load(q_ref, k_ref, v_ref, z_ref, bd_ref, al_ref, dt_ref, nw_ref, s_in):
    lane = lax.broadcasted_iota(jnp.int32, (1, 128), 1)
    bd, al, dt = bd_ref[...], al_ref[...], dt_ref[...]
    heads = range(DN_HEADS)
    wide = lambda ref: jnp.stack([ref[:, h * 128:(h + 1) * 128] for h in heads], axis=0)
    col = lambda x, off: jnp.stack([_lane_pick(x, lane, off + h) for h in heads], axis=0)
    return (wide(q_ref), wide(k_ref), wide(v_ref), wide(z_ref), col(bd, 0), col(bd, DN_HEADS), s_in,
            col(al, 0), col(dt, 0), nw_ref[...])


def _dn_fwd(qkv, p, alog, dtb, nw, name):
    t = qkv.shape[0]
    c = DN_CHUNK
    n = t // c

    def body(q_ref, k_ref, v_ref, z_ref, bd_ref, al_ref, dt_ref, nw_ref, y_ref, ss_ref, s_scr):
        @pl.when(pl.program_id(0) == 0)
        def _():
            s_scr[...] = jnp.zeros_like(s_scr)

        s_in = s_scr[...]
        y, s_out = _dn_chunk(*_dn_load(q_ref, k_ref, v_ref, z_ref, bd_ref, al_ref, dt_ref, nw_ref, s_in), _dn_consts())
        ss_ref[0] = s_in
        s_scr[...] = s_out
        for h in range(DN_HEADS):
            y_ref[:, h * 128:(h + 1) * 128] = y[h]

    wide = lambda j: pl.BlockSpec((c, DN_W), lambda i: (i, j))
    vec = pl.BlockSpec((1, 128), lambda i: (0, 0))
    return pl.pallas_call(
        body, name=name, grid=(n,),
        in_specs=[wide(0), wide(1), wide(2), wide(3), pl.BlockSpec((c, 128), lambda i: (i, CB_BD)), vec, vec, vec],
        out_specs=[wide(0), pl.BlockSpec((1, DN_HEADS, 128, 128), lambda i: (i, 0, 0, 0))],
        out_shape=[jax.ShapeDtypeStruct((t, DN_W), F32), jax.ShapeDtypeStruct((n, DN_HEADS, 128, 128), F32)],
        scratch_shapes=[pltpu.VMEM((DN_HEADS, 128, 128), F32)],
        compiler_params=_cp(("arbitrary",)),
    )(qkv, qkv, qkv, p, p, alog, dtb, nw)


def _dn_bwd(qkv, p, alog, dtb, nw, states, dycat, name, carry=None):
    t = qkv.shape[0]
    c = DN_CHUNK
    n = t // c
    sums, kinds = carry if carry is not None else ((), ())
    na = len(sums)

    def body(*refs):
        q_ref, k_ref, v_ref, z_ref, bd_ref, al_ref, dt_ref, nw_ref, ss_ref, dy_ref = refs[:10]
        dqkv_ref, dz_ref, dbd_ref, dal_ref, ddt_ref, dnw_ref = refs[10 + na:16 + na]
        ds_scr = refs[16 + 2 * na]
        if na:
            copies = _scatter_copies(refs[10:10 + na], refs[16 + na:16 + 2 * na], kinds, *refs[17 + 2 * na:])

        @pl.when(pl.program_id(0) == 0)
        def _():
            ds_scr[...] = jnp.zeros_like(ds_scr)
            dal_ref[...] = jnp.zeros_like(dal_ref)
            ddt_ref[...] = jnp.zeros_like(ddt_ref)
            dnw_ref[...] = jnp.zeros_like(dnw_ref)
            if na:
                for cp in copies:
                    cp.start()

        lane = lax.broadcasted_iota(jnp.int32, (1, 128), 1)
        args = _dn_load(q_ref, k_ref, v_ref, z_ref, bd_ref, al_ref, dt_ref, nw_ref, ss_ref[0])
        dy = jnp.stack([dy_ref[:, h * 128:(h + 1) * 128] for h in range(DN_HEADS)], axis=0)
        _, vjp = jax.vjp(functools.partial(_dn_chunk, cs=_dn_consts()), *args)
        gq, gk, gv, gz, gb, ga, gs, gal, gdt, gnw = vjp((dy, ds_scr[...]))
        ds_scr[...] = gs
        dbd = jnp.zeros((c, 128), F32)
        dal = jnp.zeros((1, 128), F32)
        ddt = jnp.zeros((1, 128), F32)
        for h in range(DN_HEADS):
            sl = slice(h * 128, (h + 1) * 128)
            dqkv_ref[:, sl] = gq[h]
            dqkv_ref[:, DN_W + h * 128:DN_W + (h + 1) * 128] = gk[h]
            dqkv_ref[:, 2 * DN_W + h * 128:2 * DN_W + (h + 1) * 128] = gv[h]
            dz_ref[:, sl] = gz[h]
            dbd = dbd + jnp.where(lane == h, gb[h], 0.0) + jnp.where(lane == DN_HEADS + h, ga[h], 0.0)
            dal = dal + jnp.where(lane == h, gal[h], 0.0)
            ddt = ddt + jnp.where(lane == h, gdt[h], 0.0)
        dbd_ref[...] = dbd
        dal_ref[...] += dal
        ddt_ref[...] += ddt
        dnw_ref[...] += gnw

        if na:
            @pl.when(pl.program_id(0) == n - 1)
            def _():
                for cp in copies:
                    cp.wait()

    rev = lambda i: n - 1 - i
    wide = lambda j: pl.BlockSpec((c, DN_W), lambda i: (rev(i), j))
    vec = pl.BlockSpec((1, 128), lambda i: (0, 0))
    any_space = pl.BlockSpec(memory_space=pl.ANY)
    return pl.pallas_call(
        body, name=name, grid=(n,),
        in_specs=[wide(0), wide(1), wide(2), wide(3), pl.BlockSpec((c, 128), lambda i: (rev(i), CB_BD)), vec, vec, vec,
                  pl.BlockSpec((1, DN_HEADS, 128, 128), lambda i: (rev(i), 0, 0, 0)), wide(0)] + [any_space] * na,
        out_specs=[pl.BlockSpec((c, 3 * DN_W), lambda i: (rev(i), 0)), wide(0),
                   pl.BlockSpec((c, 128), lambda i: (rev(i), 0)), vec, vec, vec] + [any_space] * na,
        out_shape=[jax.ShapeDtypeStruct((t, 3 * DN_W), F32), jax.ShapeDtypeStruct((t, DN_W), F32),
                   jax.ShapeDtypeStruct((t, 128), F32), jax.ShapeDtypeStruct((1, 128), F32),
                   jax.ShapeDtypeStruct((1, 128), F32), jax.ShapeDtypeStruct((1, 128), F32)] + _scatter_shapes(sums, kinds),
        scratch_shapes=[pltpu.VMEM((DN_HEADS, 128, 128), F32)] + ([pltpu.SemaphoreType.DMA((na, 3))] * 2 if na else []),
        compiler_params=_cp(("arbitrary",)),
    )(qkv, qkv, qkv, p, p, alog, dtb, nw, states, dycat, *sums)


def _rope(x, cosf, sins):
    return x * cosf + pltpu.roll(x, HEAD_DIM // 2, 1) * sins


def _rope_t(d, cosf, sins):
    return d * cosf + pltpu.roll(d * sins, HEAD_DIM // 2, 1)


def _swa_masks():
    b = SWA_BLOCK
    i = lax.broadcasted_iota(jnp.int32, (SWA_GROUP * b, b), 0) & (b - 1)
    j = lax.broadcasted_iota(jnp.int32, (SWA_GROUP * b, b), 1)
    return j > i, j <= i


def _swa_sink_col(sinks_ref, h):
    b = SWA_BLOCK
    r = lax.broadcasted_iota(jnp.int32, (SWA_GROUP * b, 1), 0)
    s = [sinks_ref[h * SWA_GROUP + g] for g in range(SWA_GROUP)]
    return jnp.where(r < b, s[0], jnp.where(r < 2 * b, s[1], s[2]))


def _swa_specs(t, h_first):
    nb = t // SWA_BLOCK

    def at(col, off):
        def imap(h, n):
            return (jnp.clip(n + off, 0, nb - 1), col(h))
        return pl.BlockSpec((SWA_BLOCK, 128), imap)
    return at


def _swa_fwd(p, cosf, sins, sinks, name):
    t = p.shape[0]
    b = SWA_BLOCK
    nb = t // b
    at = _swa_specs(t, None)
    scale = HEAD_DIM ** -0.5

    def body(q0, q1, q2, kp, kc, vp, vc, cc, sc, cp, sp, sinks_ref, o_ref, lse_ref):
        h, n = pl.program_id(0), pl.program_id(1)
        qs = jnp.concatenate([_rope(q[...], cc[...], sc[...]) for q in (q0, q1, q2)], axis=0)
        ks = jnp.concatenate([_rope(kp[...], cp[...], sp[...]), _rope(kc[...], cc[...], sc[...])], axis=0)
        vs = jnp.concatenate([vp[...], vc[...]], axis=0)
        mp, mc = _swa_masks()
        mask = jnp.concatenate([mp & (n > 0), mc], axis=1)
        s = jnp.where(mask, _dot(qs, ks, "nt") * scale, -1e30)
        sink = _swa_sink_col(sinks_ref, h)
        m = jnp.maximum(jnp.max(s, axis=1, keepdims=True), sink)
        e = jnp.exp(s - m)
        l = jnp.sum(e, axis=1, keepdims=True) + jnp.exp(sink - m)
        o = _dot(e, vs) / l
        lse = m + jnp.log(l)
        lane = lax.broadcasted_iota(jnp.int32, (1, 128), 1)
        tile = jnp.zeros((b, 128), F32)
        for g in range(SWA_GROUP):
            o_ref[:, g * 128:(g + 1) * 128] = o[g * b:(g + 1) * b]
            tile = tile + jnp.where(lane == g, lse[g * b:(g + 1) * b], 0.0)
        lse_ref[0] = tile

    qcol = lambda g: (lambda h: CB_SQ + h * SWA_GROUP + g)
    kcol, vcol, one = (lambda h: CB_SK + h), (lambda h: CB_SV + h), (lambda h: 0)
    in_specs = [at(qcol(0), 0), at(qcol(1), 0), at(qcol(2), 0), at(kcol, -1), at(kcol, 0), at(vcol, -1), at(vcol, 0),
                at(one, 0), at(one, 0), at(one, -1), at(one, -1), pl.BlockSpec(memory_space=pltpu.SMEM)]
    return pl.pallas_call(
        body, name=name, grid=(SWA_KV_HEADS, nb), in_specs=in_specs,
        out_specs=[pl.BlockSpec((b, SWA_GROUP * 128), lambda h, n: (n, h)), pl.BlockSpec((1, b, 128), lambda h, n: (h, n, 0))],
        out_shape=[jax.ShapeDtypeStruct((t, SWA_W), F32), jax.ShapeDtypeStruct((SWA_KV_HEADS, t, 128), F32)],
        compiler_params=_cp(("parallel", "parallel")),
    )(p, p, p, p, p, p, p, cosf, sins, cosf, sins, sinks)


def _swa_bwd(p, cosf, sins, sinks, o, lse, dycat, name):
    t = p.shape[0]
    b = SWA_BLOCK
    nb = t // b
    at = _swa_specs(t, None)
    scale = HEAD_DIM ** -0.5
    gb = SWA_GROUP * b

    def body(q0, q1, q2, r0, r1, r2, kp, kc, vp, vc, cc, sc, cp, sp, cn, sn, d0, d1, d2, e0, e1, e2,
             oc_ref, on_ref, lc_ref, ln_ref, sinks_ref, dq_ref, dk_ref, dv_ref, dsk_ref):
        h, n = pl.program_id(0), pl.program_id(1)
        lane = lax.broadcasted_iota(jnp.int32, (1, 128), 1)
        stack = lambda refs: jnp.concatenate([x[...] for x in refs], axis=0)
        q_c = jnp.concatenate([_rope(q[...], cc[...], sc[...]) for q in (q0, q1, q2)], axis=0)
        q_n = jnp.concatenate([_rope(q[...], cn[...], sn[...]) for q in (r0, r1, r2)], axis=0)
        k_p = _rope(kp[...], cp[...], sp[...])
        k_c = _rope(kc[...], cc[...], sc[...])
        do_c, do_n = stack((d0, d1, d2)), stack((e0, e1, e2))
        o_c = jnp.concatenate([oc_ref[:, g * 128:(g + 1) * 128] for g in range(SWA_GROUP)], axis=0)
        o_n = jnp.concatenate([on_ref[:, g * 128:(g + 1) * 128] for g in range(SWA_GROUP)], axis=0)
        lse_c = jnp.concatenate([_lane_pick(lc_ref[0], lane, g) for g in range(SWA_GROUP)], axis=0)
        lse_n = jnp.concatenate([_lane_pick(ln_ref[0], lane, g) for g in range(SWA_GROUP)], axis=0)
        dl_c = jnp.sum(do_c * o_c, axis=1, keepdims=True)
        dl_n = jnp.sum(do_n * o_n, axis=1, keepdims=True)
        mp, mc = _swa_masks()

        def pair(qr, kr, v, do, lse_, dl, mask):
            s = _dot(qr, kr, "nt") * scale
            pr = jnp.where(mask, jnp.exp(s - lse_), 0.0)
            ds = pr * (_dot(do, v, "nt") - dl) * scale
            return _dot(ds, kr), _dot(ds, qr, "tn"), _dot(pr, do, "tn")

        dq_a, _, _ = pair(q_c, k_p, vp[...], do_c, lse_c, dl_c, mp & (n > 0))
        dq_b, dk_b, dv_b = pair(q_c, k_c, vc[...], do_c, lse_c, dl_c, mc)
        _, dk_n, dv_n = pair(q_n, k_c, vc[...], do_n, lse_n, dl_n, mp & (n < nb - 1))
        dq = dq_a + dq_b
        for g in range(SWA_GROUP):
            dq_ref[:, g * 128:(g + 1) * 128] = _rope_t(dq[g * b:(g + 1) * b], cc[...], sc[...])
        dk_ref[...] = _rope_t(dk_b + dk_n, cc[...], sc[...])
        dv_ref[...] = dv_b + dv_n

        @pl.when(n == 0)
        def _():
            dsk_ref[...] = jnp.zeros_like(dsk_ref)

        w = -jnp.exp(_swa_sink_col(sinks_ref, h) - lse_c) * dl_c
        acc = jnp.zeros((1, 128), F32)
        for g in range(SWA_GROUP):
            acc = acc + jnp.where(lane == g, jnp.sum(w[g * b:(g + 1) * b], axis=0, keepdims=True), 0.0)
        dsk_ref[0] += jnp.broadcast_to(acc, (8, 128))

    qcol = lambda g: (lambda h: CB_SQ + h * SWA_GROUP + g)
    dcol = lambda g: (lambda h: (DN_W + POOL_W) // 128 + h * SWA_GROUP + g)
    kcol, vcol, one = (lambda h: CB_SK + h), (lambda h: CB_SV + h), (lambda h: 0)
    wide = lambda off: pl.BlockSpec((b, SWA_GROUP * 128), lambda h, n: (jnp.clip(n + off, 0, nb - 1), h))
    lspec = lambda off: pl.BlockSpec((1, b, 128), lambda h, n: (h, jnp.clip(n + off, 0, nb - 1), 0))
    in_specs = ([at(qcol(g), 0) for g in range(3)] + [at(qcol(g), 1) for g in range(3)]
                + [at(kcol, -1), at(kcol, 0), at(vcol, -1), at(vcol, 0)]
                + [at(one, 0), at(one, 0), at(one, -1), at(one, -1), at(one, 1), at(one, 1)]
                + [at(dcol(g), 0) for g in range(3)] + [at(dcol(g), 1) for g in range(3)]
                + [wide(0), wide(1), lspec(0), lspec(1), pl.BlockSpec(memory_space=pltpu.SMEM)])
    kv_out = pl.BlockSpec((b, 128), lambda h, n: (n, h))
    return pl.pallas_call(
        body, name=name, grid=(SWA_KV_HEADS, nb), in_specs=in_specs,
        out_specs=[wide(0), kv_out, kv_out, pl.BlockSpec((1, 8, 128), lambda h, n: (h, 0, 0))],
        out_shape=[jax.ShapeDtypeStruct((t, SWA_W), F32), jax.ShapeDtypeStruct((t, SWA_KV_W), F32),
                   jax.ShapeDtypeStruct((t, SWA_KV_W), F32), jax.ShapeDtypeStruct((SWA_KV_HEADS, 8, 128), F32)],
        compiler_params=_cp(("parallel", "arbitrary")),
    )(*([p] * 10), cosf, sins, cosf, sins, cosf, sins, *([dycat] * 6), o, o, lse, lse, sinks)


def _adam_math(w, g, m, v):
    m = ADAM_B1 * m + (1.0 - ADAM_B1) * g
    v = ADAM_B2 * v + (1.0 - ADAM_B2) * (g * g)
    m_hat = m / (1.0 - ADAM_B1 ** ADAM_STEP)
    v_hat = v / (1.0 - ADAM_B2 ** ADAM_STEP)
    return -ADAM_LR * (m_hat / (jnp.sqrt(v_hat) + ADAM_EPS) + ADAM_WD * w), m, v


def _adamw(w, g, m, v, name):
    shape = w.shape
    cols = shape[-1]
    rows = math.prod(shape[:-1])
    flat = lambda a: a.reshape(rows, cols)
    r = rows
    for cand in (512, 256, 128, 64, 32, 16, 8):
        if rows % cand == 0 and cand * cols * 4 <= (1 << 20):
            r = cand
            break

    def body(w_ref, g_ref, m_ref, v_ref, d_ref, nm_ref, nv_ref):
        d_ref[...], nm_ref[...], nv_ref[...] = _adam_math(w_ref[...], g_ref[...], m_ref[...], v_ref[...])

    spec = pl.BlockSpec((r, cols), lambda i: (i, 0))
    outs = pl.pallas_call(
        body, name=name, grid=(rows // r,), in_specs=[spec] * 4, out_specs=[spec] * 3,
        out_shape=[jax.ShapeDtypeStruct((rows, cols), F32)] * 3, compiler_params=_cp(("parallel",)),
    )(flat(w), flat(g), flat(m), flat(v))
    return tuple(o.reshape(shape) for o in outs)


BIG = ("w_in", "w_out", "ffn_w_up", "ffn_w_down")
CONV = ("dn_conv_w", "ffn_conv_w")
KIND = {"w_in": "col", "w_out": "row", "ffn_w_up": "col", "ffn_w_down": "row"}
SMALL = ("norm_mix_pre", "dn_a_log", "dn_dt_bias", "dn_norm_w", "pool_w", "pool_scale", "swa_sinks",
         "norm_mix_post", "norm_ffn_pre", "ffn_conv_b", "norm_ffn_post")
WEIGHTS = ("norm_mix_pre", "w_in", "dn_conv_w", "dn_a_log", "dn_dt_bias", "dn_norm_w", "pool_w", "pool_scale",
           "swa_sinks", "w_out", "norm_mix_post", "norm_ffn_pre", "ffn_w_up", "ffn_conv_w", "ffn_conv_b",
           "ffn_w_down", "norm_ffn_post")


def _pad_in(w):
    z = lambda n: jnp.zeros(w.shape[:-1] + (n,), w.dtype)
    return jnp.concatenate([w[..., :GATE_END], z(CB_POOL * 128 - GATE_END), w[..., GATE_END:],
                            z(IN_PAD - CB_POOL * 128 - (IN_TRUE - GATE_END))], axis=-1)


def _unpad_in(g):
    return jnp.concatenate([g[..., :GATE_END], g[..., CB_POOL * 128:CB_POOL * 128 + IN_TRUE - GATE_END]], axis=-1)


IN_SHARD = IN_TRUE // 4
IN_SHARD_PAD = -(-IN_SHARD // 128) * 128


def _in_runs():
    cuts = sorted({0, IN_TRUE, GATE_END} | {j * IN_SHARD for j in range(4)})
    runs = []
    for t0, t1 in zip(cuts[:-1], cuts[1:]):
        chip = t0 // IN_SHARD
        runs.append((chip * IN_SHARD_PAD + t0 - chip * IN_SHARD, t0 if t0 < GATE_END else t0 + CB_POOL * 128 - GATE_END, t1 - t0))
    return runs


def _remap_cols(x, runs, out_cols, name):
    rows, cols = x.shape
    r = _rows(rows)
    pieces = {}
    for src, dst, n in runs:
        while n > 0:
            step = min(n, 128 - src % 128, 128 - dst % 128)
            pieces.setdefault(dst // 128, []).append((src // 128, src % 128, dst % 128, step))
            src, dst, n = src + step, dst + step, n - step

    def body(x_ref, o_ref):
        lane = lax.broadcasted_iota(jnp.int32, (1, 128), 1)
        for ob in range(out_cols // 128):
            acc = jnp.zeros((r, 128), F32)
            for ib, ls, ld, n in pieces.get(ob, ()):
                blk = x_ref[:, ib * 128:(ib + 1) * 128].astype(F32)
                moved = blk if ls == ld else pltpu.roll(blk, (ld - ls) % 128, 1)
                acc = jnp.where((lane >= ld) & (lane < ld + n), moved, acc)
            o_ref[:, ob * 128:(ob + 1) * 128] = acc.astype(o_ref.dtype)

    return pl.pallas_call(
        body, name=name, grid=(rows // r,), in_specs=[pl.BlockSpec((r, cols), lambda i: (i, 0))],
        out_specs=pl.BlockSpec((r, out_cols), lambda i: (i, 0)),
        out_shape=jax.ShapeDtypeStruct((rows, out_cols), x.dtype), compiler_params=_cp(("parallel",)),
    )(x)


def _lanes(v):
    return jnp.zeros((1, 128), F32).at[0, :v.shape[0]].set(v)


def _rope_tables(positions):
    inv_freq = 1.0 / (ROPE_THETA ** (jnp.arange(0, HEAD_DIM, 2, dtype=F32) / HEAD_DIM))
    ang = positions.astype(F32)[:, None] * inv_freq
    cos, sin = jnp.cos(ang), jnp.sin(ang)
    return jnp.concatenate([cos, cos], axis=-1), jnp.concatenate([-sin, sin], axis=-1)


class _GradReduce:
    def __init__(self, place, shard_shapes):
        self.place = place
        self.out = {k: lax.empty(shard_shapes[k], F32) for k in BIG}
        self.pending = []

    def parts(self, l, dw):
        def view(k, g):
            if KIND[k] == "row":
                return g.reshape(4, -1, g.shape[1])
            if k == "w_in":
                g = _remap_cols(g, [(seg, chip, n) for chip, seg, n in _in_runs()], 4 * IN_SHARD_PAD, f"l{l}_w_in_grad_layout")
            return g[None]

        names = [k for k in BIG if k in dw]
        return names, [view(k, dw[k]) for k in names]

    def summed(self, l, names, mine, got):
        self.pending += [(l, k, _chip_sum(a, b, self.place, f"l{l}_chip_sum_{k}")) for k, a, b in zip(names, mine, got)]

    def submit(self, l, dw):
        names, mine = self.parts(l, dw)
        self.summed(l, names, mine, _swap_sibling(mine, f"l{l}_" + "_".join(names) + "_to_sibling"))

    def take(self, names):
        entries = [e for e in self.pending if e[1] in names]
        if not entries:
            return None, None
        self.pending = [e for e in self.pending if e[1] not in names]
        return entries, ([s for _, _, s in entries], [KIND[k] for _, k, _ in entries])

    def arrived(self, entries, got):
        for (l, k, own), g in zip(entries, got):
            self.out[k] = _owner_sum(own, g, KIND[k], self.place, (self.out[k], l), f"l{l}_owner_sum_{k}")

    def finish(self):
        entries, (sums, kinds) = self.take(BIG)
        self.arrived(entries, _scatter_chips(sums, kinds, "last_grads_to_owner"))
        return dict(zip(BIG, _join_halves([self.out[k] for k in BIG], "grads_join")))


class _LayerWeights:
    def __init__(self, layers):
        self.layers = layers

    def layer(self, l):
        return self.layers[l]

    def carry(self, l, k):
        return None


class _WeightGather(_LayerWeights):
    def __init__(self, shards, place):
        depth = shards["w_out"].shape[0]
        self.kinds = [KIND[k] for k in BIG]
        self.raw = [{k: _spread_shard(shards[k], l, KIND[k], place, BF16, f"l{l}_cast_{k}") for k in BIG} for l in range(depth)]
        first = _gather_ici([self.raw[0][k] for k in BIG], self.kinds, "l0_gather")
        self.layers = {0: self._passed(0, first)}

    def _passed(self, l, arrs):
        full = dict(zip(BIG, _gather_pass(arrs, self.kinds, f"l{l}_gather_pass")))
        full["w_in"] = _remap_cols(full["w_in"], _in_runs(), IN_PAD, f"l{l}_w_in_layout")
        return full

    def carry(self, l, k):
        return (self.raw[l + 1][k], KIND[k]) if l + 1 < len(self.raw) else None

    def carried(self, l, landed):
        self.layers[l + 1] = self._passed(l + 1, [landed[k] for k in BIG])


def _local_step(x, positions, target, w, mats, reduce=None):
    depth = w["norm_mix_pre"].shape[0]
    t = x.shape[0]
    cosf, sins = _rope_tables(positions)
    saved = []
    for l in range(depth):
        nm = f"l{l}_"
        n1, n2, n3, n4 = (w[k][l][None] for k in ("norm_mix_pre", "norm_mix_post", "norm_ffn_pre", "norm_ffn_post"))
        alog, dtb, dnw = _lanes(w["dn_a_log"][l]), _lanes(w["dn_dt_bias"][l]), w["dn_norm_w"][l][None]
        psc, cb = w["pool_scale"][l][None], w["ffn_conv_b"][l][None]
        big = mats.layer(l)
        landed = {}

        def project(a, k, name):
            riding = mats.carry(l, k)
            if riding is None:
                return _mm(a, big[k], "nn", F32, name)
            out, landed[k] = _mm(a, big[k], "nn", F32, name, carry=riding)
            return out

        h = _norm_fwd(x, n1, nm + "norm1")
        p = project(h, "w_in", nm + "in_proj")
        qkv = _dn_pre_fwd(p, w["dn_conv_w"][l], nm + "dn_pre")
        y_dn, st = _dn_fwd(qkv, p, alog, dtb, dnw, nm + "dn")
        y_pool = _pool_fwd(p, w["pool_w"][l], psc, nm + "pool")
        y_swa, lse = _swa_fwd(p, cosf, sins, w["swa_sinks"][l], nm + "swa")
        ycat = jnp.concatenate([y_dn, y_pool, y_swa], axis=1).astype(BF16)
        mix = project(ycat, "w_out", nm + "out_proj")
        x1 = _resnorm_fwd(x, mix, n2, nm + "res1")
        h2 = _norm_fwd(x1, n3, nm + "norm3")
        up = project(h2, "ffn_w_up", nm + "ffn_up")
        act = _ffn_act_fwd(up, w["ffn_conv_w"][l], cb, nm + "ffn_act")
        f = project(act, "ffn_w_down", nm + "ffn_down")
        if landed:
            mats.carried(l, landed)
        x2 = _resnorm_fwd(x1, f, n4, nm + "res2")
        saved.append(dict(x=x, h=h, p=p, qkv=qkv, st=st, y_swa=y_swa, lse=lse, ycat=ycat, mix=mix, x1=x1, h2=h2,
                          up=up, act=act, f=f, n=(n1, n2, n3, n4), alog=alog, dtb=dtb, dnw=dnw, psc=psc, cb=cb))
        x = x2
    loss, dx = _loss_head(x, target, "loss_head")
    grads = {k: [None] * depth for k in WEIGHTS}
    held = None
    for l in reversed(range(depth)):
        nm, s = f"l{l}_b_", saved[l]
        n1, n2, n3, n4 = s["n"]
        big = mats.layer(l)
        riders = (lambda names: reduce.take(names)) if reduce is not None else (lambda names: (None, None))

        def mm_swapping(layer, dw, *args, **kwargs):
            if reduce is None or not dw:
                return _mm(*args, **kwargs)
            names, mine = reduce.parts(layer, dw)
            res = _mm(*args, **kwargs, scatter=(mine, None))
            reduce.summed(layer, names, mine, res[1:])
            return res[0]

        df, g4 = _norm_bwd(s["f"], n4, dx, None, BF16, nm + "res2")
        dact = mm_swapping(l + 1, held, df, big["ffn_w_down"], "nt", F32, nm + "ffn_down_dx")
        grads["ffn_w_down"][l] = _mm(s["act"], df, "tn", BF16, nm + "ffn_down_dw")
        dup, dcw, dcb = _ffn_act_bwd(s["up"], w["ffn_conv_w"][l], s["cb"], dact, nm + "ffn_act")
        grads["ffn_conv_w"][l] = jnp.concatenate([dcw[0], dcw[1]], axis=1)
        grads["ffn_conv_b"][l] = jnp.concatenate([dcb[0], dcb[1]], axis=1)[0]
        grads["ffn_w_up"][l] = _mm(s["h2"], dup, "tn", BF16, nm + "ffn_up_dw", b_pick="split")
        dh2 = mm_swapping(l, {k: grads[k][l] for k in ("ffn_w_down", "ffn_w_up")},
                          dup, big["ffn_w_up"], "nt", BF16, nm + "ffn_up_dx", a_pick="split")
        dx1, g3 = _norm_bwd(s["x1"], n3, dh2, dx, F32, nm + "norm3")
        dmix, g2 = _norm_bwd(s["mix"], n2, dx1, None, BF16, nm + "res1")
        grads["w_out"][l] = _mm(s["ycat"], dmix, "tn", BF16, nm + "out_proj_dw")
        dycat = mm_swapping(l, {"w_out": grads["w_out"][l]}, dmix, big["w_out"], "nt", F32, nm + "out_proj_dx")
        entries, riding = riders(("ffn_w_down", "ffn_w_up"))
        res = _dn_bwd(s["qkv"], s["p"], s["alog"], s["dtb"], s["dnw"], s["st"], dycat, nm + "dn", carry=riding)
        dqkv, dz, dbd, gal, gdt, gnw = res[:6]
        if entries:
            reduce.arrived(entries, res[6:])
        dpq, gconv = _dn_pre_bwd(s["p"], w["dn_conv_w"][l], dqkv, nm + "dn_pre")
        dpool, gpw, gpsc = _pool_bwd(s["p"], w["pool_w"][l], s["psc"], dycat, nm + "pool")
        dsq, dsk, dsv, gsk = _swa_bwd(s["p"], cosf, sins, w["swa_sinks"][l], s["y_swa"], s["lse"], dycat, nm + "swa")
        dp = jnp.concatenate([dpq, dz, dbd, dpool, dsq, dsk, dsv, jnp.zeros((t, 128), F32)], axis=1).astype(BF16)
        entries, riding = riders(("w_in",))
        res = _mm(s["h"], dp, "tn", BF16, nm + "in_proj_dw", scatter=riding)
        grads["w_in"][l] = res[0] if entries else res
        if entries:
            reduce.arrived(entries, res[1:])
        entries, riding = riders(("w_out",))
        res = _mm(dp, big["w_in"], "nt", BF16, nm + "in_proj_dx", scatter=riding)
        dh = res[0] if entries else res
        if entries:
            reduce.arrived(entries, res[1:])
        dx, g1 = _norm_bwd(s["x"], n1, dh, dx1, F32, nm + "norm1")
        held = {"w_in": grads["w_in"][l]}
        grads["norm_mix_pre"][l], grads["norm_mix_post"][l] = g1[0], g2[0]
        grads["norm_ffn_pre"][l], grads["norm_ffn_post"][l] = g3[0], g4[0]
        grads["dn_conv_w"][l] = gconv
        grads["dn_a_log"][l], grads["dn_dt_bias"][l], grads["dn_norm_w"][l] = gal[0, :DN_HEADS], gdt[0, :DN_HEADS], gnw[0]
        grads["pool_w"][l], grads["pool_scale"][l] = gpw, gpsc[0]
        grads["swa_sinks"][l] = gsk[:, 0, :SWA_GROUP].reshape(SWA_HEADS)
    if reduce is not None:
        reduce.submit(0, held)
    return loss, dx, grads


def _flat2(a):
    return a.reshape(math.prod(a.shape[:-1]), a.shape[-1])


def _ew_rows(rows, cols, n_arrays):
    for cand in (512, 256, 128, 64, 32, 16):
        if rows % cand == 0 and cand * cols * 4 * n_arrays <= (8 << 20):
            return cand
    return rows


def _spread_shard(a, layer, kind, place, dtype, name):
    _, rows, cols = a.shape
    r = _ew_rows(rows, cols, 2)
    nb = rows // r

    def body(s_ref, a_ref, o_ref):
        o_ref[...] = a_ref[...].astype(o_ref.dtype)

    if kind == "row":
        out_spec = pl.BlockSpec((r, cols), lambda i, s: (s[0] * nb + i, 0))
        out_shape = (4 * rows, cols)
    else:
        out_spec = pl.BlockSpec((r, cols), lambda i, s: (i, s[0]))
        out_shape = (rows, 4 * cols)
    return pl.pallas_call(
        body, name=name,
        grid_spec=pltpu.PrefetchScalarGridSpec(
            num_scalar_prefetch=1, grid=(nb,),
            in_specs=[pl.BlockSpec((None, r, cols), lambda i, s: (layer, i, 0))], out_specs=out_spec),
        out_shape=jax.ShapeDtypeStruct(out_shape, dtype), compiler_params=_cp(("parallel",)),
    )(place, a)


def _chip_sum(mine, sib, place, name):
    parts, rows, cols = sib.shape
    r = _ew_rows(rows, cols, 3)
    nb = rows // r

    def body(s_ref, a_ref, b_ref, o_ref):
        o_ref[...] = (a_ref[...].astype(F32) + b_ref[...].astype(F32)).astype(o_ref.dtype)

    spec = pl.BlockSpec((None, r, cols), lambda j, i, s: (j, i, 0))
    return pl.pallas_call(
        body, name=name,
        grid_spec=pltpu.PrefetchScalarGridSpec(
            num_scalar_prefetch=1, grid=(parts, nb),
            in_specs=[pl.BlockSpec((None, r, cols), lambda j, i, s: (j, s[1] * nb + i, 0)), spec], out_specs=spec),
        out_shape=jax.ShapeDtypeStruct(sib.shape, BF16), compiler_params=_cp(("parallel", "parallel")),
    )(place, mine, sib)


def _sum_slots(a, name):
    s = a.shape[0]
    a3 = a.reshape(s, math.prod(a.shape[1:-1]), a.shape[-1])
    _, rows, cols = a3.shape
    r = _ew_rows(rows, cols, s + 1)

    def body(a_ref, o_ref):
        acc = a_ref[0].astype(F32)
        for k in range(1, s):
            acc = acc + a_ref[k].astype(F32)
        o_ref[...] = acc

    return pl.pallas_call(body, name=name, grid=(rows // r,),
                          in_specs=[pl.BlockSpec((s, r, cols), lambda i: (0, i, 0))],
                          out_specs=pl.BlockSpec((r, cols), lambda i: (i, 0)),
                          out_shape=jax.ShapeDtypeStruct((rows, cols), F32), compiler_params=_cp(("parallel",)),
                          )(a3).reshape(a.shape[1:])


def _owner_sum(own, got, kind, place, into, name):
    buf, slab = into
    _, rows, cols = got.shape
    r = _ew_rows(rows, cols, 6)
    nb = rows // r

    def body(s_ref, own_ref, got_ref, buf_ref, o_ref):
        acc = own_ref[...].astype(F32)
        for k in range(3):
            acc = acc + got_ref[k].astype(F32)
        o_ref[...] = acc

    if kind == "row":
        own_spec = pl.BlockSpec((None, r, cols), lambda i, s: (s[0], i, 0))
    else:
        own_spec = pl.BlockSpec((None, r, cols), lambda i, s: (0, i, s[0]))
    return pl.pallas_call(
        body, name=name,
        grid_spec=pltpu.PrefetchScalarGridSpec(
            num_scalar_prefetch=1, grid=(nb,),
            in_specs=[own_spec, pl.BlockSpec((3, r, cols), lambda i, s: (0, i, 0)), pl.BlockSpec(memory_space=pl.ANY)],
            out_specs=pl.BlockSpec((None, r, cols), lambda i, s: (slab, s[1] * nb + i, 0))),
        out_shape=jax.ShapeDtypeStruct(buf.shape, buf.dtype), input_output_aliases={3: 0},
        compiler_params=_cp(("parallel",)),
    )(place, own, got, buf)


MESH = pl.DeviceIdType.MESH
ANY = pl.BlockSpec(memory_space=pl.ANY)


def _place():
    x, y, c = lax.axis_index("x"), lax.axis_index("y"), lax.axis_index("c")
    chips = [(1 - x, y), (x, 1 - y), (1 - x, 1 - y)]
    return x, y, c, chips


def _half_part(ref, kind, chip, half):
    if kind == "row":
        h = ref.shape[0] // 8
        return ref.at[pl.ds(pl.multiple_of((2 * chip + half) * h, 16), h), :]
    h, width = ref.shape[0] // 2, ref.shape[1] // 4
    return ref.at[pl.ds(pl.multiple_of(half * h, 16), h), pl.ds(pl.multiple_of(chip * width, 128), width)]


def _gather_copies(w_ref, kind, send, recv):
    x, y, c, chips = _place()
    mine = _half_part(w_ref, kind, 2 * x + y, c)
    return [pltpu.make_async_remote_copy(mine, mine, send.at[j], recv.at[j], device_id=(px, py, c), device_id_type=MESH)
            for j, (px, py) in enumerate(chips)]


def _gather_ici(arrs, kinds, name):
    na = len(arrs)

    def body(*refs):
        outs, send, recv = refs[na:2 * na], refs[2 * na], refs[2 * na + 1]
        cps = [cp for k in range(na) for cp in _gather_copies(outs[k], kinds[k], send.at[k], recv.at[k])]
        for cp in cps:
            cp.start()
        for cp in cps:
            cp.wait()

    return pl.pallas_call(
        body, name=name, in_specs=[ANY] * na, out_specs=[ANY] * na,
        out_shape=[jax.ShapeDtypeStruct(a.shape, a.dtype) for a in arrs],
        input_output_aliases={k: k for k in range(na)},
        scratch_shapes=[pltpu.SemaphoreType.DMA((na, 3))] * 2,
    )(*arrs)


def _gather_pass(arrs, kinds, name):
    na = len(arrs)

    def body(*refs):
        outs, send, recv = refs[na:2 * na], refs[2 * na], refs[2 * na + 1]
        x, y, c, chips = _place()
        cps, arrivals = [], []
        for k in range(na):
            for j, (px, py) in enumerate(chips):
                mine = _half_part(outs[k], kinds[k], 2 * px + py, c)
                theirs = _half_part(outs[k], kinds[k], 2 * px + py, 1 - c)
                cps.append(pltpu.make_async_remote_copy(mine, mine, send.at[k, j], recv.at[k, j],
                                                        device_id=(x, y, 1 - c), device_id_type=MESH))
                arrivals.append(pltpu.make_async_remote_copy(theirs, theirs, send.at[k, j], recv.at[k, j],
                                                             device_id=(x, y, 1 - c), device_id_type=MESH))
        for cp in cps:
            cp.start()
        for cp, arrival in zip(cps, arrivals):
            cp.wait_send()
            arrival.wait_recv()

    return pl.pallas_call(
        body, name=name, in_specs=[ANY] * na, out_specs=[ANY] * na,
        out_shape=[jax.ShapeDtypeStruct(a.shape, a.dtype) for a in arrs],
        input_output_aliases={k: k for k in range(na)},
        scratch_shapes=[pltpu.SemaphoreType.DMA((na, 3))] * 2,
    )(*arrs)


def _swap_sibling(arrs, name):
    na = len(arrs)

    def body(*refs):
        cps = _sibling_copies(refs[:na], refs[na:2 * na], refs[2 * na], refs[2 * na + 1])
        for cp in cps:
            cp.start()
        for cp in cps:
            cp.wait()

    return pl.pallas_call(
        body, name=name, in_specs=[ANY] * na, out_specs=[ANY] * na, out_shape=_sibling_shapes(arrs),
        scratch_shapes=[pltpu.SemaphoreType.DMA((na, 1))] * 2,
    )(*arrs)


def _sibling_shapes(arrs):
    return [jax.ShapeDtypeStruct((a.shape[0], a.shape[1] // 2, a.shape[2]), a.dtype) for a in arrs]


def _sibling_copies(srcs, dsts, send, recv):
    x, y, c, _ = _place()
    cps = []
    for k, (src, dst) in enumerate(zip(srcs, dsts)):
        h = src.shape[1] // 2
        cps.append(pltpu.make_async_remote_copy(src.at[:, pl.ds(pl.multiple_of((1 - c) * h, 16), h), :], dst,
                                                send.at[k, 0], recv.at[k, 0], device_id=(x, y, 1 - c), device_id_type=MESH))
    return cps


def _scatter_shapes(sums, kinds):
    return [jax.ShapeDtypeStruct((3, a.shape[1], a.shape[2] if kind == "row" else a.shape[2] // 4), a.dtype)
            for a, kind in zip(sums, kinds)]


def _scatter_copies(srcs, dsts, kinds, send, recv):
    x, y, c, chips = _place()
    cps = []
    for k, (src, dst) in enumerate(zip(srcs, dsts)):
        for j, (px, py) in enumerate(chips):
            chip = 2 * px + py
            if kinds[k] == "row":
                part = src.at[chip]
            else:
                width = src.shape[2] // 4
                part = src.at[0, :, pl.ds(pl.multiple_of(chip * width, 128), width)]
            cps.append(pltpu.make_async_remote_copy(part, dst.at[j], send.at[k, j], recv.at[k, j],
                                                    device_id=(px, py, c), device_id_type=MESH))
    return cps


def _scatter_chips(sums, kinds, name):
    na = len(sums)

    def body(*refs):
        cps = _scatter_copies(refs[:na], refs[na:2 * na], kinds, refs[2 * na], refs[2 * na + 1])
        for cp in cps:
            cp.start()
        for cp in cps:
            cp.wait()

    return pl.pallas_call(
        body, name=name, in_specs=[ANY] * na, out_specs=[ANY] * na, out_shape=_scatter_shapes(sums, kinds),
        scratch_shapes=[pltpu.SemaphoreType.DMA((na, 3))] * 2,
    )(*sums)


def _join_halves(arrs, name):
    na = len(arrs)

    def body(*refs):
        outs, send, recv = refs[na:2 * na], refs[2 * na], refs[2 * na + 1]
        x, y, c, _ = _place()
        halves = [a.shape[1] // 2 for a in arrs]
        mine = [outs[k].at[:, pl.ds(pl.multiple_of(c * h, 8), h), :] for k, h in enumerate(halves)]
        theirs = [outs[k].at[:, pl.ds(pl.multiple_of((1 - c) * h, 8), h), :] for k, h in enumerate(halves)]
        cps = [pltpu.make_async_remote_copy(mine[k], mine[k], send.at[k], recv.at[k],
                                            device_id=(x, y, 1 - c), device_id_type=MESH) for k in range(na)]
        for cp in cps:
            cp.start()
        for k, cp in enumerate(cps):
            cp.wait_send()
            pltpu.make_async_remote_copy(theirs[k], theirs[k], send.at[k], recv.at[k],
                                         device_id=(x, y, 1 - c), device_id_type=MESH).wait_recv()

    return pl.pallas_call(
        body, name=name, in_specs=[ANY] * na, out_specs=[ANY] * na,
        out_shape=[jax.ShapeDtypeStruct(a.shape, a.dtype) for a in arrs],
        input_output_aliases={k: k for k in range(na)},
        scratch_shapes=[pltpu.SemaphoreType.DMA((na,))] * 2,
    )(*arrs)


def _gather_all(a, name):
    def body(a_ref, o_ref, send, recv, local):
        x, y, c, _ = _place()
        me = 4 * x + 2 * y + c
        mine = pltpu.make_async_copy(a_ref, o_ref.at[me], local)
        mine.start()
        cps = []
        for j in range(1, 8):
            peer = (x ^ (j >> 2), y ^ ((j >> 1) & 1), c ^ (j & 1))
            cps.append(pltpu.make_async_remote_copy(a_ref, o_ref.at[me], send.at[j - 1], recv.at[j - 1],
                                                    device_id=peer, device_id_type=MESH))
        for cp in cps:
            cp.start()
        for cp in cps:
            cp.wait()
        mine.wait()

    return pl.pallas_call(
        body, name=name, in_specs=[ANY], out_specs=ANY,
        out_shape=jax.ShapeDtypeStruct((8,) + a.shape, a.dtype),
        scratch_shapes=[pltpu.SemaphoreType.DMA((7,)), pltpu.SemaphoreType.DMA((7,)), pltpu.SemaphoreType.DMA],
    )(a)


def _pack(parts):
    flat = jnp.concatenate([p.reshape(-1) for p in parts])
    n = flat.shape[0]
    rows = -(-n // (PACK_ROWS * 128)) * PACK_ROWS
    return jnp.pad(flat, (0, rows * 128 - n)).reshape(rows, 128)


def _unpack(buf, like):
    flat, out, off = buf.reshape(-1), [], 0
    for p in like:
        out.append(flat[off:off + p.size].reshape(p.shape))
        off += p.size
    return out


def kernel(x, positions, norm_mix_pre, w_in, dn_conv_w, dn_a_log, dn_dt_bias, dn_norm_w, pool_w, pool_scale, swa_sinks, w_out, norm_mix_post, norm_ffn_pre, ffn_w_up, ffn_conv_w, ffn_conv_b, ffn_w_down, norm_ffn_post, loss_target, m_norm_mix_pre, m_w_in, m_dn_conv_w, m_dn_a_log, m_dn_dt_bias, m_dn_norm_w, m_pool_w, m_pool_scale, m_swa_sinks, m_w_out, m_norm_mix_post, m_norm_ffn_pre, m_ffn_w_up, m_ffn_conv_w, m_ffn_conv_b, m_ffn_w_down, m_norm_ffn_post, v_norm_mix_pre, v_w_in, v_dn_conv_w, v_dn_a_log, v_dn_dt_bias, v_dn_norm_w, v_pool_w, v_pool_scale, v_swa_sinks, v_w_out, v_norm_mix_post, v_norm_ffn_pre, v_ffn_w_up, v_ffn_conv_w, v_ffn_conv_b, v_ffn_w_down, v_norm_ffn_post):
    wts = dict(zip(WEIGHTS, (norm_mix_pre, w_in, dn_conv_w, dn_a_log, dn_dt_bias, dn_norm_w, pool_w, pool_scale, swa_sinks,
                             w_out, norm_mix_post, norm_ffn_pre, ffn_w_up, ffn_conv_w, ffn_conv_b, ffn_w_down, norm_ffn_post)))
    mom = dict(zip(WEIGHTS, (m_norm_mix_pre, m_w_in, m_dn_conv_w, m_dn_a_log, m_dn_dt_bias, m_dn_norm_w, m_pool_w, m_pool_scale,
                             m_swa_sinks, m_w_out, m_norm_mix_post, m_norm_ffn_pre, m_ffn_w_up, m_ffn_conv_w, m_ffn_conv_b,
                             m_ffn_w_down, m_norm_ffn_post)))
    var = dict(zip(WEIGHTS, (v_norm_mix_pre, v_w_in, v_dn_conv_w, v_dn_a_log, v_dn_dt_bias, v_dn_norm_w, v_pool_w, v_pool_scale,
                             v_swa_sinks, v_w_out, v_norm_mix_post, v_norm_ffn_pre, v_ffn_w_up, v_ffn_conv_w, v_ffn_conv_b,
                             v_ffn_w_down, v_norm_ffn_post)))
    c = lax.axis_index("c")
    chip = 2 * lax.axis_index("x") + lax.axis_index("y")
    place = jnp.stack([chip, c]).astype(jnp.int32)
    shards = dict(wts, w_in=jnp.pad(w_in, ((0, 0), (0, 0), (0, IN_SHARD_PAD - IN_SHARD))))
    mats = _WeightGather(shards, place)
    w = dict(wts)
    conv_like = [wts[k] for k in CONV]
    conv_all = _gather_all(_pack(conv_like), "gather_conv")
    for i, k in enumerate(CONV):
        w[k] = jnp.concatenate([_unpack(conv_all[2 * j], conv_like)[i] for j in range(4)], axis=2)

    reduce = _GradReduce(place, {k: shards[k].shape for k in BIG})
    loss, dx, grads = _local_step(x[0], positions[0], loss_target[0], w, mats, reduce)
    loss = lax.psum(loss[0, 0], ("x", "y", "c"))
    g_big = reduce.finish()
    g_big["w_in"] = g_big["w_in"][..., :IN_SHARD]

    small_like = [wts[k] for k in SMALL]
    full_like = small_like + [w[k] for k in CONV]
    g_buf = _sum_slots(_gather_all(_pack([jnp.stack(grads[k]) for k in SMALL + CONV]), "gather_small"), "sum_small")
    g_small = dict(zip(SMALL + CONV, _unpack(g_buf, full_like)))
    for k in CONV:
        width = wts[k].shape[2]
        g_small[k] = lax.dynamic_slice_in_dim(g_small[k], chip * width, width, 2)
    pk = lambda d: _pack([d[k] for k in SMALL + CONV])
    upd = _adamw(pk(wts), pk(g_small), pk(mom), pk(var), "adam_small")
    upd_small = [dict(zip(SMALL + CONV, _unpack(b, small_like + conv_like))) for b in upd]

    g_all, d_all, m_all, v_all = {}, {}, {}, {}
    for k in WEIGHTS:
        if k in BIG:
            g_all[k] = g_big[k]
            d_all[k], m_all[k], v_all[k] = _adamw(wts[k], g_big[k], mom[k], var[k], "adam_" + k)
        else:
            g_all[k], d_all[k], m_all[k], v_all[k] = g_small[k], upd_small[0][k], upd_small[1][k], upd_small[2][k]
    return (loss, dx[None], *[g_all[k] for k in WEIGHTS], *[d_all[k] for k in WEIGHTS],
            *[m_all[k] for k in WEIGHTS], *[v_all[k] for k in WEIGHTS])
```

```python
import functools
import math

import jax
import jax.numpy as jnp
from jax import lax
from jax.experimental import pallas as pl
from jax.experimental.pallas import tpu as pltpu

F32 = jnp.float32
BF16 = jnp.bfloat16

HEAD_DIM = 128
DN_HEADS = 6
DN_CONV = 4
DN_CHUNK = 64
POOL_GROUPS = 4
SWA_HEADS = 6
SWA_KV_HEADS = 2
SWA_GROUP = SWA_HEADS // SWA_KV_HEADS
SWA_BLOCK = 128
ROPE_THETA = 10000.0
FFN_CONV = 3
NORM_EPS = 1e-6
DN_W = DN_HEADS * HEAD_DIM
POOL_W = POOL_GROUPS * HEAD_DIM
SWA_W = SWA_HEADS * HEAD_DIM
SWA_KV_W = SWA_KV_HEADS * HEAD_DIM
MIX_W = DN_W + POOL_W + SWA_W
IN_TRUE = 3 * DN_W + DN_W + 2 * DN_HEADS + POOL_W + SWA_W + 2 * SWA_KV_W
GATE_END = 4 * DN_W + 2 * DN_HEADS
CB_Z = 18
CB_BD = 24
CB_POOL = 25
CB_SQ = 29
CB_SK = 35
CB_SV = 37
IN_PAD = 40 * 128
ADAM_LR, ADAM_B1, ADAM_B2, ADAM_EPS, ADAM_WD, ADAM_STEP = 0.001, 0.9, 0.999, 1e-08, 0.01, 10

VMEM_LIMIT = 48 * 1024 * 1024
PACK_ROWS = 512
MM_TK_MAX = 2816
HIGH = lax.Precision.HIGHEST


def _cp(sem):
    return pltpu.CompilerParams(dimension_semantics=sem, vmem_limit_bytes=VMEM_LIMIT)


def _tile(n, prefs):
    for p in prefs:
        if n % p == 0:
            return p
    return n


def _rows(t):
    return _tile(t, (256, 128))


_DN = {"nn": (((1,), (0,)), ((), ())), "nt": (((1,), (1,)), ((), ())), "tn": (((0,), (0,)), ((), ()))}


def _mm_operand(arr, pick, block, idx):
    if pick is None:
        return pl.BlockSpec(block, idx)
    if pick == "split":
        per = arr.shape[2] // block[1]

        def split_idx(i, j, kk):
            r, c = idx(i, j, kk)
            return lax.div(c, per), r, lax.rem(c, per)

        return pl.BlockSpec((None,) + block, split_idx)
    slab = pick[1]
    return pl.BlockSpec((None,) + block, lambda i, j, kk: (slab,) + idx(i, j, kk))


def _mm(a, b, mode, out_dtype, name, a_pick=None, b_pick=None, carry=None, scatter=None):
    def dims(arr, pick):
        r, c = arr.shape[-2:]
        return (r, c * arr.shape[0]) if pick == "split" else (r, c)

    (a0, a1), (b0, b1) = dims(a, a_pick), dims(b, b_pick)
    k, m = (a0, a1) if mode == "tn" else (a1, a0)
    n = b0 if mode == "nt" else b1
    lim = lambda arr, pick, is_last, full: arr.shape[2] if (pick == "split" and is_last) else full
    tm = _tile(lim(a, a_pick, mode == "tn", m), (1024, 512, 256, 128))
    tn = _tile(lim(b, b_pick, mode != "nt", n), (1408, 1280, 1024, 512, 256, 128))
    k_lim = min(lim(a, a_pick, mode != "tn", k), lim(b, b_pick, mode == "nt", k))
    tk = max([d for d in range(128, min(k_lim, MM_TK_MAX) + 1, 128) if k_lim % d == 0], default=k_lim)
    nk = k // tk

    grid = (m // tm, n // tn, nk)

    riding = carry is not None or scatter is not None
    ns = len(scatter[0]) if scatter is not None else 0

    def body(a_ref, b_ref, *rest):
        if carry is not None:
            _, o_ref, w_ref, *scratch = rest
            copies = _gather_copies(w_ref, carry[1], *scratch[-2:])
        elif scatter is not None:
            o_ref, scratch = rest[ns], rest[2 * ns + 1:]
            if scatter[1] is None:
                copies = _sibling_copies(rest[:ns], rest[ns + 1:2 * ns + 1], *scratch[-2:])
            else:
                copies = _scatter_copies(rest[:ns], rest[ns + 1:2 * ns + 1], scatter[1], *scratch[-2:])
        else:
            o_ref, *scratch = rest
        if riding:
            scratch = scratch[:-2]
            step = (pl.program_id(0) * grid[1] + pl.program_id(1)) * grid[2] + pl.program_id(2)

            @pl.when(step == 0)
            def _():
                for cp in copies:
                    cp.start()
        part = lax.dot_general(a_ref[...], b_ref[...], _DN[mode], preferred_element_type=F32)
        if nk == 1:
            o_ref[...] = part.astype(o_ref.dtype)
        else:
            acc_ref, = scratch
            kk = pl.program_id(2)

            @pl.when(kk == 0)
            def _():
                acc_ref[...] = part

            @pl.when(kk > 0)
            def _():
                acc_ref[...] += part

            @pl.when(kk == nk - 1)
            def _():
                o_ref[...] = acc_ref[...].astype(o_ref.dtype)
        if riding:
            @pl.when(step == grid[0] * grid[1] * grid[2] - 1)
            def _():
                for cp in copies:
                    cp.wait()

    if mode == "tn":
        a_spec = _mm_operand(a, a_pick, (tk, tm), lambda i, j, kk: (kk, i))
    else:
        a_spec = _mm_operand(a, a_pick, (tm, tk), lambda i, j, kk: (i, kk))
    if mode == "nt":
        b_spec = _mm_operand(b, b_pick, (tn, tk), lambda i, j, kk: (j, kk))
    else:
        b_spec = _mm_operand(b, b_pick, (tk, tn), lambda i, j, kk: (kk, j))
    scratch = [pltpu.VMEM((tm, tn), F32)] if nk > 1 else []
    out_spec = pl.BlockSpec((tm, tn), lambda i, j, kk: (i, j))
    out_shape = jax.ShapeDtypeStruct((m, n), out_dtype)
    if not riding:
        return pl.pallas_call(
            body, name=name, grid=grid, in_specs=[a_spec, b_spec], out_specs=out_spec, out_shape=out_shape,
            scratch_shapes=scratch, compiler_params=_cp(("parallel", "parallel", "arbitrary")),
        )(a, b)
    any_space = pl.BlockSpec(memory_space=pl.ANY)
    in_order = _cp(("arbitrary", "arbitrary", "arbitrary"))
    if carry is not None:
        return pl.pallas_call(
            body, name=name, grid=grid, in_specs=[a_spec, b_spec, any_space], out_specs=[out_spec, any_space],
            out_shape=[out_shape, jax.ShapeDtypeStruct(carry[0].shape, carry[0].dtype)], input_output_aliases={2: 1},
            scratch_shapes=scratch + [pltpu.SemaphoreType.DMA((3,))] * 2, compiler_params=in_order,
        )(a, b, carry[0])
    return pl.pallas_call(
        body, name=name, grid=grid, in_specs=[a_spec, b_spec] + [any_space] * ns, out_specs=[out_spec] + [any_space] * ns,
        out_shape=[out_shape] + (_sibling_shapes(scatter[0]) if scatter[1] is None else _scatter_shapes(*scatter)),
        scratch_shapes=scratch + [pltpu.SemaphoreType.DMA((ns, 3))] * 2, compiler_params=in_order,
    )(a, b, *scatter[0])


def _rms(x, w):
    return x * lax.rsqrt(jnp.mean(x * x, axis=-1, keepdims=True) + NORM_EPS) * w


def _norm_fwd(x, w, name):
    t, d = x.shape
    r = _rows(t)

    def body(x_ref, w_ref, h_ref):
        h_ref[...] = _rms(x_ref[...], w_ref[...]).astype(h_ref.dtype)

    return pl.pallas_call(
        body, name=name, grid=(t // r,),
        in_specs=[pl.BlockSpec((r, d), lambda i: (i, 0)), pl.BlockSpec((1, d), lambda i: (0, 0))],
        out_specs=pl.BlockSpec((r, d), lambda i: (i, 0)),
        out_shape=jax.ShapeDtypeStruct((t, d), BF16), compiler_params=_cp(("parallel",)),
    )(x, w)


def _resnorm_fwd(x, y, w, name, passing=None):
    t, d = x.shape
    r = _rows(t)
    steps = t // r
    arrs, kinds = passing if passing is not None else ((), ())
    na = len(arrs)

    def body(x_ref, y_ref, w_ref, *rest):
        o_ref = rest[na]
        if na:
            cps, arrivals = _pass_copies(rest[na + 1:2 * na + 1], kinds, rest[2 * na + 1], rest[2 * na + 2])

            @pl.when(pl.program_id(0) == 0)
            def _():
                for cp in cps:
                    cp.start()

        o_ref[...] = x_ref[...] + _rms(y_ref[...], w_ref[...])
        if na:
            @pl.when(pl.program_id(0) == steps - 1)
            def _():
                for cp, arrival in zip(cps, arrivals):
                    cp.wait_send()
                    arrival.wait_recv()

    row = pl.BlockSpec((r, d), lambda i: (i, 0))
    any_space = pl.BlockSpec(memory_space=pl.ANY)
    res = pl.pallas_call(
        body, name=name, grid=(steps,),
        in_specs=[row, row, pl.BlockSpec((1, d), lambda i: (0, 0))] + [any_space] * na,
        out_specs=[row] + [any_space] * na,
        out_shape=[jax.ShapeDtypeStruct((t, d), F32)] + [jax.ShapeDtypeStruct(a.shape, a.dtype) for a in arrs],
        input_output_aliases={3 + k: 1 + k for k in range(na)},
        scratch_shapes=[pltpu.SemaphoreType.DMA((na, 3))] * 2 if na else [],
        compiler_params=_cp(("arbitrary",) if na else ("parallel",)),
    )(x, y, w, *arrs)
    return (res[0], res[1:]) if na else res[0]


def _norm_bwd(x, w, dh, add, out_dtype, name):
    t, d = x.shape
    r = _rows(t)
    has_add = add is not None

    def body(*refs):
        if has_add:
            x_ref, w_ref, dh_ref, add_ref, dx_ref, dw_ref = refs
        else:
            x_ref, w_ref, dh_ref, dx_ref, dw_ref = refs
        xv = x_ref[...]
        g = dh_ref[...].astype(F32)
        rs = lax.rsqrt(jnp.mean(xv * xv, axis=-1, keepdims=True) + NORM_EPS)
        xh = xv * rs
        gw = g * w_ref[...]
        dx = rs * (gw - xh * jnp.mean(gw * xh, axis=-1, keepdims=True))
        if has_add:
            dx = dx + add_ref[...]
        dx_ref[...] = dx.astype(dx_ref.dtype)

        @pl.when(pl.program_id(0) == 0)
        def _():
            dw_ref[...] = jnp.zeros_like(dw_ref)

        dw_ref[...] += jnp.sum(g * xh, axis=0, keepdims=True)

    row = pl.BlockSpec((r, d), lambda i: (i, 0))
    vec = pl.BlockSpec((1, d), lambda i: (0, 0))
    ins = [x, w, dh] + ([add] if has_add else [])
    return pl.pallas_call(
        body, name=name, grid=(t // r,),
        in_specs=[row, vec, row] + ([row] if has_add else []),
        out_specs=[row, vec],
        out_shape=[jax.ShapeDtypeStruct((t, d), out_dtype), jax.ShapeDtypeStruct((1, d), F32)],
        compiler_params=_cp(("arbitrary",)),
    )(*ins)


def _loss_head(y, target, name):
    t, d = y.shape
    r = _rows(t)

    def body(y_ref, t_ref, l_ref, g_ref):
        e = y_ref[...] - t_ref[...]
        g_ref[...] = e * (1.0 / d)

        @pl.when(pl.program_id(0) == 0)
        def _():
            l_ref[...] = jnp.zeros_like(l_ref)

        l_ref[...] += jnp.sum(e * e) * (0.5 / d)

    row = pl.BlockSpec((r, d), lambda i: (i, 0))
    return pl.pallas_call(
        body, name=name, grid=(t // r,), in_specs=[row, row],
        out_specs=[pl.BlockSpec((1, 128), lambda i: (0, 0)), row],
        out_shape=[jax.ShapeDtypeStruct((1, 128), F32), jax.ShapeDtypeStruct((t, d), F32)],
        compiler_params=_cp(("arbitrary",)),
    )(y, target)


def _down(x, s):
    return x if s == 0 else pltpu.roll(x, s, 0)


def _up(x, s):
    return x if s == 0 else pltpu.roll(x, x.shape[0] - s, 0)


def _halo(t, r, hh, tc, col):
    q = r // hh
    last = t // hh - 1
    tile = pl.BlockSpec((r, tc), lambda j, i: (i, col(j)))
    prev = pl.BlockSpec((hh, tc), lambda j, i: (jnp.maximum(i * q - 1, 0), col(j)))
    nxt = pl.BlockSpec((hh, tc), lambda j, i: (jnp.minimum((i + 1) * q, last), col(j)))
    return tile, prev, nxt


def _sig(x):
    return 1.0 / (1.0 + jnp.exp(-x))


def _dsilu(x, s):
    return s * (1.0 + x * (1.0 - s))


def _dn_pre_fwd(p, conv_w, name):
    t = p.shape[0]
    r = _rows(t)

    def body(x_ref, xp_ref, w_ref, o_ref):
        j, i = pl.program_id(0), pl.program_id(1)
        xe = jnp.concatenate([jnp.where(i == 0, 0.0, xp_ref[...]), x_ref[...]], axis=0)
        c = sum(_down(xe, DN_CONV - 1 - k) * w_ref[pl.ds(k, 1), :] for k in range(DN_CONV))[8:]
        a = c * _sig(c)
        for h in range(DN_HEADS):
            ah = a[:, h * 128:(h + 1) * 128]
            fac = lax.rsqrt(jnp.sum(ah * ah, axis=-1, keepdims=True) + NORM_EPS)
            o_ref[:, h * 128:(h + 1) * 128] = ah * jnp.where(j == 0, fac * HEAD_DIM ** -0.5, jnp.where(j == 1, fac, 1.0))

    tile, prev, _ = _halo(t, r, 8, DN_W, lambda j: j)
    return pl.pallas_call(
        body, name=name, grid=(3, t // r),
        in_specs=[tile, prev, pl.BlockSpec((DN_CONV, DN_W), lambda j, i: (0, j))],
        out_specs=tile, out_shape=jax.ShapeDtypeStruct((t, 3 * DN_W), F32),
        compiler_params=_cp(("parallel", "parallel")),
    )(p, p, conv_w)


def _dn_pre_bwd(p, conv_w, dqkv, name):
    t = p.shape[0]
    r = _rows(t)
    ni = t // r

    def body(x_ref, xp_ref, xn_ref, w_ref, d_ref, dn_ref, dx_ref, dw_ref):
        j, i = pl.program_id(0), pl.program_id(1)
        xe = jnp.concatenate([jnp.where(i == 0, 0.0, xp_ref[...]), x_ref[...], xn_ref[...]], axis=0)
        de = jnp.concatenate([jnp.zeros((8, DN_W), F32), d_ref[...], jnp.where(i == ni - 1, 0.0, dn_ref[...])], axis=0)
        xs = [_down(xe, DN_CONV - 1 - k) for k in range(DN_CONV)]
        c = sum(xs[k] * w_ref[pl.ds(k, 1), :] for k in range(DN_CONV))
        s = _sig(c)
        a = c * s
        das = []
        for h in range(DN_HEADS):
            ah, dh = a[:, h * 128:(h + 1) * 128], de[:, h * 128:(h + 1) * 128]
            fac = lax.rsqrt(jnp.sum(ah * ah, axis=-1, keepdims=True) + NORM_EPS)
            dnorm = fac * dh - ah * (fac * fac * fac) * jnp.sum(dh * ah, axis=-1, keepdims=True)
            das.append(jnp.where(j == 0, dnorm * HEAD_DIM ** -0.5, jnp.where(j == 1, dnorm, dh)))
        dc = jnp.concatenate(das, axis=1) * _dsilu(c, s)
        dx_ref[...] = sum(_up(dc, DN_CONV - 1 - k) * w_ref[pl.ds(k, 1), :] for k in range(DN_CONV))[8:8 + r]

        @pl.when(i == 0)
        def _():
            dw_ref[...] = jnp.zeros_like(dw_ref)

        for k in range(DN_CONV):
            dw_ref[pl.ds(k, 1), :] += jnp.sum((dc * xs[k])[8:8 + r], axis=0, keepdims=True)

    tile, prev, nxt = _halo(t, r, 8, DN_W, lambda j: j)
    wspec = pl.BlockSpec((DN_CONV, DN_W), lambda j, i: (0, j))
    return pl.pallas_call(
        body, name=name, grid=(3, ni),
        in_specs=[tile, prev, nxt, wspec, tile, nxt],
        out_specs=[tile, wspec],
        out_shape=[jax.ShapeDtypeStruct((t, 3 * DN_W), F32), jax.ShapeDtypeStruct((DN_CONV, 3 * DN_W), F32)],
        compiler_params=_cp(("parallel", "arbitrary")),
    )(p, p, p, conv_w, dqkv, dqkv)


def _ffn_act_fwd(up, cw, cb, name):
    t, f2 = up.shape
    f = f2 // 2
    r = _tile(t, (512, 256, 128))
    tc = _tile(f, (512, 256, 128))
    nj = f // tc

    def body(a_ref, ap_ref, b_ref, bp_ref, wa_ref, wb_ref, ca_ref, cb_ref, o_ref):
        i = pl.program_id(1)

        def conv(x_ref, xp_ref, w_ref, c_ref):
            xe = jnp.concatenate([jnp.where(i == 0, 0.0, xp_ref[...]), x_ref[...]], axis=0)
            return sum(_down(xe, FFN_CONV - 1 - k) * w_ref[pl.ds(k, 1), :] for k in range(FFN_CONV))[8:] + c_ref[...]

        ua = conv(a_ref, ap_ref, wa_ref, ca_ref)
        ub = conv(b_ref, bp_ref, wb_ref, cb_ref)
        o_ref[...] = (ua * _sig(ua) * ub).astype(o_ref.dtype)

    ta, pa, _ = _halo(t, r, 8, tc, lambda j: j)
    tb, pb, _ = _halo(t, r, 8, tc, lambda j: j + nj)
    wa = pl.BlockSpec((FFN_CONV, tc), lambda j, i: (0, j))
    wb = pl.BlockSpec((FFN_CONV, tc), lambda j, i: (0, j + nj))
    ca = pl.BlockSpec((1, tc), lambda j, i: (0, j))
    cbs = pl.BlockSpec((1, tc), lambda j, i: (0, j + nj))
    return pl.pallas_call(
        body, name=name, grid=(nj, t // r),
        in_specs=[ta, pa, tb, pb, wa, wb, ca, cbs], out_specs=ta,
        out_shape=jax.ShapeDtypeStruct((t, f), BF16), compiler_params=_cp(("parallel", "parallel")),
    )(up, up, up, up, cw, cw, cb, cb)


def _ffn_act_bwd(up, cw, cb, dact, name):
    t, f2 = up.shape
    f = f2 // 2
    r = _tile(t, (512, 256, 128))
    ni = t // r
    tc = _tile(f, (512, 256, 128))
    nj = f // tc

    def body(a_ref, ap_ref, an_ref, b_ref, bp_ref, bn_ref, wa_ref, wb_ref, ca_ref, cb_ref, d_ref, dn_ref,
             du_ref, dw_ref, dc_ref):
        i = pl.program_id(1)
        dua_ref, dub_ref, dwa_ref, dwb_ref, dca_ref, dcb_ref = (du_ref.at[0], du_ref.at[1], dw_ref.at[0], dw_ref.at[1],
                                                                  dc_ref.at[0], dc_ref.at[1])

        def ext(x_ref, xp_ref, xn_ref):
            return jnp.concatenate([jnp.where(i == 0, 0.0, xp_ref[...]), x_ref[...], xn_ref[...]], axis=0)

        ae, be = ext(a_ref, ap_ref, an_ref), ext(b_ref, bp_ref, bn_ref)
        as_ = [_down(ae, FFN_CONV - 1 - k) for k in range(FFN_CONV)]
        bs_ = [_down(be, FFN_CONV - 1 - k) for k in range(FFN_CONV)]
        ua = sum(as_[k] * wa_ref[pl.ds(k, 1), :] for k in range(FFN_CONV)) + ca_ref[...]
        ub = sum(bs_[k] * wb_ref[pl.ds(k, 1), :] for k in range(FFN_CONV)) + cb_ref[...]
        de = jnp.concatenate([jnp.zeros((8, tc), F32), d_ref[...].astype(F32),
                              jnp.where(i == ni - 1, 0.0, dn_ref[...].astype(F32))], axis=0)
        s = _sig(ua)
        dua = de * ub * _dsilu(ua, s)
        dub = de * ua * s
        dua_ref[...] = sum(_up(dua, FFN_CONV - 1 - k) * wa_ref[pl.ds(k, 1), :] for k in range(FFN_CONV))[8:8 + r].astype(dua_ref.dtype)
        dub_ref[...] = sum(_up(dub, FFN_CONV - 1 - k) * wb_ref[pl.ds(k, 1), :] for k in range(FFN_CONV))[8:8 + r].astype(dub_ref.dtype)

        @pl.when(i == 0)
        def _():
            dw_ref[...] = jnp.zeros_like(dw_ref)
            dc_ref[...] = jnp.zeros_like(dc_ref)

        for k in range(FFN_CONV):
            dwa_ref[pl.ds(k, 1), :] += jnp.sum((dua * as_[k])[8:8 + r], axis=0, keepdims=True)
            dwb_ref[pl.ds(k, 1), :] += jnp.sum((dub * bs_[k])[8:8 + r], axis=0, keepdims=True)
        dca_ref[...] += jnp.sum(dua[8:8 + r], axis=0, keepdims=True)
        dcb_ref[...] += jnp.sum(dub[8:8 + r], axis=0, keepdims=True)

    ta, pa, na = _halo(t, r, 8, tc, lambda j: j)
    tb, pb, nb = _halo(t, r, 8, tc, lambda j: j + nj)
    wa = pl.BlockSpec((FFN_CONV, tc), lambda j, i: (0, j))
    wb = pl.BlockSpec((FFN_CONV, tc), lambda j, i: (0, j + nj))
    ca = pl.BlockSpec((1, tc), lambda j, i: (0, j))
    cbs = pl.BlockSpec((1, tc), lambda j, i: (0, j + nj))
    return pl.pallas_call(
        body, name=name, grid=(nj, ni),
        in_specs=[ta, pa, na, tb, pb, nb, wa, wb, ca, cbs, ta, na],
        out_specs=[pl.BlockSpec((2, r, tc), lambda j, i: (0, i, j)), pl.BlockSpec((2, FFN_CONV, tc), lambda j, i: (0, 0, j)),
                   pl.BlockSpec((2, 1, tc), lambda j, i: (0, 0, j))],
        out_shape=[jax.ShapeDtypeStruct((2, t, f), BF16), jax.ShapeDtypeStruct((2, FFN_CONV, f), F32),
                   jax.ShapeDtypeStruct((2, 1, f), F32)],
        compiler_params=_cp(("parallel", "arbitrary")),
    )(up, up, up, up, up, up, cw, cw, cb, cb, dact, dact)


def _pool_pick(g, vals):
    return jnp.where(g == 0, vals[0], jnp.where(g == 1, vals[1], jnp.where(g == 2, vals[2], vals[3])))


def _pool_pre(xe, g, t0):
    s1 = xe + _down(xe, 1)
    s2 = s1 + _down(s1, 2)
    s3 = s2 + _down(s2, 4)
    s4 = s3 + _down(s3, 8)
    r = xe.shape[0] - 16
    pos = (t0 + lax.broadcasted_iota(jnp.int32, (r, 1), 0)).astype(F32)
    cnt = jnp.minimum(pos + 1.0, _pool_pick(g, (2.0, 4.0, 8.0, 16.0)))
    return _pool_pick(g, (s1, s2, s3, s4))[16:] / cnt - xe[16:]


def _pool_fwd(p, pool_w, scale, name):
    t = p.shape[0]
    r = _tile(t, (1024, 256, 128))

    def body(x_ref, xp_ref, w_ref, sc_ref, o_ref):
        g, i = pl.program_id(0), pl.program_id(1)
        xe = jnp.concatenate([jnp.where(i == 0, 0.0, xp_ref[...]), x_ref[...]], axis=0)
        pre = _pool_pre(xe, g, i * r)
        o_ref[...] = jnp.dot(pre, w_ref[0], preferred_element_type=F32) * sc_ref[...]

    tile, prev, _ = _halo(t, r, 16, 128, lambda j: CB_POOL + j)
    return pl.pallas_call(
        body, name=name, grid=(POOL_GROUPS, t // r),
        in_specs=[tile, prev, pl.BlockSpec((1, 128, 128), lambda j, i: (j, 0, 0)), pl.BlockSpec((1, 128), lambda j, i: (0, j))],
        out_specs=pl.BlockSpec((r, 128), lambda j, i: (i, j)),
        out_shape=jax.ShapeDtypeStruct((t, POOL_W), F32), compiler_params=_cp(("parallel", "parallel")),
    )(p, p, pool_w, scale)


def _pool_bwd(p, pool_w, scale, dycat, name):
    t = p.shape[0]
    r = _tile(t, (1024, 256, 128))
    ni = t // r

    def body(x_ref, xp_ref, w_ref, sc_ref, d_ref, dn_ref, dx_ref, dw_ref, dsc_ref):
        g, i = pl.program_id(0), pl.program_id(1)
        xe = jnp.concatenate([jnp.where(i == 0, 0.0, xp_ref[...]), x_ref[...]], axis=0)
        pre = _pool_pre(xe, g, i * r)
        w = w_ref[0]
        dy = d_ref[...]
        dye = jnp.concatenate([dy, jnp.where(i == ni - 1, 0.0, dn_ref[...])], axis=0)
        dpre = lax.dot_general(dye * sc_ref[...], w, _DN["nt"], preferred_element_type=F32)
        pos = (i * r + lax.broadcasted_iota(jnp.int32, (r + 16, 1), 0)).astype(F32)
        dm = dpre / jnp.minimum(pos + 1.0, _pool_pick(g, (2.0, 4.0, 8.0, 16.0)))
        a1 = dm + _up(dm, 1)
        a2 = a1 + _up(a1, 2)
        a3 = a2 + _up(a2, 4)
        a4 = a3 + _up(a3, 8)
        dx_ref[...] = (_pool_pick(g, (a1, a2, a3, a4)) - dpre)[:r]

        @pl.when(i == 0)
        def _():
            dw_ref[...] = jnp.zeros_like(dw_ref)
            dsc_ref[...] = jnp.zeros_like(dsc_ref)

        dw_ref[0] += lax.dot_general(pre, dy * sc_ref[...], _DN["tn"], preferred_element_type=F32)
        dsc_ref[...] += jnp.sum(dy * jnp.dot(pre, w, preferred_element_type=F32), axis=0, keepdims=True)

    tile, prev, _ = _halo(t, r, 16, 128, lambda j: CB_POOL + j)
    dtile, _, dnxt = _halo(t, r, 16, 128, lambda j: DN_W // 128 + j)
    wspec = pl.BlockSpec((1, 128, 128), lambda j, i: (j, 0, 0))
    sspec = pl.BlockSpec((1, 128), lambda j, i: (0, j))
    return pl.pallas_call(
        body, name=name, grid=(POOL_GROUPS, ni),
        in_specs=[tile, prev, wspec, sspec, dtile, dnxt],
        out_specs=[pl.BlockSpec((r, 128), lambda j, i: (i, j)), wspec, sspec],
        out_shape=[jax.ShapeDtypeStruct((t, POOL_W), F32), jax.ShapeDtypeStruct((POOL_GROUPS, 128, 128), F32),
                   jax.ShapeDtypeStruct((1, POOL_W), F32)],
        compiler_params=_cp(("parallel", "arbitrary")),
    )(p, p, pool_w, scale, dycat, dycat)


_DNB = {"nn": (((2,), (1,)), ((0,), (0,))), "nt": (((2,), (2,)), ((0,), (0,))), "tn": (((1,), (1,)), ((0,), (0,)))}


def _dot(a, b, mode="nn", precision=None):
    dn = _DNB[mode] if a.ndim == 3 else _DN[mode]
    return lax.dot_general(a, b, dn, precision=precision, preferred_element_type=F32)


@functools.partial(jax.custom_vjp, nondiff_argnums=(2,))
def _bdot(a, b, mode):
    return _dot(a.astype(BF16), b.astype(BF16), mode)


def _bdot_fwd(a, b, mode):
    return _bdot(a, b, mode), (a, b)


def _bdot_bwd(mode, res, g):
    a, b = res
    if mode == "nn":
        return _bdot(g, b, "nt"), _bdot(a, g, "tn")
    if mode == "nt":
        return _bdot(g, b, "nn"), _bdot(g, a, "tn")
    return _bdot(b, g, "nt"), _bdot(a, g, "nn")


_bdot.defvjp(_bdot_fwd, _bdot_bwd)


def _dn_consts():
    c = DN_CHUNK
    ii = lax.broadcasted_iota(jnp.int32, (c, c), 0)
    jj = lax.broadcasted_iota(jnp.int32, (c, c), 1)
    one, zero = jnp.ones((c, c), F32), jnp.zeros((c, c), F32)
    return dict(ltri=jnp.where(ii >= jj, one, zero), utri=jnp.where(ii <= jj, one, zero), ones=one,
                causal=ii >= jj, strict=ii > jj, eye=jnp.where(ii == jj, one, zero))


def _dn_chunk(q, k, v, z, bcol, acol, s_in, alog, dtb, nw, cs):
    c = DN_CHUNK
    hh = q.shape[0]
    per_head = lambda m: jnp.broadcast_to(m, (hh, c, c))
    beta = _sig(bcol)
    xa = acol + dtb
    g = -jnp.exp(alog) * (jnp.maximum(xa, 0.0) + jnp.log(1.0 + jnp.exp(-jnp.abs(xa))))
    gb = jnp.broadcast_to(g, (hh, c, HEAD_DIM))
    gbc = jnp.broadcast_to(g, (hh, c, c))
    gc = _dot(per_head(cs["ltri"]), gb, precision=HIGH)
    gcol = gc[:, :, :c]
    grow = jnp.swapaxes(gcol, 1, 2)
    decay = jnp.exp(jnp.where(cs["causal"], gcol - grow, -1e30))
    kb = k * beta
    vb = v * beta
    nil = -jnp.where(cs["strict"], _bdot(kb, k, "nt") * decay, 0.0)
    inv = cs["eye"] + nil
    powk = nil
    for _ in range(int(math.log2(c)) - 1):
        powk = _bdot(powk, powk, "nn")
        inv = _bdot(inv, cs["eye"] + powk, "nn")
    eg = jnp.exp(gc)
    u = _bdot(inv, vb, "nn")
    w = _bdot(inv, kb * eg, "nn")
    a = _bdot(q, k, "nt") * decay
    v_new = u - _bdot(w, s_in, "nn")
    o = _bdot(q * eg, s_in, "nn") + _bdot(a, v_new, "nn")
    glast = jnp.sum(gb, axis=1, keepdims=True)
    s_out = s_in * jnp.exp(glast) + _bdot(k * jnp.exp(glast - gc), v_new, "tn")
    on = o * lax.rsqrt(jnp.mean(o * o, axis=-1, keepdims=True) + NORM_EPS) * nw
    return on * (z * _sig(z)), s_out


def _lane_pick(x, lane, idx):
    return jnp.sum(jnp.where(lane == idx, x, 0.0), axis=1, keepdims=True)


def _dn_load(q_ref, k_ref, v_ref, z_ref, bd_ref, al_ref, dt_ref, nw_ref, s_in):
    lane = lax.broadcasted_iota(jnp.int32, (1, 128), 1)
    bd, al, dt = bd_ref[...], al_ref[...], dt_ref[...]
    heads = range(DN_HEADS)
    wide = lambda ref: jnp.stack([ref[:, h * 128:(h + 1) * 128] for h in heads], axis=0)
    col = lambda x, off: jnp.stack([_lane_pick(x, lane, off + h) for h in heads], axis=0)
    return (wide(q_ref), wide(k_ref), wide(v_ref), wide(z_ref), col(bd, 0), col(bd, DN_HEADS), s_in,
            col(al, 0), col(dt, 0), nw_ref[...])


def _dn_fwd(qkv, p, alog, dtb, nw, name):
    t = qkv.shape[0]
    c = DN_CHUNK
    n = t // c

    def body(q_ref, k_ref, v_ref, z_ref, bd_ref, al_ref, dt_ref, nw_ref, y_ref, ss_ref, s_scr):
        @pl.when(pl.program_id(0) == 0)
        def _():
            s_scr[...] = jnp.zeros_like(s_scr)

        s_in = s_scr[...]
        y, s_out = _dn_chunk(*_dn_load(q_ref, k_ref, v_ref, z_ref, bd_ref, al_ref, dt_ref, nw_ref, s_in), _dn_consts())
        ss_ref[0] = s_in
        s_scr[...] = s_out
        for h in range(DN_HEADS):
            y_ref[:, h * 128:(h + 1) * 128] = y[h]

    wide = lambda j: pl.BlockSpec((c, DN_W), lambda i: (i, j))
    vec = pl.BlockSpec((1, 128), lambda i: (0, 0))
    return pl.pallas_call(
        body, name=name, grid=(n,),
        in_specs=[wide(0), wide(1), wide(2), wide(3), pl.BlockSpec((c, 128), lambda i: (i, CB_BD)), vec, vec, vec],
        out_specs=[wide(0), pl.BlockSpec((1, DN_HEADS, 128, 128), lambda i: (i, 0, 0, 0))],
        out_shape=[jax.ShapeDtypeStruct((t, DN_W), F32), jax.ShapeDtypeStruct((n, DN_HEADS, 128, 128), F32)],
        scratch_shapes=[pltpu.VMEM((DN_HEADS, 128, 128), F32)],
        compiler_params=_cp(("arbitrary",)),
    )(qkv, qkv, qkv, p, p, alog, dtb, nw)


def _dn_bwd(qkv, p, alog, dtb, nw, states, dycat, name, carry=None):
    t = qkv.shape[0]
    c = DN_CHUNK
    n = t // c
    sums, kinds = carry if carry is not None else ((), ())
    na = len(sums)

    def body(*refs):
        q_ref, k_ref, v_ref, z_ref, bd_ref, al_ref, dt_ref, nw_ref, ss_ref, dy_ref = refs[:10]
        dqkv_ref, dz_ref, dbd_ref, dal_ref, ddt_ref, dnw_ref = refs[10 + na:16 + na]
        ds_scr = refs[16 + 2 * na]
        if na:
            copies = _scatter_copies(refs[10:10 + na], refs[16 + na:16 + 2 * na], kinds, *refs[17 + 2 * na:])

        @pl.when(pl.program_id(0) == 0)
        def _():
            ds_scr[...] = jnp.zeros_like(ds_scr)
            dal_ref[...] = jnp.zeros_like(dal_ref)
            ddt_ref[...] = jnp.zeros_like(ddt_ref)
            dnw_ref[...] = jnp.zeros_like(dnw_ref)
            if na:
                for cp in copies:
                    cp.start()

        lane = lax.broadcasted_iota(jnp.int32, (1, 128), 1)
        args = _dn_load(q_ref, k_ref, v_ref, z_ref, bd_ref, al_ref, dt_ref, nw_ref, ss_ref[0])
        dy = jnp.stack([dy_ref[:, h * 128:(h + 1) * 128] for h in range(DN_HEADS)], axis=0)
        _, vjp = jax.vjp(functools.partial(_dn_chunk, cs=_dn_consts()), *args)
        gq, gk, gv, gz, gb, ga, gs, gal, gdt, gnw = vjp((dy, ds_scr[...]))
        ds_scr[...] = gs
        dbd = jnp.zeros((c, 128), F32)
        dal = jnp.zeros((1, 128), F32)
        ddt = jnp.zeros((1, 128), F32)
        for h in range(DN_HEADS):
            sl = slice(h * 128, (h + 1) * 128)
            dqkv_ref[:, sl] = gq[h]
            dqkv_ref[:, DN_W + h * 128:DN_W + (h + 1) * 128] = gk[h]
            dqkv_ref[:, 2 * DN_W + h * 128:2 * DN_W + (h + 1) * 128] = gv[h]
            dz_ref[:, sl] = gz[h]
            dbd = dbd + jnp.where(lane == h, gb[h], 0.0) + jnp.where(lane == DN_HEADS + h, ga[h], 0.0)
            dal = dal + jnp.where(lane == h, gal[h], 0.0)
            ddt = ddt + jnp.where(lane == h, gdt[h], 0.0)
        dbd_ref[...] = dbd
        dal_ref[...] += dal
        ddt_ref[...] += ddt
        dnw_ref[...] += gnw

        if na:
            @pl.when(pl.program_id(0) == n - 1)
            def _():
                for cp in copies:
                    cp.wait()

    rev = lambda i: n - 1 - i
    wide = lambda j: pl.BlockSpec((c, DN_W), lambda i: (rev(i), j))
    vec = pl.BlockSpec((1, 128), lambda i: (0, 0))
    any_space = pl.BlockSpec(memory_space=pl.ANY)
    return pl.pallas_call(
        body, name=name, grid=(n,),
        in_specs=[wide(0), wide(1), wide(2), wide(3), pl.BlockSpec((c, 128), lambda i: (rev(i), CB_BD)), vec, vec, vec,
                  pl.BlockSpec((1, DN_HEADS, 128, 128), lambda i: (rev(i), 0, 0, 0)), wide(0)] + [any_space] * na,
        out_specs=[pl.BlockSpec((c, 3 * DN_W), lambda i: (rev(i), 0)), wide(0),
                   pl.BlockSpec((c, 128), lambda i: (rev(i), 0)), vec, vec, vec] + [any_space] * na,
        out_shape=[jax.ShapeDtypeStruct((t, 3 * DN_W), F32), jax.ShapeDtypeStruct((t, DN_W), F32),
                   jax.ShapeDtypeStruct((t, 128), F32), jax.ShapeDtypeStruct((1, 128), F32),
                   jax.ShapeDtypeStruct((1, 128), F32), jax.ShapeDtypeStruct((1, 128), F32)] + _scatter_shapes(sums, kinds),
        scratch_shapes=[pltpu.VMEM((DN_HEADS, 128, 128), F32)] + ([pltpu.SemaphoreType.DMA((na, 3))] * 2 if na else []),
        compiler_params=_cp(("arbitrary",)),
    )(qkv, qkv, qkv, p, p, alog, dtb, nw, states, dycat, *sums)


def _rope(x, cosf, sins):
    return x * cosf + pltpu.roll(x, HEAD_DIM // 2, 1) * sins


def _rope_t(d, cosf, sins):
    return d * cosf + pltpu.roll(d * sins, HEAD_DIM // 2, 1)


def _swa_masks():
    b = SWA_BLOCK
    i = lax.broadcasted_iota(jnp.int32, (SWA_GROUP * b, b), 0) & (b - 1)
    j = lax.broadcasted_iota(jnp.int32, (SWA_GROUP * b, b), 1)
    return j > i, j <= i


def _swa_sink_col(sinks_ref, h):
    b = SWA_BLOCK
    r = lax.broadcasted_iota(jnp.int32, (SWA_GROUP * b, 1), 0)
    s = [sinks_ref[h * SWA_GROUP + g] for g in range(SWA_GROUP)]
    return jnp.where(r < b, s[0], jnp.where(r < 2 * b, s[1], s[2]))


def _swa_specs(t, h_first):
    nb = t // SWA_BLOCK

    def at(col, off):
        def imap(h, n):
            return (jnp.clip(n + off, 0, nb - 1), col(h))
        return pl.BlockSpec((SWA_BLOCK, 128), imap)
    return at


def _swa_fwd(p, cosf, sins, sinks, name):
    t = p.shape[0]
    b = SWA_BLOCK
    nb = t // b
    at = _swa_specs(t, None)
    scale = HEAD_DIM ** -0.5

    def body(q0, q1, q2, kp, kc, vp, vc, cc, sc, cp, sp, sinks_ref, o_ref, lse_ref):
        h, n = pl.program_id(0), pl.program_id(1)
        qs = jnp.concatenate([_rope(q[...], cc[...], sc[...]) for q in (q0, q1, q2)], axis=0)
        ks = jnp.concatenate([_rope(kp[...], cp[...], sp[...]), _rope(kc[...], cc[...], sc[...])], axis=0)
        vs = jnp.concatenate([vp[...], vc[...]], axis=0)
        mp, mc = _swa_masks()
        mask = jnp.concatenate([mp & (n > 0), mc], axis=1)
        s = jnp.where(mask, _dot(qs, ks, "nt") * scale, -1e30)
        sink = _swa_sink_col(sinks_ref, h)
        m = jnp.maximum(jnp.max(s, axis=1, keepdims=True), sink)
        e = jnp.exp(s - m)
        l = jnp.sum(e, axis=1, keepdims=True) + jnp.exp(sink - m)
        o = _dot(e, vs) / l
        lse = m + jnp.log(l)
        lane = lax.broadcasted_iota(jnp.int32, (1, 128), 1)
        tile = jnp.zeros((b, 128), F32)
        for g in range(SWA_GROUP):
            o_ref[:, g * 128:(g + 1) * 128] = o[g * b:(g + 1) * b]
            tile = tile + jnp.where(lane == g, lse[g * b:(g + 1) * b], 0.0)
        lse_ref[0] = tile

    qcol = lambda g: (lambda h: CB_SQ + h * SWA_GROUP + g)
    kcol, vcol, one = (lambda h: CB_SK + h), (lambda h: CB_SV + h), (lambda h: 0)
    in_specs = [at(qcol(0), 0), at(qcol(1), 0), at(qcol(2), 0), at(kcol, -1), at(kcol, 0), at(vcol, -1), at(vcol, 0),
                at(one, 0), at(one, 0), at(one, -1), at(one, -1), pl.BlockSpec(memory_space=pltpu.SMEM)]
    return pl.pallas_call(
        body, name=name, grid=(SWA_KV_HEADS, nb), in_specs=in_specs,
        out_specs=[pl.BlockSpec((b, SWA_GROUP * 128), lambda h, n: (n, h)), pl.BlockSpec((1, b, 128), lambda h, n: (h, n, 0))],
        out_shape=[jax.ShapeDtypeStruct((t, SWA_W), F32), jax.ShapeDtypeStruct((SWA_KV_HEADS, t, 128), F32)],
        compiler_params=_cp(("parallel", "parallel")),
    )(p, p, p, p, p, p, p, cosf, sins, cosf, sins, sinks)


def _swa_bwd(p, cosf, sins, sinks, o, lse, dycat, name):
    t = p.shape[0]
    b = SWA_BLOCK
    nb = t // b
    at = _swa_specs(t, None)
    scale = HEAD_DIM ** -0.5
    gb = SWA_GROUP * b

    def body(q0, q1, q2, r0, r1, r2, kp, kc, vp, vc, cc, sc, cp, sp, cn, sn, d0, d1, d2, e0, e1, e2,
             oc_ref, on_ref, lc_ref, ln_ref, sinks_ref, dq_ref, dk_ref, dv_ref, dsk_ref):
        h, n = pl.program_id(0), pl.program_id(1)
        lane = lax.broadcasted_iota(jnp.int32, (1, 128), 1)
        stack = lambda refs: jnp.concatenate([x[...] for x in refs], axis=0)
        q_c = jnp.concatenate([_rope(q[...], cc[...], sc[...]) for q in (q0, q1, q2)], axis=0)
        q_n = jnp.concatenate([_rope(q[...], cn[...], sn[...]) for q in (r0, r1, r2)], axis=0)
        k_p = _rope(kp[...], cp[...], sp[...])
        k_c = _rope(kc[...], cc[...], sc[...])
        do_c, do_n = stack((d0, d1, d2)), stack((e0, e1, e2))
        o_c = jnp.concatenate([oc_ref[:, g * 128:(g + 1) * 128] for g in range(SWA_GROUP)], axis=0)
        o_n = jnp.concatenate([on_ref[:, g * 128:(g + 1) * 128] for g in range(SWA_GROUP)], axis=0)
        lse_c = jnp.concatenate([_lane_pick(lc_ref[0], lane, g) for g in range(SWA_GROUP)], axis=0)
        lse_n = jnp.concatenate([_lane_pick(ln_ref[0], lane, g) for g in range(SWA_GROUP)], axis=0)
        dl_c = jnp.sum(do_c * o_c, axis=1, keepdims=True)
        dl_n = jnp.sum(do_n * o_n, axis=1, keepdims=True)
        mp, mc = _swa_masks()

        def pair(qr, kr, v, do, lse_, dl, mask):
            s = _dot(qr, kr, "nt") * scale
            pr = jnp.where(mask, jnp.exp(s - lse_), 0.0)
            ds = pr * (_dot(do, v, "nt") - dl) * scale
            return _dot(ds, kr), _dot(ds, qr, "tn"), _dot(pr, do, "tn")

        dq_a, _, _ = pair(q_c, k_p, vp[...], do_c, lse_c, dl_c, mp & (n > 0))
        dq_b, dk_b, dv_b = pair(q_c, k_c, vc[...], do_c, lse_c, dl_c, mc)
        _, dk_n, dv_n = pair(q_n, k_c, vc[...], do_n, lse_n, dl_n, mp & (n < nb - 1))
        dq = dq_a + dq_b
        for g in range(SWA_GROUP):
            dq_ref[:, g * 128:(g + 1) * 128] = _rope_t(dq[g * b:(g + 1) * b], cc[...], sc[...])
        dk_ref[...] = _rope_t(dk_b + dk_n, cc[...], sc[...])
        dv_ref[...] = dv_b + dv_n

        @pl.when(n == 0)
        def _():
            dsk_ref[...] = jnp.zeros_like(dsk_ref)

        w = -jnp.exp(_swa_sink_col(sinks_ref, h) - lse_c) * dl_c
        acc = jnp.zeros((1, 128), F32)
        for g in range(SWA_GROUP):
            acc = acc + jnp.where(lane == g, jnp.sum(w[g * b:(g + 1) * b], axis=0, keepdims=True), 0.0)
        dsk_ref[0] += jnp.broadcast_to(acc, (8, 128))

    qcol = lambda g: (lambda h: CB_SQ + h * SWA_GROUP + g)
    dcol = lambda g: (lambda h: (DN_W + POOL_W) // 128 + h * SWA_GROUP + g)
    kcol, vcol, one = (lambda h: CB_SK + h), (lambda h: CB_SV + h), (lambda h: 0)
    wide = lambda off: pl.BlockSpec((b, SWA_GROUP * 128), lambda h, n: (jnp.clip(n + off, 0, nb - 1), h))
    lspec = lambda off: pl.BlockSpec((1, b, 128), lambda h, n: (h, jnp.clip(n + off, 0, nb - 1), 0))
    in_specs = ([at(qcol(g), 0) for g in range(3)] + [at(qcol(g), 1) for g in range(3)]
                + [at(kcol, -1), at(kcol, 0), at(vcol, -1), at(vcol, 0)]
                + [at(one, 0), at(one, 0), at(one, -1), at(one, -1), at(one, 1), at(one, 1)]
                + [at(dcol(g), 0) for g in range(3)] + [at(dcol(g), 1) for g in range(3)]
                + [wide(0), wide(1), lspec(0), lspec(1), pl.BlockSpec(memory_space=pltpu.SMEM)])
    kv_out = pl.BlockSpec((b, 128), lambda h, n: (n, h))
    return pl.pallas_call(
        body, name=name, grid=(SWA_KV_HEADS, nb), in_specs=in_specs,
        out_specs=[wide(0), kv_out, kv_out, pl.BlockSpec((1, 8, 128), lambda h, n: (h, 0, 0))],
        out_shape=[jax.ShapeDtypeStruct((t, SWA_W), F32), jax.ShapeDtypeStruct((t, SWA_KV_W), F32),
                   jax.ShapeDtypeStruct((t, SWA_KV_W), F32), jax.ShapeDtypeStruct((SWA_KV_HEADS, 8, 128), F32)],
        compiler_params=_cp(("parallel", "arbitrary")),
    )(*([p] * 10), cosf, sins, cosf, sins, cosf, sins, *([dycat] * 6), o, o, lse, lse, sinks)


def _adam_math(w, g, m, v):
    m = ADAM_B1 * m + (1.0 - ADAM_B1) * g
    v = ADAM_B2 * v + (1.0 - ADAM_B2) * (g * g)
    m_hat = m / (1.0 - ADAM_B1 ** ADAM_STEP)
    v_hat = v / (1.0 - ADAM_B2 ** ADAM_STEP)
    return -ADAM_LR * (m_hat / (jnp.sqrt(v_hat) + ADAM_EPS) + ADAM_WD * w), m, v


def _adamw(w, g, m, v, name):
    shape = w.shape
    cols = shape[-1]
    rows = math.prod(shape[:-1])
    flat = lambda a: a.reshape(rows, cols)
    r = rows
    for cand in (512, 256, 128, 64, 32, 16, 8):
        if rows % cand == 0 and cand * cols * 4 <= (1 << 20):
            r = cand
            break

    def body(w_ref, g_ref, m_ref, v_ref, d_ref, nm_ref, nv_ref):
        d_ref[...], nm_ref[...], nv_ref[...] = _adam_math(w_ref[...], g_ref[...], m_ref[...], v_ref[...])

    spec = pl.BlockSpec((r, cols), lambda i: (i, 0))
    outs = pl.pallas_call(
        body, name=name, grid=(rows // r,), in_specs=[spec] * 4, out_specs=[spec] * 3,
        out_shape=[jax.ShapeDtypeStruct((rows, cols), F32)] * 3, compiler_params=_cp(("parallel",)),
    )(flat(w), flat(g), flat(m), flat(v))
    return tuple(o.reshape(shape) for o in outs)


BIG = ("w_in", "w_out", "ffn_w_up", "ffn_w_down")
CONV = ("dn_conv_w", "ffn_conv_w")
KIND = {"w_in": "col", "w_out": "row", "ffn_w_up": "col", "ffn_w_down": "row"}
SMALL = ("norm_mix_pre", "dn_a_log", "dn_dt_bias", "dn_norm_w", "pool_w", "pool_scale", "swa_sinks",
         "norm_mix_post", "norm_ffn_pre", "ffn_conv_b", "norm_ffn_post")
WEIGHTS = ("norm_mix_pre", "w_in", "dn_conv_w", "dn_a_log", "dn_dt_bias", "dn_norm_w", "pool_w", "pool_scale",
           "swa_sinks", "w_out", "norm_mix_post", "norm_ffn_pre", "ffn_w_up", "ffn_conv_w", "ffn_conv_b",
           "ffn_w_down", "norm_ffn_post")


def _pad_in(w):
    z = lambda n: jnp.zeros(w.shape[:-1] + (n,), w.dtype)
    return jnp.concatenate([w[..., :GATE_END], z(CB_POOL * 128 - GATE_END), w[..., GATE_END:],
                            z(IN_PAD - CB_POOL * 128 - (IN_TRUE - GATE_END))], axis=-1)


def _unpad_in(g):
    return jnp.concatenate([g[..., :GATE_END], g[..., CB_POOL * 128:CB_POOL * 128 + IN_TRUE - GATE_END]], axis=-1)


IN_SHARD = IN_TRUE // 4
IN_SHARD_PAD = -(-IN_SHARD // 128) * 128


def _in_runs():
    cuts = sorted({0, IN_TRUE, GATE_END} | {j * IN_SHARD for j in range(4)})
    runs = []
    for t0, t1 in zip(cuts[:-1], cuts[1:]):
        chip = t0 // IN_SHARD
        runs.append((chip * IN_SHARD_PAD + t0 - chip * IN_SHARD, t0 if t0 < GATE_END else t0 + CB_POOL * 128 - GATE_END, t1 - t0))
    return runs


def _remap_cols(x, runs, out_cols, name):
    rows, cols = x.shape
    r = _rows(rows)
    pieces = {}
    for src, dst, n in runs:
        while n > 0:
            step = min(n, 128 - src % 128, 128 - dst % 128)
            pieces.setdefault(dst // 128, []).append((src // 128, src % 128, dst % 128, step))
            src, dst, n = src + step, dst + step, n - step

    def body(x_ref, o_ref):
        lane = lax.broadcasted_iota(jnp.int32, (1, 128), 1)
        for ob in range(out_cols // 128):
            acc = jnp.zeros((r, 128), F32)
            for ib, ls, ld, n in pieces.get(ob, ()):
                blk = x_ref[:, ib * 128:(ib + 1) * 128].astype(F32)
                moved = blk if ls == ld else pltpu.roll(blk, (ld - ls) % 128, 1)
                acc = jnp.where((lane >= ld) & (lane < ld + n), moved, acc)
            o_ref[:, ob * 128:(ob + 1) * 128] = acc.astype(o_ref.dtype)

    return pl.pallas_call(
        body, name=name, grid=(rows // r,), in_specs=[pl.BlockSpec((r, cols), lambda i: (i, 0))],
        out_specs=pl.BlockSpec((r, out_cols), lambda i: (i, 0)),
        out_shape=jax.ShapeDtypeStruct((rows, out_cols), x.dtype), compiler_params=_cp(("parallel",)),
    )(x)


def _lanes(v):
    return jnp.zeros((1, 128), F32).at[0, :v.shape[0]].set(v)


def _rope_tables(positions):
    inv_freq = 1.0 / (ROPE_THETA ** (jnp.arange(0, HEAD_DIM, 2, dtype=F32) / HEAD_DIM))
    ang = positions.astype(F32)[:, None] * inv_freq
    cos, sin = jnp.cos(ang), jnp.sin(ang)
    return jnp.concatenate([cos, cos], axis=-1), jnp.concatenate([-sin, sin], axis=-1)


class _GradReduce:
    def __init__(self, place, shard_shapes):
        self.place = place
        self.out = {k: lax.empty(shard_shapes[k], F32) for k in BIG}
        self.pending = []

    def parts(self, l, dw):
        def view(k, g):
            if KIND[k] == "row":
                return g.reshape(4, -1, g.shape[1])
            if k == "w_in":
                g = _remap_cols(g, [(seg, chip, n) for chip, seg, n in _in_runs()], 4 * IN_SHARD_PAD, f"l{l}_w_in_grad_layout")
            return g[None]

        names = [k for k in BIG if k in dw]
        return names, [view(k, dw[k]) for k in names]

    def summed(self, l, names, mine, got):
        self.pending += [(l, k, _chip_sum(a, b, self.place, f"l{l}_chip_sum_{k}")) for k, a, b in zip(names, mine, got)]

    def submit(self, l, dw):
        names, mine = self.parts(l, dw)
        self.summed(l, names, mine, _swap_sibling(mine, f"l{l}_" + "_".join(names) + "_to_sibling"))

    def take(self, names):
        entries = [e for e in self.pending if e[1] in names]
        if not entries:
            return None, None
        self.pending = [e for e in self.pending if e[1] not in names]
        return entries, ([s for _, _, s in entries], [KIND[k] for _, k, _ in entries])

    def arrived(self, entries, got):
        for (l, k, own), g in zip(entries, got):
            self.out[k] = _owner_sum(own, g, KIND[k], self.place, (self.out[k], l), f"l{l}_owner_sum_{k}")

    def finish(self):
        entries, (sums, kinds) = self.take(BIG)
        self.arrived(entries, _scatter_chips(sums, kinds, "last_grads_to_owner"))
        return dict(zip(BIG, _join_halves([self.out[k] for k in BIG], "grads_join")))


class _LayerWeights:
    def __init__(self, layers):
        self.layers = layers

    def layer(self, l):
        return self.layers[l]

    def carry(self, l, k):
        return None


class _WeightGather(_LayerWeights):
    def __init__(self, shards, place):
        depth = shards["w_out"].shape[0]
        self.kinds = [KIND[k] for k in BIG]
        self.raw = [{k: _spread_shard(shards[k], l, KIND[k], place, BF16, f"l{l}_cast_{k}") for k in BIG} for l in range(depth)]
        first = _gather_ici([self.raw[0][k] for k in BIG], self.kinds, "l0_gather")
        self.layers = {}
        self.passed(0, _gather_pass(first, self.kinds, "l0_gather_pass"))

    def passed(self, l, arrs):
        full = dict(zip(BIG, arrs))
        full["w_in"] = _remap_cols(full["w_in"], _in_runs(), IN_PAD, f"l{l}_w_in_layout")
        self.layers[l] = full

    def carry(self, l, k):
        return (self.raw[l + 1][k], KIND[k]) if l + 1 < len(self.raw) else None

    def passing(self, landed):
        return [landed[k] for k in BIG], self.kinds


def _local_step(x, positions, target, w, mats, reduce=None):
    depth = w["norm_mix_pre"].shape[0]
    t = x.shape[0]
    cosf, sins = _rope_tables(positions)
    saved = []
    for l in range(depth):
        nm = f"l{l}_"
        n1, n2, n3, n4 = (w[k][l][None] for k in ("norm_mix_pre", "norm_mix_post", "norm_ffn_pre", "norm_ffn_post"))
        alog, dtb, dnw = _lanes(w["dn_a_log"][l]), _lanes(w["dn_dt_bias"][l]), w["dn_norm_w"][l][None]
        psc, cb = w["pool_scale"][l][None], w["ffn_conv_b"][l][None]
        big = mats.layer(l)
        landed = {}

        def project(a, k, name):
            riding = mats.carry(l, k)
            if riding is None:
                return _mm(a, big[k], "nn", F32, name)
            out, landed[k] = _mm(a, big[k], "nn", F32, name, carry=riding)
            return out

        h = _norm_fwd(x, n1, nm + "norm1")
        p = project(h, "w_in", nm + "in_proj")
        qkv = _dn_pre_fwd(p, w["dn_conv_w"][l], nm + "dn_pre")
        y_dn, st = _dn_fwd(qkv, p, alog, dtb, dnw, nm + "dn")
        y_pool = _pool_fwd(p, w["pool_w"][l], psc, nm + "pool")
        y_swa, lse = _swa_fwd(p, cosf, sins, w["swa_sinks"][l], nm + "swa")
        ycat = jnp.concatenate([y_dn, y_pool, y_swa], axis=1).astype(BF16)
        mix = project(ycat, "w_out", nm + "out_proj")
        x1 = _resnorm_fwd(x, mix, n2, nm + "res1")
        h2 = _norm_fwd(x1, n3, nm + "norm3")
        up = project(h2, "ffn_w_up", nm + "ffn_up")
        act = _ffn_act_fwd(up, w["ffn_conv_w"][l], cb, nm + "ffn_act")
        f = project(act, "ffn_w_down", nm + "ffn_down")
        if landed:
            x2, whole = _resnorm_fwd(x1, f, n4, nm + "res2", passing=mats.passing(landed))
            mats.passed(l + 1, whole)
        else:
            x2 = _resnorm_fwd(x1, f, n4, nm + "res2")
        saved.append(dict(x=x, h=h, p=p, qkv=qkv, st=st, y_swa=y_swa, lse=lse, ycat=ycat, mix=mix, x1=x1, h2=h2,
                          up=up, act=act, f=f, n=(n1, n2, n3, n4), alog=alog, dtb=dtb, dnw=dnw, psc=psc, cb=cb))
        x = x2
    loss, dx = _loss_head(x, target, "loss_head")
    grads = {k: [None] * depth for k in WEIGHTS}
    held = None
    for l in reversed(range(depth)):
        nm, s = f"l{l}_b_", saved[l]
        n1, n2, n3, n4 = s["n"]
        big = mats.layer(l)
        riders = (lambda names: reduce.take(names)) if reduce is not None else (lambda names: (None, None))

        def mm_swapping(layer, dw, *args, **kwargs):
            if reduce is None or not dw:
                return _mm(*args, **kwargs)
            names, mine = reduce.parts(layer, dw)
            res = _mm(*args, **kwargs, scatter=(mine, None))
            reduce.summed(layer, names, mine, res[1:])
            return res[0]

        df, g4 = _norm_bwd(s["f"], n4, dx, None, BF16, nm + "res2")
        dact = mm_swapping(l + 1, held, df, big["ffn_w_down"], "nt", F32, nm + "ffn_down_dx")
        grads["ffn_w_down"][l] = _mm(s["act"], df, "tn", BF16, nm + "ffn_down_dw")
        dup, dcw, dcb = _ffn_act_bwd(s["up"], w["ffn_conv_w"][l], s["cb"], dact, nm + "ffn_act")
        grads["ffn_conv_w"][l] = jnp.concatenate([dcw[0], dcw[1]], axis=1)
        grads["ffn_conv_b"][l] = jnp.concatenate([dcb[0], dcb[1]], axis=1)[0]
        grads["ffn_w_up"][l] = _mm(s["h2"], dup, "tn", BF16, nm + "ffn_up_dw", b_pick="split")
        dh2 = mm_swapping(l, {k: grads[k][l] for k in ("ffn_w_down", "ffn_w_up")},
                          dup, big["ffn_w_up"], "nt", BF16, nm + "ffn_up_dx", a_pick="split")
        dx1, g3 = _norm_bwd(s["x1"], n3, dh2, dx, F32, nm + "norm3")
        dmix, g2 = _norm_bwd(s["mix"], n2, dx1, None, BF16, nm + "res1")
        grads["w_out"][l] = _mm(s["ycat"], dmix, "tn", BF16, nm + "out_proj_dw")
        dycat = mm_swapping(l, {"w_out": grads["w_out"][l]}, dmix, big["w_out"], "nt", F32, nm + "out_proj_dx")
        entries, riding = riders(("ffn_w_down", "ffn_w_up"))
        res = _dn_bwd(s["qkv"], s["p"], s["alog"], s["dtb"], s["dnw"], s["st"], dycat, nm + "dn", carry=riding)
        dqkv, dz, dbd, gal, gdt, gnw = res[:6]
        if entries:
            reduce.arrived(entries, res[6:])
        dpq, gconv = _dn_pre_bwd(s["p"], w["dn_conv_w"][l], dqkv, nm + "dn_pre")
        dpool, gpw, gpsc = _pool_bwd(s["p"], w["pool_w"][l], s["psc"], dycat, nm + "pool")
        dsq, dsk, dsv, gsk = _swa_bwd(s["p"], cosf, sins, w["swa_sinks"][l], s["y_swa"], s["lse"], dycat, nm + "swa")
        dp = jnp.concatenate([dpq, dz, dbd, dpool, dsq, dsk, dsv, jnp.zeros((t, 128), F32)], axis=1).astype(BF16)
        entries, riding = riders(("w_in",))
        res = _mm(s["h"], dp, "tn", BF16, nm + "in_proj_dw", scatter=riding)
        grads["w_in"][l] = res[0] if entries else res
        if entries:
            reduce.arrived(entries, res[1:])
        entries, riding = riders(("w_out",))
        res = _mm(dp, big["w_in"], "nt", BF16, nm + "in_proj_dx", scatter=riding)
        dh = res[0] if entries else res
        if entries:
            reduce.arrived(entries, res[1:])
        dx, g1 = _norm_bwd(s["x"], n1, dh, dx1, F32, nm + "norm1")
        held = {"w_in": grads["w_in"][l]}
        grads["norm_mix_pre"][l], grads["norm_mix_post"][l] = g1[0], g2[0]
        grads["norm_ffn_pre"][l], grads["norm_ffn_post"][l] = g3[0], g4[0]
        grads["dn_conv_w"][l] = gconv
        grads["dn_a_log"][l], grads["dn_dt_bias"][l], grads["dn_norm_w"][l] = gal[0, :DN_HEADS], gdt[0, :DN_HEADS], gnw[0]
        grads["pool_w"][l], grads["pool_scale"][l] = gpw, gpsc[0]
        grads["swa_sinks"][l] = gsk[:, 0, :SWA_GROUP].reshape(SWA_HEADS)
    if reduce is not None:
        reduce.submit(0, held)
    return loss, dx, grads


def _flat2(a):
    return a.reshape(math.prod(a.shape[:-1]), a.shape[-1])


def _ew_rows(rows, cols, n_arrays):
    for cand in (512, 256, 128, 64, 32, 16):
        if rows % cand == 0 and cand * cols * 4 * n_arrays <= (8 << 20):
            return cand
    return rows


def _spread_shard(a, layer, kind, place, dtype, name):
    _, rows, cols = a.shape
    r = _ew_rows(rows, cols, 2)
    nb = rows // r

    def body(s_ref, a_ref, o_ref):
        o_ref[...] = a_ref[...].astype(o_ref.dtype)

    if kind == "row":
        out_spec = pl.BlockSpec((r, cols), lambda i, s: (s[0] * nb + i, 0))
        out_shape = (4 * rows, cols)
    else:
        out_spec = pl.BlockSpec((r, cols), lambda i, s: (i, s[0]))
        out_shape = (rows, 4 * cols)
    return pl.pallas_call(
        body, name=name,
        grid_spec=pltpu.PrefetchScalarGridSpec(
            num_scalar_prefetch=1, grid=(nb,),
            in_specs=[pl.BlockSpec((None, r, cols), lambda i, s: (layer, i, 0))], out_specs=out_spec),
        out_shape=jax.ShapeDtypeStruct(out_shape, dtype), compiler_params=_cp(("parallel",)),
    )(place, a)


def _chip_sum(mine, sib, place, name):
    parts, rows, cols = sib.shape
    r = _ew_rows(rows, cols, 3)
    nb = rows // r

    def body(s_ref, a_ref, b_ref, o_ref):
        o_ref[...] = (a_ref[...].astype(F32) + b_ref[...].astype(F32)).astype(o_ref.dtype)

    spec = pl.BlockSpec((None, r, cols), lambda j, i, s: (j, i, 0))
    return pl.pallas_call(
        body, name=name,
        grid_spec=pltpu.PrefetchScalarGridSpec(
            num_scalar_prefetch=1, grid=(parts, nb),
            in_specs=[pl.BlockSpec((None, r, cols), lambda j, i, s: (j, s[1] * nb + i, 0)), spec], out_specs=spec),
        out_shape=jax.ShapeDtypeStruct(sib.shape, BF16), compiler_params=_cp(("parallel", "parallel")),
    )(place, mine, sib)


def _sum_slots(a, name):
    s = a.shape[0]
    a3 = a.reshape(s, math.prod(a.shape[1:-1]), a.shape[-1])
    _, rows, cols = a3.shape
    r = _ew_rows(rows, cols, s + 1)

    def body(a_ref, o_ref):
        acc = a_ref[0].astype(F32)
        for k in range(1, s):
            acc = acc + a_ref[k].astype(F32)
        o_ref[...] = acc

    return pl.pallas_call(body, name=name, grid=(rows // r,),
                          in_specs=[pl.BlockSpec((s, r, cols), lambda i: (0, i, 0))],
                          out_specs=pl.BlockSpec((r, cols), lambda i: (i, 0)),
                          out_shape=jax.ShapeDtypeStruct((rows, cols), F32), compiler_params=_cp(("parallel",)),
                          )(a3).reshape(a.shape[1:])


def _owner_sum(own, got, kind, place, into, name):
    buf, slab = into
    _, rows, cols = got.shape
    r = _ew_rows(rows, cols, 6)
    nb = rows // r

    def body(s_ref, own_ref, got_ref, buf_ref, o_ref):
        acc = own_ref[...].astype(F32)
        for k in range(3):
            acc = acc + got_ref[k].astype(F32)
        o_ref[...] = acc

    if kind == "row":
        own_spec = pl.BlockSpec((None, r, cols), lambda i, s: (s[0], i, 0))
    else:
        own_spec = pl.BlockSpec((None, r, cols), lambda i, s: (0, i, s[0]))
    return pl.pallas_call(
        body, name=name,
        grid_spec=pltpu.PrefetchScalarGridSpec(
            num_scalar_prefetch=1, grid=(nb,),
            in_specs=[own_spec, pl.BlockSpec((3, r, cols), lambda i, s: (0, i, 0)), pl.BlockSpec(memory_space=pl.ANY)],
            out_specs=pl.BlockSpec((None, r, cols), lambda i, s: (slab, s[1] * nb + i, 0))),
        out_shape=jax.ShapeDtypeStruct(buf.shape, buf.dtype), input_output_aliases={3: 0},
        compiler_params=_cp(("parallel",)),
    )(place, own, got, buf)


MESH = pl.DeviceIdType.MESH
ANY = pl.BlockSpec(memory_space=pl.ANY)


def _place():
    x, y, c = lax.axis_index("x"), lax.axis_index("y"), lax.axis_index("c")
    chips = [(1 - x, y), (x, 1 - y), (1 - x, 1 - y)]
    return x, y, c, chips


def _half_part(ref, kind, chip, half):
    if kind == "row":
        h = ref.shape[0] // 8
        return ref.at[pl.ds(pl.multiple_of((2 * chip + half) * h, 16), h), :]
    h, width = ref.shape[0] // 2, ref.shape[1] // 4
    return ref.at[pl.ds(pl.multiple_of(half * h, 16), h), pl.ds(pl.multiple_of(chip * width, 128), width)]


def _gather_copies(w_ref, kind, send, recv):
    x, y, c, chips = _place()
    mine = _half_part(w_ref, kind, 2 * x + y, c)
    return [pltpu.make_async_remote_copy(mine, mine, send.at[j], recv.at[j], device_id=(px, py, c), device_id_type=MESH)
            for j, (px, py) in enumerate(chips)]


def _gather_ici(arrs, kinds, name):
    na = len(arrs)

    def body(*refs):
        outs, send, recv = refs[na:2 * na], refs[2 * na], refs[2 * na + 1]
        cps = [cp for k in range(na) for cp in _gather_copies(outs[k], kinds[k], send.at[k], recv.at[k])]
        for cp in cps:
            cp.start()
        for cp in cps:
            cp.wait()

    return pl.pallas_call(
        body, name=name, in_specs=[ANY] * na, out_specs=[ANY] * na,
        out_shape=[jax.ShapeDtypeStruct(a.shape, a.dtype) for a in arrs],
        input_output_aliases={k: k for k in range(na)},
        scratch_shapes=[pltpu.SemaphoreType.DMA((na, 3))] * 2,
    )(*arrs)


def _pass_copies(outs, kinds, send, recv):
    x, y, c, chips = _place()
    cps, arrivals = [], []
    for k, out in enumerate(outs):
        for j, (px, py) in enumerate(chips):
            mine = _half_part(out, kinds[k], 2 * px + py, c)
            theirs = _half_part(out, kinds[k], 2 * px + py, 1 - c)
            cps.append(pltpu.make_async_remote_copy(mine, mine, send.at[k, j], recv.at[k, j],
                                                    device_id=(x, y, 1 - c), device_id_type=MESH))
            arrivals.append(pltpu.make_async_remote_copy(theirs, theirs, send.at[k, j], recv.at[k, j],
                                                         device_id=(x, y, 1 - c), device_id_type=MESH))
    return cps, arrivals


def _gather_pass(arrs, kinds, name):
    na = len(arrs)

    def body(*refs):
        cps, arrivals = _pass_copies(refs[na:2 * na], kinds, refs[2 * na], refs[2 * na + 1])
        for cp in cps:
            cp.start()
        for cp, arrival in zip(cps, arrivals):
            cp.wait_send()
            arrival.wait_recv()

    return pl.pallas_call(
        body, name=name, in_specs=[ANY] * na, out_specs=[ANY] * na,
        out_shape=[jax.ShapeDtypeStruct(a.shape, a.dtype) for a in arrs],
        input_output_aliases={k: k for k in range(na)},
        scratch_shapes=[pltpu.SemaphoreType.DMA((na, 3))] * 2,
    )(*arrs)


def _swap_sibling(arrs, name):
    na = len(arrs)

    def body(*refs):
        cps = _sibling_copies(refs[:na], refs[na:2 * na], refs[2 * na], refs[2 * na + 1])
        for cp in cps:
            cp.start()
        for cp in cps:
            cp.wait()

    return pl.pallas_call(
        body, name=name, in_specs=[ANY] * na, out_specs=[ANY] * na, out_shape=_sibling_shapes(arrs),
        scratch_shapes=[pltpu.SemaphoreType.DMA((na, 1))] * 2,
    )(*arrs)


def _sibling_shapes(arrs):
    return [jax.ShapeDtypeStruct((a.shape[0], a.shape[1] // 2, a.shape[2]), a.dtype) for a in arrs]


def _sibling_copies(srcs, dsts, send, recv):
    x, y, c, _ = _place()
    cps = []
    for k, (src, dst) in enumerate(zip(srcs, dsts)):
        h = src.shape[1] // 2
        cps.append(pltpu.make_async_remote_copy(src.at[:, pl.ds(pl.multiple_of((1 - c) * h, 16), h), :], dst,
                                                send.at[k, 0], recv.at[k, 0], device_id=(x, y, 1 - c), device_id_type=MESH))
    return cps


def _scatter_shapes(sums, kinds):
    return [jax.ShapeDtypeStruct((3, a.shape[1], a.shape[2] if kind == "row" else a.shape[2] // 4), a.dtype)
            for a, kind in zip(sums, kinds)]


def _scatter_copies(srcs, dsts, kinds, send, recv):
    x, y, c, chips = _place()
    cps = []
    for k, (src, dst) in enumerate(zip(srcs, dsts)):
        for j, (px, py) in enumerate(chips):
            chip = 2 * px + py
            if kinds[k] == "row":
                part = src.at[chip]
            else:
                width = src.shape[2] // 4
                part = src.at[0, :, pl.ds(pl.multiple_of(chip * width, 128), width)]
            cps.append(pltpu.make_async_remote_copy(part, dst.at[j], send.at[k, j], recv.at[k, j],
                                                    device_id=(px, py, c), device_id_type=MESH))
    return cps


def _scatter_chips(sums, kinds, name):
    na = len(sums)

    def body(*refs):
        cps = _scatter_copies(refs[:na], refs[na:2 * na], kinds, refs[2 * na], refs[2 * na + 1])
        for cp in cps:
            cp.start()
        for cp in cps:
            cp.wait()

    return pl.pallas_call(
        body, name=name, in_specs=[ANY] * na, out_specs=[ANY] * na, out_shape=_scatter_shapes(sums, kinds),
        scratch_shapes=[pltpu.SemaphoreType.DMA((na, 3))] * 2,
    )(*sums)


def _join_halves(arrs, name):
    na = len(arrs)

    def body(*refs):
        outs, send, recv = refs[na:2 * na], refs[2 * na], refs[2 * na + 1]
        x, y, c, _ = _place()
        halves = [a.shape[1] // 2 for a in arrs]
        mine = [outs[k].at[:, pl.ds(pl.multiple_of(c * h, 8), h), :] for k, h in enumerate(halves)]
        theirs = [outs[k].at[:, pl.ds(pl.multiple_of((1 - c) * h, 8), h), :] for k, h in enumerate(halves)]
        cps = [pltpu.make_async_remote_copy(mine[k], mine[k], send.at[k], recv.at[k],
                                            device_id=(x, y, 1 - c), device_id_type=MESH) for k in range(na)]
        for cp in cps:
            cp.start()
        for k, cp in enumerate(cps):
            cp.wait_send()
            pltpu.make_async_remote_copy(theirs[k], theirs[k], send.at[k], recv.at[k],
                                         device_id=(x, y, 1 - c), device_id_type=MESH).wait_recv()

    return pl.pallas_call(
        body, name=name, in_specs=[ANY] * na, out_specs=[ANY] * na,
        out_shape=[jax.ShapeDtypeStruct(a.shape, a.dtype) for a in arrs],
        input_output_aliases={k: k for k in range(na)},
        scratch_shapes=[pltpu.SemaphoreType.DMA((na,))] * 2,
    )(*arrs)


def _gather_all(a, name):
    def body(a_ref, o_ref, send, recv, local):
        x, y, c, _ = _place()
        me = 4 * x + 2 * y + c
        mine = pltpu.make_async_copy(a_ref, o_ref.at[me], local)
        mine.start()
        cps = []
        for j in range(1, 8):
            peer = (x ^ (j >> 2), y ^ ((j >> 1) & 1), c ^ (j & 1))
            cps.append(pltpu.make_async_remote_copy(a_ref, o_ref.at[me], send.at[j - 1], recv.at[j - 1],
                                                    device_id=peer, device_id_type=MESH))
        for cp in cps:
            cp.start()
        for cp in cps:
            cp.wait()
        mine.wait()

    return pl.pallas_call(
        body, name=name, in_specs=[ANY], out_specs=ANY,
        out_shape=jax.ShapeDtypeStruct((8,) + a.shape, a.dtype),
        scratch_shapes=[pltpu.SemaphoreType.DMA((7,)), pltpu.SemaphoreType.DMA((7,)), pltpu.SemaphoreType.DMA],
    )(a)


def _pack(parts):
    flat = jnp.concatenate([p.reshape(-1) for p in parts])
    n = flat.shape[0]
    rows = -(-n // (PACK_ROWS * 128)) * PACK_ROWS
    return jnp.pad(flat, (0, rows * 128 - n)).reshape(rows, 128)


def _unpack(buf, like):
    flat, out, off = buf.reshape(-1), [], 0
    for p in like:
        out.append(flat[off:off + p.size].reshape(p.shape))
        off += p.size
    return out


def kernel(x, positions, norm_mix_pre, w_in, dn_conv_w, dn_a_log, dn_dt_bias, dn_norm_w, pool_w, pool_scale, swa_sinks, w_out, norm_mix_post, norm_ffn_pre, ffn_w_up, ffn_conv_w, ffn_conv_b, ffn_w_down, norm_ffn_post, loss_target, m_norm_mix_pre, m_w_in, m_dn_conv_w, m_dn_a_log, m_dn_dt_bias, m_dn_norm_w, m_pool_w, m_pool_scale, m_swa_sinks, m_w_out, m_norm_mix_post, m_norm_ffn_pre, m_ffn_w_up, m_ffn_conv_w, m_ffn_conv_b, m_ffn_w_down, m_norm_ffn_post, v_norm_mix_pre, v_w_in, v_dn_conv_w, v_dn_a_log, v_dn_dt_bias, v_dn_norm_w, v_pool_w, v_pool_scale, v_swa_sinks, v_w_out, v_norm_mix_post, v_norm_ffn_pre, v_ffn_w_up, v_ffn_conv_w, v_ffn_conv_b, v_ffn_w_down, v_norm_ffn_post):
    wts = dict(zip(WEIGHTS, (norm_mix_pre, w_in, dn_conv_w, dn_a_log, dn_dt_bias, dn_norm_w, pool_w, pool_scale, swa_sinks,
                             w_out, norm_mix_post, norm_ffn_pre, ffn_w_up, ffn_conv_w, ffn_conv_b, ffn_w_down, norm_ffn_post)))
    mom = dict(zip(WEIGHTS, (m_norm_mix_pre, m_w_in, m_dn_conv_w, m_dn_a_log, m_dn_dt_bias, m_dn_norm_w, m_pool_w, m_pool_scale,
                             m_swa_sinks, m_w_out, m_norm_mix_post, m_norm_ffn_pre, m_ffn_w_up, m_ffn_conv_w, m_ffn_conv_b,
                             m_ffn_w_down, m_norm_ffn_post)))
    var = dict(zip(WEIGHTS, (v_norm_mix_pre, v_w_in, v_dn_conv_w, v_dn_a_log, v_dn_dt_bias, v_dn_norm_w, v_pool_w, v_pool_scale,
                             v_swa_sinks, v_w_out, v_norm_mix_post, v_norm_ffn_pre, v_ffn_w_up, v_ffn_conv_w, v_ffn_conv_b,
                             v_ffn_w_down, v_norm_ffn_post)))
    c = lax.axis_index("c")
    chip = 2 * lax.axis_index("x") + lax.axis_index("y")
    place = jnp.stack([chip, c]).astype(jnp.int32)
    shards = dict(wts, w_in=jnp.pad(w_in, ((0, 0), (0, 0), (0, IN_SHARD_PAD - IN_SHARD))))
    mats = _WeightGather(shards, place)
    w = dict(wts)
    conv_like = [wts[k] for k in CONV]
    conv_all = _gather_all(_pack(conv_like), "gather_conv")
    for i, k in enumerate(CONV):
        w[k] = jnp.concatenate([_unpack(conv_all[2 * j], conv_like)[i] for j in range(4)], axis=2)

    reduce = _GradReduce(place, {k: shards[k].shape for k in BIG})
    loss, dx, grads = _local_step(x[0], positions[0], loss_target[0], w, mats, reduce)
    loss = lax.psum(loss[0, 0], ("x", "y", "c"))
    g_big = reduce.finish()
    g_big["w_in"] = g_big["w_in"][..., :IN_SHARD]

    small_like = [wts[k] for k in SMALL]
    full_like = small_like + [w[k] for k in CONV]
    g_buf = _sum_slots(_gather_all(_pack([jnp.stack(grads[k]) for k in SMALL + CONV]), "gather_small"), "sum_small")
    g_small = dict(zip(SMALL + CONV, _unpack(g_buf, full_like)))
    for k in CONV:
        width = wts[k].shape[2]
        g_small[k] = lax.dynamic_slice_in_dim(g_small[k], chip * width, width, 2)
    pk = lambda d: _pack([d[k] for k in SMALL + CONV])
    upd = _adamw(pk(wts), pk(g_small), pk(mom), pk(var), "adam_small")
    upd_small = [dict(zip(SMALL + CONV, _unpack(b, small_like + conv_like))) for b in upd]

    g_all, d_all, m_all, v_all = {}, {}, {}, {}
    for k in WEIGHTS:
        if k in BIG:
            g_all[k] = g_big[k]
            d_all[k], m_all[k], v_all[k] = _adamw(wts[k], g_big[k], mom[k], var[k], "adam_" + k)
        else:
            g_all[k], d_all[k], m_all[k], v_all[k] = g_small[k], upd_small[0][k], upd_small[1][k], upd_small[2][k]
    return (loss, dx[None], *[g_all[k] for k in WEIGHTS], *[d_all[k] for k in WEIGHTS],
            *[m_all[k] for k in WEIGHTS], *[v_all[k] for k in WEIGHTS])
```

```python
import functools
import math

import jax
import jax.numpy as jnp
from jax import lax
from jax.experimental import pallas as pl
from jax.experimental.pallas import tpu as pltpu

F32 = jnp.float32
BF16 = jnp.bfloat16

HEAD_DIM = 128
DN_HEADS = 6
DN_CONV = 4
DN_CHUNK = 64
POOL_GROUPS = 4
SWA_HEADS = 6
SWA_KV_HEADS = 2
SWA_GROUP = SWA_HEADS // SWA_KV_HEADS
SWA_BLOCK = 128
ROPE_THETA = 10000.0
FFN_CONV = 3
NORM_EPS = 1e-6
DN_W = DN_HEADS * HEAD_DIM
POOL_W = POOL_GROUPS * HEAD_DIM
SWA_W = SWA_HEADS * HEAD_DIM
SWA_KV_W = SWA_KV_HEADS * HEAD_DIM
MIX_W = DN_W + POOL_W + SWA_W
IN_TRUE = 3 * DN_W + DN_W + 2 * DN_HEADS + POOL_W + SWA_W + 2 * SWA_KV_W
GATE_END = 4 * DN_W + 2 * DN_HEADS
CB_Z = 18
CB_BD = 24
CB_POOL = 25
CB_SQ = 29
CB_SK = 35
CB_SV = 37
IN_PAD = 40 * 128
ADAM_LR, ADAM_B1, ADAM_B2, ADAM_EPS, ADAM_WD, ADAM_STEP = 0.001, 0.9, 0.999, 1e-08, 0.01, 10

VMEM_LIMIT = 48 * 1024 * 1024
PACK_ROWS = 512
MM_TK_MAX = 2816
HIGH = lax.Precision.HIGHEST


def _cp(sem):
    return pltpu.CompilerParams(dimension_semantics=sem, vmem_limit_bytes=VMEM_LIMIT)


def _tile(n, prefs):
    for p in prefs:
        if n % p == 0:
            return p
    return n


def _rows(t):
    return _tile(t, (512, 256, 128))


_DN = {"nn": (((1,), (0,)), ((), ())), "nt": (((1,), (1,)), ((), ())), "tn": (((0,), (0,)), ((), ()))}


def _mm_operand(arr, pick, block, idx):
    if pick is None:
        return pl.BlockSpec(block, idx)
    if pick == "split":
        per = arr.shape[2] // block[1]

        def split_idx(i, j, kk):
            r, c = idx(i, j, kk)
            return lax.div(c, per), r, lax.rem(c, per)

        return pl.BlockSpec((None,) + block, split_idx)
    slab = pick[1]
    return pl.BlockSpec((None,) + block, lambda i, j, kk: (slab,) + idx(i, j, kk))


def _mm(a, b, mode, out_dtype, name, a_pick=None, b_pick=None, carry=None, scatter=None):
    def dims(arr, pick):
        r, c = arr.shape[-2:]
        return (r, c * arr.shape[0]) if pick == "split" else (r, c)

    (a0, a1), (b0, b1) = dims(a, a_pick), dims(b, b_pick)
    k, m = (a0, a1) if mode == "tn" else (a1, a0)
    n = b0 if mode == "nt" else b1
    lim = lambda arr, pick, is_last, full: arr.shape[2] if (pick == "split" and is_last) else full
    tm = _tile(lim(a, a_pick, mode == "tn", m), (1024, 512, 256, 128))
    tn = _tile(lim(b, b_pick, mode != "nt", n), (1408, 1280, 1024, 512, 256, 128))
    k_lim = min(lim(a, a_pick, mode != "tn", k), lim(b, b_pick, mode == "nt", k))
    tk = max([d for d in range(128, min(k_lim, MM_TK_MAX) + 1, 128) if k_lim % d == 0], default=k_lim)
    nk = k // tk

    grid = (m // tm, n // tn, nk)

    riding = carry is not None or scatter is not None
    ns = len(scatter[0]) if scatter is not None else 0

    def body(a_ref, b_ref, *rest):
        if carry is not None:
            _, o_ref, w_ref, *scratch = rest
            copies = _gather_copies(w_ref, carry[1], *scratch[-2:])
        elif scatter is not None:
            o_ref, scratch = rest[ns], rest[2 * ns + 1:]
            if scatter[1] is None:
                copies = _sibling_copies(rest[:ns], rest[ns + 1:2 * ns + 1], *scratch[-2:])
            else:
                copies = _scatter_copies(rest[:ns], rest[ns + 1:2 * ns + 1], scatter[1], *scratch[-2:])
        else:
            o_ref, *scratch = rest
        if riding:
            scratch = scratch[:-2]
            step = (pl.program_id(0) * grid[1] + pl.program_id(1)) * grid[2] + pl.program_id(2)

            @pl.when(step == 0)
            def _():
                for cp in copies:
                    cp.start()
        part = lax.dot_general(a_ref[...], b_ref[...], _DN[mode], preferred_element_type=F32)
        if nk == 1:
            o_ref[...] = part.astype(o_ref.dtype)
        else:
            acc_ref, = scratch
            kk = pl.program_id(2)

            @pl.when(kk == 0)
            def _():
                acc_ref[...] = part

            @pl.when(kk > 0)
            def _():
                acc_ref[...] += part

            @pl.when(kk == nk - 1)
            def _():
                o_ref[...] = acc_ref[...].astype(o_ref.dtype)
        if riding:
            @pl.when(step == grid[0] * grid[1] * grid[2] - 1)
            def _():
                for cp in copies:
                    cp.wait()

    if mode == "tn":
        a_spec = _mm_operand(a, a_pick, (tk, tm), lambda i, j, kk: (kk, i))
    else:
        a_spec = _mm_operand(a, a_pick, (tm, tk), lambda i, j, kk: (i, kk))
    if mode == "nt":
        b_spec = _mm_operand(b, b_pick, (tn, tk), lambda i, j, kk: (j, kk))
    else:
        b_spec = _mm_operand(b, b_pick, (tk, tn), lambda i, j, kk: (kk, j))
    scratch = [pltpu.VMEM((tm, tn), F32)] if nk > 1 else []
    out_spec = pl.BlockSpec((tm, tn), lambda i, j, kk: (i, j))
    out_shape = jax.ShapeDtypeStruct((m, n), out_dtype)
    if not riding:
        return pl.pallas_call(
            body, name=name, grid=grid, in_specs=[a_spec, b_spec], out_specs=out_spec, out_shape=out_shape,
            scratch_shapes=scratch, compiler_params=_cp(("parallel", "parallel", "arbitrary")),
        )(a, b)
    any_space = pl.BlockSpec(memory_space=pl.ANY)
    in_order = _cp(("arbitrary", "arbitrary", "arbitrary"))
    if carry is not None:
        return pl.pallas_call(
            body, name=name, grid=grid, in_specs=[a_spec, b_spec, any_space], out_specs=[out_spec, any_space],
            out_shape=[out_shape, jax.ShapeDtypeStruct(carry[0].shape, carry[0].dtype)], input_output_aliases={2: 1},
            scratch_shapes=scratch + [pltpu.SemaphoreType.DMA((3,))] * 2, compiler_params=in_order,
        )(a, b, carry[0])
    return pl.pallas_call(
        body, name=name, grid=grid, in_specs=[a_spec, b_spec] + [any_space] * ns, out_specs=[out_spec] + [any_space] * ns,
        out_shape=[out_shape] + (_sibling_shapes(scatter[0]) if scatter[1] is None else _scatter_shapes(*scatter)),
        scratch_shapes=scratch + [pltpu.SemaphoreType.DMA((ns, 3))] * 2, compiler_params=in_order,
    )(a, b, *scatter[0])


def _rms(x, w):
    return x * lax.rsqrt(jnp.mean(x * x, axis=-1, keepdims=True) + NORM_EPS) * w


def _norm_fwd(x, w, name):
    t, d = x.shape
    r = _rows(t)

    def body(x_ref, w_ref, h_ref):
        h_ref[...] = _rms(x_ref[...], w_ref[...]).astype(h_ref.dtype)

    return pl.pallas_call(
        body, name=name, grid=(t // r,),
        in_specs=[pl.BlockSpec((r, d), lambda i: (i, 0)), pl.BlockSpec((1, d), lambda i: (0, 0))],
        out_specs=pl.BlockSpec((r, d), lambda i: (i, 0)),
        out_shape=jax.ShapeDtypeStruct((t, d), BF16), compiler_params=_cp(("parallel",)),
    )(x, w)


def _resnorm_fwd(x, y, w, name, passing=None):
    t, d = x.shape
    r = _rows(t)
    steps = t // r
    arrs, kinds = passing if passing is not None else ((), ())
    na = len(arrs)

    def body(x_ref, y_ref, w_ref, *rest):
        o_ref = rest[na]
        if na:
            cps, arrivals = _pass_copies(rest[na + 1:2 * na + 1], kinds, rest[2 * na + 1], rest[2 * na + 2])

            @pl.when(pl.program_id(0) == 0)
            def _():
                for cp in cps:
                    cp.start()

        o_ref[...] = x_ref[...] + _rms(y_ref[...], w_ref[...])
        if na:
            @pl.when(pl.program_id(0) == steps - 1)
            def _():
                for cp, arrival in zip(cps, arrivals):
                    cp.wait_send()
                    arrival.wait_recv()

    row = pl.BlockSpec((r, d), lambda i: (i, 0))
    any_space = pl.BlockSpec(memory_space=pl.ANY)
    res = pl.pallas_call(
        body, name=name, grid=(steps,),
        in_specs=[row, row, pl.BlockSpec((1, d), lambda i: (0, 0))] + [any_space] * na,
        out_specs=[row] + [any_space] * na,
        out_shape=[jax.ShapeDtypeStruct((t, d), F32)] + [jax.ShapeDtypeStruct(a.shape, a.dtype) for a in arrs],
        input_output_aliases={3 + k: 1 + k for k in range(na)},
        scratch_shapes=[pltpu.SemaphoreType.DMA((na, 3))] * 2 if na else [],
        compiler_params=_cp(("arbitrary",) if na else ("parallel",)),
    )(x, y, w, *arrs)
    return (res[0], res[1:]) if na else res[0]


def _norm_bwd(x, w, dh, add, out_dtype, name):
    t, d = x.shape
    r = _rows(t)
    has_add = add is not None

    def body(*refs):
        if has_add:
            x_ref, w_ref, dh_ref, add_ref, dx_ref, dw_ref = refs
        else:
            x_ref, w_ref, dh_ref, dx_ref, dw_ref = refs
        xv = x_ref[...]
        g = dh_ref[...].astype(F32)
        rs = lax.rsqrt(jnp.mean(xv * xv, axis=-1, keepdims=True) + NORM_EPS)
        xh = xv * rs
        gw = g * w_ref[...]
        dx = rs * (gw - xh * jnp.mean(gw * xh, axis=-1, keepdims=True))
        if has_add:
            dx = dx + add_ref[...]
        dx_ref[...] = dx.astype(dx_ref.dtype)

        @pl.when(pl.program_id(0) == 0)
        def _():
            dw_ref[...] = jnp.zeros_like(dw_ref)

        dw_ref[...] += jnp.sum(g * xh, axis=0, keepdims=True)

    row = pl.BlockSpec((r, d), lambda i: (i, 0))
    vec = pl.BlockSpec((1, d), lambda i: (0, 0))
    ins = [x, w, dh] + ([add] if has_add else [])
    return pl.pallas_call(
        body, name=name, grid=(t // r,),
        in_specs=[row, vec, row] + ([row] if has_add else []),
        out_specs=[row, vec],
        out_shape=[jax.ShapeDtypeStruct((t, d), out_dtype), jax.ShapeDtypeStruct((1, d), F32)],
        compiler_params=_cp(("arbitrary",)),
    )(*ins)


def _loss_head(y, target, name):
    t, d = y.shape
    r = _rows(t)

    def body(y_ref, t_ref, l_ref, g_ref):
        e = y_ref[...] - t_ref[...]
        g_ref[...] = e * (1.0 / d)

        @pl.when(pl.program_id(0) == 0)
        def _():
            l_ref[...] = jnp.zeros_like(l_ref)

        l_ref[...] += jnp.sum(e * e) * (0.5 / d)

    row = pl.BlockSpec((r, d), lambda i: (i, 0))
    return pl.pallas_call(
        body, name=name, grid=(t // r,), in_specs=[row, row],
        out_specs=[pl.BlockSpec((1, 128), lambda i: (0, 0)), row],
        out_shape=[jax.ShapeDtypeStruct((1, 128), F32), jax.ShapeDtypeStruct((t, d), F32)],
        compiler_params=_cp(("arbitrary",)),
    )(y, target)


def _down(x, s):
    return x if s == 0 else pltpu.roll(x, s, 0)


def _up(x, s):
    return x if s == 0 else pltpu.roll(x, x.shape[0] - s, 0)


def _halo(t, r, hh, tc, col):
    q = r // hh
    last = t // hh - 1
    tile = pl.BlockSpec((r, tc), lambda j, i: (i, col(j)))
    prev = pl.BlockSpec((hh, tc), lambda j, i: (jnp.maximum(i * q - 1, 0), col(j)))
    nxt = pl.BlockSpec((hh, tc), lambda j, i: (jnp.minimum((i + 1) * q, last), col(j)))
    return tile, prev, nxt


def _sig(x):
    return 1.0 / (1.0 + jnp.exp(-x))


def _dsilu(x, s):
    return s * (1.0 + x * (1.0 - s))


def _dn_pre_fwd(p, conv_w, name):
    t = p.shape[0]
    r = _rows(t)

    def body(x_ref, xp_ref, w_ref, o_ref):
        j, i = pl.program_id(0), pl.program_id(1)
        xe = jnp.concatenate([jnp.where(i == 0, 0.0, xp_ref[...]), x_ref[...]], axis=0)
        c = sum(_down(xe, DN_CONV - 1 - k) * w_ref[pl.ds(k, 1), :] for k in range(DN_CONV))[8:]
        a = c * _sig(c)
        for h in range(DN_HEADS):
            ah = a[:, h * 128:(h + 1) * 128]
            fac = lax.rsqrt(jnp.sum(ah * ah, axis=-1, keepdims=True) + NORM_EPS)
            o_ref[:, h * 128:(h + 1) * 128] = ah * jnp.where(j == 0, fac * HEAD_DIM ** -0.5, jnp.where(j == 1, fac, 1.0))

    tile, prev, _ = _halo(t, r, 8, DN_W, lambda j: j)
    return pl.pallas_call(
        body, name=name, grid=(3, t // r),
        in_specs=[tile, prev, pl.BlockSpec((DN_CONV, DN_W), lambda j, i: (0, j))],
        out_specs=tile, out_shape=jax.ShapeDtypeStruct((t, 3 * DN_W), F32),
        compiler_params=_cp(("parallel", "parallel")),
    )(p, p, conv_w)


def _dn_pre_bwd(p, conv_w, dqkv, name):
    t = p.shape[0]
    r = _rows(t)
    ni = t // r

    def body(x_ref, xp_ref, xn_ref, w_ref, d_ref, dn_ref, dx_ref, dw_ref):
        j, i = pl.program_id(0), pl.program_id(1)
        xe = jnp.concatenate([jnp.where(i == 0, 0.0, xp_ref[...]), x_ref[...], xn_ref[...]], axis=0)
        de = jnp.concatenate([jnp.zeros((8, DN_W), F32), d_ref[...], jnp.where(i == ni - 1, 0.0, dn_ref[...])], axis=0)
        xs = [_down(xe, DN_CONV - 1 - k) for k in range(DN_CONV)]
        c = sum(xs[k] * w_ref[pl.ds(k, 1), :] for k in range(DN_CONV))
        s = _sig(c)
        a = c * s
        das = []
        for h in range(DN_HEADS):
            ah, dh = a[:, h * 128:(h + 1) * 128], de[:, h * 128:(h + 1) * 128]
            fac = lax.rsqrt(jnp.sum(ah * ah, axis=-1, keepdims=True) + NORM_EPS)
            dnorm = fac * dh - ah * (fac * fac * fac) * jnp.sum(dh * ah, axis=-1, keepdims=True)
            das.append(jnp.where(j == 0, dnorm * HEAD_DIM ** -0.5, jnp.where(j == 1, dnorm, dh)))
        dc = jnp.concatenate(das, axis=1) * _dsilu(c, s)
        dx_ref[...] = sum(_up(dc, DN_CONV - 1 - k) * w_ref[pl.ds(k, 1), :] for k in range(DN_CONV))[8:8 + r]

        @pl.when(i == 0)
        def _():
            dw_ref[...] = jnp.zeros_like(dw_ref)

        for k in range(DN_CONV):
            dw_ref[pl.ds(k, 1), :] += jnp.sum((dc * xs[k])[8:8 + r], axis=0, keepdims=True)

    tile, prev, nxt = _halo(t, r, 8, DN_W, lambda j: j)
    wspec = pl.BlockSpec((DN_CONV, DN_W), lambda j, i: (0, j))
    return pl.pallas_call(
        body, name=name, grid=(3, ni),
        in_specs=[tile, prev, nxt, wspec, tile, nxt],
        out_specs=[tile, wspec],
        out_shape=[jax.ShapeDtypeStruct((t, 3 * DN_W), F32), jax.ShapeDtypeStruct((DN_CONV, 3 * DN_W), F32)],
        compiler_params=_cp(("parallel", "arbitrary")),
    )(p, p, p, conv_w, dqkv, dqkv)


def _ffn_act_fwd(up, cw, cb, name):
    t, f2 = up.shape
    f = f2 // 2
    r = _tile(t, (512, 256, 128))
    tc = _tile(f, (512, 256, 128))
    nj = f // tc

    def body(a_ref, ap_ref, b_ref, bp_ref, wa_ref, wb_ref, ca_ref, cb_ref, o_ref):
        i = pl.program_id(1)

        def conv(x_ref, xp_ref, w_ref, c_ref):
            xe = jnp.concatenate([jnp.where(i == 0, 0.0, xp_ref[...]), x_ref[...]], axis=0)
            return sum(_down(xe, FFN_CONV - 1 - k) * w_ref[pl.ds(k, 1), :] for k in range(FFN_CONV))[8:] + c_ref[...]

        ua = conv(a_ref, ap_ref, wa_ref, ca_ref)
        ub = conv(b_ref, bp_ref, wb_ref, cb_ref)
        o_ref[...] = (ua * _sig(ua) * ub).astype(o_ref.dtype)

    ta, pa, _ = _halo(t, r, 8, tc, lambda j: j)
    tb, pb, _ = _halo(t, r, 8, tc, lambda j: j + nj)
    wa = pl.BlockSpec((FFN_CONV, tc), lambda j, i: (0, j))
    wb = pl.BlockSpec((FFN_CONV, tc), lambda j, i: (0, j + nj))
    ca = pl.BlockSpec((1, tc), lambda j, i: (0, j))
    cbs = pl.BlockSpec((1, tc), lambda j, i: (0, j + nj))
    return pl.pallas_call(
        body, name=name, grid=(nj, t // r),
        in_specs=[ta, pa, tb, pb, wa, wb, ca, cbs], out_specs=ta,
        out_shape=jax.ShapeDtypeStruct((t, f), BF16), compiler_params=_cp(("parallel", "parallel")),
    )(up, up, up, up, cw, cw, cb, cb)


def _ffn_act_bwd(up, cw, cb, dact, name):
    t, f2 = up.shape
    f = f2 // 2
    r = _tile(t, (512, 256, 128))
    ni = t // r
    tc = _tile(f, (512, 256, 128))
    nj = f // tc

    def body(a_ref, ap_ref, an_ref, b_ref, bp_ref, bn_ref, wa_ref, wb_ref, ca_ref, cb_ref, d_ref, dn_ref,
             du_ref, dw_ref, dc_ref):
        i = pl.program_id(1)
        dua_ref, dub_ref, dwa_ref, dwb_ref, dca_ref, dcb_ref = (du_ref.at[0], du_ref.at[1], dw_ref.at[0], dw_ref.at[1],
                                                                  dc_ref.at[0], dc_ref.at[1])

        def ext(x_ref, xp_ref, xn_ref):
            return jnp.concatenate([jnp.where(i == 0, 0.0, xp_ref[...]), x_ref[...], xn_ref[...]], axis=0)

        ae, be = ext(a_ref, ap_ref, an_ref), ext(b_ref, bp_ref, bn_ref)
        as_ = [_down(ae, FFN_CONV - 1 - k) for k in range(FFN_CONV)]
        bs_ = [_down(be, FFN_CONV - 1 - k) for k in range(FFN_CONV)]
        ua = sum(as_[k] * wa_ref[pl.ds(k, 1), :] for k in range(FFN_CONV)) + ca_ref[...]
        ub = sum(bs_[k] * wb_ref[pl.ds(k, 1), :] for k in range(FFN_CONV)) + cb_ref[...]
        de = jnp.concatenate([jnp.zeros((8, tc), F32), d_ref[...].astype(F32),
                              jnp.where(i == ni - 1, 0.0, dn_ref[...].astype(F32))], axis=0)
        s = _sig(ua)
        dua = de * ub * _dsilu(ua, s)
        dub = de * ua * s
        dua_ref[...] = sum(_up(dua, FFN_CONV - 1 - k) * wa_ref[pl.ds(k, 1), :] for k in range(FFN_CONV))[8:8 + r].astype(dua_ref.dtype)
        dub_ref[...] = sum(_up(dub, FFN_CONV - 1 - k) * wb_ref[pl.ds(k, 1), :] for k in range(FFN_CONV))[8:8 + r].astype(dub_ref.dtype)

        @pl.when(i == 0)
        def _():
            dw_ref[...] = jnp.zeros_like(dw_ref)
            dc_ref[...] = jnp.zeros_like(dc_ref)

        for k in range(FFN_CONV):
            dwa_ref[pl.ds(k, 1), :] += jnp.sum((dua * as_[k])[8:8 + r], axis=0, keepdims=True)
            dwb_ref[pl.ds(k, 1), :] += jnp.sum((dub * bs_[k])[8:8 + r], axis=0, keepdims=True)
        dca_ref[...] += jnp.sum(dua[8:8 + r], axis=0, keepdims=True)
        dcb_ref[...] += jnp.sum(dub[8:8 + r], axis=0, keepdims=True)

    ta, pa, na = _halo(t, r, 8, tc, lambda j: j)
    tb, pb, nb = _halo(t, r, 8, tc, lambda j: j + nj)
    wa = pl.BlockSpec((FFN_CONV, tc), lambda j, i: (0, j))
    wb = pl.BlockSpec((FFN_CONV, tc), lambda j, i: (0, j + nj))
    ca = pl.BlockSpec((1, tc), lambda j, i: (0, j))
    cbs = pl.BlockSpec((1, tc), lambda j, i: (0, j + nj))
    return pl.pallas_call(
        body, name=name, grid=(nj, ni),
        in_specs=[ta, pa, na, tb, pb, nb, wa, wb, ca, cbs, ta, na],
        out_specs=[pl.BlockSpec((2, r, tc), lambda j, i: (0, i, j)), pl.BlockSpec((2, FFN_CONV, tc), lambda j, i: (0, 0, j)),
                   pl.BlockSpec((2, 1, tc), lambda j, i: (0, 0, j))],
        out_shape=[jax.ShapeDtypeStruct((2, t, f), BF16), jax.ShapeDtypeStruct((2, FFN_CONV, f), F32),
                   jax.ShapeDtypeStruct((2, 1, f), F32)],
        compiler_params=_cp(("parallel", "arbitrary")),
    )(up, up, up, up, up, up, cw, cw, cb, cb, dact, dact)


def _pool_pick(g, vals):
    return jnp.where(g == 0, vals[0], jnp.where(g == 1, vals[1], jnp.where(g == 2, vals[2], vals[3])))


def _pool_pre(xe, g, t0):
    s1 = xe + _down(xe, 1)
    s2 = s1 + _down(s1, 2)
    s3 = s2 + _down(s2, 4)
    s4 = s3 + _down(s3, 8)
    r = xe.shape[0] - 16
    pos = (t0 + lax.broadcasted_iota(jnp.int32, (r, 1), 0)).astype(F32)
    cnt = jnp.minimum(pos + 1.0, _pool_pick(g, (2.0, 4.0, 8.0, 16.0)))
    return _pool_pick(g, (s1, s2, s3, s4))[16:] / cnt - xe[16:]


def _pool_fwd(p, pool_w, scale, name):
    t = p.shape[0]
    r = _tile(t, (1024, 256, 128))

    def body(x_ref, xp_ref, w_ref, sc_ref, o_ref):
        g, i = pl.program_id(0), pl.program_id(1)
        xe = jnp.concatenate([jnp.where(i == 0, 0.0, xp_ref[...]), x_ref[...]], axis=0)
        pre = _pool_pre(xe, g, i * r)
        o_ref[...] = jnp.dot(pre, w_ref[0], preferred_element_type=F32) * sc_ref[...]

    tile, prev, _ = _halo(t, r, 16, 128, lambda j: CB_POOL + j)
    return pl.pallas_call(
        body, name=name, grid=(POOL_GROUPS, t // r),
        in_specs=[tile, prev, pl.BlockSpec((1, 128, 128), lambda j, i: (j, 0, 0)), pl.BlockSpec((1, 128), lambda j, i: (0, j))],
        out_specs=pl.BlockSpec((r, 128), lambda j, i: (i, j)),
        out_shape=jax.ShapeDtypeStruct((t, POOL_W), F32), compiler_params=_cp(("parallel", "parallel")),
    )(p, p, pool_w, scale)


def _pool_bwd(p, pool_w, scale, dycat, name):
    t = p.shape[0]
    r = _tile(t, (1024, 256, 128))
    ni = t // r

    def body(x_ref, xp_ref, w_ref, sc_ref, d_ref, dn_ref, dx_ref, dw_ref, dsc_ref):
        g, i = pl.program_id(0), pl.program_id(1)
        xe = jnp.concatenate([jnp.where(i == 0, 0.0, xp_ref[...]), x_ref[...]], axis=0)
        pre = _pool_pre(xe, g, i * r)
        w = w_ref[0]
        dy = d_ref[...]
        dye = jnp.concatenate([dy, jnp.where(i == ni - 1, 0.0, dn_ref[...])], axis=0)
        dpre = lax.dot_general(dye * sc_ref[...], w, _DN["nt"], preferred_element_type=F32)
        pos = (i * r + lax.broadcasted_iota(jnp.int32, (r + 16, 1), 0)).astype(F32)
        dm = dpre / jnp.minimum(pos + 1.0, _pool_pick(g, (2.0, 4.0, 8.0, 16.0)))
        a1 = dm + _up(dm, 1)
        a2 = a1 + _up(a1, 2)
        a3 = a2 + _up(a2, 4)
        a4 = a3 + _up(a3, 8)
        dx_ref[...] = (_pool_pick(g, (a1, a2, a3, a4)) - dpre)[:r]

        @pl.when(i == 0)
        def _():
            dw_ref[...] = jnp.zeros_like(dw_ref)
            dsc_ref[...] = jnp.zeros_like(dsc_ref)

        dw_ref[0] += lax.dot_general(pre, dy * sc_ref[...], _DN["tn"], preferred_element_type=F32)
        dsc_ref[...] += jnp.sum(dy * jnp.dot(pre, w, preferred_element_type=F32), axis=0, keepdims=True)

    tile, prev, _ = _halo(t, r, 16, 128, lambda j: CB_POOL + j)
    dtile, _, dnxt = _halo(t, r, 16, 128, lambda j: DN_W // 128 + j)
    wspec = pl.BlockSpec((1, 128, 128), lambda j, i: (j, 0, 0))
    sspec = pl.BlockSpec((1, 128), lambda j, i: (0, j))
    return pl.pallas_call(
        body, name=name, grid=(POOL_GROUPS, ni),
        in_specs=[tile, prev, wspec, sspec, dtile, dnxt],
        out_specs=[pl.BlockSpec((r, 128), lambda j, i: (i, j)), wspec, sspec],
        out_shape=[jax.ShapeDtypeStruct((t, POOL_W), F32), jax.ShapeDtypeStruct((POOL_GROUPS, 128, 128), F32),
                   jax.ShapeDtypeStruct((1, POOL_W), F32)],
        compiler_params=_cp(("parallel", "arbitrary")),
    )(p, p, pool_w, scale, dycat, dycat)


_DNB = {"nn": (((2,), (1,)), ((0,), (0,))), "nt": (((2,), (2,)), ((0,), (0,))), "tn": (((1,), (1,)), ((0,), (0,)))}


def _dot(a, b, mode="nn", precision=None):
    dn = _DNB[mode] if a.ndim == 3 else _DN[mode]
    return lax.dot_general(a, b, dn, precision=precision, preferred_element_type=F32)


@functools.partial(jax.custom_vjp, nondiff_argnums=(2,))
def _bdot(a, b, mode):
    return _dot(a.astype(BF16), b.astype(BF16), mode)


def _bdot_fwd(a, b, mode):
    return _bdot(a, b, mode), (a, b)


def _bdot_bwd(mode, res, g):
    a, b = res
    if mode == "nn":
        return _bdot(g, b, "nt"), _bdot(a, g, "tn")
    if mode == "nt":
        return _bdot(g, b, "nn"), _bdot(g, a, "tn")
    return _bdot(b, g, "nt"), _bdot(a, g, "nn")


_bdot.defvjp(_bdot_fwd, _bdot_bwd)


def _dn_consts():
    c = DN_CHUNK
    ii = lax.broadcasted_iota(jnp.int32, (c, c), 0)
    jj = lax.broadcasted_iota(jnp.int32, (c, c), 1)
    one, zero = jnp.ones((c, c), F32), jnp.zeros((c, c), F32)
    return dict(ltri=jnp.where(ii >= jj, one, zero), utri=jnp.where(ii <= jj, one, zero), ones=one,
                causal=ii >= jj, strict=ii > jj, eye=jnp.where(ii == jj, one, zero))


def _dn_chunk(q, k, v, z, bcol, acol, s_in, alog, dtb, nw, cs):
    c = DN_CHUNK
    hh = q.shape[0]
    per_head = lambda m: jnp.broadcast_to(m, (hh, c, c))
    beta = _sig(bcol)
    xa = acol + dtb
    g = -jnp.exp(alog) * (jnp.maximum(xa, 0.0) + jnp.log(1.0 + jnp.exp(-jnp.abs(xa))))
    gb = jnp.broadcast_to(g, (hh, c, HEAD_DIM))
    gbc = jnp.broadcast_to(g, (hh, c, c))
    gc = _dot(per_head(cs["ltri"]), gb, precision=HIGH)
    gcol = gc[:, :, :c]
    grow = jnp.swapaxes(gcol, 1, 2)
    decay = jnp.exp(jnp.where(cs["causal"], gcol - grow, -1e30))
    kb = k * beta
    vb = v * beta
    nil = -jnp.where(cs["strict"], _bdot(kb, k, "nt") * decay, 0.0)
    inv = cs["eye"] + nil
    powk = nil
    for _ in range(int(math.log2(c)) - 1):
        powk = _bdot(powk, powk, "nn")
        inv = _bdot(inv, cs["eye"] + powk, "nn")
    eg = jnp.exp(gc)
    u = _bdot(inv, vb, "nn")
    w = _bdot(inv, kb * eg, "nn")
    a = _bdot(q, k, "nt") * decay
    v_new = u - _bdot(w, s_in, "nn")
    o = _bdot(q * eg, s_in, "nn") + _bdot(a, v_new, "nn")
    glast = jnp.sum(gb, axis=1, keepdims=True)
    s_out = s_in * jnp.exp(glast) + _bdot(k * jnp.exp(glast - gc), v_new, "tn")
    on = o * lax.rsqrt(jnp.mean(o * o, axis=-1, keepdims=True) + NORM_EPS) * nw
    return on * (z * _sig(z)), s_out


def _lane_pick(x, lane, idx):
    return jnp.sum(jnp.where(lane == idx, x, 0.0), axis=1, keepdims=True)


def _dn_load(q_ref, k_ref, v_ref, z_ref, bd_ref, al_ref, dt_ref, nw_ref, s_in):
    lane = lax.broadcasted_iota(jnp.int32, (1, 128), 1)
    bd, al, dt = bd_ref[...], al_ref[...], dt_ref[...]
    heads = range(DN_HEADS)
    wide = lambda ref: jnp.stack([ref[:, h * 128:(h + 1) * 128] for h in heads], axis=0)
    col = lambda x, off: jnp.stack([_lane_pick(x, lane, off + h) for h in heads], axis=0)
    return (wide(q_ref), wide(k_ref), wide(v_ref), wide(z_ref), col(bd, 0), col(bd, DN_HEADS), s_in,
            col(al, 0), col(dt, 0), nw_ref[...])


def _dn_fwd(qkv, p, alog, dtb, nw, name):
    t = qkv.shape[0]
    c = DN_CHUNK
    n = t // c

    def body(q_ref, k_ref, v_ref, z_ref, bd_ref, al_ref, dt_ref, nw_ref, y_ref, ss_ref, s_scr):
        @pl.when(pl.program_id(0) == 0)
        def _():
            s_scr[...] = jnp.zeros_like(s_scr)

        s_in = s_scr[...]
        y, s_out = _dn_chunk(*_dn_load(q_ref, k_ref, v_ref, z_ref, bd_ref, al_ref, dt_ref, nw_ref, s_in), _dn_consts())
        ss_ref[0] = s_in
        s_scr[...] = s_out
        for h in range(DN_HEADS):
            y_ref[:, h * 128:(h + 1) * 128] = y[h]

    wide = lambda j: pl.BlockSpec((c, DN_W), lambda i: (i, j))
    vec = pl.BlockSpec((1, 128), lambda i: (0, 0))
    return pl.pallas_call(
        body, name=name, grid=(n,),
        in_specs=[wide(0), wide(1), wide(2), wide(3), pl.BlockSpec((c, 128), lambda i: (i, CB_BD)), vec, vec, vec],
        out_specs=[wide(0), pl.BlockSpec((1, DN_HEADS, 128, 128), lambda i: (i, 0, 0, 0))],
        out_shape=[jax.ShapeDtypeStruct((t, DN_W), F32), jax.ShapeDtypeStruct((n, DN_HEADS, 128, 128), F32)],
        scratch_shapes=[pltpu.VMEM((DN_HEADS, 128, 128), F32)],
        compiler_params=_cp(("arbitrary",)),
    )(qkv, qkv, qkv, p, p, alog, dtb, nw)


def _dn_bwd(qkv, p, alog, dtb, nw, states, dycat, name, carry=None):
    t = qkv.shape[0]
    c = DN_CHUNK
    n = t // c
    sums, kinds = carry if carry is not None else ((), ())
    na = len(sums)

    def body(*refs):
        q_ref, k_ref, v_ref, z_ref, bd_ref, al_ref, dt_ref, nw_ref, ss_ref, dy_ref = refs[:10]
        dqkv_ref, dz_ref, dbd_ref, dal_ref, ddt_ref, dnw_ref = refs[10 + na:16 + na]
        ds_scr = refs[16 + 2 * na]
        if na:
            copies = _scatter_copies(refs[10:10 + na], refs[16 + na:16 + 2 * na], kinds, *refs[17 + 2 * na:])

        @pl.when(pl.program_id(0) == 0)
        def _():
            ds_scr[...] = jnp.zeros_like(ds_scr)
            dal_ref[...] = jnp.zeros_like(dal_ref)
            ddt_ref[...] = jnp.zeros_like(ddt_ref)
            dnw_ref[...] = jnp.zeros_like(dnw_ref)
            if na:
                for cp in copies:
                    cp.start()

        lane = lax.broadcasted_iota(jnp.int32, (1, 128), 1)
        args = _dn_load(q_ref, k_ref, v_ref, z_ref, bd_ref, al_ref, dt_ref, nw_ref, ss_ref[0])
        dy = jnp.stack([dy_ref[:, h * 128:(h + 1) * 128] for h in range(DN_HEADS)], axis=0)
        _, vjp = jax.vjp(functools.partial(_dn_chunk, cs=_dn_consts()), *args)
        gq, gk, gv, gz, gb, ga, gs, gal, gdt, gnw = vjp((dy, ds_scr[...]))
        ds_scr[...] = gs
        dbd = jnp.zeros((c, 128), F32)
        dal = jnp.zeros((1, 128), F32)
        ddt = jnp.zeros((1, 128), F32)
        for h in range(DN_HEADS):
            sl = slice(h * 128, (h + 1) * 128)
            dqkv_ref[:, sl] = gq[h]
            dqkv_ref[:, DN_W + h * 128:DN_W + (h + 1) * 128] = gk[h]
            dqkv_ref[:, 2 * DN_W + h * 128:2 * DN_W + (h + 1) * 128] = gv[h]
            dz_ref[:, sl] = gz[h]
            dbd = dbd + jnp.where(lane == h, gb[h], 0.0) + jnp.where(lane == DN_HEADS + h, ga[h], 0.0)
            dal = dal + jnp.where(lane == h, gal[h], 0.0)
            ddt = ddt + jnp.where(lane == h, gdt[h], 0.0)
        dbd_ref[...] = dbd
        dal_ref[...] += dal
        ddt_ref[...] += ddt
        dnw_ref[...] += gnw

        if na:
            @pl.when(pl.program_id(0) == n - 1)
            def _():
                for cp in copies:
                    cp.wait()

    rev = lambda i: n - 1 - i
    wide = lambda j: pl.BlockSpec((c, DN_W), lambda i: (rev(i), j))
    vec = pl.BlockSpec((1, 128), lambda i: (0, 0))
    any_space = pl.BlockSpec(memory_space=pl.ANY)
    return pl.pallas_call(
        body, name=name, grid=(n,),
        in_specs=[wide(0), wide(1), wide(2), wide(3), pl.BlockSpec((c, 128), lambda i: (rev(i), CB_BD)), vec, vec, vec,
                  pl.BlockSpec((1, DN_HEADS, 128, 128), lambda i: (rev(i), 0, 0, 0)), wide(0)] + [any_space] * na,
        out_specs=[pl.BlockSpec((c, 3 * DN_W), lambda i: (rev(i), 0)), wide(0),
                   pl.BlockSpec((c, 128), lambda i: (rev(i), 0)), vec, vec, vec] + [any_space] * na,
        out_shape=[jax.ShapeDtypeStruct((t, 3 * DN_W), F32), jax.ShapeDtypeStruct((t, DN_W), F32),
                   jax.ShapeDtypeStruct((t, 128), F32), jax.ShapeDtypeStruct((1, 128), F32),
                   jax.ShapeDtypeStruct((1, 128), F32), jax.ShapeDtypeStruct((1, 128), F32)] + _scatter_shapes(sums, kinds),
        scratch_shapes=[pltpu.VMEM((DN_HEADS, 128, 128), F32)] + ([pltpu.SemaphoreType.DMA((na, 3))] * 2 if na else []),
        compiler_params=_cp(("arbitrary",)),
    )(qkv, qkv, qkv, p, p, alog, dtb, nw, states, dycat, *sums)


def _rope(x, cosf, sins):
    return x * cosf + pltpu.roll(x, HEAD_DIM // 2, 1) * sins


def _rope_t(d, cosf, sins):
    return d * cosf + pltpu.roll(d * sins, HEAD_DIM // 2, 1)


def _swa_masks():
    b = SWA_BLOCK
    i = lax.broadcasted_iota(jnp.int32, (SWA_GROUP * b, b), 0) & (b - 1)
    j = lax.broadcasted_iota(jnp.int32, (SWA_GROUP * b, b), 1)
    return j > i, j <= i


def _swa_sink_col(sinks_ref, h):
    b = SWA_BLOCK
    r = lax.broadcasted_iota(jnp.int32, (SWA_GROUP * b, 1), 0)
    s = [sinks_ref[h * SWA_GROUP + g] for g in range(SWA_GROUP)]
    return jnp.where(r < b, s[0], jnp.where(r < 2 * b, s[1], s[2]))


def _swa_specs(t, h_first):
    nb = t // SWA_BLOCK

    def at(col, off):
        def imap(h, n):
            return (jnp.clip(n + off, 0, nb - 1), col(h))
        return pl.BlockSpec((SWA_BLOCK, 128), imap)
    return at


def _swa_fwd(p, cosf, sins, sinks, name):
    t = p.shape[0]
    b = SWA_BLOCK
    nb = t // b
    at = _swa_specs(t, None)
    scale = HEAD_DIM ** -0.5

    def body(q0, q1, q2, kp, kc, vp, vc, cc, sc, cp, sp, sinks_ref, o_ref, lse_ref):
        h, n = pl.program_id(0), pl.program_id(1)
        qs = jnp.concatenate([_rope(q[...], cc[...], sc[...]) for q in (q0, q1, q2)], axis=0)
        ks = jnp.concatenate([_rope(kp[...], cp[...], sp[...]), _rope(kc[...], cc[...], sc[...])], axis=0)
        vs = jnp.concatenate([vp[...], vc[...]], axis=0)
        mp, mc = _swa_masks()
        mask = jnp.concatenate([mp & (n > 0), mc], axis=1)
        s = jnp.where(mask, _dot(qs, ks, "nt") * scale, -1e30)
        sink = _swa_sink_col(sinks_ref, h)
        m = jnp.maximum(jnp.max(s, axis=1, keepdims=True), sink)
        e = jnp.exp(s - m)
        l = jnp.sum(e, axis=1, keepdims=True) + jnp.exp(sink - m)
        o = _dot(e, vs) / l
        lse = m + jnp.log(l)
        lane = lax.broadcasted_iota(jnp.int32, (1, 128), 1)
        tile = jnp.zeros((b, 128), F32)
        for g in range(SWA_GROUP):
            o_ref[:, g * 128:(g + 1) * 128] = o[g * b:(g + 1) * b]
            tile = tile + jnp.where(lane == g, lse[g * b:(g + 1) * b], 0.0)
        lse_ref[0] = tile

    qcol = lambda g: (lambda h: CB_SQ + h * SWA_GROUP + g)
    kcol, vcol, one = (lambda h: CB_SK + h), (lambda h: CB_SV + h), (lambda h: 0)
    in_specs = [at(qcol(0), 0), at(qcol(1), 0), at(qcol(2), 0), at(kcol, -1), at(kcol, 0), at(vcol, -1), at(vcol, 0),
                at(one, 0), at(one, 0), at(one, -1), at(one, -1), pl.BlockSpec(memory_space=pltpu.SMEM)]
    return pl.pallas_call(
        body, name=name, grid=(SWA_KV_HEADS, nb), in_specs=in_specs,
        out_specs=[pl.BlockSpec((b, SWA_GROUP * 128), lambda h, n: (n, h)), pl.BlockSpec((1, b, 128), lambda h, n: (h, n, 0))],
        out_shape=[jax.ShapeDtypeStruct((t, SWA_W), F32), jax.ShapeDtypeStruct((SWA_KV_HEADS, t, 128), F32)],
        compiler_params=_cp(("parallel", "parallel")),
    )(p, p, p, p, p, p, p, cosf, sins, cosf, sins, sinks)


def _swa_bwd(p, cosf, sins, sinks, o, lse, dycat, name):
    t = p.shape[0]
    b = SWA_BLOCK
    nb = t // b
    at = _swa_specs(t, None)
    scale = HEAD_DIM ** -0.5
    gb = SWA_GROUP * b

    def body(q0, q1, q2, r0, r1, r2, kp, kc, vp, vc, cc, sc, cp, sp, cn, sn, d0, d1, d2, e0, e1, e2,
             oc_ref, on_ref, lc_ref, ln_ref, sinks_ref, dq_ref, dk_ref, dv_ref, dsk_ref):
        h, n = pl.program_id(0), pl.program_id(1)
        lane = lax.broadcasted_iota(jnp.int32, (1, 128), 1)
        stack = lambda refs: jnp.concatenate([x[...] for x in refs], axis=0)
        q_c = jnp.concatenate([_rope(q[...], cc[...], sc[...]) for q in (q0, q1, q2)], axis=0)
        q_n = jnp.concatenate([_rope(q[...], cn[...], sn[...]) for q in (r0, r1, r2)], axis=0)
        k_p = _rope(kp[...], cp[...], sp[...])
        k_c = _rope(kc[...], cc[...], sc[...])
        do_c, do_n = stack((d0, d1, d2)), stack((e0, e1, e2))
        o_c = jnp.concatenate([oc_ref[:, g * 128:(g + 1) * 128] for g in range(SWA_GROUP)], axis=0)
        o_n = jnp.concatenate([on_ref[:, g * 128:(g + 1) * 128] for g in range(SWA_GROUP)], axis=0)
        lse_c = jnp.concatenate([_lane_pick(lc_ref[0], lane, g) for g in range(SWA_GROUP)], axis=0)
        lse_n = jnp.concatenate([_lane_pick(ln_ref[0], lane, g) for g in range(SWA_GROUP)], axis=0)
        dl_c = jnp.sum(do_c * o_c, axis=1, keepdims=True)
        dl_n = jnp.sum(do_n * o_n, axis=1, keepdims=True)
        mp, mc = _swa_masks()

        def pair(qr, kr, v, do, lse_, dl, mask):
            s = _dot(qr, kr, "nt") * scale
            pr = jnp.where(mask, jnp.exp(s - lse_), 0.0)
            ds = pr * (_dot(do, v, "nt") - dl) * scale
            return _dot(ds, kr), _dot(ds, qr, "tn"), _dot(pr, do, "tn")

        dq_a, _, _ = pair(q_c, k_p, vp[...], do_c, lse_c, dl_c, mp & (n > 0))
        dq_b, dk_b, dv_b = pair(q_c, k_c, vc[...], do_c, lse_c, dl_c, mc)
        _, dk_n, dv_n = pair(q_n, k_c, vc[...], do_n, lse_n, dl_n, mp & (n < nb - 1))
        dq = dq_a + dq_b
        for g in range(SWA_GROUP):
            dq_ref[:, g * 128:(g + 1) * 128] = _rope_t(dq[g * b:(g + 1) * b], cc[...], sc[...])
        dk_ref[...] = _rope_t(dk_b + dk_n, cc[...], sc[...])
        dv_ref[...] = dv_b + dv_n

        @pl.when(n == 0)
        def _():
            dsk_ref[...] = jnp.zeros_like(dsk_ref)

        w = -jnp.exp(_swa_sink_col(sinks_ref, h) - lse_c) * dl_c
        acc = jnp.zeros((1, 128), F32)
        for g in range(SWA_GROUP):
            acc = acc + jnp.where(lane == g, jnp.sum(w[g * b:(g + 1) * b], axis=0, keepdims=True), 0.0)
        dsk_ref[0] += jnp.broadcast_to(acc, (8, 128))

    qcol = lambda g: (lambda h: CB_SQ + h * SWA_GROUP + g)
    dcol = lambda g: (lambda h: (DN_W + POOL_W) // 128 + h * SWA_GROUP + g)
    kcol, vcol, one = (lambda h: CB_SK + h), (lambda h: CB_SV + h), (lambda h: 0)
    wide = lambda off: pl.BlockSpec((b, SWA_GROUP * 128), lambda h, n: (jnp.clip(n + off, 0, nb - 1), h))
    lspec = lambda off: pl.BlockSpec((1, b, 128), lambda h, n: (h, jnp.clip(n + off, 0, nb - 1), 0))
    in_specs = ([at(qcol(g), 0) for g in range(3)] + [at(qcol(g), 1) for g in range(3)]
                + [at(kcol, -1), at(kcol, 0), at(vcol, -1), at(vcol, 0)]
                + [at(one, 0), at(one, 0), at(one, -1), at(one, -1), at(one, 1), at(one, 1)]
                + [at(dcol(g), 0) for g in range(3)] + [at(dcol(g), 1) for g in range(3)]
                + [wide(0), wide(1), lspec(0), lspec(1), pl.BlockSpec(memory_space=pltpu.SMEM)])
    kv_out = pl.BlockSpec((b, 128), lambda h, n: (n, h))
    return pl.pallas_call(
        body, name=name, grid=(SWA_KV_HEADS, nb), in_specs=in_specs,
        out_specs=[wide(0), kv_out, kv_out, pl.BlockSpec((1, 8, 128), lambda h, n: (h, 0, 0))],
        out_shape=[jax.ShapeDtypeStruct((t, SWA_W), F32), jax.ShapeDtypeStruct((t, SWA_KV_W), F32),
                   jax.ShapeDtypeStruct((t, SWA_KV_W), F32), jax.ShapeDtypeStruct((SWA_KV_HEADS, 8, 128), F32)],
        compiler_params=_cp(("parallel", "arbitrary")),
    )(*([p] * 10), cosf, sins, cosf, sins, cosf, sins, *([dycat] * 6), o, o, lse, lse, sinks)


def _adam_math(w, g, m, v):
    m = ADAM_B1 * m + (1.0 - ADAM_B1) * g
    v = ADAM_B2 * v + (1.0 - ADAM_B2) * (g * g)
    m_hat = m / (1.0 - ADAM_B1 ** ADAM_STEP)
    v_hat = v / (1.0 - ADAM_B2 ** ADAM_STEP)
    return -ADAM_LR * (m_hat / (jnp.sqrt(v_hat) + ADAM_EPS) + ADAM_WD * w), m, v


def _adamw(w, g, m, v, name):
    shape = w.shape
    cols = shape[-1]
    rows = math.prod(shape[:-1])
    flat = lambda a: a.reshape(rows, cols)
    r = rows
    for cand in (512, 256, 128, 64, 32, 16, 8):
        if rows % cand == 0 and cand * cols * 4 <= (1 << 20):
            r = cand
            break

    def body(w_ref, g_ref, m_ref, v_ref, d_ref, nm_ref, nv_ref):
        d_ref[...], nm_ref[...], nv_ref[...] = _adam_math(w_ref[...], g_ref[...], m_ref[...], v_ref[...])

    spec = pl.BlockSpec((r, cols), lambda i: (i, 0))
    outs = pl.pallas_call(
        body, name=name, grid=(rows // r,), in_specs=[spec] * 4, out_specs=[spec] * 3,
        out_shape=[jax.ShapeDtypeStruct((rows, cols), F32)] * 3, compiler_params=_cp(("parallel",)),
    )(flat(w), flat(g), flat(m), flat(v))
    return tuple(o.reshape(shape) for o in outs)


BIG = ("w_in", "w_out", "ffn_w_up", "ffn_w_down")
CONV = ("dn_conv_w", "ffn_conv_w")
KIND = {"w_in": "col", "w_out": "row", "ffn_w_up": "col", "ffn_w_down": "row"}
SMALL = ("norm_mix_pre", "dn_a_log", "dn_dt_bias", "dn_norm_w", "pool_w", "pool_scale", "swa_sinks",
         "norm_mix_post", "norm_ffn_pre", "ffn_conv_b", "norm_ffn_post")
WEIGHTS = ("norm_mix_pre", "w_in", "dn_conv_w", "dn_a_log", "dn_dt_bias", "dn_norm_w", "pool_w", "pool_scale",
           "swa_sinks", "w_out", "norm_mix_post", "norm_ffn_pre", "ffn_w_up", "ffn_conv_w", "ffn_conv_b",
           "ffn_w_down", "norm_ffn_post")


def _pad_in(w):
    z = lambda n: jnp.zeros(w.shape[:-1] + (n,), w.dtype)
    return jnp.concatenate([w[..., :GATE_END], z(CB_POOL * 128 - GATE_END), w[..., GATE_END:],
                            z(IN_PAD - CB_POOL * 128 - (IN_TRUE - GATE_END))], axis=-1)


def _unpad_in(g):
    return jnp.concatenate([g[..., :GATE_END], g[..., CB_POOL * 128:CB_POOL * 128 + IN_TRUE - GATE_END]], axis=-1)


IN_SHARD = IN_TRUE // 4
IN_SHARD_PAD = -(-IN_SHARD // 128) * 128


def _in_runs():
    cuts = sorted({0, IN_TRUE, GATE_END} | {j * IN_SHARD for j in range(4)})
    runs = []
    for t0, t1 in zip(cuts[:-1], cuts[1:]):
        chip = t0 // IN_SHARD
        runs.append((chip * IN_SHARD_PAD + t0 - chip * IN_SHARD, t0 if t0 < GATE_END else t0 + CB_POOL * 128 - GATE_END, t1 - t0))
    return runs


def _remap_cols(x, runs, out_cols, name):
    rows, cols = x.shape
    r = _rows(rows)
    pieces = {}
    for src, dst, n in runs:
        while n > 0:
            step = min(n, 128 - src % 128, 128 - dst % 128)
            pieces.setdefault(dst // 128, []).append((src // 128, src % 128, dst % 128, step))
            src, dst, n = src + step, dst + step, n - step

    def body(x_ref, o_ref):
        lane = lax.broadcasted_iota(jnp.int32, (1, 128), 1)
        for ob in range(out_cols // 128):
            acc = jnp.zeros((r, 128), F32)
            for ib, ls, ld, n in pieces.get(ob, ()):
                blk = x_ref[:, ib * 128:(ib + 1) * 128].astype(F32)
                moved = blk if ls == ld else pltpu.roll(blk, (ld - ls) % 128, 1)
                acc = jnp.where((lane >= ld) & (lane < ld + n), moved, acc)
            o_ref[:, ob * 128:(ob + 1) * 128] = acc.astype(o_ref.dtype)

    return pl.pallas_call(
        body, name=name, grid=(rows // r,), in_specs=[pl.BlockSpec((r, cols), lambda i: (i, 0))],
        out_specs=pl.BlockSpec((r, out_cols), lambda i: (i, 0)),
        out_shape=jax.ShapeDtypeStruct((rows, out_cols), x.dtype), compiler_params=_cp(("parallel",)),
    )(x)


def _lanes(v):
    return jnp.zeros((1, 128), F32).at[0, :v.shape[0]].set(v)


def _rope_tables(positions):
    inv_freq = 1.0 / (ROPE_THETA ** (jnp.arange(0, HEAD_DIM, 2, dtype=F32) / HEAD_DIM))
    ang = positions.astype(F32)[:, None] * inv_freq
    cos, sin = jnp.cos(ang), jnp.sin(ang)
    return jnp.concatenate([cos, cos], axis=-1), jnp.concatenate([-sin, sin], axis=-1)


class _GradReduce:
    def __init__(self, place, shard_shapes):
        self.place = place
        self.out = {k: lax.empty(shard_shapes[k], F32) for k in BIG}
        self.pending = []

    def parts(self, l, dw):
        def view(k, g):
            if KIND[k] == "row":
                return g.reshape(4, -1, g.shape[1])
            if k == "w_in":
                g = _remap_cols(g, [(seg, chip, n) for chip, seg, n in _in_runs()], 4 * IN_SHARD_PAD, f"l{l}_w_in_grad_layout")
            return g[None]

        names = [k for k in BIG if k in dw]
        return names, [view(k, dw[k]) for k in names]

    def summed(self, l, names, mine, got):
        self.pending += [(l, k, _chip_sum(a, b, self.place, f"l{l}_chip_sum_{k}")) for k, a, b in zip(names, mine, got)]

    def submit(self, l, dw):
        names, mine = self.parts(l, dw)
        self.summed(l, names, mine, _swap_sibling(mine, f"l{l}_" + "_".join(names) + "_to_sibling"))

    def take(self, names):
        entries = [e for e in self.pending if e[1] in names]
        if not entries:
            return None, None
        self.pending = [e for e in self.pending if e[1] not in names]
        return entries, ([s for _, _, s in entries], [KIND[k] for _, k, _ in entries])

    def arrived(self, entries, got):
        for (l, k, own), g in zip(entries, got):
            self.out[k] = _owner_sum(own, g, KIND[k], self.place, (self.out[k], l), f"l{l}_owner_sum_{k}")

    def finish(self):
        entries, (sums, kinds) = self.take(BIG)
        self.arrived(entries, _scatter_chips(sums, kinds, "last_grads_to_owner"))
        return dict(zip(BIG, _join_halves([self.out[k] for k in BIG], "grads_join")))


class _LayerWeights:
    def __init__(self, layers):
        self.layers = layers

    def layer(self, l):
        return self.layers[l]

    def carry(self, l, k):
        return None


class _WeightGather(_LayerWeights):
    def __init__(self, shards, place):
        depth = shards["w_out"].shape[0]
        self.kinds = [KIND[k] for k in BIG]
        self.raw = [{k: _spread_shard(shards[k], l, KIND[k], place, BF16, f"l{l}_cast_{k}") for k in BIG} for l in range(depth)]
        first = _gather_ici([self.raw[0][k] for k in BIG], self.kinds, "l0_gather")
        self.layers = {}
        self.passed(0, _gather_pass(first, self.kinds, "l0_gather_pass"))

    def passed(self, l, arrs):
        full = dict(zip(BIG, arrs))
        full["w_in"] = _remap_cols(full["w_in"], _in_runs(), IN_PAD, f"l{l}_w_in_layout")
        self.layers[l] = full

    def carry(self, l, k):
        return (self.raw[l + 1][k], KIND[k]) if l + 1 < len(self.raw) else None

    def passing(self, landed):
        return [landed[k] for k in BIG], self.kinds


def _local_step(x, positions, target, w, mats, reduce=None):
    depth = w["norm_mix_pre"].shape[0]
    t = x.shape[0]
    cosf, sins = _rope_tables(positions)
    saved = []
    for l in range(depth):
        nm = f"l{l}_"
        n1, n2, n3, n4 = (w[k][l][None] for k in ("norm_mix_pre", "norm_mix_post", "norm_ffn_pre", "norm_ffn_post"))
        alog, dtb, dnw = _lanes(w["dn_a_log"][l]), _lanes(w["dn_dt_bias"][l]), w["dn_norm_w"][l][None]
        psc, cb = w["pool_scale"][l][None], w["ffn_conv_b"][l][None]
        big = mats.layer(l)
        landed = {}

        def project(a, k, name):
            riding = mats.carry(l, k)
            if riding is None:
                return _mm(a, big[k], "nn", F32, name)
            out, landed[k] = _mm(a, big[k], "nn", F32, name, carry=riding)
            return out

        h = _norm_fwd(x, n1, nm + "norm1")
        p = project(h, "w_in", nm + "in_proj")
        qkv = _dn_pre_fwd(p, w["dn_conv_w"][l], nm + "dn_pre")
        y_dn, st = _dn_fwd(qkv, p, alog, dtb, dnw, nm + "dn")
        y_pool = _pool_fwd(p, w["pool_w"][l], psc, nm + "pool")
        y_swa, lse = _swa_fwd(p, cosf, sins, w["swa_sinks"][l], nm + "swa")
        ycat = jnp.concatenate([y_dn, y_pool, y_swa], axis=1).astype(BF16)
        mix = project(ycat, "w_out", nm + "out_proj")
        x1 = _resnorm_fwd(x, mix, n2, nm + "res1")
        h2 = _norm_fwd(x1, n3, nm + "norm3")
        up = project(h2, "ffn_w_up", nm + "ffn_up")
        act = _ffn_act_fwd(up, w["ffn_conv_w"][l], cb, nm + "ffn_act")
        f = project(act, "ffn_w_down", nm + "ffn_down")
        if landed:
            x2, whole = _resnorm_fwd(x1, f, n4, nm + "res2", passing=mats.passing(landed))
            mats.passed(l + 1, whole)
        else:
            x2 = _resnorm_fwd(x1, f, n4, nm + "res2")
        saved.append(dict(x=x, h=h, p=p, qkv=qkv, st=st, y_swa=y_swa, lse=lse, ycat=ycat, mix=mix, x1=x1, h2=h2,
                          up=up, act=act, f=f, n=(n1, n2, n3, n4), alog=alog, dtb=dtb, dnw=dnw, psc=psc, cb=cb))
        x = x2
    loss, dx = _loss_head(x, target, "loss_head")
    grads = {k: [None] * depth for k in WEIGHTS}
    held = None
    for l in reversed(range(depth)):
        nm, s = f"l{l}_b_", saved[l]
        n1, n2, n3, n4 = s["n"]
        big = mats.layer(l)
        riders = (lambda names: reduce.take(names)) if reduce is not None else (lambda names: (None, None))

        def mm_swapping(layer, dw, *args, **kwargs):
            if reduce is None or not dw:
                return _mm(*args, **kwargs)
            names, mine = reduce.parts(layer, dw)
            res = _mm(*args, **kwargs, scatter=(mine, None))
            reduce.summed(layer, names, mine, res[1:])
            return res[0]

        df, g4 = _norm_bwd(s["f"], n4, dx, None, BF16, nm + "res2")
        dact = mm_swapping(l + 1, held, df, big["ffn_w_down"], "nt", F32, nm + "ffn_down_dx")
        grads["ffn_w_down"][l] = _mm(s["act"], df, "tn", BF16, nm + "ffn_down_dw")
        dup, dcw, dcb = _ffn_act_bwd(s["up"], w["ffn_conv_w"][l], s["cb"], dact, nm + "ffn_act")
        grads["ffn_conv_w"][l] = jnp.concatenate([dcw[0], dcw[1]], axis=1)
        grads["ffn_conv_b"][l] = jnp.concatenate([dcb[0], dcb[1]], axis=1)[0]
        grads["ffn_w_up"][l] = _mm(s["h2"], dup, "tn", BF16, nm + "ffn_up_dw", b_pick="split")
        dh2 = mm_swapping(l, {k: grads[k][l] for k in ("ffn_w_down", "ffn_w_up")},
                          dup, big["ffn_w_up"], "nt", BF16, nm + "ffn_up_dx", a_pick="split")
        dx1, g3 = _norm_bwd(s["x1"], n3, dh2, dx, F32, nm + "norm3")
        dmix, g2 = _norm_bwd(s["mix"], n2, dx1, None, BF16, nm + "res1")
        grads["w_out"][l] = _mm(s["ycat"], dmix, "tn", BF16, nm + "out_proj_dw")
        dycat = mm_swapping(l, {"w_out": grads["w_out"][l]}, dmix, big["w_out"], "nt", F32, nm + "out_proj_dx")
        entries, riding = riders(("ffn_w_down", "ffn_w_up"))
        res = _dn_bwd(s["qkv"], s["p"], s["alog"], s["dtb"], s["dnw"], s["st"], dycat, nm + "dn", carry=riding)
        dqkv, dz, dbd, gal, gdt, gnw = res[:6]
        if entries:
            reduce.arrived(entries, res[6:])
        dpq, gconv = _dn_pre_bwd(s["p"], w["dn_conv_w"][l], dqkv, nm + "dn_pre")
        dpool, gpw, gpsc = _pool_bwd(s["p"], w["pool_w"][l], s["psc"], dycat, nm + "pool")
        dsq, dsk, dsv, gsk = _swa_bwd(s["p"], cosf, sins, w["swa_sinks"][l], s["y_swa"], s["lse"], dycat, nm + "swa")
        dp = jnp.concatenate([dpq, dz, dbd, dpool, dsq, dsk, dsv, jnp.zeros((t, 128), F32)], axis=1).astype(BF16)
        entries, riding = riders(("w_in",))
        res = _mm(s["h"], dp, "tn", BF16, nm + "in_proj_dw", scatter=riding)
        grads["w_in"][l] = res[0] if entries else res
        if entries:
            reduce.arrived(entries, res[1:])
        entries, riding = riders(("w_out",))
        res = _mm(dp, big["w_in"], "nt", BF16, nm + "in_proj_dx", scatter=riding)
        dh = res[0] if entries else res
        if entries:
            reduce.arrived(entries, res[1:])
        dx, g1 = _norm_bwd(s["x"], n1, dh, dx1, F32, nm + "norm1")
        held = {"w_in": grads["w_in"][l]}
        grads["norm_mix_pre"][l], grads["norm_mix_post"][l] = g1[0], g2[0]
        grads["norm_ffn_pre"][l], grads["norm_ffn_post"][l] = g3[0], g4[0]
        grads["dn_conv_w"][l] = gconv
        grads["dn_a_log"][l], grads["dn_dt_bias"][l], grads["dn_norm_w"][l] = gal[0, :DN_HEADS], gdt[0, :DN_HEADS], gnw[0]
        grads["pool_w"][l], grads["pool_scale"][l] = gpw, gpsc[0]
        grads["swa_sinks"][l] = gsk[:, 0, :SWA_GROUP].reshape(SWA_HEADS)
    if reduce is not None:
        reduce.submit(0, held)
    return loss, dx, grads


def _flat2(a):
    return a.reshape(math.prod(a.shape[:-1]), a.shape[-1])


def _ew_rows(rows, cols, n_arrays):
    for cand in (512, 256, 128, 64, 32, 16):
        if rows % cand == 0 and cand * cols * 4 * n_arrays <= (8 << 20):
            return cand
    return rows


def _spread_shard(a, layer, kind, place, dtype, name):
    _, rows, cols = a.shape
    r = _ew_rows(rows, cols, 2)
    nb = rows // r

    def body(s_ref, a_ref, o_ref):
        o_ref[...] = a_ref[...].astype(o_ref.dtype)

    if kind == "row":
        out_spec = pl.BlockSpec((r, cols), lambda i, s: (s[0] * nb + i, 0))
        out_shape = (4 * rows, cols)
    else:
        out_spec = pl.BlockSpec((r, cols), lambda i, s: (i, s[0]))
        out_shape = (rows, 4 * cols)
    return pl.pallas_call(
        body, name=name,
        grid_spec=pltpu.PrefetchScalarGridSpec(
            num_scalar_prefetch=1, grid=(nb,),
            in_specs=[pl.BlockSpec((None, r, cols), lambda i, s: (layer, i, 0))], out_specs=out_spec),
        out_shape=jax.ShapeDtypeStruct(out_shape, dtype), compiler_params=_cp(("parallel",)),
    )(place, a)


def _chip_sum(mine, sib, place, name):
    parts, rows, cols = sib.shape
    r = _ew_rows(rows, cols, 3)
    nb = rows // r

    def body(s_ref, a_ref, b_ref, o_ref):
        o_ref[...] = (a_ref[...].astype(F32) + b_ref[...].astype(F32)).astype(o_ref.dtype)

    spec = pl.BlockSpec((None, r, cols), lambda j, i, s: (j, i, 0))
    return pl.pallas_call(
        body, name=name,
        grid_spec=pltpu.PrefetchScalarGridSpec(
            num_scalar_prefetch=1, grid=(parts, nb),
            in_specs=[pl.BlockSpec((None, r, cols), lambda j, i, s: (j, s[1] * nb + i, 0)), spec], out_specs=spec),
        out_shape=jax.ShapeDtypeStruct(sib.shape, BF16), compiler_params=_cp(("parallel", "parallel")),
    )(place, mine, sib)


def _sum_slots(a, name):
    s = a.shape[0]
    a3 = a.reshape(s, math.prod(a.shape[1:-1]), a.shape[-1])
    _, rows, cols = a3.shape
    r = _ew_rows(rows, cols, s + 1)

    def body(a_ref, o_ref):
        acc = a_ref[0].astype(F32)
        for k in range(1, s):
            acc = acc + a_ref[k].astype(F32)
        o_ref[...] = acc

    return pl.pallas_call(body, name=name, grid=(rows // r,),
                          in_specs=[pl.BlockSpec((s, r, cols), lambda i: (0, i, 0))],
                          out_specs=pl.BlockSpec((r, cols), lambda i: (i, 0)),
                          out_shape=jax.ShapeDtypeStruct((rows, cols), F32), compiler_params=_cp(("parallel",)),
                          )(a3).reshape(a.shape[1:])


def _owner_sum(own, got, kind, place, into, name):
    buf, slab = into
    _, rows, cols = got.shape
    r = _ew_rows(rows, cols, 6)
    nb = rows // r

    def body(s_ref, own_ref, got_ref, buf_ref, o_ref):
        acc = own_ref[...].astype(F32)
        for k in range(3):
            acc = acc + got_ref[k].astype(F32)
        o_ref[...] = acc

    if kind == "row":
        own_spec = pl.BlockSpec((None, r, cols), lambda i, s: (s[0], i, 0))
    else:
        own_spec = pl.BlockSpec((None, r, cols), lambda i, s: (0, i, s[0]))
    return pl.pallas_call(
        body, name=name,
        grid_spec=pltpu.PrefetchScalarGridSpec(
            num_scalar_prefetch=1, grid=(nb,),
            in_specs=[own_spec, pl.BlockSpec((3, r, cols), lambda i, s: (0, i, 0)), pl.BlockSpec(memory_space=pl.ANY)],
            out_specs=pl.BlockSpec((None, r, cols), lambda i, s: (slab, s[1] * nb + i, 0))),
        out_shape=jax.ShapeDtypeStruct(buf.shape, buf.dtype), input_output_aliases={3: 0},
        compiler_params=_cp(("parallel",)),
    )(place, own, got, buf)


MESH = pl.DeviceIdType.MESH
ANY = pl.BlockSpec(memory_space=pl.ANY)


def _place():
    x, y, c = lax.axis_index("x"), lax.axis_index("y"), lax.axis_index("c")
    chips = [(1 - x, y), (x, 1 - y), (1 - x, 1 - y)]
    return x, y, c, chips


def _half_part(ref, kind, chip, half):
    if kind == "row":
        h = ref.shape[0] // 8
        return ref.at[pl.ds(pl.multiple_of((2 * chip + half) * h, 16), h), :]
    h, width = ref.shape[0] // 2, ref.shape[1] // 4
    return ref.at[pl.ds(pl.multiple_of(half * h, 16), h), pl.ds(pl.multiple_of(chip * width, 128), width)]


def _gather_copies(w_ref, kind, send, recv):
    x, y, c, chips = _place()
    mine = _half_part(w_ref, kind, 2 * x + y, c)
    return [pltpu.make_async_remote_copy(mine, mine, send.at[j], recv.at[j], device_id=(px, py, c), device_id_type=MESH)
            for j, (px, py) in enumerate(chips)]


def _gather_ici(arrs, kinds, name):
    na = len(arrs)

    def body(*refs):
        outs, send, recv = refs[na:2 * na], refs[2 * na], refs[2 * na + 1]
        cps = [cp for k in range(na) for cp in _gather_copies(outs[k], kinds[k], send.at[k], recv.at[k])]
        for cp in cps:
            cp.start()
        for cp in cps:
            cp.wait()

    return pl.pallas_call(
        body, name=name, in_specs=[ANY] * na, out_specs=[ANY] * na,
        out_shape=[jax.ShapeDtypeStruct(a.shape, a.dtype) for a in arrs],
        input_output_aliases={k: k for k in range(na)},
        scratch_shapes=[pltpu.SemaphoreType.DMA((na, 3))] * 2,
    )(*arrs)


def _pass_copies(outs, kinds, send, recv):
    x, y, c, chips = _place()
    cps, arrivals = [], []
    for k, out in enumerate(outs):
        for j, (px, py) in enumerate(chips):
            mine = _half_part(out, kinds[k], 2 * px + py, c)
            theirs = _half_part(out, kinds[k], 2 * px + py, 1 - c)
            cps.append(pltpu.make_async_remote_copy(mine, mine, send.at[k, j], recv.at[k, j],
                                                    device_id=(x, y, 1 - c), device_id_type=MESH))
            arrivals.append(pltpu.make_async_remote_copy(theirs, theirs, send.at[k, j], recv.at[k, j],
                                                         device_id=(x, y, 1 - c), device_id_type=MESH))
    return cps, arrivals


def _gather_pass(arrs, kinds, name):
    na = len(arrs)

    def body(*refs):
        cps, arrivals = _pass_copies(refs[na:2 * na], kinds, refs[2 * na], refs[2 * na + 1])
        for cp in cps:
            cp.start()
        for cp, arrival in zip(cps, arrivals):
            cp.wait_send()
            arrival.wait_recv()

    return pl.pallas_call(
        body, name=name, in_specs=[ANY] * na, out_specs=[ANY] * na,
        out_shape=[jax.ShapeDtypeStruct(a.shape, a.dtype) for a in arrs],
        input_output_aliases={k: k for k in range(na)},
        scratch_shapes=[pltpu.SemaphoreType.DMA((na, 3))] * 2,
    )(*arrs)


def _swap_sibling(arrs, name):
    na = len(arrs)

    def body(*refs):
        cps = _sibling_copies(refs[:na], refs[na:2 * na], refs[2 * na], refs[2 * na + 1])
        for cp in cps:
            cp.start()
        for cp in cps:
            cp.wait()

    return pl.pallas_call(
        body, name=name, in_specs=[ANY] * na, out_specs=[ANY] * na, out_shape=_sibling_shapes(arrs),
        scratch_shapes=[pltpu.SemaphoreType.DMA((na, 1))] * 2,
    )(*arrs)


def _sibling_shapes(arrs):
    return [jax.ShapeDtypeStruct((a.shape[0], a.shape[1] // 2, a.shape[2]), a.dtype) for a in arrs]


def _sibling_copies(srcs, dsts, send, recv):
    x, y, c, _ = _place()
    cps = []
    for k, (src, dst) in enumerate(zip(srcs, dsts)):
        h = src.shape[1] // 2
        cps.append(pltpu.make_async_remote_copy(src.at[:, pl.ds(pl.multiple_of((1 - c) * h, 16), h), :], dst,
                                                send.at[k, 0], recv.at[k, 0], device_id=(x, y, 1 - c), device_id_type=MESH))
    return cps


def _scatter_shapes(sums, kinds):
    return [jax.ShapeDtypeStruct((3, a.shape[1], a.shape[2] if kind == "row" else a.shape[2] // 4), a.dtype)
            for a, kind in zip(sums, kinds)]


def _scatter_copies(srcs, dsts, kinds, send, recv):
    x, y, c, chips = _place()
    cps = []
    for k, (src, dst) in enumerate(zip(srcs, dsts)):
        for j, (px, py) in enumerate(chips):
            chip = 2 * px + py
            if kinds[k] == "row":
                part = src.at[chip]
            else:
                width = src.shape[2] // 4
                part = src.at[0, :, pl.ds(pl.multiple_of(chip * width, 128), width)]
            cps.append(pltpu.make_async_remote_copy(part, dst.at[j], send.at[k, j], recv.at[k, j],
                                                    device_id=(px, py, c), device_id_type=MESH))
    return cps


def _scatter_chips(sums, kinds, name):
    na = len(sums)

    def body(*refs):
        cps = _scatter_copies(refs[:na], refs[na:2 * na], kinds, refs[2 * na], refs[2 * na + 1])
        for cp in cps:
            cp.start()
        for cp in cps:
            cp.wait()

    return pl.pallas_call(
        body, name=name, in_specs=[ANY] * na, out_specs=[ANY] * na, out_shape=_scatter_shapes(sums, kinds),
        scratch_shapes=[pltpu.SemaphoreType.DMA((na, 3))] * 2,
    )(*sums)


def _join_halves(arrs, name):
    na = len(arrs)

    def body(*refs):
        outs, send, recv = refs[na:2 * na], refs[2 * na], refs[2 * na + 1]
        x, y, c, _ = _place()
        halves = [a.shape[1] // 2 for a in arrs]
        mine = [outs[k].at[:, pl.ds(pl.multiple_of(c * h, 8), h), :] for k, h in enumerate(halves)]
        theirs = [outs[k].at[:, pl.ds(pl.multiple_of((1 - c) * h, 8), h), :] for k, h in enumerate(halves)]
        cps = [pltpu.make_async_remote_copy(mine[k], mine[k], send.at[k], recv.at[k],
                                            device_id=(x, y, 1 - c), device_id_type=MESH) for k in range(na)]
        for cp in cps:
            cp.start()
        for k, cp in enumerate(cps):
            cp.wait_send()
            pltpu.make_async_remote_copy(theirs[k], theirs[k], send.at[k], recv.at[k],
                                         device_id=(x, y, 1 - c), device_id_type=MESH).wait_recv()

    return pl.pallas_call(
        body, name=name, in_specs=[ANY] * na, out_specs=[ANY] * na,
        out_shape=[jax.ShapeDtypeStruct(a.shape, a.dtype) for a in arrs],
        input_output_aliases={k: k for k in range(na)},
        scratch_shapes=[pltpu.SemaphoreType.DMA((na,))] * 2,
    )(*arrs)


def _gather_all(a, name):
    def body(a_ref, o_ref, send, recv, local):
        x, y, c, _ = _place()
        me = 4 * x + 2 * y + c
        mine = pltpu.make_async_copy(a_ref, o_ref.at[me], local)
        mine.start()
        cps = []
        for j in range(1, 8):
            peer = (x ^ (j >> 2), y ^ ((j >> 1) & 1), c ^ (j & 1))
            cps.append(pltpu.make_async_remote_copy(a_ref, o_ref.at[me], send.at[j - 1], recv.at[j - 1],
                                                    device_id=peer, device_id_type=MESH))
        for cp in cps:
            cp.start()
        for cp in cps:
            cp.wait()
        mine.wait()

    return pl.pallas_call(
        body, name=name, in_specs=[ANY], out_specs=ANY,
        out_shape=jax.ShapeDtypeStruct((8,) + a.shape, a.dtype),
        scratch_shapes=[pltpu.SemaphoreType.DMA((7,)), pltpu.SemaphoreType.DMA((7,)), pltpu.SemaphoreType.DMA],
    )(a)


def _pack(parts):
    flat = jnp.concatenate([p.reshape(-1) for p in parts])
    n = flat.shape[0]
    rows = -(-n // (PACK_ROWS * 128)) * PACK_ROWS
    return jnp.pad(flat, (0, rows * 128 - n)).reshape(rows, 128)


def _unpack(buf, like):
    flat, out, off = buf.reshape(-1), [], 0
    for p in like:
        out.append(flat[off:off + p.size].reshape(p.shape))
        off += p.size
    return out


def kernel(x, positions, norm_mix_pre, w_in, dn_conv_w, dn_a_log, dn_dt_bias, dn_norm_w, pool_w, pool_scale, swa_sinks, w_out, norm_mix_post, norm_ffn_pre, ffn_w_up, ffn_conv_w, ffn_conv_b, ffn_w_down, norm_ffn_post, loss_target, m_norm_mix_pre, m_w_in, m_dn_conv_w, m_dn_a_log, m_dn_dt_bias, m_dn_norm_w, m_pool_w, m_pool_scale, m_swa_sinks, m_w_out, m_norm_mix_post, m_norm_ffn_pre, m_ffn_w_up, m_ffn_conv_w, m_ffn_conv_b, m_ffn_w_down, m_norm_ffn_post, v_norm_mix_pre, v_w_in, v_dn_conv_w, v_dn_a_log, v_dn_dt_bias, v_dn_norm_w, v_pool_w, v_pool_scale, v_swa_sinks, v_w_out, v_norm_mix_post, v_norm_ffn_pre, v_ffn_w_up, v_ffn_conv_w, v_ffn_conv_b, v_ffn_w_down, v_norm_ffn_post):
    wts = dict(zip(WEIGHTS, (norm_mix_pre, w_in, dn_conv_w, dn_a_log, dn_dt_bias, dn_norm_w, pool_w, pool_scale, swa_sinks,
                             w_out, norm_mix_post, norm_ffn_pre, ffn_w_up, ffn_conv_w, ffn_conv_b, ffn_w_down, norm_ffn_post)))
    mom = dict(zip(WEIGHTS, (m_norm_mix_pre, m_w_in, m_dn_conv_w, m_dn_a_log, m_dn_dt_bias, m_dn_norm_w, m_pool_w, m_pool_scale,
                             m_swa_sinks, m_w_out, m_norm_mix_post, m_norm_ffn_pre, m_ffn_w_up, m_ffn_conv_w, m_ffn_conv_b,
                             m_ffn_w_down, m_norm_ffn_post)))
    var = dict(zip(WEIGHTS, (v_norm_mix_pre, v_w_in, v_dn_conv_w, v_dn_a_log, v_dn_dt_bias, v_dn_norm_w, v_pool_w, v_pool_scale,
                             v_swa_sinks, v_w_out, v_norm_mix_post, v_norm_ffn_pre, v_ffn_w_up, v_ffn_conv_w, v_ffn_conv_b,
                             v_ffn_w_down, v_norm_ffn_post)))
    c = lax.axis_index("c")
    chip = 2 * lax.axis_index("x") + lax.axis_index("y")
    place = jnp.stack([chip, c]).astype(jnp.int32)
    shards = dict(wts, w_in=jnp.pad(w_in, ((0, 0), (0, 0), (0, IN_SHARD_PAD - IN_SHARD))))
    mats = _WeightGather(shards, place)
    w = dict(wts)
    conv_like = [wts[k] for k in CONV]
    conv_all = _gather_all(_pack(conv_like), "gather_conv")
    for i, k in enumerate(CONV):
        w[k] = jnp.concatenate([_unpack(conv_all[2 * j], conv_like)[i] for j in range(4)], axis=2)

    reduce = _GradReduce(place, {k: shards[k].shape for k in BIG})
    loss, dx, grads = _local_step(x[0], positions[0], loss_target[0], w, mats, reduce)
    loss = lax.psum(loss[0, 0], ("x", "y", "c"))
    g_big = reduce.finish()
    g_big["w_in"] = g_big["w_in"][..., :IN_SHARD]

    small_like = [wts[k] for k in SMALL]
    full_like = small_like + [w[k] for k in CONV]
    g_buf = _sum_slots(_gather_all(_pack([jnp.stack(grads[k]) for k in SMALL + CONV]), "gather_small"), "sum_small")
    g_small = dict(zip(SMALL + CONV, _unpack(g_buf, full_like)))
    for k in CONV:
        width = wts[k].shape[2]
        g_small[k] = lax.dynamic_slice_in_dim(g_small[k], chip * width, width, 2)
    pk = lambda d: _pack([d[k] for k in SMALL + CONV])
    upd = _adamw(pk(wts), pk(g_small), pk(mom), pk(var), "adam_small")
    upd_small = [dict(zip(SMALL + CONV, _unpack(b, small_like + conv_like))) for b in upd]

    g_all, d_all, m_all, v_all = {}, {}, {}, {}
    for k in WEIGHTS:
        if k in BIG:
            g_all[k] = g_big[k]
            d_all[k], m_all[k], v_all[k] = _adamw(wts[k], g_big[k], mom[k], var[k], "adam_" + k)
        else:
            g_all[k], d_all[k], m_all[k], v_all[k] = g_small[k], upd_small[0][k], upd_small[1][k], upd_small[2][k]
    return (loss, dx[None], *[g_all[k] for k in WEIGHTS], *[d_all[k] for k in WEIGHTS],
            *[m_all[k] for k in WEIGHTS], *[v_all[k] for k in WEIGHTS])
```
